```python
import math
import jax, jax.numpy as jnp
from jax import lax
import numpy as np

D_MODEL = 1024
BATCH = 32
SEQ = 2048
DEPTH = 1

HEAD_DIM = 64
SB_HEADS = 8
DIL_GROUPS = ((128, 1), (512, 4), (2048, 16))
DIL_HEADS = 4
MEM_HEADS = 4
MEM_HEAD_DIM = 128
MEM_LEN = 256
N_BRANCHES = 3
D_FF = ((-(-8 * D_MODEL // 3) + 255) // 256) * 256
BLOCK = 128
ROPE_THETA = 10000.0
NORM_EPS = 1e-6
NEG_INF = -1e30

SB_W = SB_HEADS * HEAD_DIM
DIL_W = DIL_HEADS * HEAD_DIM
MEM_W = MEM_HEADS * MEM_HEAD_DIM
IN_SPLITS = (SB_W,) * 3 + (DIL_W,) * (3 * len(DIL_GROUPS)) + (MEM_W,)
D_IN = sum(IN_SPLITS)

kernel_name = 'hybrid_stickbreak_dilated_memory_block'


def rms_norm(x, g):
    xf = x.astype(jnp.float32)
    y = xf * lax.rsqrt(jnp.mean(xf * xf, axis=-1, keepdims=True) + NORM_EPS)
    return (y * g.astype(jnp.float32)).astype(x.dtype)


def rope(x, pos):
    dh = x.shape[-1]
    half = dh // 2
    inv_freq = ROPE_THETA ** (-jnp.arange(half, dtype=jnp.float32) * 2.0 / dh)
    ang = pos.astype(jnp.float32)[:, None] * inv_freq[None, :]
    cos = jnp.cos(ang)[None, :, None, :]
    sin = jnp.sin(ang)[None, :, None, :]
    xf = x.astype(jnp.float32)
    x1, x2 = xf[..., :half], xf[..., half:]
    return jnp.concatenate([x1 * cos - x2 * sin, x2 * cos + x1 * sin], axis=-1).astype(x.dtype)


def stick_breaking_attention(q, k, v):
    B, S, H, dh = q.shape
    scale = dh ** -0.5
    outs = []
    for i in range(S // BLOCK):
        t0 = i * BLOCK
        t1 = t0 + BLOCK
        z = jnp.einsum('bqhd,bkhd->bhqk', q[:, t0:t1], k[:, :t1]).astype(jnp.float32) * scale
        t_pos = t0 + jnp.arange(BLOCK)[:, None]
        s_pos = jnp.arange(t1)[None, :]
        causal = s_pos < t_pos
        log_beta = jax.nn.log_sigmoid(z)
        log_keep = jnp.where(causal, jax.nn.log_sigmoid(-z), 0.0)
        log_keep_after = lax.cumsum(log_keep, axis=3, reverse=True) - log_keep
        weight = jnp.where(causal, jnp.exp(log_beta + log_keep_after), 0.0)
        outs.append(jnp.einsum('bhqk,bkhd->bqhd', weight, v[:, :t1].astype(jnp.float32)))
    return jnp.concatenate(outs, axis=1).astype(q.dtype)


def banded_attention(q, k, v, span):
    N, L, H, dh = q.shape
    nb = -(-L // BLOCK)
    pad = nb * BLOCK - L
    padf = lambda t: jnp.pad(t, ((0, 0), (0, pad), (0, 0), (0, 0))).reshape(N, nb, BLOCK, H, dh)
    qb, kb, vb = padf(q), padf(k), padf(v)
    prev = lambda t: jnp.concatenate([jnp.zeros_like(t[:, :1]), t[:, :-1]], axis=1)
    kk = jnp.concatenate([prev(kb), kb], axis=2)
    vv = jnp.concatenate([prev(vb), vb], axis=2)
    s = jnp.einsum('nbqhd,nbkhd->nbhqk', qb, kk).astype(jnp.float32) * (dh ** -0.5)
    qi = jnp.arange(BLOCK)[:, None] + BLOCK
    kj = jnp.arange(2 * BLOCK)[None, :]
    dist = qi - kj
    band = (dist >= 0) & (dist <= span)
    has_prev = (jnp.arange(nb)[:, None, None] > 0) | (kj[None] >= BLOCK)
    valid = band[None] & has_prev
    s = jnp.where(valid[None, :, None], s, NEG_INF)
    m = jnp.max(s, axis=-1, keepdims=True)
    p = jnp.exp(s - m)
    den = jnp.sum(p, axis=-1, keepdims=True)
    o = jnp.einsum('nbhqk,nbkhd->nbhqd', p, vv.astype(jnp.float32)) / den
    lse = (m + jnp.log(den))[..., 0]
    o = o.transpose(0, 1, 3, 2, 4).reshape(N, nb * BLOCK, H, dh)[:, :L]
    lse = lse.transpose(0, 1, 3, 2).reshape(N, nb * BLOCK, H)[:, :L]
    return o, lse


def dilated_window_attention(q, k, v, window, dilation):
    B, S, H, dh = q.shape
    L = S // dilation
    def to_sub(t):
        return t.reshape(B, L, dilation, H, dh).transpose(0, 2, 1, 3, 4).reshape(B * dilation, L, H, dh)
    o, lse = banded_attention(to_sub(q), to_sub(k), to_sub(v), window // dilation)
    o = o.reshape(B, dilation, L, H, dh).transpose(0, 2, 1, 3, 4).reshape(B, S, H, dh)
    lse = lse.reshape(B, dilation, L, H).transpose(0, 2, 1, 3).reshape(B, S, H)
    return o, lse


def memory_cross_attention(q, k, v):
    s = jnp.einsum('bshd,bmhd->bhsm', q, k).astype(jnp.float32) * (q.shape[-1] ** -0.5)
    p = jax.nn.softmax(s, axis=-1)
    return jnp.einsum('bhsm,bmhd->bshd', p, v.astype(jnp.float32)).astype(q.dtype)


def _fwd_setup_inputs(seed: int = 0) -> dict:
    key = jax.random.key(seed)
    ks = jax.random.split(key, 20)
    def w(k, shape, fan_in):
        return jax.random.normal(k, shape, jnp.float32) * (fan_in ** -0.5)
    def gain(k):
        return 1.0 + 0.05 * jax.random.normal(k, (DEPTH, D_MODEL), jnp.float32)
    return {
        'x': jax.random.normal(ks[0], (BATCH, SEQ, D_MODEL), jnp.float32),
        'mem': jax.random.normal(ks[1], (BATCH, MEM_LEN, D_MODEL), jnp.float32),
        'g_pre_mix': gain(ks[2]),
        'g_post_mix': gain(ks[3]),
        'g_pre_ffn': gain(ks[4]),
        'g_post_ffn': gain(ks[5]),
        'g_mem': gain(ks[6]),
        'w_in': w(ks[7], (DEPTH, D_MODEL, D_IN), D_MODEL),
        'w_mem_kv': w(ks[8], (DEPTH, D_MODEL, 2 * MEM_W), D_MODEL),
        'w_br_sb': w(ks[9], (DEPTH, SB_W, D_MODEL), SB_W),
        'w_br_dil': w(ks[10], (DEPTH, DIL_W, D_MODEL), DIL_W),
        'w_br_mem': w(ks[11], (DEPTH, MEM_W, D_MODEL), MEM_W),
        'w_gate': w(ks[12], (DEPTH, D_MODEL, N_BRANCHES * D_MODEL), D_MODEL),
        'b_gate': 0.02 * jax.random.normal(ks[13], (DEPTH, N_BRANCHES * D_MODEL), jnp.float32),
        'w_o': w(ks[14], (DEPTH, D_MODEL, D_MODEL), D_MODEL),
        'w_ffn_in': w(ks[15], (DEPTH, D_MODEL, 2 * D_FF), D_MODEL),
        'w_ffn_out': w(ks[16], (DEPTH, D_FF, D_MODEL), D_FF),
    }


def _fwd_reference(x, mem, g_pre_mix, g_post_mix, g_pre_ffn, g_post_ffn, g_mem, w_in, w_mem_kv,
              w_br_sb, w_br_dil, w_br_mem, w_gate, b_gate, w_o, w_ffn_in, w_ffn_out):
    B, S, D = x.shape
    pos = jnp.arange(S)
    split_idx = [int(i) for i in np.cumsum(IN_SPLITS)[:-1]]
    n_g = len(DIL_GROUPS)
    for l in range(DEPTH):
        h = rms_norm(x, g_pre_mix[l])
        proj = jnp.einsum('bsd,de->bse', h, w_in[l])
        parts = jnp.split(proj, split_idx, axis=-1)
        heads = lambda t, n, dh: t.reshape(B, S, n, dh)

        q_a, k_a, v_a = (heads(t, SB_HEADS, HEAD_DIM) for t in parts[0:3])
        o_a = stick_breaking_attention(q_a, k_a, v_a).reshape(B, S, SB_W)

        outs, lses = [], []
        for g, (window, dilation) in enumerate(DIL_GROUPS):
            q_g, k_g, v_g = (heads(t, DIL_HEADS, HEAD_DIM) for t in parts[3 + 3 * g: 6 + 3 * g])
            o_g, lse_g = dilated_window_attention(rope(q_g, pos), rope(k_g, pos), v_g, window, dilation)
            outs.append(o_g)
            lses.append(lse_g)
        alpha = jax.nn.softmax(jnp.stack(lses, axis=0), axis=0)[..., None]
        o_b = jnp.sum(alpha * jnp.stack(outs, axis=0), axis=0).astype(x.dtype).reshape(B, S, DIL_W)

        q_c = heads(parts[3 + 3 * n_g], MEM_HEADS, MEM_HEAD_DIM)
        kv_m = jnp.einsum('bmd,de->bme', rms_norm(mem, g_mem[l]), w_mem_kv[l])
        k_m = kv_m[..., :MEM_W].reshape(B, MEM_LEN, MEM_HEADS, MEM_HEAD_DIM)
        v_m = kv_m[..., MEM_W:].reshape(B, MEM_LEN, MEM_HEADS, MEM_HEAD_DIM)
        o_c = memory_cross_attention(q_c, k_m, v_m).reshape(B, S, MEM_W)

        y_a = jnp.einsum('bse,ed->bsd', o_a, w_br_sb[l])
        y_b = jnp.einsum('bse,ed->bsd', o_b, w_br_dil[l])
        y_c = jnp.einsum('bse,ed->bsd', o_c, w_br_mem[l])
        gates = jax.nn.sigmoid(jnp.einsum('bsd,de->bse', h, w_gate[l]) + b_gate[l]).reshape(B, S, N_BRANCHES, D)
        merged = gates[:, :, 0] * y_a + gates[:, :, 1] * y_b + gates[:, :, 2] * y_c
        mix = jnp.einsum('bsd,de->bse', merged, w_o[l])
        x = x + rms_norm(mix, g_post_mix[l])

        h2 = rms_norm(x, g_pre_ffn[l])
        gu = jnp.einsum('bsd,df->bsf', h2, w_ffn_in[l])
        f = jax.nn.silu(gu[..., :D_FF]) * gu[..., D_FF:]
        f = jnp.einsum('bsf,fd->bsd', f, w_ffn_out[l])
        x = x + rms_norm(f, g_post_ffn[l])
    return x


import jax as _jax
import jax.numpy as _jnp

TWIN_FORMAT = 'train_step'
FWD_PARAMS = ['x', 'mem', 'g_pre_mix', 'g_post_mix', 'g_pre_ffn', 'g_post_ffn', 'g_mem', 'w_in', 'w_mem_kv', 'w_br_sb', 'w_br_dil', 'w_br_mem', 'w_gate', 'b_gate', 'w_o', 'w_ffn_in', 'w_ffn_out']
TWIN_WEIGHTS = ['g_pre_mix', 'g_post_mix', 'g_pre_ffn', 'g_post_ffn', 'g_mem', 'w_in', 'w_mem_kv', 'w_br_sb', 'w_br_dil', 'w_br_mem', 'w_gate', 'b_gate', 'w_o', 'w_ffn_in', 'w_ffn_out']
TWIN_DIFF_INPUT = 'x'
TWIN_INPUTS = ['x', 'mem', 'g_pre_mix', 'g_post_mix', 'g_pre_ffn', 'g_post_ffn', 'g_mem', 'w_in', 'w_mem_kv', 'w_br_sb', 'w_br_dil', 'w_br_mem', 'w_gate', 'b_gate', 'w_o', 'w_ffn_in', 'w_ffn_out', 'loss_target', 'm_g_pre_mix', 'm_g_post_mix', 'm_g_pre_ffn', 'm_g_post_ffn', 'm_g_mem', 'm_w_in', 'm_w_mem_kv', 'm_w_br_sb', 'm_w_br_dil', 'm_w_br_mem', 'm_w_gate', 'm_b_gate', 'm_w_o', 'm_w_ffn_in', 'm_w_ffn_out', 'v_g_pre_mix', 'v_g_post_mix', 'v_g_pre_ffn', 'v_g_post_ffn', 'v_g_mem', 'v_w_in', 'v_w_mem_kv', 'v_w_br_sb', 'v_w_br_dil', 'v_w_br_mem', 'v_w_gate', 'v_b_gate', 'v_w_o', 'v_w_ffn_in', 'v_w_ffn_out']
TWIN_OUTPUTS = ['loss', 'grad_x', 'grad_g_pre_mix', 'grad_g_post_mix', 'grad_g_pre_ffn', 'grad_g_post_ffn', 'grad_g_mem', 'grad_w_in', 'grad_w_mem_kv', 'grad_w_br_sb', 'grad_w_br_dil', 'grad_w_br_mem', 'grad_w_gate', 'grad_b_gate', 'grad_w_o', 'grad_w_ffn_in', 'grad_w_ffn_out', 'delta_g_pre_mix', 'delta_g_post_mix', 'delta_g_pre_ffn', 'delta_g_post_ffn', 'delta_g_mem', 'delta_w_in', 'delta_w_mem_kv', 'delta_w_br_sb', 'delta_w_br_dil', 'delta_w_br_mem', 'delta_w_gate', 'delta_b_gate', 'delta_w_o', 'delta_w_ffn_in', 'delta_w_ffn_out', 'new_m_g_pre_mix', 'new_m_g_post_mix', 'new_m_g_pre_ffn', 'new_m_g_post_ffn', 'new_m_g_mem', 'new_m_w_in', 'new_m_w_mem_kv', 'new_m_w_br_sb', 'new_m_w_br_dil', 'new_m_w_br_mem', 'new_m_w_gate', 'new_m_b_gate', 'new_m_w_o', 'new_m_w_ffn_in', 'new_m_w_ffn_out', 'new_v_g_pre_mix', 'new_v_g_post_mix', 'new_v_g_pre_ffn', 'new_v_g_post_ffn', 'new_v_g_mem', 'new_v_w_in', 'new_v_w_mem_kv', 'new_v_w_br_sb', 'new_v_w_br_dil', 'new_v_w_br_mem', 'new_v_w_gate', 'new_v_b_gate', 'new_v_w_o', 'new_v_w_ffn_in', 'new_v_w_ffn_out']
TWIN_LEAF_KINDS = {'loss': 'loss', 'grad_x': 'grad_x', 'grad_g_pre_mix': 'grad_w', 'grad_g_post_mix': 'grad_w', 'grad_g_pre_ffn': 'grad_w', 'grad_g_post_ffn': 'grad_w', 'grad_g_mem': 'grad_w', 'grad_w_in': 'grad_w', 'grad_w_mem_kv': 'grad_w', 'grad_w_br_sb': 'grad_w', 'grad_w_br_dil': 'grad_w', 'grad_w_br_mem': 'grad_w', 'grad_w_gate': 'grad_w', 'grad_b_gate': 'grad_w', 'grad_w_o': 'grad_w', 'grad_w_ffn_in': 'grad_w', 'grad_w_ffn_out': 'grad_w', 'delta_g_pre_mix': 'delta_w', 'delta_g_post_mix': 'delta_w', 'delta_g_pre_ffn': 'delta_w', 'delta_g_post_ffn': 'delta_w', 'delta_g_mem': 'delta_w', 'delta_w_in': 'delta_w', 'delta_w_mem_kv': 'delta_w', 'delta_w_br_sb': 'delta_w', 'delta_w_br_dil': 'delta_w', 'delta_w_br_mem': 'delta_w', 'delta_w_gate': 'delta_w', 'delta_b_gate': 'delta_w', 'delta_w_o': 'delta_w', 'delta_w_ffn_in': 'delta_w', 'delta_w_ffn_out': 'delta_w', 'new_m_g_pre_mix': 'new_m', 'new_m_g_post_mix': 'new_m', 'new_m_g_pre_ffn': 'new_m', 'new_m_g_post_ffn': 'new_m', 'new_m_g_mem': 'new_m', 'new_m_w_in': 'new_m', 'new_m_w_mem_kv': 'new_m', 'new_m_w_br_sb': 'new_m', 'new_m_w_br_dil': 'new_m', 'new_m_w_br_mem': 'new_m', 'new_m_w_gate': 'new_m', 'new_m_b_gate': 'new_m', 'new_m_w_o': 'new_m', 'new_m_w_ffn_in': 'new_m', 'new_m_w_ffn_out': 'new_m', 'new_v_g_pre_mix': 'new_v', 'new_v_g_post_mix': 'new_v', 'new_v_g_pre_ffn': 'new_v', 'new_v_g_post_ffn': 'new_v', 'new_v_g_mem': 'new_v', 'new_v_w_in': 'new_v', 'new_v_w_mem_kv': 'new_v', 'new_v_w_br_sb': 'new_v', 'new_v_w_br_dil': 'new_v', 'new_v_w_br_mem': 'new_v', 'new_v_w_gate': 'new_v', 'new_v_b_gate': 'new_v', 'new_v_w_o': 'new_v', 'new_v_w_ffn_in': 'new_v', 'new_v_w_ffn_out': 'new_v'}


def _forward(args):
    return _fwd_reference(*[args[k] for k in FWD_PARAMS])


def _output_shape():
    out = _jax.eval_shape(lambda: _forward(_fwd_setup_inputs(0)))
    return out.shape, out.dtype

N_MICROBATCH = 1
ADAM_LR = 0.001
ADAM_B1 = 0.9
ADAM_B2 = 0.999
ADAM_EPS = 1e-08
ADAM_WD = 0.01
ADAM_STEP = 10
PER_EXAMPLE_BATCH_AXIS = {'x': 0, 'mem': 0, 'loss_target': 0}
SHARED_INPUTS = []
_WEIGHT_DTYPES = {'g_pre_mix': _jnp.float32, 'g_post_mix': _jnp.float32, 'g_pre_ffn': _jnp.float32, 'g_post_ffn': _jnp.float32, 'g_mem': _jnp.float32, 'w_in': _jnp.float32, 'w_mem_kv': _jnp.float32, 'w_br_sb': _jnp.float32, 'w_br_dil': _jnp.float32, 'w_br_mem': _jnp.float32, 'w_gate': _jnp.float32, 'b_gate': _jnp.float32, 'w_o': _jnp.float32, 'w_ffn_in': _jnp.float32, 'w_ffn_out': _jnp.float32}
MOMENT_SCALE = {'g_pre_mix': 9.492558e-01, 'g_post_mix': 6.406888e+01, 'g_pre_ffn': 8.382963e-01, 'g_post_ffn': 6.364091e+01, 'g_mem': 1.776973e-01, 'w_in': 3.748706e-01, 'w_mem_kv': 1.822958e-01, 'w_br_sb': 7.554665e-01, 'w_br_dil': 1.712079e-01, 'w_br_mem': 1.445323e-01, 'w_gate': 1.406577e-01, 'b_gate': 1.981653e-01, 'w_o': 8.176493e-01, 'w_ffn_in': 3.620086e-01, 'w_ffn_out': 7.417923e-01}


def _to_microbatches(a, axis):
    t = _jnp.moveaxis(a, axis, 0)
    t = t.reshape((N_MICROBATCH, t.shape[0] // N_MICROBATCH) + t.shape[1:])
    return _jnp.moveaxis(t, 1, axis + 1)


def setup_inputs(seed: int = 0) -> dict:
    inp = _fwd_setup_inputs(seed)
    key = _jax.random.fold_in(_jax.random.key(seed), 7919)
    shape, _ = _output_shape()
    out = dict(inp)
    out["loss_target"] = _jax.random.normal(_jax.random.fold_in(key, 0), shape, _jnp.float32)
    for i, name in enumerate(TWIN_WEIGHTS):
        w = inp[name].astype(_jnp.float32)
        if MOMENT_SCALE is None:
            s = _jnp.sqrt(_jnp.mean(_jnp.square(w)) + 1e-30)
        else:
            s = MOMENT_SCALE[name]
        km, kv = _jax.random.split(_jax.random.fold_in(key, i + 1))
        out[name] = w
        out["m_" + name] = s * _jax.random.normal(km, w.shape, _jnp.float32)
        out["v_" + name] = (s * s) * _jax.random.uniform(kv, w.shape, _jnp.float32, 0.5, 1.5)
    if N_MICROBATCH > 1:
        for name, axis in PER_EXAMPLE_BATCH_AXIS.items():
            out[name] = _to_microbatches(out[name], axis)
    return {'x': out['x'], 'mem': out['mem'], 'g_pre_mix': out['g_pre_mix'], 'g_post_mix': out['g_post_mix'], 'g_pre_ffn': out['g_pre_ffn'], 'g_post_ffn': out['g_post_ffn'], 'g_mem': out['g_mem'], 'w_in': out['w_in'], 'w_mem_kv': out['w_mem_kv'], 'w_br_sb': out['w_br_sb'], 'w_br_dil': out['w_br_dil'], 'w_br_mem': out['w_br_mem'], 'w_gate': out['w_gate'], 'b_gate': out['b_gate'], 'w_o': out['w_o'], 'w_ffn_in': out['w_ffn_in'], 'w_ffn_out': out['w_ffn_out'], 'loss_target': out['loss_target'], 'm_g_pre_mix': out['m_g_pre_mix'], 'm_g_post_mix': out['m_g_post_mix'], 'm_g_pre_ffn': out['m_g_pre_ffn'], 'm_g_post_ffn': out['m_g_post_ffn'], 'm_g_mem': out['m_g_mem'], 'm_w_in': out['m_w_in'], 'm_w_mem_kv': out['m_w_mem_kv'], 'm_w_br_sb': out['m_w_br_sb'], 'm_w_br_dil': out['m_w_br_dil'], 'm_w_br_mem': out['m_w_br_mem'], 'm_w_gate': out['m_w_gate'], 'm_b_gate': out['m_b_gate'], 'm_w_o': out['m_w_o'], 'm_w_ffn_in': out['m_w_ffn_in'], 'm_w_ffn_out': out['m_w_ffn_out'], 'v_g_pre_mix': out['v_g_pre_mix'], 'v_g_post_mix': out['v_g_post_mix'], 'v_g_pre_ffn': out['v_g_pre_ffn'], 'v_g_post_ffn': out['v_g_post_ffn'], 'v_g_mem': out['v_g_mem'], 'v_w_in': out['v_w_in'], 'v_w_mem_kv': out['v_w_mem_kv'], 'v_w_br_sb': out['v_w_br_sb'], 'v_w_br_dil': out['v_w_br_dil'], 'v_w_br_mem': out['v_w_br_mem'], 'v_w_gate': out['v_w_gate'], 'v_b_gate': out['v_b_gate'], 'v_w_o': out['v_w_o'], 'v_w_ffn_in': out['v_w_ffn_in'], 'v_w_ffn_out': out['v_w_ffn_out']}


def _loss(weights, diff, rest, loss_target):
    with _jax.named_scope("forward"):
        args = {**rest, TWIN_DIFF_INPUT: diff, **{k: w.astype(_WEIGHT_DTYPES[k]) for k, w in weights.items()}}
        y = _forward(args)
    with _jax.named_scope("loss_head"):
        err = _jnp.square(y.astype(_jnp.float32) - loss_target)
        return 0.5 * _jnp.sum(_jnp.mean(err, axis=-1)) if err.ndim else 0.5 * err


def _adamw(w, g, m, v):
    m = ADAM_B1 * m + (1.0 - ADAM_B1) * g
    v = ADAM_B2 * v + (1.0 - ADAM_B2) * _jnp.square(g)
    m_hat = m / (1.0 - ADAM_B1 ** ADAM_STEP)
    v_hat = v / (1.0 - ADAM_B2 ** ADAM_STEP)
    delta = -ADAM_LR * (m_hat / (_jnp.sqrt(v_hat) + ADAM_EPS) + ADAM_WD * w)
    return delta, m, v


def reference(x, mem, g_pre_mix, g_post_mix, g_pre_ffn, g_post_ffn, g_mem, w_in, w_mem_kv, w_br_sb, w_br_dil, w_br_mem, w_gate, b_gate, w_o, w_ffn_in, w_ffn_out, loss_target, m_g_pre_mix, m_g_post_mix, m_g_pre_ffn, m_g_post_ffn, m_g_mem, m_w_in, m_w_mem_kv, m_w_br_sb, m_w_br_dil, m_w_br_mem, m_w_gate, m_b_gate, m_w_o, m_w_ffn_in, m_w_ffn_out, v_g_pre_mix, v_g_post_mix, v_g_pre_ffn, v_g_post_ffn, v_g_mem, v_w_in, v_w_mem_kv, v_w_br_sb, v_w_br_dil, v_w_br_mem, v_w_gate, v_b_gate, v_w_o, v_w_ffn_in, v_w_ffn_out):
    given = dict(x=x, mem=mem, g_pre_mix=g_pre_mix, g_post_mix=g_post_mix, g_pre_ffn=g_pre_ffn, g_post_ffn=g_post_ffn, g_mem=g_mem, w_in=w_in, w_mem_kv=w_mem_kv, w_br_sb=w_br_sb, w_br_dil=w_br_dil, w_br_mem=w_br_mem, w_gate=w_gate, b_gate=b_gate, w_o=w_o, w_ffn_in=w_ffn_in, w_ffn_out=w_ffn_out, loss_target=loss_target, m_g_pre_mix=m_g_pre_mix, m_g_post_mix=m_g_post_mix, m_g_pre_ffn=m_g_pre_ffn, m_g_post_ffn=m_g_post_ffn, m_g_mem=m_g_mem, m_w_in=m_w_in, m_w_mem_kv=m_w_mem_kv, m_w_br_sb=m_w_br_sb, m_w_br_dil=m_w_br_dil, m_w_br_mem=m_w_br_mem, m_w_gate=m_w_gate, m_b_gate=m_b_gate, m_w_o=m_w_o, m_w_ffn_in=m_w_ffn_in, m_w_ffn_out=m_w_ffn_out, v_g_pre_mix=v_g_pre_mix, v_g_post_mix=v_g_post_mix, v_g_pre_ffn=v_g_pre_ffn, v_g_post_ffn=v_g_post_ffn, v_g_mem=v_g_mem, v_w_in=v_w_in, v_w_mem_kv=v_w_mem_kv, v_w_br_sb=v_w_br_sb, v_w_br_dil=v_w_br_dil, v_w_br_mem=v_w_br_mem, v_w_gate=v_w_gate, v_b_gate=v_b_gate, v_w_o=v_w_o, v_w_ffn_in=v_w_ffn_in, v_w_ffn_out=v_w_ffn_out)
    weights = {n: given[n] for n in TWIN_WEIGHTS}
    shared = {n: given[n] for n in SHARED_INPUTS}
    per_example = {n: given[n] for n in ['x', 'mem']}
    grad_fn = _jax.value_and_grad(_loss, argnums=(0, 1))

    def one_microbatch(ex, loss_target):
        ex = dict(ex)
        diff = ex.pop(TWIN_DIFF_INPUT)
        return grad_fn(weights, diff, {**shared, **ex}, loss_target)

    if N_MICROBATCH == 1:
        loss, (grad_w, grad_x) = one_microbatch(per_example, given["loss_target"])
    else:
        def body(carry, xs):
            loss_sum, grad_sum = carry
            l_k, (gw_k, gx_k) = one_microbatch(xs[0], xs[1])
            with _jax.named_scope("update"):
                return (loss_sum + l_k, _jax.tree.map(_jnp.add, grad_sum, gw_k)), gx_k

        init = (_jnp.zeros((), _jnp.float32), _jax.tree.map(_jnp.zeros_like, weights))
        (loss, grad_w), grad_x = _jax.lax.scan(body, init, (per_example, given["loss_target"]))
    with _jax.named_scope("update"):
        delta_w, new_m, new_v = {}, {}, {}
        for n in TWIN_WEIGHTS:
            delta_w[n], new_m[n], new_v[n] = _adamw(weights[n], grad_w[n], given["m_" + n], given["v_" + n])
    return (loss, grad_x, *[grad_w[n] for n in TWIN_WEIGHTS], *[delta_w[n] for n in TWIN_WEIGHTS],
            *[new_m[n] for n in TWIN_WEIGHTS], *[new_v[n] for n in TWIN_WEIGHTS])
```

```python
import functools

import jax
import jax.numpy as jnp
from jax import lax
from jax.experimental import pallas as pl
from jax.experimental.pallas import tpu as pltpu

F32 = jnp.float32
BF16 = jnp.bfloat16
D = 1024
BLK = 128
MEM_LEN = 256
D_FF = 2816
NORM_EPS = 1e-6
NEG_INF = -1e30
ROPE_THETA = 10000.0
ADAM_LR, ADAM_B1, ADAM_B2, ADAM_EPS, ADAM_WD, ADAM_STEP = 0.001, 0.9, 0.999, 1e-08, 0.01, 10
N_DEV = 8
VMEM_LIMIT_BYTES = 56 * 1024 * 1024
MESH = pl.DeviceIdType.MESH
ANY = pl.BlockSpec(memory_space=pl.ANY)

NT = (((1,), (1,)), ((), ()))
TN = (((0,), (0,)), ((), ()))
NN = (((1,), (0,)), ((), ()))
_DIMS = {"nn": NN, "nt": NT, "tn": TN}

PACK_NAMES = ("w_in", "w_mem_kv", "w_br_sb", "w_br_dil", "w_br_mem", "w_gate", "w_o", "w_ffn_in", "w_ffn_out")
PACK_ROWS = (544, 128, 64, 32, 64, 384, 128, 704, 352)
PACK_OFFS = tuple(sum(PACK_ROWS[:i]) for i in range(len(PACK_ROWS)))
PACK_TOTAL = sum(PACK_ROWS)

CHUNKS_A = tuple(c for hp in range(4) for c in (hp, 4 + hp, 8 + hp)) + (30, 31, 32, 33)
CHUNKS_B = tuple(c for g in range(3) for hp in range(2) for c in (12 + 6 * g + hp, 14 + 6 * g + hp, 16 + 6 * g + hp))
WA, WB = 128 * len(CHUNKS_A), 128 * len(CHUNKS_B)
DIL_GROUPS = (1, 4, 16)


def _params(*sem):
    return pltpu.CompilerParams(dimension_semantics=sem or None, vmem_limit_bytes=VMEM_LIMIT_BYTES)


def _tile(n, cap):
    if n <= 128:
        return n
    assert n % 128 == 0, n
    best = 128
    for t in range(128, min(n, cap) + 1, 128):
        if n % t == 0:
            best = t
    return best


def _matmul(a, b, mode, out_dtype, name, tm_cap=512, tn_cap=1536, tk_cap=1536):
    if mode == "tn":
        (K, M), N = a.shape, b.shape[1]
    elif mode == "nt":
        (M, K), N = a.shape, b.shape[0]
    else:
        (M, K), N = a.shape, b.shape[1]
    tm, tn, tk = _tile(M, tm_cap), _tile(N, tn_cap), _tile(K, tk_cap)
    nm, nn, nk = M // tm, N // tn, K // tk
    dims = _DIMS[mode]

    def body(a_ref, b_ref, o_ref, *acc):
        p = lax.dot_general(a_ref[...], b_ref[...], dims, preferred_element_type=F32)
        if nk == 1:
            o_ref[...] = p.astype(o_ref.dtype)
        else:
            acc_ref, = acc
            k = pl.program_id(2)

            @pl.when(k == 0)
            def _():
                acc_ref[...] = p

            @pl.when(k > 0)
            def _():
                acc_ref[...] += p

            @pl.when(k == nk - 1)
            def _():
                o_ref[...] = acc_ref[...].astype(o_ref.dtype)

    n_outer = nk == 1 and (a.size * nn + b.size) < (a.size + b.size * nm)
    if n_outer:
        grid, ij = (nn, nm, nk), (lambda g0, g1: (g1, g0))
    else:
        grid, ij = (nm, nn, nk), (lambda g0, g1: (g0, g1))
    if mode == "tn":
        a_spec = pl.BlockSpec((tk, tm), lambda g0, g1, k: (k, ij(g0, g1)[0]))
    else:
        a_spec = pl.BlockSpec((tm, tk), lambda g0, g1, k: (ij(g0, g1)[0], k))
    if mode == "nt":
        b_spec = pl.BlockSpec((tn, tk), lambda g0, g1, k: (ij(g0, g1)[1], k))
    else:
        b_spec = pl.BlockSpec((tk, tn), lambda g0, g1, k: (k, ij(g0, g1)[1]))
    return pl.pallas_call(
        body, name=name, grid=grid,
        out_shape=jax.ShapeDtypeStruct((M, N), out_dtype),
        in_specs=[a_spec, b_spec],
        out_specs=pl.BlockSpec((tm, tn), lambda g0, g1, k: ij(g0, g1)),
        scratch_shapes=[pltpu.VMEM((tm, tn), F32)] if nk > 1 else [],
        compiler_params=_params("parallel", "parallel", "arbitrary"),
    )(a, b)


def _rowwise(body, name, rows, tr, row_ins, vec_ins, row_outs, acc_outs=()):
    tr = min(tr, rows)
    assert rows % tr == 0
    in_specs, args = [], []
    for r in row_ins:
        arr, w, cb = r if isinstance(r, tuple) else (r, r.shape[1], 0)
        in_specs.append(pl.BlockSpec((tr, w), functools.partial(lambda i, cb: (i, cb), cb=cb)))
        args.append(arr)
    for v in vec_ins:
        in_specs.append(pl.BlockSpec(v.shape, lambda i: (0, 0)))
        args.append(v)
    out_shape = [jax.ShapeDtypeStruct((rows, w), dt) for w, dt in row_outs]
    out_shape += [jax.ShapeDtypeStruct((1, w), F32) for w in acc_outs]
    out_specs = [pl.BlockSpec((tr, w), lambda i: (i, 0)) for w, _ in row_outs]
    out_specs += [pl.BlockSpec((1, w), lambda i: (0, 0)) for w in acc_outs]
    n_acc = len(acc_outs)

    def wrapped(*refs):
        if n_acc:
            @pl.when(pl.program_id(0) == 0)
            def _():
                for r in refs[len(refs) - n_acc:]:
                    r[...] = jnp.zeros_like(r)
        body(*refs)

    return pl.pallas_call(
        wrapped, name=name, grid=(rows // tr,), out_shape=out_shape, in_specs=in_specs, out_specs=out_specs,
        compiler_params=_params("arbitrary"),
    )(*args)


def _rstd(x):
    return lax.rsqrt(jnp.mean(x * x, axis=-1, keepdims=True) + NORM_EPS)


def _norm_bwd(u, n, r):
    return r * (u - n * jnp.mean(u * n, axis=-1, keepdims=True))


def _colsum(v):
    return jnp.sum(v, axis=0, keepdims=True)


def _norm_fwd(x, g, name):
    def body(x_ref, g_ref, h_ref):
        xv = x_ref[...]
        h_ref[...] = ((xv * _rstd(xv)) * g_ref[...]).astype(BF16)

    return _rowwise(body, name, x.shape[0], 512, [x], [g], [(D, BF16)])[0]


def _resid_norm_fwd(x, mix, g_post, g_pre):
    def body(x_ref, mix_ref, g2_ref, g3_ref, x1_ref, h2_ref):
        mv = mix_ref[...]
        x1 = x_ref[...] + (mv * _rstd(mv)) * g2_ref[...]
        x1_ref[...] = x1
        h2_ref[...] = ((x1 * _rstd(x1)) * g3_ref[...]).astype(BF16)

    return _rowwise(body, "resid_norm_fwd", x.shape[0], 512, [x, mix], [g_post, g_pre], [(D, F32), (D, BF16)])


def _gate_merge(gpre, ys, b_gate):
    def body(gp_ref, ya_ref, yb_ref, yc_ref, b_ref, m_ref):
        acc = None
        for k, y_ref in enumerate((ya_ref, yb_ref, yc_ref)):
            cols = slice(k * D, (k + 1) * D)
            gate = jax.nn.sigmoid(gp_ref[:, cols].astype(F32) + b_ref[:, cols])
            term = gate * y_ref[...].astype(F32)
            acc = term if acc is None else acc + term
        m_ref[...] = acc.astype(BF16)

    return _rowwise(body, "gate_merge", gpre.shape[0], 256, [gpre, *ys], [b_gate], [(D, BF16)])[0]


def _swiglu_fwd(gu):
    def body(a_ref, b_ref, f_ref):
        a = a_ref[...].astype(F32)
        f_ref[...] = (a * jax.nn.sigmoid(a) * b_ref[...].astype(F32)).astype(BF16)

    return _rowwise(body, "swiglu_fwd", gu.shape[0], 256, [(gu, D_FF, 0), (gu, D_FF, 1)], [], [(D_FF, BF16)])[0]


def _loss_head(x1, fo, tgt, g_post):
    def body(x1_ref, fo_ref, t_ref, g_ref, dy_ref, dfo_ref, dg_ref, loss_ref):
        fo_v = fo_ref[...]
        r = _rstd(fo_v)
        n = fo_v * r
        err = (x1_ref[...] + n * g_ref[...]) - t_ref[...]
        loss_ref[...] += _colsum(err * err)
        dy = err * (1.0 / D)
        dy_ref[...] = dy
        dg_ref[...] += _colsum(dy * n)
        dfo_ref[...] = _norm_bwd(dy * g_ref[...], n, r).astype(BF16)

    return _rowwise(body, "loss_head", x1.shape[0], 256, [x1, fo, tgt], [g_post], [(D, F32), (D, BF16)], (D, D))


def _swiglu_bwd(df, gu):
    def body(df_ref, gu_ref, dgu_ref):
        a = gu_ref[:, :D_FF].astype(F32)
        b = gu_ref[:, D_FF:].astype(F32)
        d = df_ref[...].astype(F32)
        s = jax.nn.sigmoid(a)
        dgu_ref[:, :D_FF] = (d * b * (s * (1.0 + a * (1.0 - s)))).astype(BF16)
        dgu_ref[:, D_FF:] = (d * (a * s)).astype(BF16)

    return _rowwise(body, "swiglu_bwd", df.shape[0], 256, [df, gu], [], [(2 * D_FF, BF16)])[0]


def _mid_norm_bwd(dh2, x1, dy, mix, g_pre, g_post):
    def body(dh_ref, x1_ref, dy_ref, mix_ref, g3_ref, g2_ref, dx1_ref, dmix_ref, dg3_ref, dg2_ref):
        x1v = x1_ref[...]
        r3 = _rstd(x1v)
        n3 = x1v * r3
        dh = dh_ref[...]
        dg3_ref[...] += _colsum(dh * n3)
        dx1 = dy_ref[...] + _norm_bwd(dh * g3_ref[...], n3, r3)
        dx1_ref[...] = dx1
        mv = mix_ref[...]
        r2 = _rstd(mv)
        n2 = mv * r2
        dg2_ref[...] += _colsum(dx1 * n2)
        dmix_ref[...] = _norm_bwd(dx1 * g2_ref[...], n2, r2).astype(BF16)

    return _rowwise(body, "mid_norm_bwd", x1.shape[0], 256, [dh2, x1, dy, mix], [g_pre, g_post],
                    [(D, F32), (D, BF16)], (D, D))


def _gate_bwd(dmerged, gpre, ys, b_gate):
    def body(dm_ref, gp_ref, ya_ref, yb_ref, yc_ref, b_ref, dya_ref, dyb_ref, dyc_ref, dgp_ref, db_ref):
        dm = dm_ref[...].astype(F32)
        for k, (y_ref, dy_ref) in enumerate(((ya_ref, dya_ref), (yb_ref, dyb_ref), (yc_ref, dyc_ref))):
            cols = slice(k * D, (k + 1) * D)
            gate = jax.nn.sigmoid(gp_ref[:, cols].astype(F32) + b_ref[:, cols])
            dy_ref[...] = (dm * gate).astype(BF16)
            dgp = (dm * y_ref[...].astype(F32)) * (gate * (1.0 - gate))
            dgp_ref[:, cols] = dgp.astype(BF16)
            db_ref[:, cols] += _colsum(dgp)

    return _rowwise(body, "gate_bwd", gpre.shape[0], 256, [dmerged, gpre, *ys], [b_gate],
                    [(D, BF16), (D, BF16), (D, BF16), (3 * D, BF16)], (3 * D,))


def _in_norm_bwd(dhs, x, dx1, g_pre):
    def body(da_ref, db_ref, dc_ref, x_ref, dx1_ref, g_ref, dx_ref, dg_ref):
        dh = (da_ref[...] + db_ref[...]) + dc_ref[...]
        xv = x_ref[...]
        r = _rstd(xv)
        n = xv * r
        dg_ref[...] += _colsum(dh * n)
        dx_ref[...] = dx1_ref[...] + _norm_bwd(dh * g_ref[...], n, r)

    return _rowwise(body, "in_norm_bwd", x.shape[0], 256, [*dhs, x, dx1], [g_pre], [(D, F32)], (D,))


def _gain_grad(dmn, mem):
    def body(d_ref, m_ref, dg_ref):
        mv = m_ref[...]
        dg_ref[...] += _colsum(d_ref[...] * (mv * _rstd(mv)))

    return _rowwise(body, "mem_gain_grad", mem.shape[0], 256, [dmn, mem], [], [], (D,))[0]


def _dil_combine(outs, lses):
    def body(o0_ref, o1_ref, o2_ref, l0_ref, l1_ref, l2_ref, ob_ref, lse_ref):
        ls = [l0_ref[...], l1_ref[...], l2_ref[...]]
        m = jnp.maximum(jnp.maximum(ls[0], ls[1]), ls[2])
        ws = [jnp.exp(l - m) for l in ls]
        den = (ws[0] + ws[1]) + ws[2]
        num = (ws[0] * o0_ref[...] + ws[1] * o1_ref[...]) + ws[2] * o2_ref[...]
        ob_ref[...] = (num / den).astype(BF16)
        lse_ref[...] = m + jnp.log(den)

    return _rowwise(body, "dil_combine", outs[0].shape[0], 512, [*outs, *lses], [], [(256, BF16), (256, F32)])


def _iotas():
    lane = lax.broadcasted_iota(jnp.int32, (BLK, BLK), 1)
    row = lax.broadcasted_iota(jnp.int32, (BLK, BLK), 0)
    return lane, row, lane < 64


def _split_heads(v, head0):
    zero = jnp.zeros_like(v)
    return jnp.where(head0, v, zero), jnp.where(head0, zero, v)


def _head_rowsum(v, head0):
    return (jnp.sum(jnp.where(head0, v, 0.0), axis=1, keepdims=True),
            jnp.sum(jnp.where(head0, 0.0, v), axis=1, keepdims=True))


def _make_suffix(row, lane):
    tri = (row > lane).astype(BF16)
    tri3 = jnp.concatenate([tri, tri, tri], axis=0)

    def suffix(x):
        return jnp.dot(_split3(x), tri3, preferred_element_type=F32)

    return suffix


def _split3(x):
    hi = x.astype(BF16)
    r1 = x - hi.astype(F32)
    mid = r1.astype(BF16)
    lo = (r1 - mid.astype(F32)).astype(BF16)
    return jnp.concatenate([hi, mid, lo], axis=1)


def _sb_scores(qh, k, causal, suffix, run):
    z = lax.dot_general(qh, k, NT, preferred_element_type=F32) * 0.125
    lb = jnp.minimum(z, 0.0) - jnp.log1p(jnp.exp(-jnp.abs(z)))
    lk = jnp.where(causal, lb - z, 0.0)
    a = jnp.where(causal, jnp.exp(lb + suffix(lk) + run), 0.0)
    return lb, lk, a


def _sb_fwd(proj_a, seq):
    bsz = proj_a.shape[0]
    nq = seq // BLK

    def body(x_ref, o_ref, o32_ref):
        lane, row, head0 = _iotas()
        suffix = _make_suffix(row, lane)

        def qblock(i, carry):
            r0 = pl.multiple_of(i * BLK, BLK)
            qh = _split_heads(x_ref[pl.ds(r0, BLK), 0:128], head0)

            def kblock(jj, st):
                j = i - jj
                c0 = pl.multiple_of(j * BLK, BLK)
                k = x_ref[pl.ds(c0, BLK), 128:256]
                v = x_ref[pl.ds(c0, BLK), 256:384]
                v3 = jnp.concatenate([v, v, v], axis=0)
                causal = (row + jj * BLK) > lane
                new = []
                for h in range(2):
                    run, acc = st[2 * h], st[2 * h + 1]
                    _, lk, a = _sb_scores(qh[h], k, causal, suffix, run)
                    acc = acc + jnp.dot(_split3(a), v3, preferred_element_type=F32)
                    new += [run + jnp.sum(lk, axis=1, keepdims=True), acc]
                return tuple(new)

            z1, z2 = jnp.zeros((BLK, 1), F32), jnp.zeros((BLK, BLK), F32)
            st = lax.fori_loop(0, i + 1, kblock, (z1, z2, z1, z2))
            o = jnp.where(head0, st[1], st[3])
            o32_ref[pl.ds(r0, BLK), :] = o
            o_ref[pl.ds(r0, BLK), :] = o.astype(BF16)
            return carry

        lax.fori_loop(0, nq, qblock, 0)

    out_spec = pl.BlockSpec((None, seq, BLK), lambda b, hp: (b, 0, hp))
    return pl.pallas_call(
        body, name="sb_attn_fwd", grid=(bsz, 4),
        out_shape=(jax.ShapeDtypeStruct((bsz, seq, 512), BF16), jax.ShapeDtypeStruct((bsz, seq, 512), F32)),
        in_specs=[pl.BlockSpec((None, seq, 384), lambda b, hp: (b, 0, hp))],
        out_specs=(out_spec, out_spec),
        compiler_params=_params("parallel", "parallel"),
    )(proj_a)


def _sb_bwd(proj_a, d_o, o_a, seq):
    bsz = proj_a.shape[0]
    nq = seq // BLK

    def body(x_ref, do_ref, o_ref, d_ref, dk_acc, dv_acc):
        lane, row, head0 = _iotas()
        suffix = _make_suffix(row, lane)
        dk_acc[...] = jnp.zeros_like(dk_acc)
        dv_acc[...] = jnp.zeros_like(dv_acc)

        def qblock(i, carry):
            r0 = pl.multiple_of(i * BLK, BLK)
            qh = _split_heads(x_ref[pl.ds(r0, BLK), 0:128], head0)
            do = do_ref[pl.ds(r0, BLK), :]
            doh = _split_heads(do, head0)
            dsum = _head_rowsum(do.astype(F32) * o_ref[pl.ds(r0, BLK), :], head0)

            def kblock(jj, st):
                j = i - jj
                c0 = pl.multiple_of(j * BLK, BLK)
                k = x_ref[pl.ds(c0, BLK), 128:256]
                v = x_ref[pl.ds(c0, BLK), 256:384]
                causal = (row + jj * BLK) > lane
                dqs, new = [], []
                for h in range(2):
                    run, grun = st[1 + 2 * h], st[2 + 2 * h]
                    lb, lk, a = _sb_scores(qh[h], k, causal, suffix, run)
                    g = a * lax.dot_general(doh[h], v, NT, preferred_element_type=F32)
                    beta = jnp.exp(lb)
                    before = dsum[h] - ((grun + suffix(g)) + g)
                    dz = (jnp.where(causal, g * (1.0 - beta) - before * beta, 0.0) * 0.125).astype(BF16)
                    dqs.append(jnp.dot(dz, k, preferred_element_type=F32))
                    dk_acc[pl.ds(c0, BLK), :] += lax.dot_general(dz, qh[h], TN, preferred_element_type=F32)
                    dv_acc[pl.ds(c0, BLK), :] += lax.dot_general(a.astype(BF16), doh[h], TN,
                                                                 preferred_element_type=F32)
                    new += [run + jnp.sum(lk, axis=1, keepdims=True), grun + jnp.sum(g, axis=1, keepdims=True)]
                return (st[0] + jnp.where(head0, dqs[0], dqs[1]), *new)

            z1 = jnp.zeros((BLK, 1), F32)
            st = lax.fori_loop(0, i + 1, kblock, (jnp.zeros((BLK, BLK), F32), z1, z1, z1, z1))
            d_ref[pl.ds(r0, BLK), 0:128] = st[0].astype(BF16)
            return carry

        lax.fori_loop(0, nq, qblock, 0)
        d_ref[:, 128:256] = dk_acc[...].astype(BF16)
        d_ref[:, 256:384] = dv_acc[...].astype(BF16)

    return pl.pallas_call(
        body, name="sb_attn_bwd", grid=(bsz, 4),
        out_shape=jax.ShapeDtypeStruct((bsz, seq, WA), BF16),
        in_specs=[pl.BlockSpec((None, seq, 384), lambda b, hp: (b, 0, hp)),
                  pl.BlockSpec((None, seq, BLK), lambda b, hp: (b, 0, hp)),
                  pl.BlockSpec((None, seq, BLK), lambda b, hp: (b, 0, hp))],
        out_specs=pl.BlockSpec((None, seq, 384), lambda b, hp: (b, 0, hp)),
        scratch_shapes=[pltpu.VMEM((seq, BLK), F32), pltpu.VMEM((seq, BLK), F32)],
        compiler_params=_params("parallel", "parallel"),
    )(proj_a, d_o, o_a)


def _rope_tables(seq):
    inv_freq = ROPE_THETA ** (-jnp.arange(32, dtype=F32) * 2.0 / 64)
    ang = jnp.arange(seq).astype(F32)[:, None] * inv_freq[None, :]
    cos, sin = jnp.cos(ang), jnp.sin(ang)
    return jnp.tile(cos, (1, 4)), jnp.concatenate([-sin, sin, -sin, sin], axis=1)


def _make_rope(n_rows):
    lane = lax.broadcasted_iota(jnp.int32, (n_rows, BLK), 1)
    first = (lane & 63) < 32

    def rope(x, cos, sin):
        partner = jnp.where(first, pltpu.roll(x, 96, 1), pltpu.roll(x, 32, 1))
        return x * cos + partner * sin

    return rope


def _dil_masks(row, lane, i):
    return row >= lane, jnp.logical_and(lane >= row, i > 0)


def _dil_fwd(proj_b, cos_t, sin_t, group, bsz, seq):
    dil = DIL_GROUPS[group]
    sub = seq // dil
    nq = sub // BLK

    def body(x_ref, cos_ref, sin_ref, o_ref, lse_ref, qs, ks):
        lane, row, head0 = _iotas()
        rope = _make_rope(sub)
        cos, sin = cos_ref[...], sin_ref[...]
        qs[...] = rope(x_ref[:, 0:128].astype(F32), cos, sin).astype(BF16)
        ks[...] = rope(x_ref[:, 128:256].astype(F32), cos, sin).astype(BF16)

        def qblock(i, carry):
            r0 = pl.multiple_of(i * BLK, BLK)
            p0 = pl.multiple_of(jnp.maximum(i - 1, 0) * BLK, BLK)
            qh = _split_heads(qs[pl.ds(r0, BLK), :], head0)
            kc, vc = ks[pl.ds(r0, BLK), :], x_ref[pl.ds(r0, BLK), 256:384]
            kp, vp = ks[pl.ds(p0, BLK), :], x_ref[pl.ds(p0, BLK), 256:384]
            valid_c, valid_p = _dil_masks(row, lane, i)
            outs, lses = [], []
            for h in range(2):
                sc = jnp.where(valid_c, lax.dot_general(qh[h], kc, NT, preferred_element_type=F32) * 0.125, NEG_INF)
                sp = jnp.where(valid_p, lax.dot_general(qh[h], kp, NT, preferred_element_type=F32) * 0.125, NEG_INF)
                m = jnp.maximum(jnp.max(sc, axis=1, keepdims=True), jnp.max(sp, axis=1, keepdims=True))
                pc, pp = jnp.exp(sc - m), jnp.exp(sp - m)
                den = jnp.sum(pc, axis=1, keepdims=True) + jnp.sum(pp, axis=1, keepdims=True)
                num = (jnp.dot(pc.astype(BF16), vc, preferred_element_type=F32)
                       + jnp.dot(pp.astype(BF16), vp, preferred_element_type=F32))
                outs.append(num / den)
                lses.append(jnp.broadcast_to(m + jnp.log(den), (BLK, BLK)))
            o_ref[pl.ds(r0, BLK), :] = jnp.where(head0, outs[0], outs[1])
            lse_ref[pl.ds(r0, BLK), :] = jnp.where(head0, lses[0], lses[1])
            return carry

        lax.fori_loop(0, nq, qblock, 0)

    out_view = jax.ShapeDtypeStruct((bsz, sub, dil * 256), F32)
    out_spec = pl.BlockSpec((None, sub, BLK), lambda b, r, hp: (b, 0, 2 * r + hp))
    tab_spec = pl.BlockSpec((sub, BLK), lambda b, r, hp: (0, r))
    o, lse = pl.pallas_call(
        body, name=f"dil_attn_fwd_g{group}", grid=(bsz, dil, 2),
        out_shape=(out_view, out_view),
        in_specs=[pl.BlockSpec((None, sub, 384), lambda b, r, hp: (b, 0, 6 * r + 2 * group + hp)), tab_spec, tab_spec],
        out_specs=(out_spec, out_spec),
        scratch_shapes=[pltpu.VMEM((sub, BLK), BF16), pltpu.VMEM((sub, BLK), BF16)],
        compiler_params=_params("parallel", "parallel", "parallel"),
    )(proj_b.reshape(bsz, sub, dil * WB), cos_t.reshape(sub, dil * BLK), sin_t.reshape(sub, dil * BLK))
    return o.reshape(bsz * seq, 256), lse.reshape(bsz * seq, 256)


def _dil_bwd(proj_b, cos_t, sin_t, d_ob, o_b, lse, d_proj_b, group, bsz, seq):
    dil = DIL_GROUPS[group]
    sub = seq // dil
    nq = sub // BLK

    def body(x_ref, cos_ref, sin_ref, do_ref, ob_ref, lse_ref, *rest):
        d_ref, qs, ks, dq_s, dk_acc, dv_acc = rest[-6:]
        lane, row, head0 = _iotas()
        rope = _make_rope(sub)
        cos, sin = cos_ref[...], sin_ref[...]
        qs[...] = rope(x_ref[:, 0:128].astype(F32), cos, sin).astype(BF16)
        ks[...] = rope(x_ref[:, 128:256].astype(F32), cos, sin).astype(BF16)
        dk_acc[...] = jnp.zeros_like(dk_acc)
        dv_acc[...] = jnp.zeros_like(dv_acc)

        def qblock(i, carry):
            r0 = pl.multiple_of(i * BLK, BLK)
            p0 = pl.multiple_of(jnp.maximum(i - 1, 0) * BLK, BLK)
            qh = _split_heads(qs[pl.ds(r0, BLK), :], head0)
            do = do_ref[pl.ds(r0, BLK), :]
            doh = _split_heads(do, head0)
            dsum = _head_rowsum(do.astype(F32) * ob_ref[pl.ds(r0, BLK), :].astype(F32), head0)
            lse_v = lse_ref[pl.ds(r0, BLK), :]
            lse_h = (lse_v[:, 0:1], lse_v[:, 64:65])
            dq = jnp.zeros((BLK, BLK), F32)
            for c0, valid in zip((r0, p0), _dil_masks(row, lane, i)):
                k, v = ks[pl.ds(c0, BLK), :], x_ref[pl.ds(c0, BLK), 256:384]
                dqs = []
                for h in range(2):
                    s = lax.dot_general(qh[h], k, NT, preferred_element_type=F32) * 0.125
                    p = jnp.where(valid, jnp.exp(s - lse_h[h]), 0.0)
                    dp = lax.dot_general(doh[h], v, NT, preferred_element_type=F32)
                    ds = ((p * (dp - dsum[h])) * 0.125).astype(BF16)
                    dqs.append(jnp.dot(ds, k, preferred_element_type=F32))
                    dk_acc[pl.ds(c0, BLK), :] += lax.dot_general(ds, qh[h], TN, preferred_element_type=F32)
                    dv_acc[pl.ds(c0, BLK), :] += lax.dot_general(p.astype(BF16), doh[h], TN,
                                                                 preferred_element_type=F32)
                dq = dq + jnp.where(head0, dqs[0], dqs[1])
            dq_s[pl.ds(r0, BLK), :] = dq
            return carry

        lax.fori_loop(0, nq, qblock, 0)
        d_ref[:, 0:128] = rope(dq_s[...], cos, -sin).astype(BF16)
        d_ref[:, 128:256] = rope(dk_acc[...], cos, -sin).astype(BF16)
        d_ref[:, 256:384] = dv_acc[...].astype(BF16)

    x_spec = pl.BlockSpec((None, sub, 384), lambda b, r, hp: (b, 0, 6 * r + 2 * group + hp))
    tab_spec = pl.BlockSpec((sub, BLK), lambda b, r, hp: (0, r))
    tok_spec = pl.BlockSpec((None, sub, BLK), lambda b, r, hp: (b, 0, 2 * r + hp))
    args = [proj_b.reshape(bsz, sub, dil * WB), cos_t.reshape(sub, dil * BLK), sin_t.reshape(sub, dil * BLK),
            d_ob.reshape(bsz, sub, dil * 256), o_b.reshape(bsz, sub, dil * 256), lse.reshape(bsz, sub, dil * 256)]
    in_specs = [x_spec, tab_spec, tab_spec, tok_spec, tok_spec, tok_spec]
    aliases = {}
    if d_proj_b is not None:
        args.append(d_proj_b.reshape(bsz, sub, dil * WB))
        in_specs.append(ANY)
        aliases = {6: 0}
    out = pl.pallas_call(
        body, name=f"dil_attn_bwd_g{group}", grid=(bsz, dil, 2),
        out_shape=jax.ShapeDtypeStruct((bsz, sub, dil * WB), BF16),
        in_specs=in_specs, out_specs=x_spec, input_output_aliases=aliases,
        scratch_shapes=[pltpu.VMEM((sub, BLK), BF16), pltpu.VMEM((sub, BLK), BF16), pltpu.VMEM((sub, BLK), F32),
                        pltpu.VMEM((sub, BLK), F32), pltpu.VMEM((sub, BLK), F32)],
        compiler_params=_params("parallel", "parallel", "parallel"),
    )(*args)
    return out.reshape(bsz, seq, WB)


MEM_SCALE = 128 ** -0.5
MEM_QB = 256


def _mem_fwd(proj_a, kv, seq):
    bsz = proj_a.shape[0]

    def body(q_ref, k_ref, v_ref, o_ref):
        k, v = k_ref[...], v_ref[...]

        def qblock(i, carry):
            r0 = pl.multiple_of(i * MEM_QB, MEM_QB)
            s = lax.dot_general(q_ref[pl.ds(r0, MEM_QB), :], k, NT, preferred_element_type=F32) * MEM_SCALE
            p = jnp.exp(s - jnp.max(s, axis=1, keepdims=True))
            p = p / jnp.sum(p, axis=1, keepdims=True)
            o_ref[pl.ds(r0, MEM_QB), :] = jnp.dot(p.astype(BF16), v, preferred_element_type=F32).astype(BF16)
            return carry

        lax.fori_loop(0, seq // MEM_QB, qblock, 0)

    return pl.pallas_call(
        body, name="mem_attn_fwd", grid=(bsz, 4),
        out_shape=jax.ShapeDtypeStruct((bsz, seq, 512), BF16),
        in_specs=[pl.BlockSpec((None, seq, BLK), lambda b, h: (b, 0, 12 + h)),
                  pl.BlockSpec((None, MEM_LEN, BLK), lambda b, h: (b, 0, h)),
                  pl.BlockSpec((None, MEM_LEN, BLK), lambda b, h: (b, 0, 4 + h))],
        out_specs=pl.BlockSpec((None, seq, BLK), lambda b, h: (b, 0, h)),
        compiler_params=_params("parallel", "parallel"),
    )(proj_a, kv, kv)


def _mem_bwd(proj_a, kv, d_o, d_proj_a, seq):
    bsz = proj_a.shape[0]

    def body(q_ref, k_ref, v_ref, do_ref, _, dq_ref, dk_ref, dv_ref):
        k, v = k_ref[...], v_ref[...]

        def qblock(i, carry):
            dk, dv = carry
            r0 = pl.multiple_of(i * MEM_QB, MEM_QB)
            q, do = q_ref[pl.ds(r0, MEM_QB), :], do_ref[pl.ds(r0, MEM_QB), :]
            s = lax.dot_general(q, k, NT, preferred_element_type=F32) * MEM_SCALE
            p = jnp.exp(s - jnp.max(s, axis=1, keepdims=True))
            p = p / jnp.sum(p, axis=1, keepdims=True)
            dp = lax.dot_general(do, v, NT, preferred_element_type=F32)
            ds = ((p * (dp - jnp.sum(p * dp, axis=1, keepdims=True))) * MEM_SCALE).astype(BF16)
            dq_ref[pl.ds(r0, MEM_QB), :] = jnp.dot(ds, k, preferred_element_type=F32).astype(BF16)
            dk = dk + lax.dot_general(ds, q, TN, preferred_element_type=F32)
            dv = dv + lax.dot_general(p.astype(BF16), do, TN, preferred_element_type=F32)
            return dk, dv

        zero = jnp.zeros((MEM_LEN, BLK), F32)
        dk, dv = lax.fori_loop(0, seq // MEM_QB, qblock, (zero, zero))
        dk_ref[...] = dk.astype(BF16)
        dv_ref[...] = dv.astype(BF16)

    kv_spec = pl.BlockSpec((None, MEM_LEN, BLK), lambda b, h: (b, 0, h))
    return pl.pallas_call(
        body, name="mem_attn_bwd", grid=(bsz, 4),
        out_shape=(jax.ShapeDtypeStruct((bsz, seq, WA), BF16), jax.ShapeDtypeStruct((bsz, MEM_LEN, 512), BF16),
                   jax.ShapeDtypeStruct((bsz, MEM_LEN, 512), BF16)),
        in_specs=[pl.BlockSpec((None, seq, BLK), lambda b, h: (b, 0, 12 + h)), kv_spec,
                  pl.BlockSpec((None, MEM_LEN, BLK), lambda b, h: (b, 0, 4 + h)),
                  pl.BlockSpec((None, seq, BLK), lambda b, h: (b, 0, h)), ANY],
        out_specs=(pl.BlockSpec((None, seq, BLK), lambda b, h: (b, 0, 12 + h)), kv_spec, kv_spec),
        input_output_aliases={4: 0},
        compiler_params=_params("parallel", "parallel"),
    )(proj_a, kv, kv, d_o, d_proj_a)


def _mesh_pos():
    return lax.axis_index("x"), lax.axis_index("y"), lax.axis_index("c")


def _all_gather(shard, name):
    m_per, n = shard.shape

    def body(x_ref, out_ref, send_sems, recv_sems, local_sem):
        x, y, c = _mesh_pos()
        me, sibling = (x, y, c), (x, y, 1 - c)
        chips = [(1 - x, y), (x, 1 - y), (1 - x, 1 - y)]

        def rows(px, py, pc):
            return out_ref.at[pl.ds((4 * px + 2 * py + pc) * m_per, m_per), :]

        def copy(k, block, to, src=None):
            return pltpu.make_async_remote_copy(
                src_ref=rows(*block) if src is None else src, dst_ref=rows(*block),
                send_sem=send_sems.at[k], recv_sem=recv_sems.at[k], device_id=to, device_id_type=MESH)

        mine = pltpu.make_async_copy(x_ref, rows(*me), local_sem)
        mine.start()
        first = [copy(0, me, sibling, src=x_ref)]
        first += [copy(1 + j, me, (*chip, c), src=x_ref) for j, chip in enumerate(chips)]
        for cp in first:
            cp.start()
        passed = [copy(4 + j, (*chip, c), sibling) for j, chip in enumerate(chips)]
        for j, chip in enumerate(chips):
            copy(1 + j, (*chip, c), me).wait_recv()
            passed[j].start()
        copy(0, sibling, me).wait_recv()
        for j, chip in enumerate(chips):
            copy(4 + j, (*chip, 1 - c), me).wait_recv()
        for cp in first + passed:
            cp.wait_send()
        mine.wait()

    return pl.pallas_call(
        body, name=name, out_shape=jax.ShapeDtypeStruct((N_DEV * m_per, n), shard.dtype),
        in_specs=[ANY], out_specs=ANY,
        scratch_shapes=[pltpu.SemaphoreType.DMA((7,)), pltpu.SemaphoreType.DMA((7,)), pltpu.SemaphoreType.DMA(())],
    )(shard)


def _exchange(parts):
    n_rows, n = parts.shape[1:]

    def body(g_ref, out_ref, send_sems, recv_sems, local_sem):
        x, y, c = _mesh_pos()
        me = 4 * x + 2 * y + c
        own = pltpu.make_async_copy(g_ref.at[me], out_ref.at[me], local_sem)
        own.start()
        sends, recvs = [], []
        for k in range(1, N_DEV):
            px = 1 - x if k & 4 else x
            py = 1 - y if k & 2 else y
            pc = 1 - c if k & 1 else c
            peer = 4 * px + 2 * py + pc
            sems = dict(send_sem=send_sems.at[k - 1], recv_sem=recv_sems.at[k - 1],
                        device_id=(px, py, pc), device_id_type=MESH)
            sends.append(pltpu.make_async_remote_copy(src_ref=g_ref.at[peer], dst_ref=out_ref.at[me], **sems))
            recvs.append(pltpu.make_async_remote_copy(src_ref=g_ref.at[me], dst_ref=out_ref.at[peer], **sems))
        for cp in sends:
            cp.start()
        for cp in recvs:
            cp.wait_recv()
        for cp in sends:
            cp.wait_send()
        own.wait()

    return pl.pallas_call(
        body, name="grad_exchange", out_shape=jax.ShapeDtypeStruct((N_DEV, n_rows, n), parts.dtype),
        in_specs=[ANY], out_specs=ANY,
        scratch_shapes=[pltpu.SemaphoreType.DMA((7,)), pltpu.SemaphoreType.DMA((7,)), pltpu.SemaphoreType.DMA(())],
    )(parts)


def _adamw(w, g, m, v):
    m = ADAM_B1 * m + (1.0 - ADAM_B1) * g
    v = ADAM_B2 * v + (1.0 - ADAM_B2) * (g * g)
    m_hat = m / (1.0 - ADAM_B1 ** ADAM_STEP)
    v_hat = v / (1.0 - ADAM_B2 ** ADAM_STEP)
    return -ADAM_LR * (m_hat / (jnp.sqrt(v_hat) + ADAM_EPS) + ADAM_WD * w), m, v


def _reduce_adamw(recv, w, m, v):
    rows = w.shape[0]
    tr = 240

    def body(r_ref, w_ref, m_ref, v_ref, g_out, d_out, m_out, v_out):
        g = r_ref[0].astype(F32)
        for s in range(1, N_DEV):
            g = g + r_ref[s].astype(F32)
        g_out[...] = g
        d_out[...], m_out[...], v_out[...] = _adamw(w_ref[...], g, m_ref[...], v_ref[...])

    spec = pl.BlockSpec((tr, D), lambda i: (i, 0))
    return pl.pallas_call(
        body, name="reduce_adamw", grid=(rows // tr,),
        out_shape=[jax.ShapeDtypeStruct((rows, D), F32)] * 4,
        in_specs=[pl.BlockSpec((N_DEV, tr, D), lambda i: (0, i, 0)), spec, spec, spec],
        out_specs=[spec] * 4, compiler_params=_params("arbitrary"),
    )(recv, w, m, v)


def _small_adamw(gathered, w, m, v):
    def body(g_ref, w_ref, m_ref, v_ref, g_out, d_out, m_out, v_out, loss_out):
        tot = g_ref[0]
        for s in range(1, N_DEV):
            tot = tot + g_ref[s]
        g = tot[0:8]
        g_out[...] = g
        d_out[...], m_out[...], v_out[...] = _adamw(w_ref[...], g, m_ref[...], v_ref[...])
        loss_out[...] = jnp.broadcast_to((0.5 / D) * jnp.sum(tot[8:9], axis=1, keepdims=True), (8, BLK))

    out = [jax.ShapeDtypeStruct((8, D), F32)] * 4 + [jax.ShapeDtypeStruct((8, BLK), F32)]
    return pl.pallas_call(body, name="small_adamw", out_shape=out, compiler_params=_params())(gathered, w, m, v)


def _pack(ws, dtype):
    return jnp.concatenate([w.astype(dtype).reshape(-1, D) for w in ws], axis=0)


def _pick_chunks(w, chunks):
    return jnp.concatenate([w[:, BLK * c:BLK * (c + 1)] for c in chunks], axis=1)


def kernel(x, mem, g_pre_mix, g_post_mix, g_pre_ffn, g_post_ffn, g_mem, w_in, w_mem_kv, w_br_sb, w_br_dil, w_br_mem, w_gate, b_gate, w_o, w_ffn_in, w_ffn_out, loss_target, m_g_pre_mix, m_g_post_mix, m_g_pre_ffn, m_g_post_ffn, m_g_mem, m_w_in, m_w_mem_kv, m_w_br_sb, m_w_br_dil, m_w_br_mem, m_w_gate, m_b_gate, m_w_o, m_w_ffn_in, m_w_ffn_out, v_g_pre_mix, v_g_post_mix, v_g_pre_ffn, v_g_post_ffn, v_g_mem, v_w_in, v_w_mem_kv, v_w_br_sb, v_w_br_dil, v_w_br_mem, v_w_gate, v_b_gate, v_w_o, v_w_ffn_in, v_w_ffn_out):
    bsz, seq, _ = x.shape
    tokens = bsz * seq
    xf, tgt, memf = x.reshape(tokens, D), loss_target.reshape(tokens, D), mem.reshape(bsz * MEM_LEN, D)
    big_w = [w_in, w_mem_kv, w_br_sb, w_br_dil, w_br_mem, w_gate, w_o, w_ffn_in, w_ffn_out]
    big_m = [m_w_in, m_w_mem_kv, m_w_br_sb, m_w_br_dil, m_w_br_mem, m_w_gate, m_w_o, m_w_ffn_in, m_w_ffn_out]
    big_v = [v_w_in, v_w_mem_kv, v_w_br_sb, v_w_br_dil, v_w_br_mem, v_w_gate, v_w_o, v_w_ffn_in, v_w_ffn_out]
    shard_shapes = [w.shape for w in big_w]

    gathered = _all_gather(_pack([w[0] for w in big_w], BF16), "weight_all_gather").reshape(N_DEV, PACK_TOTAL, D)

    def full(i, by_rows):
        part = gathered[:, PACK_OFFS[i]:PACK_OFFS[i] + PACK_ROWS[i]]
        k, n = shard_shapes[i][1:]
        if by_rows:
            return part.reshape(N_DEV * k, n)
        return part.reshape(N_DEV, k, n).transpose(1, 0, 2).reshape(k, N_DEV * n)

    fw_in, fw_mem_kv, fw_br_sb, fw_br_dil, fw_br_mem = full(0, False), full(1, True), full(2, False), full(3, False), full(4, False)
    fw_gate, fw_o, fw_ffn_in, fw_ffn_out = full(5, False), full(6, True), full(7, False), full(8, True)
    w_a, w_b = _pick_chunks(fw_in, CHUNKS_A), _pick_chunks(fw_in, CHUNKS_B)

    h = _norm_fwd(xf, g_pre_mix, "pre_mix_norm")
    proj_a = _matmul(h, w_a, "nn", BF16, "proj_a").reshape(bsz, seq, WA)
    proj_b = _matmul(h, w_b, "nn", BF16, "proj_b")
    gpre = _matmul(h, fw_gate, "nn", BF16, "gate_proj")
    o_a, o_a32 = _sb_fwd(proj_a, seq)
    cos_t, sin_t = _rope_tables(seq)
    dil = [_dil_fwd(proj_b, cos_t, sin_t, g, bsz, seq) for g in range(3)]
    o_b, lse_b = _dil_combine([o for o, _ in dil], [l for _, l in dil])
    mn = _norm_fwd(memf, g_mem, "mem_norm")
    kv = _matmul(mn, fw_mem_kv, "nn", BF16, "mem_kv_proj").reshape(bsz, MEM_LEN, D)
    o_c = _mem_fwd(proj_a, kv, seq)
    o_a2, o_c2 = o_a.reshape(tokens, 512), o_c.reshape(tokens, 512)
    ys = [_matmul(o_a2, fw_br_sb, "nn", BF16, "branch_sb"), _matmul(o_b, fw_br_dil, "nn", BF16, "branch_dil"),
          _matmul(o_c2, fw_br_mem, "nn", BF16, "branch_mem")]
    merged = _gate_merge(gpre, ys, b_gate)
    mix = _matmul(merged, fw_o, "nn", F32, "out_proj")
    x1, h2 = _resid_norm_fwd(xf, mix, g_post_mix, g_pre_ffn)
    gu = _matmul(h2, fw_ffn_in, "nn", BF16, "ffn_in")
    f = _swiglu_fwd(gu)
    fo = _matmul(f, fw_ffn_out, "nn", F32, "ffn_out")
    dy, dfo, dg_post_ffn, loss_lanes = _loss_head(x1, fo, tgt, g_post_ffn)

    df = _matmul(dfo, fw_ffn_out, "nt", BF16, "d_ffn_act")
    gw_ffn_out = _matmul(f, dfo, "tn", F32, "gw_ffn_out")
    dgu = _swiglu_bwd(df, gu)
    dh2 = _matmul(dgu, fw_ffn_in, "nt", F32, "d_h2")
    gw_ffn_in = _matmul(h2, dgu, "tn", F32, "gw_ffn_in")
    dx1, dmix, dg_pre_ffn, dg_post_mix = _mid_norm_bwd(dh2, x1, dy, mix, g_pre_ffn, g_post_mix)
    dmerged = _matmul(dmix, fw_o, "nt", BF16, "d_merged")
    gw_o = _matmul(merged, dmix, "tn", F32, "gw_o")
    dya, dyb, dyc, dgpre, db_gate = _gate_bwd(dmerged, gpre, ys, b_gate)
    d_oa = _matmul(dya, fw_br_sb, "nt", BF16, "d_o_sb").reshape(bsz, seq, 512)
    d_ob = _matmul(dyb, fw_br_dil, "nt", BF16, "d_o_dil")
    d_oc = _matmul(dyc, fw_br_mem, "nt", BF16, "d_o_mem").reshape(bsz, seq, 512)
    gw_br_sb = _matmul(o_a2, dya, "tn", F32, "gw_br_sb")
    gw_br_dil = _matmul(o_b, dyb, "tn", F32, "gw_br_dil")
    gw_br_mem = _matmul(o_c2, dyc, "tn", F32, "gw_br_mem")
    d_proj_a = _sb_bwd(proj_a, d_oa, o_a32, seq)
    d_proj_a, dk_m, dv_m = _mem_bwd(proj_a, kv, d_oc, d_proj_a, seq)
    d_proj_b = None
    for g in range(3):
        d_proj_b = _dil_bwd(proj_b, cos_t, sin_t, d_ob, o_b, lse_b, d_proj_b, g, bsz, seq)
    d_proj_a, d_proj_b = d_proj_a.reshape(tokens, WA), d_proj_b.reshape(tokens, WB)
    dhs = [_matmul(d_proj_a, w_a, "nt", F32, "d_h_a"), _matmul(d_proj_b, w_b, "nt", F32, "d_h_b"),
           _matmul(dgpre, fw_gate, "nt", F32, "d_h_gate")]
    gw_a = _matmul(h, d_proj_a, "tn", F32, "gw_in_a")
    gw_b = _matmul(h, d_proj_b, "tn", F32, "gw_in_b")
    gw_gate = _matmul(h, dgpre, "tn", F32, "gw_gate")
    dx, dg_pre_mix = _in_norm_bwd(dhs, xf, dx1, g_pre_mix)
    dkv = jnp.concatenate([dk_m, dv_m], axis=-1).reshape(bsz * MEM_LEN, D)
    gw_mem_kv = _matmul(mn, dkv, "tn", F32, "gw_mem_kv")
    dmn = _matmul(dkv, fw_mem_kv, "nt", F32, "d_mem_norm")
    dg_mem = _gain_grad(dmn, memf)

    gw_ab = jnp.concatenate([gw_a, gw_b], axis=1)
    where = {c: i for i, c in enumerate(CHUNKS_A + CHUNKS_B)}
    gw_in = _pick_chunks(gw_ab, [where[c] for c in range(34)])
    full_grads = [gw_in, gw_mem_kv, gw_br_sb, gw_br_dil, gw_br_mem, gw_gate, gw_o, gw_ffn_in, gw_ffn_out]
    by_rows = [False, True, False, False, False, False, True, False, True]
    parts = []
    for gw, shp, rows_sharded in zip(full_grads, shard_shapes, by_rows):
        k, n = shp[1:]
        if rows_sharded:
            parts.append(gw.astype(BF16).reshape(N_DEV, (k * n) // D, D))
        else:
            parts.append(gw.astype(BF16).reshape(k, N_DEV, n).transpose(1, 0, 2).reshape(N_DEV, (k * n) // D, D))
    recv = _exchange(jnp.concatenate(parts, axis=1))
    g_p, d_p, m_p, v_p = _reduce_adamw(recv, _pack([w[0] for w in big_w], F32), _pack([m[0] for m in big_m], F32),
                                       _pack([v[0] for v in big_v], F32))

    def unpack(p):
        return [p[PACK_OFFS[i]:PACK_OFFS[i] + PACK_ROWS[i]].reshape(shard_shapes[i]) for i in range(len(big_w))]

    big = [unpack(p) for p in (g_p, d_p, m_p, v_p)]

    small = jnp.concatenate([dg_pre_mix, dg_post_mix, dg_pre_ffn, dg_post_ffn, dg_mem, db_gate.reshape(3, D),
                             loss_lanes, jnp.zeros((7, D), F32)], axis=0)
    small_all = _all_gather(small, "small_all_gather").reshape(N_DEV, 16, D)

    def small_pack(gs, b):
        return jnp.concatenate([*gs, b.reshape(3, D)], axis=0)

    sm = _small_adamw(
        small_all, small_pack([g_pre_mix, g_post_mix, g_pre_ffn, g_post_ffn, g_mem], b_gate),
        small_pack([m_g_pre_mix, m_g_post_mix, m_g_pre_ffn, m_g_post_ffn, m_g_mem], m_b_gate),
        small_pack([v_g_pre_mix, v_g_post_mix, v_g_pre_ffn, v_g_post_ffn, v_g_mem], v_b_gate))
    loss = sm[4][0, 0]

    def leaves(k):
        t, bw = sm[k], big[k]
        return [t[0:1], t[1:2], t[2:3], t[3:4], t[4:5], *bw[0:6], t[5:8].reshape(1, 3 * D), *bw[6:9]]

    return (loss, dx.reshape(bsz, seq, D), *leaves(0), *leaves(1), *leaves(2), *leaves(3))
```

```python
import functools

import jax
import jax.numpy as jnp
from jax import lax
from jax.experimental import pallas as pl
from jax.experimental.pallas import tpu as pltpu

F32 = jnp.float32
BF16 = jnp.bfloat16
D = 1024
BLK = 128
MEM_LEN = 256
D_FF = 2816
NORM_EPS = 1e-6
NEG_INF = -1e30
ROPE_THETA = 10000.0
ADAM_LR, ADAM_B1, ADAM_B2, ADAM_EPS, ADAM_WD, ADAM_STEP = 0.001, 0.9, 0.999, 1e-08, 0.01, 10
N_DEV = 8
VMEM_LIMIT_BYTES = 56 * 1024 * 1024
MESH = pl.DeviceIdType.MESH
ANY = pl.BlockSpec(memory_space=pl.ANY)

NT = (((1,), (1,)), ((), ()))
TN = (((0,), (0,)), ((), ()))
NN = (((1,), (0,)), ((), ()))
_DIMS = {"nn": NN, "nt": NT, "tn": TN}

PACK_NAMES = ("w_in", "w_mem_kv", "w_br_sb", "w_br_dil", "w_br_mem", "w_gate", "w_o", "w_ffn_in", "w_ffn_out")
PACK_ROWS = (544, 128, 64, 32, 64, 384, 128, 704, 352)
PACK_OFFS = tuple(sum(PACK_ROWS[:i]) for i in range(len(PACK_ROWS)))
PACK_TOTAL = sum(PACK_ROWS)

CHUNKS_A = tuple(c for hp in range(4) for c in (hp, 4 + hp, 8 + hp)) + (30, 31, 32, 33)
CHUNKS_B = tuple(c for hp in range(2) for g in range(3) for c in (12 + 6 * g + hp, 14 + 6 * g + hp, 16 + 6 * g + hp))
WA, WB = 128 * len(CHUNKS_A), 128 * len(CHUNKS_B)
DIL_GROUPS = (1, 4, 16)


def _params(*sem):
    return pltpu.CompilerParams(dimension_semantics=sem or None, vmem_limit_bytes=VMEM_LIMIT_BYTES)


def _tile(n, cap):
    if n <= 128:
        return n
    assert n % 128 == 0, n
    best = 128
    for t in range(128, min(n, cap) + 1, 128):
        if n % t == 0:
            best = t
    return best


def _matmul(a, b, mode, out_dtype, name, tm_cap=512, tn_cap=1536, tk_cap=1536):
    if mode == "tn":
        (K, M), N = a.shape, b.shape[1]
    elif mode == "nt":
        (M, K), N = a.shape, b.shape[0]
    else:
        (M, K), N = a.shape, b.shape[1]
    tm, tn, tk = _tile(M, tm_cap), _tile(N, tn_cap), _tile(K, tk_cap)
    nm, nn, nk = M // tm, N // tn, K // tk
    dims = _DIMS[mode]

    def body(a_ref, b_ref, o_ref, *acc):
        p = lax.dot_general(a_ref[...], b_ref[...], dims, preferred_element_type=F32)
        if nk == 1:
            o_ref[...] = p.astype(o_ref.dtype)
        else:
            acc_ref, = acc
            k = pl.program_id(2)

            @pl.when(k == 0)
            def _():
                acc_ref[...] = p

            @pl.when(k > 0)
            def _():
                acc_ref[...] += p

            @pl.when(k == nk - 1)
            def _():
                o_ref[...] = acc_ref[...].astype(o_ref.dtype)

    n_outer = nk == 1 and (a.size * nn + b.size) < (a.size + b.size * nm)
    if n_outer:
        grid, ij = (nn, nm, nk), (lambda g0, g1: (g1, g0))
    else:
        grid, ij = (nm, nn, nk), (lambda g0, g1: (g0, g1))
    if mode == "tn":
        a_spec = pl.BlockSpec((tk, tm), lambda g0, g1, k: (k, ij(g0, g1)[0]))
    else:
        a_spec = pl.BlockSpec((tm, tk), lambda g0, g1, k: (ij(g0, g1)[0], k))
    if mode == "nt":
        b_spec = pl.BlockSpec((tn, tk), lambda g0, g1, k: (ij(g0, g1)[1], k))
    else:
        b_spec = pl.BlockSpec((tk, tn), lambda g0, g1, k: (k, ij(g0, g1)[1]))
    return pl.pallas_call(
        body, name=name, grid=grid,
        out_shape=jax.ShapeDtypeStruct((M, N), out_dtype),
        in_specs=[a_spec, b_spec],
        out_specs=pl.BlockSpec((tm, tn), lambda g0, g1, k: ij(g0, g1)),
        scratch_shapes=[pltpu.VMEM((tm, tn), F32)] if nk > 1 else [],
        compiler_params=_params("parallel", "parallel", "arbitrary"),
    )(a, b)


def _rowwise(body, name, rows, tr, row_ins, vec_ins, row_outs, acc_outs=()):
    tr = min(tr, rows)
    assert rows % tr == 0
    in_specs, args = [], []
    for r in row_ins:
        arr, w, cb = r if isinstance(r, tuple) else (r, r.shape[1], 0)
        in_specs.append(pl.BlockSpec((tr, w), functools.partial(lambda i, cb: (i, cb), cb=cb)))
        args.append(arr)
    for v in vec_ins:
        in_specs.append(pl.BlockSpec(v.shape, lambda i: (0, 0)))
        args.append(v)
    out_shape = [jax.ShapeDtypeStruct((rows, w), dt) for w, dt in row_outs]
    out_shape += [jax.ShapeDtypeStruct((1, w), F32) for w in acc_outs]
    out_specs = [pl.BlockSpec((tr, w), lambda i: (i, 0)) for w, _ in row_outs]
    out_specs += [pl.BlockSpec((1, w), lambda i: (0, 0)) for w in acc_outs]
    n_acc = len(acc_outs)

    def wrapped(*refs):
        if n_acc:
            @pl.when(pl.program_id(0) == 0)
            def _():
                for r in refs[len(refs) - n_acc:]:
                    r[...] = jnp.zeros_like(r)
        body(*refs)

    return pl.pallas_call(
        wrapped, name=name, grid=(rows // tr,), out_shape=out_shape, in_specs=in_specs, out_specs=out_specs,
        compiler_params=_params("arbitrary"),
    )(*args)


def _rstd(x):
    return lax.rsqrt(jnp.mean(x * x, axis=-1, keepdims=True) + NORM_EPS)


def _norm_bwd(u, n, r):
    return r * (u - n * jnp.mean(u * n, axis=-1, keepdims=True))


def _colsum(v):
    return jnp.sum(v, axis=0, keepdims=True)


def _norm_fwd(x, g, name):
    def body(x_ref, g_ref, h_ref):
        xv = x_ref[...]
        h_ref[...] = ((xv * _rstd(xv)) * g_ref[...]).astype(BF16)

    return _rowwise(body, name, x.shape[0], 512, [x], [g], [(D, BF16)])[0]


def _resid_norm_fwd(x, mix, g_post, g_pre):
    def body(x_ref, mix_ref, g2_ref, g3_ref, x1_ref, h2_ref):
        mv = mix_ref[...]
        x1 = x_ref[...] + (mv * _rstd(mv)) * g2_ref[...]
        x1_ref[...] = x1
        h2_ref[...] = ((x1 * _rstd(x1)) * g3_ref[...]).astype(BF16)

    return _rowwise(body, "resid_norm_fwd", x.shape[0], 512, [x, mix], [g_post, g_pre], [(D, F32), (D, BF16)])


def _gate_merge(gpre, ys, b_gate):
    def body(gp_ref, ya_ref, yb_ref, yc_ref, b_ref, m_ref):
        acc = None
        for k, y_ref in enumerate((ya_ref, yb_ref, yc_ref)):
            cols = slice(k * D, (k + 1) * D)
            gate = jax.nn.sigmoid(gp_ref[:, cols].astype(F32) + b_ref[:, cols])
            term = gate * y_ref[...].astype(F32)
            acc = term if acc is None else acc + term
        m_ref[...] = acc.astype(BF16)

    return _rowwise(body, "gate_merge", gpre.shape[0], 256, [gpre, *ys], [b_gate], [(D, BF16)])[0]


def _swiglu_fwd(gu):
    def body(a_ref, b_ref, f_ref):
        a = a_ref[...].astype(F32)
        f_ref[...] = (a * jax.nn.sigmoid(a) * b_ref[...].astype(F32)).astype(BF16)

    return _rowwise(body, "swiglu_fwd", gu.shape[0], 256, [(gu, D_FF, 0), (gu, D_FF, 1)], [], [(D_FF, BF16)])[0]


def _loss_head(x1, fo, tgt, g_post):
    def body(x1_ref, fo_ref, t_ref, g_ref, dy_ref, dfo_ref, dg_ref, loss_ref):
        fo_v = fo_ref[...]
        r = _rstd(fo_v)
        n = fo_v * r
        err = (x1_ref[...] + n * g_ref[...]) - t_ref[...]
        loss_ref[...] += _colsum(err * err)
        dy = err * (1.0 / D)
        dy_ref[...] = dy
        dg_ref[...] += _colsum(dy * n)
        dfo_ref[...] = _norm_bwd(dy * g_ref[...], n, r).astype(BF16)

    return _rowwise(body, "loss_head", x1.shape[0], 256, [x1, fo, tgt], [g_post], [(D, F32), (D, BF16)], (D, D))


def _swiglu_bwd(df, gu):
    def body(df_ref, gu_ref, dgu_ref):
        a = gu_ref[:, :D_FF].astype(F32)
        b = gu_ref[:, D_FF:].astype(F32)
        d = df_ref[...].astype(F32)
        s = jax.nn.sigmoid(a)
        dgu_ref[:, :D_FF] = (d * b * (s * (1.0 + a * (1.0 - s)))).astype(BF16)
        dgu_ref[:, D_FF:] = (d * (a * s)).astype(BF16)

    return _rowwise(body, "swiglu_bwd", df.shape[0], 256, [df, gu], [], [(2 * D_FF, BF16)])[0]


def _mid_norm_bwd(dh2, x1, dy, mix, g_pre, g_post):
    def body(dh_ref, x1_ref, dy_ref, mix_ref, g3_ref, g2_ref, dx1_ref, dmix_ref, dg3_ref, dg2_ref):
        x1v = x1_ref[...]
        r3 = _rstd(x1v)
        n3 = x1v * r3
        dh = dh_ref[...]
        dg3_ref[...] += _colsum(dh * n3)
        dx1 = dy_ref[...] + _norm_bwd(dh * g3_ref[...], n3, r3)
        dx1_ref[...] = dx1
        mv = mix_ref[...]
        r2 = _rstd(mv)
        n2 = mv * r2
        dg2_ref[...] += _colsum(dx1 * n2)
        dmix_ref[...] = _norm_bwd(dx1 * g2_ref[...], n2, r2).astype(BF16)

    return _rowwise(body, "mid_norm_bwd", x1.shape[0], 256, [dh2, x1, dy, mix], [g_pre, g_post],
                    [(D, F32), (D, BF16)], (D, D))


def _gate_bwd(dmerged, gpre, ys, b_gate):
    def body(dm_ref, gp_ref, ya_ref, yb_ref, yc_ref, b_ref, dya_ref, dyb_ref, dyc_ref, dgp_ref, db_ref):
        dm = dm_ref[...].astype(F32)
        for k, (y_ref, dy_ref) in enumerate(((ya_ref, dya_ref), (yb_ref, dyb_ref), (yc_ref, dyc_ref))):
            cols = slice(k * D, (k + 1) * D)
            gate = jax.nn.sigmoid(gp_ref[:, cols].astype(F32) + b_ref[:, cols])
            dy_ref[...] = (dm * gate).astype(BF16)
            dgp = (dm * y_ref[...].astype(F32)) * (gate * (1.0 - gate))
            dgp_ref[:, cols] = dgp.astype(BF16)
            db_ref[:, cols] += _colsum(dgp)

    return _rowwise(body, "gate_bwd", gpre.shape[0], 256, [dmerged, gpre, *ys], [b_gate],
                    [(D, BF16), (D, BF16), (D, BF16), (3 * D, BF16)], (3 * D,))


def _in_norm_bwd(dhs, x, dx1, g_pre):
    def body(da_ref, db_ref, dc_ref, x_ref, dx1_ref, g_ref, dx_ref, dg_ref):
        dh = (da_ref[...] + db_ref[...]) + dc_ref[...]
        xv = x_ref[...]
        r = _rstd(xv)
        n = xv * r
        dg_ref[...] += _colsum(dh * n)
        dx_ref[...] = dx1_ref[...] + _norm_bwd(dh * g_ref[...], n, r)

    return _rowwise(body, "in_norm_bwd", x.shape[0], 256, [*dhs, x, dx1], [g_pre], [(D, F32)], (D,))


def _gain_grad(dmn, mem):
    def body(d_ref, m_ref, dg_ref):
        mv = m_ref[...]
        dg_ref[...] += _colsum(d_ref[...] * (mv * _rstd(mv)))

    return _rowwise(body, "mem_gain_grad", mem.shape[0], 256, [dmn, mem], [], [], (D,))[0]


def _iotas():
    lane = lax.broadcasted_iota(jnp.int32, (BLK, BLK), 1)
    row = lax.broadcasted_iota(jnp.int32, (BLK, BLK), 0)
    return lane, row, lane < 64


def _split_heads(v, head0):
    zero = jnp.zeros_like(v)
    return jnp.where(head0, v, zero), jnp.where(head0, zero, v)


def _head_rowsum(v, head0):
    return (jnp.sum(jnp.where(head0, v, 0.0), axis=1, keepdims=True),
            jnp.sum(jnp.where(head0, 0.0, v), axis=1, keepdims=True))


KT = 256


def _make_suffix():
    tri = (lax.broadcasted_iota(jnp.int32, (KT, KT), 0) > lax.broadcasted_iota(jnp.int32, (KT, KT), 1)).astype(BF16)
    tri2 = jnp.concatenate([tri, tri], axis=0)

    def suffix(x):
        hi = x.astype(BF16)
        lo = (x - hi.astype(F32)).astype(BF16)
        return jnp.dot(jnp.concatenate([hi, lo], axis=1), tri2, preferred_element_type=F32)

    return suffix


def _sb_scores(qh, k, mask, suffix, run):
    z = lax.dot_general(qh, k, NT, preferred_element_type=F32) * 0.125
    lb = jnp.minimum(z, 0.0) - jnp.log1p(jnp.exp(-jnp.abs(z)))
    lk = lb - z
    if mask is not None:
        lk = jnp.where(mask, lk, 0.0)
    a = jnp.exp(lb + suffix(lk) + run)
    if mask is not None:
        a = jnp.where(mask, a, 0.0)
    return lb, lk, a


QB = KT


def _sb_tiles(i, tile, init):
    st = tile(i, init, True)
    return lax.fori_loop(0, i, lambda t, s: tile(i - 1 - t, s, False), st)


def _sb_consts():
    head0 = lax.broadcasted_iota(jnp.int32, (QB, BLK), 1) < 64
    row = lax.broadcasted_iota(jnp.int32, (2 * QB, KT), 0) & (QB - 1)
    return head0, row > lax.broadcasted_iota(jnp.int32, (2 * QB, KT), 1)


def _stack_heads(v, head0):
    zero = jnp.zeros_like(v)
    return jnp.concatenate([jnp.where(head0, v, zero), jnp.where(head0, zero, v)], axis=0)


def _unstack_heads(v, head0):
    return jnp.where(head0, v[:QB], v[QB:])


def _sb_fwd(proj_a, seq):
    bsz = proj_a.shape[0]

    def body(x_ref, o_ref, o32_ref, acc_ref):
        head0, diag_mask = _sb_consts()
        suffix = _make_suffix()

        def qblock(i, carry):
            r0 = pl.multiple_of(i * QB, QB)
            qs = _stack_heads(x_ref[pl.ds(r0, QB), 0:128], head0)

            def tile(jt, run, masked):
                c0 = pl.multiple_of(jt * KT, KT)
                k = x_ref[pl.ds(c0, KT), 128:256]
                v = x_ref[pl.ds(c0, KT), 256:384]
                _, lk, a = _sb_scores(qs, k, diag_mask if masked else None, suffix, run)
                pv = jnp.dot(a.astype(BF16), v, preferred_element_type=F32)
                if masked:
                    acc_ref[...] = pv
                else:
                    acc_ref[...] += pv
                return run + jnp.sum(lk, axis=1, keepdims=True)

            _sb_tiles(i, tile, jnp.zeros((2 * QB, 1), F32))
            o = _unstack_heads(acc_ref[...], head0)
            o32_ref[pl.ds(r0, QB), :] = o
            o_ref[pl.ds(r0, QB), :] = o.astype(BF16)
            return carry

        lax.fori_loop(0, seq // QB, qblock, 0)

    out_spec = pl.BlockSpec((None, seq, BLK), lambda b, hp: (b, 0, hp))
    return pl.pallas_call(
        body, name="sb_attn_fwd", grid=(bsz, 4),
        out_shape=(jax.ShapeDtypeStruct((bsz, seq, 512), BF16), jax.ShapeDtypeStruct((bsz, seq, 512), F32)),
        in_specs=[pl.BlockSpec((None, seq, 384), lambda b, hp: (b, 0, hp))],
        out_specs=(out_spec, out_spec),
        scratch_shapes=[pltpu.VMEM((2 * QB, BLK), F32)],
        compiler_params=_params("parallel", "parallel"),
    )(proj_a)


def _sb_bwd(proj_a, d_o, o_a, seq):
    bsz = proj_a.shape[0]

    def body(x_ref, do_ref, o_ref, d_ref, dq_acc, dk_acc, dv_acc):
        head0, diag_mask = _sb_consts()
        suffix = _make_suffix()
        dk_acc[...] = jnp.zeros_like(dk_acc)
        dv_acc[...] = jnp.zeros_like(dv_acc)

        def qblock(i, carry):
            r0 = pl.multiple_of(i * QB, QB)
            qs = _stack_heads(x_ref[pl.ds(r0, QB), 0:128], head0)
            do = do_ref[pl.ds(r0, QB), :]
            dos = _stack_heads(do, head0)
            dsum = jnp.concatenate(_head_rowsum(do.astype(F32) * o_ref[pl.ds(r0, QB), :], head0), axis=0)

            def tile(jt, st, masked):
                run, grun = st
                c0 = pl.multiple_of(jt * KT, KT)
                k = x_ref[pl.ds(c0, KT), 128:256]
                v = x_ref[pl.ds(c0, KT), 256:384]
                lb, lk, a = _sb_scores(qs, k, diag_mask if masked else None, suffix, run)
                a16 = a.astype(BF16)
                g = a16.astype(F32) * lax.dot_general(dos, v, NT, preferred_element_type=F32)
                before = dsum - ((grun + suffix(g)) + g)
                dz = g - jnp.exp(lb) * (g + before)
                if masked:
                    dz = jnp.where(diag_mask, dz, 0.0)
                dz = (dz * 0.125).astype(BF16)
                dq = jnp.dot(dz, k, preferred_element_type=F32)
                if masked:
                    dq_acc[...] = dq
                else:
                    dq_acc[...] += dq
                dk_acc[pl.ds(c0, KT), :] += lax.dot_general(dz, qs, TN, preferred_element_type=F32)
                dv_acc[pl.ds(c0, KT), :] += lax.dot_general(a16, dos, TN, preferred_element_type=F32)
                return run + jnp.sum(lk, axis=1, keepdims=True), grun + jnp.sum(g, axis=1, keepdims=True)

            z1 = jnp.zeros((2 * QB, 1), F32)
            _sb_tiles(i, tile, (z1, z1))
            d_ref[pl.ds(r0, QB), 0:128] = _unstack_heads(dq_acc[...], head0).astype(BF16)
            return carry

        lax.fori_loop(0, seq // QB, qblock, 0)
        d_ref[:, 128:256] = dk_acc[...].astype(BF16)
        d_ref[:, 256:384] = dv_acc[...].astype(BF16)

    return pl.pallas_call(
        body, name="sb_attn_bwd", grid=(bsz, 4),
        out_shape=jax.ShapeDtypeStruct((bsz, seq, WA), BF16),
        in_specs=[pl.BlockSpec((None, seq, 384), lambda b, hp: (b, 0, hp)),
                  pl.BlockSpec((None, seq, BLK), lambda b, hp: (b, 0, hp)),
                  pl.BlockSpec((None, seq, BLK), lambda b, hp: (b, 0, hp))],
        out_specs=pl.BlockSpec((None, seq, 384), lambda b, hp: (b, 0, hp)),
        scratch_shapes=[pltpu.VMEM((2 * QB, BLK), F32), pltpu.VMEM((seq, BLK), F32), pltpu.VMEM((seq, BLK), F32)],
        compiler_params=_params("parallel", "parallel"),
    )(proj_a, d_o, o_a)


def _rope_tables(seq):
    inv_freq = ROPE_THETA ** (-jnp.arange(32, dtype=F32) * 2.0 / 64)
    ang = jnp.arange(seq).astype(F32)[:, None] * inv_freq[None, :]
    cos, sin = jnp.cos(ang), jnp.sin(ang)
    return jnp.tile(cos, (1, 4)), jnp.concatenate([-sin, sin, -sin, sin], axis=1)


def _make_rope(n_rows):
    lane = lax.broadcasted_iota(jnp.int32, (n_rows, BLK), 1)
    first = (lane & 63) < 32

    def rope(x, cos, sin):
        partner = jnp.where(first, pltpu.roll(x, 96, 1), pltpu.roll(x, 32, 1))
        return x * cos + partner * sin

    return rope


def _dil_masks(row, lane, i):
    return row >= lane, jnp.logical_and(lane >= row, i > 0)


def _dil_blocks(dil, seq, block):
    nq = seq // dil // BLK

    def rows(r, i):
        if dil == 1:
            return pl.ds(pl.multiple_of(i * BLK, BLK), BLK)
        return pl.ds(r + (dil * BLK) * i, BLK, stride=dil)

    def step(n, carry):
        r, i = lax.div(n, nq), lax.rem(n, nq)
        block(rows(r, i), rows(r, jnp.maximum(i - 1, 0)), i)
        return carry

    lax.fori_loop(0, seq // BLK, step, 0)


def _dil_load_qkv(x_ref, c, rope, cos, sin, qf, kf, vf):
    qf[...] = rope(x_ref[:, c:c + 128].astype(F32), cos, sin).astype(BF16).astype(F32)
    kf[...] = rope(x_ref[:, c + 128:c + 256].astype(F32), cos, sin).astype(BF16).astype(F32)
    vf[...] = x_ref[:, c + 256:c + 384].astype(F32)


def _dil_fwd(proj_b, cos_t, sin_t, seq):
    bsz = proj_b.shape[0]

    def body(x_ref, cos_ref, sin_ref, ob_ref, lse_ref, qf, kf, vf, og, lg):
        lane, row, head0 = _iotas()
        rope = _make_rope(seq)
        cos, sin = cos_ref[...], sin_ref[...]
        for g, dil in enumerate(DIL_GROUPS):
            _dil_load_qkv(x_ref, 384 * g, rope, cos, sin, qf, kf, vf)

            def block(cur, prev, i, g=g):
                qh = _split_heads(qf[cur, :].astype(BF16), head0)
                kc, vc = kf[cur, :].astype(BF16), vf[cur, :].astype(BF16)
                kp, vp = kf[prev, :].astype(BF16), vf[prev, :].astype(BF16)
                valid_c, valid_p = _dil_masks(row, lane, i)
                outs, lses = [], []
                for h in range(2):
                    sc = jnp.where(valid_c, lax.dot_general(qh[h], kc, NT, preferred_element_type=F32) * 0.125, NEG_INF)
                    sp = jnp.where(valid_p, lax.dot_general(qh[h], kp, NT, preferred_element_type=F32) * 0.125, NEG_INF)
                    m = jnp.maximum(jnp.max(sc, axis=1, keepdims=True), jnp.max(sp, axis=1, keepdims=True))
                    pc, pp = jnp.exp(sc - m), jnp.exp(sp - m)
                    den = jnp.sum(pc, axis=1, keepdims=True) + jnp.sum(pp, axis=1, keepdims=True)
                    num = (jnp.dot(pc.astype(BF16), vc, preferred_element_type=F32)
                           + jnp.dot(pp.astype(BF16), vp, preferred_element_type=F32))
                    outs.append(num / den)
                    lses.append(jnp.broadcast_to(m + jnp.log(den), (BLK, BLK)))
                og[g, cur, :] = jnp.where(head0, outs[0], outs[1])
                lg[g, cur, :] = jnp.where(head0, lses[0], lses[1])

            _dil_blocks(dil, seq, block)
        ls = [lg[0], lg[1], lg[2]]
        m = jnp.maximum(jnp.maximum(ls[0], ls[1]), ls[2])
        ws = [jnp.exp(l - m) for l in ls]
        den = (ws[0] + ws[1]) + ws[2]
        ob_ref[...] = (((ws[0] * og[0] + ws[1] * og[1]) + ws[2] * og[2]) / den).astype(BF16)
        lse_ref[...] = m + jnp.log(den)

    tab_spec = pl.BlockSpec((seq, BLK), lambda b, hp: (0, 0))
    out_spec = pl.BlockSpec((None, seq, BLK), lambda b, hp: (b, 0, hp))
    slab = pltpu.VMEM((seq, BLK), F32)
    return pl.pallas_call(
        body, name="dil_attn_fwd", grid=(bsz, 2),
        out_shape=(jax.ShapeDtypeStruct((bsz, seq, 256), BF16), jax.ShapeDtypeStruct((bsz, seq, 256), F32)),
        in_specs=[pl.BlockSpec((None, seq, WB // 2), lambda b, hp: (b, 0, hp)), tab_spec, tab_spec],
        out_specs=(out_spec, out_spec),
        scratch_shapes=[slab, slab, slab, pltpu.VMEM((3, seq, BLK), F32), pltpu.VMEM((3, seq, BLK), F32)],
        compiler_params=_params("parallel", "parallel"),
    )(proj_b, cos_t, sin_t)


def _dil_bwd(proj_b, cos_t, sin_t, d_ob, o_b, lse, seq):
    bsz = proj_b.shape[0]

    def body(x_ref, cos_ref, sin_ref, do_ref, ob_ref, lse_ref, d_ref, qf, kf, vf, dof, dsf, dq_s, dk_acc, dv_acc):
        lane, row, head0 = _iotas()
        rope = _make_rope(seq)
        cos, sin = cos_ref[...], sin_ref[...]
        do_all = do_ref[...].astype(F32)
        dof[...] = do_all
        head0_all = lax.broadcasted_iota(jnp.int32, (seq, BLK), 1) < 64
        d0, d1 = _head_rowsum(do_all * ob_ref[...].astype(F32), head0_all)
        dsf[...] = jnp.where(head0_all, d0, d1)
        for g, dil in enumerate(DIL_GROUPS):
            _dil_load_qkv(x_ref, 384 * g, rope, cos, sin, qf, kf, vf)
            dk_acc[...] = jnp.zeros_like(dk_acc)
            dv_acc[...] = jnp.zeros_like(dv_acc)

            def block(cur, prev, i):
                qh = _split_heads(qf[cur, :].astype(BF16), head0)
                doh = _split_heads(dof[cur, :].astype(BF16), head0)
                ds_v, lse_v = dsf[cur, :], lse_ref[cur, :]
                dsum, lse_h = (ds_v[:, 0:1], ds_v[:, 64:65]), (lse_v[:, 0:1], lse_v[:, 64:65])
                dq = jnp.zeros((BLK, BLK), F32)
                for sel, valid in zip((cur, prev), _dil_masks(row, lane, i)):
                    k, v = kf[sel, :].astype(BF16), vf[sel, :].astype(BF16)
                    dqs = []
                    for h in range(2):
                        s = lax.dot_general(qh[h], k, NT, preferred_element_type=F32) * 0.125
                        p = jnp.where(valid, jnp.exp(s - lse_h[h]), 0.0)
                        dp = lax.dot_general(doh[h], v, NT, preferred_element_type=F32)
                        ds = ((p * (dp - dsum[h])) * 0.125).astype(BF16)
                        dqs.append(jnp.dot(ds, k, preferred_element_type=F32))
                        dk_acc[sel, :] += lax.dot_general(ds, qh[h], TN, preferred_element_type=F32)
                        dv_acc[sel, :] += lax.dot_general(p.astype(BF16), doh[h], TN, preferred_element_type=F32)
                    dq = dq + jnp.where(head0, dqs[0], dqs[1])
                dq_s[cur, :] = dq

            _dil_blocks(dil, seq, block)
            c = 384 * g
            d_ref[:, c:c + 128] = rope(dq_s[...], cos, -sin).astype(BF16)
            d_ref[:, c + 128:c + 256] = rope(dk_acc[...], cos, -sin).astype(BF16)
            d_ref[:, c + 256:c + 384] = dv_acc[...].astype(BF16)

    x_spec = pl.BlockSpec((None, seq, WB // 2), lambda b, hp: (b, 0, hp))
    tab_spec = pl.BlockSpec((seq, BLK), lambda b, hp: (0, 0))
    tok_spec = pl.BlockSpec((None, seq, BLK), lambda b, hp: (b, 0, hp))
    return pl.pallas_call(
        body, name="dil_attn_bwd", grid=(bsz, 2),
        out_shape=jax.ShapeDtypeStruct((bsz, seq, WB), BF16),
        in_specs=[x_spec, tab_spec, tab_spec, tok_spec, tok_spec, tok_spec], out_specs=x_spec,
        scratch_shapes=[pltpu.VMEM((seq, BLK), F32)] * 8,
        compiler_params=_params("parallel", "parallel"),
    )(proj_b, cos_t, sin_t, d_ob, o_b, lse)


MEM_SCALE = 128 ** -0.5
MEM_QB = 256


def _mem_fwd(proj_a, kv, seq):
    bsz = proj_a.shape[0]

    def body(q_ref, k_ref, v_ref, o_ref):
        k, v = k_ref[...], v_ref[...]

        def qblock(i, carry):
            r0 = pl.multiple_of(i * MEM_QB, MEM_QB)
            s = lax.dot_general(q_ref[pl.ds(r0, MEM_QB), :], k, NT, preferred_element_type=F32) * MEM_SCALE
            p = jnp.exp(s - jnp.max(s, axis=1, keepdims=True))
            p = p / jnp.sum(p, axis=1, keepdims=True)
            o_ref[pl.ds(r0, MEM_QB), :] = jnp.dot(p.astype(BF16), v, preferred_element_type=F32).astype(BF16)
            return carry

        lax.fori_loop(0, seq // MEM_QB, qblock, 0)

    return pl.pallas_call(
        body, name="mem_attn_fwd", grid=(bsz, 4),
        out_shape=jax.ShapeDtypeStruct((bsz, seq, 512), BF16),
        in_specs=[pl.BlockSpec((None, seq, BLK), lambda b, h: (b, 0, 12 + h)),
                  pl.BlockSpec((None, MEM_LEN, BLK), lambda b, h: (b, 0, h)),
                  pl.BlockSpec((None, MEM_LEN, BLK), lambda b, h: (b, 0, 4 + h))],
        out_specs=pl.BlockSpec((None, seq, BLK), lambda b, h: (b, 0, h)),
        compiler_params=_params("parallel", "parallel"),
    )(proj_a, kv, kv)


def _mem_bwd(proj_a, kv, d_o, d_proj_a, seq):
    bsz = proj_a.shape[0]

    def body(q_ref, k_ref, v_ref, do_ref, _, dq_ref, dk_ref, dv_ref):
        k, v = k_ref[...], v_ref[...]

        def qblock(i, carry):
            dk, dv = carry
            r0 = pl.multiple_of(i * MEM_QB, MEM_QB)
            q, do = q_ref[pl.ds(r0, MEM_QB), :], do_ref[pl.ds(r0, MEM_QB), :]
            s = lax.dot_general(q, k, NT, preferred_element_type=F32) * MEM_SCALE
            p = jnp.exp(s - jnp.max(s, axis=1, keepdims=True))
            p = p / jnp.sum(p, axis=1, keepdims=True)
            dp = lax.dot_general(do, v, NT, preferred_element_type=F32)
            ds = ((p * (dp - jnp.sum(p * dp, axis=1, keepdims=True))) * MEM_SCALE).astype(BF16)
            dq_ref[pl.ds(r0, MEM_QB), :] = jnp.dot(ds, k, preferred_element_type=F32).astype(BF16)
            dk = dk + lax.dot_general(ds, q, TN, preferred_element_type=F32)
            dv = dv + lax.dot_general(p.astype(BF16), do, TN, preferred_element_type=F32)
            return dk, dv

        zero = jnp.zeros((MEM_LEN, BLK), F32)
        dk, dv = lax.fori_loop(0, seq // MEM_QB, qblock, (zero, zero))
        dk_ref[...] = dk.astype(BF16)
        dv_ref[...] = dv.astype(BF16)

    kv_spec = pl.BlockSpec((None, MEM_LEN, BLK), lambda b, h: (b, 0, h))
    return pl.pallas_call(
        body, name="mem_attn_bwd", grid=(bsz, 4),
        out_shape=(jax.ShapeDtypeStruct((bsz, seq, WA), BF16), jax.ShapeDtypeStruct((bsz, MEM_LEN, 512), BF16),
                   jax.ShapeDtypeStruct((bsz, MEM_LEN, 512), BF16)),
        in_specs=[pl.BlockSpec((None, seq, BLK), lambda b, h: (b, 0, 12 + h)), kv_spec,
                  pl.BlockSpec((None, MEM_LEN, BLK), lambda b, h: (b, 0, 4 + h)),
                  pl.BlockSpec((None, seq, BLK), lambda b, h: (b, 0, h)), ANY],
        out_specs=(pl.BlockSpec((None, seq, BLK), lambda b, h: (b, 0, 12 + h)), kv_spec, kv_spec),
        input_output_aliases={4: 0},
        compiler_params=_params("parallel", "parallel"),
    )(proj_a, kv, kv, d_o, d_proj_a)


def _mesh_pos():
    return lax.axis_index("x"), lax.axis_index("y"), lax.axis_index("c")


def _all_gather(shard, name):
    m_per, n = shard.shape

    def body(x_ref, out_ref, send_sems, recv_sems, local_sem):
        x, y, c = _mesh_pos()
        me, sibling = (x, y, c), (x, y, 1 - c)
        chips = [(1 - x, y), (x, 1 - y), (1 - x, 1 - y)]

        def rows(px, py, pc):
            return out_ref.at[pl.ds((4 * px + 2 * py + pc) * m_per, m_per), :]

        def copy(k, block, to, src=None):
            return pltpu.make_async_remote_copy(
                src_ref=rows(*block) if src is None else src, dst_ref=rows(*block),
                send_sem=send_sems.at[k], recv_sem=recv_sems.at[k], device_id=to, device_id_type=MESH)

        mine = pltpu.make_async_copy(x_ref, rows(*me), local_sem)
        mine.start()
        first = [copy(0, me, sibling, src=x_ref)]
        first += [copy(1 + j, me, (*chip, c), src=x_ref) for j, chip in enumerate(chips)]
        for cp in first:
            cp.start()
        passed = [copy(4 + j, (*chip, c), sibling) for j, chip in enumerate(chips)]
        for j, chip in enumerate(chips):
            copy(1 + j, (*chip, c), me).wait_recv()
            passed[j].start()
        copy(0, sibling, me).wait_recv()
        for j, chip in enumerate(chips):
            copy(4 + j, (*chip, 1 - c), me).wait_recv()
        for cp in first + passed:
            cp.wait_send()
        mine.wait()

    return pl.pallas_call(
        body, name=name, out_shape=jax.ShapeDtypeStruct((N_DEV * m_per, n), shard.dtype),
        in_specs=[ANY], out_specs=ANY,
        scratch_shapes=[pltpu.SemaphoreType.DMA((7,)), pltpu.SemaphoreType.DMA((7,)), pltpu.SemaphoreType.DMA(())],
    )(shard)


def _exchange(parts):
    n_rows, n = parts.shape[1:]

    def body(g_ref, out_ref, send_sems, recv_sems, local_sem):
        x, y, c = _mesh_pos()
        me = 4 * x + 2 * y + c
        own = pltpu.make_async_copy(g_ref.at[me], out_ref.at[me], local_sem)
        own.start()
        sends, recvs = [], []
        for k in range(1, N_DEV):
            px = 1 - x if k & 4 else x
            py = 1 - y if k & 2 else y
            pc = 1 - c if k & 1 else c
            peer = 4 * px + 2 * py + pc
            sems = dict(send_sem=send_sems.at[k - 1], recv_sem=recv_sems.at[k - 1],
                        device_id=(px, py, pc), device_id_type=MESH)
            sends.append(pltpu.make_async_remote_copy(src_ref=g_ref.at[peer], dst_ref=out_ref.at[me], **sems))
            recvs.append(pltpu.make_async_remote_copy(src_ref=g_ref.at[me], dst_ref=out_ref.at[peer], **sems))
        for cp in sends:
            cp.start()
        for cp in recvs:
            cp.wait_recv()
        for cp in sends:
            cp.wait_send()
        own.wait()

    return pl.pallas_call(
        body, name="grad_exchange", out_shape=jax.ShapeDtypeStruct((N_DEV, n_rows, n), parts.dtype),
        in_specs=[ANY], out_specs=ANY,
        scratch_shapes=[pltpu.SemaphoreType.DMA((7,)), pltpu.SemaphoreType.DMA((7,)), pltpu.SemaphoreType.DMA(())],
    )(parts)


def _adamw(w, g, m, v):
    m = ADAM_B1 * m + (1.0 - ADAM_B1) * g
    v = ADAM_B2 * v + (1.0 - ADAM_B2) * (g * g)
    m_hat = m / (1.0 - ADAM_B1 ** ADAM_STEP)
    v_hat = v / (1.0 - ADAM_B2 ** ADAM_STEP)
    return -ADAM_LR * (m_hat / (jnp.sqrt(v_hat) + ADAM_EPS) + ADAM_WD * w), m, v


def _reduce_adamw(recv, w, m, v):
    rows = w.shape[0]
    tr = 240

    def body(r_ref, w_ref, m_ref, v_ref, g_out, d_out, m_out, v_out):
        g = r_ref[0].astype(F32)
        for s in range(1, N_DEV):
            g = g + r_ref[s].astype(F32)
        g_out[...] = g
        d_out[...], m_out[...], v_out[...] = _adamw(w_ref[...], g, m_ref[...], v_ref[...])

    spec = pl.BlockSpec((tr, D), lambda i: (i, 0))
    return pl.pallas_call(
        body, name="reduce_adamw", grid=(rows // tr,),
        out_shape=[jax.ShapeDtypeStruct((rows, D), F32)] * 4,
        in_specs=[pl.BlockSpec((N_DEV, tr, D), lambda i: (0, i, 0)), spec, spec, spec],
        out_specs=[spec] * 4, compiler_params=_params("arbitrary"),
    )(recv, w, m, v)


def _small_adamw(gathered, w, m, v):
    def body(g_ref, w_ref, m_ref, v_ref, g_out, d_out, m_out, v_out, loss_out):
        tot = g_ref[0]
        for s in range(1, N_DEV):
            tot = tot + g_ref[s]
        g = tot[0:8]
        g_out[...] = g
        d_out[...], m_out[...], v_out[...] = _adamw(w_ref[...], g, m_ref[...], v_ref[...])
        loss_out[...] = jnp.broadcast_to((0.5 / D) * jnp.sum(tot[8:9], axis=1, keepdims=True), (8, BLK))

    out = [jax.ShapeDtypeStruct((8, D), F32)] * 4 + [jax.ShapeDtypeStruct((8, BLK), F32)]
    return pl.pallas_call(body, name="small_adamw", out_shape=out, compiler_params=_params())(gathered, w, m, v)


def _pack(ws, dtype):
    return jnp.concatenate([w.astype(dtype).reshape(-1, D) for w in ws], axis=0)


def _pick_chunks(w, chunks):
    return jnp.concatenate([w[:, BLK * c:BLK * (c + 1)] for c in chunks], axis=1)


def kernel(x, mem, g_pre_mix, g_post_mix, g_pre_ffn, g_post_ffn, g_mem, w_in, w_mem_kv, w_br_sb, w_br_dil, w_br_mem, w_gate, b_gate, w_o, w_ffn_in, w_ffn_out, loss_target, m_g_pre_mix, m_g_post_mix, m_g_pre_ffn, m_g_post_ffn, m_g_mem, m_w_in, m_w_mem_kv, m_w_br_sb, m_w_br_dil, m_w_br_mem, m_w_gate, m_b_gate, m_w_o, m_w_ffn_in, m_w_ffn_out, v_g_pre_mix, v_g_post_mix, v_g_pre_ffn, v_g_post_ffn, v_g_mem, v_w_in, v_w_mem_kv, v_w_br_sb, v_w_br_dil, v_w_br_mem, v_w_gate, v_b_gate, v_w_o, v_w_ffn_in, v_w_ffn_out):
    bsz, seq, _ = x.shape
    tokens = bsz * seq
    xf, tgt, memf = x.reshape(tokens, D), loss_target.reshape(tokens, D), mem.reshape(bsz * MEM_LEN, D)
    big_w = [w_in, w_mem_kv, w_br_sb, w_br_dil, w_br_mem, w_gate, w_o, w_ffn_in, w_ffn_out]
    big_m = [m_w_in, m_w_mem_kv, m_w_br_sb, m_w_br_dil, m_w_br_mem, m_w_gate, m_w_o, m_w_ffn_in, m_w_ffn_out]
    big_v = [v_w_in, v_w_mem_kv, v_w_br_sb, v_w_br_dil, v_w_br_mem, v_w_gate, v_w_o, v_w_ffn_in, v_w_ffn_out]
    shard_shapes = [w.shape for w in big_w]

    gathered = _all_gather(_pack([w[0] for w in big_w], BF16), "weight_all_gather").reshape(N_DEV, PACK_TOTAL, D)

    def full(i, by_rows):
        part = gathered[:, PACK_OFFS[i]:PACK_OFFS[i] + PACK_ROWS[i]]
        k, n = shard_shapes[i][1:]
        if by_rows:
            return part.reshape(N_DEV * k, n)
        return part.reshape(N_DEV, k, n).transpose(1, 0, 2).reshape(k, N_DEV * n)

    fw_in, fw_mem_kv, fw_br_sb, fw_br_dil, fw_br_mem = full(0, False), full(1, True), full(2, False), full(3, False), full(4, False)
    fw_gate, fw_o, fw_ffn_in, fw_ffn_out = full(5, False), full(6, True), full(7, False), full(8, True)
    w_a, w_b = _pick_chunks(fw_in, CHUNKS_A), _pick_chunks(fw_in, CHUNKS_B)

    h = _norm_fwd(xf, g_pre_mix, "pre_mix_norm")
    proj_a = _matmul(h, w_a, "nn", BF16, "proj_a").reshape(bsz, seq, WA)
    proj_b = _matmul(h, w_b, "nn", BF16, "proj_b").reshape(bsz, seq, WB)
    gpre = _matmul(h, fw_gate, "nn", BF16, "gate_proj")
    o_a, o_a32 = _sb_fwd(proj_a, seq)
    cos_t, sin_t = _rope_tables(seq)
    o_b, lse_b = _dil_fwd(proj_b, cos_t, sin_t, seq)
    mn = _norm_fwd(memf, g_mem, "mem_norm")
    kv = _matmul(mn, fw_mem_kv, "nn", BF16, "mem_kv_proj").reshape(bsz, MEM_LEN, D)
    o_c = _mem_fwd(proj_a, kv, seq)
    o_a2, o_b2, o_c2 = o_a.reshape(tokens, 512), o_b.reshape(tokens, 256), o_c.reshape(tokens, 512)
    ys = [_matmul(o_a2, fw_br_sb, "nn", BF16, "branch_sb"), _matmul(o_b2, fw_br_dil, "nn", BF16, "branch_dil"),
          _matmul(o_c2, fw_br_mem, "nn", BF16, "branch_mem")]
    merged = _gate_merge(gpre, ys, b_gate)
    mix = _matmul(merged, fw_o, "nn", F32, "out_proj")
    x1, h2 = _resid_norm_fwd(xf, mix, g_post_mix, g_pre_ffn)
    gu = _matmul(h2, fw_ffn_in, "nn", BF16, "ffn_in")
    f = _swiglu_fwd(gu)
    fo = _matmul(f, fw_ffn_out, "nn", F32, "ffn_out")
    dy, dfo, dg_post_ffn, loss_lanes = _loss_head(x1, fo, tgt, g_post_ffn)

    df = _matmul(dfo, fw_ffn_out, "nt", BF16, "d_ffn_act")
    gw_ffn_out = _matmul(f, dfo, "tn", F32, "gw_ffn_out")
    dgu = _swiglu_bwd(df, gu)
    dh2 = _matmul(dgu, fw_ffn_in, "nt", F32, "d_h2")
    gw_ffn_in = _matmul(h2, dgu, "tn", F32, "gw_ffn_in")
    dx1, dmix, dg_pre_ffn, dg_post_mix = _mid_norm_bwd(dh2, x1, dy, mix, g_pre_ffn, g_post_mix)
    dmerged = _matmul(dmix, fw_o, "nt", BF16, "d_merged")
    gw_o = _matmul(merged, dmix, "tn", F32, "gw_o")
    dya, dyb, dyc, dgpre, db_gate = _gate_bwd(dmerged, gpre, ys, b_gate)
    d_oa = _matmul(dya, fw_br_sb, "nt", BF16, "d_o_sb").reshape(bsz, seq, 512)
    d_ob = _matmul(dyb, fw_br_dil, "nt", BF16, "d_o_dil").reshape(bsz, seq, 256)
    d_oc = _matmul(dyc, fw_br_mem, "nt", BF16, "d_o_mem").reshape(bsz, seq, 512)
    gw_br_sb = _matmul(o_a2, dya, "tn", F32, "gw_br_sb")
    gw_br_dil = _matmul(o_b2, dyb, "tn", F32, "gw_br_dil")
    gw_br_mem = _matmul(o_c2, dyc, "tn", F32, "gw_br_mem")
    d_proj_a = _sb_bwd(proj_a, d_oa, o_a32, seq)
    d_proj_a, dk_m, dv_m = _mem_bwd(proj_a, kv, d_oc, d_proj_a, seq)
    d_proj_b = _dil_bwd(proj_b, cos_t, sin_t, d_ob, o_b, lse_b, seq).reshape(tokens, WB)
    d_proj_a = d_proj_a.reshape(tokens, WA)
    dhs = [_matmul(d_proj_a, w_a, "nt", F32, "d_h_a"), _matmul(d_proj_b, w_b, "nt", F32, "d_h_b"),
           _matmul(dgpre, fw_gate, "nt", F32, "d_h_gate")]
    gw_a = _matmul(h, d_proj_a, "tn", F32, "gw_in_a")
    gw_b = _matmul(h, d_proj_b, "tn", F32, "gw_in_b")
    gw_gate = _matmul(h, dgpre, "tn", F32, "gw_gate")
    dx, dg_pre_mix = _in_norm_bwd(dhs, xf, dx1, g_pre_mix)
    dkv = jnp.concatenate([dk_m, dv_m], axis=-1).reshape(bsz * MEM_LEN, D)
    gw_mem_kv = _matmul(mn, dkv, "tn", F32, "gw_mem_kv")
    dmn = _matmul(dkv, fw_mem_kv, "nt", F32, "d_mem_norm")
    dg_mem = _gain_grad(dmn, memf)

    gw_ab = jnp.concatenate([gw_a, gw_b], axis=1)
    where = {c: i for i, c in enumerate(CHUNKS_A + CHUNKS_B)}
    gw_in = _pick_chunks(gw_ab, [where[c] for c in range(34)])
    full_grads = [gw_in, gw_mem_kv, gw_br_sb, gw_br_dil, gw_br_mem, gw_gate, gw_o, gw_ffn_in, gw_ffn_out]
    by_rows = [False, True, False, False, False, False, True, False, True]
    parts = []
    for gw, shp, rows_sharded in zip(full_grads, shard_shapes, by_rows):
        k, n = shp[1:]
        if rows_sharded:
            parts.append(gw.astype(BF16).reshape(N_DEV, (k * n) // D, D))
        else:
            parts.append(gw.astype(BF16).reshape(k, N_DEV, n).transpose(1, 0, 2).reshape(N_DEV, (k * n) // D, D))
    recv = _exchange(jnp.concatenate(parts, axis=1))
    g_p, d_p, m_p, v_p = _reduce_adamw(recv, _pack([w[0] for w in big_w], F32), _pack([m[0] for m in big_m], F32),
                                       _pack([v[0] for v in big_v], F32))

    def unpack(p):
        return [p[PACK_OFFS[i]:PACK_OFFS[i] + PACK_ROWS[i]].reshape(shard_shapes[i]) for i in range(len(big_w))]

    big = [unpack(p) for p in (g_p, d_p, m_p, v_p)]

    small = jnp.concatenate([dg_pre_mix, dg_post_mix, dg_pre_ffn, dg_post_ffn, dg_mem, db_gate.reshape(3, D),
                             loss_lanes, jnp.zeros((7, D), F32)], axis=0)
    small_all = _all_gather(small, "small_all_gather").reshape(N_DEV, 16, D)

    def small_pack(gs, b):
        return jnp.concatenate([*gs, b.reshape(3, D)], axis=0)

    sm = _small_adamw(
        small_all, small_pack([g_pre_mix, g_post_mix, g_pre_ffn, g_post_ffn, g_mem], b_gate),
        small_pack([m_g_pre_mix, m_g_post_mix, m_g_pre_ffn, m_g_post_ffn, m_g_mem], m_b_gate),
        small_pack([v_g_pre_mix, v_g_post_mix, v_g_pre_ffn, v_g_post_ffn, v_g_mem], v_b_gate))
    loss = sm[4][0, 0]

    def leaves(k):
        t, bw = sm[k], big[k]
        return [t[0:1], t[1:2], t[2:3], t[3:4], t[4:5], *bw[0:6], t[5:8].reshape(1, 3 * D), *bw[6:9]]

    return (loss, dx.reshape(bsz, seq, D), *leaves(0), *leaves(1), *leaves(2), *leaves(3))
```

```python
import functools

import jax
import jax.numpy as jnp
from jax import lax
from jax.experimental import pallas as pl
from jax.experimental.pallas import tpu as pltpu

F32 = jnp.float32
BF16 = jnp.bfloat16
D = 1024
BLK = 128
MEM_LEN = 256
D_FF = 2816
NORM_EPS = 1e-6
NEG_INF = -1e30
ROPE_THETA = 10000.0
ADAM_LR, ADAM_B1, ADAM_B2, ADAM_EPS, ADAM_WD, ADAM_STEP = 0.001, 0.9, 0.999, 1e-08, 0.01, 10
N_DEV = 8
VMEM_LIMIT_BYTES = 56 * 1024 * 1024
MESH = pl.DeviceIdType.MESH
ANY = pl.BlockSpec(memory_space=pl.ANY)

NT = (((1,), (1,)), ((), ()))
TN = (((0,), (0,)), ((), ()))
NN = (((1,), (0,)), ((), ()))
_DIMS = {"nn": NN, "nt": NT, "tn": TN}

PACK_NAMES = ("w_in", "w_mem_kv", "w_br_sb", "w_br_dil", "w_br_mem", "w_gate", "w_o", "w_ffn_in", "w_ffn_out")
PACK_ROWS = (544, 128, 64, 32, 64, 384, 128, 704, 352)
PACK_OFFS = tuple(sum(PACK_ROWS[:i]) for i in range(len(PACK_ROWS)))
PACK_TOTAL = sum(PACK_ROWS)

CHUNKS_A = tuple(c for hp in range(4) for c in (hp, 4 + hp, 8 + hp)) + (30, 31, 32, 33)
CHUNKS_B = tuple(c for hp in range(2) for g in range(3) for c in (12 + 6 * g + hp, 14 + 6 * g + hp, 16 + 6 * g + hp))
WA, WB = 128 * len(CHUNKS_A), 128 * len(CHUNKS_B)
DIL_GROUPS = (1, 4, 16)


def _params(*sem):
    return pltpu.CompilerParams(dimension_semantics=sem or None, vmem_limit_bytes=VMEM_LIMIT_BYTES)


def _tile(n, cap):
    if n <= 128:
        return n
    assert n % 128 == 0, n
    best = 128
    for t in range(128, min(n, cap) + 1, 128):
        if n % t == 0:
            best = t
    return best


def _matmul(a, b, mode, out_dtype, name, tm_cap=1536, tn_cap=1536, tk_cap=1536):
    if mode == "tn":
        (K, M), N = a.shape, b.shape[1]
    elif mode == "nt":
        (M, K), N = a.shape, b.shape[0]
    else:
        (M, K), N = a.shape, b.shape[1]
    tm, tn, tk = _tile(M, tm_cap), _tile(N, tn_cap), _tile(K, tk_cap)
    nm, nn, nk = M // tm, N // tn, K // tk
    dims = _DIMS[mode]

    def body(a_ref, b_ref, o_ref, *acc):
        p = lax.dot_general(a_ref[...], b_ref[...], dims, preferred_element_type=F32)
        if nk == 1:
            o_ref[...] = p.astype(o_ref.dtype)
        else:
            acc_ref, = acc
            k = pl.program_id(2)

            @pl.when(k == 0)
            def _():
                acc_ref[...] = p

            @pl.when(k > 0)
            def _():
                acc_ref[...] += p

            @pl.when(k == nk - 1)
            def _():
                o_ref[...] = acc_ref[...].astype(o_ref.dtype)

    n_outer = nk == 1 and (a.size * nn + b.size) < (a.size + b.size * nm)
    if n_outer:
        grid, ij = (nn, nm, nk), (lambda g0, g1: (g1, g0))
    else:
        grid, ij = (nm, nn, nk), (lambda g0, g1: (g0, g1))
    if mode == "tn":
        a_spec = pl.BlockSpec((tk, tm), lambda g0, g1, k: (k, ij(g0, g1)[0]))
    else:
        a_spec = pl.BlockSpec((tm, tk), lambda g0, g1, k: (ij(g0, g1)[0], k))
    if mode == "nt":
        b_spec = pl.BlockSpec((tn, tk), lambda g0, g1, k: (ij(g0, g1)[1], k))
    else:
        b_spec = pl.BlockSpec((tk, tn), lambda g0, g1, k: (k, ij(g0, g1)[1]))
    return pl.pallas_call(
        body, name=name, grid=grid,
        out_shape=jax.ShapeDtypeStruct((M, N), out_dtype),
        in_specs=[a_spec, b_spec],
        out_specs=pl.BlockSpec((tm, tn), lambda g0, g1, k: ij(g0, g1)),
        scratch_shapes=[pltpu.VMEM((tm, tn), F32)] if nk > 1 else [],
        compiler_params=_params("parallel", "parallel", "arbitrary"),
    )(a, b)


def _rowwise(body, name, rows, tr, row_ins, vec_ins, row_outs, acc_outs=()):
    tr = min(tr, rows)
    assert rows % tr == 0
    in_specs, args = [], []
    for r in row_ins:
        arr, w, cb = r if isinstance(r, tuple) else (r, r.shape[1], 0)
        in_specs.append(pl.BlockSpec((tr, w), functools.partial(lambda i, cb: (i, cb), cb=cb)))
        args.append(arr)
    for v in vec_ins:
        in_specs.append(pl.BlockSpec(v.shape, lambda i: (0, 0)))
        args.append(v)
    out_shape = [jax.ShapeDtypeStruct((rows, w), dt) for w, dt in row_outs]
    out_shape += [jax.ShapeDtypeStruct((1, w), F32) for w in acc_outs]
    out_specs = [pl.BlockSpec((tr, w), lambda i: (i, 0)) for w, _ in row_outs]
    out_specs += [pl.BlockSpec((1, w), lambda i: (0, 0)) for w in acc_outs]
    n_acc = len(acc_outs)

    def wrapped(*refs):
        if n_acc:
            @pl.when(pl.program_id(0) == 0)
            def _():
                for r in refs[len(refs) - n_acc:]:
                    r[...] = jnp.zeros_like(r)
        body(*refs)

    return pl.pallas_call(
        wrapped, name=name, grid=(rows // tr,), out_shape=out_shape, in_specs=in_specs, out_specs=out_specs,
        compiler_params=_params("arbitrary"),
    )(*args)


def _rstd(x):
    return lax.rsqrt(jnp.mean(x * x, axis=-1, keepdims=True) + NORM_EPS)


def _norm_bwd(u, n, r):
    return r * (u - n * jnp.mean(u * n, axis=-1, keepdims=True))


def _colsum(v):
    return jnp.sum(v, axis=0, keepdims=True)


def _norm_fwd(x, g, name):
    def body(x_ref, g_ref, h_ref):
        xv = x_ref[...]
        h_ref[...] = ((xv * _rstd(xv)) * g_ref[...]).astype(BF16)

    return _rowwise(body, name, x.shape[0], 512, [x], [g], [(D, BF16)])[0]


def _resid_norm_fwd(x, mix, g_post, g_pre):
    def body(x_ref, mix_ref, g2_ref, g3_ref, x1_ref, h2_ref):
        mv = mix_ref[...]
        x1 = x_ref[...] + (mv * _rstd(mv)) * g2_ref[...]
        x1_ref[...] = x1
        h2_ref[...] = ((x1 * _rstd(x1)) * g3_ref[...]).astype(BF16)

    return _rowwise(body, "resid_norm_fwd", x.shape[0], 512, [x, mix], [g_post, g_pre], [(D, F32), (D, BF16)])


def _gate_merge(gpre, ys, b_gate):
    def body(gp_ref, ya_ref, yb_ref, yc_ref, b_ref, m_ref):
        acc = None
        for k, y_ref in enumerate((ya_ref, yb_ref, yc_ref)):
            cols = slice(k * D, (k + 1) * D)
            gate = jax.nn.sigmoid(gp_ref[:, cols].astype(F32) + b_ref[:, cols])
            term = gate * y_ref[...].astype(F32)
            acc = term if acc is None else acc + term
        m_ref[...] = acc.astype(BF16)

    return _rowwise(body, "gate_merge", gpre.shape[0], 256, [gpre, *ys], [b_gate], [(D, BF16)])[0]


def _swiglu_fwd(gu):
    def body(a_ref, b_ref, f_ref):
        a = a_ref[...].astype(F32)
        f_ref[...] = (a * jax.nn.sigmoid(a) * b_ref[...].astype(F32)).astype(BF16)

    return _rowwise(body, "swiglu_fwd", gu.shape[0], 256, [(gu, D_FF, 0), (gu, D_FF, 1)], [], [(D_FF, BF16)])[0]


def _loss_head(x1, fo, tgt, g_post):
    def body(x1_ref, fo_ref, t_ref, g_ref, dy_ref, dfo_ref, dg_ref, loss_ref):
        fo_v = fo_ref[...]
        r = _rstd(fo_v)
        n = fo_v * r
        err = (x1_ref[...] + n * g_ref[...]) - t_ref[...]
        loss_ref[...] += _colsum(err * err)
        dy = err * (1.0 / D)
        dy_ref[...] = dy
        dg_ref[...] += _colsum(dy * n)
        dfo_ref[...] = _norm_bwd(dy * g_ref[...], n, r).astype(BF16)

    return _rowwise(body, "loss_head", x1.shape[0], 256, [x1, fo, tgt], [g_post], [(D, F32), (D, BF16)], (D, D))


def _swiglu_bwd(df, gu):
    def body(df_ref, gu_ref, dgu_ref):
        a = gu_ref[:, :D_FF].astype(F32)
        b = gu_ref[:, D_FF:].astype(F32)
        d = df_ref[...].astype(F32)
        s = jax.nn.sigmoid(a)
        dgu_ref[:, :D_FF] = (d * b * (s * (1.0 + a * (1.0 - s)))).astype(BF16)
        dgu_ref[:, D_FF:] = (d * (a * s)).astype(BF16)

    return _rowwise(body, "swiglu_bwd", df.shape[0], 256, [df, gu], [], [(2 * D_FF, BF16)])[0]


def _mid_norm_bwd(dh2, x1, dy, mix, g_pre, g_post):
    def body(dh_ref, x1_ref, dy_ref, mix_ref, g3_ref, g2_ref, dx1_ref, dmix_ref, dg3_ref, dg2_ref):
        x1v = x1_ref[...]
        r3 = _rstd(x1v)
        n3 = x1v * r3
        dh = dh_ref[...]
        dg3_ref[...] += _colsum(dh * n3)
        dx1 = dy_ref[...] + _norm_bwd(dh * g3_ref[...], n3, r3)
        dx1_ref[...] = dx1
        mv = mix_ref[...]
        r2 = _rstd(mv)
        n2 = mv * r2
        dg2_ref[...] += _colsum(dx1 * n2)
        dmix_ref[...] = _norm_bwd(dx1 * g2_ref[...], n2, r2).astype(BF16)

    return _rowwise(body, "mid_norm_bwd", x1.shape[0], 256, [dh2, x1, dy, mix], [g_pre, g_post],
                    [(D, F32), (D, BF16)], (D, D))


def _gate_bwd(dmerged, gpre, ys, b_gate):
    def body(dm_ref, gp_ref, ya_ref, yb_ref, yc_ref, b_ref, dya_ref, dyb_ref, dyc_ref, dgp_ref, db_ref):
        dm = dm_ref[...].astype(F32)
        for k, (y_ref, dy_ref) in enumerate(((ya_ref, dya_ref), (yb_ref, dyb_ref), (yc_ref, dyc_ref))):
            cols = slice(k * D, (k + 1) * D)
            gate = jax.nn.sigmoid(gp_ref[:, cols].astype(F32) + b_ref[:, cols])
            dy_ref[...] = (dm * gate).astype(BF16)
            dgp = (dm * y_ref[...].astype(F32)) * (gate * (1.0 - gate))
            dgp_ref[:, cols] = dgp.astype(BF16)
            db_ref[:, cols] += _colsum(dgp)

    return _rowwise(body, "gate_bwd", gpre.shape[0], 256, [dmerged, gpre, *ys], [b_gate],
                    [(D, BF16), (D, BF16), (D, BF16), (3 * D, BF16)], (3 * D,))


def _in_norm_bwd(dhs, x, dx1, g_pre):
    def body(da_ref, db_ref, dc_ref, x_ref, dx1_ref, g_ref, dx_ref, dg_ref):
        dh = (da_ref[...] + db_ref[...]) + dc_ref[...]
        xv = x_ref[...]
        r = _rstd(xv)
        n = xv * r
        dg_ref[...] += _colsum(dh * n)
        dx_ref[...] = dx1_ref[...] + _norm_bwd(dh * g_ref[...], n, r)

    return _rowwise(body, "in_norm_bwd", x.shape[0], 256, [*dhs, x, dx1], [g_pre], [(D, F32)], (D,))


def _gain_grad(dmn, mem):
    def body(d_ref, m_ref, dg_ref):
        mv = m_ref[...]
        dg_ref[...] += _colsum(d_ref[...] * (mv * _rstd(mv)))

    return _rowwise(body, "mem_gain_grad", mem.shape[0], 256, [dmn, mem], [], [], (D,))[0]


def _head_rowsum(v, head0):
    return (jnp.sum(jnp.where(head0, v, 0.0), axis=1, keepdims=True),
            jnp.sum(jnp.where(head0, 0.0, v), axis=1, keepdims=True))


KT = 256


def _make_suffix():
    tri = (lax.broadcasted_iota(jnp.int32, (KT, KT), 0) > lax.broadcasted_iota(jnp.int32, (KT, KT), 1)).astype(BF16)
    tri2 = jnp.concatenate([tri, tri], axis=0)

    def suffix(x):
        hi = x.astype(BF16)
        lo = (x - hi.astype(F32)).astype(BF16)
        return jnp.dot(jnp.concatenate([hi, lo], axis=1), tri2, preferred_element_type=F32)

    return suffix


def _sb_scores(qh, k, mask, suffix, run):
    z = lax.dot_general(qh, k, NT, preferred_element_type=F32) * 0.125
    lb = jnp.minimum(z, 0.0) - jnp.log1p(jnp.exp(-jnp.abs(z)))
    lk = lb - z
    if mask is not None:
        lk = jnp.where(mask, lk, 0.0)
    a = jnp.exp(lb + suffix(lk) + run)
    if mask is not None:
        a = jnp.where(mask, a, 0.0)
    return lb, lk, a


QB = KT


def _sb_tiles(i, tile, init):
    st = tile(i, init, True)
    return lax.fori_loop(0, i, lambda t, s: tile(i - 1 - t, s, False), st)


def _sb_consts():
    head0 = lax.broadcasted_iota(jnp.int32, (QB, BLK), 1) < 64
    row = lax.broadcasted_iota(jnp.int32, (2 * QB, KT), 0) & (QB - 1)
    return head0, row > lax.broadcasted_iota(jnp.int32, (2 * QB, KT), 1)


def _stack_heads(v, head0):
    zero = jnp.zeros_like(v)
    return jnp.concatenate([jnp.where(head0, v, zero), jnp.where(head0, zero, v)], axis=0)


def _unstack_heads(v, head0):
    n = v.shape[0] // 2
    return jnp.where(head0, v[:n], v[n:])


def _sb_fwd(proj_a, seq):
    bsz = proj_a.shape[0]

    def body(x_ref, o_ref, o32_ref, acc_ref):
        head0, diag_mask = _sb_consts()
        suffix = _make_suffix()

        def qblock(i, carry):
            r0 = pl.multiple_of(i * QB, QB)
            qs = _stack_heads(x_ref[pl.ds(r0, QB), 0:128], head0)

            def tile(jt, run, masked):
                c0 = pl.multiple_of(jt * KT, KT)
                k = x_ref[pl.ds(c0, KT), 128:256]
                v = x_ref[pl.ds(c0, KT), 256:384]
                _, lk, a = _sb_scores(qs, k, diag_mask if masked else None, suffix, run)
                pv = jnp.dot(a.astype(BF16), v, preferred_element_type=F32)
                if masked:
                    acc_ref[...] = pv
                else:
                    acc_ref[...] += pv
                return run + jnp.sum(lk, axis=1, keepdims=True)

            _sb_tiles(i, tile, jnp.zeros((2 * QB, 1), F32))
            o = _unstack_heads(acc_ref[...], head0)
            o32_ref[pl.ds(r0, QB), :] = o
            o_ref[pl.ds(r0, QB), :] = o.astype(BF16)
            return carry

        lax.fori_loop(0, seq // QB, qblock, 0)

    out_spec = pl.BlockSpec((None, seq, BLK), lambda b, hp: (b, 0, hp))
    return pl.pallas_call(
        body, name="sb_attn_fwd", grid=(bsz, 4),
        out_shape=(jax.ShapeDtypeStruct((bsz, seq, 512), BF16), jax.ShapeDtypeStruct((bsz, seq, 512), F32)),
        in_specs=[pl.BlockSpec((None, seq, 384), lambda b, hp: (b, 0, hp))],
        out_specs=(out_spec, out_spec),
        scratch_shapes=[pltpu.VMEM((2 * QB, BLK), F32)],
        compiler_params=_params("parallel", "parallel"),
    )(proj_a)


def _sb_bwd(proj_a, d_o, o_a, seq):
    bsz = proj_a.shape[0]

    def body(x_ref, do_ref, o_ref, d_ref, dq_acc, dk_acc, dv_acc):
        head0, diag_mask = _sb_consts()
        suffix = _make_suffix()
        dk_acc[...] = jnp.zeros_like(dk_acc)
        dv_acc[...] = jnp.zeros_like(dv_acc)

        def qblock(i, carry):
            r0 = pl.multiple_of(i * QB, QB)
            qs = _stack_heads(x_ref[pl.ds(r0, QB), 0:128], head0)
            do = do_ref[pl.ds(r0, QB), :]
            dos = _stack_heads(do, head0)
            dsum = jnp.concatenate(_head_rowsum(do.astype(F32) * o_ref[pl.ds(r0, QB), :], head0), axis=0)

            def tile(jt, st, masked):
                run, grun = st
                c0 = pl.multiple_of(jt * KT, KT)
                k = x_ref[pl.ds(c0, KT), 128:256]
                v = x_ref[pl.ds(c0, KT), 256:384]
                lb, lk, a = _sb_scores(qs, k, diag_mask if masked else None, suffix, run)
                a16 = a.astype(BF16)
                g = a16.astype(F32) * lax.dot_general(dos, v, NT, preferred_element_type=F32)
                before = dsum - ((grun + suffix(g)) + g)
                dz = g - jnp.exp(lb) * (g + before)
                if masked:
                    dz = jnp.where(diag_mask, dz, 0.0)
                dz = (dz * 0.125).astype(BF16)
                dq = jnp.dot(dz, k, preferred_element_type=F32)
                if masked:
                    dq_acc[...] = dq
                else:
                    dq_acc[...] += dq
                dk_acc[pl.ds(c0, KT), :] += lax.dot_general(dz, qs, TN, preferred_element_type=F32)
                dv_acc[pl.ds(c0, KT), :] += lax.dot_general(a16, dos, TN, preferred_element_type=F32)
                return run + jnp.sum(lk, axis=1, keepdims=True), grun + jnp.sum(g, axis=1, keepdims=True)

            z1 = jnp.zeros((2 * QB, 1), F32)
            _sb_tiles(i, tile, (z1, z1))
            d_ref[pl.ds(r0, QB), 0:128] = _unstack_heads(dq_acc[...], head0).astype(BF16)
            return carry

        lax.fori_loop(0, seq // QB, qblock, 0)
        d_ref[:, 128:256] = dk_acc[...].astype(BF16)
        d_ref[:, 256:384] = dv_acc[...].astype(BF16)

    return pl.pallas_call(
        body, name="sb_attn_bwd", grid=(bsz, 4),
        out_shape=jax.ShapeDtypeStruct((bsz, seq, WA), BF16),
        in_specs=[pl.BlockSpec((None, seq, 384), lambda b, hp: (b, 0, hp)),
                  pl.BlockSpec((None, seq, BLK), lambda b, hp: (b, 0, hp)),
                  pl.BlockSpec((None, seq, BLK), lambda b, hp: (b, 0, hp))],
        out_specs=pl.BlockSpec((None, seq, 384), lambda b, hp: (b, 0, hp)),
        scratch_shapes=[pltpu.VMEM((2 * QB, BLK), F32), pltpu.VMEM((seq, BLK), F32), pltpu.VMEM((seq, BLK), F32)],
        compiler_params=_params("parallel", "parallel"),
    )(proj_a, d_o, o_a)


def _rope_tables(seq):
    inv_freq = ROPE_THETA ** (-jnp.arange(32, dtype=F32) * 2.0 / 64)
    ang = jnp.arange(seq).astype(F32)[:, None] * inv_freq[None, :]
    cos, sin = jnp.cos(ang), jnp.sin(ang)
    return jnp.tile(cos, (1, 4)), jnp.concatenate([-sin, sin, -sin, sin], axis=1)


def _make_rope(n_rows):
    lane = lax.broadcasted_iota(jnp.int32, (n_rows, BLK), 1)
    first = (lane & 63) < 32

    def rope(x, cos, sin):
        partner = jnp.where(first, pltpu.roll(x, 96, 1), pltpu.roll(x, 32, 1))
        return x * cos + partner * sin

    return rope


DIL_UNROLL = 4


def _dil_consts():
    head0 = lax.broadcasted_iota(jnp.int32, (BLK, BLK), 1) < 64
    row = lax.broadcasted_iota(jnp.int32, (2 * BLK, 2 * BLK), 0) & (BLK - 1)
    col = lax.broadcasted_iota(jnp.int32, (2 * BLK, 2 * BLK), 1)
    valid_prev = jnp.logical_and(col < BLK, col >= row)
    valid_cur = jnp.logical_and(col >= BLK, row >= col - BLK)
    return head0, valid_prev, valid_cur


def _dil_blocks(dil, seq, block):
    nq = seq // dil // BLK

    def rows(r, i):
        if dil == 1:
            return pl.ds(pl.multiple_of(i * BLK, BLK), BLK)
        return pl.ds(r + (dil * BLK) * i, BLK, stride=dil)

    def step(t, carry):
        for u in range(DIL_UNROLL):
            n = t * DIL_UNROLL + u
            r, i = lax.div(n, nq), lax.rem(n, nq)
            block(rows(r, i), rows(r, jnp.maximum(i - 1, 0)), i)
        return carry

    lax.fori_loop(0, seq // BLK // DIL_UNROLL, step, 0)


def _dil_scores(qf, kf, vf, cur, prev, i, consts):
    head0, valid_prev, valid_cur = consts
    qs = _stack_heads(qf[cur, :].astype(BF16), head0)
    kcat = jnp.concatenate([kf[prev, :], kf[cur, :]], axis=0).astype(BF16)
    vcat = jnp.concatenate([vf[prev, :], vf[cur, :]], axis=0).astype(BF16)
    valid = jnp.logical_or(valid_cur, jnp.logical_and(valid_prev, i > 0))
    s = lax.dot_general(qs, kcat, NT, preferred_element_type=F32) * 0.125
    return qs, kcat, vcat, s, valid


def _head_cols(v):
    return jnp.concatenate([v[:, 0:1], v[:, 64:65]], axis=0)


def _dil_load_qkv(x_ref, c, rope, cos, sin, qf, kf, vf):
    qf[...] = rope(x_ref[:, c:c + 128].astype(F32), cos, sin).astype(BF16).astype(F32)
    kf[...] = rope(x_ref[:, c + 128:c + 256].astype(F32), cos, sin).astype(BF16).astype(F32)
    vf[...] = x_ref[:, c + 256:c + 384].astype(F32)


def _dil_fwd(proj_b, cos_t, sin_t, seq):
    bsz = proj_b.shape[0]

    def body(x_ref, cos_ref, sin_ref, ob_ref, lse_ref, qf, kf, vf, og, lg):
        consts = _dil_consts()
        head0 = consts[0]
        rope = _make_rope(seq)
        cos, sin = cos_ref[...], sin_ref[...]
        for g, dil in enumerate(DIL_GROUPS):
            _dil_load_qkv(x_ref, 384 * g, rope, cos, sin, qf, kf, vf)

            def block(cur, prev, i, g=g):
                _, _, vcat, s, valid = _dil_scores(qf, kf, vf, cur, prev, i, consts)
                s = jnp.where(valid, s, NEG_INF)
                m = jnp.max(s, axis=1, keepdims=True)
                p = jnp.exp(s - m)
                den = jnp.sum(p, axis=1, keepdims=True)
                o = jnp.dot(p.astype(BF16), vcat, preferred_element_type=F32) / den
                og[g, cur, :] = _unstack_heads(o, head0)
                lg[g, cur, :] = _unstack_heads(jnp.broadcast_to(m + jnp.log(den), (2 * BLK, BLK)), head0)

            _dil_blocks(dil, seq, block)
        ls = [lg[0], lg[1], lg[2]]
        m = jnp.maximum(jnp.maximum(ls[0], ls[1]), ls[2])
        ws = [jnp.exp(l - m) for l in ls]
        den = (ws[0] + ws[1]) + ws[2]
        ob_ref[...] = (((ws[0] * og[0] + ws[1] * og[1]) + ws[2] * og[2]) / den).astype(BF16)
        lse_ref[...] = m + jnp.log(den)

    tab_spec = pl.BlockSpec((seq, BLK), lambda b, hp: (0, 0))
    out_spec = pl.BlockSpec((None, seq, BLK), lambda b, hp: (b, 0, hp))
    slab = pltpu.VMEM((seq, BLK), F32)
    return pl.pallas_call(
        body, name="dil_attn_fwd", grid=(bsz, 2),
        out_shape=(jax.ShapeDtypeStruct((bsz, seq, 256), BF16), jax.ShapeDtypeStruct((bsz, seq, 256), F32)),
        in_specs=[pl.BlockSpec((None, seq, WB // 2), lambda b, hp: (b, 0, hp)), tab_spec, tab_spec],
        out_specs=(out_spec, out_spec),
        scratch_shapes=[slab, slab, slab, pltpu.VMEM((3, seq, BLK), F32), pltpu.VMEM((3, seq, BLK), F32)],
        compiler_params=_params("parallel", "parallel"),
    )(proj_b, cos_t, sin_t)


def _dil_bwd(proj_b, cos_t, sin_t, d_ob, o_b, lse, seq):
    bsz = proj_b.shape[0]

    def body(x_ref, cos_ref, sin_ref, do_ref, ob_ref, lse_ref, d_ref, qf, kf, vf, dof, dsf, dq_s, dk_acc, dv_acc):
        consts = _dil_consts()
        head0 = consts[0]
        rope = _make_rope(seq)
        cos, sin = cos_ref[...], sin_ref[...]
        do_all = do_ref[...].astype(F32)
        dof[...] = do_all
        head0_all = lax.broadcasted_iota(jnp.int32, (seq, BLK), 1) < 64
        d0, d1 = _head_rowsum(do_all * ob_ref[...].astype(F32), head0_all)
        dsf[...] = jnp.where(head0_all, d0, d1)
        for g, dil in enumerate(DIL_GROUPS):
            _dil_load_qkv(x_ref, 384 * g, rope, cos, sin, qf, kf, vf)
            dk_acc[...] = jnp.zeros_like(dk_acc)
            dv_acc[...] = jnp.zeros_like(dv_acc)

            def block(cur, prev, i):
                qs, kcat, vcat, s, valid = _dil_scores(qf, kf, vf, cur, prev, i, consts)
                dos = _stack_heads(dof[cur, :].astype(BF16), head0)
                p = jnp.where(valid, jnp.exp(s - _head_cols(lse_ref[cur, :])), 0.0)
                dp = lax.dot_general(dos, vcat, NT, preferred_element_type=F32)
                ds = ((p * (dp - _head_cols(dsf[cur, :]))) * 0.125).astype(BF16)
                dq_s[cur, :] = _unstack_heads(jnp.dot(ds, kcat, preferred_element_type=F32), head0)
                dk = lax.dot_general(ds, qs, TN, preferred_element_type=F32)
                dv = lax.dot_general(p.astype(BF16), dos, TN, preferred_element_type=F32)
                dk_acc[prev, :] += dk[:BLK]
                dk_acc[cur, :] += dk[BLK:]
                dv_acc[prev, :] += dv[:BLK]
                dv_acc[cur, :] += dv[BLK:]

            _dil_blocks(dil, seq, block)
            c = 384 * g
            d_ref[:, c:c + 128] = rope(dq_s[...], cos, -sin).astype(BF16)
            d_ref[:, c + 128:c + 256] = rope(dk_acc[...], cos, -sin).astype(BF16)
            d_ref[:, c + 256:c + 384] = dv_acc[...].astype(BF16)

    x_spec = pl.BlockSpec((None, seq, WB // 2), lambda b, hp: (b, 0, hp))
    tab_spec = pl.BlockSpec((seq, BLK), lambda b, hp: (0, 0))
    tok_spec = pl.BlockSpec((None, seq, BLK), lambda b, hp: (b, 0, hp))
    return pl.pallas_call(
        body, name="dil_attn_bwd", grid=(bsz, 2),
        out_shape=jax.ShapeDtypeStruct((bsz, seq, WB), BF16),
        in_specs=[x_spec, tab_spec, tab_spec, tok_spec, tok_spec, tok_spec], out_specs=x_spec,
        scratch_shapes=[pltpu.VMEM((seq, BLK), F32)] * 8,
        compiler_params=_params("parallel", "parallel"),
    )(proj_b, cos_t, sin_t, d_ob, o_b, lse)


MEM_SCALE = 128 ** -0.5
MEM_QB = 256


def _mem_fwd(proj_a, kv, seq):
    bsz = proj_a.shape[0]

    def body(q_ref, k_ref, v_ref, o_ref):
        k, v = k_ref[...], v_ref[...]

        def qblock(i, carry):
            r0 = pl.multiple_of(i * MEM_QB, MEM_QB)
            s = lax.dot_general(q_ref[pl.ds(r0, MEM_QB), :], k, NT, preferred_element_type=F32) * MEM_SCALE
            p = jnp.exp(s - jnp.max(s, axis=1, keepdims=True))
            p = p / jnp.sum(p, axis=1, keepdims=True)
            o_ref[pl.ds(r0, MEM_QB), :] = jnp.dot(p.astype(BF16), v, preferred_element_type=F32).astype(BF16)
            return carry

        lax.fori_loop(0, seq // MEM_QB, qblock, 0)

    return pl.pallas_call(
        body, name="mem_attn_fwd", grid=(bsz, 4),
        out_shape=jax.ShapeDtypeStruct((bsz, seq, 512), BF16),
        in_specs=[pl.BlockSpec((None, seq, BLK), lambda b, h: (b, 0, 12 + h)),
                  pl.BlockSpec((None, MEM_LEN, BLK), lambda b, h: (b, 0, h)),
                  pl.BlockSpec((None, MEM_LEN, BLK), lambda b, h: (b, 0, 4 + h))],
        out_specs=pl.BlockSpec((None, seq, BLK), lambda b, h: (b, 0, h)),
        compiler_params=_params("parallel", "parallel"),
    )(proj_a, kv, kv)


def _mem_bwd(proj_a, kv, d_o, d_proj_a, seq):
    bsz = proj_a.shape[0]

    def body(q_ref, k_ref, v_ref, do_ref, _, dq_ref, dk_ref, dv_ref):
        k, v = k_ref[...], v_ref[...]

        def qblock(i, carry):
            dk, dv = carry
            r0 = pl.multiple_of(i * MEM_QB, MEM_QB)
            q, do = q_ref[pl.ds(r0, MEM_QB), :], do_ref[pl.ds(r0, MEM_QB), :]
            s = lax.dot_general(q, k, NT, preferred_element_type=F32) * MEM_SCALE
            p = jnp.exp(s - jnp.max(s, axis=1, keepdims=True))
            p = p / jnp.sum(p, axis=1, keepdims=True)
            dp = lax.dot_general(do, v, NT, preferred_element_type=F32)
            ds = ((p * (dp - jnp.sum(p * dp, axis=1, keepdims=True))) * MEM_SCALE).astype(BF16)
            dq_ref[pl.ds(r0, MEM_QB), :] = jnp.dot(ds, k, preferred_element_type=F32).astype(BF16)
            dk = dk + lax.dot_general(ds, q, TN, preferred_element_type=F32)
            dv = dv + lax.dot_general(p.astype(BF16), do, TN, preferred_element_type=F32)
            return dk, dv

        zero = jnp.zeros((MEM_LEN, BLK), F32)
        dk, dv = lax.fori_loop(0, seq // MEM_QB, qblock, (zero, zero))
        dk_ref[...] = dk.astype(BF16)
        dv_ref[...] = dv.astype(BF16)

    kv_spec = pl.BlockSpec((None, MEM_LEN, BLK), lambda b, h: (b, 0, h))
    return pl.pallas_call(
        body, name="mem_attn_bwd", grid=(bsz, 4),
        out_shape=(jax.ShapeDtypeStruct((bsz, seq, WA), BF16), jax.ShapeDtypeStruct((bsz, MEM_LEN, 512), BF16),
                   jax.ShapeDtypeStruct((bsz, MEM_LEN, 512), BF16)),
        in_specs=[pl.BlockSpec((None, seq, BLK), lambda b, h: (b, 0, 12 + h)), kv_spec,
                  pl.BlockSpec((None, MEM_LEN, BLK), lambda b, h: (b, 0, 4 + h)),
                  pl.BlockSpec((None, seq, BLK), lambda b, h: (b, 0, h)), ANY],
        out_specs=(pl.BlockSpec((None, seq, BLK), lambda b, h: (b, 0, 12 + h)), kv_spec, kv_spec),
        input_output_aliases={4: 0},
        compiler_params=_params("parallel", "parallel"),
    )(proj_a, kv, kv, d_o, d_proj_a)


def _mesh_pos():
    return lax.axis_index("x"), lax.axis_index("y"), lax.axis_index("c")


def _all_gather(shard, name):
    m_per, n = shard.shape

    def body(x_ref, out_ref, send_sems, recv_sems, local_sem):
        x, y, c = _mesh_pos()
        me, sibling = (x, y, c), (x, y, 1 - c)
        chips = [(1 - x, y), (x, 1 - y), (1 - x, 1 - y)]

        def rows(px, py, pc):
            return out_ref.at[pl.ds((4 * px + 2 * py + pc) * m_per, m_per), :]

        def copy(k, block, to, src=None):
            return pltpu.make_async_remote_copy(
                src_ref=rows(*block) if src is None else src, dst_ref=rows(*block),
                send_sem=send_sems.at[k], recv_sem=recv_sems.at[k], device_id=to, device_id_type=MESH)

        mine = pltpu.make_async_copy(x_ref, rows(*me), local_sem)
        mine.start()
        first = [copy(0, me, sibling, src=x_ref)]
        first += [copy(1 + j, me, (*chip, c), src=x_ref) for j, chip in enumerate(chips)]
        for cp in first:
            cp.start()
        passed = [copy(4 + j, (*chip, c), sibling) for j, chip in enumerate(chips)]
        for j, chip in enumerate(chips):
            copy(1 + j, (*chip, c), me).wait_recv()
            passed[j].start()
        copy(0, sibling, me).wait_recv()
        for j, chip in enumerate(chips):
            copy(4 + j, (*chip, 1 - c), me).wait_recv()
        for cp in first + passed:
            cp.wait_send()
        mine.wait()

    return pl.pallas_call(
        body, name=name, out_shape=jax.ShapeDtypeStruct((N_DEV * m_per, n), shard.dtype),
        in_specs=[ANY], out_specs=ANY,
        scratch_shapes=[pltpu.SemaphoreType.DMA((7,)), pltpu.SemaphoreType.DMA((7,)), pltpu.SemaphoreType.DMA(())],
    )(shard)


def _exchange(parts):
    n_rows, n = parts.shape[1:]

    def body(g_ref, out_ref, send_sems, recv_sems, local_sem):
        x, y, c = _mesh_pos()
        me = 4 * x + 2 * y + c
        own = pltpu.make_async_copy(g_ref.at[me], out_ref.at[me], local_sem)
        own.start()
        sends, recvs = [], []
        for k in range(1, N_DEV):
            px = 1 - x if k & 4 else x
            py = 1 - y if k & 2 else y
            pc = 1 - c if k & 1 else c
            peer = 4 * px + 2 * py + pc
            sems = dict(send_sem=send_sems.at[k - 1], recv_sem=recv_sems.at[k - 1],
                        device_id=(px, py, pc), device_id_type=MESH)
            sends.append(pltpu.make_async_remote_copy(src_ref=g_ref.at[peer], dst_ref=out_ref.at[me], **sems))
            recvs.append(pltpu.make_async_remote_copy(src_ref=g_ref.at[me], dst_ref=out_ref.at[peer], **sems))
        for cp in sends:
            cp.start()
        for cp in recvs:
            cp.wait_recv()
        for cp in sends:
            cp.wait_send()
        own.wait()

    return pl.pallas_call(
        body, name="grad_exchange", out_shape=jax.ShapeDtypeStruct((N_DEV, n_rows, n), parts.dtype),
        in_specs=[ANY], out_specs=ANY,
        scratch_shapes=[pltpu.SemaphoreType.DMA((7,)), pltpu.SemaphoreType.DMA((7,)), pltpu.SemaphoreType.DMA(())],
    )(parts)


def _adamw(w, g, m, v):
    m = ADAM_B1 * m + (1.0 - ADAM_B1) * g
    v = ADAM_B2 * v + (1.0 - ADAM_B2) * (g * g)
    m_hat = m / (1.0 - ADAM_B1 ** ADAM_STEP)
    v_hat = v / (1.0 - ADAM_B2 ** ADAM_STEP)
    return -ADAM_LR * (m_hat / (jnp.sqrt(v_hat) + ADAM_EPS) + ADAM_WD * w), m, v


def _reduce_adamw(recv, w, m, v):
    rows = w.shape[0]
    tr = 240

    def body(r_ref, w_ref, m_ref, v_ref, g_out, d_out, m_out, v_out):
        g = r_ref[0].astype(F32)
        for s in range(1, N_DEV):
            g = g + r_ref[s].astype(F32)
        g_out[...] = g
        d_out[...], m_out[...], v_out[...] = _adamw(w_ref[...], g, m_ref[...], v_ref[...])

    spec = pl.BlockSpec((tr, D), lambda i: (i, 0))
    return pl.pallas_call(
        body, name="reduce_adamw", grid=(rows // tr,),
        out_shape=[jax.ShapeDtypeStruct((rows, D), F32)] * 4,
        in_specs=[pl.BlockSpec((N_DEV, tr, D), lambda i: (0, i, 0)), spec, spec, spec],
        out_specs=[spec] * 4, compiler_params=_params("arbitrary"),
    )(recv, w, m, v)


def _small_adamw(gathered, w, m, v):
    def body(g_ref, w_ref, m_ref, v_ref, g_out, d_out, m_out, v_out, loss_out):
        tot = g_ref[0]
        for s in range(1, N_DEV):
            tot = tot + g_ref[s]
        g = tot[0:8]
        g_out[...] = g
        d_out[...], m_out[...], v_out[...] = _adamw(w_ref[...], g, m_ref[...], v_ref[...])
        loss_out[...] = jnp.broadcast_to((0.5 / D) * jnp.sum(tot[8:9], axis=1, keepdims=True), (8, BLK))

    out = [jax.ShapeDtypeStruct((8, D), F32)] * 4 + [jax.ShapeDtypeStruct((8, BLK), F32)]
    return pl.pallas_call(body, name="small_adamw", out_shape=out, compiler_params=_params())(gathered, w, m, v)


def _pack(ws, dtype):
    return jnp.concatenate([w.astype(dtype).reshape(-1, D) for w in ws], axis=0)


def _pick_chunks(w, chunks):
    return jnp.concatenate([w[:, BLK * c:BLK * (c + 1)] for c in chunks], axis=1)


def kernel(x, mem, g_pre_mix, g_post_mix, g_pre_ffn, g_post_ffn, g_mem, w_in, w_mem_kv, w_br_sb, w_br_dil, w_br_mem, w_gate, b_gate, w_o, w_ffn_in, w_ffn_out, loss_target, m_g_pre_mix, m_g_post_mix, m_g_pre_ffn, m_g_post_ffn, m_g_mem, m_w_in, m_w_mem_kv, m_w_br_sb, m_w_br_dil, m_w_br_mem, m_w_gate, m_b_gate, m_w_o, m_w_ffn_in, m_w_ffn_out, v_g_pre_mix, v_g_post_mix, v_g_pre_ffn, v_g_post_ffn, v_g_mem, v_w_in, v_w_mem_kv, v_w_br_sb, v_w_br_dil, v_w_br_mem, v_w_gate, v_b_gate, v_w_o, v_w_ffn_in, v_w_ffn_out):
    bsz, seq, _ = x.shape
    tokens = bsz * seq
    xf, tgt, memf = x.reshape(tokens, D), loss_target.reshape(tokens, D), mem.reshape(bsz * MEM_LEN, D)
    big_w = [w_in, w_mem_kv, w_br_sb, w_br_dil, w_br_mem, w_gate, w_o, w_ffn_in, w_ffn_out]
    big_m = [m_w_in, m_w_mem_kv, m_w_br_sb, m_w_br_dil, m_w_br_mem, m_w_gate, m_w_o, m_w_ffn_in, m_w_ffn_out]
    big_v = [v_w_in, v_w_mem_kv, v_w_br_sb, v_w_br_dil, v_w_br_mem, v_w_gate, v_w_o, v_w_ffn_in, v_w_ffn_out]
    shard_shapes = [w.shape for w in big_w]

    gathered = _all_gather(_pack([w[0] for w in big_w], BF16), "weight_all_gather").reshape(N_DEV, PACK_TOTAL, D)

    def full(i, by_rows):
        part = gathered[:, PACK_OFFS[i]:PACK_OFFS[i] + PACK_ROWS[i]]
        k, n = shard_shapes[i][1:]
        if by_rows:
            return part.reshape(N_DEV * k, n)
        return part.reshape(N_DEV, k, n).transpose(1, 0, 2).reshape(k, N_DEV * n)

    fw_in, fw_mem_kv, fw_br_sb, fw_br_dil, fw_br_mem = full(0, False), full(1, True), full(2, False), full(3, False), full(4, False)
    fw_gate, fw_o, fw_ffn_in, fw_ffn_out = full(5, False), full(6, True), full(7, False), full(8, True)
    w_a, w_b = _pick_chunks(fw_in, CHUNKS_A), _pick_chunks(fw_in, CHUNKS_B)

    h = _norm_fwd(xf, g_pre_mix, "pre_mix_norm")
    proj_a = _matmul(h, w_a, "nn", BF16, "proj_a").reshape(bsz, seq, WA)
    proj_b = _matmul(h, w_b, "nn", BF16, "proj_b").reshape(bsz, seq, WB)
    gpre = _matmul(h, fw_gate, "nn", BF16, "gate_proj")
    o_a, o_a32 = _sb_fwd(proj_a, seq)
    cos_t, sin_t = _rope_tables(seq)
    o_b, lse_b = _dil_fwd(proj_b, cos_t, sin_t, seq)
    mn = _norm_fwd(memf, g_mem, "mem_norm")
    kv = _matmul(mn, fw_mem_kv, "nn", BF16, "mem_kv_proj").reshape(bsz, MEM_LEN, D)
    o_c = _mem_fwd(proj_a, kv, seq)
    o_a2, o_b2, o_c2 = o_a.reshape(tokens, 512), o_b.reshape(tokens, 256), o_c.reshape(tokens, 512)
    ys = [_matmul(o_a2, fw_br_sb, "nn", BF16, "branch_sb"), _matmul(o_b2, fw_br_dil, "nn", BF16, "branch_dil"),
          _matmul(o_c2, fw_br_mem, "nn", BF16, "branch_mem")]
    merged = _gate_merge(gpre, ys, b_gate)
    mix = _matmul(merged, fw_o, "nn", F32, "out_proj")
    x1, h2 = _resid_norm_fwd(xf, mix, g_post_mix, g_pre_ffn)
    gu = _matmul(h2, fw_ffn_in, "nn", BF16, "ffn_in")
    f = _swiglu_fwd(gu)
    fo = _matmul(f, fw_ffn_out, "nn", F32, "ffn_out")
    dy, dfo, dg_post_ffn, loss_lanes = _loss_head(x1, fo, tgt, g_post_ffn)

    df = _matmul(dfo, fw_ffn_out, "nt", BF16, "d_ffn_act")
    gw_ffn_out = _matmul(f, dfo, "tn", F32, "gw_ffn_out")
    dgu = _swiglu_bwd(df, gu)
    dh2 = _matmul(dgu, fw_ffn_in, "nt", F32, "d_h2")
    gw_ffn_in = _matmul(h2, dgu, "tn", F32, "gw_ffn_in")
    dx1, dmix, dg_pre_ffn, dg_post_mix = _mid_norm_bwd(dh2, x1, dy, mix, g_pre_ffn, g_post_mix)
    dmerged = _matmul(dmix, fw_o, "nt", BF16, "d_merged")
    gw_o = _matmul(merged, dmix, "tn", F32, "gw_o")
    dya, dyb, dyc, dgpre, db_gate = _gate_bwd(dmerged, gpre, ys, b_gate)
    d_oa = _matmul(dya, fw_br_sb, "nt", BF16, "d_o_sb").reshape(bsz, seq, 512)
    d_ob = _matmul(dyb, fw_br_dil, "nt", BF16, "d_o_dil").reshape(bsz, seq, 256)
    d_oc = _matmul(dyc, fw_br_mem, "nt", BF16, "d_o_mem").reshape(bsz, seq, 512)
    gw_br_sb = _matmul(o_a2, dya, "tn", F32, "gw_br_sb")
    gw_br_dil = _matmul(o_b2, dyb, "tn", F32, "gw_br_dil")
    gw_br_mem = _matmul(o_c2, dyc, "tn", F32, "gw_br_mem")
    d_proj_a = _sb_bwd(proj_a, d_oa, o_a32, seq)
    d_proj_a, dk_m, dv_m = _mem_bwd(proj_a, kv, d_oc, d_proj_a, seq)
    d_proj_b = _dil_bwd(proj_b, cos_t, sin_t, d_ob, o_b, lse_b, seq).reshape(tokens, WB)
    d_proj_a = d_proj_a.reshape(tokens, WA)
    dhs = [_matmul(d_proj_a, w_a, "nt", F32, "d_h_a"), _matmul(d_proj_b, w_b, "nt", F32, "d_h_b"),
           _matmul(dgpre, fw_gate, "nt", F32, "d_h_gate")]
    gw_a = _matmul(h, d_proj_a, "tn", F32, "gw_in_a")
    gw_b = _matmul(h, d_proj_b, "tn", F32, "gw_in_b")
    gw_gate = _matmul(h, dgpre, "tn", F32, "gw_gate")
    dx, dg_pre_mix = _in_norm_bwd(dhs, xf, dx1, g_pre_mix)
    dkv = jnp.concatenate([dk_m, dv_m], axis=-1).reshape(bsz * MEM_LEN, D)
    gw_mem_kv = _matmul(mn, dkv, "tn", F32, "gw_mem_kv")
    dmn = _matmul(dkv, fw_mem_kv, "nt", F32, "d_mem_norm")
    dg_mem = _gain_grad(dmn, memf)

    gw_ab = jnp.concatenate([gw_a, gw_b], axis=1)
    where = {c: i for i, c in enumerate(CHUNKS_A + CHUNKS_B)}
    gw_in = _pick_chunks(gw_ab, [where[c] for c in range(34)])
    full_grads = [gw_in, gw_mem_kv, gw_br_sb, gw_br_dil, gw_br_mem, gw_gate, gw_o, gw_ffn_in, gw_ffn_out]
    by_rows = [False, True, False, False, False, False, True, False, True]
    parts = []
    for gw, shp, rows_sharded in zip(full_grads, shard_shapes, by_rows):
        k, n = shp[1:]
        if rows_sharded:
            parts.append(gw.astype(BF16).reshape(N_DEV, (k * n) // D, D))
        else:
            parts.append(gw.astype(BF16).reshape(k, N_DEV, n).transpose(1, 0, 2).reshape(N_DEV, (k * n) // D, D))
    recv = _exchange(jnp.concatenate(parts, axis=1))
    g_p, d_p, m_p, v_p = _reduce_adamw(recv, _pack([w[0] for w in big_w], F32), _pack([m[0] for m in big_m], F32),
                                       _pack([v[0] for v in big_v], F32))

    def unpack(p):
        return [p[PACK_OFFS[i]:PACK_OFFS[i] + PACK_ROWS[i]].reshape(shard_shapes[i]) for i in range(len(big_w))]

    big = [unpack(p) for p in (g_p, d_p, m_p, v_p)]

    small = jnp.concatenate([dg_pre_mix, dg_post_mix, dg_pre_ffn, dg_post_ffn, dg_mem, db_gate.reshape(3, D),
                             loss_lanes, jnp.zeros((7, D), F32)], axis=0)
    small_all = _all_gather(small, "small_all_gather").reshape(N_DEV, 16, D)

    def small_pack(gs, b):
        return jnp.concatenate([*gs, b.reshape(3, D)], axis=0)

    sm = _small_adamw(
        small_all, small_pack([g_pre_mix, g_post_mix, g_pre_ffn, g_post_ffn, g_mem], b_gate),
        small_pack([m_g_pre_mix, m_g_post_mix, m_g_pre_ffn, m_g_post_ffn, m_g_mem], m_b_gate),
        small_pack([v_g_pre_mix, v_g_post_mix, v_g_pre_ffn, v_g_post_ffn, v_g_mem], v_b_gate))
    loss = sm[4][0, 0]

    def leaves(k):
        t, bw = sm[k], big[k]
        return [t[0:1], t[1:2], t[2:3], t[3:4], t[4:5], *bw[0:6], t[5:8].reshape(1, 3 * D), *bw[6:9]]

    return (loss, dx.reshape(bsz, seq, D), *leaves(0), *leaves(1), *leaves(2), *leaves(3))
```

```python
import functools

import jax
import jax.numpy as jnp
from jax import lax
from jax.experimental import pallas as pl
from jax.experimental.pallas import tpu as pltpu

F32 = jnp.float32
BF16 = jnp.bfloat16
D = 1024
BLK = 128
MEM_LEN = 256
D_FF = 2816
NORM_EPS = 1e-6
NEG_INF = -1e30
ROPE_THETA = 10000.0
ADAM_LR, ADAM_B1, ADAM_B2, ADAM_EPS, ADAM_WD, ADAM_STEP = 0.001, 0.9, 0.999, 1e-08, 0.01, 10
N_DEV = 8
VMEM_LIMIT_BYTES = 56 * 1024 * 1024
MESH = pl.DeviceIdType.MESH
ANY = pl.BlockSpec(memory_space=pl.ANY)

NT = (((1,), (1,)), ((), ()))
TN = (((0,), (0,)), ((), ()))
NN = (((1,), (0,)), ((), ()))
_DIMS = {"nn": NN, "nt": NT, "tn": TN}

BIG_NAMES = ("w_in", "w_mem_kv", "w_br_sb", "w_br_dil", "w_br_mem", "w_gate", "w_o", "w_ffn_in", "w_ffn_out")
BY_ROWS = (False, True, False, False, False, False, True, False, True)
GATHER_FIRST = (0, 5)
GATHER_BEHIND = (1, 2, 3, 4, 6, 7, 8)
REDUCE_BEHIND = (2, 3, 4, 5, 6, 7, 8)
REDUCE_LAST = (0, 1)

CHUNKS_A = tuple(c for hp in range(4) for c in (hp, 4 + hp, 8 + hp)) + (30, 31, 32, 33)
CHUNKS_B = tuple(c for hp in range(2) for g in range(3) for c in (12 + 6 * g + hp, 14 + 6 * g + hp, 16 + 6 * g + hp))
WA, WB = 128 * len(CHUNKS_A), 128 * len(CHUNKS_B)
DIL_GROUPS = (1, 4, 16)


def _params(*sem):
    return pltpu.CompilerParams(dimension_semantics=sem or None, vmem_limit_bytes=VMEM_LIMIT_BYTES)


def _tile(n, cap):
    if n <= 128:
        return n
    assert n % 128 == 0, n
    best = 128
    for t in range(128, min(n, cap) + 1, 128):
        if n % t == 0:
            best = t
    return best


def _matmul(a, b, mode, out_dtype, name, tm_cap=1536, tn_cap=1536, tk_cap=1536):
    if mode == "tn":
        (K, M), N = a.shape, b.shape[1]
    elif mode == "nt":
        (M, K), N = a.shape, b.shape[0]
    else:
        (M, K), N = a.shape, b.shape[1]
    tm, tn, tk = _tile(M, tm_cap), _tile(N, tn_cap), _tile(K, tk_cap)
    nm, nn, nk = M // tm, N // tn, K // tk
    dims = _DIMS[mode]

    def body(a_ref, b_ref, o_ref, *acc):
        p = lax.dot_general(a_ref[...], b_ref[...], dims, preferred_element_type=F32)
        if nk == 1:
            o_ref[...] = p.astype(o_ref.dtype)
        else:
            acc_ref, = acc
            k = pl.program_id(2)

            @pl.when(k == 0)
            def _():
                acc_ref[...] = p

            @pl.when(k > 0)
            def _():
                acc_ref[...] += p

            @pl.when(k == nk - 1)
            def _():
                o_ref[...] = acc_ref[...].astype(o_ref.dtype)

    n_outer = nk == 1 and (a.size * nn + b.size) < (a.size + b.size * nm)
    if n_outer:
        grid, ij = (nn, nm, nk), (lambda g0, g1: (g1, g0))
    else:
        grid, ij = (nm, nn, nk), (lambda g0, g1: (g0, g1))
    if mode == "tn":
        a_spec = pl.BlockSpec((tk, tm), lambda g0, g1, k: (k, ij(g0, g1)[0]))
    else:
        a_spec = pl.BlockSpec((tm, tk), lambda g0, g1, k: (ij(g0, g1)[0], k))
    if mode == "nt":
        b_spec = pl.BlockSpec((tn, tk), lambda g0, g1, k: (ij(g0, g1)[1], k))
    else:
        b_spec = pl.BlockSpec((tk, tn), lambda g0, g1, k: (k, ij(g0, g1)[1]))
    return pl.pallas_call(
        body, name=name, grid=grid,
        out_shape=jax.ShapeDtypeStruct((M, N), out_dtype),
        in_specs=[a_spec, b_spec],
        out_specs=pl.BlockSpec((tm, tn), lambda g0, g1, k: ij(g0, g1)),
        scratch_shapes=[pltpu.VMEM((tm, tn), F32)] if nk > 1 else [],
        compiler_params=_params("parallel", "parallel", "arbitrary"),
    )(a, b)


def _rowwise(body, name, rows, tr, row_ins, vec_ins, row_outs, acc_outs=()):
    tr = min(tr, rows)
    assert rows % tr == 0
    in_specs, args = [], []
    for r in row_ins:
        arr, w, cb = r if isinstance(r, tuple) else (r, r.shape[1], 0)
        in_specs.append(pl.BlockSpec((tr, w), functools.partial(lambda i, cb: (i, cb), cb=cb)))
        args.append(arr)
    for v in vec_ins:
        in_specs.append(pl.BlockSpec(v.shape, lambda i: (0, 0)))
        args.append(v)
    out_shape = [jax.ShapeDtypeStruct((rows, w), dt) for w, dt in row_outs]
    out_shape += [jax.ShapeDtypeStruct((1, w), F32) for w in acc_outs]
    out_specs = [pl.BlockSpec((tr, w), lambda i: (i, 0)) for w, _ in row_outs]
    out_specs += [pl.BlockSpec((1, w), lambda i: (0, 0)) for w in acc_outs]
    n_acc = len(acc_outs)

    def wrapped(*refs):
        if n_acc:
            @pl.when(pl.program_id(0) == 0)
            def _():
                for r in refs[len(refs) - n_acc:]:
                    r[...] = jnp.zeros_like(r)
        body(*refs)

    return pl.pallas_call(
        wrapped, name=name, grid=(rows // tr,), out_shape=out_shape, in_specs=in_specs, out_specs=out_specs,
        compiler_params=_params("arbitrary"),
    )(*args)


def _rstd(x):
    return lax.rsqrt(jnp.mean(x * x, axis=-1, keepdims=True) + NORM_EPS)


def _norm_bwd(u, n, r):
    return r * (u - n * jnp.mean(u * n, axis=-1, keepdims=True))


def _colsum(v):
    return jnp.sum(v, axis=0, keepdims=True)


def _norm_fwd(x, g, name):
    def body(x_ref, g_ref, h_ref):
        xv = x_ref[...]
        h_ref[...] = ((xv * _rstd(xv)) * g_ref[...]).astype(BF16)

    return _rowwise(body, name, x.shape[0], 512, [x], [g], [(D, BF16)])[0]


def _resid_norm_fwd(x, mix, g_post, g_pre):
    def body(x_ref, mix_ref, g2_ref, g3_ref, x1_ref, h2_ref):
        mv = mix_ref[...]
        x1 = x_ref[...] + (mv * _rstd(mv)) * g2_ref[...]
        x1_ref[...] = x1
        h2_ref[...] = ((x1 * _rstd(x1)) * g3_ref[...]).astype(BF16)

    return _rowwise(body, "resid_norm_fwd", x.shape[0], 512, [x, mix], [g_post, g_pre], [(D, F32), (D, BF16)])


def _gate_merge(gpre, ys, b_gate):
    def body(gp_ref, ya_ref, yb_ref, yc_ref, b_ref, m_ref):
        acc = None
        for k, y_ref in enumerate((ya_ref, yb_ref, yc_ref)):
            cols = slice(k * D, (k + 1) * D)
            gate = jax.nn.sigmoid(gp_ref[:, cols].astype(F32) + b_ref[:, cols])
            term = gate * y_ref[...].astype(F32)
            acc = term if acc is None else acc + term
        m_ref[...] = acc.astype(BF16)

    return _rowwise(body, "gate_merge", gpre.shape[0], 256, [gpre, *ys], [b_gate], [(D, BF16)])[0]


def _swiglu_fwd(gu):
    def body(a_ref, b_ref, f_ref):
        a = a_ref[...].astype(F32)
        f_ref[...] = (a * jax.nn.sigmoid(a) * b_ref[...].astype(F32)).astype(BF16)

    return _rowwise(body, "swiglu_fwd", gu.shape[0], 256, [(gu, D_FF, 0), (gu, D_FF, 1)], [], [(D_FF, BF16)])[0]


def _loss_head(x1, fo, tgt, g_post):
    def body(x1_ref, fo_ref, t_ref, g_ref, dy_ref, dfo_ref, dg_ref, loss_ref):
        fo_v = fo_ref[...]
        r = _rstd(fo_v)
        n = fo_v * r
        err = (x1_ref[...] + n * g_ref[...]) - t_ref[...]
        loss_ref[...] += _colsum(err * err)
        dy = err * (1.0 / D)
        dy_ref[...] = dy
        dg_ref[...] += _colsum(dy * n)
        dfo_ref[...] = _norm_bwd(dy * g_ref[...], n, r).astype(BF16)

    return _rowwise(body, "loss_head", x1.shape[0], 256, [x1, fo, tgt], [g_post], [(D, F32), (D, BF16)], (D, D))


def _swiglu_bwd(df, gu):
    def body(df_ref, gu_ref, dgu_ref):
        a = gu_ref[:, :D_FF].astype(F32)
        b = gu_ref[:, D_FF:].astype(F32)
        d = df_ref[...].astype(F32)
        s = jax.nn.sigmoid(a)
        dgu_ref[:, :D_FF] = (d * b * (s * (1.0 + a * (1.0 - s)))).astype(BF16)
        dgu_ref[:, D_FF:] = (d * (a * s)).astype(BF16)

    return _rowwise(body, "swiglu_bwd", df.shape[0], 256, [df, gu], [], [(2 * D_FF, BF16)])[0]


def _mid_norm_bwd(dh2, x1, dy, mix, g_pre, g_post):
    def body(dh_ref, x1_ref, dy_ref, mix_ref, g3_ref, g2_ref, dx1_ref, dmix_ref, dg3_ref, dg2_ref):
        x1v = x1_ref[...]
        r3 = _rstd(x1v)
        n3 = x1v * r3
        dh = dh_ref[...]
        dg3_ref[...] += _colsum(dh * n3)
        dx1 = dy_ref[...] + _norm_bwd(dh * g3_ref[...], n3, r3)
        dx1_ref[...] = dx1
        mv = mix_ref[...]
        r2 = _rstd(mv)
        n2 = mv * r2
        dg2_ref[...] += _colsum(dx1 * n2)
        dmix_ref[...] = _norm_bwd(dx1 * g2_ref[...], n2, r2).astype(BF16)

    return _rowwise(body, "mid_norm_bwd", x1.shape[0], 256, [dh2, x1, dy, mix], [g_pre, g_post],
                    [(D, F32), (D, BF16)], (D, D))


def _gate_bwd(dmerged, gpre, ys, b_gate):
    def body(dm_ref, gp_ref, ya_ref, yb_ref, yc_ref, b_ref, dya_ref, dyb_ref, dyc_ref, dgp_ref, db_ref):
        dm = dm_ref[...].astype(F32)
        for k, (y_ref, dy_ref) in enumerate(((ya_ref, dya_ref), (yb_ref, dyb_ref), (yc_ref, dyc_ref))):
            cols = slice(k * D, (k + 1) * D)
            gate = jax.nn.sigmoid(gp_ref[:, cols].astype(F32) + b_ref[:, cols])
            dy_ref[...] = (dm * gate).astype(BF16)
            dgp = (dm * y_ref[...].astype(F32)) * (gate * (1.0 - gate))
            dgp_ref[:, cols] = dgp.astype(BF16)
            db_ref[:, cols] += _colsum(dgp)

    return _rowwise(body, "gate_bwd", gpre.shape[0], 256, [dmerged, gpre, *ys], [b_gate],
                    [(D, BF16), (D, BF16), (D, BF16), (3 * D, BF16)], (3 * D,))


def _in_norm_bwd(dhs, x, dx1, g_pre):
    def body(da_ref, db_ref, dc_ref, x_ref, dx1_ref, g_ref, dx_ref, dg_ref):
        dh = (da_ref[...] + db_ref[...]) + dc_ref[...]
        xv = x_ref[...]
        r = _rstd(xv)
        n = xv * r
        dg_ref[...] += _colsum(dh * n)
        dx_ref[...] = dx1_ref[...] + _norm_bwd(dh * g_ref[...], n, r)

    return _rowwise(body, "in_norm_bwd", x.shape[0], 256, [*dhs, x, dx1], [g_pre], [(D, F32)], (D,))


def _gain_grad(dmn, mem):
    def body(d_ref, m_ref, dg_ref):
        mv = m_ref[...]
        dg_ref[...] += _colsum(d_ref[...] * (mv * _rstd(mv)))

    return _rowwise(body, "mem_gain_grad", mem.shape[0], 256, [dmn, mem], [], [], (D,))[0]


def _head_rowsum(v, head0):
    return (jnp.sum(jnp.where(head0, v, 0.0), axis=1, keepdims=True),
            jnp.sum(jnp.where(head0, 0.0, v), axis=1, keepdims=True))


KT = 256


def _make_suffix():
    tri = (lax.broadcasted_iota(jnp.int32, (KT, KT), 0) > lax.broadcasted_iota(jnp.int32, (KT, KT), 1)).astype(BF16)
    tri2 = jnp.concatenate([tri, tri], axis=0)

    def suffix(x):
        hi = x.astype(BF16)
        lo = (x - hi.astype(F32)).astype(BF16)
        return jnp.dot(jnp.concatenate([hi, lo], axis=1), tri2, preferred_element_type=F32)

    return suffix


def _sb_scores(qh, k, mask, suffix, run):
    z = lax.dot_general(qh, k, NT, preferred_element_type=F32) * 0.125
    lb = jnp.minimum(z, 0.0) - jnp.log1p(jnp.exp(-jnp.abs(z)))
    lk = lb - z
    if mask is not None:
        lk = jnp.where(mask, lk, 0.0)
    a = jnp.exp(lb + suffix(lk) + run)
    if mask is not None:
        a = jnp.where(mask, a, 0.0)
    return lb, lk, a


QB = KT


def _sb_tiles(i, tile, init):
    st = tile(i, init, True)
    return lax.fori_loop(0, i, lambda t, s: tile(i - 1 - t, s, False), st)


def _sb_consts():
    head0 = lax.broadcasted_iota(jnp.int32, (QB, BLK), 1) < 64
    row = lax.broadcasted_iota(jnp.int32, (2 * QB, KT), 0) & (QB - 1)
    return head0, row > lax.broadcasted_iota(jnp.int32, (2 * QB, KT), 1)


def _stack_heads(v, head0):
    zero = jnp.zeros_like(v)
    return jnp.concatenate([jnp.where(head0, v, zero), jnp.where(head0, zero, v)], axis=0)


def _unstack_heads(v, head0):
    n = v.shape[0] // 2
    return jnp.where(head0, v[:n], v[n:])


def _sb_fwd(proj_a, seq, ride):
    bsz = proj_a.shape[0]

    def body(x_ref, ride_ref, o_ref, o32_ref, gathered_ref, acc_ref, *sems):
        finish_ride = _riding_exchange(ride_ref, gathered_ref, sems, gather=True)
        head0, diag_mask = _sb_consts()
        suffix = _make_suffix()

        def qblock(i, carry):
            r0 = pl.multiple_of(i * QB, QB)
            qs = _stack_heads(x_ref[pl.ds(r0, QB), 0:128], head0)

            def tile(jt, run, masked):
                c0 = pl.multiple_of(jt * KT, KT)
                k = x_ref[pl.ds(c0, KT), 128:256]
                v = x_ref[pl.ds(c0, KT), 256:384]
                _, lk, a = _sb_scores(qs, k, diag_mask if masked else None, suffix, run)
                pv = jnp.dot(a.astype(BF16), v, preferred_element_type=F32)
                if masked:
                    acc_ref[...] = pv
                else:
                    acc_ref[...] += pv
                return run + jnp.sum(lk, axis=1, keepdims=True)

            _sb_tiles(i, tile, jnp.zeros((2 * QB, 1), F32))
            o = _unstack_heads(acc_ref[...], head0)
            o32_ref[pl.ds(r0, QB), :] = o
            o_ref[pl.ds(r0, QB), :] = o.astype(BF16)
            return carry

        lax.fori_loop(0, seq // QB, qblock, 0)
        finish_ride()

    out_spec = pl.BlockSpec((None, seq, BLK), lambda b, hp: (b, 0, hp))
    return pl.pallas_call(
        body, name="sb_attn_fwd", grid=(bsz, 4),
        out_shape=(jax.ShapeDtypeStruct((bsz, seq, 512), BF16), jax.ShapeDtypeStruct((bsz, seq, 512), F32),
                   jax.ShapeDtypeStruct((N_DEV, *ride.shape), ride.dtype)),
        in_specs=[pl.BlockSpec((None, seq, 384), lambda b, hp: (b, 0, hp)), ANY],
        out_specs=(out_spec, out_spec, ANY),
        scratch_shapes=[pltpu.VMEM((2 * QB, BLK), F32), *EXCHANGE_SEMS],
        compiler_params=_params("arbitrary", "arbitrary"),
    )(proj_a, ride)


def _sb_bwd(proj_a, d_o, o_a, seq, ride):
    bsz = proj_a.shape[0]

    def body(x_ref, do_ref, o_ref, ride_ref, d_ref, received_ref, dq_acc, dk_acc, dv_acc, *sems):
        finish_ride = _riding_exchange(ride_ref, received_ref, sems, gather=False)
        head0, diag_mask = _sb_consts()
        suffix = _make_suffix()
        dk_acc[...] = jnp.zeros_like(dk_acc)
        dv_acc[...] = jnp.zeros_like(dv_acc)

        def qblock(i, carry):
            r0 = pl.multiple_of(i * QB, QB)
            qs = _stack_heads(x_ref[pl.ds(r0, QB), 0:128], head0)
            do = do_ref[pl.ds(r0, QB), :]
            dos = _stack_heads(do, head0)
            dsum = jnp.concatenate(_head_rowsum(do.astype(F32) * o_ref[pl.ds(r0, QB), :], head0), axis=0)

            def tile(jt, st, masked):
                run, grun = st
                c0 = pl.multiple_of(jt * KT, KT)
                k = x_ref[pl.ds(c0, KT), 128:256]
                v = x_ref[pl.ds(c0, KT), 256:384]
                lb, lk, a = _sb_scores(qs, k, diag_mask if masked else None, suffix, run)
                a16 = a.astype(BF16)
                g = a16.astype(F32) * lax.dot_general(dos, v, NT, preferred_element_type=F32)
                before = dsum - ((grun + suffix(g)) + g)
                dz = g - jnp.exp(lb) * (g + before)
                if masked:
                    dz = jnp.where(diag_mask, dz, 0.0)
                dz = (dz * 0.125).astype(BF16)
                dq = jnp.dot(dz, k, preferred_element_type=F32)
                if masked:
                    dq_acc[...] = dq
                else:
                    dq_acc[...] += dq
                dk_acc[pl.ds(c0, KT), :] += lax.dot_general(dz, qs, TN, preferred_element_type=F32)
                dv_acc[pl.ds(c0, KT), :] += lax.dot_general(a16, dos, TN, preferred_element_type=F32)
                return run + jnp.sum(lk, axis=1, keepdims=True), grun + jnp.sum(g, axis=1, keepdims=True)

            z1 = jnp.zeros((2 * QB, 1), F32)
            _sb_tiles(i, tile, (z1, z1))
            d_ref[pl.ds(r0, QB), 0:128] = _unstack_heads(dq_acc[...], head0).astype(BF16)
            return carry

        lax.fori_loop(0, seq // QB, qblock, 0)
        d_ref[:, 128:256] = dk_acc[...].astype(BF16)
        d_ref[:, 256:384] = dv_acc[...].astype(BF16)
        finish_ride()

    return pl.pallas_call(
        body, name="sb_attn_bwd", grid=(bsz, 4),
        out_shape=(jax.ShapeDtypeStruct((bsz, seq, WA), BF16), jax.ShapeDtypeStruct(ride.shape, ride.dtype)),
        in_specs=[pl.BlockSpec((None, seq, 384), lambda b, hp: (b, 0, hp)),
                  pl.BlockSpec((None, seq, BLK), lambda b, hp: (b, 0, hp)),
                  pl.BlockSpec((None, seq, BLK), lambda b, hp: (b, 0, hp)), ANY],
        out_specs=(pl.BlockSpec((None, seq, 384), lambda b, hp: (b, 0, hp)), ANY),
        scratch_shapes=[pltpu.VMEM((2 * QB, BLK), F32), pltpu.VMEM((seq, BLK), F32), pltpu.VMEM((seq, BLK), F32),
                        *EXCHANGE_SEMS],
        compiler_params=_params("arbitrary", "arbitrary"),
    )(proj_a, d_o, o_a, ride)


def _rope_tables(seq):
    inv_freq = ROPE_THETA ** (-jnp.arange(32, dtype=F32) * 2.0 / 64)
    ang = jnp.arange(seq).astype(F32)[:, None] * inv_freq[None, :]
    cos, sin = jnp.cos(ang), jnp.sin(ang)
    return jnp.tile(cos, (1, 4)), jnp.concatenate([-sin, sin, -sin, sin], axis=1)


def _make_rope(n_rows):
    lane = lax.broadcasted_iota(jnp.int32, (n_rows, BLK), 1)
    first = (lane & 63) < 32

    def rope(x, cos, sin):
        partner = jnp.where(first, pltpu.roll(x, 96, 1), pltpu.roll(x, 32, 1))
        return x * cos + partner * sin

    return rope


DIL_UNROLL = 4


def _dil_consts():
    head0 = lax.broadcasted_iota(jnp.int32, (BLK, BLK), 1) < 64
    row = lax.broadcasted_iota(jnp.int32, (2 * BLK, 2 * BLK), 0) & (BLK - 1)
    col = lax.broadcasted_iota(jnp.int32, (2 * BLK, 2 * BLK), 1)
    valid_prev = jnp.logical_and(col < BLK, col >= row)
    valid_cur = jnp.logical_and(col >= BLK, row >= col - BLK)
    return head0, valid_prev, valid_cur


def _dil_blocks(dil, seq, block):
    nq = seq // dil // BLK

    def rows(r, i):
        if dil == 1:
            return pl.ds(pl.multiple_of(i * BLK, BLK), BLK)
        return pl.ds(r + (dil * BLK) * i, BLK, stride=dil)

    def step(t, carry):
        for u in range(DIL_UNROLL):
            n = t * DIL_UNROLL + u
            r, i = lax.div(n, nq), lax.rem(n, nq)
            block(rows(r, i), rows(r, jnp.maximum(i - 1, 0)), i)
        return carry

    lax.fori_loop(0, seq // BLK // DIL_UNROLL, step, 0)


def _dil_scores(qf, kf, vf, cur, prev, i, consts):
    head0, valid_prev, valid_cur = consts
    qs = _stack_heads(qf[cur, :].astype(BF16), head0)
    kcat = jnp.concatenate([kf[prev, :], kf[cur, :]], axis=0).astype(BF16)
    vcat = jnp.concatenate([vf[prev, :], vf[cur, :]], axis=0).astype(BF16)
    valid = jnp.logical_or(valid_cur, jnp.logical_and(valid_prev, i > 0))
    s = lax.dot_general(qs, kcat, NT, preferred_element_type=F32) * 0.125
    return qs, kcat, vcat, s, valid


def _head_cols(v):
    return jnp.concatenate([v[:, 0:1], v[:, 64:65]], axis=0)


def _dil_load_qkv(x_ref, c, rope, cos, sin, qf, kf, vf):
    qf[...] = rope(x_ref[:, c:c + 128].astype(F32), cos, sin).astype(BF16).astype(F32)
    kf[...] = rope(x_ref[:, c + 128:c + 256].astype(F32), cos, sin).astype(BF16).astype(F32)
    vf[...] = x_ref[:, c + 256:c + 384].astype(F32)


def _dil_fwd(proj_b, cos_t, sin_t, seq):
    bsz = proj_b.shape[0]

    def body(x_ref, cos_ref, sin_ref, ob_ref, lse_ref, qf, kf, vf, og, lg):
        consts = _dil_consts()
        head0 = consts[0]
        rope = _make_rope(seq)
        cos, sin = cos_ref[...], sin_ref[...]
        for g, dil in enumerate(DIL_GROUPS):
            _dil_load_qkv(x_ref, 384 * g, rope, cos, sin, qf, kf, vf)

            def block(cur, prev, i, g=g):
                _, _, vcat, s, valid = _dil_scores(qf, kf, vf, cur, prev, i, consts)
                s = jnp.where(valid, s, NEG_INF)
                m = jnp.max(s, axis=1, keepdims=True)
                p = jnp.exp(s - m)
                den = jnp.sum(p, axis=1, keepdims=True)
                o = jnp.dot(p.astype(BF16), vcat, preferred_element_type=F32) / den
                og[g, cur, :] = _unstack_heads(o, head0)
                lg[g, cur, :] = _unstack_heads(jnp.broadcast_to(m + jnp.log(den), (2 * BLK, BLK)), head0)

            _dil_blocks(dil, seq, block)
        ls = [lg[0], lg[1], lg[2]]
        m = jnp.maximum(jnp.maximum(ls[0], ls[1]), ls[2])
        ws = [jnp.exp(l - m) for l in ls]
        den = (ws[0] + ws[1]) + ws[2]
        ob_ref[...] = (((ws[0] * og[0] + ws[1] * og[1]) + ws[2] * og[2]) / den).astype(BF16)
        lse_ref[...] = m + jnp.log(den)

    tab_spec = pl.BlockSpec((seq, BLK), lambda b, hp: (0, 0))
    out_spec = pl.BlockSpec((None, seq, BLK), lambda b, hp: (b, 0, hp))
    slab = pltpu.VMEM((seq, BLK), F32)
    return pl.pallas_call(
        body, name="dil_attn_fwd", grid=(bsz, 2),
        out_shape=(jax.ShapeDtypeStruct((bsz, seq, 256), BF16), jax.ShapeDtypeStruct((bsz, seq, 256), F32)),
        in_specs=[pl.BlockSpec((None, seq, WB // 2), lambda b, hp: (b, 0, hp)), tab_spec, tab_spec],
        out_specs=(out_spec, out_spec),
        scratch_shapes=[slab, slab, slab, pltpu.VMEM((3, seq, BLK), F32), pltpu.VMEM((3, seq, BLK), F32)],
        compiler_params=_params("parallel", "parallel"),
    )(proj_b, cos_t, sin_t)


def _dil_bwd(proj_b, cos_t, sin_t, d_ob, o_b, lse, seq):
    bsz = proj_b.shape[0]

    def body(x_ref, cos_ref, sin_ref, do_ref, ob_ref, lse_ref, d_ref, qf, kf, vf, dof, dsf, dq_s, dk_acc, dv_acc):
        consts = _dil_consts()
        head0 = consts[0]
        rope = _make_rope(seq)
        cos, sin = cos_ref[...], sin_ref[...]
        do_all = do_ref[...].astype(F32)
        dof[...] = do_all
        head0_all = lax.broadcasted_iota(jnp.int32, (seq, BLK), 1) < 64
        d0, d1 = _head_rowsum(do_all * ob_ref[...].astype(F32), head0_all)
        dsf[...] = jnp.where(head0_all, d0, d1)
        for g, dil in enumerate(DIL_GROUPS):
            _dil_load_qkv(x_ref, 384 * g, rope, cos, sin, qf, kf, vf)
            dk_acc[...] = jnp.zeros_like(dk_acc)
            dv_acc[...] = jnp.zeros_like(dv_acc)

            def block(cur, prev, i):
                qs, kcat, vcat, s, valid = _dil_scores(qf, kf, vf, cur, prev, i, consts)
                dos = _stack_heads(dof[cur, :].astype(BF16), head0)
                p = jnp.where(valid, jnp.exp(s - _head_cols(lse_ref[cur, :])), 0.0)
                dp = lax.dot_general(dos, vcat, NT, preferred_element_type=F32)
                ds = ((p * (dp - _head_cols(dsf[cur, :]))) * 0.125).astype(BF16)
                dq_s[cur, :] = _unstack_heads(jnp.dot(ds, kcat, preferred_element_type=F32), head0)
                dk = lax.dot_general(ds, qs, TN, preferred_element_type=F32)
                dv = lax.dot_general(p.astype(BF16), dos, TN, preferred_element_type=F32)
                dk_acc[prev, :] += dk[:BLK]
                dk_acc[cur, :] += dk[BLK:]
                dv_acc[prev, :] += dv[:BLK]
                dv_acc[cur, :] += dv[BLK:]

            _dil_blocks(dil, seq, block)
            c = 384 * g
            d_ref[:, c:c + 128] = rope(dq_s[...], cos, -sin).astype(BF16)
            d_ref[:, c + 128:c + 256] = rope(dk_acc[...], cos, -sin).astype(BF16)
            d_ref[:, c + 256:c + 384] = dv_acc[...].astype(BF16)

    x_spec = pl.BlockSpec((None, seq, WB // 2), lambda b, hp: (b, 0, hp))
    tab_spec = pl.BlockSpec((seq, BLK), lambda b, hp: (0, 0))
    tok_spec = pl.BlockSpec((None, seq, BLK), lambda b, hp: (b, 0, hp))
    return pl.pallas_call(
        body, name="dil_attn_bwd", grid=(bsz, 2),
        out_shape=jax.ShapeDtypeStruct((bsz, seq, WB), BF16),
        in_specs=[x_spec, tab_spec, tab_spec, tok_spec, tok_spec, tok_spec], out_specs=x_spec,
        scratch_shapes=[pltpu.VMEM((seq, BLK), F32)] * 8,
        compiler_params=_params("parallel", "parallel"),
    )(proj_b, cos_t, sin_t, d_ob, o_b, lse)


MEM_SCALE = 128 ** -0.5
MEM_QB = 256


def _mem_fwd(proj_a, kv, seq):
    bsz = proj_a.shape[0]

    def body(q_ref, k_ref, v_ref, o_ref):
        k, v = k_ref[...], v_ref[...]

        def qblock(i, carry):
            r0 = pl.multiple_of(i * MEM_QB, MEM_QB)
            s = lax.dot_general(q_ref[pl.ds(r0, MEM_QB), :], k, NT, preferred_element_type=F32) * MEM_SCALE
            p = jnp.exp(s - jnp.max(s, axis=1, keepdims=True))
            p = p / jnp.sum(p, axis=1, keepdims=True)
            o_ref[pl.ds(r0, MEM_QB), :] = jnp.dot(p.astype(BF16), v, preferred_element_type=F32).astype(BF16)
            return carry

        lax.fori_loop(0, seq // MEM_QB, qblock, 0)

    return pl.pallas_call(
        body, name="mem_attn_fwd", grid=(bsz, 4),
        out_shape=jax.ShapeDtypeStruct((bsz, seq, 512), BF16),
        in_specs=[pl.BlockSpec((None, seq, BLK), lambda b, h: (b, 0, 12 + h)),
                  pl.BlockSpec((None, MEM_LEN, BLK), lambda b, h: (b, 0, h)),
                  pl.BlockSpec((None, MEM_LEN, BLK), lambda b, h: (b, 0, 4 + h))],
        out_specs=pl.BlockSpec((None, seq, BLK), lambda b, h: (b, 0, h)),
        compiler_params=_params("parallel", "parallel"),
    )(proj_a, kv, kv)


def _mem_bwd(proj_a, kv, d_o, d_proj_a, seq):
    bsz = proj_a.shape[0]

    def body(q_ref, k_ref, v_ref, do_ref, _, dq_ref, dk_ref, dv_ref):
        k, v = k_ref[...], v_ref[...]

        def qblock(i, carry):
            dk, dv = carry
            r0 = pl.multiple_of(i * MEM_QB, MEM_QB)
            q, do = q_ref[pl.ds(r0, MEM_QB), :], do_ref[pl.ds(r0, MEM_QB), :]
            s = lax.dot_general(q, k, NT, preferred_element_type=F32) * MEM_SCALE
            p = jnp.exp(s - jnp.max(s, axis=1, keepdims=True))
            p = p / jnp.sum(p, axis=1, keepdims=True)
            dp = lax.dot_general(do, v, NT, preferred_element_type=F32)
            ds = ((p * (dp - jnp.sum(p * dp, axis=1, keepdims=True))) * MEM_SCALE).astype(BF16)
            dq_ref[pl.ds(r0, MEM_QB), :] = jnp.dot(ds, k, preferred_element_type=F32).astype(BF16)
            dk = dk + lax.dot_general(ds, q, TN, preferred_element_type=F32)
            dv = dv + lax.dot_general(p.astype(BF16), do, TN, preferred_element_type=F32)
            return dk, dv

        zero = jnp.zeros((MEM_LEN, BLK), F32)
        dk, dv = lax.fori_loop(0, seq // MEM_QB, qblock, (zero, zero))
        dk_ref[...] = dk.astype(BF16)
        dv_ref[...] = dv.astype(BF16)

    kv_spec = pl.BlockSpec((None, MEM_LEN, BLK), lambda b, h: (b, 0, h))
    return pl.pallas_call(
        body, name="mem_attn_bwd", grid=(bsz, 4),
        out_shape=(jax.ShapeDtypeStruct((bsz, seq, WA), BF16), jax.ShapeDtypeStruct((bsz, MEM_LEN, 512), BF16),
                   jax.ShapeDtypeStruct((bsz, MEM_LEN, 512), BF16)),
        in_specs=[pl.BlockSpec((None, seq, BLK), lambda b, h: (b, 0, 12 + h)), kv_spec,
                  pl.BlockSpec((None, MEM_LEN, BLK), lambda b, h: (b, 0, 4 + h)),
                  pl.BlockSpec((None, seq, BLK), lambda b, h: (b, 0, h)), ANY],
        out_specs=(pl.BlockSpec((None, seq, BLK), lambda b, h: (b, 0, 12 + h)), kv_spec, kv_spec),
        input_output_aliases={4: 0},
        compiler_params=_params("parallel", "parallel"),
    )(proj_a, kv, kv, d_o, d_proj_a)


def _mesh_pos():
    return lax.axis_index("x"), lax.axis_index("y"), lax.axis_index("c")


def _all_gather(shard, name):
    m_per, n = shard.shape

    def body(x_ref, out_ref, send_sems, recv_sems, local_sem):
        x, y, c = _mesh_pos()
        me, sibling = (x, y, c), (x, y, 1 - c)
        chips = [(1 - x, y), (x, 1 - y), (1 - x, 1 - y)]

        def rows(px, py, pc):
            return out_ref.at[pl.ds((4 * px + 2 * py + pc) * m_per, m_per), :]

        def copy(k, block, to, src=None):
            return pltpu.make_async_remote_copy(
                src_ref=rows(*block) if src is None else src, dst_ref=rows(*block),
                send_sem=send_sems.at[k], recv_sem=recv_sems.at[k], device_id=to, device_id_type=MESH)

        mine = pltpu.make_async_copy(x_ref, rows(*me), local_sem)
        mine.start()
        first = [copy(0, me, sibling, src=x_ref)]
        first += [copy(1 + j, me, (*chip, c), src=x_ref) for j, chip in enumerate(chips)]
        for cp in first:
            cp.start()
        passed = [copy(4 + j, (*chip, c), sibling) for j, chip in enumerate(chips)]
        for j, chip in enumerate(chips):
            copy(1 + j, (*chip, c), me).wait_recv()
            passed[j].start()
        copy(0, sibling, me).wait_recv()
        for j, chip in enumerate(chips):
            copy(4 + j, (*chip, 1 - c), me).wait_recv()
        for cp in first + passed:
            cp.wait_send()
        mine.wait()

    return pl.pallas_call(
        body, name=name, out_shape=jax.ShapeDtypeStruct((N_DEV * m_per, n), shard.dtype),
        in_specs=[ANY], out_specs=ANY,
        scratch_shapes=[pltpu.SemaphoreType.DMA((7,)), pltpu.SemaphoreType.DMA((7,)), pltpu.SemaphoreType.DMA(())],
    )(shard)


EXCHANGE_SEMS = [pltpu.SemaphoreType.DMA((7,)), pltpu.SemaphoreType.DMA((7,)), pltpu.SemaphoreType.DMA(())]


def _direct_exchange(src_ref, dst_ref, send_sems, recv_sems, local_sem, gather):
    x, y, c = _mesh_pos()
    me = 4 * x + 2 * y + c
    own = pltpu.make_async_copy(src_ref if gather else src_ref.at[me], dst_ref.at[me], local_sem)
    sends, recvs = [], []
    for k in range(1, N_DEV):
        px = 1 - x if k & 4 else x
        py = 1 - y if k & 2 else y
        pc = 1 - c if k & 1 else c
        peer = 4 * px + 2 * py + pc
        sems = dict(send_sem=send_sems.at[k - 1], recv_sem=recv_sems.at[k - 1],
                    device_id=(px, py, pc), device_id_type=MESH)
        sends.append(pltpu.make_async_remote_copy(
            src_ref=src_ref if gather else src_ref.at[peer], dst_ref=dst_ref.at[me], **sems))
        recvs.append(pltpu.make_async_remote_copy(
            src_ref=src_ref if gather else src_ref.at[me], dst_ref=dst_ref.at[peer], **sems))

    def start():
        own.start()
        for cp in sends:
            cp.start()

    def wait():
        for cp in recvs:
            cp.wait_recv()
        for cp in sends:
            cp.wait_send()
        own.wait()

    return start, wait


def _riding_exchange(src_ref, dst_ref, sems, gather):
    start, wait = _direct_exchange(src_ref, dst_ref, *sems, gather)
    ids = [pl.program_id(a) for a in range(2)]
    last = [pl.num_programs(a) - 1 for a in range(2)]
    pl.when(jnp.logical_and(ids[0] == 0, ids[1] == 0))(start)
    return lambda: pl.when(jnp.logical_and(ids[0] == last[0], ids[1] == last[1]))(wait)


def _exchange(parts):
    n_rows, n = parts.shape[1:]

    def body(g_ref, out_ref, *sems):
        start, wait = _direct_exchange(g_ref, out_ref, *sems, gather=False)
        start()
        wait()

    return pl.pallas_call(
        body, name="grad_exchange", out_shape=jax.ShapeDtypeStruct((N_DEV, n_rows, n), parts.dtype),
        in_specs=[ANY], out_specs=ANY, scratch_shapes=EXCHANGE_SEMS,
    )(parts)


def _adamw(w, g, m, v):
    m = ADAM_B1 * m + (1.0 - ADAM_B1) * g
    v = ADAM_B2 * v + (1.0 - ADAM_B2) * (g * g)
    m_hat = m / (1.0 - ADAM_B1 ** ADAM_STEP)
    v_hat = v / (1.0 - ADAM_B2 ** ADAM_STEP)
    return -ADAM_LR * (m_hat / (jnp.sqrt(v_hat) + ADAM_EPS) + ADAM_WD * w), m, v


def _reduce_adamw(recv, w, m, v, name):
    rows = w.shape[0]
    tr = max(t for t in range(16, 353, 16) if rows % t == 0)

    def body(r_ref, w_ref, m_ref, v_ref, g_out, d_out, m_out, v_out):
        g = r_ref[0].astype(F32)
        for s in range(1, N_DEV):
            g = g + r_ref[s].astype(F32)
        g_out[...] = g
        d_out[...], m_out[...], v_out[...] = _adamw(w_ref[...], g, m_ref[...], v_ref[...])

    spec = pl.BlockSpec((tr, D), lambda i: (i, 0))
    return pl.pallas_call(
        body, name=name, grid=(rows // tr,),
        out_shape=[jax.ShapeDtypeStruct((rows, D), F32)] * 4,
        in_specs=[pl.BlockSpec((N_DEV, tr, D), lambda i: (0, i, 0)), spec, spec, spec],
        out_specs=[spec] * 4, compiler_params=_params("arbitrary"),
    )(recv, w, m, v)


def _small_adamw(gathered, w, m, v):
    def body(g_ref, w_ref, m_ref, v_ref, g_out, d_out, m_out, v_out, loss_out):
        tot = g_ref[0]
        for s in range(1, N_DEV):
            tot = tot + g_ref[s]
        g = tot[0:8]
        g_out[...] = g
        d_out[...], m_out[...], v_out[...] = _adamw(w_ref[...], g, m_ref[...], v_ref[...])
        loss_out[...] = jnp.broadcast_to((0.5 / D) * jnp.sum(tot[8:9], axis=1, keepdims=True), (8, BLK))

    out = [jax.ShapeDtypeStruct((8, D), F32)] * 4 + [jax.ShapeDtypeStruct((8, BLK), F32)]
    return pl.pallas_call(body, name="small_adamw", out_shape=out, compiler_params=_params())(gathered, w, m, v)


def _pack(ws, dtype):
    return jnp.concatenate([w.astype(dtype).reshape(-1, D) for w in ws], axis=0)


def _pick_chunks(w, chunks):
    return jnp.concatenate([w[:, BLK * c:BLK * (c + 1)] for c in chunks], axis=1)


def _shard_rows(shape):
    return (shape[1] * shape[2]) // D


def _full_weights(gathered, idxs, shard_shapes):
    out, off = {}, 0
    for i in idxs:
        k, n = shard_shapes[i][1:]
        part = gathered[:, off:off + _shard_rows(shard_shapes[i])]
        off += _shard_rows(shard_shapes[i])
        if BY_ROWS[i]:
            out[i] = part.reshape(N_DEV * k, n)
        else:
            out[i] = part.reshape(N_DEV, k, n).transpose(1, 0, 2).reshape(k, N_DEV * n)
    return out


def _grad_parts(grads, idxs, shard_shapes):
    parts = []
    for i in idxs:
        k, n = shard_shapes[i][1:]
        g = grads[i].astype(BF16)
        if not BY_ROWS[i]:
            g = g.reshape(k, N_DEV, n).transpose(1, 0, 2)
        parts.append(g.reshape(N_DEV, _shard_rows(shard_shapes[i]), D))
    return jnp.concatenate(parts, axis=1)


def _unpack_shards(packed, idxs, shard_shapes):
    out, off = {}, 0
    for i in idxs:
        rows = _shard_rows(shard_shapes[i])
        out[i] = packed[off:off + rows].reshape(shard_shapes[i])
        off += rows
    return out


def kernel(x, mem, g_pre_mix, g_post_mix, g_pre_ffn, g_post_ffn, g_mem, w_in, w_mem_kv, w_br_sb, w_br_dil, w_br_mem, w_gate, b_gate, w_o, w_ffn_in, w_ffn_out, loss_target, m_g_pre_mix, m_g_post_mix, m_g_pre_ffn, m_g_post_ffn, m_g_mem, m_w_in, m_w_mem_kv, m_w_br_sb, m_w_br_dil, m_w_br_mem, m_w_gate, m_b_gate, m_w_o, m_w_ffn_in, m_w_ffn_out, v_g_pre_mix, v_g_post_mix, v_g_pre_ffn, v_g_post_ffn, v_g_mem, v_w_in, v_w_mem_kv, v_w_br_sb, v_w_br_dil, v_w_br_mem, v_w_gate, v_b_gate, v_w_o, v_w_ffn_in, v_w_ffn_out):
    bsz, seq, _ = x.shape
    tokens = bsz * seq
    xf, tgt, memf = x.reshape(tokens, D), loss_target.reshape(tokens, D), mem.reshape(bsz * MEM_LEN, D)
    big_w = [w_in, w_mem_kv, w_br_sb, w_br_dil, w_br_mem, w_gate, w_o, w_ffn_in, w_ffn_out]
    big_m = [m_w_in, m_w_mem_kv, m_w_br_sb, m_w_br_dil, m_w_br_mem, m_w_gate, m_w_o, m_w_ffn_in, m_w_ffn_out]
    big_v = [v_w_in, v_w_mem_kv, v_w_br_sb, v_w_br_dil, v_w_br_mem, v_w_gate, v_w_o, v_w_ffn_in, v_w_ffn_out]
    shard_shapes = [w.shape for w in big_w]

    def gather_payload(idxs):
        return _pack([big_w[i][0] for i in idxs], BF16)

    first = _all_gather(gather_payload(GATHER_FIRST), "weight_all_gather")
    fw = _full_weights(first.reshape(N_DEV, -1, D), GATHER_FIRST, shard_shapes)
    fw_in, fw_gate = fw[0], fw[5]
    w_a, w_b = _pick_chunks(fw_in, CHUNKS_A), _pick_chunks(fw_in, CHUNKS_B)

    h = _norm_fwd(xf, g_pre_mix, "pre_mix_norm")
    proj_a = _matmul(h, w_a, "nn", BF16, "proj_a").reshape(bsz, seq, WA)
    proj_b = _matmul(h, w_b, "nn", BF16, "proj_b").reshape(bsz, seq, WB)
    gpre = _matmul(h, fw_gate, "nn", BF16, "gate_proj")
    o_a, o_a32, behind = _sb_fwd(proj_a, seq, gather_payload(GATHER_BEHIND))
    fw = _full_weights(behind, GATHER_BEHIND, shard_shapes)
    fw_mem_kv, fw_br_sb, fw_br_dil, fw_br_mem, fw_o, fw_ffn_in, fw_ffn_out = (fw[i] for i in GATHER_BEHIND)
    cos_t, sin_t = _rope_tables(seq)
    o_b, lse_b = _dil_fwd(proj_b, cos_t, sin_t, seq)
    mn = _norm_fwd(memf, g_mem, "mem_norm")
    kv = _matmul(mn, fw_mem_kv, "nn", BF16, "mem_kv_proj").reshape(bsz, MEM_LEN, D)
    o_c = _mem_fwd(proj_a, kv, seq)
    o_a2, o_b2, o_c2 = o_a.reshape(tokens, 512), o_b.reshape(tokens, 256), o_c.reshape(tokens, 512)
    ys = [_matmul(o_a2, fw_br_sb, "nn", BF16, "branch_sb"), _matmul(o_b2, fw_br_dil, "nn", BF16, "branch_dil"),
          _matmul(o_c2, fw_br_mem, "nn", BF16, "branch_mem")]
    merged = _gate_merge(gpre, ys, b_gate)
    mix = _matmul(merged, fw_o, "nn", F32, "out_proj")
    x1, h2 = _resid_norm_fwd(xf, mix, g_post_mix, g_pre_ffn)
    gu = _matmul(h2, fw_ffn_in, "nn", BF16, "ffn_in")
    f = _swiglu_fwd(gu)
    fo = _matmul(f, fw_ffn_out, "nn", F32, "ffn_out")
    dy, dfo, dg_post_ffn, loss_lanes = _loss_head(x1, fo, tgt, g_post_ffn)

    df = _matmul(dfo, fw_ffn_out, "nt", BF16, "d_ffn_act")
    gw_ffn_out = _matmul(f, dfo, "tn", F32, "gw_ffn_out")
    dgu = _swiglu_bwd(df, gu)
    dh2 = _matmul(dgu, fw_ffn_in, "nt", F32, "d_h2")
    gw_ffn_in = _matmul(h2, dgu, "tn", F32, "gw_ffn_in")
    dx1, dmix, dg_pre_ffn, dg_post_mix = _mid_norm_bwd(dh2, x1, dy, mix, g_pre_ffn, g_post_mix)
    dmerged = _matmul(dmix, fw_o, "nt", BF16, "d_merged")
    gw_o = _matmul(merged, dmix, "tn", F32, "gw_o")
    dya, dyb, dyc, dgpre, db_gate = _gate_bwd(dmerged, gpre, ys, b_gate)
    d_oa = _matmul(dya, fw_br_sb, "nt", BF16, "d_o_sb").reshape(bsz, seq, 512)
    d_ob = _matmul(dyb, fw_br_dil, "nt", BF16, "d_o_dil").reshape(bsz, seq, 256)
    d_oc = _matmul(dyc, fw_br_mem, "nt", BF16, "d_o_mem").reshape(bsz, seq, 512)
    gw_br_sb = _matmul(o_a2, dya, "tn", F32, "gw_br_sb")
    gw_br_dil = _matmul(o_b2, dyb, "tn", F32, "gw_br_dil")
    gw_br_mem = _matmul(o_c2, dyc, "tn", F32, "gw_br_mem")
    gw_gate = _matmul(h, dgpre, "tn", F32, "gw_gate")
    grads = {2: gw_br_sb, 3: gw_br_dil, 4: gw_br_mem, 5: gw_gate, 6: gw_o, 7: gw_ffn_in, 8: gw_ffn_out}
    d_proj_a, recv_behind = _sb_bwd(proj_a, d_oa, o_a32, seq, _grad_parts(grads, REDUCE_BEHIND, shard_shapes))
    d_proj_a, dk_m, dv_m = _mem_bwd(proj_a, kv, d_oc, d_proj_a, seq)
    d_proj_b = _dil_bwd(proj_b, cos_t, sin_t, d_ob, o_b, lse_b, seq).reshape(tokens, WB)
    d_proj_a = d_proj_a.reshape(tokens, WA)
    dhs = [_matmul(d_proj_a, w_a, "nt", F32, "d_h_a"), _matmul(d_proj_b, w_b, "nt", F32, "d_h_b"),
           _matmul(dgpre, fw_gate, "nt", F32, "d_h_gate")]
    gw_a = _matmul(h, d_proj_a, "tn", F32, "gw_in_a")
    gw_b = _matmul(h, d_proj_b, "tn", F32, "gw_in_b")
    dx, dg_pre_mix = _in_norm_bwd(dhs, xf, dx1, g_pre_mix)
    dkv = jnp.concatenate([dk_m, dv_m], axis=-1).reshape(bsz * MEM_LEN, D)
    gw_mem_kv = _matmul(mn, dkv, "tn", F32, "gw_mem_kv")
    dmn = _matmul(dkv, fw_mem_kv, "nt", F32, "d_mem_norm")
    dg_mem = _gain_grad(dmn, memf)

    gw_ab = jnp.concatenate([gw_a, gw_b], axis=1)
    where = {c: i for i, c in enumerate(CHUNKS_A + CHUNKS_B)}
    grads = {0: _pick_chunks(gw_ab, [where[c] for c in range(34)]), 1: gw_mem_kv}
    recv_last = _exchange(_grad_parts(grads, REDUCE_LAST, shard_shapes))
    big = [{}, {}, {}, {}]
    for recv, idxs, name in ((recv_behind, REDUCE_BEHIND, "reduce_adamw_behind"), (recv_last, REDUCE_LAST, "reduce_adamw_last")):
        packed = _reduce_adamw(recv, *(_pack([t[i][0] for i in idxs], F32) for t in (big_w, big_m, big_v)), name)
        for group, p in zip(big, packed):
            group.update(_unpack_shards(p, idxs, shard_shapes))
    big = [[group[i] for i in range(len(big_w))] for group in big]

    small = jnp.concatenate([dg_pre_mix, dg_post_mix, dg_pre_ffn, dg_post_ffn, dg_mem, db_gate.reshape(3, D),
                             loss_lanes, jnp.zeros((7, D), F32)], axis=0)
    small_all = _all_gather(small, "small_all_gather").reshape(N_DEV, 16, D)

    def small_pack(gs, b):
        return jnp.concatenate([*gs, b.reshape(3, D)], axis=0)

    sm = _small_adamw(
        small_all, small_pack([g_pre_mix, g_post_mix, g_pre_ffn, g_post_ffn, g_mem], b_gate),
        small_pack([m_g_pre_mix, m_g_post_mix, m_g_pre_ffn, m_g_post_ffn, m_g_mem], m_b_gate),
        small_pack([v_g_pre_mix, v_g_post_mix, v_g_pre_ffn, v_g_post_ffn, v_g_mem], v_b_gate))
    loss = sm[4][0, 0]

    def leaves(k):
        t, bw = sm[k], big[k]
        return [t[0:1], t[1:2], t[2:3], t[3:4], t[4:5], *bw[0:6], t[5:8].reshape(1, 3 * D), *bw[6:9]]

    return (loss, dx.reshape(bsz, seq, D), *leaves(0), *leaves(1), *leaves(2), *leaves(3))
```

```python
import functools

import jax
import jax.numpy as jnp
from jax import lax
from jax.experimental import pallas as pl
from jax.experimental.pallas import tpu as pltpu

F32 = jnp.float32
BF16 = jnp.bfloat16
D = 1024
BLK = 128
MEM_LEN = 256
D_FF = 2816
NORM_EPS = 1e-6
NEG_INF = -1e30
ROPE_THETA = 10000.0
ADAM_LR, ADAM_B1, ADAM_B2, ADAM_EPS, ADAM_WD, ADAM_STEP = 0.001, 0.9, 0.999, 1e-08, 0.01, 10
N_DEV = 8
VMEM_LIMIT_BYTES = 56 * 1024 * 1024
MESH = pl.DeviceIdType.MESH
ANY = pl.BlockSpec(memory_space=pl.ANY)

NT = (((1,), (1,)), ((), ()))
TN = (((0,), (0,)), ((), ()))
NN = (((1,), (0,)), ((), ()))
_DIMS = {"nn": NN, "nt": NT, "tn": TN}

BIG_NAMES = ("w_in", "w_mem_kv", "w_br_sb", "w_br_dil", "w_br_mem", "w_gate", "w_o", "w_ffn_in", "w_ffn_out")
BY_ROWS = (False, True, False, False, False, False, True, False, True)
GATHER_FIRST = (0, 5)
GATHER_BEHIND = (1, 2, 3, 4, 6, 7, 8)
REDUCE_BEHIND = (2, 3, 4, 5, 6, 7, 8)
REDUCE_LAST = (0, 1)

CHUNKS_A = tuple(c for hp in range(4) for c in (hp, 4 + hp, 8 + hp)) + (30, 31, 32, 33)
CHUNKS_B = tuple(c for hp in range(2) for g in range(3) for c in (12 + 6 * g + hp, 14 + 6 * g + hp, 16 + 6 * g + hp))
WA, WB = 128 * len(CHUNKS_A), 128 * len(CHUNKS_B)
DIL_GROUPS = (1, 4, 16)


def _params(*sem):
    return pltpu.CompilerParams(dimension_semantics=sem or None, vmem_limit_bytes=VMEM_LIMIT_BYTES)


def _tile(n, cap):
    if n <= 128:
        return n
    assert n % 128 == 0, n
    best = 128
    for t in range(128, min(n, cap) + 1, 128):
        if n % t == 0:
            best = t
    return best


def _matmul(a, b, mode, out_dtype, name, add=None, tm_cap=1536, tn_cap=1536, tk_cap=1536):
    if mode == "tn":
        (K, M), N = a.shape, b.shape[1]
    elif mode == "nt":
        (M, K), N = a.shape, b.shape[0]
    else:
        (M, K), N = a.shape, b.shape[1]
    tm, tn, tk = _tile(M, tm_cap), _tile(N, tn_cap), _tile(K, tk_cap)
    nm, nn, nk = M // tm, N // tn, K // tk
    dims = _DIMS[mode]
    n_add = 0 if add is None else 1

    def body(a_ref, b_ref, *rest):
        o_ref = rest[n_add]

        def finish(v):
            if n_add:
                v = v + rest[0][...]
            o_ref[...] = v.astype(o_ref.dtype)

        p = lax.dot_general(a_ref[...], b_ref[...], dims, preferred_element_type=F32)
        if nk == 1:
            finish(p)
        else:
            acc_ref = rest[n_add + 1]
            k = pl.program_id(2)

            @pl.when(k == 0)
            def _():
                acc_ref[...] = p

            @pl.when(k > 0)
            def _():
                acc_ref[...] += p

            @pl.when(k == nk - 1)
            def _():
                finish(acc_ref[...])

    n_outer = nk == 1 and (a.size * nn + b.size) < (a.size + b.size * nm)
    if n_outer:
        grid, ij = (nn, nm, nk), (lambda g0, g1: (g1, g0))
    else:
        grid, ij = (nm, nn, nk), (lambda g0, g1: (g0, g1))
    if mode == "tn":
        a_spec = pl.BlockSpec((tk, tm), lambda g0, g1, k: (k, ij(g0, g1)[0]))
    else:
        a_spec = pl.BlockSpec((tm, tk), lambda g0, g1, k: (ij(g0, g1)[0], k))
    if mode == "nt":
        b_spec = pl.BlockSpec((tn, tk), lambda g0, g1, k: (ij(g0, g1)[1], k))
    else:
        b_spec = pl.BlockSpec((tk, tn), lambda g0, g1, k: (k, ij(g0, g1)[1]))
    o_spec = pl.BlockSpec((tm, tn), lambda g0, g1, k: ij(g0, g1))
    return pl.pallas_call(
        body, name=name, grid=grid,
        out_shape=jax.ShapeDtypeStruct((M, N), out_dtype),
        in_specs=[a_spec, b_spec] + [o_spec] * n_add,
        out_specs=o_spec,
        scratch_shapes=[pltpu.VMEM((tm, tn), F32)] if nk > 1 else [],
        compiler_params=_params("parallel", "parallel", "arbitrary"),
    )(a, b, *([add] if n_add else []))


def _rowwise(body, name, rows, tr, row_ins, vec_ins, row_outs, acc_outs=()):
    tr = min(tr, rows)
    assert rows % tr == 0
    in_specs, args = [], []
    for r in row_ins:
        arr, w, cb = r if isinstance(r, tuple) else (r, r.shape[1], 0)
        in_specs.append(pl.BlockSpec((tr, w), functools.partial(lambda i, cb: (i, cb), cb=cb)))
        args.append(arr)
    for v in vec_ins:
        in_specs.append(pl.BlockSpec(v.shape, lambda i: (0, 0)))
        args.append(v)
    out_shape = [jax.ShapeDtypeStruct((rows, w), dt) for w, dt in row_outs]
    out_shape += [jax.ShapeDtypeStruct((1, w), F32) for w in acc_outs]
    out_specs = [pl.BlockSpec((tr, w), lambda i: (i, 0)) for w, _ in row_outs]
    out_specs += [pl.BlockSpec((1, w), lambda i: (0, 0)) for w in acc_outs]
    n_acc = len(acc_outs)

    def wrapped(*refs):
        if n_acc:
            @pl.when(pl.program_id(0) == 0)
            def _():
                for r in refs[len(refs) - n_acc:]:
                    r[...] = jnp.zeros_like(r)
        body(*refs)

    return pl.pallas_call(
        wrapped, name=name, grid=(rows // tr,), out_shape=out_shape, in_specs=in_specs, out_specs=out_specs,
        compiler_params=_params("arbitrary"),
    )(*args)


def _rstd(x):
    return lax.rsqrt(jnp.mean(x * x, axis=-1, keepdims=True) + NORM_EPS)


def _norm_bwd(u, n, r):
    return r * (u - n * jnp.mean(u * n, axis=-1, keepdims=True))


def _colsum(v):
    return jnp.sum(v, axis=0, keepdims=True)


def _norm_fwd(x, g, name):
    def body(x_ref, g_ref, h_ref):
        xv = x_ref[...]
        h_ref[...] = ((xv * _rstd(xv)) * g_ref[...]).astype(BF16)

    return _rowwise(body, name, x.shape[0], 512, [x], [g], [(D, BF16)])[0]


def _resid_norm_fwd(x, mix, g_post, g_pre):
    def body(x_ref, mix_ref, g2_ref, g3_ref, x1_ref, h2_ref):
        mv = mix_ref[...]
        x1 = x_ref[...] + (mv * _rstd(mv)) * g2_ref[...]
        x1_ref[...] = x1
        h2_ref[...] = ((x1 * _rstd(x1)) * g3_ref[...]).astype(BF16)

    return _rowwise(body, "resid_norm_fwd", x.shape[0], 512, [x, mix], [g_post, g_pre], [(D, F32), (D, BF16)])


def _gate_merge(gpre, ys, b_gate):
    def body(gp_ref, ya_ref, yb_ref, yc_ref, b_ref, m_ref):
        acc = None
        for k, y_ref in enumerate((ya_ref, yb_ref, yc_ref)):
            cols = slice(k * D, (k + 1) * D)
            gate = jax.nn.sigmoid(gp_ref[:, cols].astype(F32) + b_ref[:, cols])
            term = gate * y_ref[...].astype(F32)
            acc = term if acc is None else acc + term
        m_ref[...] = acc.astype(BF16)

    return _rowwise(body, "gate_merge", gpre.shape[0], 256, [gpre, *ys], [b_gate], [(D, BF16)])[0]


def _swiglu_fwd(gu):
    def body(a_ref, b_ref, f_ref):
        a = a_ref[...].astype(F32)
        f_ref[...] = (a * jax.nn.sigmoid(a) * b_ref[...].astype(F32)).astype(BF16)

    return _rowwise(body, "swiglu_fwd", gu.shape[0], 256, [(gu, D_FF, 0), (gu, D_FF, 1)], [], [(D_FF, BF16)])[0]


def _loss_head(x1, fo, tgt, g_post):
    def body(x1_ref, fo_ref, t_ref, g_ref, dy_ref, dfo_ref, dg_ref, loss_ref):
        fo_v = fo_ref[...]
        r = _rstd(fo_v)
        n = fo_v * r
        err = (x1_ref[...] + n * g_ref[...]) - t_ref[...]
        loss_ref[...] += _colsum(err * err)
        dy = err * (1.0 / D)
        dy_ref[...] = dy
        dg_ref[...] += _colsum(dy * n)
        dfo_ref[...] = _norm_bwd(dy * g_ref[...], n, r).astype(BF16)

    return _rowwise(body, "loss_head", x1.shape[0], 256, [x1, fo, tgt], [g_post], [(D, F32), (D, BF16)], (D, D))


def _swiglu_bwd(df, gu):
    def body(df_ref, gu_ref, dgu_ref):
        a = gu_ref[:, :D_FF].astype(F32)
        b = gu_ref[:, D_FF:].astype(F32)
        d = df_ref[...].astype(F32)
        s = jax.nn.sigmoid(a)
        dgu_ref[:, :D_FF] = (d * b * (s * (1.0 + a * (1.0 - s)))).astype(BF16)
        dgu_ref[:, D_FF:] = (d * (a * s)).astype(BF16)

    return _rowwise(body, "swiglu_bwd", df.shape[0], 256, [df, gu], [], [(2 * D_FF, BF16)])[0]


def _mid_norm_bwd(dh2, x1, dy, mix, g_pre, g_post):
    def body(dh_ref, x1_ref, dy_ref, mix_ref, g3_ref, g2_ref, dx1_ref, dmix_ref, dg3_ref, dg2_ref):
        x1v = x1_ref[...]
        r3 = _rstd(x1v)
        n3 = x1v * r3
        dh = dh_ref[...]
        dg3_ref[...] += _colsum(dh * n3)
        dx1 = dy_ref[...] + _norm_bwd(dh * g3_ref[...], n3, r3)
        dx1_ref[...] = dx1
        mv = mix_ref[...]
        r2 = _rstd(mv)
        n2 = mv * r2
        dg2_ref[...] += _colsum(dx1 * n2)
        dmix_ref[...] = _norm_bwd(dx1 * g2_ref[...], n2, r2).astype(BF16)

    return _rowwise(body, "mid_norm_bwd", x1.shape[0], 256, [dh2, x1, dy, mix], [g_pre, g_post],
                    [(D, F32), (D, BF16)], (D, D))


def _gate_bwd(dmerged, gpre, ys, b_gate):
    def body(dm_ref, gp_ref, ya_ref, yb_ref, yc_ref, b_ref, dya_ref, dyb_ref, dyc_ref, dgp_ref, db_ref):
        dm = dm_ref[...].astype(F32)
        for k, (y_ref, dy_ref) in enumerate(((ya_ref, dya_ref), (yb_ref, dyb_ref), (yc_ref, dyc_ref))):
            cols = slice(k * D, (k + 1) * D)
            gate = jax.nn.sigmoid(gp_ref[:, cols].astype(F32) + b_ref[:, cols])
            dy_ref[...] = (dm * gate).astype(BF16)
            dgp = (dm * y_ref[...].astype(F32)) * (gate * (1.0 - gate))
            dgp_ref[:, cols] = dgp.astype(BF16)
            db_ref[:, cols] += _colsum(dgp)

    return _rowwise(body, "gate_bwd", gpre.shape[0], 256, [dmerged, gpre, *ys], [b_gate],
                    [(D, BF16), (D, BF16), (D, BF16), (3 * D, BF16)], (3 * D,))


def _in_norm_bwd(dh, x, dx1, g_pre):
    def body(dh_ref, x_ref, dx1_ref, g_ref, dx_ref, dg_ref):
        dh = dh_ref[...]
        xv = x_ref[...]
        r = _rstd(xv)
        n = xv * r
        dg_ref[...] += _colsum(dh * n)
        dx_ref[...] = dx1_ref[...] + _norm_bwd(dh * g_ref[...], n, r)

    return _rowwise(body, "in_norm_bwd", x.shape[0], 256, [dh, x, dx1], [g_pre], [(D, F32)], (D,))


def _gain_grad(dmn, mem):
    def body(d_ref, m_ref, dg_ref):
        mv = m_ref[...]
        dg_ref[...] += _colsum(d_ref[...] * (mv * _rstd(mv)))

    return _rowwise(body, "mem_gain_grad", mem.shape[0], 256, [dmn, mem], [], [], (D,))[0]


def _head_rowsum(v, head0):
    return (jnp.sum(jnp.where(head0, v, 0.0), axis=1, keepdims=True),
            jnp.sum(jnp.where(head0, 0.0, v), axis=1, keepdims=True))


KT = 256
SB_SCALE = 0.125


def _make_suffix():
    tri = (lax.broadcasted_iota(jnp.int32, (KT, KT), 0) > lax.broadcasted_iota(jnp.int32, (KT, KT), 1)).astype(BF16)
    tri2 = jnp.concatenate([tri, tri], axis=0)

    def suffix(x):
        hi = x.astype(BF16)
        lo = (x - hi.astype(F32)).astype(BF16)
        return jnp.dot(jnp.concatenate([hi, lo], axis=1), tri2, preferred_element_type=F32)

    return suffix


def _sb_scores(qh, k, mask, suffix, run):
    z = lax.dot_general(qh, k, NT, preferred_element_type=F32)
    lb = jnp.minimum(z, 0.0) - jnp.log1p(jnp.exp(-jnp.abs(z)))
    lk = lb - z
    if mask is not None:
        lk = jnp.where(mask, lk, 0.0)
    a = jnp.exp(lb + suffix(lk) + run)
    if mask is not None:
        a = jnp.where(mask, a, 0.0)
    return lb, lk, a


QB = KT


def _sb_tiles(i, tile, init):
    st = tile(i, init, True)
    st = lax.fori_loop(0, lax.shift_right_logical(i, 1),
                       lambda t, s: tile(i - 2 - 2 * t, tile(i - 1 - 2 * t, s, False), False), st)
    return lax.cond((i & 1) == 1, lambda s: tile(0, s, False), lambda s: s, st)


def _sb_consts():
    head0 = lax.broadcasted_iota(jnp.int32, (QB, BLK), 1) < 64
    row = lax.broadcasted_iota(jnp.int32, (2 * QB, KT), 0) & (QB - 1)
    return head0, row > lax.broadcasted_iota(jnp.int32, (2 * QB, KT), 1)


def _stack_heads(v, head0):
    zero = jnp.zeros_like(v)
    return jnp.concatenate([jnp.where(head0, v, zero), jnp.where(head0, zero, v)], axis=0)


def _unstack_heads(v, head0):
    n = v.shape[0] // 2
    return jnp.where(head0, v[:n], v[n:])


def _sb_fwd(proj_a, seq, ride):
    bsz = proj_a.shape[0]

    def body(x_ref, ride_ref, o_ref, o32_ref, gathered_ref, acc_ref, *sems):
        finish_ride = _riding_exchange(ride_ref, gathered_ref, sems, gather=True)
        head0, diag_mask = _sb_consts()
        suffix = _make_suffix()

        def qblock(i, carry):
            r0 = pl.multiple_of(i * QB, QB)
            qs = _stack_heads(x_ref[pl.ds(r0, QB), 0:128] * jnp.asarray(SB_SCALE, BF16), head0)

            def tile(jt, run, masked):
                c0 = pl.multiple_of(jt * KT, KT)
                k = x_ref[pl.ds(c0, KT), 128:256]
                v = x_ref[pl.ds(c0, KT), 256:384]
                _, lk, a = _sb_scores(qs, k, diag_mask if masked else None, suffix, run)
                pv = jnp.dot(a.astype(BF16), v, preferred_element_type=F32)
                if masked:
                    acc_ref[...] = pv
                else:
                    acc_ref[...] += pv
                return run + jnp.sum(lk, axis=1, keepdims=True)

            _sb_tiles(i, tile, jnp.zeros((2 * QB, 1), F32))
            o = _unstack_heads(acc_ref[...], head0)
            o32_ref[pl.ds(r0, QB), :] = o
            o_ref[pl.ds(r0, QB), :] = o.astype(BF16)
            return carry

        lax.fori_loop(0, seq // QB, qblock, 0)
        finish_ride()

    out_spec = pl.BlockSpec((None, seq, BLK), lambda b, hp: (b, 0, hp))
    return pl.pallas_call(
        body, name="sb_attn_fwd", grid=(bsz, 4),
        out_shape=(jax.ShapeDtypeStruct((bsz, seq, 512), BF16), jax.ShapeDtypeStruct((bsz, seq, 512), F32),
                   jax.ShapeDtypeStruct((N_DEV, *ride.shape), ride.dtype)),
        in_specs=[pl.BlockSpec((None, seq, 384), lambda b, hp: (b, 0, hp)), ANY],
        out_specs=(out_spec, out_spec, ANY),
        scratch_shapes=[pltpu.VMEM((2 * QB, BLK), F32), *EXCHANGE_SEMS],
        compiler_params=_params("arbitrary", "arbitrary"),
    )(proj_a, ride)


def _sb_bwd(proj_a, d_o, o_a, seq, ride):
    bsz = proj_a.shape[0]

    def body(x_ref, do_ref, o_ref, ride_ref, d_ref, received_ref, dq_acc, dk_acc, dv_acc, *sems):
        finish_ride = _riding_exchange(ride_ref, received_ref, sems, gather=False)
        head0, diag_mask = _sb_consts()
        suffix = _make_suffix()
        dk_acc[...] = jnp.zeros_like(dk_acc)
        dv_acc[...] = jnp.zeros_like(dv_acc)

        def qblock(i, carry):
            r0 = pl.multiple_of(i * QB, QB)
            qs = _stack_heads(x_ref[pl.ds(r0, QB), 0:128] * jnp.asarray(SB_SCALE, BF16), head0)
            do = do_ref[pl.ds(r0, QB), :]
            dos = _stack_heads(do, head0)
            dsum = jnp.concatenate(_head_rowsum(do.astype(F32) * o_ref[pl.ds(r0, QB), :], head0), axis=0)

            def tile(jt, st, masked):
                run, grun = st
                c0 = pl.multiple_of(jt * KT, KT)
                k = x_ref[pl.ds(c0, KT), 128:256]
                v = x_ref[pl.ds(c0, KT), 256:384]
                lb, lk, a = _sb_scores(qs, k, diag_mask if masked else None, suffix, run)
                a16 = a.astype(BF16)
                g = a16.astype(F32) * lax.dot_general(dos, v, NT, preferred_element_type=F32)
                before = dsum - ((grun + suffix(g)) + g)
                dz = g - jnp.exp(lb) * (g + before)
                if masked:
                    dz = jnp.where(diag_mask, dz, 0.0)
                dz = dz.astype(BF16)
                dq = jnp.dot(dz, k, preferred_element_type=F32)
                if masked:
                    dq_acc[...] = dq
                else:
                    dq_acc[...] += dq
                dk_acc[pl.ds(c0, KT), :] += lax.dot_general(dz, qs, TN, preferred_element_type=F32)
                dv_acc[pl.ds(c0, KT), :] += lax.dot_general(a16, dos, TN, preferred_element_type=F32)
                return run + jnp.sum(lk, axis=1, keepdims=True), grun + jnp.sum(g, axis=1, keepdims=True)

            z1 = jnp.zeros((2 * QB, 1), F32)
            _sb_tiles(i, tile, (z1, z1))
            d_ref[pl.ds(r0, QB), 0:128] = (_unstack_heads(dq_acc[...], head0) * SB_SCALE).astype(BF16)
            return carry

        lax.fori_loop(0, seq // QB, qblock, 0)
        d_ref[:, 128:256] = dk_acc[...].astype(BF16)
        d_ref[:, 256:384] = dv_acc[...].astype(BF16)
        finish_ride()

    return pl.pallas_call(
        body, name="sb_attn_bwd", grid=(bsz, 4),
        out_shape=(jax.ShapeDtypeStruct((bsz, seq, WA), BF16), jax.ShapeDtypeStruct(ride.shape, ride.dtype)),
        in_specs=[pl.BlockSpec((None, seq, 384), lambda b, hp: (b, 0, hp)),
                  pl.BlockSpec((None, seq, BLK), lambda b, hp: (b, 0, hp)),
                  pl.BlockSpec((None, seq, BLK), lambda b, hp: (b, 0, hp)), ANY],
        out_specs=(pl.BlockSpec((None, seq, 384), lambda b, hp: (b, 0, hp)), ANY),
        scratch_shapes=[pltpu.VMEM((2 * QB, BLK), F32), pltpu.VMEM((seq, BLK), F32), pltpu.VMEM((seq, BLK), F32),
                        *EXCHANGE_SEMS],
        compiler_params=_params("arbitrary", "arbitrary"),
    )(proj_a, d_o, o_a, ride)


def _rope_tables(seq):
    inv_freq = ROPE_THETA ** (-jnp.arange(32, dtype=F32) * 2.0 / 64)
    ang = jnp.arange(seq).astype(F32)[:, None] * inv_freq[None, :]
    cos, sin = jnp.cos(ang), jnp.sin(ang)
    return jnp.tile(cos, (1, 4)), jnp.concatenate([-sin, sin, -sin, sin], axis=1)


def _make_rope(n_rows):
    lane = lax.broadcasted_iota(jnp.int32, (n_rows, BLK), 1)
    first = (lane & 63) < 32

    def rope(x, cos, sin):
        partner = jnp.where(first, pltpu.roll(x, 96, 1), pltpu.roll(x, 32, 1))
        return x * cos + partner * sin

    return rope


DIL_UNROLL = 4


def _dil_consts():
    head0 = lax.broadcasted_iota(jnp.int32, (BLK, BLK), 1) < 64
    row = lax.broadcasted_iota(jnp.int32, (2 * BLK, 2 * BLK), 0) & (BLK - 1)
    col = lax.broadcasted_iota(jnp.int32, (2 * BLK, 2 * BLK), 1)
    valid_prev = jnp.logical_and(col < BLK, col >= row)
    valid_cur = jnp.logical_and(col >= BLK, row >= col - BLK)
    return head0, valid_prev, valid_cur


def _dil_blocks(dil, seq, block):
    nq = seq // dil // BLK

    def rows(r, i):
        if dil == 1:
            return pl.ds(pl.multiple_of(i * BLK, BLK), BLK)
        return pl.ds(r + (dil * BLK) * i, BLK, stride=dil)

    def step(t, carry):
        for u in range(DIL_UNROLL):
            n = t * DIL_UNROLL + u
            r, i = lax.div(n, nq), lax.rem(n, nq)
            block(rows(r, i), rows(r, jnp.maximum(i - 1, 0)), i)
        return carry

    lax.fori_loop(0, seq // BLK // DIL_UNROLL, step, 0)


def _dil_scores(qf, kf, vf, cur, prev, i, consts):
    head0, valid_prev, valid_cur = consts
    qs = _stack_heads(qf[cur, :].astype(BF16), head0)
    kcat = jnp.concatenate([kf[prev, :], kf[cur, :]], axis=0).astype(BF16)
    vcat = jnp.concatenate([vf[prev, :], vf[cur, :]], axis=0).astype(BF16)
    valid = jnp.logical_or(valid_cur, jnp.logical_and(valid_prev, i > 0))
    s = lax.dot_general(qs, kcat, NT, preferred_element_type=F32) * 0.125
    return qs, kcat, vcat, s, valid


def _head_cols(v):
    return jnp.concatenate([v[:, 0:1], v[:, 64:65]], axis=0)


def _dil_load_qkv(x_ref, c, rope, cos, sin, qf, kf, vf):
    qf[...] = rope(x_ref[:, c:c + 128].astype(F32), cos, sin).astype(BF16).astype(F32)
    kf[...] = rope(x_ref[:, c + 128:c + 256].astype(F32), cos, sin).astype(BF16).astype(F32)
    vf[...] = x_ref[:, c + 256:c + 384].astype(F32)


def _dil_fwd(proj_b, cos_t, sin_t, seq):
    bsz = proj_b.shape[0]

    def body(x_ref, cos_ref, sin_ref, ob_ref, lse_ref, qf, kf, vf, og, lg):
        consts = _dil_consts()
        head0 = consts[0]
        rope = _make_rope(seq)
        cos, sin = cos_ref[...], sin_ref[...]
        for g, dil in enumerate(DIL_GROUPS):
            _dil_load_qkv(x_ref, 384 * g, rope, cos, sin, qf, kf, vf)

            def block(cur, prev, i, g=g):
                _, _, vcat, s, valid = _dil_scores(qf, kf, vf, cur, prev, i, consts)
                s = jnp.where(valid, s, NEG_INF)
                m = jnp.max(s, axis=1, keepdims=True)
                p = jnp.exp(s - m)
                den = jnp.sum(p, axis=1, keepdims=True)
                o = jnp.dot(p.astype(BF16), vcat, preferred_element_type=F32) / den
                og[g, cur, :] = _unstack_heads(o, head0)
                lg[g, cur, :] = _unstack_heads(jnp.broadcast_to(m + jnp.log(den), (2 * BLK, BLK)), head0)

            _dil_blocks(dil, seq, block)
        ls = [lg[0], lg[1], lg[2]]
        m = jnp.maximum(jnp.maximum(ls[0], ls[1]), ls[2])
        ws = [jnp.exp(l - m) for l in ls]
        den = (ws[0] + ws[1]) + ws[2]
        ob_ref[...] = (((ws[0] * og[0] + ws[1] * og[1]) + ws[2] * og[2]) / den).astype(BF16)
        lse_ref[...] = m + jnp.log(den)

    tab_spec = pl.BlockSpec((seq, BLK), lambda b, hp: (0, 0))
    out_spec = pl.BlockSpec((None, seq, BLK), lambda b, hp: (b, 0, hp))
    slab = pltpu.VMEM((seq, BLK), F32)
    return pl.pallas_call(
        body, name="dil_attn_fwd", grid=(bsz, 2),
        out_shape=(jax.ShapeDtypeStruct((bsz, seq, 256), BF16), jax.ShapeDtypeStruct((bsz, seq, 256), F32)),
        in_specs=[pl.BlockSpec((None, seq, WB // 2), lambda b, hp: (b, 0, hp)), tab_spec, tab_spec],
        out_specs=(out_spec, out_spec),
        scratch_shapes=[slab, slab, slab, pltpu.VMEM((3, seq, BLK), F32), pltpu.VMEM((3, seq, BLK), F32)],
        compiler_params=_params("parallel", "parallel"),
    )(proj_b, cos_t, sin_t)


def _dil_bwd(proj_b, cos_t, sin_t, d_ob, o_b, lse, seq):
    bsz = proj_b.shape[0]

    def body(x_ref, cos_ref, sin_ref, do_ref, ob_ref, lse_ref, d_ref, qf, kf, vf, dof, dsf, dq_s, dk_acc, dv_acc):
        consts = _dil_consts()
        head0 = consts[0]
        rope = _make_rope(seq)
        cos, sin = cos_ref[...], sin_ref[...]
        do_all = do_ref[...].astype(F32)
        dof[...] = do_all
        head0_all = lax.broadcasted_iota(jnp.int32, (seq, BLK), 1) < 64
        d0, d1 = _head_rowsum(do_all * ob_ref[...].astype(F32), head0_all)
        dsf[...] = jnp.where(head0_all, d0, d1)
        for g, dil in enumerate(DIL_GROUPS):
            _dil_load_qkv(x_ref, 384 * g, rope, cos, sin, qf, kf, vf)
            dk_acc[...] = jnp.zeros_like(dk_acc)
            dv_acc[...] = jnp.zeros_like(dv_acc)

            def block(cur, prev, i):
                qs, kcat, vcat, s, valid = _dil_scores(qf, kf, vf, cur, prev, i, consts)
                dos = _stack_heads(dof[cur, :].astype(BF16), head0)
                p = jnp.where(valid, jnp.exp(s - _head_cols(lse_ref[cur, :])), 0.0)
                dp = lax.dot_general(dos, vcat, NT, preferred_element_type=F32)
                ds = ((p * (dp - _head_cols(dsf[cur, :]))) * 0.125).astype(BF16)
                dq_s[cur, :] = _unstack_heads(jnp.dot(ds, kcat, preferred_element_type=F32), head0)
                dk = lax.dot_general(ds, qs, TN, preferred_element_type=F32)
                dv = lax.dot_general(p.astype(BF16), dos, TN, preferred_element_type=F32)
                dk_acc[prev, :] += dk[:BLK]
                dk_acc[cur, :] += dk[BLK:]
                dv_acc[prev, :] += dv[:BLK]
                dv_acc[cur, :] += dv[BLK:]

            _dil_blocks(dil, seq, block)
            c = 384 * g
            d_ref[:, c:c + 128] = rope(dq_s[...], cos, -sin).astype(BF16)
            d_ref[:, c + 128:c + 256] = rope(dk_acc[...], cos, -sin).astype(BF16)
            d_ref[:, c + 256:c + 384] = dv_acc[...].astype(BF16)

    x_spec = pl.BlockSpec((None, seq, WB // 2), lambda b, hp: (b, 0, hp))
    tab_spec = pl.BlockSpec((seq, BLK), lambda b, hp: (0, 0))
    tok_spec = pl.BlockSpec((None, seq, BLK), lambda b, hp: (b, 0, hp))
    return pl.pallas_call(
        body, name="dil_attn_bwd", grid=(bsz, 2),
        out_shape=jax.ShapeDtypeStruct((bsz, seq, WB), BF16),
        in_specs=[x_spec, tab_spec, tab_spec, tok_spec, tok_spec, tok_spec], out_specs=x_spec,
        scratch_shapes=[pltpu.VMEM((seq, BLK), F32)] * 8,
        compiler_params=_params("parallel", "parallel"),
    )(proj_b, cos_t, sin_t, d_ob, o_b, lse)


MEM_SCALE = 128 ** -0.5
MEM_QB = 1024


def _mem_fwd(proj_a, kv, seq):
    bsz = proj_a.shape[0]

    def body(q_ref, k_ref, v_ref, o_ref):
        k, v = k_ref[...], v_ref[...]

        def qblock(i, carry):
            r0 = pl.multiple_of(i * MEM_QB, MEM_QB)
            s = lax.dot_general(q_ref[pl.ds(r0, MEM_QB), :], k, NT, preferred_element_type=F32) * MEM_SCALE
            p = jnp.exp(s - jnp.max(s, axis=1, keepdims=True))
            p = p / jnp.sum(p, axis=1, keepdims=True)
            o_ref[pl.ds(r0, MEM_QB), :] = jnp.dot(p.astype(BF16), v, preferred_element_type=F32).astype(BF16)
            return carry

        lax.fori_loop(0, seq // MEM_QB, qblock, 0)

    return pl.pallas_call(
        body, name="mem_attn_fwd", grid=(bsz, 4),
        out_shape=jax.ShapeDtypeStruct((bsz, seq, 512), BF16),
        in_specs=[pl.BlockSpec((None, seq, BLK), lambda b, h: (b, 0, 12 + h)),
                  pl.BlockSpec((None, MEM_LEN, BLK), lambda b, h: (b, 0, h)),
                  pl.BlockSpec((None, MEM_LEN, BLK), lambda b, h: (b, 0, 4 + h))],
        out_specs=pl.BlockSpec((None, seq, BLK), lambda b, h: (b, 0, h)),
        compiler_params=_params("parallel", "parallel"),
    )(proj_a, kv, kv)


def _mem_bwd(proj_a, kv, d_o, d_proj_a, seq):
    bsz = proj_a.shape[0]

    def body(q_ref, k_ref, v_ref, do_ref, _, dq_ref, dk_ref, dv_ref):
        k, v = k_ref[...], v_ref[...]

        def qblock(i, carry):
            dk, dv = carry
            r0 = pl.multiple_of(i * MEM_QB, MEM_QB)
            q, do = q_ref[pl.ds(r0, MEM_QB), :], do_ref[pl.ds(r0, MEM_QB), :]
            s = lax.dot_general(q, k, NT, preferred_element_type=F32) * MEM_SCALE
            p = jnp.exp(s - jnp.max(s, axis=1, keepdims=True))
            p = p / jnp.sum(p, axis=1, keepdims=True)
            dp = lax.dot_general(do, v, NT, preferred_element_type=F32)
            ds = ((p * (dp - jnp.sum(p * dp, axis=1, keepdims=True))) * MEM_SCALE).astype(BF16)
            dq_ref[pl.ds(r0, MEM_QB), :] = jnp.dot(ds, k, preferred_element_type=F32).astype(BF16)
            dk = dk + lax.dot_general(ds, q, TN, preferred_element_type=F32)
            dv = dv + lax.dot_general(p.astype(BF16), do, TN, preferred_element_type=F32)
            return dk, dv

        zero = jnp.zeros((MEM_LEN, BLK), F32)
        dk, dv = lax.fori_loop(0, seq // MEM_QB, qblock, (zero, zero))
        dk_ref[...] = dk.astype(BF16)
        dv_ref[...] = dv.astype(BF16)

    kv_spec = pl.BlockSpec((None, MEM_LEN, BLK), lambda b, h: (b, 0, h))
    return pl.pallas_call(
        body, name="mem_attn_bwd", grid=(bsz, 4),
        out_shape=(jax.ShapeDtypeStruct((bsz, seq, WA), BF16), jax.ShapeDtypeStruct((bsz, MEM_LEN, 512), BF16),
                   jax.ShapeDtypeStruct((bsz, MEM_LEN, 512), BF16)),
        in_specs=[pl.BlockSpec((None, seq, BLK), lambda b, h: (b, 0, 12 + h)), kv_spec,
                  pl.BlockSpec((None, MEM_LEN, BLK), lambda b, h: (b, 0, 4 + h)),
                  pl.BlockSpec((None, seq, BLK), lambda b, h: (b, 0, h)), ANY],
        out_specs=(pl.BlockSpec((None, seq, BLK), lambda b, h: (b, 0, 12 + h)), kv_spec, kv_spec),
        input_output_aliases={4: 0},
        compiler_params=_params("parallel", "parallel"),
    )(proj_a, kv, kv, d_o, d_proj_a)


def _mesh_pos():
    return lax.axis_index("x"), lax.axis_index("y"), lax.axis_index("c")


def _all_gather(shard, name):
    m_per, n = shard.shape

    def body(x_ref, out_ref, send_sems, recv_sems, local_sem):
        x, y, c = _mesh_pos()
        me, sibling = (x, y, c), (x, y, 1 - c)
        chips = [(1 - x, y), (x, 1 - y), (1 - x, 1 - y)]

        def rows(px, py, pc):
            return out_ref.at[pl.ds((4 * px + 2 * py + pc) * m_per, m_per), :]

        def copy(k, block, to, src=None):
            return pltpu.make_async_remote_copy(
                src_ref=rows(*block) if src is None else src, dst_ref=rows(*block),
                send_sem=send_sems.at[k], recv_sem=recv_sems.at[k], device_id=to, device_id_type=MESH)

        mine = pltpu.make_async_copy(x_ref, rows(*me), local_sem)
        mine.start()
        first = [copy(0, me, sibling, src=x_ref)]
        first += [copy(1 + j, me, (*chip, c), src=x_ref) for j, chip in enumerate(chips)]
        for cp in first:
            cp.start()
        passed = [copy(4 + j, (*chip, c), sibling) for j, chip in enumerate(chips)]
        for j, chip in enumerate(chips):
            copy(1 + j, (*chip, c), me).wait_recv()
            passed[j].start()
        copy(0, sibling, me).wait_recv()
        for j, chip in enumerate(chips):
            copy(4 + j, (*chip, 1 - c), me).wait_recv()
        for cp in first + passed:
            cp.wait_send()
        mine.wait()

    return pl.pallas_call(
        body, name=name, out_shape=jax.ShapeDtypeStruct((N_DEV * m_per, n), shard.dtype),
        in_specs=[ANY], out_specs=ANY,
        scratch_shapes=[pltpu.SemaphoreType.DMA((7,)), pltpu.SemaphoreType.DMA((7,)), pltpu.SemaphoreType.DMA(())],
    )(shard)


EXCHANGE_SEMS = [pltpu.SemaphoreType.DMA((7,)), pltpu.SemaphoreType.DMA((7,)), pltpu.SemaphoreType.DMA(())]


def _direct_exchange(src_ref, dst_ref, send_sems, recv_sems, local_sem, gather):
    x, y, c = _mesh_pos()
    me = 4 * x + 2 * y + c
    own = pltpu.make_async_copy(src_ref if gather else src_ref.at[me], dst_ref.at[me], local_sem)
    sends, recvs = [], []
    for k in range(1, N_DEV):
        px = 1 - x if k & 4 else x
        py = 1 - y if k & 2 else y
        pc = 1 - c if k & 1 else c
        peer = 4 * px + 2 * py + pc
        sems = dict(send_sem=send_sems.at[k - 1], recv_sem=recv_sems.at[k - 1],
                    device_id=(px, py, pc), device_id_type=MESH)
        sends.append(pltpu.make_async_remote_copy(
            src_ref=src_ref if gather else src_ref.at[peer], dst_ref=dst_ref.at[me], **sems))
        recvs.append(pltpu.make_async_remote_copy(
            src_ref=src_ref if gather else src_ref.at[me], dst_ref=dst_ref.at[peer], **sems))

    def start():
        own.start()
        for cp in sends:
            cp.start()

    def wait():
        for cp in recvs:
            cp.wait_recv()
        for cp in sends:
            cp.wait_send()
        own.wait()

    return start, wait


def _riding_exchange(src_ref, dst_ref, sems, gather):
    start, wait = _direct_exchange(src_ref, dst_ref, *sems, gather)
    ids = [pl.program_id(a) for a in range(2)]
    last = [pl.num_programs(a) - 1 for a in range(2)]
    pl.when(jnp.logical_and(ids[0] == 0, ids[1] == 0))(start)
    return lambda: pl.when(jnp.logical_and(ids[0] == last[0], ids[1] == last[1]))(wait)


def _exchange(parts):
    n_rows, n = parts.shape[1:]

    def body(g_ref, out_ref, *sems):
        start, wait = _direct_exchange(g_ref, out_ref, *sems, gather=False)
        start()
        wait()

    return pl.pallas_call(
        body, name="grad_exchange", out_shape=jax.ShapeDtypeStruct((N_DEV, n_rows, n), parts.dtype),
        in_specs=[ANY], out_specs=ANY, scratch_shapes=EXCHANGE_SEMS,
    )(parts)


def _adamw(w, g, m, v):
    m = ADAM_B1 * m + (1.0 - ADAM_B1) * g
    v = ADAM_B2 * v + (1.0 - ADAM_B2) * (g * g)
    m_hat = m / (1.0 - ADAM_B1 ** ADAM_STEP)
    v_hat = v / (1.0 - ADAM_B2 ** ADAM_STEP)
    return -ADAM_LR * (m_hat / (jnp.sqrt(v_hat) + ADAM_EPS) + ADAM_WD * w), m, v


def _reduce_adamw(recv, w, m, v, name):
    rows = w.shape[0]
    tr = max(t for t in range(16, 353, 16) if rows % t == 0)

    def body(r_ref, w_ref, m_ref, v_ref, g_out, d_out, m_out, v_out):
        g = r_ref[0].astype(F32)
        for s in range(1, N_DEV):
            g = g + r_ref[s].astype(F32)
        g_out[...] = g
        d_out[...], m_out[...], v_out[...] = _adamw(w_ref[...], g, m_ref[...], v_ref[...])

    spec = pl.BlockSpec((tr, D), lambda i: (i, 0))
    return pl.pallas_call(
        body, name=name, grid=(rows // tr,),
        out_shape=[jax.ShapeDtypeStruct((rows, D), F32)] * 4,
        in_specs=[pl.BlockSpec((N_DEV, tr, D), lambda i: (0, i, 0)), spec, spec, spec],
        out_specs=[spec] * 4, compiler_params=_params("arbitrary"),
    )(recv, w, m, v)


def _small_adamw(gathered, w, m, v):
    def body(g_ref, w_ref, m_ref, v_ref, g_out, d_out, m_out, v_out, loss_out):
        tot = g_ref[0]
        for s in range(1, N_DEV):
            tot = tot + g_ref[s]
        g = tot[0:8]
        g_out[...] = g
        d_out[...], m_out[...], v_out[...] = _adamw(w_ref[...], g, m_ref[...], v_ref[...])
        loss_out[...] = jnp.broadcast_to((0.5 / D) * jnp.sum(tot[8:9], axis=1, keepdims=True), (8, BLK))

    out = [jax.ShapeDtypeStruct((8, D), F32)] * 4 + [jax.ShapeDtypeStruct((8, BLK), F32)]
    return pl.pallas_call(body, name="small_adamw", out_shape=out, compiler_params=_params())(gathered, w, m, v)


def _pack(ws, dtype):
    return jnp.concatenate([w.astype(dtype).reshape(-1, D) for w in ws], axis=0)


def _pick_chunks(w, chunks):
    return jnp.concatenate([w[:, BLK * c:BLK * (c + 1)] for c in chunks], axis=1)


def _shard_rows(shape):
    return (shape[1] * shape[2]) // D


def _full_weights(gathered, idxs, shard_shapes):
    out, off = {}, 0
    for i in idxs:
        k, n = shard_shapes[i][1:]
        part = gathered[:, off:off + _shard_rows(shard_shapes[i])]
        off += _shard_rows(shard_shapes[i])
        if BY_ROWS[i]:
            out[i] = part.reshape(N_DEV * k, n)
        else:
            out[i] = part.reshape(N_DEV, k, n).transpose(1, 0, 2).reshape(k, N_DEV * n)
    return out


def _grad_parts(grads, idxs, shard_shapes):
    parts = []
    for i in idxs:
        k, n = shard_shapes[i][1:]
        g = grads[i].astype(BF16)
        if not BY_ROWS[i]:
            g = g.reshape(k, N_DEV, n).transpose(1, 0, 2)
        parts.append(g.reshape(N_DEV, _shard_rows(shard_shapes[i]), D))
    return jnp.concatenate(parts, axis=1)


def _unpack_shards(packed, idxs, shard_shapes):
    out, off = {}, 0
    for i in idxs:
        rows = _shard_rows(shard_shapes[i])
        out[i] = packed[off:off + rows].reshape(shard_shapes[i])
        off += rows
    return out


def kernel(x, mem, g_pre_mix, g_post_mix, g_pre_ffn, g_post_ffn, g_mem, w_in, w_mem_kv, w_br_sb, w_br_dil, w_br_mem, w_gate, b_gate, w_o, w_ffn_in, w_ffn_out, loss_target, m_g_pre_mix, m_g_post_mix, m_g_pre_ffn, m_g_post_ffn, m_g_mem, m_w_in, m_w_mem_kv, m_w_br_sb, m_w_br_dil, m_w_br_mem, m_w_gate, m_b_gate, m_w_o, m_w_ffn_in, m_w_ffn_out, v_g_pre_mix, v_g_post_mix, v_g_pre_ffn, v_g_post_ffn, v_g_mem, v_w_in, v_w_mem_kv, v_w_br_sb, v_w_br_dil, v_w_br_mem, v_w_gate, v_b_gate, v_w_o, v_w_ffn_in, v_w_ffn_out):
    bsz, seq, _ = x.shape
    tokens = bsz * seq
    xf, tgt, memf = x.reshape(tokens, D), loss_target.reshape(tokens, D), mem.reshape(bsz * MEM_LEN, D)
    big_w = [w_in, w_mem_kv, w_br_sb, w_br_dil, w_br_mem, w_gate, w_o, w_ffn_in, w_ffn_out]
    big_m = [m_w_in, m_w_mem_kv, m_w_br_sb, m_w_br_dil, m_w_br_mem, m_w_gate, m_w_o, m_w_ffn_in, m_w_ffn_out]
    big_v = [v_w_in, v_w_mem_kv, v_w_br_sb, v_w_br_dil, v_w_br_mem, v_w_gate, v_w_o, v_w_ffn_in, v_w_ffn_out]
    shard_shapes = [w.shape for w in big_w]

    def gather_payload(idxs):
        return _pack([big_w[i][0] for i in idxs], BF16)

    first = _all_gather(gather_payload(GATHER_FIRST), "weight_all_gather")
    fw = _full_weights(first.reshape(N_DEV, -1, D), GATHER_FIRST, shard_shapes)
    fw_in, fw_gate = fw[0], fw[5]
    w_a, w_b = _pick_chunks(fw_in, CHUNKS_A), _pick_chunks(fw_in, CHUNKS_B)

    h = _norm_fwd(xf, g_pre_mix, "pre_mix_norm")
    proj_a = _matmul(h, w_a, "nn", BF16, "proj_a").reshape(bsz, seq, WA)
    proj_b = _matmul(h, w_b, "nn", BF16, "proj_b").reshape(bsz, seq, WB)
    gpre = _matmul(h, fw_gate, "nn", BF16, "gate_proj")
    o_a, o_a32, behind = _sb_fwd(proj_a, seq, gather_payload(GATHER_BEHIND))
    fw = _full_weights(behind, GATHER_BEHIND, shard_shapes)
    fw_mem_kv, fw_br_sb, fw_br_dil, fw_br_mem, fw_o, fw_ffn_in, fw_ffn_out = (fw[i] for i in GATHER_BEHIND)
    cos_t, sin_t = _rope_tables(seq)
    o_b, lse_b = _dil_fwd(proj_b, cos_t, sin_t, seq)
    mn = _norm_fwd(memf, g_mem, "mem_norm")
    kv = _matmul(mn, fw_mem_kv, "nn", BF16, "mem_kv_proj").reshape(bsz, MEM_LEN, D)
    o_c = _mem_fwd(proj_a, kv, seq)
    o_a2, o_b2, o_c2 = o_a.reshape(tokens, 512), o_b.reshape(tokens, 256), o_c.reshape(tokens, 512)
    ys = [_matmul(o_a2, fw_br_sb, "nn", BF16, "branch_sb"), _matmul(o_b2, fw_br_dil, "nn", BF16, "branch_dil"),
          _matmul(o_c2, fw_br_mem, "nn", BF16, "branch_mem")]
    merged = _gate_merge(gpre, ys, b_gate)
    mix = _matmul(merged, fw_o, "nn", F32, "out_proj")
    x1, h2 = _resid_norm_fwd(xf, mix, g_post_mix, g_pre_ffn)
    gu = _matmul(h2, fw_ffn_in, "nn", BF16, "ffn_in")
    f = _swiglu_fwd(gu)
    fo = _matmul(f, fw_ffn_out, "nn", F32, "ffn_out")
    dy, dfo, dg_post_ffn, loss_lanes = _loss_head(x1, fo, tgt, g_post_ffn)

    df = _matmul(dfo, fw_ffn_out, "nt", BF16, "d_ffn_act")
    gw_ffn_out = _matmul(f, dfo, "tn", F32, "gw_ffn_out")
    dgu = _swiglu_bwd(df, gu)
    dh2 = _matmul(dgu, fw_ffn_in, "nt", F32, "d_h2")
    gw_ffn_in = _matmul(h2, dgu, "tn", F32, "gw_ffn_in")
    dx1, dmix, dg_pre_ffn, dg_post_mix = _mid_norm_bwd(dh2, x1, dy, mix, g_pre_ffn, g_post_mix)
    dmerged = _matmul(dmix, fw_o, "nt", BF16, "d_merged")
    gw_o = _matmul(merged, dmix, "tn", F32, "gw_o")
    dya, dyb, dyc, dgpre, db_gate = _gate_bwd(dmerged, gpre, ys, b_gate)
    d_oa = _matmul(dya, fw_br_sb, "nt", BF16, "d_o_sb").reshape(bsz, seq, 512)
    d_ob = _matmul(dyb, fw_br_dil, "nt", BF16, "d_o_dil").reshape(bsz, seq, 256)
    d_oc = _matmul(dyc, fw_br_mem, "nt", BF16, "d_o_mem").reshape(bsz, seq, 512)
    gw_br_sb = _matmul(o_a2, dya, "tn", F32, "gw_br_sb")
    gw_br_dil = _matmul(o_b2, dyb, "tn", F32, "gw_br_dil")
    gw_br_mem = _matmul(o_c2, dyc, "tn", F32, "gw_br_mem")
    gw_gate = _matmul(h, dgpre, "tn", F32, "gw_gate")
    grads = {2: gw_br_sb, 3: gw_br_dil, 4: gw_br_mem, 5: gw_gate, 6: gw_o, 7: gw_ffn_in, 8: gw_ffn_out}
    d_proj_a, recv_behind = _sb_bwd(proj_a, d_oa, o_a32, seq, _grad_parts(grads, REDUCE_BEHIND, shard_shapes))
    d_proj_a, dk_m, dv_m = _mem_bwd(proj_a, kv, d_oc, d_proj_a, seq)
    d_proj_b = _dil_bwd(proj_b, cos_t, sin_t, d_ob, o_b, lse_b, seq).reshape(tokens, WB)
    d_proj_a = d_proj_a.reshape(tokens, WA)
    dh = _matmul(dgpre, fw_gate, "nt", F32, "d_h_gate")
    dh = _matmul(d_proj_a, w_a, "nt", F32, "d_h_a", add=dh)
    dh = _matmul(d_proj_b, w_b, "nt", F32, "d_h_b", add=dh)
    gw_a = _matmul(h, d_proj_a, "tn", F32, "gw_in_a")
    gw_b = _matmul(h, d_proj_b, "tn", F32, "gw_in_b")
    dx, dg_pre_mix = _in_norm_bwd(dh, xf, dx1, g_pre_mix)
    dkv = jnp.concatenate([dk_m, dv_m], axis=-1).reshape(bsz * MEM_LEN, D)
    gw_mem_kv = _matmul(mn, dkv, "tn", F32, "gw_mem_kv")
    dmn = _matmul(dkv, fw_mem_kv, "nt", F32, "d_mem_norm")
    dg_mem = _gain_grad(dmn, memf)

    gw_ab = jnp.concatenate([gw_a, gw_b], axis=1)
    where = {c: i for i, c in enumerate(CHUNKS_A + CHUNKS_B)}
    grads = {0: _pick_chunks(gw_ab, [where[c] for c in range(34)]), 1: gw_mem_kv}
    recv_last = _exchange(_grad_parts(grads, REDUCE_LAST, shard_shapes))
    big = [{}, {}, {}, {}]
    for recv, idxs, name in ((recv_behind, REDUCE_BEHIND, "reduce_adamw_behind"), (recv_last, REDUCE_LAST, "reduce_adamw_last")):
        packed = _reduce_adamw(recv, *(_pack([t[i][0] for i in idxs], F32) for t in (big_w, big_m, big_v)), name)
        for group, p in zip(big, packed):
            group.update(_unpack_shards(p, idxs, shard_shapes))
    big = [[group[i] for i in range(len(big_w))] for group in big]

    small = jnp.concatenate([dg_pre_mix, dg_post_mix, dg_pre_ffn, dg_post_ffn, dg_mem, db_gate.reshape(3, D),
                             loss_lanes, jnp.zeros((7, D), F32)], axis=0)
    small_all = _all_gather(small, "small_all_gather").reshape(N_DEV, 16, D)

    def small_pack(gs, b):
        return jnp.concatenate([*gs, b.reshape(3, D)], axis=0)

    sm = _small_adamw(
        small_all, small_pack([g_pre_mix, g_post_mix, g_pre_ffn, g_post_ffn, g_mem], b_gate),
        small_pack([m_g_pre_mix, m_g_post_mix, m_g_pre_ffn, m_g_post_ffn, m_g_mem], m_b_gate),
        small_pack([v_g_pre_mix, v_g_post_mix, v_g_pre_ffn, v_g_post_ffn, v_g_mem], v_b_gate))
    loss = sm[4][0, 0]

    def leaves(k):
        t, bw = sm[k], big[k]
        return [t[0:1], t[1:2], t[2:3], t[3:4], t[4:5], *bw[0:6], t[5:8].reshape(1, 3 * D), *bw[6:9]]

    return (loss, dx.reshape(bsz, seq, D), *leaves(0), *leaves(1), *leaves(2), *leaves(3))
```

```python
import functools

import jax
import jax.numpy as jnp
from jax import lax
from jax.experimental import pallas as pl
from jax.experimental.pallas import tpu as pltpu

F32 = jnp.float32
BF16 = jnp.bfloat16
D = 1024
BLK = 128
MEM_LEN = 256
D_FF = 2816
NORM_EPS = 1e-6
NEG_INF = -1e30
ROPE_THETA = 10000.0
ADAM_LR, ADAM_B1, ADAM_B2, ADAM_EPS, ADAM_WD, ADAM_STEP = 0.001, 0.9, 0.999, 1e-08, 0.01, 10
N_DEV = 8
VMEM_LIMIT_BYTES = 56 * 1024 * 1024
MESH = pl.DeviceIdType.MESH
ANY = pl.BlockSpec(memory_space=pl.ANY)

NT = (((1,), (1,)), ((), ()))
TN = (((0,), (0,)), ((), ()))
NN = (((1,), (0,)), ((), ()))
_DIMS = {"nn": NN, "nt": NT, "tn": TN}

BIG_NAMES = ("w_in", "w_mem_kv", "w_br_sb", "w_br_dil", "w_br_mem", "w_gate", "w_o", "w_ffn_in", "w_ffn_out")
BY_ROWS = (False, True, False, False, False, False, True, False, True)
GATHER_FIRST = (0,)
GATHER_BEHIND = (1, 2, 3, 4, 5, 6, 7, 8)
REDUCE_BEHIND = (2, 3, 4, 5, 6, 7, 8)
REDUCE_LAST = (0, 1)

CHUNKS_A = tuple(c for hp in range(4) for c in (hp, 4 + hp, 8 + hp)) + (30, 31, 32, 33)
CHUNKS_B = tuple(c for hp in range(2) for g in range(3) for c in (12 + 6 * g + hp, 14 + 6 * g + hp, 16 + 6 * g + hp))
WA, WB = 128 * len(CHUNKS_A), 128 * len(CHUNKS_B)
DIL_GROUPS = (1, 4, 16)


def _params(*sem):
    return pltpu.CompilerParams(dimension_semantics=sem or None, vmem_limit_bytes=VMEM_LIMIT_BYTES)


def _tile(n, cap):
    if n <= 128:
        return n
    assert n % 128 == 0, n
    best = 128
    for t in range(128, min(n, cap) + 1, 128):
        if n % t == 0:
            best = t
    return best


def _matmul(a, b, mode, out_dtype, name, add=None, tm_cap=1536, tn_cap=1536, tk_cap=1536):
    if mode == "tn":
        (K, M), N = a.shape, b.shape[1]
    elif mode == "nt":
        (M, K), N = a.shape, b.shape[0]
    else:
        (M, K), N = a.shape, b.shape[1]
    tm, tn, tk = _tile(M, tm_cap), _tile(N, tn_cap), _tile(K, tk_cap)
    nm, nn, nk = M // tm, N // tn, K // tk
    dims = _DIMS[mode]
    n_add = 0 if add is None else 1

    def body(a_ref, b_ref, *rest):
        o_ref = rest[n_add]

        def finish(v):
            if n_add:
                v = v + rest[0][...]
            o_ref[...] = v.astype(o_ref.dtype)

        p = lax.dot_general(a_ref[...], b_ref[...], dims, preferred_element_type=F32)
        if nk == 1:
            finish(p)
        else:
            acc_ref = rest[n_add + 1]
            k = pl.program_id(2)

            @pl.when(k == 0)
            def _():
                acc_ref[...] = p

            @pl.when(k > 0)
            def _():
                acc_ref[...] += p

            @pl.when(k == nk - 1)
            def _():
                finish(acc_ref[...])

    n_outer = nk == 1 and (a.size * nn + b.size) < (a.size + b.size * nm)
    if n_outer:
        grid, ij = (nn, nm, nk), (lambda g0, g1: (g1, g0))
    else:
        grid, ij = (nm, nn, nk), (lambda g0, g1: (g0, g1))
    if mode == "tn":
        a_spec = pl.BlockSpec((tk, tm), lambda g0, g1, k: (k, ij(g0, g1)[0]))
    else:
        a_spec = pl.BlockSpec((tm, tk), lambda g0, g1, k: (ij(g0, g1)[0], k))
    if mode == "nt":
        b_spec = pl.BlockSpec((tn, tk), lambda g0, g1, k: (ij(g0, g1)[1], k))
    else:
        b_spec = pl.BlockSpec((tk, tn), lambda g0, g1, k: (k, ij(g0, g1)[1]))
    o_spec = pl.BlockSpec((tm, tn), lambda g0, g1, k: ij(g0, g1))
    return pl.pallas_call(
        body, name=name, grid=grid,
        out_shape=jax.ShapeDtypeStruct((M, N), out_dtype),
        in_specs=[a_spec, b_spec] + [o_spec] * n_add,
        out_specs=o_spec,
        scratch_shapes=[pltpu.VMEM((tm, tn), F32)] if nk > 1 else [],
        compiler_params=_params("parallel", "parallel", "arbitrary"),
    )(a, b, *([add] if n_add else []))


def _rowwise(body, name, rows, tr, row_ins, vec_ins, row_outs, acc_outs=()):
    tr = min(tr, rows)
    assert rows % tr == 0
    in_specs, args = [], []
    for r in row_ins:
        arr, w, cb = r if isinstance(r, tuple) else (r, r.shape[1], 0)
        in_specs.append(pl.BlockSpec((tr, w), functools.partial(lambda i, cb: (i, cb), cb=cb)))
        args.append(arr)
    for v in vec_ins:
        in_specs.append(pl.BlockSpec(v.shape, lambda i: (0, 0)))
        args.append(v)
    out_shape = [jax.ShapeDtypeStruct((rows, w), dt) for w, dt in row_outs]
    out_shape += [jax.ShapeDtypeStruct((1, w), F32) for w in acc_outs]
    out_specs = [pl.BlockSpec((tr, w), lambda i: (i, 0)) for w, _ in row_outs]
    out_specs += [pl.BlockSpec((1, w), lambda i: (0, 0)) for w in acc_outs]
    n_acc = len(acc_outs)

    def wrapped(*refs):
        if n_acc:
            @pl.when(pl.program_id(0) == 0)
            def _():
                for r in refs[len(refs) - n_acc:]:
                    r[...] = jnp.zeros_like(r)
        body(*refs)

    return pl.pallas_call(
        wrapped, name=name, grid=(rows // tr,), out_shape=out_shape, in_specs=in_specs, out_specs=out_specs,
        compiler_params=_params("arbitrary"),
    )(*args)


def _rstd(x):
    return lax.rsqrt(jnp.mean(x * x, axis=-1, keepdims=True) + NORM_EPS)


def _norm_bwd(u, n, r):
    return r * (u - n * jnp.mean(u * n, axis=-1, keepdims=True))


def _colsum(v):
    return jnp.sum(v, axis=0, keepdims=True)


def _norm_fwd(x, g, name):
    def body(x_ref, g_ref, h_ref):
        xv = x_ref[...]
        h_ref[...] = ((xv * _rstd(xv)) * g_ref[...]).astype(BF16)

    return _rowwise(body, name, x.shape[0], 512, [x], [g], [(D, BF16)])[0]


def _resid_norm_fwd(x, mix, g_post, g_pre):
    def body(x_ref, mix_ref, g2_ref, g3_ref, x1_ref, h2_ref):
        mv = mix_ref[...]
        x1 = x_ref[...] + (mv * _rstd(mv)) * g2_ref[...]
        x1_ref[...] = x1
        h2_ref[...] = ((x1 * _rstd(x1)) * g3_ref[...]).astype(BF16)

    return _rowwise(body, "resid_norm_fwd", x.shape[0], 512, [x, mix], [g_post, g_pre], [(D, F32), (D, BF16)])


def _gate_merge(gpre, ys, b_gate):
    def body(gp_ref, ya_ref, yb_ref, yc_ref, b_ref, m_ref):
        acc = None
        for k, y_ref in enumerate((ya_ref, yb_ref, yc_ref)):
            cols = slice(k * D, (k + 1) * D)
            gate = jax.nn.sigmoid(gp_ref[:, cols].astype(F32) + b_ref[:, cols])
            term = gate * y_ref[...].astype(F32)
            acc = term if acc is None else acc + term
        m_ref[...] = acc.astype(BF16)

    return _rowwise(body, "gate_merge", gpre.shape[0], 256, [gpre, *ys], [b_gate], [(D, BF16)])[0]


def _swiglu_fwd(gu):
    def body(a_ref, b_ref, f_ref):
        a = a_ref[...].astype(F32)
        f_ref[...] = (a * jax.nn.sigmoid(a) * b_ref[...].astype(F32)).astype(BF16)

    return _rowwise(body, "swiglu_fwd", gu.shape[0], 256, [(gu, D_FF, 0), (gu, D_FF, 1)], [], [(D_FF, BF16)])[0]


def _loss_head(x1, fo, tgt, g_post):
    def body(x1_ref, fo_ref, t_ref, g_ref, dy_ref, dfo_ref, dg_ref, loss_ref):
        fo_v = fo_ref[...]
        r = _rstd(fo_v)
        n = fo_v * r
        err = (x1_ref[...] + n * g_ref[...]) - t_ref[...]
        loss_ref[...] += _colsum(err * err)
        dy = err * (1.0 / D)
        dy_ref[...] = dy
        dg_ref[...] += _colsum(dy * n)
        dfo_ref[...] = _norm_bwd(dy * g_ref[...], n, r).astype(BF16)

    return _rowwise(body, "loss_head", x1.shape[0], 256, [x1, fo, tgt], [g_post], [(D, F32), (D, BF16)], (D, D))


def _swiglu_bwd(df, gu):
    def body(df_ref, gu_ref, dgu_ref):
        a = gu_ref[:, :D_FF].astype(F32)
        b = gu_ref[:, D_FF:].astype(F32)
        d = df_ref[...].astype(F32)
        s = jax.nn.sigmoid(a)
        dgu_ref[:, :D_FF] = (d * b * (s * (1.0 + a * (1.0 - s)))).astype(BF16)
        dgu_ref[:, D_FF:] = (d * (a * s)).astype(BF16)

    return _rowwise(body, "swiglu_bwd", df.shape[0], 256, [df, gu], [], [(2 * D_FF, BF16)])[0]


def _mid_norm_bwd(dh2, x1, dy, mix, g_pre, g_post):
    def body(dh_ref, x1_ref, dy_ref, mix_ref, g3_ref, g2_ref, dx1_ref, dmix_ref, dg3_ref, dg2_ref):
        x1v = x1_ref[...]
        r3 = _rstd(x1v)
        n3 = x1v * r3
        dh = dh_ref[...]
        dg3_ref[...] += _colsum(dh * n3)
        dx1 = dy_ref[...] + _norm_bwd(dh * g3_ref[...], n3, r3)
        dx1_ref[...] = dx1
        mv = mix_ref[...]
        r2 = _rstd(mv)
        n2 = mv * r2
        dg2_ref[...] += _colsum(dx1 * n2)
        dmix_ref[...] = _norm_bwd(dx1 * g2_ref[...], n2, r2).astype(BF16)

    return _rowwise(body, "mid_norm_bwd", x1.shape[0], 256, [dh2, x1, dy, mix], [g_pre, g_post],
                    [(D, F32), (D, BF16)], (D, D))


def _gate_bwd(dmerged, gpre, ys, b_gate):
    def body(dm_ref, gp_ref, ya_ref, yb_ref, yc_ref, b_ref, dya_ref, dyb_ref, dyc_ref, dgp_ref, db_ref):
        dm = dm_ref[...].astype(F32)
        for k, (y_ref, dy_ref) in enumerate(((ya_ref, dya_ref), (yb_ref, dyb_ref), (yc_ref, dyc_ref))):
            cols = slice(k * D, (k + 1) * D)
            gate = jax.nn.sigmoid(gp_ref[:, cols].astype(F32) + b_ref[:, cols])
            dy_ref[...] = (dm * gate).astype(BF16)
            dgp = (dm * y_ref[...].astype(F32)) * (gate * (1.0 - gate))
            dgp_ref[:, cols] = dgp.astype(BF16)
            db_ref[:, cols] += _colsum(dgp)

    return _rowwise(body, "gate_bwd", gpre.shape[0], 256, [dmerged, gpre, *ys], [b_gate],
                    [(D, BF16), (D, BF16), (D, BF16), (3 * D, BF16)], (3 * D,))


def _in_norm_bwd(dh, x, dx1, g_pre):
    def body(dh_ref, x_ref, dx1_ref, g_ref, dx_ref, dg_ref):
        dh = dh_ref[...]
        xv = x_ref[...]
        r = _rstd(xv)
        n = xv * r
        dg_ref[...] += _colsum(dh * n)
        dx_ref[...] = dx1_ref[...] + _norm_bwd(dh * g_ref[...], n, r)

    return _rowwise(body, "in_norm_bwd", x.shape[0], 256, [dh, x, dx1], [g_pre], [(D, F32)], (D,))


def _gain_grad(dmn, mem):
    def body(d_ref, m_ref, dg_ref):
        mv = m_ref[...]
        dg_ref[...] += _colsum(d_ref[...] * (mv * _rstd(mv)))

    return _rowwise(body, "mem_gain_grad", mem.shape[0], 256, [dmn, mem], [], [], (D,))[0]


def _head_rowsum(v, head0):
    return (jnp.sum(jnp.where(head0, v, 0.0), axis=1, keepdims=True),
            jnp.sum(jnp.where(head0, 0.0, v), axis=1, keepdims=True))


KT = 256
SB_SCALE = 0.125


def _make_suffix():
    tri = (lax.broadcasted_iota(jnp.int32, (KT, KT), 0) > lax.broadcasted_iota(jnp.int32, (KT, KT), 1)).astype(BF16)
    tri2 = jnp.concatenate([tri, tri], axis=0)

    def suffix(x):
        hi = x.astype(BF16)
        lo = (x - hi.astype(F32)).astype(BF16)
        return jnp.dot(jnp.concatenate([hi, lo], axis=1), tri2, preferred_element_type=F32)

    return suffix


def _sb_scores(qh, k, mask, suffix, run):
    z = lax.dot_general(qh, k, NT, preferred_element_type=F32)
    zc = jnp.minimum(z, 60.0)
    sp = jnp.log(1.0 + jnp.exp(zc))
    lb = zc - sp
    lk = -sp
    if mask is not None:
        lk = jnp.where(mask, lk, 0.0)
    a = jnp.exp(lb + suffix(lk) + run)
    if mask is not None:
        a = jnp.where(mask, a, 0.0)
    return lb, lk, a


QB = KT


def _sb_tiles(i, tile, init):
    st = tile(i, init, True)
    st = lax.fori_loop(0, lax.shift_right_logical(i, 1),
                       lambda t, s: tile(i - 2 - 2 * t, tile(i - 1 - 2 * t, s, False), False), st)
    return lax.cond((i & 1) == 1, lambda s: tile(0, s, False), lambda s: s, st)


def _sb_consts():
    head0 = lax.broadcasted_iota(jnp.int32, (QB, BLK), 1) < 64
    row = lax.broadcasted_iota(jnp.int32, (2 * QB, KT), 0) & (QB - 1)
    return head0, row > lax.broadcasted_iota(jnp.int32, (2 * QB, KT), 1)


def _stack_heads(v, head0):
    zero = jnp.zeros_like(v)
    return jnp.concatenate([jnp.where(head0, v, zero), jnp.where(head0, zero, v)], axis=0)


def _unstack_heads(v, head0):
    n = v.shape[0] // 2
    return jnp.where(head0, v[:n], v[n:])


def _sb_fwd(proj_a, seq, ride):
    bsz = proj_a.shape[0]

    def body(x_ref, ride_ref, o_ref, o32_ref, gathered_ref, acc_ref, *sems):
        finish_ride = _riding_exchange(ride_ref, gathered_ref, sems, gather=True)
        head0, diag_mask = _sb_consts()
        suffix = _make_suffix()

        def qblock(i, carry):
            r0 = pl.multiple_of(i * QB, QB)
            qs = _stack_heads(x_ref[pl.ds(r0, QB), 0:128] * jnp.asarray(SB_SCALE, BF16), head0)

            def tile(jt, run, masked):
                c0 = pl.multiple_of(jt * KT, KT)
                k = x_ref[pl.ds(c0, KT), 128:256]
                v = x_ref[pl.ds(c0, KT), 256:384]
                _, lk, a = _sb_scores(qs, k, diag_mask if masked else None, suffix, run)
                pv = jnp.dot(a.astype(BF16), v, preferred_element_type=F32)
                if masked:
                    acc_ref[...] = pv
                else:
                    acc_ref[...] += pv
                return run + jnp.sum(lk, axis=1, keepdims=True)

            _sb_tiles(i, tile, jnp.zeros((2 * QB, 1), F32))
            o = _unstack_heads(acc_ref[...], head0)
            o32_ref[pl.ds(r0, QB), :] = o
            o_ref[pl.ds(r0, QB), :] = o.astype(BF16)
            return carry

        lax.fori_loop(0, seq // QB, qblock, 0)
        finish_ride()

    out_spec = pl.BlockSpec((None, seq, BLK), lambda b, hp: (b, 0, hp))
    return pl.pallas_call(
        body, name="sb_attn_fwd", grid=(bsz, 4),
        out_shape=(jax.ShapeDtypeStruct((bsz, seq, 512), BF16), jax.ShapeDtypeStruct((bsz, seq, 512), F32),
                   jax.ShapeDtypeStruct((N_DEV, *ride.shape), ride.dtype)),
        in_specs=[pl.BlockSpec((None, seq, 384), lambda b, hp: (b, 0, hp)), ANY],
        out_specs=(out_spec, out_spec, ANY),
        scratch_shapes=[pltpu.VMEM((2 * QB, BLK), F32), *EXCHANGE_SEMS],
        compiler_params=_params("arbitrary", "arbitrary"),
    )(proj_a, ride)


def _sb_bwd(proj_a, d_o, o_a, seq, ride):
    bsz = proj_a.shape[0]

    def body(x_ref, do_ref, o_ref, ride_ref, d_ref, received_ref, dq_acc, dk_acc, dv_acc, *sems):
        finish_ride = _riding_exchange(ride_ref, received_ref, sems, gather=False)
        head0, diag_mask = _sb_consts()
        suffix = _make_suffix()
        dk_acc[...] = jnp.zeros_like(dk_acc)
        dv_acc[...] = jnp.zeros_like(dv_acc)

        def qblock(i, carry):
            r0 = pl.multiple_of(i * QB, QB)
            qs = _stack_heads(x_ref[pl.ds(r0, QB), 0:128] * jnp.asarray(SB_SCALE, BF16), head0)
            do = do_ref[pl.ds(r0, QB), :]
            dos = _stack_heads(do, head0)
            dsum = jnp.concatenate(_head_rowsum(do.astype(F32) * o_ref[pl.ds(r0, QB), :], head0), axis=0)

            def tile(jt, st, masked):
                run, grun = st
                c0 = pl.multiple_of(jt * KT, KT)
                k = x_ref[pl.ds(c0, KT), 128:256]
                v = x_ref[pl.ds(c0, KT), 256:384]
                lb, lk, a = _sb_scores(qs, k, diag_mask if masked else None, suffix, run)
                a16 = a.astype(BF16)
                g = a16.astype(F32) * lax.dot_general(dos, v, NT, preferred_element_type=F32)
                before = dsum - ((grun + suffix(g)) + g)
                dz = g - jnp.exp(lb) * (g + before)
                if masked:
                    dz = jnp.where(diag_mask, dz, 0.0)
                dz = dz.astype(BF16)
                dq = jnp.dot(dz, k, preferred_element_type=F32)
                if masked:
                    dq_acc[...] = dq
                else:
                    dq_acc[...] += dq
                dk_acc[pl.ds(c0, KT), :] += lax.dot_general(dz, qs, TN, preferred_element_type=F32)
                dv_acc[pl.ds(c0, KT), :] += lax.dot_general(a16, dos, TN, preferred_element_type=F32)
                return run + jnp.sum(lk, axis=1, keepdims=True), grun + jnp.sum(g, axis=1, keepdims=True)

            z1 = jnp.zeros((2 * QB, 1), F32)
            _sb_tiles(i, tile, (z1, z1))
            d_ref[pl.ds(r0, QB), 0:128] = (_unstack_heads(dq_acc[...], head0) * SB_SCALE).astype(BF16)
            return carry

        lax.fori_loop(0, seq // QB, qblock, 0)
        d_ref[:, 128:256] = dk_acc[...].astype(BF16)
        d_ref[:, 256:384] = dv_acc[...].astype(BF16)
        finish_ride()

    return pl.pallas_call(
        body, name="sb_attn_bwd", grid=(bsz, 4),
        out_shape=(jax.ShapeDtypeStruct((bsz, seq, WA), BF16), jax.ShapeDtypeStruct(ride.shape, ride.dtype)),
        in_specs=[pl.BlockSpec((None, seq, 384), lambda b, hp: (b, 0, hp)),
                  pl.BlockSpec((None, seq, BLK), lambda b, hp: (b, 0, hp)),
                  pl.BlockSpec((None, seq, BLK), lambda b, hp: (b, 0, hp)), ANY],
        out_specs=(pl.BlockSpec((None, seq, 384), lambda b, hp: (b, 0, hp)), ANY),
        scratch_shapes=[pltpu.VMEM((2 * QB, BLK), F32), pltpu.VMEM((seq, BLK), F32), pltpu.VMEM((seq, BLK), F32),
                        *EXCHANGE_SEMS],
        compiler_params=_params("arbitrary", "arbitrary"),
    )(proj_a, d_o, o_a, ride)


def _rope_tables(seq):
    inv_freq = ROPE_THETA ** (-jnp.arange(32, dtype=F32) * 2.0 / 64)
    ang = jnp.arange(seq).astype(F32)[:, None] * inv_freq[None, :]
    cos, sin = jnp.cos(ang), jnp.sin(ang)
    return jnp.tile(cos, (1, 4)), jnp.concatenate([-sin, sin, -sin, sin], axis=1)


def _make_rope(n_rows):
    lane = lax.broadcasted_iota(jnp.int32, (n_rows, BLK), 1)
    first = (lane & 63) < 32

    def rope(x, cos, sin):
        partner = jnp.where(first, pltpu.roll(x, 96, 1), pltpu.roll(x, 32, 1))
        return x * cos + partner * sin

    return rope


DIL_UNROLL = 4


def _dil_consts():
    head0 = lax.broadcasted_iota(jnp.int32, (BLK, BLK), 1) < 64
    row = lax.broadcasted_iota(jnp.int32, (2 * BLK, 2 * BLK), 0) & (BLK - 1)
    col = lax.broadcasted_iota(jnp.int32, (2 * BLK, 2 * BLK), 1)
    valid_prev = jnp.logical_and(col < BLK, col >= row)
    valid_cur = jnp.logical_and(col >= BLK, row >= col - BLK)
    return head0, valid_prev, valid_cur


def _dil_blocks(dil, seq, block):
    nq = seq // dil // BLK

    def rows(r, i):
        if dil == 1:
            return pl.ds(pl.multiple_of(i * BLK, BLK), BLK)
        return pl.ds(r + (dil * BLK) * i, BLK, stride=dil)

    def step(t, carry):
        for u in range(DIL_UNROLL):
            n = t * DIL_UNROLL + u
            r, i = lax.div(n, nq), lax.rem(n, nq)
            block(rows(r, i), rows(r, jnp.maximum(i - 1, 0)), i)
        return carry

    lax.fori_loop(0, seq // BLK // DIL_UNROLL, step, 0)


def _dil_scores(qf, kf, vf, cur, prev, i, consts):
    head0, valid_prev, valid_cur = consts
    qs = _stack_heads(qf[cur, :].astype(BF16), head0)
    kcat = jnp.concatenate([kf[prev, :], kf[cur, :]], axis=0).astype(BF16)
    vcat = jnp.concatenate([vf[prev, :], vf[cur, :]], axis=0).astype(BF16)
    valid = jnp.logical_or(valid_cur, jnp.logical_and(valid_prev, i > 0))
    s = lax.dot_general(qs, kcat, NT, preferred_element_type=F32) * 0.125
    return qs, kcat, vcat, s, valid


def _head_cols(v):
    return jnp.concatenate([v[:, 0:1], v[:, 64:65]], axis=0)


def _dil_load_qkv(x_ref, c, rope, cos, sin, qf, kf, vf):
    qf[...] = rope(x_ref[:, c:c + 128].astype(F32), cos, sin).astype(BF16).astype(F32)
    kf[...] = rope(x_ref[:, c + 128:c + 256].astype(F32), cos, sin).astype(BF16).astype(F32)
    vf[...] = x_ref[:, c + 256:c + 384].astype(F32)


def _dil_fwd(proj_b, cos_t, sin_t, seq):
    bsz = proj_b.shape[0]

    def body(x_ref, cos_ref, sin_ref, ob_ref, lse_ref, qf, kf, vf, og, lg):
        consts = _dil_consts()
        head0 = consts[0]
        rope = _make_rope(seq)
        cos, sin = cos_ref[...], sin_ref[...]
        for g, dil in enumerate(DIL_GROUPS):
            _dil_load_qkv(x_ref, 384 * g, rope, cos, sin, qf, kf, vf)

            def block(cur, prev, i, g=g):
                _, _, vcat, s, valid = _dil_scores(qf, kf, vf, cur, prev, i, consts)
                s = jnp.where(valid, s, NEG_INF)
                m = jnp.max(s, axis=1, keepdims=True)
                p = jnp.exp(s - m)
                den = jnp.sum(p, axis=1, keepdims=True)
                o = jnp.dot(p.astype(BF16), vcat, preferred_element_type=F32) / den
                og[g, cur, :] = _unstack_heads(o, head0)
                lg[g, cur, :] = _unstack_heads(jnp.broadcast_to(m + jnp.log(den), (2 * BLK, BLK)), head0)

            _dil_blocks(dil, seq, block)
        ls = [lg[0], lg[1], lg[2]]
        m = jnp.maximum(jnp.maximum(ls[0], ls[1]), ls[2])
        ws = [jnp.exp(l - m) for l in ls]
        den = (ws[0] + ws[1]) + ws[2]
        ob_ref[...] = (((ws[0] * og[0] + ws[1] * og[1]) + ws[2] * og[2]) / den).astype(BF16)
        lse_ref[...] = m + jnp.log(den)

    tab_spec = pl.BlockSpec((seq, BLK), lambda b, hp: (0, 0))
    out_spec = pl.BlockSpec((None, seq, BLK), lambda b, hp: (b, 0, hp))
    slab = pltpu.VMEM((seq, BLK), F32)
    return pl.pallas_call(
        body, name="dil_attn_fwd", grid=(bsz, 2),
        out_shape=(jax.ShapeDtypeStruct((bsz, seq, 256), BF16), jax.ShapeDtypeStruct((bsz, seq, 256), F32)),
        in_specs=[pl.BlockSpec((None, seq, WB // 2), lambda b, hp: (b, 0, hp)), tab_spec, tab_spec],
        out_specs=(out_spec, out_spec),
        scratch_shapes=[slab, slab, slab, pltpu.VMEM((3, seq, BLK), F32), pltpu.VMEM((3, seq, BLK), F32)],
        compiler_params=_params("parallel", "parallel"),
    )(proj_b, cos_t, sin_t)


def _dil_bwd(proj_b, cos_t, sin_t, d_ob, o_b, lse, seq):
    bsz = proj_b.shape[0]

    def body(x_ref, cos_ref, sin_ref, do_ref, ob_ref, lse_ref, d_ref, qf, kf, vf, dof, dsf, dq_s, dk_acc, dv_acc):
        consts = _dil_consts()
        head0 = consts[0]
        rope = _make_rope(seq)
        cos, sin = cos_ref[...], sin_ref[...]
        do_all = do_ref[...].astype(F32)
        dof[...] = do_all
        head0_all = lax.broadcasted_iota(jnp.int32, (seq, BLK), 1) < 64
        d0, d1 = _head_rowsum(do_all * ob_ref[...].astype(F32), head0_all)
        dsf[...] = jnp.where(head0_all, d0, d1)
        for g, dil in enumerate(DIL_GROUPS):
            _dil_load_qkv(x_ref, 384 * g, rope, cos, sin, qf, kf, vf)
            dk_acc[...] = jnp.zeros_like(dk_acc)
            dv_acc[...] = jnp.zeros_like(dv_acc)

            def block(cur, prev, i):
                qs, kcat, vcat, s, valid = _dil_scores(qf, kf, vf, cur, prev, i, consts)
                dos = _stack_heads(dof[cur, :].astype(BF16), head0)
                p = jnp.where(valid, jnp.exp(s - _head_cols(lse_ref[cur, :])), 0.0)
                dp = lax.dot_general(dos, vcat, NT, preferred_element_type=F32)
                ds = ((p * (dp - _head_cols(dsf[cur, :]))) * 0.125).astype(BF16)
                dq_s[cur, :] = _unstack_heads(jnp.dot(ds, kcat, preferred_element_type=F32), head0)
                dk = lax.dot_general(ds, qs, TN, preferred_element_type=F32)
                dv = lax.dot_general(p.astype(BF16), dos, TN, preferred_element_type=F32)
                dk_acc[prev, :] += dk[:BLK]
                dk_acc[cur, :] += dk[BLK:]
                dv_acc[prev, :] += dv[:BLK]
                dv_acc[cur, :] += dv[BLK:]

            _dil_blocks(dil, seq, block)
            c = 384 * g
            d_ref[:, c:c + 128] = rope(dq_s[...], cos, -sin).astype(BF16)
            d_ref[:, c + 128:c + 256] = rope(dk_acc[...], cos, -sin).astype(BF16)
            d_ref[:, c + 256:c + 384] = dv_acc[...].astype(BF16)

    x_spec = pl.BlockSpec((None, seq, WB // 2), lambda b, hp: (b, 0, hp))
    tab_spec = pl.BlockSpec((seq, BLK), lambda b, hp: (0, 0))
    tok_spec = pl.BlockSpec((None, seq, BLK), lambda b, hp: (b, 0, hp))
    return pl.pallas_call(
        body, name="dil_attn_bwd", grid=(bsz, 2),
        out_shape=jax.ShapeDtypeStruct((bsz, seq, WB), BF16),
        in_specs=[x_spec, tab_spec, tab_spec, tok_spec, tok_spec, tok_spec], out_specs=x_spec,
        scratch_shapes=[pltpu.VMEM((seq, BLK), F32)] * 8,
        compiler_params=_params("parallel", "parallel"),
    )(proj_b, cos_t, sin_t, d_ob, o_b, lse)


MEM_SCALE = 128 ** -0.5
MEM_QB = 1024


def _mem_fwd(proj_a, kv, seq):
    bsz = proj_a.shape[0]

    def body(q_ref, k_ref, v_ref, o_ref):
        k, v = k_ref[...], v_ref[...]

        def qblock(i, carry):
            r0 = pl.multiple_of(i * MEM_QB, MEM_QB)
            s = lax.dot_general(q_ref[pl.ds(r0, MEM_QB), :], k, NT, preferred_element_type=F32) * MEM_SCALE
            p = jnp.exp(s - jnp.max(s, axis=1, keepdims=True))
            p = p / jnp.sum(p, axis=1, keepdims=True)
            o_ref[pl.ds(r0, MEM_QB), :] = jnp.dot(p.astype(BF16), v, preferred_element_type=F32).astype(BF16)
            return carry

        lax.fori_loop(0, seq // MEM_QB, qblock, 0)

    return pl.pallas_call(
        body, name="mem_attn_fwd", grid=(bsz, 4),
        out_shape=jax.ShapeDtypeStruct((bsz, seq, 512), BF16),
        in_specs=[pl.BlockSpec((None, seq, BLK), lambda b, h: (b, 0, 12 + h)),
                  pl.BlockSpec((None, MEM_LEN, BLK), lambda b, h: (b, 0, h)),
                  pl.BlockSpec((None, MEM_LEN, BLK), lambda b, h: (b, 0, 4 + h))],
        out_specs=pl.BlockSpec((None, seq, BLK), lambda b, h: (b, 0, h)),
        compiler_params=_params("parallel", "parallel"),
    )(proj_a, kv, kv)


def _mem_bwd(proj_a, kv, d_o, d_proj_a, seq):
    bsz = proj_a.shape[0]

    def body(q_ref, k_ref, v_ref, do_ref, _, dq_ref, dk_ref, dv_ref):
        k, v = k_ref[...], v_ref[...]

        def qblock(i, carry):
            dk, dv = carry
            r0 = pl.multiple_of(i * MEM_QB, MEM_QB)
            q, do = q_ref[pl.ds(r0, MEM_QB), :], do_ref[pl.ds(r0, MEM_QB), :]
            s = lax.dot_general(q, k, NT, preferred_element_type=F32) * MEM_SCALE
            p = jnp.exp(s - jnp.max(s, axis=1, keepdims=True))
            p = p / jnp.sum(p, axis=1, keepdims=True)
            dp = lax.dot_general(do, v, NT, preferred_element_type=F32)
            ds = ((p * (dp - jnp.sum(p * dp, axis=1, keepdims=True))) * MEM_SCALE).astype(BF16)
            dq_ref[pl.ds(r0, MEM_QB), :] = jnp.dot(ds, k, preferred_element_type=F32).astype(BF16)
            dk = dk + lax.dot_general(ds, q, TN, preferred_element_type=F32)
            dv = dv + lax.dot_general(p.astype(BF16), do, TN, preferred_element_type=F32)
            return dk, dv

        zero = jnp.zeros((MEM_LEN, BLK), F32)
        dk, dv = lax.fori_loop(0, seq // MEM_QB, qblock, (zero, zero))
        dk_ref[...] = dk.astype(BF16)
        dv_ref[...] = dv.astype(BF16)

    kv_spec = pl.BlockSpec((None, MEM_LEN, BLK), lambda b, h: (b, 0, h))
    return pl.pallas_call(
        body, name="mem_attn_bwd", grid=(bsz, 4),
        out_shape=(jax.ShapeDtypeStruct((bsz, seq, WA), BF16), jax.ShapeDtypeStruct((bsz, MEM_LEN, 512), BF16),
                   jax.ShapeDtypeStruct((bsz, MEM_LEN, 512), BF16)),
        in_specs=[pl.BlockSpec((None, seq, BLK), lambda b, h: (b, 0, 12 + h)), kv_spec,
                  pl.BlockSpec((None, MEM_LEN, BLK), lambda b, h: (b, 0, 4 + h)),
                  pl.BlockSpec((None, seq, BLK), lambda b, h: (b, 0, h)), ANY],
        out_specs=(pl.BlockSpec((None, seq, BLK), lambda b, h: (b, 0, 12 + h)), kv_spec, kv_spec),
        input_output_aliases={4: 0},
        compiler_params=_params("parallel", "parallel"),
    )(proj_a, kv, kv, d_o, d_proj_a)


def _mesh_pos():
    return lax.axis_index("x"), lax.axis_index("y"), lax.axis_index("c")


def _all_gather(shard, name):
    m_per, n = shard.shape

    def body(x_ref, out_ref, send_sems, recv_sems, local_sem):
        x, y, c = _mesh_pos()
        me, sibling = (x, y, c), (x, y, 1 - c)
        chips = [(1 - x, y), (x, 1 - y), (1 - x, 1 - y)]

        def rows(px, py, pc):
            return out_ref.at[pl.ds((4 * px + 2 * py + pc) * m_per, m_per), :]

        def copy(k, block, to, src=None):
            return pltpu.make_async_remote_copy(
                src_ref=rows(*block) if src is None else src, dst_ref=rows(*block),
                send_sem=send_sems.at[k], recv_sem=recv_sems.at[k], device_id=to, device_id_type=MESH)

        mine = pltpu.make_async_copy(x_ref, rows(*me), local_sem)
        mine.start()
        first = [copy(0, me, sibling, src=x_ref)]
        first += [copy(1 + j, me, (*chip, c), src=x_ref) for j, chip in enumerate(chips)]
        for cp in first:
            cp.start()
        passed = [copy(4 + j, (*chip, c), sibling) for j, chip in enumerate(chips)]
        for j, chip in enumerate(chips):
            copy(1 + j, (*chip, c), me).wait_recv()
            passed[j].start()
        copy(0, sibling, me).wait_recv()
        for j, chip in enumerate(chips):
            copy(4 + j, (*chip, 1 - c), me).wait_recv()
        for cp in first + passed:
            cp.wait_send()
        mine.wait()

    return pl.pallas_call(
        body, name=name, out_shape=jax.ShapeDtypeStruct((N_DEV * m_per, n), shard.dtype),
        in_specs=[ANY], out_specs=ANY,
        scratch_shapes=[pltpu.SemaphoreType.DMA((7,)), pltpu.SemaphoreType.DMA((7,)), pltpu.SemaphoreType.DMA(())],
    )(shard)


EXCHANGE_SEMS = [pltpu.SemaphoreType.DMA((7,)), pltpu.SemaphoreType.DMA((7,)), pltpu.SemaphoreType.DMA(())]


def _direct_exchange(src_ref, dst_ref, send_sems, recv_sems, local_sem, gather):
    x, y, c = _mesh_pos()
    me = 4 * x + 2 * y + c
    own = pltpu.make_async_copy(src_ref if gather else src_ref.at[me], dst_ref.at[me], local_sem)
    sends, recvs = [], []
    for k in range(1, N_DEV):
        px = 1 - x if k & 4 else x
        py = 1 - y if k & 2 else y
        pc = 1 - c if k & 1 else c
        peer = 4 * px + 2 * py + pc
        sems = dict(send_sem=send_sems.at[k - 1], recv_sem=recv_sems.at[k - 1],
                    device_id=(px, py, pc), device_id_type=MESH)
        sends.append(pltpu.make_async_remote_copy(
            src_ref=src_ref if gather else src_ref.at[peer], dst_ref=dst_ref.at[me], **sems))
        recvs.append(pltpu.make_async_remote_copy(
            src_ref=src_ref if gather else src_ref.at[me], dst_ref=dst_ref.at[peer], **sems))

    def start():
        own.start()
        for cp in sends:
            cp.start()

    def wait():
        for cp in recvs:
            cp.wait_recv()
        for cp in sends:
            cp.wait_send()
        own.wait()

    return start, wait


def _riding_exchange(src_ref, dst_ref, sems, gather):
    start, wait = _direct_exchange(src_ref, dst_ref, *sems, gather)
    ids = [pl.program_id(a) for a in range(2)]
    last = [pl.num_programs(a) - 1 for a in range(2)]
    pl.when(jnp.logical_and(ids[0] == 0, ids[1] == 0))(start)
    return lambda: pl.when(jnp.logical_and(ids[0] == last[0], ids[1] == last[1]))(wait)


def _exchange(src, gather, name):
    def body(src_ref, out_ref, *sems):
        start, wait = _direct_exchange(src_ref, out_ref, *sems, gather=gather)
        start()
        wait()

    return pl.pallas_call(
        body, name=name, out_shape=jax.ShapeDtypeStruct((N_DEV, *src.shape[-2:]), src.dtype),
        in_specs=[ANY], out_specs=ANY, scratch_shapes=EXCHANGE_SEMS,
    )(src)


def _adamw(w, g, m, v):
    m = ADAM_B1 * m + (1.0 - ADAM_B1) * g
    v = ADAM_B2 * v + (1.0 - ADAM_B2) * (g * g)
    m_hat = m / (1.0 - ADAM_B1 ** ADAM_STEP)
    v_hat = v / (1.0 - ADAM_B2 ** ADAM_STEP)
    return -ADAM_LR * (m_hat / (jnp.sqrt(v_hat) + ADAM_EPS) + ADAM_WD * w), m, v


def _reduce_adamw(recv, w, m, v, name):
    rows = w.shape[0]
    tr = max(t for t in range(16, 353, 16) if rows % t == 0)

    def body(r_ref, w_ref, m_ref, v_ref, g_out, d_out, m_out, v_out):
        g = r_ref[0].astype(F32)
        for s in range(1, N_DEV):
            g = g + r_ref[s].astype(F32)
        g_out[...] = g
        d_out[...], m_out[...], v_out[...] = _adamw(w_ref[...], g, m_ref[...], v_ref[...])

    spec = pl.BlockSpec((tr, D), lambda i: (i, 0))
    return pl.pallas_call(
        body, name=name, grid=(rows // tr,),
        out_shape=[jax.ShapeDtypeStruct((rows, D), F32)] * 4,
        in_specs=[pl.BlockSpec((N_DEV, tr, D), lambda i: (0, i, 0)), spec, spec, spec],
        out_specs=[spec] * 4, compiler_params=_params("arbitrary"),
    )(recv, w, m, v)


def _small_adamw(gathered, w, m, v):
    def body(g_ref, w_ref, m_ref, v_ref, g_out, d_out, m_out, v_out, loss_out):
        tot = g_ref[0]
        for s in range(1, N_DEV):
            tot = tot + g_ref[s]
        g = tot[0:8]
        g_out[...] = g
        d_out[...], m_out[...], v_out[...] = _adamw(w_ref[...], g, m_ref[...], v_ref[...])
        loss_out[...] = jnp.broadcast_to((0.5 / D) * jnp.sum(tot[8:9], axis=1, keepdims=True), (8, BLK))

    out = [jax.ShapeDtypeStruct((8, D), F32)] * 4 + [jax.ShapeDtypeStruct((8, BLK), F32)]
    return pl.pallas_call(body, name="small_adamw", out_shape=out, compiler_params=_params())(gathered, w, m, v)


def _pack(ws, dtype):
    return jnp.concatenate([w.astype(dtype).reshape(-1, D) for w in ws], axis=0)


def _pick_chunks(w, chunks):
    return jnp.concatenate([w[:, BLK * c:BLK * (c + 1)] for c in chunks], axis=1)


def _shard_rows(shape):
    return (shape[1] * shape[2]) // D


def _full_weights(gathered, idxs, shard_shapes):
    out, off = {}, 0
    for i in idxs:
        k, n = shard_shapes[i][1:]
        part = gathered[:, off:off + _shard_rows(shard_shapes[i])]
        off += _shard_rows(shard_shapes[i])
        if BY_ROWS[i]:
            out[i] = part.reshape(N_DEV * k, n)
        else:
            out[i] = part.reshape(N_DEV, k, n).transpose(1, 0, 2).reshape(k, N_DEV * n)
    return out


def _grad_parts(grads, idxs, shard_shapes):
    parts = []
    for i in idxs:
        k, n = shard_shapes[i][1:]
        g = grads[i]
        if not BY_ROWS[i]:
            g = g.reshape(k, N_DEV, n).transpose(1, 0, 2)
        parts.append(g.reshape(N_DEV, _shard_rows(shard_shapes[i]), D))
    return jnp.concatenate(parts, axis=1)


def _unpack_shards(packed, idxs, shard_shapes):
    out, off = {}, 0
    for i in idxs:
        rows = _shard_rows(shard_shapes[i])
        out[i] = packed[off:off + rows].reshape(shard_shapes[i])
        off += rows
    return out


def kernel(x, mem, g_pre_mix, g_post_mix, g_pre_ffn, g_post_ffn, g_mem, w_in, w_mem_kv, w_br_sb, w_br_dil, w_br_mem, w_gate, b_gate, w_o, w_ffn_in, w_ffn_out, loss_target, m_g_pre_mix, m_g_post_mix, m_g_pre_ffn, m_g_post_ffn, m_g_mem, m_w_in, m_w_mem_kv, m_w_br_sb, m_w_br_dil, m_w_br_mem, m_w_gate, m_b_gate, m_w_o, m_w_ffn_in, m_w_ffn_out, v_g_pre_mix, v_g_post_mix, v_g_pre_ffn, v_g_post_ffn, v_g_mem, v_w_in, v_w_mem_kv, v_w_br_sb, v_w_br_dil, v_w_br_mem, v_w_gate, v_b_gate, v_w_o, v_w_ffn_in, v_w_ffn_out):
    bsz, seq, _ = x.shape
    tokens = bsz * seq
    xf, tgt, memf = x.reshape(tokens, D), loss_target.reshape(tokens, D), mem.reshape(bsz * MEM_LEN, D)
    big_w = [w_in, w_mem_kv, w_br_sb, w_br_dil, w_br_mem, w_gate, w_o, w_ffn_in, w_ffn_out]
    big_m = [m_w_in, m_w_mem_kv, m_w_br_sb, m_w_br_dil, m_w_br_mem, m_w_gate, m_w_o, m_w_ffn_in, m_w_ffn_out]
    big_v = [v_w_in, v_w_mem_kv, v_w_br_sb, v_w_br_dil, v_w_br_mem, v_w_gate, v_w_o, v_w_ffn_in, v_w_ffn_out]
    shard_shapes = [w.shape for w in big_w]

    def gather_payload(idxs):
        return _pack([big_w[i][0] for i in idxs], BF16)

    first = _all_gather(gather_payload(GATHER_FIRST), "weight_all_gather")
    fw_in = _full_weights(first.reshape(N_DEV, -1, D), GATHER_FIRST, shard_shapes)[0]
    w_a, w_b = _pick_chunks(fw_in, CHUNKS_A), _pick_chunks(fw_in, CHUNKS_B)

    h = _norm_fwd(xf, g_pre_mix, "pre_mix_norm")
    proj_a = _matmul(h, w_a, "nn", BF16, "proj_a").reshape(bsz, seq, WA)
    proj_b = _matmul(h, w_b, "nn", BF16, "proj_b").reshape(bsz, seq, WB)
    o_a, o_a32, behind = _sb_fwd(proj_a, seq, gather_payload(GATHER_BEHIND))
    fw = _full_weights(behind, GATHER_BEHIND, shard_shapes)
    fw_mem_kv, fw_br_sb, fw_br_dil, fw_br_mem, fw_gate, fw_o, fw_ffn_in, fw_ffn_out = (fw[i] for i in GATHER_BEHIND)
    gpre = _matmul(h, fw_gate, "nn", BF16, "gate_proj")
    cos_t, sin_t = _rope_tables(seq)
    o_b, lse_b = _dil_fwd(proj_b, cos_t, sin_t, seq)
    mn = _norm_fwd(memf, g_mem, "mem_norm")
    kv = _matmul(mn, fw_mem_kv, "nn", BF16, "mem_kv_proj").reshape(bsz, MEM_LEN, D)
    o_c = _mem_fwd(proj_a, kv, seq)
    o_a2, o_b2, o_c2 = o_a.reshape(tokens, 512), o_b.reshape(tokens, 256), o_c.reshape(tokens, 512)
    ys = [_matmul(o_a2, fw_br_sb, "nn", BF16, "branch_sb"), _matmul(o_b2, fw_br_dil, "nn", BF16, "branch_dil"),
          _matmul(o_c2, fw_br_mem, "nn", BF16, "branch_mem")]
    merged = _gate_merge(gpre, ys, b_gate)
    mix = _matmul(merged, fw_o, "nn", F32, "out_proj")
    x1, h2 = _resid_norm_fwd(xf, mix, g_post_mix, g_pre_ffn)
    gu = _matmul(h2, fw_ffn_in, "nn", BF16, "ffn_in")
    f = _swiglu_fwd(gu)
    fo = _matmul(f, fw_ffn_out, "nn", F32, "ffn_out")
    dy, dfo, dg_post_ffn, loss_lanes = _loss_head(x1, fo, tgt, g_post_ffn)

    df = _matmul(dfo, fw_ffn_out, "nt", BF16, "d_ffn_act")
    gw_ffn_out = _matmul(f, dfo, "tn", BF16, "gw_ffn_out")
    dgu = _swiglu_bwd(df, gu)
    dh2 = _matmul(dgu, fw_ffn_in, "nt", F32, "d_h2")
    gw_ffn_in = _matmul(h2, dgu, "tn", BF16, "gw_ffn_in")
    dx1, dmix, dg_pre_ffn, dg_post_mix = _mid_norm_bwd(dh2, x1, dy, mix, g_pre_ffn, g_post_mix)
    dmerged = _matmul(dmix, fw_o, "nt", BF16, "d_merged")
    gw_o = _matmul(merged, dmix, "tn", BF16, "gw_o")
    dya, dyb, dyc, dgpre, db_gate = _gate_bwd(dmerged, gpre, ys, b_gate)
    d_oa = _matmul(dya, fw_br_sb, "nt", BF16, "d_o_sb").reshape(bsz, seq, 512)
    d_ob = _matmul(dyb, fw_br_dil, "nt", BF16, "d_o_dil").reshape(bsz, seq, 256)
    d_oc = _matmul(dyc, fw_br_mem, "nt", BF16, "d_o_mem").reshape(bsz, seq, 512)
    gw_br_sb = _matmul(o_a2, dya, "tn", BF16, "gw_br_sb")
    gw_br_dil = _matmul(o_b2, dyb, "tn", BF16, "gw_br_dil")
    gw_br_mem = _matmul(o_c2, dyc, "tn", BF16, "gw_br_mem")
    gw_gate = _matmul(h, dgpre, "tn", BF16, "gw_gate")
    grads = {2: gw_br_sb, 3: gw_br_dil, 4: gw_br_mem, 5: gw_gate, 6: gw_o, 7: gw_ffn_in, 8: gw_ffn_out}
    d_proj_a, recv_behind = _sb_bwd(proj_a, d_oa, o_a32, seq, _grad_parts(grads, REDUCE_BEHIND, shard_shapes))
    d_proj_a, dk_m, dv_m = _mem_bwd(proj_a, kv, d_oc, d_proj_a, seq)
    d_proj_b = _dil_bwd(proj_b, cos_t, sin_t, d_ob, o_b, lse_b, seq).reshape(tokens, WB)
    d_proj_a = d_proj_a.reshape(tokens, WA)
    dh = _matmul(dgpre, fw_gate, "nt", F32, "d_h_gate")
    dh = _matmul(d_proj_a, w_a, "nt", F32, "d_h_a", add=dh)
    dh = _matmul(d_proj_b, w_b, "nt", F32, "d_h_b", add=dh)
    gw_a = _matmul(h, d_proj_a, "tn", BF16, "gw_in_a")
    gw_b = _matmul(h, d_proj_b, "tn", BF16, "gw_in_b")
    dx, dg_pre_mix = _in_norm_bwd(dh, xf, dx1, g_pre_mix)
    dkv = jnp.concatenate([dk_m, dv_m], axis=-1).reshape(bsz * MEM_LEN, D)
    gw_mem_kv = _matmul(mn, dkv, "tn", BF16, "gw_mem_kv")
    dmn = _matmul(dkv, fw_mem_kv, "nt", F32, "d_mem_norm")
    dg_mem = _gain_grad(dmn, memf)

    gw_ab = jnp.concatenate([gw_a, gw_b], axis=1)
    where = {c: i for i, c in enumerate(CHUNKS_A + CHUNKS_B)}
    grads = {0: _pick_chunks(gw_ab, [where[c] for c in range(34)]), 1: gw_mem_kv}
    recv_last = _exchange(_grad_parts(grads, REDUCE_LAST, shard_shapes), False, "grad_exchange")
    big = [{}, {}, {}, {}]
    for recv, idxs, name in ((recv_behind, REDUCE_BEHIND, "reduce_adamw_behind"), (recv_last, REDUCE_LAST, "reduce_adamw_last")):
        packed = _reduce_adamw(recv, *(_pack([t[i][0] for i in idxs], F32) for t in (big_w, big_m, big_v)), name)
        for group, p in zip(big, packed):
            group.update(_unpack_shards(p, idxs, shard_shapes))
    big = [[group[i] for i in range(len(big_w))] for group in big]

    small = jnp.concatenate([dg_pre_mix, dg_post_mix, dg_pre_ffn, dg_post_ffn, dg_mem, db_gate.reshape(3, D),
                             loss_lanes, jnp.zeros((7, D), F32)], axis=0)
    small_all = _exchange(small, True, "small_all_gather")

    def small_pack(gs, b):
        return jnp.concatenate([*gs, b.reshape(3, D)], axis=0)

    sm = _small_adamw(
        small_all, small_pack([g_pre_mix, g_post_mix, g_pre_ffn, g_post_ffn, g_mem], b_gate),
        small_pack([m_g_pre_mix, m_g_post_mix, m_g_pre_ffn, m_g_post_ffn, m_g_mem], m_b_gate),
        small_pack([v_g_pre_mix, v_g_post_mix, v_g_pre_ffn, v_g_post_ffn, v_g_mem], v_b_gate))
    loss = sm[4][0, 0]

    def leaves(k):
        t, bw = sm[k], big[k]
        return [t[0:1], t[1:2], t[2:3], t[3:4], t[4:5], *bw[0:6], t[5:8].reshape(1, 3 * D), *bw[6:9]]

    return (loss, dx.reshape(bsz, seq, D), *leaves(0), *leaves(1), *leaves(2), *leaves(3))
```

```python
import functools

import jax
import jax.numpy as jnp
from jax import lax
from jax.experimental import pallas as pl
from jax.experimental.pallas import tpu as pltpu

F32 = jnp.float32
BF16 = jnp.bfloat16
D = 1024
BLK = 128
MEM_LEN = 256
D_FF = 2816
NORM_EPS = 1e-6
NEG_INF = -1e30
ROPE_THETA = 10000.0
ADAM_LR, ADAM_B1, ADAM_B2, ADAM_EPS, ADAM_WD, ADAM_STEP = 0.001, 0.9, 0.999, 1e-08, 0.01, 10
N_DEV = 8
VMEM_LIMIT_BYTES = 56 * 1024 * 1024
MESH = pl.DeviceIdType.MESH
ANY = pl.BlockSpec(memory_space=pl.ANY)

NT = (((1,), (1,)), ((), ()))
TN = (((0,), (0,)), ((), ()))
NN = (((1,), (0,)), ((), ()))
_DIMS = {"nn": NN, "nt": NT, "tn": TN}

BIG_NAMES = ("w_in", "w_mem_kv", "w_br_sb", "w_br_dil", "w_br_mem", "w_gate", "w_o", "w_ffn_in", "w_ffn_out")
BY_ROWS = (False, True, False, False, False, False, True, False, True)
GATHER_FIRST = (0,)
GATHER_BEHIND = (1, 2, 3, 4, 5, 6, 7, 8)
REDUCE_BEHIND = (2, 3, 4, 5, 6, 7, 8)
REDUCE_LAST = (0, 1)

CHUNKS_A = tuple(c for hp in range(4) for c in (hp, 4 + hp, 8 + hp)) + (30, 31, 32, 33)
CHUNKS_B = tuple(c for hp in range(2) for g in range(3) for c in (12 + 6 * g + hp, 14 + 6 * g + hp, 16 + 6 * g + hp))
WA, WB = 128 * len(CHUNKS_A), 128 * len(CHUNKS_B)
DIL_GROUPS = (1, 4, 16)


def _params(*sem):
    return pltpu.CompilerParams(dimension_semantics=sem or None, vmem_limit_bytes=VMEM_LIMIT_BYTES)


def _tile(n, cap):
    if n <= 128:
        return n
    assert n % 128 == 0, n
    best = 128
    for t in range(128, min(n, cap) + 1, 128):
        if n % t == 0:
            best = t
    return best


def _matmul(a, b, mode, out_dtype, name, add=None, tm_cap=1536, tn_cap=1536, tk_cap=1536):
    if mode == "tn":
        (K, M), N = a.shape, b.shape[1]
    elif mode == "nt":
        (M, K), N = a.shape, b.shape[0]
    else:
        (M, K), N = a.shape, b.shape[1]
    tm, tn, tk = _tile(M, tm_cap), _tile(N, tn_cap), _tile(K, tk_cap)
    nm, nn, nk = M // tm, N // tn, K // tk
    dims = _DIMS[mode]
    n_add = 0 if add is None else 1

    def body(a_ref, b_ref, *rest):
        o_ref = rest[n_add]

        def finish(v):
            if n_add:
                v = v + rest[0][...]
            o_ref[...] = v.astype(o_ref.dtype)

        p = lax.dot_general(a_ref[...], b_ref[...], dims, preferred_element_type=F32)
        if nk == 1:
            finish(p)
        else:
            acc_ref = rest[n_add + 1]
            k = pl.program_id(2)

            @pl.when(k == 0)
            def _():
                acc_ref[...] = p

            @pl.when(k > 0)
            def _():
                acc_ref[...] += p

            @pl.when(k == nk - 1)
            def _():
                finish(acc_ref[...])

    n_outer = nk == 1 and (a.size * nn + b.size) < (a.size + b.size * nm)
    if n_outer:
        grid, ij = (nn, nm, nk), (lambda g0, g1: (g1, g0))
    else:
        grid, ij = (nm, nn, nk), (lambda g0, g1: (g0, g1))
    if mode == "tn":
        a_spec = pl.BlockSpec((tk, tm), lambda g0, g1, k: (k, ij(g0, g1)[0]))
    else:
        a_spec = pl.BlockSpec((tm, tk), lambda g0, g1, k: (ij(g0, g1)[0], k))
    if mode == "nt":
        b_spec = pl.BlockSpec((tn, tk), lambda g0, g1, k: (ij(g0, g1)[1], k))
    else:
        b_spec = pl.BlockSpec((tk, tn), lambda g0, g1, k: (k, ij(g0, g1)[1]))
    o_spec = pl.BlockSpec((tm, tn), lambda g0, g1, k: ij(g0, g1))
    return pl.pallas_call(
        body, name=name, grid=grid,
        out_shape=jax.ShapeDtypeStruct((M, N), out_dtype),
        in_specs=[a_spec, b_spec] + [o_spec] * n_add,
        out_specs=o_spec,
        scratch_shapes=[pltpu.VMEM((tm, tn), F32)] if nk > 1 else [],
        compiler_params=_params("parallel", "parallel", "arbitrary"),
    )(a, b, *([add] if n_add else []))


def _rowwise(body, name, rows, tr, row_ins, vec_ins, row_outs, acc_outs=()):
    tr = min(tr, rows)
    assert rows % tr == 0
    in_specs, args = [], []
    for r in row_ins:
        arr, w, cb = r if isinstance(r, tuple) else (r, r.shape[1], 0)
        in_specs.append(pl.BlockSpec((tr, w), functools.partial(lambda i, cb: (i, cb), cb=cb)))
        args.append(arr)
    for v in vec_ins:
        in_specs.append(pl.BlockSpec(v.shape, lambda i: (0, 0)))
        args.append(v)
    out_shape = [jax.ShapeDtypeStruct((rows, w), dt) for w, dt in row_outs]
    out_shape += [jax.ShapeDtypeStruct((1, w), F32) for w in acc_outs]
    out_specs = [pl.BlockSpec((tr, w), lambda i: (i, 0)) for w, _ in row_outs]
    out_specs += [pl.BlockSpec((1, w), lambda i: (0, 0)) for w in acc_outs]
    n_acc = len(acc_outs)

    def wrapped(*refs):
        if n_acc:
            @pl.when(pl.program_id(0) == 0)
            def _():
                for r in refs[len(refs) - n_acc:]:
                    r[...] = jnp.zeros_like(r)
        body(*refs)

    return pl.pallas_call(
        wrapped, name=name, grid=(rows // tr,), out_shape=out_shape, in_specs=in_specs, out_specs=out_specs,
        compiler_params=_params("arbitrary"),
    )(*args)


def _rstd(x):
    return lax.rsqrt(jnp.mean(x * x, axis=-1, keepdims=True) + NORM_EPS)


def _norm_bwd(u, n, r):
    return r * (u - n * jnp.mean(u * n, axis=-1, keepdims=True))


def _colsum(v):
    return jnp.sum(v, axis=0, keepdims=True)


def _norm_fwd(x, g, name):
    def body(x_ref, g_ref, h_ref):
        xv = x_ref[...]
        h_ref[...] = ((xv * _rstd(xv)) * g_ref[...]).astype(BF16)

    return _rowwise(body, name, x.shape[0], 512, [x], [g], [(D, BF16)])[0]


def _matmul_rows(a, b, mode, name, epilogue, row_ins=(), vec_ins=(), row_outs=(), acc_outs=(), tm=512, tk_cap=1536):
    M, K = a.shape
    N = b.shape[1] if mode == "nn" else b.shape[0]
    tm, tk = min(tm, M), _tile(K, tk_cap)
    nm, nk = M // tm, K // tk
    dims = _DIMS[mode]
    n_extra, n_out = len(row_ins) + len(vec_ins), len(row_outs) + len(acc_outs)

    def body(a_ref, b_ref, *rest):
        extra, outs = rest[:n_extra], rest[n_extra:n_extra + n_out]
        i, k = pl.program_id(0), pl.program_id(1)
        if acc_outs:
            @pl.when(jnp.logical_and(i == 0, k == 0))
            def _():
                for r in outs[len(row_outs):]:
                    r[...] = jnp.zeros_like(r)
        p = lax.dot_general(a_ref[...], b_ref[...], dims, preferred_element_type=F32)
        if nk == 1:
            epilogue(p, *extra, *outs)
        else:
            acc_ref = rest[n_extra + n_out]

            @pl.when(k == 0)
            def _():
                acc_ref[...] = p

            @pl.when(k > 0)
            def _():
                acc_ref[...] += p

            @pl.when(k == nk - 1)
            def _():
                epilogue(acc_ref[...], *extra, *outs)

    b_spec = pl.BlockSpec((tk, N), lambda i, k: (k, 0)) if mode == "nn" else pl.BlockSpec((N, tk), lambda i, k: (0, k))
    in_specs = [pl.BlockSpec((tm, tk), lambda i, k: (i, k)), b_spec]
    in_specs += [pl.BlockSpec((tm, r.shape[1]), lambda i, k: (i, 0)) for r in row_ins]
    in_specs += [pl.BlockSpec(v.shape, lambda i, k: (0, 0)) for v in vec_ins]
    out_shape = [jax.ShapeDtypeStruct((M, w), dt) for w, dt in row_outs]
    out_shape += [jax.ShapeDtypeStruct((1, w), F32) for w in acc_outs]
    out_specs = [pl.BlockSpec((tm, w), lambda i, k: (i, 0)) for w, _ in row_outs]
    out_specs += [pl.BlockSpec((1, w), lambda i, k: (0, 0)) for w in acc_outs]
    return pl.pallas_call(
        body, name=name, grid=(nm, nk), out_shape=out_shape, in_specs=in_specs, out_specs=out_specs,
        scratch_shapes=[pltpu.VMEM((tm, N), F32)] if nk > 1 else [],
        compiler_params=_params("arbitrary", "arbitrary"),
    )(a, b, *row_ins, *vec_ins)


def _out_proj_norm(merged, w_o, x, g_post, g_pre):
    def epilogue(mv, x_ref, g2_ref, g3_ref, mix_ref, x1_ref, h2_ref):
        mix_ref[...] = mv
        x1 = x_ref[...] + (mv * _rstd(mv)) * g2_ref[...]
        x1_ref[...] = x1
        h2_ref[...] = ((x1 * _rstd(x1)) * g3_ref[...]).astype(BF16)

    return _matmul_rows(merged, w_o, "nn", "out_proj_norm", epilogue, [x], [g_post, g_pre],
                        [(D, F32), (D, F32), (D, BF16)])


def _gate_merge(gpre, ys, b_gate):
    def body(gp_ref, ya_ref, yb_ref, yc_ref, b_ref, m_ref):
        acc = None
        for k, y_ref in enumerate((ya_ref, yb_ref, yc_ref)):
            cols = slice(k * D, (k + 1) * D)
            gate = jax.nn.sigmoid(gp_ref[:, cols].astype(F32) + b_ref[:, cols])
            term = gate * y_ref[...].astype(F32)
            acc = term if acc is None else acc + term
        m_ref[...] = acc.astype(BF16)

    return _rowwise(body, "gate_merge", gpre.shape[0], 256, [gpre, *ys], [b_gate], [(D, BF16)])[0]


def _swiglu_fwd(gu):
    def body(a_ref, b_ref, f_ref):
        a = a_ref[...].astype(F32)
        f_ref[...] = (a * jax.nn.sigmoid(a) * b_ref[...].astype(F32)).astype(BF16)

    return _rowwise(body, "swiglu_fwd", gu.shape[0], 256, [(gu, D_FF, 0), (gu, D_FF, 1)], [], [(D_FF, BF16)])[0]


def _ffn_out_loss(f, w_ffn_out, x1, tgt, g_post):
    def epilogue(fo_v, x1_ref, t_ref, g_ref, dy_ref, dfo_ref, dg_ref, loss_ref):
        r = _rstd(fo_v)
        n = fo_v * r
        err = (x1_ref[...] + n * g_ref[...]) - t_ref[...]
        loss_ref[...] += _colsum(err * err)
        dy = err * (1.0 / D)
        dy_ref[...] = dy
        dg_ref[...] += _colsum(dy * n)
        dfo_ref[...] = _norm_bwd(dy * g_ref[...], n, r).astype(BF16)

    return _matmul_rows(f, w_ffn_out, "nn", "ffn_out_loss", epilogue, [x1, tgt], [g_post],
                        [(D, F32), (D, BF16)], (D, D))


def _d_ffn_swiglu_bwd(dfo, w_ffn_out, gu):
    def epilogue(d, gu_ref, dgu_ref):
        a = gu_ref[:, :D_FF].astype(F32)
        b = gu_ref[:, D_FF:].astype(F32)
        s = jax.nn.sigmoid(a)
        dgu_ref[:, :D_FF] = (d * b * (s * (1.0 + a * (1.0 - s)))).astype(BF16)
        dgu_ref[:, D_FF:] = (d * (a * s)).astype(BF16)

    return _matmul_rows(dfo, w_ffn_out, "nt", "d_ffn_swiglu_bwd", epilogue, [gu], [], [(2 * D_FF, BF16)], tm=256)[0]


def _d_h2_norm_bwd(dgu, w_ffn_in, x1, dy, mix, g_pre, g_post):
    def epilogue(dh, x1_ref, dy_ref, mix_ref, g3_ref, g2_ref, dx1_ref, dmix_ref, dg3_ref, dg2_ref):
        x1v = x1_ref[...]
        r3 = _rstd(x1v)
        n3 = x1v * r3
        dg3_ref[...] += _colsum(dh * n3)
        dx1 = dy_ref[...] + _norm_bwd(dh * g3_ref[...], n3, r3)
        dx1_ref[...] = dx1
        mv = mix_ref[...]
        r2 = _rstd(mv)
        n2 = mv * r2
        dg2_ref[...] += _colsum(dx1 * n2)
        dmix_ref[...] = _norm_bwd(dx1 * g2_ref[...], n2, r2).astype(BF16)

    return _matmul_rows(dgu, w_ffn_in, "nt", "d_h2_norm_bwd", epilogue, [x1, dy, mix], [g_pre, g_post],
                        [(D, F32), (D, BF16)], (D, D))


def _gate_bwd(dmerged, gpre, ys, b_gate):
    def body(dm_ref, gp_ref, ya_ref, yb_ref, yc_ref, b_ref, dya_ref, dyb_ref, dyc_ref, dgp_ref, db_ref):
        dm = dm_ref[...].astype(F32)
        for k, (y_ref, dy_ref) in enumerate(((ya_ref, dya_ref), (yb_ref, dyb_ref), (yc_ref, dyc_ref))):
            cols = slice(k * D, (k + 1) * D)
            gate = jax.nn.sigmoid(gp_ref[:, cols].astype(F32) + b_ref[:, cols])
            dy_ref[...] = (dm * gate).astype(BF16)
            dgp = (dm * y_ref[...].astype(F32)) * (gate * (1.0 - gate))
            dgp_ref[:, cols] = dgp.astype(BF16)
            db_ref[:, cols] += _colsum(dgp)

    return _rowwise(body, "gate_bwd", gpre.shape[0], 256, [dmerged, gpre, *ys], [b_gate],
                    [(D, BF16), (D, BF16), (D, BF16), (3 * D, BF16)], (3 * D,))


def _d_h_norm_bwd(d_proj, w, dh_rest, x, dx1, g_pre):
    def epilogue(dh_part, rest_ref, x_ref, dx1_ref, g_ref, dx_ref, dg_ref):
        dh = dh_part + rest_ref[...]
        xv = x_ref[...]
        r = _rstd(xv)
        n = xv * r
        dg_ref[...] += _colsum(dh * n)
        dx_ref[...] = dx1_ref[...] + _norm_bwd(dh * g_ref[...], n, r)

    return _matmul_rows(d_proj, w, "nt", "d_h_norm_bwd", epilogue, [dh_rest, x, dx1], [g_pre], [(D, F32)], (D,))


def _gain_grad(dmn, mem):
    def body(d_ref, m_ref, dg_ref):
        mv = m_ref[...]
        dg_ref[...] += _colsum(d_ref[...] * (mv * _rstd(mv)))

    return _rowwise(body, "mem_gain_grad", mem.shape[0], 256, [dmn, mem], [], [], (D,))[0]


def _head_rowsum(v, head0):
    return (jnp.sum(jnp.where(head0, v, 0.0), axis=1, keepdims=True),
            jnp.sum(jnp.where(head0, 0.0, v), axis=1, keepdims=True))


KT = 256
SB_SCALE = 0.125


def _make_suffix():
    tri = (lax.broadcasted_iota(jnp.int32, (KT, KT), 0) > lax.broadcasted_iota(jnp.int32, (KT, KT), 1)).astype(BF16)
    tri2 = jnp.concatenate([tri, tri], axis=0)

    def suffix(x):
        hi = x.astype(BF16)
        lo = (x - hi.astype(F32)).astype(BF16)
        return jnp.dot(jnp.concatenate([hi, lo], axis=1), tri2, preferred_element_type=F32)

    return suffix


def _sb_scores(qh, k, mask, suffix, run):
    z = lax.dot_general(qh, k, NT, preferred_element_type=F32)
    zc = jnp.minimum(z, 60.0)
    sp = jnp.log(1.0 + jnp.exp(zc))
    lb = zc - sp
    lk = -sp
    if mask is not None:
        lk = jnp.where(mask, lk, 0.0)
    a = jnp.exp(lb + suffix(lk) + run)
    if mask is not None:
        a = jnp.where(mask, a, 0.0)
    return lb, lk, a


QB = KT


def _sb_tiles(i, tile, init):
    st = tile(i, init, True)
    st = lax.fori_loop(0, lax.shift_right_logical(i, 1),
                       lambda t, s: tile(i - 2 - 2 * t, tile(i - 1 - 2 * t, s, False), False), st)
    return lax.cond((i & 1) == 1, lambda s: tile(0, s, False), lambda s: s, st)


def _sb_consts():
    head0 = lax.broadcasted_iota(jnp.int32, (QB, BLK), 1) < 64
    row = lax.broadcasted_iota(jnp.int32, (2 * QB, KT), 0) & (QB - 1)
    return head0, row > lax.broadcasted_iota(jnp.int32, (2 * QB, KT), 1)


def _stack_heads(v, head0):
    zero = jnp.zeros_like(v)
    return jnp.concatenate([jnp.where(head0, v, zero), jnp.where(head0, zero, v)], axis=0)


def _unstack_heads(v, head0):
    n = v.shape[0] // 2
    return jnp.where(head0, v[:n], v[n:])


def _sb_fwd(proj_a, seq, ride):
    bsz = proj_a.shape[0]

    def body(x_ref, ride_ref, o_ref, o32_ref, gathered_ref, acc_ref, *sems):
        finish_ride = _riding_exchange(ride_ref, gathered_ref, sems, gather=True)
        head0, diag_mask = _sb_consts()
        suffix = _make_suffix()

        def qblock(i, carry):
            r0 = pl.multiple_of(i * QB, QB)
            qs = _stack_heads(x_ref[pl.ds(r0, QB), 0:128] * jnp.asarray(SB_SCALE, BF16), head0)

            def tile(jt, run, masked):
                c0 = pl.multiple_of(jt * KT, KT)
                k = x_ref[pl.ds(c0, KT), 128:256]
                v = x_ref[pl.ds(c0, KT), 256:384]
                _, lk, a = _sb_scores(qs, k, diag_mask if masked else None, suffix, run)
                pv = jnp.dot(a.astype(BF16), v, preferred_element_type=F32)
                if masked:
                    acc_ref[...] = pv
                else:
                    acc_ref[...] += pv
                return run + jnp.sum(lk, axis=1, keepdims=True)

            _sb_tiles(i, tile, jnp.zeros((2 * QB, 1), F32))
            o = _unstack_heads(acc_ref[...], head0)
            o32_ref[pl.ds(r0, QB), :] = o
            o_ref[pl.ds(r0, QB), :] = o.astype(BF16)
            return carry

        lax.fori_loop(0, seq // QB, qblock, 0)
        finish_ride()

    out_spec = pl.BlockSpec((None, seq, BLK), lambda b, hp: (b, 0, hp))
    return pl.pallas_call(
        body, name="sb_attn_fwd", grid=(bsz, 4),
        out_shape=(jax.ShapeDtypeStruct((bsz, seq, 512), BF16), jax.ShapeDtypeStruct((bsz, seq, 512), F32),
                   jax.ShapeDtypeStruct((N_DEV, *ride.shape), ride.dtype)),
        in_specs=[pl.BlockSpec((None, seq, 384), lambda b, hp: (b, 0, hp)), ANY],
        out_specs=(out_spec, out_spec, ANY),
        scratch_shapes=[pltpu.VMEM((2 * QB, BLK), F32), *EXCHANGE_SEMS],
        compiler_params=_params("arbitrary", "arbitrary"),
    )(proj_a, ride)


def _sb_bwd(proj_a, d_o, o_a, seq, ride):
    bsz = proj_a.shape[0]

    def body(x_ref, do_ref, o_ref, ride_ref, d_ref, received_ref, dq_acc, dk_acc, dv_acc, *sems):
        finish_ride = _riding_exchange(ride_ref, received_ref, sems, gather=False)
        head0, diag_mask = _sb_consts()
        suffix = _make_suffix()
        dk_acc[...] = jnp.zeros_like(dk_acc)
        dv_acc[...] = jnp.zeros_like(dv_acc)

        def qblock(i, carry):
            r0 = pl.multiple_of(i * QB, QB)
            qs = _stack_heads(x_ref[pl.ds(r0, QB), 0:128] * jnp.asarray(SB_SCALE, BF16), head0)
            do = do_ref[pl.ds(r0, QB), :]
            dos = _stack_heads(do, head0)
            dsum = jnp.concatenate(_head_rowsum(do.astype(F32) * o_ref[pl.ds(r0, QB), :], head0), axis=0)

            def tile(jt, st, masked):
                run, grun = st
                c0 = pl.multiple_of(jt * KT, KT)
                k = x_ref[pl.ds(c0, KT), 128:256]
                v = x_ref[pl.ds(c0, KT), 256:384]
                lb, lk, a = _sb_scores(qs, k, diag_mask if masked else None, suffix, run)
                a16 = a.astype(BF16)
                g = a16.astype(F32) * lax.dot_general(dos, v, NT, preferred_element_type=F32)
                before = dsum - ((grun + suffix(g)) + g)
                dz = g - jnp.exp(lb) * (g + before)
                if masked:
                    dz = jnp.where(diag_mask, dz, 0.0)
                dz = dz.astype(BF16)
                dq = jnp.dot(dz, k, preferred_element_type=F32)
                if masked:
                    dq_acc[...] = dq
                else:
                    dq_acc[...] += dq
                dk_acc[pl.ds(c0, KT), :] += lax.dot_general(dz, qs, TN, preferred_element_type=F32)
                dv_acc[pl.ds(c0, KT), :] += lax.dot_general(a16, dos, TN, preferred_element_type=F32)
                return run + jnp.sum(lk, axis=1, keepdims=True), grun + jnp.sum(g, axis=1, keepdims=True)

            z1 = jnp.zeros((2 * QB, 1), F32)
            _sb_tiles(i, tile, (z1, z1))
            d_ref[pl.ds(r0, QB), 0:128] = (_unstack_heads(dq_acc[...], head0) * SB_SCALE).astype(BF16)
            return carry

        lax.fori_loop(0, seq // QB, qblock, 0)
        d_ref[:, 128:256] = dk_acc[...].astype(BF16)
        d_ref[:, 256:384] = dv_acc[...].astype(BF16)
        finish_ride()

    return pl.pallas_call(
        body, name="sb_attn_bwd", grid=(bsz, 4),
        out_shape=(jax.ShapeDtypeStruct((bsz, seq, WA), BF16), jax.ShapeDtypeStruct(ride.shape, ride.dtype)),
        in_specs=[pl.BlockSpec((None, seq, 384), lambda b, hp: (b, 0, hp)),
                  pl.BlockSpec((None, seq, BLK), lambda b, hp: (b, 0, hp)),
                  pl.BlockSpec((None, seq, BLK), lambda b, hp: (b, 0, hp)), ANY],
        out_specs=(pl.BlockSpec((None, seq, 384), lambda b, hp: (b, 0, hp)), ANY),
        scratch_shapes=[pltpu.VMEM((2 * QB, BLK), F32), pltpu.VMEM((seq, BLK), F32), pltpu.VMEM((seq, BLK), F32),
                        *EXCHANGE_SEMS],
        compiler_params=_params("arbitrary", "arbitrary"),
    )(proj_a, d_o, o_a, ride)


def _rope_tables(seq):
    inv_freq = ROPE_THETA ** (-jnp.arange(32, dtype=F32) * 2.0 / 64)
    ang = jnp.arange(seq).astype(F32)[:, None] * inv_freq[None, :]
    cos, sin = jnp.cos(ang), jnp.sin(ang)
    return jnp.tile(cos, (1, 4)), jnp.concatenate([-sin, sin, -sin, sin], axis=1)


def _make_rope(n_rows):
    lane = lax.broadcasted_iota(jnp.int32, (n_rows, BLK), 1)
    first = (lane & 63) < 32

    def rope(x, cos, sin):
        partner = jnp.where(first, pltpu.roll(x, 96, 1), pltpu.roll(x, 32, 1))
        return x * cos + partner * sin

    return rope


DIL_UNROLL = 4


def _dil_consts():
    head0 = lax.broadcasted_iota(jnp.int32, (BLK, BLK), 1) < 64
    row = lax.broadcasted_iota(jnp.int32, (2 * BLK, 2 * BLK), 0) & (BLK - 1)
    col = lax.broadcasted_iota(jnp.int32, (2 * BLK, 2 * BLK), 1)
    valid_prev = jnp.logical_and(col < BLK, col >= row)
    valid_cur = jnp.logical_and(col >= BLK, row >= col - BLK)
    return head0, valid_prev, valid_cur


def _dil_blocks(dil, seq, block):
    nq = seq // dil // BLK

    def rows(r, i):
        if dil == 1:
            return pl.ds(pl.multiple_of(i * BLK, BLK), BLK)
        return pl.ds(r + (dil * BLK) * i, BLK, stride=dil)

    def step(t, carry):
        for u in range(DIL_UNROLL):
            n = t * DIL_UNROLL + u
            r, i = lax.div(n, nq), lax.rem(n, nq)
            block(rows(r, i), rows(r, jnp.maximum(i - 1, 0)), i)
        return carry

    lax.fori_loop(0, seq // BLK // DIL_UNROLL, step, 0)


def _dil_scores(qf, kf, vf, cur, prev, i, consts):
    head0, valid_prev, valid_cur = consts
    qs = _stack_heads(qf[cur, :].astype(BF16), head0)
    kcat = jnp.concatenate([kf[prev, :], kf[cur, :]], axis=0).astype(BF16)
    vcat = jnp.concatenate([vf[prev, :], vf[cur, :]], axis=0).astype(BF16)
    valid = jnp.logical_or(valid_cur, jnp.logical_and(valid_prev, i > 0))
    s = lax.dot_general(qs, kcat, NT, preferred_element_type=F32) * 0.125
    return qs, kcat, vcat, s, valid


def _head_cols(v):
    return jnp.concatenate([v[:, 0:1], v[:, 64:65]], axis=0)


def _dil_load_qkv(x_ref, c, rope, cos, sin, qf, kf, vf):
    qf[...] = rope(x_ref[:, c:c + 128].astype(F32), cos, sin).astype(BF16).astype(F32)
    kf[...] = rope(x_ref[:, c + 128:c + 256].astype(F32), cos, sin).astype(BF16).astype(F32)
    vf[...] = x_ref[:, c + 256:c + 384].astype(F32)


def _dil_fwd(proj_b, cos_t, sin_t, seq):
    bsz = proj_b.shape[0]

    def body(x_ref, cos_ref, sin_ref, ob_ref, lse_ref, qf, kf, vf, og, lg):
        consts = _dil_consts()
        head0 = consts[0]
        rope = _make_rope(seq)
        cos, sin = cos_ref[...], sin_ref[...]
        for g, dil in enumerate(DIL_GROUPS):
            _dil_load_qkv(x_ref, 384 * g, rope, cos, sin, qf, kf, vf)

            def block(cur, prev, i, g=g):
                _, _, vcat, s, valid = _dil_scores(qf, kf, vf, cur, prev, i, consts)
                s = jnp.where(valid, s, NEG_INF)
                m = jnp.max(s, axis=1, keepdims=True)
                p = jnp.exp(s - m)
                den = jnp.sum(p, axis=1, keepdims=True)
                o = jnp.dot(p.astype(BF16), vcat, preferred_element_type=F32) / den
                og[g, cur, :] = _unstack_heads(o, head0)
                lg[g, cur, :] = _unstack_heads(jnp.broadcast_to(m + jnp.log(den), (2 * BLK, BLK)), head0)

            _dil_blocks(dil, seq, block)
        ls = [lg[0], lg[1], lg[2]]
        m = jnp.maximum(jnp.maximum(ls[0], ls[1]), ls[2])
        ws = [jnp.exp(l - m) for l in ls]
        den = (ws[0] + ws[1]) + ws[2]
        ob_ref[...] = (((ws[0] * og[0] + ws[1] * og[1]) + ws[2] * og[2]) / den).astype(BF16)
        lse_ref[...] = m + jnp.log(den)

    tab_spec = pl.BlockSpec((seq, BLK), lambda b, hp: (0, 0))
    out_spec = pl.BlockSpec((None, seq, BLK), lambda b, hp: (b, 0, hp))
    slab = pltpu.VMEM((seq, BLK), F32)
    return pl.pallas_call(
        body, name="dil_attn_fwd", grid=(bsz, 2),
        out_shape=(jax.ShapeDtypeStruct((bsz, seq, 256), BF16), jax.ShapeDtypeStruct((bsz, seq, 256), F32)),
        in_specs=[pl.BlockSpec((None, seq, WB // 2), lambda b, hp: (b, 0, hp)), tab_spec, tab_spec],
        out_specs=(out_spec, out_spec),
        scratch_shapes=[slab, slab, slab, pltpu.VMEM((3, seq, BLK), F32), pltpu.VMEM((3, seq, BLK), F32)],
        compiler_params=_params("parallel", "parallel"),
    )(proj_b, cos_t, sin_t)


def _dil_bwd(proj_b, cos_t, sin_t, d_ob, o_b, lse, seq):
    bsz = proj_b.shape[0]

    def body(x_ref, cos_ref, sin_ref, do_ref, ob_ref, lse_ref, d_ref, qf, kf, vf, dof, dsf, dq_s, dk_acc, dv_acc):
        consts = _dil_consts()
        head0 = consts[0]
        rope = _make_rope(seq)
        cos, sin = cos_ref[...], sin_ref[...]
        do_all = do_ref[...].astype(F32)
        dof[...] = do_all
        head0_all = lax.broadcasted_iota(jnp.int32, (seq, BLK), 1) < 64
        d0, d1 = _head_rowsum(do_all * ob_ref[...].astype(F32), head0_all)
        dsf[...] = jnp.where(head0_all, d0, d1)
        for g, dil in enumerate(DIL_GROUPS):
            _dil_load_qkv(x_ref, 384 * g, rope, cos, sin, qf, kf, vf)
            dk_acc[...] = jnp.zeros_like(dk_acc)
            dv_acc[...] = jnp.zeros_like(dv_acc)

            def block(cur, prev, i):
                qs, kcat, vcat, s, valid = _dil_scores(qf, kf, vf, cur, prev, i, consts)
                dos = _stack_heads(dof[cur, :].astype(BF16), head0)
                p = jnp.where(valid, jnp.exp(s - _head_cols(lse_ref[cur, :])), 0.0)
                dp = lax.dot_general(dos, vcat, NT, preferred_element_type=F32)
                ds = ((p * (dp - _head_cols(dsf[cur, :]))) * 0.125).astype(BF16)
                dq_s[cur, :] = _unstack_heads(jnp.dot(ds, kcat, preferred_element_type=F32), head0)
                dk = lax.dot_general(ds, qs, TN, preferred_element_type=F32)
                dv = lax.dot_general(p.astype(BF16), dos, TN, preferred_element_type=F32)
                dk_acc[prev, :] += dk[:BLK]
                dk_acc[cur, :] += dk[BLK:]
                dv_acc[prev, :] += dv[:BLK]
                dv_acc[cur, :] += dv[BLK:]

            _dil_blocks(dil, seq, block)
            c = 384 * g
            d_ref[:, c:c + 128] = rope(dq_s[...], cos, -sin).astype(BF16)
            d_ref[:, c + 128:c + 256] = rope(dk_acc[...], cos, -sin).astype(BF16)
            d_ref[:, c + 256:c + 384] = dv_acc[...].astype(BF16)

    x_spec = pl.BlockSpec((None, seq, WB // 2), lambda b, hp: (b, 0, hp))
    tab_spec = pl.BlockSpec((seq, BLK), lambda b, hp: (0, 0))
    tok_spec = pl.BlockSpec((None, seq, BLK), lambda b, hp: (b, 0, hp))
    return pl.pallas_call(
        body, name="dil_attn_bwd", grid=(bsz, 2),
        out_shape=jax.ShapeDtypeStruct((bsz, seq, WB), BF16),
        in_specs=[x_spec, tab_spec, tab_spec, tok_spec, tok_spec, tok_spec], out_specs=x_spec,
        scratch_shapes=[pltpu.VMEM((seq, BLK), F32)] * 8,
        compiler_params=_params("parallel", "parallel"),
    )(proj_b, cos_t, sin_t, d_ob, o_b, lse)


MEM_SCALE = 128 ** -0.5
MEM_QB = 1024


def _mem_fwd(proj_a, kv, seq):
    bsz = proj_a.shape[0]

    def body(q_ref, k_ref, v_ref, o_ref):
        k, v = k_ref[...], v_ref[...]

        def qblock(i, carry):
            r0 = pl.multiple_of(i * MEM_QB, MEM_QB)
            s = lax.dot_general(q_ref[pl.ds(r0, MEM_QB), :], k, NT, preferred_element_type=F32) * MEM_SCALE
            p = jnp.exp(s - jnp.max(s, axis=1, keepdims=True))
            p = p / jnp.sum(p, axis=1, keepdims=True)
            o_ref[pl.ds(r0, MEM_QB), :] = jnp.dot(p.astype(BF16), v, preferred_element_type=F32).astype(BF16)
            return carry

        lax.fori_loop(0, seq // MEM_QB, qblock, 0)

    return pl.pallas_call(
        body, name="mem_attn_fwd", grid=(bsz, 4),
        out_shape=jax.ShapeDtypeStruct((bsz, seq, 512), BF16),
        in_specs=[pl.BlockSpec((None, seq, BLK), lambda b, h: (b, 0, 12 + h)),
                  pl.BlockSpec((None, MEM_LEN, BLK), lambda b, h: (b, 0, h)),
                  pl.BlockSpec((None, MEM_LEN, BLK), lambda b, h: (b, 0, 4 + h))],
        out_specs=pl.BlockSpec((None, seq, BLK), lambda b, h: (b, 0, h)),
        compiler_params=_params("parallel", "parallel"),
    )(proj_a, kv, kv)


def _mem_bwd(proj_a, kv, d_o, d_proj_a, seq):
    bsz = proj_a.shape[0]

    def body(q_ref, k_ref, v_ref, do_ref, _, dq_ref, dk_ref, dv_ref):
        k, v = k_ref[...], v_ref[...]

        def qblock(i, carry):
            dk, dv = carry
            r0 = pl.multiple_of(i * MEM_QB, MEM_QB)
            q, do = q_ref[pl.ds(r0, MEM_QB), :], do_ref[pl.ds(r0, MEM_QB), :]
            s = lax.dot_general(q, k, NT, preferred_element_type=F32) * MEM_SCALE
            p = jnp.exp(s - jnp.max(s, axis=1, keepdims=True))
            p = p / jnp.sum(p, axis=1, keepdims=True)
            dp = lax.dot_general(do, v, NT, preferred_element_type=F32)
            ds = ((p * (dp - jnp.sum(p * dp, axis=1, keepdims=True))) * MEM_SCALE).astype(BF16)
            dq_ref[pl.ds(r0, MEM_QB), :] = jnp.dot(ds, k, preferred_element_type=F32).astype(BF16)
            dk = dk + lax.dot_general(ds, q, TN, preferred_element_type=F32)
            dv = dv + lax.dot_general(p.astype(BF16), do, TN, preferred_element_type=F32)
            return dk, dv

        zero = jnp.zeros((MEM_LEN, BLK), F32)
        dk, dv = lax.fori_loop(0, seq // MEM_QB, qblock, (zero, zero))
        dk_ref[...] = dk.astype(BF16)
        dv_ref[...] = dv.astype(BF16)

    kv_spec = pl.BlockSpec((None, MEM_LEN, BLK), lambda b, h: (b, 0, h))
    return pl.pallas_call(
        body, name="mem_attn_bwd", grid=(bsz, 4),
        out_shape=(jax.ShapeDtypeStruct((bsz, seq, WA), BF16), jax.ShapeDtypeStruct((bsz, MEM_LEN, 512), BF16),
                   jax.ShapeDtypeStruct((bsz, MEM_LEN, 512), BF16)),
        in_specs=[pl.BlockSpec((None, seq, BLK), lambda b, h: (b, 0, 12 + h)), kv_spec,
                  pl.BlockSpec((None, MEM_LEN, BLK), lambda b, h: (b, 0, 4 + h)),
                  pl.BlockSpec((None, seq, BLK), lambda b, h: (b, 0, h)), ANY],
        out_specs=(pl.BlockSpec((None, seq, BLK), lambda b, h: (b, 0, 12 + h)), kv_spec, kv_spec),
        input_output_aliases={4: 0},
        compiler_params=_params("parallel", "parallel"),
    )(proj_a, kv, kv, d_o, d_proj_a)


def _mesh_pos():
    return lax.axis_index("x"), lax.axis_index("y"), lax.axis_index("c")


def _all_gather(shard, name):
    m_per, n = shard.shape

    def body(x_ref, out_ref, send_sems, recv_sems, local_sem):
        x, y, c = _mesh_pos()
        me, sibling = (x, y, c), (x, y, 1 - c)
        chips = [(1 - x, y), (x, 1 - y), (1 - x, 1 - y)]

        def rows(px, py, pc):
            return out_ref.at[pl.ds((4 * px + 2 * py + pc) * m_per, m_per), :]

        def copy(k, block, to, src=None):
            return pltpu.make_async_remote_copy(
                src_ref=rows(*block) if src is None else src, dst_ref=rows(*block),
                send_sem=send_sems.at[k], recv_sem=recv_sems.at[k], device_id=to, device_id_type=MESH)

        mine = pltpu.make_async_copy(x_ref, rows(*me), local_sem)
        mine.start()
        first = [copy(0, me, sibling, src=x_ref)]
        first += [copy(1 + j, me, (*chip, c), src=x_ref) for j, chip in enumerate(chips)]
        for cp in first:
            cp.start()
        passed = [copy(4 + j, (*chip, c), sibling) for j, chip in enumerate(chips)]
        for j, chip in enumerate(chips):
            copy(1 + j, (*chip, c), me).wait_recv()
            passed[j].start()
        copy(0, sibling, me).wait_recv()
        for j, chip in enumerate(chips):
            copy(4 + j, (*chip, 1 - c), me).wait_recv()
        for cp in first + passed:
            cp.wait_send()
        mine.wait()

    return pl.pallas_call(
        body, name=name, out_shape=jax.ShapeDtypeStruct((N_DEV * m_per, n), shard.dtype),
        in_specs=[ANY], out_specs=ANY,
        scratch_shapes=[pltpu.SemaphoreType.DMA((7,)), pltpu.SemaphoreType.DMA((7,)), pltpu.SemaphoreType.DMA(())],
    )(shard)


EXCHANGE_SEMS = [pltpu.SemaphoreType.DMA((7,)), pltpu.SemaphoreType.DMA((7,)), pltpu.SemaphoreType.DMA(())]


def _direct_exchange(src_ref, dst_ref, send_sems, recv_sems, local_sem, gather):
    x, y, c = _mesh_pos()
    me = 4 * x + 2 * y + c
    own = pltpu.make_async_copy(src_ref if gather else src_ref.at[me], dst_ref.at[me], local_sem)
    sends, recvs = [], []
    for k in range(1, N_DEV):
        px = 1 - x if k & 4 else x
        py = 1 - y if k & 2 else y
        pc = 1 - c if k & 1 else c
        peer = 4 * px + 2 * py + pc
        sems = dict(send_sem=send_sems.at[k - 1], recv_sem=recv_sems.at[k - 1],
                    device_id=(px, py, pc), device_id_type=MESH)
        sends.append(pltpu.make_async_remote_copy(
            src_ref=src_ref if gather else src_ref.at[peer], dst_ref=dst_ref.at[me], **sems))
        recvs.append(pltpu.make_async_remote_copy(
            src_ref=src_ref if gather else src_ref.at[me], dst_ref=dst_ref.at[peer], **sems))

    def start():
        own.start()
        for cp in sends:
            cp.start()

    def wait():
        for cp in recvs:
            cp.wait_recv()
        for cp in sends:
            cp.wait_send()
        own.wait()

    return start, wait


def _riding_exchange(src_ref, dst_ref, sems, gather):
    start, wait = _direct_exchange(src_ref, dst_ref, *sems, gather)
    ids = [pl.program_id(a) for a in range(2)]
    last = [pl.num_programs(a) - 1 for a in range(2)]
    pl.when(jnp.logical_and(ids[0] == 0, ids[1] == 0))(start)
    return lambda: pl.when(jnp.logical_and(ids[0] == last[0], ids[1] == last[1]))(wait)


def _exchange(src, gather, name):
    def body(src_ref, out_ref, *sems):
        start, wait = _direct_exchange(src_ref, out_ref, *sems, gather=gather)
        start()
        wait()

    return pl.pallas_call(
        body, name=name, out_shape=jax.ShapeDtypeStruct((N_DEV, *src.shape[-2:]), src.dtype),
        in_specs=[ANY], out_specs=ANY, scratch_shapes=EXCHANGE_SEMS,
    )(src)


def _adamw(w, g, m, v):
    m = ADAM_B1 * m + (1.0 - ADAM_B1) * g
    v = ADAM_B2 * v + (1.0 - ADAM_B2) * (g * g)
    m_hat = m / (1.0 - ADAM_B1 ** ADAM_STEP)
    v_hat = v / (1.0 - ADAM_B2 ** ADAM_STEP)
    return -ADAM_LR * (m_hat / (jnp.sqrt(v_hat) + ADAM_EPS) + ADAM_WD * w), m, v


def _reduce_adamw(recv, w, m, v, name):
    rows = w.shape[0]
    tr = max(t for t in range(16, 353, 16) if rows % t == 0)

    def body(r_ref, w_ref, m_ref, v_ref, g_out, d_out, m_out, v_out):
        g = r_ref[0].astype(F32)
        for s in range(1, N_DEV):
            g = g + r_ref[s].astype(F32)
        g_out[...] = g
        d_out[...], m_out[...], v_out[...] = _adamw(w_ref[...], g, m_ref[...], v_ref[...])

    spec = pl.BlockSpec((tr, D), lambda i: (i, 0))
    return pl.pallas_call(
        body, name=name, grid=(rows // tr,),
        out_shape=[jax.ShapeDtypeStruct((rows, D), F32)] * 4,
        in_specs=[pl.BlockSpec((N_DEV, tr, D), lambda i: (0, i, 0)), spec, spec, spec],
        out_specs=[spec] * 4, compiler_params=_params("arbitrary"),
    )(recv, w, m, v)


def _small_adamw(gathered, w, m, v):
    def body(g_ref, w_ref, m_ref, v_ref, g_out, d_out, m_out, v_out, loss_out):
        tot = g_ref[0]
        for s in range(1, N_DEV):
            tot = tot + g_ref[s]
        g = tot[0:8]
        g_out[...] = g
        d_out[...], m_out[...], v_out[...] = _adamw(w_ref[...], g, m_ref[...], v_ref[...])
        loss_out[...] = jnp.broadcast_to((0.5 / D) * jnp.sum(tot[8:9], axis=1, keepdims=True), (8, BLK))

    out = [jax.ShapeDtypeStruct((8, D), F32)] * 4 + [jax.ShapeDtypeStruct((8, BLK), F32)]
    return pl.pallas_call(body, name="small_adamw", out_shape=out, compiler_params=_params())(gathered, w, m, v)


def _pack(ws, dtype):
    return jnp.concatenate([w.astype(dtype).reshape(-1, D) for w in ws], axis=0)


def _pick_chunks(w, chunks):
    return jnp.concatenate([w[:, BLK * c:BLK * (c + 1)] for c in chunks], axis=1)


def _shard_rows(shape):
    return (shape[1] * shape[2]) // D


def _full_weights(gathered, idxs, shard_shapes):
    out, off = {}, 0
    for i in idxs:
        k, n = shard_shapes[i][1:]
        part = gathered[:, off:off + _shard_rows(shard_shapes[i])]
        off += _shard_rows(shard_shapes[i])
        if BY_ROWS[i]:
            out[i] = part.reshape(N_DEV * k, n)
        else:
            out[i] = part.reshape(N_DEV, k, n).transpose(1, 0, 2).reshape(k, N_DEV * n)
    return out


def _grad_parts(grads, idxs, shard_shapes):
    parts = []
    for i in idxs:
        k, n = shard_shapes[i][1:]
        g = grads[i]
        if not BY_ROWS[i]:
            g = g.reshape(k, N_DEV, n).transpose(1, 0, 2)
        parts.append(g.reshape(N_DEV, _shard_rows(shard_shapes[i]), D))
    return jnp.concatenate(parts, axis=1)


def _unpack_shards(packed, idxs, shard_shapes):
    out, off = {}, 0
    for i in idxs:
        rows = _shard_rows(shard_shapes[i])
        out[i] = packed[off:off + rows].reshape(shard_shapes[i])
        off += rows
    return out


def kernel(x, mem, g_pre_mix, g_post_mix, g_pre_ffn, g_post_ffn, g_mem, w_in, w_mem_kv, w_br_sb, w_br_dil, w_br_mem, w_gate, b_gate, w_o, w_ffn_in, w_ffn_out, loss_target, m_g_pre_mix, m_g_post_mix, m_g_pre_ffn, m_g_post_ffn, m_g_mem, m_w_in, m_w_mem_kv, m_w_br_sb, m_w_br_dil, m_w_br_mem, m_w_gate, m_b_gate, m_w_o, m_w_ffn_in, m_w_ffn_out, v_g_pre_mix, v_g_post_mix, v_g_pre_ffn, v_g_post_ffn, v_g_mem, v_w_in, v_w_mem_kv, v_w_br_sb, v_w_br_dil, v_w_br_mem, v_w_gate, v_b_gate, v_w_o, v_w_ffn_in, v_w_ffn_out):
    bsz, seq, _ = x.shape
    tokens = bsz * seq
    xf, tgt, memf = x.reshape(tokens, D), loss_target.reshape(tokens, D), mem.reshape(bsz * MEM_LEN, D)
    big_w = [w_in, w_mem_kv, w_br_sb, w_br_dil, w_br_mem, w_gate, w_o, w_ffn_in, w_ffn_out]
    big_m = [m_w_in, m_w_mem_kv, m_w_br_sb, m_w_br_dil, m_w_br_mem, m_w_gate, m_w_o, m_w_ffn_in, m_w_ffn_out]
    big_v = [v_w_in, v_w_mem_kv, v_w_br_sb, v_w_br_dil, v_w_br_mem, v_w_gate, v_w_o, v_w_ffn_in, v_w_ffn_out]
    shard_shapes = [w.shape for w in big_w]

    def gather_payload(idxs):
        return _pack([big_w[i][0] for i in idxs], BF16)

    first = _all_gather(gather_payload(GATHER_FIRST), "weight_all_gather")
    fw_in = _full_weights(first.reshape(N_DEV, -1, D), GATHER_FIRST, shard_shapes)[0]
    w_a, w_b = _pick_chunks(fw_in, CHUNKS_A), _pick_chunks(fw_in, CHUNKS_B)

    h = _norm_fwd(xf, g_pre_mix, "pre_mix_norm")
    proj_a = _matmul(h, w_a, "nn", BF16, "proj_a").reshape(bsz, seq, WA)
    proj_b = _matmul(h, w_b, "nn", BF16, "proj_b").reshape(bsz, seq, WB)
    o_a, o_a32, behind = _sb_fwd(proj_a, seq, gather_payload(GATHER_BEHIND))
    fw = _full_weights(behind, GATHER_BEHIND, shard_shapes)
    fw_mem_kv, fw_br_sb, fw_br_dil, fw_br_mem, fw_gate, fw_o, fw_ffn_in, fw_ffn_out = (fw[i] for i in GATHER_BEHIND)
    gpre = _matmul(h, fw_gate, "nn", BF16, "gate_proj")
    cos_t, sin_t = _rope_tables(seq)
    o_b, lse_b = _dil_fwd(proj_b, cos_t, sin_t, seq)
    mn = _norm_fwd(memf, g_mem, "mem_norm")
    kv = _matmul(mn, fw_mem_kv, "nn", BF16, "mem_kv_proj").reshape(bsz, MEM_LEN, D)
    o_c = _mem_fwd(proj_a, kv, seq)
    o_a2, o_b2, o_c2 = o_a.reshape(tokens, 512), o_b.reshape(tokens, 256), o_c.reshape(tokens, 512)
    ys = [_matmul(o_a2, fw_br_sb, "nn", BF16, "branch_sb"), _matmul(o_b2, fw_br_dil, "nn", BF16, "branch_dil"),
          _matmul(o_c2, fw_br_mem, "nn", BF16, "branch_mem")]
    merged = _gate_merge(gpre, ys, b_gate)
    mix, x1, h2 = _out_proj_norm(merged, fw_o, xf, g_post_mix, g_pre_ffn)
    gu = _matmul(h2, fw_ffn_in, "nn", BF16, "ffn_in")
    f = _swiglu_fwd(gu)
    dy, dfo, dg_post_ffn, loss_lanes = _ffn_out_loss(f, fw_ffn_out, x1, tgt, g_post_ffn)

    gw_ffn_out = _matmul(f, dfo, "tn", BF16, "gw_ffn_out")
    dgu = _d_ffn_swiglu_bwd(dfo, fw_ffn_out, gu)
    gw_ffn_in = _matmul(h2, dgu, "tn", BF16, "gw_ffn_in")
    dx1, dmix, dg_pre_ffn, dg_post_mix = _d_h2_norm_bwd(dgu, fw_ffn_in, x1, dy, mix, g_pre_ffn, g_post_mix)
    dmerged = _matmul(dmix, fw_o, "nt", BF16, "d_merged")
    gw_o = _matmul(merged, dmix, "tn", BF16, "gw_o")
    dya, dyb, dyc, dgpre, db_gate = _gate_bwd(dmerged, gpre, ys, b_gate)
    d_oa = _matmul(dya, fw_br_sb, "nt", BF16, "d_o_sb").reshape(bsz, seq, 512)
    d_ob = _matmul(dyb, fw_br_dil, "nt", BF16, "d_o_dil").reshape(bsz, seq, 256)
    d_oc = _matmul(dyc, fw_br_mem, "nt", BF16, "d_o_mem").reshape(bsz, seq, 512)
    gw_br_sb = _matmul(o_a2, dya, "tn", BF16, "gw_br_sb")
    gw_br_dil = _matmul(o_b2, dyb, "tn", BF16, "gw_br_dil")
    gw_br_mem = _matmul(o_c2, dyc, "tn", BF16, "gw_br_mem")
    gw_gate = _matmul(h, dgpre, "tn", BF16, "gw_gate")
    grads = {2: gw_br_sb, 3: gw_br_dil, 4: gw_br_mem, 5: gw_gate, 6: gw_o, 7: gw_ffn_in, 8: gw_ffn_out}
    d_proj_a, recv_behind = _sb_bwd(proj_a, d_oa, o_a32, seq, _grad_parts(grads, REDUCE_BEHIND, shard_shapes))
    d_proj_a, dk_m, dv_m = _mem_bwd(proj_a, kv, d_oc, d_proj_a, seq)
    d_proj_b = _dil_bwd(proj_b, cos_t, sin_t, d_ob, o_b, lse_b, seq).reshape(tokens, WB)
    d_proj_a = d_proj_a.reshape(tokens, WA)
    dh = _matmul(dgpre, fw_gate, "nt", F32, "d_h_gate")
    dh = _matmul(d_proj_a, w_a, "nt", F32, "d_h_a", add=dh)
    dx, dg_pre_mix = _d_h_norm_bwd(d_proj_b, w_b, dh, xf, dx1, g_pre_mix)
    gw_a = _matmul(h, d_proj_a, "tn", BF16, "gw_in_a")
    gw_b = _matmul(h, d_proj_b, "tn", BF16, "gw_in_b")
    dkv = jnp.concatenate([dk_m, dv_m], axis=-1).reshape(bsz * MEM_LEN, D)
    gw_mem_kv = _matmul(mn, dkv, "tn", BF16, "gw_mem_kv")
    dmn = _matmul(dkv, fw_mem_kv, "nt", F32, "d_mem_norm")
    dg_mem = _gain_grad(dmn, memf)

    gw_ab = jnp.concatenate([gw_a, gw_b], axis=1)
    where = {c: i for i, c in enumerate(CHUNKS_A + CHUNKS_B)}
    grads = {0: _pick_chunks(gw_ab, [where[c] for c in range(34)]), 1: gw_mem_kv}
    recv_last = _exchange(_grad_parts(grads, REDUCE_LAST, shard_shapes), False, "grad_exchange")
    big = [{}, {}, {}, {}]
    for recv, idxs, name in ((recv_behind, REDUCE_BEHIND, "reduce_adamw_behind"), (recv_last, REDUCE_LAST, "reduce_adamw_last")):
        packed = _reduce_adamw(recv, *(_pack([t[i][0] for i in idxs], F32) for t in (big_w, big_m, big_v)), name)
        for group, p in zip(big, packed):
            group.update(_unpack_shards(p, idxs, shard_shapes))
    big = [[group[i] for i in range(len(big_w))] for group in big]

    small = jnp.concatenate([dg_pre_mix, dg_post_mix, dg_pre_ffn, dg_post_ffn, dg_mem, db_gate.reshape(3, D),
                             loss_lanes, jnp.zeros((7, D), F32)], axis=0)
    small_all = _exchange(small, True, "small_all_gather")

    def small_pack(gs, b):
        return jnp.concatenate([*gs, b.reshape(3, D)], axis=0)

    sm = _small_adamw(
        small_all, small_pack([g_pre_mix, g_post_mix, g_pre_ffn, g_post_ffn, g_mem], b_gate),
        small_pack([m_g_pre_mix, m_g_post_mix, m_g_pre_ffn, m_g_post_ffn, m_g_mem], m_b_gate),
        small_pack([v_g_pre_mix, v_g_post_mix, v_g_pre_ffn, v_g_post_ffn, v_g_mem], v_b_gate))
    loss = sm[4][0, 0]

    def leaves(k):
        t, bw = sm[k], big[k]
        return [t[0:1], t[1:2], t[2:3], t[3:4], t[4:5], *bw[0:6], t[5:8].reshape(1, 3 * D), *bw[6:9]]

    return (loss, dx.reshape(bsz, seq, D), *leaves(0), *leaves(1), *leaves(2), *leaves(3))
```

```python
import functools

import jax
import jax.numpy as jnp
from jax import lax
from jax.experimental import pallas as pl
from jax.experimental.pallas import tpu as pltpu

F32 = jnp.float32
BF16 = jnp.bfloat16
D = 1024
BLK = 128
MEM_LEN = 256
D_FF = 2816
NORM_EPS = 1e-6
NEG_INF = -1e30
ROPE_THETA = 10000.0
ADAM_LR, ADAM_B1, ADAM_B2, ADAM_EPS, ADAM_WD, ADAM_STEP = 0.001, 0.9, 0.999, 1e-08, 0.01, 10
N_DEV = 8
VMEM_LIMIT_BYTES = 56 * 1024 * 1024
MESH = pl.DeviceIdType.MESH
ANY = pl.BlockSpec(memory_space=pl.ANY)

NT = (((1,), (1,)), ((), ()))
TN = (((0,), (0,)), ((), ()))
NN = (((1,), (0,)), ((), ()))
_DIMS = {"nn": NN, "nt": NT, "tn": TN}

BIG_NAMES = ("w_in", "w_mem_kv", "w_br_sb", "w_br_dil", "w_br_mem", "w_gate", "w_o", "w_ffn_in", "w_ffn_out")
BY_ROWS = (False, True, False, False, False, False, True, False, True)
GATHER_FIRST = (0,)
GATHER_BEHIND = (1, 2, 3, 4, 5, 6, 7, 8)
REDUCE_BEHIND = (2, 3, 4, 5, 6, 7, 8)
REDUCE_LAST = (0, 1)

CHUNKS_A = tuple(c for hp in range(4) for c in (hp, 4 + hp, 8 + hp)) + (30, 31, 32, 33)
CHUNKS_B = tuple(c for hp in range(2) for g in range(3) for c in (12 + 6 * g + hp, 14 + 6 * g + hp, 16 + 6 * g + hp))
WA, WB = 128 * len(CHUNKS_A), 128 * len(CHUNKS_B)
DIL_GROUPS = (1, 4, 16)


def _params(*sem):
    return pltpu.CompilerParams(dimension_semantics=sem or None, vmem_limit_bytes=VMEM_LIMIT_BYTES)


def _tile(n, cap):
    if n <= 128:
        return n
    assert n % 128 == 0, n
    best = 128
    for t in range(128, min(n, cap) + 1, 128):
        if n % t == 0:
            best = t
    return best


def _matmul(a, b, mode, out_dtype, name, add=None, tm_cap=1536, tn_cap=1536, tk_cap=1536):
    if mode == "tn":
        (K, M), N = a.shape, b.shape[1]
    elif mode == "nt":
        (M, K), N = a.shape, b.shape[0]
    else:
        (M, K), N = a.shape, b.shape[1]
    tm, tn, tk = _tile(M, tm_cap), _tile(N, tn_cap), _tile(K, tk_cap)
    nm, nn, nk = M // tm, N // tn, K // tk
    dims = _DIMS[mode]
    n_add = 0 if add is None else 1

    def body(a_ref, b_ref, *rest):
        o_ref = rest[n_add]

        def finish(v):
            if n_add:
                v = v + rest[0][...]
            o_ref[...] = v.astype(o_ref.dtype)

        p = lax.dot_general(a_ref[...], b_ref[...], dims, preferred_element_type=F32)
        if nk == 1:
            finish(p)
        else:
            acc_ref = rest[n_add + 1]
            k = pl.program_id(2)

            @pl.when(k == 0)
            def _():
                acc_ref[...] = p

            @pl.when(k > 0)
            def _():
                acc_ref[...] += p

            @pl.when(k == nk - 1)
            def _():
                finish(acc_ref[...])

    n_outer = nk == 1 and (a.size * nn + b.size) < (a.size + b.size * nm)
    if n_outer:
        grid, ij = (nn, nm, nk), (lambda g0, g1: (g1, g0))
    else:
        grid, ij = (nm, nn, nk), (lambda g0, g1: (g0, g1))
    if mode == "tn":
        a_spec = pl.BlockSpec((tk, tm), lambda g0, g1, k: (k, ij(g0, g1)[0]))
    else:
        a_spec = pl.BlockSpec((tm, tk), lambda g0, g1, k: (ij(g0, g1)[0], k))
    if mode == "nt":
        b_spec = pl.BlockSpec((tn, tk), lambda g0, g1, k: (ij(g0, g1)[1], k))
    else:
        b_spec = pl.BlockSpec((tk, tn), lambda g0, g1, k: (k, ij(g0, g1)[1]))
    o_spec = pl.BlockSpec((tm, tn), lambda g0, g1, k: ij(g0, g1))
    return pl.pallas_call(
        body, name=name, grid=grid,
        out_shape=jax.ShapeDtypeStruct((M, N), out_dtype),
        in_specs=[a_spec, b_spec] + [o_spec] * n_add,
        out_specs=o_spec,
        scratch_shapes=[pltpu.VMEM((tm, tn), F32)] if nk > 1 else [],
        compiler_params=_params("parallel", "parallel", "arbitrary"),
    )(a, b, *([add] if n_add else []))


def _rowwise(body, name, rows, tr, row_ins, vec_ins, row_outs, acc_outs=()):
    tr = min(tr, rows)
    assert rows % tr == 0
    in_specs, args = [], []
    for r in row_ins:
        arr, w, cb = r if isinstance(r, tuple) else (r, r.shape[1], 0)
        in_specs.append(pl.BlockSpec((tr, w), functools.partial(lambda i, cb: (i, cb), cb=cb)))
        args.append(arr)
    for v in vec_ins:
        in_specs.append(pl.BlockSpec(v.shape, lambda i: (0, 0)))
        args.append(v)
    out_shape = [jax.ShapeDtypeStruct((rows, w), dt) for w, dt in row_outs]
    out_shape += [jax.ShapeDtypeStruct((1, w), F32) for w in acc_outs]
    out_specs = [pl.BlockSpec((tr, w), lambda i: (i, 0)) for w, _ in row_outs]
    out_specs += [pl.BlockSpec((1, w), lambda i: (0, 0)) for w in acc_outs]
    n_acc = len(acc_outs)

    def wrapped(*refs):
        if n_acc:
            @pl.when(pl.program_id(0) == 0)
            def _():
                for r in refs[len(refs) - n_acc:]:
                    r[...] = jnp.zeros_like(r)
        body(*refs)

    return pl.pallas_call(
        wrapped, name=name, grid=(rows // tr,), out_shape=out_shape, in_specs=in_specs, out_specs=out_specs,
        compiler_params=_params("arbitrary"),
    )(*args)


def _rstd(x):
    return lax.rsqrt(jnp.mean(x * x, axis=-1, keepdims=True) + NORM_EPS)


def _norm_bwd(u, n, r):
    return r * (u - n * jnp.mean(u * n, axis=-1, keepdims=True))


def _colsum(v):
    return jnp.sum(v, axis=0, keepdims=True)


def _norm_fwd(x, g, name):
    def body(x_ref, g_ref, h_ref):
        xv = x_ref[...]
        h_ref[...] = ((xv * _rstd(xv)) * g_ref[...]).astype(BF16)

    return _rowwise(body, name, x.shape[0], 512, [x], [g], [(D, BF16)])[0]


def _matmul_rows(a, b, mode, name, epilogue, row_ins=(), vec_ins=(), row_outs=(), acc_outs=(), tm=512, tk_cap=1536):
    M, K = a.shape
    N = b.shape[1] if mode == "nn" else b.shape[0]
    tm, tk = min(tm, M), _tile(K, tk_cap)
    nm, nk = M // tm, K // tk
    dims = _DIMS[mode]
    n_extra, n_out = len(row_ins) + len(vec_ins), len(row_outs) + len(acc_outs)

    def body(a_ref, b_ref, *rest):
        extra, outs = rest[:n_extra], rest[n_extra:n_extra + n_out]
        i, k = pl.program_id(0), pl.program_id(1)
        if acc_outs:
            @pl.when(jnp.logical_and(i == 0, k == 0))
            def _():
                for r in outs[len(row_outs):]:
                    r[...] = jnp.zeros_like(r)
        p = lax.dot_general(a_ref[...], b_ref[...], dims, preferred_element_type=F32)
        if nk == 1:
            epilogue(p, *extra, *outs)
        else:
            acc_ref = rest[n_extra + n_out]

            @pl.when(k == 0)
            def _():
                acc_ref[...] = p

            @pl.when(k > 0)
            def _():
                acc_ref[...] += p

            @pl.when(k == nk - 1)
            def _():
                epilogue(acc_ref[...], *extra, *outs)

    b_spec = pl.BlockSpec((tk, N), lambda i, k: (k, 0)) if mode == "nn" else pl.BlockSpec((N, tk), lambda i, k: (0, k))
    in_specs = [pl.BlockSpec((tm, tk), lambda i, k: (i, k)), b_spec]
    in_specs += [pl.BlockSpec((tm, r.shape[1]), lambda i, k: (i, 0)) for r in row_ins]
    in_specs += [pl.BlockSpec(v.shape, lambda i, k: (0, 0)) for v in vec_ins]
    out_shape = [jax.ShapeDtypeStruct((M, w), dt) for w, dt in row_outs]
    out_shape += [jax.ShapeDtypeStruct((1, w), F32) for w in acc_outs]
    out_specs = [pl.BlockSpec((tm, w), lambda i, k: (i, 0)) for w, _ in row_outs]
    out_specs += [pl.BlockSpec((1, w), lambda i, k: (0, 0)) for w in acc_outs]
    return pl.pallas_call(
        body, name=name, grid=(nm, nk), out_shape=out_shape, in_specs=in_specs, out_specs=out_specs,
        scratch_shapes=[pltpu.VMEM((tm, N), F32)] if nk > 1 else [],
        compiler_params=_params("arbitrary", "arbitrary"),
    )(a, b, *row_ins, *vec_ins)


def _out_proj_norm(merged, w_o, x, g_post, g_pre):
    def epilogue(mv, x_ref, g2_ref, g3_ref, mix_ref, x1_ref, h2_ref):
        mix_ref[...] = mv
        x1 = x_ref[...] + (mv * _rstd(mv)) * g2_ref[...]
        x1_ref[...] = x1
        h2_ref[...] = ((x1 * _rstd(x1)) * g3_ref[...]).astype(BF16)

    return _matmul_rows(merged, w_o, "nn", "out_proj_norm", epilogue, [x], [g_post, g_pre],
                        [(D, F32), (D, F32), (D, BF16)])


def _gate_merge(gpre, ys, b_gate):
    def body(gp_ref, ya_ref, yb_ref, yc_ref, b_ref, m_ref):
        acc = None
        for k, y_ref in enumerate((ya_ref, yb_ref, yc_ref)):
            cols = slice(k * D, (k + 1) * D)
            gate = jax.nn.sigmoid(gp_ref[:, cols].astype(F32) + b_ref[:, cols])
            term = gate * y_ref[...].astype(F32)
            acc = term if acc is None else acc + term
        m_ref[...] = acc.astype(BF16)

    return _rowwise(body, "gate_merge", gpre.shape[0], 256, [gpre, *ys], [b_gate], [(D, BF16)])[0]


def _swiglu_fwd(gu):
    def body(a_ref, b_ref, f_ref):
        a = a_ref[...].astype(F32)
        f_ref[...] = (a * jax.nn.sigmoid(a) * b_ref[...].astype(F32)).astype(BF16)

    return _rowwise(body, "swiglu_fwd", gu.shape[0], 256, [(gu, D_FF, 0), (gu, D_FF, 1)], [], [(D_FF, BF16)])[0]


def _ffn_out_loss(f, w_ffn_out, x1, tgt, g_post):
    def epilogue(fo_v, x1_ref, t_ref, g_ref, dy_ref, dfo_ref, dg_ref, loss_ref):
        r = _rstd(fo_v)
        n = fo_v * r
        err = (x1_ref[...] + n * g_ref[...]) - t_ref[...]
        loss_ref[...] += _colsum(err * err)
        dy = err * (1.0 / D)
        dy_ref[...] = dy
        dg_ref[...] += _colsum(dy * n)
        dfo_ref[...] = _norm_bwd(dy * g_ref[...], n, r).astype(BF16)

    return _matmul_rows(f, w_ffn_out, "nn", "ffn_out_loss", epilogue, [x1, tgt], [g_post],
                        [(D, F32), (D, BF16)], (D, D))


def _d_ffn_swiglu_bwd(dfo, w_ffn_out, gu):
    def epilogue(d, gu_ref, dgu_ref):
        a = gu_ref[:, :D_FF].astype(F32)
        b = gu_ref[:, D_FF:].astype(F32)
        s = jax.nn.sigmoid(a)
        dgu_ref[:, :D_FF] = (d * b * (s * (1.0 + a * (1.0 - s)))).astype(BF16)
        dgu_ref[:, D_FF:] = (d * (a * s)).astype(BF16)

    return _matmul_rows(dfo, w_ffn_out, "nt", "d_ffn_swiglu_bwd", epilogue, [gu], [], [(2 * D_FF, BF16)], tm=256)[0]


def _d_h2_norm_bwd(dgu, w_ffn_in, x1, dy, mix, g_pre, g_post):
    def epilogue(dh, x1_ref, dy_ref, mix_ref, g3_ref, g2_ref, dx1_ref, dmix_ref, dg3_ref, dg2_ref):
        x1v = x1_ref[...]
        r3 = _rstd(x1v)
        n3 = x1v * r3
        dg3_ref[...] += _colsum(dh * n3)
        dx1 = dy_ref[...] + _norm_bwd(dh * g3_ref[...], n3, r3)
        dx1_ref[...] = dx1
        mv = mix_ref[...]
        r2 = _rstd(mv)
        n2 = mv * r2
        dg2_ref[...] += _colsum(dx1 * n2)
        dmix_ref[...] = _norm_bwd(dx1 * g2_ref[...], n2, r2).astype(BF16)

    return _matmul_rows(dgu, w_ffn_in, "nt", "d_h2_norm_bwd", epilogue, [x1, dy, mix], [g_pre, g_post],
                        [(D, F32), (D, BF16)], (D, D))


def _gate_bwd(dmerged, gpre, ys, b_gate):
    def body(dm_ref, gp_ref, ya_ref, yb_ref, yc_ref, b_ref, dya_ref, dyb_ref, dyc_ref, dgp_ref, db_ref):
        dm = dm_ref[...].astype(F32)
        for k, (y_ref, dy_ref) in enumerate(((ya_ref, dya_ref), (yb_ref, dyb_ref), (yc_ref, dyc_ref))):
            cols = slice(k * D, (k + 1) * D)
            gate = jax.nn.sigmoid(gp_ref[:, cols].astype(F32) + b_ref[:, cols])
            dy_ref[...] = (dm * gate).astype(BF16)
            dgp = (dm * y_ref[...].astype(F32)) * (gate * (1.0 - gate))
            dgp_ref[:, cols] = dgp.astype(BF16)
            db_ref[:, cols] += _colsum(dgp)

    return _rowwise(body, "gate_bwd", gpre.shape[0], 256, [dmerged, gpre, *ys], [b_gate],
                    [(D, BF16), (D, BF16), (D, BF16), (3 * D, BF16)], (3 * D,))


def _d_h_norm_bwd(d_proj, w, dh_rest, x, dx1, g_pre):
    def epilogue(dh_part, rest_ref, x_ref, dx1_ref, g_ref, dx_ref, dg_ref):
        dh = dh_part + rest_ref[...]
        xv = x_ref[...]
        r = _rstd(xv)
        n = xv * r
        dg_ref[...] += _colsum(dh * n)
        dx_ref[...] = dx1_ref[...] + _norm_bwd(dh * g_ref[...], n, r)

    return _matmul_rows(d_proj, w, "nt", "d_h_norm_bwd", epilogue, [dh_rest, x, dx1], [g_pre], [(D, F32)], (D,))


def _gain_grad(dmn, mem):
    def body(d_ref, m_ref, dg_ref):
        mv = m_ref[...]
        dg_ref[...] += _colsum(d_ref[...] * (mv * _rstd(mv)))

    return _rowwise(body, "mem_gain_grad", mem.shape[0], 256, [dmn, mem], [], [], (D,))[0]


def _head_rowsum(v, head0):
    return (jnp.sum(jnp.where(head0, v, 0.0), axis=1, keepdims=True),
            jnp.sum(jnp.where(head0, 0.0, v), axis=1, keepdims=True))


KT = 256
SB_SCALE = 0.125


def _make_suffix():
    tri = (lax.broadcasted_iota(jnp.int32, (KT, KT), 0) > lax.broadcasted_iota(jnp.int32, (KT, KT), 1)).astype(BF16)
    tri2 = jnp.concatenate([tri, tri], axis=0)

    def suffix(x):
        hi = x.astype(BF16)
        lo = (x - hi.astype(F32)).astype(BF16)
        return jnp.dot(jnp.concatenate([hi, lo], axis=1), tri2, preferred_element_type=F32)

    return suffix


def _sb_scores(qh, k, mask, suffix, run):
    z = lax.dot_general(qh, k, NT, preferred_element_type=F32)
    zc = jnp.minimum(z, 60.0)
    sp = jnp.log(1.0 + jnp.exp(zc))
    lb = zc - sp
    lk = -sp
    if mask is not None:
        lk = jnp.where(mask, lk, 0.0)
    a = jnp.exp(lb + suffix(lk) + run)
    if mask is not None:
        a = jnp.where(mask, a, 0.0)
    return lb, lk, a


QB = KT


def _sb_tiles(i, tile, init):
    st = tile(i, init, True)
    st = lax.fori_loop(0, lax.shift_right_logical(i, 1),
                       lambda t, s: tile(i - 2 - 2 * t, tile(i - 1 - 2 * t, s, False), False), st)
    return lax.cond((i & 1) == 1, lambda s: tile(0, s, False), lambda s: s, st)


def _sb_consts():
    head0 = lax.broadcasted_iota(jnp.int32, (QB, BLK), 1) < 64
    row = lax.broadcasted_iota(jnp.int32, (2 * QB, KT), 0) & (QB - 1)
    return head0, row > lax.broadcasted_iota(jnp.int32, (2 * QB, KT), 1)


def _stack_heads(v, head0):
    zero = jnp.zeros_like(v)
    return jnp.concatenate([jnp.where(head0, v, zero), jnp.where(head0, zero, v)], axis=0)


def _unstack_heads(v, head0):
    n = v.shape[0] // 2
    return jnp.where(head0, v[:n], v[n:])


def _sb_fwd(proj_a, seq, ride):
    bsz = proj_a.shape[0]
    n_ride = len(ride)

    def body(x_ref, *rest):
        ride_refs, (o_ref, o32_ref), rest = rest[:n_ride], rest[n_ride:n_ride + 2], rest[n_ride + 2:]
        gathered_refs, acc_ref, sems = rest[:n_ride], rest[n_ride], rest[n_ride + 1:]
        finish_ride = _riding_exchange(ride_refs, gathered_refs, sems, gather=True)
        head0, diag_mask = _sb_consts()
        suffix = _make_suffix()

        def qblock(i, carry):
            r0 = pl.multiple_of(i * QB, QB)
            qs = _stack_heads(x_ref[pl.ds(r0, QB), 0:128] * jnp.asarray(SB_SCALE, BF16), head0)

            def tile(jt, run, masked):
                c0 = pl.multiple_of(jt * KT, KT)
                k = x_ref[pl.ds(c0, KT), 128:256]
                v = x_ref[pl.ds(c0, KT), 256:384]
                _, lk, a = _sb_scores(qs, k, diag_mask if masked else None, suffix, run)
                pv = jnp.dot(a.astype(BF16), v, preferred_element_type=F32)
                if masked:
                    acc_ref[...] = pv
                else:
                    acc_ref[...] += pv
                return run + jnp.sum(lk, axis=1, keepdims=True)

            _sb_tiles(i, tile, jnp.zeros((2 * QB, 1), F32))
            o = _unstack_heads(acc_ref[...], head0)
            o32_ref[pl.ds(r0, QB), :] = o
            o_ref[pl.ds(r0, QB), :] = o.astype(BF16)
            return carry

        lax.fori_loop(0, seq // QB, qblock, 0)
        finish_ride()

    out_spec = pl.BlockSpec((None, seq, BLK), lambda b, hp: (b, 0, hp))
    return pl.pallas_call(
        body, name="sb_attn_fwd", grid=(bsz, 4),
        out_shape=[jax.ShapeDtypeStruct((bsz, seq, 512), BF16), jax.ShapeDtypeStruct((bsz, seq, 512), F32),
                   *_exchange_out(ride, True)],
        in_specs=[pl.BlockSpec((None, seq, 384), lambda b, hp: (b, 0, hp))] + [ANY] * n_ride,
        out_specs=[out_spec, out_spec] + [ANY] * n_ride,
        scratch_shapes=[pltpu.VMEM((2 * QB, BLK), F32), *_exchange_sems(n_ride)],
        compiler_params=_params("arbitrary", "arbitrary"),
    )(proj_a, *ride)


def _sb_bwd(proj_a, d_o, o_a, seq, ride):
    bsz = proj_a.shape[0]
    n_ride = len(ride)

    def body(x_ref, do_ref, o_ref, *rest):
        ride_refs, d_ref, rest = rest[:n_ride], rest[n_ride], rest[n_ride + 1:]
        received_refs, (dq_acc, dk_acc, dv_acc), sems = rest[:n_ride], rest[n_ride:n_ride + 3], rest[n_ride + 3:]
        finish_ride = _riding_exchange(ride_refs, received_refs, sems, gather=False)
        head0, diag_mask = _sb_consts()
        suffix = _make_suffix()
        dk_acc[...] = jnp.zeros_like(dk_acc)
        dv_acc[...] = jnp.zeros_like(dv_acc)

        def qblock(i, carry):
            r0 = pl.multiple_of(i * QB, QB)
            qs = _stack_heads(x_ref[pl.ds(r0, QB), 0:128] * jnp.asarray(SB_SCALE, BF16), head0)
            do = do_ref[pl.ds(r0, QB), :]
            dos = _stack_heads(do, head0)
            dsum = jnp.concatenate(_head_rowsum(do.astype(F32) * o_ref[pl.ds(r0, QB), :], head0), axis=0)

            def tile(jt, st, masked):
                run, grun = st
                c0 = pl.multiple_of(jt * KT, KT)
                k = x_ref[pl.ds(c0, KT), 128:256]
                v = x_ref[pl.ds(c0, KT), 256:384]
                lb, lk, a = _sb_scores(qs, k, diag_mask if masked else None, suffix, run)
                a16 = a.astype(BF16)
                g = a16.astype(F32) * lax.dot_general(dos, v, NT, preferred_element_type=F32)
                before = dsum - ((grun + suffix(g)) + g)
                dz = g - jnp.exp(lb) * (g + before)
                if masked:
                    dz = jnp.where(diag_mask, dz, 0.0)
                dz = dz.astype(BF16)
                dq = jnp.dot(dz, k, preferred_element_type=F32)
                if masked:
                    dq_acc[...] = dq
                else:
                    dq_acc[...] += dq
                dk_acc[pl.ds(c0, KT), :] += lax.dot_general(dz, qs, TN, preferred_element_type=F32)
                dv_acc[pl.ds(c0, KT), :] += lax.dot_general(a16, dos, TN, preferred_element_type=F32)
                return run + jnp.sum(lk, axis=1, keepdims=True), grun + jnp.sum(g, axis=1, keepdims=True)

            z1 = jnp.zeros((2 * QB, 1), F32)
            _sb_tiles(i, tile, (z1, z1))
            d_ref[pl.ds(r0, QB), 0:128] = (_unstack_heads(dq_acc[...], head0) * SB_SCALE).astype(BF16)
            return carry

        lax.fori_loop(0, seq // QB, qblock, 0)
        d_ref[:, 128:256] = dk_acc[...].astype(BF16)
        d_ref[:, 256:384] = dv_acc[...].astype(BF16)
        finish_ride()

    return pl.pallas_call(
        body, name="sb_attn_bwd", grid=(bsz, 4),
        out_shape=[jax.ShapeDtypeStruct((bsz, seq, WA), BF16), *_exchange_out(ride, False)],
        in_specs=[pl.BlockSpec((None, seq, 384), lambda b, hp: (b, 0, hp)),
                  pl.BlockSpec((None, seq, BLK), lambda b, hp: (b, 0, hp)),
                  pl.BlockSpec((None, seq, BLK), lambda b, hp: (b, 0, hp))] + [ANY] * n_ride,
        out_specs=[pl.BlockSpec((None, seq, 384), lambda b, hp: (b, 0, hp))] + [ANY] * n_ride,
        scratch_shapes=[pltpu.VMEM((2 * QB, BLK), F32), pltpu.VMEM((seq, BLK), F32), pltpu.VMEM((seq, BLK), F32),
                        *_exchange_sems(n_ride)],
        compiler_params=_params("arbitrary", "arbitrary"),
    )(proj_a, d_o, o_a, *ride)


def _rope_tables(seq):
    inv_freq = ROPE_THETA ** (-jnp.arange(32, dtype=F32) * 2.0 / 64)
    ang = jnp.arange(seq).astype(F32)[:, None] * inv_freq[None, :]
    cos, sin = jnp.cos(ang), jnp.sin(ang)
    return jnp.tile(cos, (1, 4)), jnp.concatenate([-sin, sin, -sin, sin], axis=1)


def _make_rope(n_rows):
    lane = lax.broadcasted_iota(jnp.int32, (n_rows, BLK), 1)
    first = (lane & 63) < 32

    def rope(x, cos, sin):
        partner = jnp.where(first, pltpu.roll(x, 96, 1), pltpu.roll(x, 32, 1))
        return x * cos + partner * sin

    return rope


DIL_UNROLL = 4


def _dil_consts():
    head0 = lax.broadcasted_iota(jnp.int32, (BLK, BLK), 1) < 64
    row = lax.broadcasted_iota(jnp.int32, (2 * BLK, 2 * BLK), 0) & (BLK - 1)
    col = lax.broadcasted_iota(jnp.int32, (2 * BLK, 2 * BLK), 1)
    valid_prev = jnp.logical_and(col < BLK, col >= row)
    valid_cur = jnp.logical_and(col >= BLK, row >= col - BLK)
    return head0, valid_prev, valid_cur


def _dil_blocks(dil, seq, block):
    nq = seq // dil // BLK

    def rows(r, i):
        if dil == 1:
            return pl.ds(pl.multiple_of(i * BLK, BLK), BLK)
        return pl.ds(r + (dil * BLK) * i, BLK, stride=dil)

    def step(t, carry):
        for u in range(DIL_UNROLL):
            n = t * DIL_UNROLL + u
            r, i = lax.div(n, nq), lax.rem(n, nq)
            block(rows(r, i), rows(r, jnp.maximum(i - 1, 0)), i)
        return carry

    lax.fori_loop(0, seq // BLK // DIL_UNROLL, step, 0)


def _dil_scores(qf, kf, vf, cur, prev, i, consts):
    head0, valid_prev, valid_cur = consts
    qs = _stack_heads(qf[cur, :].astype(BF16), head0)
    kcat = jnp.concatenate([kf[prev, :], kf[cur, :]], axis=0).astype(BF16)
    vcat = jnp.concatenate([vf[prev, :], vf[cur, :]], axis=0).astype(BF16)
    valid = jnp.logical_or(valid_cur, jnp.logical_and(valid_prev, i > 0))
    s = lax.dot_general(qs, kcat, NT, preferred_element_type=F32) * 0.125
    return qs, kcat, vcat, s, valid


def _head_cols(v):
    return jnp.concatenate([v[:, 0:1], v[:, 64:65]], axis=0)


def _dil_load_qkv(x_ref, c, rope, cos, sin, qf, kf, vf):
    qf[...] = rope(x_ref[:, c:c + 128].astype(F32), cos, sin).astype(BF16).astype(F32)
    kf[...] = rope(x_ref[:, c + 128:c + 256].astype(F32), cos, sin).astype(BF16).astype(F32)
    vf[...] = x_ref[:, c + 256:c + 384].astype(F32)


def _dil_fwd(proj_b, cos_t, sin_t, seq):
    bsz = proj_b.shape[0]

    def body(x_ref, cos_ref, sin_ref, ob_ref, lse_ref, qf, kf, vf, og, lg):
        consts = _dil_consts()
        head0 = consts[0]
        rope = _make_rope(seq)
        cos, sin = cos_ref[...], sin_ref[...]
        for g, dil in enumerate(DIL_GROUPS):
            _dil_load_qkv(x_ref, 384 * g, rope, cos, sin, qf, kf, vf)

            def block(cur, prev, i, g=g):
                _, _, vcat, s, valid = _dil_scores(qf, kf, vf, cur, prev, i, consts)
                s = jnp.where(valid, s, NEG_INF)
                m = jnp.max(s, axis=1, keepdims=True)
                p = jnp.exp(s - m)
                den = jnp.sum(p, axis=1, keepdims=True)
                o = jnp.dot(p.astype(BF16), vcat, preferred_element_type=F32) / den
                og[g, cur, :] = _unstack_heads(o, head0)
                lg[g, cur, :] = _unstack_heads(jnp.broadcast_to(m + jnp.log(den), (2 * BLK, BLK)), head0)

            _dil_blocks(dil, seq, block)
        ls = [lg[0], lg[1], lg[2]]
        m = jnp.maximum(jnp.maximum(ls[0], ls[1]), ls[2])
        ws = [jnp.exp(l - m) for l in ls]
        den = (ws[0] + ws[1]) + ws[2]
        ob_ref[...] = (((ws[0] * og[0] + ws[1] * og[1]) + ws[2] * og[2]) / den).astype(BF16)
        lse_ref[...] = m + jnp.log(den)

    tab_spec = pl.BlockSpec((seq, BLK), lambda b, hp: (0, 0))
    out_spec = pl.BlockSpec((None, seq, BLK), lambda b, hp: (b, 0, hp))
    slab = pltpu.VMEM((seq, BLK), F32)
    return pl.pallas_call(
        body, name="dil_attn_fwd", grid=(bsz, 2),
        out_shape=(jax.ShapeDtypeStruct((bsz, seq, 256), BF16), jax.ShapeDtypeStruct((bsz, seq, 256), F32)),
        in_specs=[pl.BlockSpec((None, seq, WB // 2), lambda b, hp: (b, 0, hp)), tab_spec, tab_spec],
        out_specs=(out_spec, out_spec),
        scratch_shapes=[slab, slab, slab, pltpu.VMEM((3, seq, BLK), F32), pltpu.VMEM((3, seq, BLK), F32)],
        compiler_params=_params("parallel", "parallel"),
    )(proj_b, cos_t, sin_t)


def _dil_bwd(proj_b, cos_t, sin_t, d_ob, o_b, lse, seq):
    bsz = proj_b.shape[0]

    def body(x_ref, cos_ref, sin_ref, do_ref, ob_ref, lse_ref, d_ref, qf, kf, vf, dof, dsf, dq_s, dk_acc, dv_acc):
        consts = _dil_consts()
        head0 = consts[0]
        rope = _make_rope(seq)
        cos, sin = cos_ref[...], sin_ref[...]
        do_all = do_ref[...].astype(F32)
        dof[...] = do_all
        head0_all = lax.broadcasted_iota(jnp.int32, (seq, BLK), 1) < 64
        d0, d1 = _head_rowsum(do_all * ob_ref[...].astype(F32), head0_all)
        dsf[...] = jnp.where(head0_all, d0, d1)
        for g, dil in enumerate(DIL_GROUPS):
            _dil_load_qkv(x_ref, 384 * g, rope, cos, sin, qf, kf, vf)
            dk_acc[...] = jnp.zeros_like(dk_acc)
            dv_acc[...] = jnp.zeros_like(dv_acc)

            def block(cur, prev, i):
                qs, kcat, vcat, s, valid = _dil_scores(qf, kf, vf, cur, prev, i, consts)
                dos = _stack_heads(dof[cur, :].astype(BF16), head0)
                p = jnp.where(valid, jnp.exp(s - _head_cols(lse_ref[cur, :])), 0.0)
                dp = lax.dot_general(dos, vcat, NT, preferred_element_type=F32)
                ds = ((p * (dp - _head_cols(dsf[cur, :]))) * 0.125).astype(BF16)
                dq_s[cur, :] = _unstack_heads(jnp.dot(ds, kcat, preferred_element_type=F32), head0)
                dk = lax.dot_general(ds, qs, TN, preferred_element_type=F32)
                dv = lax.dot_general(p.astype(BF16), dos, TN, preferred_element_type=F32)
                dk_acc[prev, :] += dk[:BLK]
                dk_acc[cur, :] += dk[BLK:]
                dv_acc[prev, :] += dv[:BLK]
                dv_acc[cur, :] += dv[BLK:]

            _dil_blocks(dil, seq, block)
            c = 384 * g
            d_ref[:, c:c + 128] = rope(dq_s[...], cos, -sin).astype(BF16)
            d_ref[:, c + 128:c + 256] = rope(dk_acc[...], cos, -sin).astype(BF16)
            d_ref[:, c + 256:c + 384] = dv_acc[...].astype(BF16)

    x_spec = pl.BlockSpec((None, seq, WB // 2), lambda b, hp: (b, 0, hp))
    tab_spec = pl.BlockSpec((seq, BLK), lambda b, hp: (0, 0))
    tok_spec = pl.BlockSpec((None, seq, BLK), lambda b, hp: (b, 0, hp))
    return pl.pallas_call(
        body, name="dil_attn_bwd", grid=(bsz, 2),
        out_shape=jax.ShapeDtypeStruct((bsz, seq, WB), BF16),
        in_specs=[x_spec, tab_spec, tab_spec, tok_spec, tok_spec, tok_spec], out_specs=x_spec,
        scratch_shapes=[pltpu.VMEM((seq, BLK), F32)] * 8,
        compiler_params=_params("parallel", "parallel"),
    )(proj_b, cos_t, sin_t, d_ob, o_b, lse)


MEM_SCALE = 128 ** -0.5
MEM_QB = 1024


def _mem_fwd(proj_a, kv, seq):
    bsz = proj_a.shape[0]

    def body(q_ref, k_ref, v_ref, o_ref):
        k, v = k_ref[...], v_ref[...]

        def qblock(i, carry):
            r0 = pl.multiple_of(i * MEM_QB, MEM_QB)
            s = lax.dot_general(q_ref[pl.ds(r0, MEM_QB), :], k, NT, preferred_element_type=F32) * MEM_SCALE
            p = jnp.exp(s - jnp.max(s, axis=1, keepdims=True))
            p = p / jnp.sum(p, axis=1, keepdims=True)
            o_ref[pl.ds(r0, MEM_QB), :] = jnp.dot(p.astype(BF16), v, preferred_element_type=F32).astype(BF16)
            return carry

        lax.fori_loop(0, seq // MEM_QB, qblock, 0)

    return pl.pallas_call(
        body, name="mem_attn_fwd", grid=(bsz, 4),
        out_shape=jax.ShapeDtypeStruct((bsz, seq, 512), BF16),
        in_specs=[pl.BlockSpec((None, seq, BLK), lambda b, h: (b, 0, 12 + h)),
                  pl.BlockSpec((None, MEM_LEN, BLK), lambda b, h: (b, 0, h)),
                  pl.BlockSpec((None, MEM_LEN, BLK), lambda b, h: (b, 0, 4 + h))],
        out_specs=pl.BlockSpec((None, seq, BLK), lambda b, h: (b, 0, h)),
        compiler_params=_params("parallel", "parallel"),
    )(proj_a, kv, kv)


def _mem_bwd(proj_a, kv, d_o, d_proj_a, seq):
    bsz = proj_a.shape[0]

    def body(q_ref, k_ref, v_ref, do_ref, _, dq_ref, dk_ref, dv_ref):
        k, v = k_ref[...], v_ref[...]

        def qblock(i, carry):
            dk, dv = carry
            r0 = pl.multiple_of(i * MEM_QB, MEM_QB)
            q, do = q_ref[pl.ds(r0, MEM_QB), :], do_ref[pl.ds(r0, MEM_QB), :]
            s = lax.dot_general(q, k, NT, preferred_element_type=F32) * MEM_SCALE
            p = jnp.exp(s - jnp.max(s, axis=1, keepdims=True))
            p = p / jnp.sum(p, axis=1, keepdims=True)
            dp = lax.dot_general(do, v, NT, preferred_element_type=F32)
            ds = ((p * (dp - jnp.sum(p * dp, axis=1, keepdims=True))) * MEM_SCALE).astype(BF16)
            dq_ref[pl.ds(r0, MEM_QB), :] = jnp.dot(ds, k, preferred_element_type=F32).astype(BF16)
            dk = dk + lax.dot_general(ds, q, TN, preferred_element_type=F32)
            dv = dv + lax.dot_general(p.astype(BF16), do, TN, preferred_element_type=F32)
            return dk, dv

        zero = jnp.zeros((MEM_LEN, BLK), F32)
        dk, dv = lax.fori_loop(0, seq // MEM_QB, qblock, (zero, zero))
        dk_ref[...] = dk.astype(BF16)
        dv_ref[...] = dv.astype(BF16)

    kv_spec = pl.BlockSpec((None, MEM_LEN, BLK), lambda b, h: (b, 0, h))
    return pl.pallas_call(
        body, name="mem_attn_bwd", grid=(bsz, 4),
        out_shape=(jax.ShapeDtypeStruct((bsz, seq, WA), BF16), jax.ShapeDtypeStruct((bsz, MEM_LEN, 512), BF16),
                   jax.ShapeDtypeStruct((bsz, MEM_LEN, 512), BF16)),
        in_specs=[pl.BlockSpec((None, seq, BLK), lambda b, h: (b, 0, 12 + h)), kv_spec,
                  pl.BlockSpec((None, MEM_LEN, BLK), lambda b, h: (b, 0, 4 + h)),
                  pl.BlockSpec((None, seq, BLK), lambda b, h: (b, 0, h)), ANY],
        out_specs=(pl.BlockSpec((None, seq, BLK), lambda b, h: (b, 0, 12 + h)), kv_spec, kv_spec),
        input_output_aliases={4: 0},
        compiler_params=_params("parallel", "parallel"),
    )(proj_a, kv, kv, d_o, d_proj_a)


def _mesh_pos():
    return lax.axis_index("x"), lax.axis_index("y"), lax.axis_index("c")


def _all_gather(shard, name):
    m_per, n = shard.shape

    def body(x_ref, out_ref, send_sems, recv_sems, local_sem):
        x, y, c = _mesh_pos()
        me, sibling = (x, y, c), (x, y, 1 - c)
        chips = [(1 - x, y), (x, 1 - y), (1 - x, 1 - y)]

        def rows(px, py, pc):
            return out_ref.at[pl.ds((4 * px + 2 * py + pc) * m_per, m_per), :]

        def copy(k, block, to, src=None):
            return pltpu.make_async_remote_copy(
                src_ref=rows(*block) if src is None else src, dst_ref=rows(*block),
                send_sem=send_sems.at[k], recv_sem=recv_sems.at[k], device_id=to, device_id_type=MESH)

        mine = pltpu.make_async_copy(x_ref, rows(*me), local_sem)
        mine.start()
        first = [copy(0, me, sibling, src=x_ref)]
        first += [copy(1 + j, me, (*chip, c), src=x_ref) for j, chip in enumerate(chips)]
        for cp in first:
            cp.start()
        passed = [copy(4 + j, (*chip, c), sibling) for j, chip in enumerate(chips)]
        for j, chip in enumerate(chips):
            copy(1 + j, (*chip, c), me).wait_recv()
            passed[j].start()
        copy(0, sibling, me).wait_recv()
        for j, chip in enumerate(chips):
            copy(4 + j, (*chip, 1 - c), me).wait_recv()
        for cp in first + passed:
            cp.wait_send()
        mine.wait()

    return pl.pallas_call(
        body, name=name, out_shape=jax.ShapeDtypeStruct((N_DEV * m_per, n), shard.dtype),
        in_specs=[ANY], out_specs=ANY,
        scratch_shapes=[pltpu.SemaphoreType.DMA((7,)), pltpu.SemaphoreType.DMA((7,)), pltpu.SemaphoreType.DMA(())],
    )(shard)


def _exchange_sems(n_arrays):
    return [pltpu.SemaphoreType.DMA((7 * n_arrays,)), pltpu.SemaphoreType.DMA((7 * n_arrays,)),
            pltpu.SemaphoreType.DMA((n_arrays,))]


def _exchange_out(srcs, gather):
    return [jax.ShapeDtypeStruct((N_DEV, *s.shape[-2:]), s.dtype) for s in srcs]


def _direct_exchange(src_refs, dst_refs, send_sems, recv_sems, local_sems, gather):
    x, y, c = _mesh_pos()
    me = 4 * x + 2 * y + c
    owns, sends, recvs = [], [], []
    for a, (src, dst) in enumerate(zip(src_refs, dst_refs)):
        owns.append(pltpu.make_async_copy(src if gather else src.at[me], dst.at[me], local_sems.at[a]))
        for j in range(1, N_DEV):
            px = 1 - x if j & 4 else x
            py = 1 - y if j & 2 else y
            pc = 1 - c if j & 1 else c
            peer = 4 * px + 2 * py + pc
            sems = dict(send_sem=send_sems.at[7 * a + j - 1], recv_sem=recv_sems.at[7 * a + j - 1],
                        device_id=(px, py, pc), device_id_type=MESH)
            sends.append(pltpu.make_async_remote_copy(
                src_ref=src if gather else src.at[peer], dst_ref=dst.at[me], **sems))
            recvs.append(pltpu.make_async_remote_copy(
                src_ref=src if gather else src.at[me], dst_ref=dst.at[peer], **sems))

    def start():
        for cp in owns + sends:
            cp.start()

    def wait():
        for cp in recvs:
            cp.wait_recv()
        for cp in sends:
            cp.wait_send()
        for cp in owns:
            cp.wait()

    return start, wait


def _riding_exchange(src_refs, dst_refs, sems, gather):
    start, wait = _direct_exchange(src_refs, dst_refs, *sems, gather)
    ids = [pl.program_id(a) for a in range(2)]
    last = [pl.num_programs(a) - 1 for a in range(2)]
    pl.when(jnp.logical_and(ids[0] == 0, ids[1] == 0))(start)
    return lambda: pl.when(jnp.logical_and(ids[0] == last[0], ids[1] == last[1]))(wait)


def _exchange(srcs, gather, name):
    n = len(srcs)

    def body(*refs):
        start, wait = _direct_exchange(refs[:n], refs[n:2 * n], *refs[2 * n:], gather=gather)
        start()
        wait()

    return pl.pallas_call(
        body, name=name, out_shape=_exchange_out(srcs, gather),
        in_specs=[ANY] * n, out_specs=[ANY] * n, scratch_shapes=_exchange_sems(n),
    )(*srcs)


def _adamw(w, g, m, v):
    m = ADAM_B1 * m + (1.0 - ADAM_B1) * g
    v = ADAM_B2 * v + (1.0 - ADAM_B2) * (g * g)
    m_hat = m / (1.0 - ADAM_B1 ** ADAM_STEP)
    v_hat = v / (1.0 - ADAM_B2 ** ADAM_STEP)
    return -ADAM_LR * (m_hat / (jnp.sqrt(v_hat) + ADAM_EPS) + ADAM_WD * w), m, v


def _reduce_adamw(recv, w, m, v, name):
    _, k, n = w.shape
    tr = max(t for t in range(16, 257, 16) if k % t == 0)

    def body(r_ref, w_ref, m_ref, v_ref, g_out, d_out, m_out, v_out):
        g = r_ref[0].astype(F32)
        for s in range(1, N_DEV):
            g = g + r_ref[s].astype(F32)
        g_out[...] = g
        d_out[...], m_out[...], v_out[...] = _adamw(w_ref[...], g, m_ref[...], v_ref[...])

    spec = pl.BlockSpec((None, tr, n), lambda i: (0, i, 0))
    return pl.pallas_call(
        body, name=name, grid=(k // tr,),
        out_shape=[jax.ShapeDtypeStruct((1, k, n), F32)] * 4,
        in_specs=[pl.BlockSpec((N_DEV, tr, n), lambda i: (0, i, 0)), spec, spec, spec],
        out_specs=[spec] * 4, compiler_params=_params("arbitrary"),
    )(recv, w, m, v)


def _small_adamw(gathered, w, m, v):
    def body(g_ref, w_ref, m_ref, v_ref, g_out, d_out, m_out, v_out, loss_out):
        tot = g_ref[0]
        for s in range(1, N_DEV):
            tot = tot + g_ref[s]
        g = tot[0:8]
        g_out[...] = g
        d_out[...], m_out[...], v_out[...] = _adamw(w_ref[...], g, m_ref[...], v_ref[...])
        loss_out[...] = jnp.broadcast_to((0.5 / D) * jnp.sum(tot[8:9], axis=1, keepdims=True), (8, BLK))

    out = [jax.ShapeDtypeStruct((8, D), F32)] * 4 + [jax.ShapeDtypeStruct((8, BLK), F32)]
    return pl.pallas_call(body, name="small_adamw", out_shape=out, compiler_params=_params())(gathered, w, m, v)


def _pick_chunks(w, chunks):
    return jnp.concatenate([w[:, BLK * c:BLK * (c + 1)] for c in chunks], axis=1)


def _whole_weight(gathered, i):
    _, k, n = gathered.shape
    if BY_ROWS[i]:
        return gathered.reshape(N_DEV * k, n)
    return gathered.transpose(1, 0, 2).reshape(k, N_DEV * n)


def _shard_parts(grad, i):
    if BY_ROWS[i]:
        return grad.reshape(N_DEV, grad.shape[0] // N_DEV, grad.shape[1])
    k, n8 = grad.shape
    return grad.reshape(k, N_DEV, n8 // N_DEV).transpose(1, 0, 2)


def kernel(x, mem, g_pre_mix, g_post_mix, g_pre_ffn, g_post_ffn, g_mem, w_in, w_mem_kv, w_br_sb, w_br_dil, w_br_mem, w_gate, b_gate, w_o, w_ffn_in, w_ffn_out, loss_target, m_g_pre_mix, m_g_post_mix, m_g_pre_ffn, m_g_post_ffn, m_g_mem, m_w_in, m_w_mem_kv, m_w_br_sb, m_w_br_dil, m_w_br_mem, m_w_gate, m_b_gate, m_w_o, m_w_ffn_in, m_w_ffn_out, v_g_pre_mix, v_g_post_mix, v_g_pre_ffn, v_g_post_ffn, v_g_mem, v_w_in, v_w_mem_kv, v_w_br_sb, v_w_br_dil, v_w_br_mem, v_w_gate, v_b_gate, v_w_o, v_w_ffn_in, v_w_ffn_out):
    bsz, seq, _ = x.shape
    tokens = bsz * seq
    xf, tgt, memf = x.reshape(tokens, D), loss_target.reshape(tokens, D), mem.reshape(bsz * MEM_LEN, D)
    big_w = [w_in, w_mem_kv, w_br_sb, w_br_dil, w_br_mem, w_gate, w_o, w_ffn_in, w_ffn_out]
    big_m = [m_w_in, m_w_mem_kv, m_w_br_sb, m_w_br_dil, m_w_br_mem, m_w_gate, m_w_o, m_w_ffn_in, m_w_ffn_out]
    big_v = [v_w_in, v_w_mem_kv, v_w_br_sb, v_w_br_dil, v_w_br_mem, v_w_gate, v_w_o, v_w_ffn_in, v_w_ffn_out]

    shards = [w[0].astype(BF16) for w in big_w]
    k_in, n_in = shards[0].shape
    fw_in = _whole_weight(_all_gather(shards[0], "weight_all_gather").reshape(N_DEV, k_in, n_in), 0)
    w_a, w_b = _pick_chunks(fw_in, CHUNKS_A), _pick_chunks(fw_in, CHUNKS_B)

    h = _norm_fwd(xf, g_pre_mix, "pre_mix_norm")
    proj_a = _matmul(h, w_a, "nn", BF16, "proj_a").reshape(bsz, seq, WA)
    proj_b = _matmul(h, w_b, "nn", BF16, "proj_b").reshape(bsz, seq, WB)
    o_a, o_a32, *behind = _sb_fwd(proj_a, seq, [shards[i] for i in GATHER_BEHIND])
    fw_mem_kv, fw_br_sb, fw_br_dil, fw_br_mem, fw_gate, fw_o, fw_ffn_in, fw_ffn_out = (
        _whole_weight(g, i) for g, i in zip(behind, GATHER_BEHIND))
    gpre = _matmul(h, fw_gate, "nn", BF16, "gate_proj")
    cos_t, sin_t = _rope_tables(seq)
    o_b, lse_b = _dil_fwd(proj_b, cos_t, sin_t, seq)
    mn = _norm_fwd(memf, g_mem, "mem_norm")
    kv = _matmul(mn, fw_mem_kv, "nn", BF16, "mem_kv_proj").reshape(bsz, MEM_LEN, D)
    o_c = _mem_fwd(proj_a, kv, seq)
    o_a2, o_b2, o_c2 = o_a.reshape(tokens, 512), o_b.reshape(tokens, 256), o_c.reshape(tokens, 512)
    ys = [_matmul(o_a2, fw_br_sb, "nn", BF16, "branch_sb"), _matmul(o_b2, fw_br_dil, "nn", BF16, "branch_dil"),
          _matmul(o_c2, fw_br_mem, "nn", BF16, "branch_mem")]
    merged = _gate_merge(gpre, ys, b_gate)
    mix, x1, h2 = _out_proj_norm(merged, fw_o, xf, g_post_mix, g_pre_ffn)
    gu = _matmul(h2, fw_ffn_in, "nn", BF16, "ffn_in")
    f = _swiglu_fwd(gu)
    dy, dfo, dg_post_ffn, loss_lanes = _ffn_out_loss(f, fw_ffn_out, x1, tgt, g_post_ffn)

    gw_ffn_out = _matmul(f, dfo, "tn", BF16, "gw_ffn_out")
    dgu = _d_ffn_swiglu_bwd(dfo, fw_ffn_out, gu)
    gw_ffn_in = _matmul(h2, dgu, "tn", BF16, "gw_ffn_in")
    dx1, dmix, dg_pre_ffn, dg_post_mix = _d_h2_norm_bwd(dgu, fw_ffn_in, x1, dy, mix, g_pre_ffn, g_post_mix)
    dmerged = _matmul(dmix, fw_o, "nt", BF16, "d_merged")
    gw_o = _matmul(merged, dmix, "tn", BF16, "gw_o")
    dya, dyb, dyc, dgpre, db_gate = _gate_bwd(dmerged, gpre, ys, b_gate)
    d_oa = _matmul(dya, fw_br_sb, "nt", BF16, "d_o_sb").reshape(bsz, seq, 512)
    d_ob = _matmul(dyb, fw_br_dil, "nt", BF16, "d_o_dil").reshape(bsz, seq, 256)
    d_oc = _matmul(dyc, fw_br_mem, "nt", BF16, "d_o_mem").reshape(bsz, seq, 512)
    gw_br_sb = _matmul(o_a2, dya, "tn", BF16, "gw_br_sb")
    gw_br_dil = _matmul(o_b2, dyb, "tn", BF16, "gw_br_dil")
    gw_br_mem = _matmul(o_c2, dyc, "tn", BF16, "gw_br_mem")
    gw_gate = _matmul(h, dgpre, "tn", BF16, "gw_gate")
    grads = {2: gw_br_sb, 3: gw_br_dil, 4: gw_br_mem, 5: gw_gate, 6: gw_o, 7: gw_ffn_in, 8: gw_ffn_out}
    d_proj_a, *recv_behind = _sb_bwd(proj_a, d_oa, o_a32, seq, [_shard_parts(grads[i], i) for i in REDUCE_BEHIND])
    d_proj_a, dk_m, dv_m = _mem_bwd(proj_a, kv, d_oc, d_proj_a, seq)
    d_proj_b = _dil_bwd(proj_b, cos_t, sin_t, d_ob, o_b, lse_b, seq).reshape(tokens, WB)
    d_proj_a = d_proj_a.reshape(tokens, WA)
    dh = _matmul(dgpre, fw_gate, "nt", F32, "d_h_gate")
    dh = _matmul(d_proj_a, w_a, "nt", F32, "d_h_a", add=dh)
    dx, dg_pre_mix = _d_h_norm_bwd(d_proj_b, w_b, dh, xf, dx1, g_pre_mix)
    gw_a = _matmul(h, d_proj_a, "tn", BF16, "gw_in_a")
    gw_b = _matmul(h, d_proj_b, "tn", BF16, "gw_in_b")
    dkv = jnp.concatenate([dk_m, dv_m], axis=-1).reshape(bsz * MEM_LEN, D)
    gw_mem_kv = _matmul(mn, dkv, "tn", BF16, "gw_mem_kv")
    dmn = _matmul(dkv, fw_mem_kv, "nt", F32, "d_mem_norm")
    dg_mem = _gain_grad(dmn, memf)

    gw_ab = jnp.concatenate([gw_a, gw_b], axis=1)
    where = {c: i for i, c in enumerate(CHUNKS_A + CHUNKS_B)}
    grads = {0: _pick_chunks(gw_ab, [where[c] for c in range(34)]), 1: gw_mem_kv}
    recv_last = _exchange([_shard_parts(grads[i], i) for i in REDUCE_LAST], False, "grad_exchange")
    received = dict(zip(REDUCE_BEHIND + REDUCE_LAST, [*recv_behind, *recv_last]))
    adam = [_reduce_adamw(received[i], big_w[i], big_m[i], big_v[i], "reduce_adamw_" + BIG_NAMES[i])
            for i in range(len(big_w))]
    big = [[a[k] for a in adam] for k in range(4)]

    small = jnp.concatenate([dg_pre_mix, dg_post_mix, dg_pre_ffn, dg_post_ffn, dg_mem, db_gate.reshape(3, D),
                             loss_lanes, jnp.zeros((7, D), F32)], axis=0)
    small_all, = _exchange([small], True, "small_all_gather")

    def small_pack(gs, b):
        return jnp.concatenate([*gs, b.reshape(3, D)], axis=0)

    sm = _small_adamw(
        small_all, small_pack([g_pre_mix, g_post_mix, g_pre_ffn, g_post_ffn, g_mem], b_gate),
        small_pack([m_g_pre_mix, m_g_post_mix, m_g_pre_ffn, m_g_post_ffn, m_g_mem], m_b_gate),
        small_pack([v_g_pre_mix, v_g_post_mix, v_g_pre_ffn, v_g_post_ffn, v_g_mem], v_b_gate))
    loss = sm[4][0, 0]

    def leaves(k):
        t, bw = sm[k], big[k]
        return [t[0:1], t[1:2], t[2:3], t[3:4], t[4:5], *bw[0:6], t[5:8].reshape(1, 3 * D), *bw[6:9]]

    return (loss, dx.reshape(bsz, seq, D), *leaves(0), *leaves(1), *leaves(2), *leaves(3))
```

```python
import functools

import jax
import jax.numpy as jnp
from jax import lax
from jax.experimental import pallas as pl
from jax.experimental.pallas import tpu as pltpu

F32 = jnp.float32
BF16 = jnp.bfloat16
D = 1024
BLK = 128
MEM_LEN = 256
D_FF = 2816
NORM_EPS = 1e-6
NEG_INF = -1e30
ROPE_THETA = 10000.0
ADAM_LR, ADAM_B1, ADAM_B2, ADAM_EPS, ADAM_WD, ADAM_STEP = 0.001, 0.9, 0.999, 1e-08, 0.01, 10
N_DEV = 8
VMEM_LIMIT_BYTES = 56 * 1024 * 1024
MESH = pl.DeviceIdType.MESH
ANY = pl.BlockSpec(memory_space=pl.ANY)

NT = (((1,), (1,)), ((), ()))
TN = (((0,), (0,)), ((), ()))
NN = (((1,), (0,)), ((), ()))
_DIMS = {"nn": NN, "nt": NT, "tn": TN}

BIG_NAMES = ("w_in", "w_mem_kv", "w_br_sb", "w_br_dil", "w_br_mem", "w_gate", "w_o", "w_ffn_in", "w_ffn_out")
BY_ROWS = (False, True, False, False, False, False, True, False, True)
GATHER_FIRST = (0,)
GATHER_BEHIND = (1, 2, 3, 4, 5, 6, 7, 8)
REDUCE_BEHIND = (2, 3, 4, 5, 6, 7, 8)
REDUCE_LAST = (0, 1)

CHUNKS_A = tuple(c for hp in range(4) for c in (hp, 4 + hp, 8 + hp)) + (30, 31, 32, 33)
CHUNKS_B = tuple(c for hp in range(2) for g in range(3) for c in (12 + 6 * g + hp, 14 + 6 * g + hp, 16 + 6 * g + hp))
WA, WB = 128 * len(CHUNKS_A), 128 * len(CHUNKS_B)
DIL_GROUPS = (1, 4, 16)


def _params(*sem):
    return pltpu.CompilerParams(dimension_semantics=sem or None, vmem_limit_bytes=VMEM_LIMIT_BYTES)


def _tile(n, cap):
    if n <= 128:
        return n
    assert n % 128 == 0, n
    best = 128
    for t in range(128, min(n, cap) + 1, 128):
        if n % t == 0:
            best = t
    return best


def _matmul(a, b, mode, out_dtype, name, tm_cap=1536, tn_cap=1536, tk_cap=1536):
    if mode == "tn":
        (K, M), N = a.shape, b.shape[1]
    elif mode == "nt":
        (M, K), N = a.shape, b.shape[0]
    else:
        (M, K), N = a.shape, b.shape[1]
    tm, tn, tk = _tile(M, tm_cap), _tile(N, tn_cap), _tile(K, tk_cap)
    nm, nn, nk = M // tm, N // tn, K // tk
    dims = _DIMS[mode]

    def body(a_ref, b_ref, o_ref, *acc):
        p = lax.dot_general(a_ref[...], b_ref[...], dims, preferred_element_type=F32)
        if nk == 1:
            o_ref[...] = p.astype(o_ref.dtype)
        else:
            acc_ref, = acc
            k = pl.program_id(2)

            @pl.when(k == 0)
            def _():
                acc_ref[...] = p

            @pl.when(k > 0)
            def _():
                acc_ref[...] += p

            @pl.when(k == nk - 1)
            def _():
                o_ref[...] = acc_ref[...].astype(o_ref.dtype)

    n_outer = nk == 1 and (a.size * nn + b.size) < (a.size + b.size * nm)
    if n_outer:
        grid, ij = (nn, nm, nk), (lambda g0, g1: (g1, g0))
    else:
        grid, ij = (nm, nn, nk), (lambda g0, g1: (g0, g1))
    if mode == "tn":
        a_spec = pl.BlockSpec((tk, tm), lambda g0, g1, k: (k, ij(g0, g1)[0]))
    else:
        a_spec = pl.BlockSpec((tm, tk), lambda g0, g1, k: (ij(g0, g1)[0], k))
    if mode == "nt":
        b_spec = pl.BlockSpec((tn, tk), lambda g0, g1, k: (ij(g0, g1)[1], k))
    else:
        b_spec = pl.BlockSpec((tk, tn), lambda g0, g1, k: (k, ij(g0, g1)[1]))
    return pl.pallas_call(
        body, name=name, grid=grid,
        out_shape=jax.ShapeDtypeStruct((M, N), out_dtype),
        in_specs=[a_spec, b_spec],
        out_specs=pl.BlockSpec((tm, tn), lambda g0, g1, k: ij(g0, g1)),
        scratch_shapes=[pltpu.VMEM((tm, tn), F32)] if nk > 1 else [],
        compiler_params=_params("parallel", "parallel", "arbitrary"),
    )(a, b)


def _rowwise(body, name, rows, tr, row_ins, vec_ins, row_outs, acc_outs=()):
    tr = min(tr, rows)
    assert rows % tr == 0
    in_specs, args = [], []
    for r in row_ins:
        arr, w, cb = r if isinstance(r, tuple) else (r, r.shape[1], 0)
        in_specs.append(pl.BlockSpec((tr, w), functools.partial(lambda i, cb: (i, cb), cb=cb)))
        args.append(arr)
    for v in vec_ins:
        in_specs.append(pl.BlockSpec(v.shape, lambda i: (0, 0)))
        args.append(v)
    out_shape = [jax.ShapeDtypeStruct((rows, w), dt) for w, dt in row_outs]
    out_shape += [jax.ShapeDtypeStruct((1, w), F32) for w in acc_outs]
    out_specs = [pl.BlockSpec((tr, w), lambda i: (i, 0)) for w, _ in row_outs]
    out_specs += [pl.BlockSpec((1, w), lambda i: (0, 0)) for w in acc_outs]
    n_acc = len(acc_outs)

    def wrapped(*refs):
        if n_acc:
            @pl.when(pl.program_id(0) == 0)
            def _():
                for r in refs[len(refs) - n_acc:]:
                    r[...] = jnp.zeros_like(r)
        body(*refs)

    return pl.pallas_call(
        wrapped, name=name, grid=(rows // tr,), out_shape=out_shape, in_specs=in_specs, out_specs=out_specs,
        compiler_params=_params("arbitrary"),
    )(*args)


def _rstd(x):
    return lax.rsqrt(jnp.mean(x * x, axis=-1, keepdims=True) + NORM_EPS)


def _norm_bwd(u, n, r):
    return r * (u - n * jnp.mean(u * n, axis=-1, keepdims=True))


def _colsum(v):
    return jnp.sum(v, axis=0, keepdims=True)


def _norm_fwd(x, g, name):
    def body(x_ref, g_ref, h_ref):
        xv = x_ref[...]
        h_ref[...] = ((xv * _rstd(xv)) * g_ref[...]).astype(BF16)

    return _rowwise(body, name, x.shape[0], 512, [x], [g], [(D, BF16)])[0]


def _matmul_rows(pairs, mode, name, epilogue, row_ins=(), vec_ins=(), row_outs=(), acc_outs=(), ride=None,
                 tm=512, tk_cap=1536):
    M = pairs[0][0].shape[0]
    N = pairs[0][1].shape[1] if mode == "nn" else pairs[0][1].shape[0]
    tm = min(tm, M)
    tks = [_tile(a.shape[1], tk_cap) for a, _ in pairs]
    nks = [a.shape[1] // tk for (a, _), tk in zip(pairs, tks)]
    offs = [sum(nks[:p]) for p in range(len(pairs))]
    nm, nk = M // tm, sum(nks)
    dims = _DIMS[mode]
    n_ab, n_extra, n_out = 2 * len(pairs), len(row_ins) + len(vec_ins), len(row_outs) + len(acc_outs)
    n_ride = 0 if ride is None else len(ride)

    def body(*refs):
        ab, extra, rest = refs[:n_ab], refs[n_ab:n_ab + n_extra], refs[n_ab + n_extra:]
        ride_refs, outs, rest = rest[:n_ride], rest[n_ride:n_ride + n_out], rest[n_ride + n_out:]
        received_refs, rest = rest[:n_ride], rest[n_ride:]
        if n_ride:
            finish_ride = _riding_exchange(ride_refs, received_refs, rest[len(rest) - 3:], gather=False)
        i, k = pl.program_id(0), pl.program_id(1)
        if acc_outs:
            @pl.when(jnp.logical_and(i == 0, k == 0))
            def _():
                for r in outs[len(row_outs):]:
                    r[...] = jnp.zeros_like(r)

        def product(p):
            return lax.dot_general(ab[2 * p][...], ab[2 * p + 1][...], dims, preferred_element_type=F32)

        if nk == 1:
            epilogue(product(0), *extra, *outs)
        else:
            acc_ref = rest[0]

            @pl.when(k == 0)
            def _():
                acc_ref[...] = product(0)

            for p in range(len(pairs)):
                @pl.when(jnp.logical_and(k >= max(offs[p], 1), k < offs[p] + nks[p]))
                def _(p=p):
                    acc_ref[...] += product(p)

            @pl.when(k == nk - 1)
            def _():
                epilogue(acc_ref[...], *extra, *outs)
        if n_ride:
            finish_ride()

    in_specs, args = [], []
    for p, ((a, b), tk) in enumerate(zip(pairs, tks)):
        step = functools.partial(lambda k, p: jnp.clip(k - offs[p], 0, nks[p] - 1), p=p)
        in_specs.append(pl.BlockSpec((tm, tk), functools.partial(lambda i, k, step: (i, step(k)), step=step)))
        if mode == "nn":
            in_specs.append(pl.BlockSpec((tk, N), functools.partial(lambda i, k, step: (step(k), 0), step=step)))
        else:
            in_specs.append(pl.BlockSpec((N, tk), functools.partial(lambda i, k, step: (0, step(k)), step=step)))
        args += [a, b]
    in_specs += [pl.BlockSpec((tm, r.shape[1]), lambda i, k: (i, 0)) for r in row_ins]
    in_specs += [pl.BlockSpec(v.shape, lambda i, k: (0, 0)) for v in vec_ins]
    in_specs += [ANY] * n_ride
    out_shape = [jax.ShapeDtypeStruct((M, w), dt) for w, dt in row_outs]
    out_shape += [jax.ShapeDtypeStruct((1, w), F32) for w in acc_outs]
    out_specs = [pl.BlockSpec((tm, w), lambda i, k: (i, 0)) for w, _ in row_outs]
    out_specs += [pl.BlockSpec((1, w), lambda i, k: (0, 0)) for w in acc_outs]
    scratch = [pltpu.VMEM((tm, N), F32)] if nk > 1 else []
    if n_ride:
        out_shape += _exchange_out(ride, False)
        out_specs += [ANY] * n_ride
        scratch += _exchange_sems(n_ride)
    return pl.pallas_call(
        body, name=name, grid=(nm, nk), out_shape=out_shape, in_specs=in_specs, out_specs=out_specs,
        scratch_shapes=scratch, compiler_params=_params("arbitrary", "arbitrary"),
    )(*args, *row_ins, *vec_ins, *(ride or []))


def _out_proj_norm(merged, w_o, x, g_post, g_pre):
    def epilogue(mv, x_ref, g2_ref, g3_ref, mix_ref, x1_ref, h2_ref):
        mix_ref[...] = mv
        x1 = x_ref[...] + (mv * _rstd(mv)) * g2_ref[...]
        x1_ref[...] = x1
        h2_ref[...] = ((x1 * _rstd(x1)) * g3_ref[...]).astype(BF16)

    return _matmul_rows([(merged, w_o)], "nn", "out_proj_norm", epilogue, [x], [g_post, g_pre],
                        [(D, F32), (D, F32), (D, BF16)])


def _gate_merge(gpre, ys, b_gate):
    def body(gp_ref, ya_ref, yb_ref, yc_ref, b_ref, m_ref):
        acc = None
        for k, y_ref in enumerate((ya_ref, yb_ref, yc_ref)):
            cols = slice(k * D, (k + 1) * D)
            gate = jax.nn.sigmoid(gp_ref[:, cols].astype(F32) + b_ref[:, cols])
            term = gate * y_ref[...].astype(F32)
            acc = term if acc is None else acc + term
        m_ref[...] = acc.astype(BF16)

    return _rowwise(body, "gate_merge", gpre.shape[0], 256, [gpre, *ys], [b_gate], [(D, BF16)])[0]


def _swiglu_fwd(gu):
    def body(a_ref, b_ref, f_ref):
        a = a_ref[...].astype(F32)
        f_ref[...] = (a * jax.nn.sigmoid(a) * b_ref[...].astype(F32)).astype(BF16)

    return _rowwise(body, "swiglu_fwd", gu.shape[0], 256, [(gu, D_FF, 0), (gu, D_FF, 1)], [], [(D_FF, BF16)])[0]


def _ffn_out_loss(f, w_ffn_out, x1, tgt, g_post):
    def epilogue(fo_v, x1_ref, t_ref, g_ref, dy_ref, dfo_ref, dg_ref, loss_ref):
        r = _rstd(fo_v)
        n = fo_v * r
        err = (x1_ref[...] + n * g_ref[...]) - t_ref[...]
        loss_ref[...] += _colsum(err * err)
        dy = err * (1.0 / D)
        dy_ref[...] = dy
        dg_ref[...] += _colsum(dy * n)
        dfo_ref[...] = _norm_bwd(dy * g_ref[...], n, r).astype(BF16)

    return _matmul_rows([(f, w_ffn_out)], "nn", "ffn_out_loss", epilogue, [x1, tgt], [g_post],
                        [(D, F32), (D, BF16)], (D, D))


def _d_ffn_swiglu_bwd(dfo, w_ffn_out, gu):
    def epilogue(d, gu_ref, dgu_ref):
        a = gu_ref[:, :D_FF].astype(F32)
        b = gu_ref[:, D_FF:].astype(F32)
        s = jax.nn.sigmoid(a)
        dgu_ref[:, :D_FF] = (d * b * (s * (1.0 + a * (1.0 - s)))).astype(BF16)
        dgu_ref[:, D_FF:] = (d * (a * s)).astype(BF16)

    return _matmul_rows([(dfo, w_ffn_out)], "nt", "d_ffn_swiglu_bwd", epilogue, [gu], [], [(2 * D_FF, BF16)],
                        tm=256)[0]


def _d_h2_norm_bwd(dgu, w_ffn_in, x1, dy, mix, g_pre, g_post):
    def epilogue(dh, x1_ref, dy_ref, mix_ref, g3_ref, g2_ref, dx1_ref, dmix_ref, dg3_ref, dg2_ref):
        x1v = x1_ref[...]
        r3 = _rstd(x1v)
        n3 = x1v * r3
        dg3_ref[...] += _colsum(dh * n3)
        dx1 = dy_ref[...] + _norm_bwd(dh * g3_ref[...], n3, r3)
        dx1_ref[...] = dx1
        mv = mix_ref[...]
        r2 = _rstd(mv)
        n2 = mv * r2
        dg2_ref[...] += _colsum(dx1 * n2)
        dmix_ref[...] = _norm_bwd(dx1 * g2_ref[...], n2, r2).astype(BF16)

    return _matmul_rows([(dgu, w_ffn_in)], "nt", "d_h2_norm_bwd", epilogue, [x1, dy, mix], [g_pre, g_post],
                        [(D, F32), (D, BF16)], (D, D))


def _gate_bwd(dmerged, gpre, ys, b_gate):
    def body(dm_ref, gp_ref, ya_ref, yb_ref, yc_ref, b_ref, dya_ref, dyb_ref, dyc_ref, dgp_ref, db_ref):
        dm = dm_ref[...].astype(F32)
        for k, (y_ref, dy_ref) in enumerate(((ya_ref, dya_ref), (yb_ref, dyb_ref), (yc_ref, dyc_ref))):
            cols = slice(k * D, (k + 1) * D)
            gate = jax.nn.sigmoid(gp_ref[:, cols].astype(F32) + b_ref[:, cols])
            dy_ref[...] = (dm * gate).astype(BF16)
            dgp = (dm * y_ref[...].astype(F32)) * (gate * (1.0 - gate))
            dgp_ref[:, cols] = dgp.astype(BF16)
            db_ref[:, cols] += _colsum(dgp)

    return _rowwise(body, "gate_bwd", gpre.shape[0], 256, [dmerged, gpre, *ys], [b_gate],
                    [(D, BF16), (D, BF16), (D, BF16), (3 * D, BF16)], (3 * D,))


def _d_h_norm_bwd(pairs, x, dx1, g_pre, ride):
    def epilogue(dh, x_ref, dx1_ref, g_ref, dx_ref, dg_ref):
        xv = x_ref[...]
        r = _rstd(xv)
        n = xv * r
        dg_ref[...] += _colsum(dh * n)
        dx_ref[...] = dx1_ref[...] + _norm_bwd(dh * g_ref[...], n, r)

    return _matmul_rows(pairs, "nt", "d_h_norm_bwd", epilogue, [x, dx1], [g_pre], [(D, F32)], (D,), ride=ride)


def _gain_grad(dmn, mem):
    def body(d_ref, m_ref, dg_ref):
        mv = m_ref[...]
        dg_ref[...] += _colsum(d_ref[...] * (mv * _rstd(mv)))

    return _rowwise(body, "mem_gain_grad", mem.shape[0], 256, [dmn, mem], [], [], (D,))[0]


def _head_rowsum(v, head0):
    return (jnp.sum(jnp.where(head0, v, 0.0), axis=1, keepdims=True),
            jnp.sum(jnp.where(head0, 0.0, v), axis=1, keepdims=True))


KT = 256
SB_SCALE = 0.125


def _make_suffix():
    tri = (lax.broadcasted_iota(jnp.int32, (KT, KT), 0) > lax.broadcasted_iota(jnp.int32, (KT, KT), 1)).astype(BF16)
    tri2 = jnp.concatenate([tri, tri], axis=0)

    def suffix(x):
        hi = x.astype(BF16)
        lo = (x - hi.astype(F32)).astype(BF16)
        return jnp.dot(jnp.concatenate([hi, lo], axis=1), tri2, preferred_element_type=F32)

    return suffix


def _sb_scores(qh, k, mask, suffix, run):
    z = lax.dot_general(qh, k, NT, preferred_element_type=F32)
    zc = jnp.minimum(z, 60.0)
    sp = jnp.log(1.0 + jnp.exp(zc))
    lb = zc - sp
    lk = -sp
    if mask is not None:
        lk = jnp.where(mask, lk, 0.0)
    a = jnp.exp(lb + suffix(lk) + run)
    if mask is not None:
        a = jnp.where(mask, a, 0.0)
    return lb, lk, a


QB = KT


def _sb_tiles(i, tile, init):
    st = tile(i, init, True)
    st = lax.fori_loop(0, lax.shift_right_logical(i, 1),
                       lambda t, s: tile(i - 2 - 2 * t, tile(i - 1 - 2 * t, s, False), False), st)
    return lax.cond((i & 1) == 1, lambda s: tile(0, s, False), lambda s: s, st)


def _sb_consts():
    head0 = lax.broadcasted_iota(jnp.int32, (QB, BLK), 1) < 64
    row = lax.broadcasted_iota(jnp.int32, (2 * QB, KT), 0) & (QB - 1)
    return head0, row > lax.broadcasted_iota(jnp.int32, (2 * QB, KT), 1)


def _stack_heads(v, head0):
    zero = jnp.zeros_like(v)
    return jnp.concatenate([jnp.where(head0, v, zero), jnp.where(head0, zero, v)], axis=0)


def _unstack_heads(v, head0):
    n = v.shape[0] // 2
    return jnp.where(head0, v[:n], v[n:])


def _sb_fwd(proj_a, seq, ride):
    bsz = proj_a.shape[0]
    n_ride = len(ride)

    def body(x_ref, *rest):
        ride_refs, (o_ref, o32_ref), rest = rest[:n_ride], rest[n_ride:n_ride + 2], rest[n_ride + 2:]
        gathered_refs, acc_ref, sems = rest[:n_ride], rest[n_ride], rest[n_ride + 1:]
        finish_ride = _riding_exchange(ride_refs, gathered_refs, sems, gather=True)
        head0, diag_mask = _sb_consts()
        suffix = _make_suffix()

        def qblock(i, carry):
            r0 = pl.multiple_of(i * QB, QB)
            qs = _stack_heads(x_ref[pl.ds(r0, QB), 0:128] * jnp.asarray(SB_SCALE, BF16), head0)

            def tile(jt, run, masked):
                c0 = pl.multiple_of(jt * KT, KT)
                k = x_ref[pl.ds(c0, KT), 128:256]
                v = x_ref[pl.ds(c0, KT), 256:384]
                _, lk, a = _sb_scores(qs, k, diag_mask if masked else None, suffix, run)
                pv = jnp.dot(a.astype(BF16), v, preferred_element_type=F32)
                if masked:
                    acc_ref[...] = pv
                else:
                    acc_ref[...] += pv
                return run + jnp.sum(lk, axis=1, keepdims=True)

            _sb_tiles(i, tile, jnp.zeros((2 * QB, 1), F32))
            o = _unstack_heads(acc_ref[...], head0)
            o32_ref[pl.ds(r0, QB), :] = o
            o_ref[pl.ds(r0, QB), :] = o.astype(BF16)
            return carry

        lax.fori_loop(0, seq // QB, qblock, 0)
        finish_ride()

    out_spec = pl.BlockSpec((None, seq, BLK), lambda b, hp: (b, 0, hp))
    return pl.pallas_call(
        body, name="sb_attn_fwd", grid=(bsz, 4),
        out_shape=[jax.ShapeDtypeStruct((bsz, seq, 512), BF16), jax.ShapeDtypeStruct((bsz, seq, 512), F32),
                   *_exchange_out(ride, True)],
        in_specs=[pl.BlockSpec((None, seq, 384), lambda b, hp: (b, 0, hp))] + [ANY] * n_ride,
        out_specs=[out_spec, out_spec] + [ANY] * n_ride,
        scratch_shapes=[pltpu.VMEM((2 * QB, BLK), F32), *_exchange_sems(n_ride)],
        compiler_params=_params("arbitrary", "arbitrary"),
    )(proj_a, *ride)


def _sb_bwd(proj_a, d_o, o_a, seq, ride):
    bsz = proj_a.shape[0]
    n_ride = len(ride)

    def body(x_ref, do_ref, o_ref, *rest):
        ride_refs, d_ref, rest = rest[:n_ride], rest[n_ride], rest[n_ride + 1:]
        received_refs, (dq_acc, dk_acc, dv_acc), sems = rest[:n_ride], rest[n_ride:n_ride + 3], rest[n_ride + 3:]
        finish_ride = _riding_exchange(ride_refs, received_refs, sems, gather=False)
        head0, diag_mask = _sb_consts()
        suffix = _make_suffix()
        dk_acc[...] = jnp.zeros_like(dk_acc)
        dv_acc[...] = jnp.zeros_like(dv_acc)

        def qblock(i, carry):
            r0 = pl.multiple_of(i * QB, QB)
            qs = _stack_heads(x_ref[pl.ds(r0, QB), 0:128] * jnp.asarray(SB_SCALE, BF16), head0)
            do = do_ref[pl.ds(r0, QB), :]
            dos = _stack_heads(do, head0)
            dsum = jnp.concatenate(_head_rowsum(do.astype(F32) * o_ref[pl.ds(r0, QB), :], head0), axis=0)

            def tile(jt, st, masked):
                run, grun = st
                c0 = pl.multiple_of(jt * KT, KT)
                k = x_ref[pl.ds(c0, KT), 128:256]
                v = x_ref[pl.ds(c0, KT), 256:384]
                lb, lk, a = _sb_scores(qs, k, diag_mask if masked else None, suffix, run)
                a16 = a.astype(BF16)
                g = a16.astype(F32) * lax.dot_general(dos, v, NT, preferred_element_type=F32)
                before = dsum - ((grun + suffix(g)) + g)
                dz = g - jnp.exp(lb) * (g + before)
                if masked:
                    dz = jnp.where(diag_mask, dz, 0.0)
                dz = dz.astype(BF16)
                dq = jnp.dot(dz, k, preferred_element_type=F32)
                if masked:
                    dq_acc[...] = dq
                else:
                    dq_acc[...] += dq
                dk_acc[pl.ds(c0, KT), :] += lax.dot_general(dz, qs, TN, preferred_element_type=F32)
                dv_acc[pl.ds(c0, KT), :] += lax.dot_general(a16, dos, TN, preferred_element_type=F32)
                return run + jnp.sum(lk, axis=1, keepdims=True), grun + jnp.sum(g, axis=1, keepdims=True)

            z1 = jnp.zeros((2 * QB, 1), F32)
            _sb_tiles(i, tile, (z1, z1))
            d_ref[pl.ds(r0, QB), 0:128] = (_unstack_heads(dq_acc[...], head0) * SB_SCALE).astype(BF16)
            return carry

        lax.fori_loop(0, seq // QB, qblock, 0)
        d_ref[:, 128:256] = dk_acc[...].astype(BF16)
        d_ref[:, 256:384] = dv_acc[...].astype(BF16)
        finish_ride()

    return pl.pallas_call(
        body, name="sb_attn_bwd", grid=(bsz, 4),
        out_shape=[jax.ShapeDtypeStruct((bsz, seq, WA), BF16), *_exchange_out(ride, False)],
        in_specs=[pl.BlockSpec((None, seq, 384), lambda b, hp: (b, 0, hp)),
                  pl.BlockSpec((None, seq, BLK), lambda b, hp: (b, 0, hp)),
                  pl.BlockSpec((None, seq, BLK), lambda b, hp: (b, 0, hp))] + [ANY] * n_ride,
        out_specs=[pl.BlockSpec((None, seq, 384), lambda b, hp: (b, 0, hp))] + [ANY] * n_ride,
        scratch_shapes=[pltpu.VMEM((2 * QB, BLK), F32), pltpu.VMEM((seq, BLK), F32), pltpu.VMEM((seq, BLK), F32),
                        *_exchange_sems(n_ride)],
        compiler_params=_params("arbitrary", "arbitrary"),
    )(proj_a, d_o, o_a, *ride)


def _rope_tables(seq):
    inv_freq = ROPE_THETA ** (-jnp.arange(32, dtype=F32) * 2.0 / 64)
    ang = jnp.arange(seq).astype(F32)[:, None] * inv_freq[None, :]
    cos, sin = jnp.cos(ang), jnp.sin(ang)
    return jnp.tile(cos, (1, 4)), jnp.concatenate([-sin, sin, -sin, sin], axis=1)


def _make_rope(n_rows):
    lane = lax.broadcasted_iota(jnp.int32, (n_rows, BLK), 1)
    first = (lane & 63) < 32

    def rope(x, cos, sin):
        partner = jnp.where(first, pltpu.roll(x, 96, 1), pltpu.roll(x, 32, 1))
        return x * cos + partner * sin

    return rope


DIL_UNROLL = 4


def _dil_consts():
    head0 = lax.broadcasted_iota(jnp.int32, (BLK, BLK), 1) < 64
    row = lax.broadcasted_iota(jnp.int32, (2 * BLK, 2 * BLK), 0) & (BLK - 1)
    col = lax.broadcasted_iota(jnp.int32, (2 * BLK, 2 * BLK), 1)
    valid_prev = jnp.logical_and(col < BLK, col >= row)
    valid_cur = jnp.logical_and(col >= BLK, row >= col - BLK)
    return head0, valid_prev, valid_cur


def _dil_blocks(dil, seq, block):
    nq = seq // dil // BLK

    def rows(r, i):
        if dil == 1:
            return pl.ds(pl.multiple_of(i * BLK, BLK), BLK)
        return pl.ds(r + (dil * BLK) * i, BLK, stride=dil)

    def step(t, carry):
        for u in range(DIL_UNROLL):
            n = t * DIL_UNROLL + u
            r, i = lax.div(n, nq), lax.rem(n, nq)
            block(rows(r, i), rows(r, jnp.maximum(i - 1, 0)), i)
        return carry

    lax.fori_loop(0, seq // BLK // DIL_UNROLL, step, 0)


def _dil_scores(qf, kf, vf, cur, prev, i, consts):
    head0, valid_prev, valid_cur = consts
    qs = _stack_heads(qf[cur, :].astype(BF16), head0)
    kcat = jnp.concatenate([kf[prev, :], kf[cur, :]], axis=0).astype(BF16)
    vcat = jnp.concatenate([vf[prev, :], vf[cur, :]], axis=0).astype(BF16)
    valid = jnp.logical_or(valid_cur, jnp.logical_and(valid_prev, i > 0))
    s = lax.dot_general(qs, kcat, NT, preferred_element_type=F32) * 0.125
    return qs, kcat, vcat, s, valid


def _head_cols(v):
    return jnp.concatenate([v[:, 0:1], v[:, 64:65]], axis=0)


def _dil_load_qkv(x_ref, c, rope, cos, sin, qf, kf, vf):
    qf[...] = rope(x_ref[:, c:c + 128].astype(F32), cos, sin).astype(BF16).astype(F32)
    kf[...] = rope(x_ref[:, c + 128:c + 256].astype(F32), cos, sin).astype(BF16).astype(F32)
    vf[...] = x_ref[:, c + 256:c + 384].astype(F32)


def _dil_fwd(proj_b, cos_t, sin_t, seq):
    bsz = proj_b.shape[0]

    def body(x_ref, cos_ref, sin_ref, ob_ref, lse_ref, qf, kf, vf, og, lg):
        consts = _dil_consts()
        head0 = consts[0]
        rope = _make_rope(seq)
        cos, sin = cos_ref[...], sin_ref[...]
        for g, dil in enumerate(DIL_GROUPS):
            _dil_load_qkv(x_ref, 384 * g, rope, cos, sin, qf, kf, vf)

            def block(cur, prev, i, g=g):
                _, _, vcat, s, valid = _dil_scores(qf, kf, vf, cur, prev, i, consts)
                s = jnp.where(valid, s, NEG_INF)
                m = jnp.max(s, axis=1, keepdims=True)
                p = jnp.exp(s - m)
                den = jnp.sum(p, axis=1, keepdims=True)
                o = jnp.dot(p.astype(BF16), vcat, preferred_element_type=F32) / den
                og[g, cur, :] = _unstack_heads(o, head0)
                lg[g, cur, :] = _unstack_heads(jnp.broadcast_to(m + jnp.log(den), (2 * BLK, BLK)), head0)

            _dil_blocks(dil, seq, block)
        ls = [lg[0], lg[1], lg[2]]
        m = jnp.maximum(jnp.maximum(ls[0], ls[1]), ls[2])
        ws = [jnp.exp(l - m) for l in ls]
        den = (ws[0] + ws[1]) + ws[2]
        ob_ref[...] = (((ws[0] * og[0] + ws[1] * og[1]) + ws[2] * og[2]) / den).astype(BF16)
        lse_ref[...] = m + jnp.log(den)

    tab_spec = pl.BlockSpec((seq, BLK), lambda b, hp: (0, 0))
    out_spec = pl.BlockSpec((None, seq, BLK), lambda b, hp: (b, 0, hp))
    slab = pltpu.VMEM((seq, BLK), F32)
    return pl.pallas_call(
        body, name="dil_attn_fwd", grid=(bsz, 2),
        out_shape=(jax.ShapeDtypeStruct((bsz, seq, 256), BF16), jax.ShapeDtypeStruct((bsz, seq, 256), F32)),
        in_specs=[pl.BlockSpec((None, seq, WB // 2), lambda b, hp: (b, 0, hp)), tab_spec, tab_spec],
        out_specs=(out_spec, out_spec),
        scratch_shapes=[slab, slab, slab, pltpu.VMEM((3, seq, BLK), F32), pltpu.VMEM((3, seq, BLK), F32)],
        compiler_params=_params("parallel", "parallel"),
    )(proj_b, cos_t, sin_t)


def _dil_bwd(proj_b, cos_t, sin_t, d_ob, o_b, lse, seq):
    bsz = proj_b.shape[0]

    def body(x_ref, cos_ref, sin_ref, do_ref, ob_ref, lse_ref, d_ref, qf, kf, vf, dof, dsf, dq_s, dk_acc, dv_acc):
        consts = _dil_consts()
        head0 = consts[0]
        rope = _make_rope(seq)
        cos, sin = cos_ref[...], sin_ref[...]
        do_all = do_ref[...].astype(F32)
        dof[...] = do_all
        head0_all = lax.broadcasted_iota(jnp.int32, (seq, BLK), 1) < 64
        d0, d1 = _head_rowsum(do_all * ob_ref[...].astype(F32), head0_all)
        dsf[...] = jnp.where(head0_all, d0, d1)
        for g, dil in enumerate(DIL_GROUPS):
            _dil_load_qkv(x_ref, 384 * g, rope, cos, sin, qf, kf, vf)
            dk_acc[...] = jnp.zeros_like(dk_acc)
            dv_acc[...] = jnp.zeros_like(dv_acc)

            def block(cur, prev, i):
                qs, kcat, vcat, s, valid = _dil_scores(qf, kf, vf, cur, prev, i, consts)
                dos = _stack_heads(dof[cur, :].astype(BF16), head0)
                p = jnp.where(valid, jnp.exp(s - _head_cols(lse_ref[cur, :])), 0.0)
                dp = lax.dot_general(dos, vcat, NT, preferred_element_type=F32)
                ds = ((p * (dp - _head_cols(dsf[cur, :]))) * 0.125).astype(BF16)
                dq_s[cur, :] = _unstack_heads(jnp.dot(ds, kcat, preferred_element_type=F32), head0)
                dk = lax.dot_general(ds, qs, TN, preferred_element_type=F32)
                dv = lax.dot_general(p.astype(BF16), dos, TN, preferred_element_type=F32)
                dk_acc[prev, :] += dk[:BLK]
                dk_acc[cur, :] += dk[BLK:]
                dv_acc[prev, :] += dv[:BLK]
                dv_acc[cur, :] += dv[BLK:]

            _dil_blocks(dil, seq, block)
            c = 384 * g
            d_ref[:, c:c + 128] = rope(dq_s[...], cos, -sin).astype(BF16)
            d_ref[:, c + 128:c + 256] = rope(dk_acc[...], cos, -sin).astype(BF16)
            d_ref[:, c + 256:c + 384] = dv_acc[...].astype(BF16)

    x_spec = pl.BlockSpec((None, seq, WB // 2), lambda b, hp: (b, 0, hp))
    tab_spec = pl.BlockSpec((seq, BLK), lambda b, hp: (0, 0))
    tok_spec = pl.BlockSpec((None, seq, BLK), lambda b, hp: (b, 0, hp))
    return pl.pallas_call(
        body, name="dil_attn_bwd", grid=(bsz, 2),
        out_shape=jax.ShapeDtypeStruct((bsz, seq, WB), BF16),
        in_specs=[x_spec, tab_spec, tab_spec, tok_spec, tok_spec, tok_spec], out_specs=x_spec,
        scratch_shapes=[pltpu.VMEM((seq, BLK), F32)] * 8,
        compiler_params=_params("parallel", "parallel"),
    )(proj_b, cos_t, sin_t, d_ob, o_b, lse)


MEM_SCALE = 128 ** -0.5
MEM_QB = 1024


def _mem_fwd(proj_a, kv, seq):
    bsz = proj_a.shape[0]

    def body(q_ref, k_ref, v_ref, o_ref):
        k, v = k_ref[...], v_ref[...]

        def qblock(i, carry):
            r0 = pl.multiple_of(i * MEM_QB, MEM_QB)
            s = lax.dot_general(q_ref[pl.ds(r0, MEM_QB), :], k, NT, preferred_element_type=F32) * MEM_SCALE
            p = jnp.exp(s - jnp.max(s, axis=1, keepdims=True))
            p = p / jnp.sum(p, axis=1, keepdims=True)
            o_ref[pl.ds(r0, MEM_QB), :] = jnp.dot(p.astype(BF16), v, preferred_element_type=F32).astype(BF16)
            return carry

        lax.fori_loop(0, seq // MEM_QB, qblock, 0)

    return pl.pallas_call(
        body, name="mem_attn_fwd", grid=(bsz, 4),
        out_shape=jax.ShapeDtypeStruct((bsz, seq, 512), BF16),
        in_specs=[pl.BlockSpec((None, seq, BLK), lambda b, h: (b, 0, 12 + h)),
                  pl.BlockSpec((None, MEM_LEN, BLK), lambda b, h: (b, 0, h)),
                  pl.BlockSpec((None, MEM_LEN, BLK), lambda b, h: (b, 0, 4 + h))],
        out_specs=pl.BlockSpec((None, seq, BLK), lambda b, h: (b, 0, h)),
        compiler_params=_params("parallel", "parallel"),
    )(proj_a, kv, kv)


def _mem_bwd(proj_a, kv, d_o, d_proj_a, seq):
    bsz = proj_a.shape[0]

    def body(q_ref, k_ref, v_ref, do_ref, _, dq_ref, dk_ref, dv_ref):
        k, v = k_ref[...], v_ref[...]

        def qblock(i, carry):
            dk, dv = carry
            r0 = pl.multiple_of(i * MEM_QB, MEM_QB)
            q, do = q_ref[pl.ds(r0, MEM_QB), :], do_ref[pl.ds(r0, MEM_QB), :]
            s = lax.dot_general(q, k, NT, preferred_element_type=F32) * MEM_SCALE
            p = jnp.exp(s - jnp.max(s, axis=1, keepdims=True))
            p = p / jnp.sum(p, axis=1, keepdims=True)
            dp = lax.dot_general(do, v, NT, preferred_element_type=F32)
            ds = ((p * (dp - jnp.sum(p * dp, axis=1, keepdims=True))) * MEM_SCALE).astype(BF16)
            dq_ref[pl.ds(r0, MEM_QB), :] = jnp.dot(ds, k, preferred_element_type=F32).astype(BF16)
            dk = dk + lax.dot_general(ds, q, TN, preferred_element_type=F32)
            dv = dv + lax.dot_general(p.astype(BF16), do, TN, preferred_element_type=F32)
            return dk, dv

        zero = jnp.zeros((MEM_LEN, BLK), F32)
        dk, dv = lax.fori_loop(0, seq // MEM_QB, qblock, (zero, zero))
        dk_ref[...] = dk.astype(BF16)
        dv_ref[...] = dv.astype(BF16)

    kv_spec = pl.BlockSpec((None, MEM_LEN, BLK), lambda b, h: (b, 0, h))
    return pl.pallas_call(
        body, name="mem_attn_bwd", grid=(bsz, 4),
        out_shape=(jax.ShapeDtypeStruct((bsz, seq, WA), BF16), jax.ShapeDtypeStruct((bsz, MEM_LEN, 512), BF16),
                   jax.ShapeDtypeStruct((bsz, MEM_LEN, 512), BF16)),
        in_specs=[pl.BlockSpec((None, seq, BLK), lambda b, h: (b, 0, 12 + h)), kv_spec,
                  pl.BlockSpec((None, MEM_LEN, BLK), lambda b, h: (b, 0, 4 + h)),
                  pl.BlockSpec((None, seq, BLK), lambda b, h: (b, 0, h)), ANY],
        out_specs=(pl.BlockSpec((None, seq, BLK), lambda b, h: (b, 0, 12 + h)), kv_spec, kv_spec),
        input_output_aliases={4: 0},
        compiler_params=_params("parallel", "parallel"),
    )(proj_a, kv, kv, d_o, d_proj_a)


def _mesh_pos():
    return lax.axis_index("x"), lax.axis_index("y"), lax.axis_index("c")


def _all_gather(shard, name):
    m_per, n = shard.shape

    def body(x_ref, out_ref, send_sems, recv_sems, local_sem):
        x, y, c = _mesh_pos()
        me, sibling = (x, y, c), (x, y, 1 - c)
        chips = [(1 - x, y), (x, 1 - y), (1 - x, 1 - y)]

        def rows(px, py, pc):
            return out_ref.at[pl.ds((4 * px + 2 * py + pc) * m_per, m_per), :]

        def copy(k, block, to, src=None):
            return pltpu.make_async_remote_copy(
                src_ref=rows(*block) if src is None else src, dst_ref=rows(*block),
                send_sem=send_sems.at[k], recv_sem=recv_sems.at[k], device_id=to, device_id_type=MESH)

        mine = pltpu.make_async_copy(x_ref, rows(*me), local_sem)
        mine.start()
        first = [copy(0, me, sibling, src=x_ref)]
        first += [copy(1 + j, me, (*chip, c), src=x_ref) for j, chip in enumerate(chips)]
        for cp in first:
            cp.start()
        passed = [copy(4 + j, (*chip, c), sibling) for j, chip in enumerate(chips)]
        for j, chip in enumerate(chips):
            copy(1 + j, (*chip, c), me).wait_recv()
            passed[j].start()
        copy(0, sibling, me).wait_recv()
        for j, chip in enumerate(chips):
            copy(4 + j, (*chip, 1 - c), me).wait_recv()
        for cp in first + passed:
            cp.wait_send()
        mine.wait()

    return pl.pallas_call(
        body, name=name, out_shape=jax.ShapeDtypeStruct((N_DEV * m_per, n), shard.dtype),
        in_specs=[ANY], out_specs=ANY,
        scratch_shapes=[pltpu.SemaphoreType.DMA((7,)), pltpu.SemaphoreType.DMA((7,)), pltpu.SemaphoreType.DMA(())],
    )(shard)


def _exchange_sems(n_arrays):
    return [pltpu.SemaphoreType.DMA((7 * n_arrays,)), pltpu.SemaphoreType.DMA((7 * n_arrays,)),
            pltpu.SemaphoreType.DMA((n_arrays,))]


def _exchange_out(srcs, gather):
    return [jax.ShapeDtypeStruct((N_DEV, *s.shape[-2:]), s.dtype) for s in srcs]


def _direct_exchange(src_refs, dst_refs, send_sems, recv_sems, local_sems, gather):
    x, y, c = _mesh_pos()
    me = 4 * x + 2 * y + c
    owns, sends, recvs = [], [], []
    for a, (src, dst) in enumerate(zip(src_refs, dst_refs)):
        owns.append(pltpu.make_async_copy(src if gather else src.at[me], dst.at[me], local_sems.at[a]))
        for j in range(1, N_DEV):
            px = 1 - x if j & 4 else x
            py = 1 - y if j & 2 else y
            pc = 1 - c if j & 1 else c
            peer = 4 * px + 2 * py + pc
            sems = dict(send_sem=send_sems.at[7 * a + j - 1], recv_sem=recv_sems.at[7 * a + j - 1],
                        device_id=(px, py, pc), device_id_type=MESH)
            sends.append(pltpu.make_async_remote_copy(
                src_ref=src if gather else src.at[peer], dst_ref=dst.at[me], **sems))
            recvs.append(pltpu.make_async_remote_copy(
                src_ref=src if gather else src.at[me], dst_ref=dst.at[peer], **sems))

    def start():
        for cp in owns + sends:
            cp.start()

    def wait():
        for cp in recvs:
            cp.wait_recv()
        for cp in sends:
            cp.wait_send()
        for cp in owns:
            cp.wait()

    return start, wait


def _riding_exchange(src_refs, dst_refs, sems, gather):
    start, wait = _direct_exchange(src_refs, dst_refs, *sems, gather)
    ids = [pl.program_id(a) for a in range(2)]
    last = [pl.num_programs(a) - 1 for a in range(2)]
    pl.when(jnp.logical_and(ids[0] == 0, ids[1] == 0))(start)
    return lambda: pl.when(jnp.logical_and(ids[0] == last[0], ids[1] == last[1]))(wait)


def _exchange(srcs, gather, name):
    n = len(srcs)

    def body(*refs):
        start, wait = _direct_exchange(refs[:n], refs[n:2 * n], *refs[2 * n:], gather=gather)
        start()
        wait()

    return pl.pallas_call(
        body, name=name, out_shape=_exchange_out(srcs, gather),
        in_specs=[ANY] * n, out_specs=[ANY] * n, scratch_shapes=_exchange_sems(n),
    )(*srcs)


def _adamw(w, g, m, v):
    m = ADAM_B1 * m + (1.0 - ADAM_B1) * g
    v = ADAM_B2 * v + (1.0 - ADAM_B2) * (g * g)
    m_hat = m / (1.0 - ADAM_B1 ** ADAM_STEP)
    v_hat = v / (1.0 - ADAM_B2 ** ADAM_STEP)
    return -ADAM_LR * (m_hat / (jnp.sqrt(v_hat) + ADAM_EPS) + ADAM_WD * w), m, v


def _reduce_adamw(recv, w, m, v, name):
    _, k, n = w.shape
    tr = max(t for t in range(16, 257, 16) if k % t == 0)

    def body(r_ref, w_ref, m_ref, v_ref, g_out, d_out, m_out, v_out):
        g = r_ref[0].astype(F32)
        for s in range(1, N_DEV):
            g = g + r_ref[s].astype(F32)
        g_out[...] = g
        d_out[...], m_out[...], v_out[...] = _adamw(w_ref[...], g, m_ref[...], v_ref[...])

    spec = pl.BlockSpec((None, tr, n), lambda i: (0, i, 0))
    return pl.pallas_call(
        body, name=name, grid=(k // tr,),
        out_shape=[jax.ShapeDtypeStruct((1, k, n), F32)] * 4,
        in_specs=[pl.BlockSpec((N_DEV, tr, n), lambda i: (0, i, 0)), spec, spec, spec],
        out_specs=[spec] * 4, compiler_params=_params("arbitrary"),
    )(recv, w, m, v)


def _small_adamw(gathered, w, m, v):
    def body(g_ref, w_ref, m_ref, v_ref, g_out, d_out, m_out, v_out, loss_out):
        tot = g_ref[0]
        for s in range(1, N_DEV):
            tot = tot + g_ref[s]
        g = tot[0:8]
        g_out[...] = g
        d_out[...], m_out[...], v_out[...] = _adamw(w_ref[...], g, m_ref[...], v_ref[...])
        loss_out[...] = jnp.broadcast_to((0.5 / D) * jnp.sum(tot[8:9], axis=1, keepdims=True), (8, BLK))

    out = [jax.ShapeDtypeStruct((8, D), F32)] * 4 + [jax.ShapeDtypeStruct((8, BLK), F32)]
    return pl.pallas_call(body, name="small_adamw", out_shape=out, compiler_params=_params())(gathered, w, m, v)


def _pick_chunks(w, chunks):
    return jnp.concatenate([w[:, BLK * c:BLK * (c + 1)] for c in chunks], axis=1)


def _whole_weight(gathered, i):
    _, k, n = gathered.shape
    if BY_ROWS[i]:
        return gathered.reshape(N_DEV * k, n)
    return gathered.transpose(1, 0, 2).reshape(k, N_DEV * n)


def _shard_parts(grad, i):
    if BY_ROWS[i]:
        return grad.reshape(N_DEV, grad.shape[0] // N_DEV, grad.shape[1])
    k, n8 = grad.shape
    return grad.reshape(k, N_DEV, n8 // N_DEV).transpose(1, 0, 2)


def kernel(x, mem, g_pre_mix, g_post_mix, g_pre_ffn, g_post_ffn, g_mem, w_in, w_mem_kv, w_br_sb, w_br_dil, w_br_mem, w_gate, b_gate, w_o, w_ffn_in, w_ffn_out, loss_target, m_g_pre_mix, m_g_post_mix, m_g_pre_ffn, m_g_post_ffn, m_g_mem, m_w_in, m_w_mem_kv, m_w_br_sb, m_w_br_dil, m_w_br_mem, m_w_gate, m_b_gate, m_w_o, m_w_ffn_in, m_w_ffn_out, v_g_pre_mix, v_g_post_mix, v_g_pre_ffn, v_g_post_ffn, v_g_mem, v_w_in, v_w_mem_kv, v_w_br_sb, v_w_br_dil, v_w_br_mem, v_w_gate, v_b_gate, v_w_o, v_w_ffn_in, v_w_ffn_out):
    bsz, seq, _ = x.shape
    tokens = bsz * seq
    xf, tgt, memf = x.reshape(tokens, D), loss_target.reshape(tokens, D), mem.reshape(bsz * MEM_LEN, D)
    big_w = [w_in, w_mem_kv, w_br_sb, w_br_dil, w_br_mem, w_gate, w_o, w_ffn_in, w_ffn_out]
    big_m = [m_w_in, m_w_mem_kv, m_w_br_sb, m_w_br_dil, m_w_br_mem, m_w_gate, m_w_o, m_w_ffn_in, m_w_ffn_out]
    big_v = [v_w_in, v_w_mem_kv, v_w_br_sb, v_w_br_dil, v_w_br_mem, v_w_gate, v_w_o, v_w_ffn_in, v_w_ffn_out]

    shards = [w[0].astype(BF16) for w in big_w]
    k_in, n_in = shards[0].shape
    fw_in = _whole_weight(_all_gather(shards[0], "weight_all_gather").reshape(N_DEV, k_in, n_in), 0)
    w_a, w_b = _pick_chunks(fw_in, CHUNKS_A), _pick_chunks(fw_in, CHUNKS_B)

    h = _norm_fwd(xf, g_pre_mix, "pre_mix_norm")
    proj_a = _matmul(h, w_a, "nn", BF16, "proj_a").reshape(bsz, seq, WA)
    proj_b = _matmul(h, w_b, "nn", BF16, "proj_b").reshape(bsz, seq, WB)
    o_a, o_a32, *behind = _sb_fwd(proj_a, seq, [shards[i] for i in GATHER_BEHIND])
    fw_mem_kv, fw_br_sb, fw_br_dil, fw_br_mem, fw_gate, fw_o, fw_ffn_in, fw_ffn_out = (
        _whole_weight(g, i) for g, i in zip(behind, GATHER_BEHIND))
    gpre = _matmul(h, fw_gate, "nn", BF16, "gate_proj")
    cos_t, sin_t = _rope_tables(seq)
    o_b, lse_b = _dil_fwd(proj_b, cos_t, sin_t, seq)
    mn = _norm_fwd(memf, g_mem, "mem_norm")
    kv = _matmul(mn, fw_mem_kv, "nn", BF16, "mem_kv_proj").reshape(bsz, MEM_LEN, D)
    o_c = _mem_fwd(proj_a, kv, seq)
    o_a2, o_b2, o_c2 = o_a.reshape(tokens, 512), o_b.reshape(tokens, 256), o_c.reshape(tokens, 512)
    ys = [_matmul(o_a2, fw_br_sb, "nn", BF16, "branch_sb"), _matmul(o_b2, fw_br_dil, "nn", BF16, "branch_dil"),
          _matmul(o_c2, fw_br_mem, "nn", BF16, "branch_mem")]
    merged = _gate_merge(gpre, ys, b_gate)
    mix, x1, h2 = _out_proj_norm(merged, fw_o, xf, g_post_mix, g_pre_ffn)
    gu = _matmul(h2, fw_ffn_in, "nn", BF16, "ffn_in")
    f = _swiglu_fwd(gu)
    dy, dfo, dg_post_ffn, loss_lanes = _ffn_out_loss(f, fw_ffn_out, x1, tgt, g_post_ffn)

    gw_ffn_out = _matmul(f, dfo, "tn", BF16, "gw_ffn_out")
    dgu = _d_ffn_swiglu_bwd(dfo, fw_ffn_out, gu)
    gw_ffn_in = _matmul(h2, dgu, "tn", BF16, "gw_ffn_in")
    dx1, dmix, dg_pre_ffn, dg_post_mix = _d_h2_norm_bwd(dgu, fw_ffn_in, x1, dy, mix, g_pre_ffn, g_post_mix)
    dmerged = _matmul(dmix, fw_o, "nt", BF16, "d_merged")
    gw_o = _matmul(merged, dmix, "tn", BF16, "gw_o")
    dya, dyb, dyc, dgpre, db_gate = _gate_bwd(dmerged, gpre, ys, b_gate)
    d_oa = _matmul(dya, fw_br_sb, "nt", BF16, "d_o_sb").reshape(bsz, seq, 512)
    d_ob = _matmul(dyb, fw_br_dil, "nt", BF16, "d_o_dil").reshape(bsz, seq, 256)
    d_oc = _matmul(dyc, fw_br_mem, "nt", BF16, "d_o_mem").reshape(bsz, seq, 512)
    gw_br_sb = _matmul(o_a2, dya, "tn", BF16, "gw_br_sb")
    gw_br_dil = _matmul(o_b2, dyb, "tn", BF16, "gw_br_dil")
    gw_br_mem = _matmul(o_c2, dyc, "tn", BF16, "gw_br_mem")
    gw_gate = _matmul(h, dgpre, "tn", BF16, "gw_gate")
    grads = {2: gw_br_sb, 3: gw_br_dil, 4: gw_br_mem, 5: gw_gate, 6: gw_o, 7: gw_ffn_in, 8: gw_ffn_out}
    d_proj_a, *recv_behind = _sb_bwd(proj_a, d_oa, o_a32, seq, [_shard_parts(grads[i], i) for i in REDUCE_BEHIND])
    d_proj_a, dk_m, dv_m = _mem_bwd(proj_a, kv, d_oc, d_proj_a, seq)
    d_proj_b = _dil_bwd(proj_b, cos_t, sin_t, d_ob, o_b, lse_b, seq).reshape(tokens, WB)
    d_proj_a = d_proj_a.reshape(tokens, WA)
    gw_a = _matmul(h, d_proj_a, "tn", BF16, "gw_in_a")
    gw_b = _matmul(h, d_proj_b, "tn", BF16, "gw_in_b")
    dkv = jnp.concatenate([dk_m, dv_m], axis=-1).reshape(bsz * MEM_LEN, D)
    gw_mem_kv = _matmul(mn, dkv, "tn", BF16, "gw_mem_kv")
    dmn = _matmul(dkv, fw_mem_kv, "nt", F32, "d_mem_norm")
    dg_mem = _gain_grad(dmn, memf)
    gw_ab = jnp.concatenate([gw_a, gw_b], axis=1)
    where = {c: i for i, c in enumerate(CHUNKS_A + CHUNKS_B)}
    grads = {0: _pick_chunks(gw_ab, [where[c] for c in range(34)]), 1: gw_mem_kv}
    dx, dg_pre_mix, *recv_last = _d_h_norm_bwd(
        [(dgpre, fw_gate), (d_proj_a, w_a), (d_proj_b, w_b)], xf, dx1, g_pre_mix,
        [_shard_parts(grads[i], i) for i in REDUCE_LAST])

    received = dict(zip(REDUCE_BEHIND + REDUCE_LAST, [*recv_behind, *recv_last]))
    adam = [_reduce_adamw(received[i], big_w[i], big_m[i], big_v[i], "reduce_adamw_" + BIG_NAMES[i])
            for i in range(len(big_w))]
    big = [[a[k] for a in adam] for k in range(4)]

    small = jnp.concatenate([dg_pre_mix, dg_post_mix, dg_pre_ffn, dg_post_ffn, dg_mem, db_gate.reshape(3, D),
                             loss_lanes, jnp.zeros((7, D), F32)], axis=0)
    small_all, = _exchange([small], True, "small_all_gather")

    def small_pack(gs, b):
        return jnp.concatenate([*gs, b.reshape(3, D)], axis=0)

    sm = _small_adamw(
        small_all, small_pack([g_pre_mix, g_post_mix, g_pre_ffn, g_post_ffn, g_mem], b_gate),
        small_pack([m_g_pre_mix, m_g_post_mix, m_g_pre_ffn, m_g_post_ffn, m_g_mem], m_b_gate),
        small_pack([v_g_pre_mix, v_g_post_mix, v_g_pre_ffn, v_g_post_ffn, v_g_mem], v_b_gate))
    loss = sm[4][0, 0]

    def leaves(k):
        t, bw = sm[k], big[k]
        return [t[0:1], t[1:2], t[2:3], t[3:4], t[4:5], *bw[0:6], t[5:8].reshape(1, 3 * D), *bw[6:9]]

    return (loss, dx.reshape(bsz, seq, D), *leaves(0), *leaves(1), *leaves(2), *leaves(3))
```

```python
import functools

import jax
import jax.numpy as jnp
from jax import lax
from jax.experimental import pallas as pl
from jax.experimental.pallas import tpu as pltpu

F32 = jnp.float32
BF16 = jnp.bfloat16
D = 1024
BLK = 128
MEM_LEN = 256
D_FF = 2816
NORM_EPS = 1e-6
NEG_INF = -1e30
ROPE_THETA = 10000.0
ADAM_LR, ADAM_B1, ADAM_B2, ADAM_EPS, ADAM_WD, ADAM_STEP = 0.001, 0.9, 0.999, 1e-08, 0.01, 10
N_DEV = 8
VMEM_LIMIT_BYTES = 56 * 1024 * 1024
MESH = pl.DeviceIdType.MESH
ANY = pl.BlockSpec(memory_space=pl.ANY)

NT = (((1,), (1,)), ((), ()))
TN = (((0,), (0,)), ((), ()))
NN = (((1,), (0,)), ((), ()))
_DIMS = {"nn": NN, "nt": NT, "tn": TN}

BIG_NAMES = ("w_in", "w_mem_kv", "w_br_sb", "w_br_dil", "w_br_mem", "w_gate", "w_o", "w_ffn_in", "w_ffn_out")
BY_ROWS = (False, True, False, False, False, False, True, False, True)
GATHER_FIRST = (0,)
GATHER_BEHIND = (1, 2, 3, 4, 5, 6, 7, 8)
REDUCE_BEHIND = (2, 3, 4, 5, 6, 7, 8)
REDUCE_LAST = (0, 1)

CHUNKS_A = tuple(c for hp in range(4) for c in (hp, 4 + hp, 8 + hp)) + (30, 31, 32, 33)
CHUNKS_B = tuple(c for hp in range(2) for g in range(3) for c in (12 + 6 * g + hp, 14 + 6 * g + hp, 16 + 6 * g + hp))
WA, WB = 128 * len(CHUNKS_A), 128 * len(CHUNKS_B)
DIL_GROUPS = (1, 4, 16)


def _params(*sem):
    return pltpu.CompilerParams(dimension_semantics=sem or None, vmem_limit_bytes=VMEM_LIMIT_BYTES)


def _tile(n, cap):
    if n <= 128:
        return n
    assert n % 128 == 0, n
    best = 128
    for t in range(128, min(n, cap) + 1, 128):
        if n % t == 0:
            best = t
    return best


def _k_steps(k, nk, step):
    if nk == 1:
        step(True, True)
        return
    pl.when(k == 0)(functools.partial(step, True, False))
    if nk > 2:
        pl.when(jnp.logical_and(k > 0, k < nk - 1))(functools.partial(step, False, False))
    pl.when(k == nk - 1)(functools.partial(step, False, True))


def _matmul(a, b, mode, out_dtype, name, tm_cap=1536, tn_cap=1536, tk_cap=1536):
    if mode == "tn":
        (K, M), N = a.shape, b.shape[1]
    elif mode == "nt":
        (M, K), N = a.shape, b.shape[0]
    else:
        (M, K), N = a.shape, b.shape[1]
    tm, tn, tk = _tile(M, tm_cap), _tile(N, tn_cap), _tile(K, tk_cap)
    nm, nn, nk = M // tm, N // tn, K // tk
    dims = _DIMS[mode]

    def body(a_ref, b_ref, o_ref, *acc):
        def step(first, last):
            d = lax.dot_general(a_ref[...], b_ref[...], dims, preferred_element_type=F32)
            if not first:
                d = d + acc[0][...]
            if last:
                o_ref[...] = d.astype(o_ref.dtype)
            else:
                acc[0][...] = d

        _k_steps(pl.program_id(2), nk, step)

    n_outer = nk == 1 and (a.size * nn + b.size) < (a.size + b.size * nm)
    if n_outer:
        grid, ij = (nn, nm, nk), (lambda g0, g1: (g1, g0))
    else:
        grid, ij = (nm, nn, nk), (lambda g0, g1: (g0, g1))
    if mode == "tn":
        a_spec = pl.BlockSpec((tk, tm), lambda g0, g1, k: (k, ij(g0, g1)[0]))
    else:
        a_spec = pl.BlockSpec((tm, tk), lambda g0, g1, k: (ij(g0, g1)[0], k))
    if mode == "nt":
        b_spec = pl.BlockSpec((tn, tk), lambda g0, g1, k: (ij(g0, g1)[1], k))
    else:
        b_spec = pl.BlockSpec((tk, tn), lambda g0, g1, k: (k, ij(g0, g1)[1]))
    return pl.pallas_call(
        body, name=name, grid=grid,
        out_shape=jax.ShapeDtypeStruct((M, N), out_dtype),
        in_specs=[a_spec, b_spec],
        out_specs=pl.BlockSpec((tm, tn), lambda g0, g1, k: ij(g0, g1)),
        scratch_shapes=[pltpu.VMEM((tm, tn), F32)] if nk > 1 else [],
        compiler_params=_params("parallel", "parallel", "arbitrary"),
    )(a, b)


def _rowwise(body, name, rows, tr, row_ins, vec_ins, row_outs, acc_outs=()):
    tr = min(tr, rows)
    assert rows % tr == 0
    in_specs, args = [], []
    for r in row_ins:
        arr, w, cb = r if isinstance(r, tuple) else (r, r.shape[1], 0)
        in_specs.append(pl.BlockSpec((tr, w), functools.partial(lambda i, cb: (i, cb), cb=cb)))
        args.append(arr)
    for v in vec_ins:
        in_specs.append(pl.BlockSpec(v.shape, lambda i: (0, 0)))
        args.append(v)
    out_shape = [jax.ShapeDtypeStruct((rows, w), dt) for w, dt in row_outs]
    out_shape += [jax.ShapeDtypeStruct((1, w), F32) for w in acc_outs]
    out_specs = [pl.BlockSpec((tr, w), lambda i: (i, 0)) for w, _ in row_outs]
    out_specs += [pl.BlockSpec((1, w), lambda i: (0, 0)) for w in acc_outs]
    n_acc = len(acc_outs)

    def wrapped(*refs):
        if n_acc:
            @pl.when(pl.program_id(0) == 0)
            def _():
                for r in refs[len(refs) - n_acc:]:
                    r[...] = jnp.zeros_like(r)
        body(*refs)

    return pl.pallas_call(
        wrapped, name=name, grid=(rows // tr,), out_shape=out_shape, in_specs=in_specs, out_specs=out_specs,
        compiler_params=_params("arbitrary"),
    )(*args)


def _rstd(x):
    return lax.rsqrt(jnp.mean(x * x, axis=-1, keepdims=True) + NORM_EPS)


def _norm_bwd(u, n, r):
    return r * (u - n * jnp.mean(u * n, axis=-1, keepdims=True))


def _colsum(v):
    return jnp.sum(v, axis=0, keepdims=True)


def _norm_fwd(x, g, name):
    def body(x_ref, g_ref, h_ref):
        xv = x_ref[...]
        h_ref[...] = ((xv * _rstd(xv)) * g_ref[...]).astype(BF16)

    return _rowwise(body, name, x.shape[0], 512, [x], [g], [(D, BF16)])[0]


def _matmul_rows(pairs, mode, name, epilogue, row_ins=(), vec_ins=(), row_outs=(), acc_outs=(), ride=None,
                 tm=512, tk_cap=1536):
    M = pairs[0][0].shape[0]
    N = pairs[0][1].shape[1] if mode == "nn" else pairs[0][1].shape[0]
    tm = min(tm, M)
    tks = [_tile(a.shape[1], tk_cap) for a, _ in pairs]
    nks = [a.shape[1] // tk for (a, _), tk in zip(pairs, tks)]
    offs = [sum(nks[:p]) for p in range(len(pairs))]
    nm, nk = M // tm, sum(nks)
    dims = _DIMS[mode]
    n_ab, n_extra, n_out = 2 * len(pairs), len(row_ins) + len(vec_ins), len(row_outs) + len(acc_outs)
    n_ride = 0 if ride is None else len(ride)

    def body(*refs):
        ab, extra, rest = refs[:n_ab], refs[n_ab:n_ab + n_extra], refs[n_ab + n_extra:]
        ride_refs, outs, rest = rest[:n_ride], rest[n_ride:n_ride + n_out], rest[n_ride + n_out:]
        received_refs, rest = rest[:n_ride], rest[n_ride:]
        if n_ride:
            finish_ride = _riding_exchange(ride_refs, received_refs, rest[len(rest) - 3:], gather=False)
        i, k = pl.program_id(0), pl.program_id(1)
        if acc_outs:
            @pl.when(jnp.logical_and(i == 0, k == 0))
            def _():
                for r in outs[len(row_outs):]:
                    r[...] = jnp.zeros_like(r)

        def step(p, first, last):
            d = lax.dot_general(ab[2 * p][...], ab[2 * p + 1][...], dims, preferred_element_type=F32)
            if not first:
                d = d + rest[0][...]
            if last:
                epilogue(d, *extra, *outs)
            else:
                rest[0][...] = d

        last_p = len(pairs) - 1
        if nk == 1:
            step(0, True, True)
        else:
            pl.when(k == 0)(functools.partial(step, 0, True, False))
            for p in range(len(pairs)):
                lo, hi = max(offs[p], 1), min(offs[p] + nks[p], nk - 1)
                if hi > lo:
                    pl.when(jnp.logical_and(k >= lo, k < hi))(functools.partial(step, p, False, False))
            pl.when(k == nk - 1)(functools.partial(step, last_p, False, True))
        if n_ride:
            finish_ride()

    in_specs, args = [], []
    for p, ((a, b), tk) in enumerate(zip(pairs, tks)):
        step = functools.partial(lambda k, p: jnp.clip(k - offs[p], 0, nks[p] - 1), p=p)
        in_specs.append(pl.BlockSpec((tm, tk), functools.partial(lambda i, k, step: (i, step(k)), step=step)))
        if mode == "nn":
            in_specs.append(pl.BlockSpec((tk, N), functools.partial(lambda i, k, step: (step(k), 0), step=step)))
        else:
            in_specs.append(pl.BlockSpec((N, tk), functools.partial(lambda i, k, step: (0, step(k)), step=step)))
        args += [a, b]
    in_specs += [pl.BlockSpec((tm, r.shape[1]), lambda i, k: (i, 0)) for r in row_ins]
    in_specs += [pl.BlockSpec(v.shape, lambda i, k: (0, 0)) for v in vec_ins]
    in_specs += [ANY] * n_ride
    out_shape = [jax.ShapeDtypeStruct((M, w), dt) for w, dt in row_outs]
    out_shape += [jax.ShapeDtypeStruct((1, w), F32) for w in acc_outs]
    out_specs = [pl.BlockSpec((tm, w), lambda i, k: (i, 0)) for w, _ in row_outs]
    out_specs += [pl.BlockSpec((1, w), lambda i, k: (0, 0)) for w in acc_outs]
    scratch = [pltpu.VMEM((tm, N), F32)] if nk > 1 else []
    if n_ride:
        out_shape += _exchange_out(ride, False)
        out_specs += [ANY] * n_ride
        scratch += _exchange_sems(n_ride)
    return pl.pallas_call(
        body, name=name, grid=(nm, nk), out_shape=out_shape, in_specs=in_specs, out_specs=out_specs,
        scratch_shapes=scratch, compiler_params=_params("arbitrary", "arbitrary"),
    )(*args, *row_ins, *vec_ins, *(ride or []))


def _out_proj_norm(merged, w_o, x, g_post, g_pre):
    def epilogue(mv, x_ref, g2_ref, g3_ref, mix_ref, x1_ref, h2_ref):
        mix_ref[...] = mv
        x1 = x_ref[...] + (mv * _rstd(mv)) * g2_ref[...]
        x1_ref[...] = x1
        h2_ref[...] = ((x1 * _rstd(x1)) * g3_ref[...]).astype(BF16)

    return _matmul_rows([(merged, w_o)], "nn", "out_proj_norm", epilogue, [x], [g_post, g_pre],
                        [(D, F32), (D, F32), (D, BF16)])


def _gate_merge(gpre, ys, b_gate):
    def body(gp_ref, ya_ref, yb_ref, yc_ref, b_ref, m_ref):
        acc = None
        for k, y_ref in enumerate((ya_ref, yb_ref, yc_ref)):
            cols = slice(k * D, (k + 1) * D)
            gate = jax.nn.sigmoid(gp_ref[:, cols].astype(F32) + b_ref[:, cols])
            term = gate * y_ref[...].astype(F32)
            acc = term if acc is None else acc + term
        m_ref[...] = acc.astype(BF16)

    return _rowwise(body, "gate_merge", gpre.shape[0], 256, [gpre, *ys], [b_gate], [(D, BF16)])[0]


def _swiglu_fwd(gu):
    def body(a_ref, b_ref, f_ref):
        a = a_ref[...].astype(F32)
        f_ref[...] = (a * jax.nn.sigmoid(a) * b_ref[...].astype(F32)).astype(BF16)

    return _rowwise(body, "swiglu_fwd", gu.shape[0], 256, [(gu, D_FF, 0), (gu, D_FF, 1)], [], [(D_FF, BF16)])[0]


def _ffn_out_loss(f, w_ffn_out, x1, tgt, g_post):
    def epilogue(fo_v, x1_ref, t_ref, g_ref, dy_ref, dfo_ref, dg_ref, loss_ref):
        r = _rstd(fo_v)
        n = fo_v * r
        err = (x1_ref[...] + n * g_ref[...]) - t_ref[...]
        loss_ref[...] += _colsum(err * err)
        dy = err * (1.0 / D)
        dy_ref[...] = dy
        dg_ref[...] += _colsum(dy * n)
        dfo_ref[...] = _norm_bwd(dy * g_ref[...], n, r).astype(BF16)

    return _matmul_rows([(f, w_ffn_out)], "nn", "ffn_out_loss", epilogue, [x1, tgt], [g_post],
                        [(D, F32), (D, BF16)], (D, D))


def _d_ffn_swiglu_bwd(dfo, w_ffn_out, gu):
    def epilogue(d, gu_ref, dgu_ref):
        a = gu_ref[:, :D_FF].astype(F32)
        b = gu_ref[:, D_FF:].astype(F32)
        s = jax.nn.sigmoid(a)
        dgu_ref[:, :D_FF] = (d * b * (s * (1.0 + a * (1.0 - s)))).astype(BF16)
        dgu_ref[:, D_FF:] = (d * (a * s)).astype(BF16)

    return _matmul_rows([(dfo, w_ffn_out)], "nt", "d_ffn_swiglu_bwd", epilogue, [gu], [], [(2 * D_FF, BF16)],
                        tm=256)[0]


def _d_h2_norm_bwd(dgu, w_ffn_in, x1, dy, mix, g_pre, g_post):
    def epilogue(dh, x1_ref, dy_ref, mix_ref, g3_ref, g2_ref, dx1_ref, dmix_ref, dg3_ref, dg2_ref):
        x1v = x1_ref[...]
        r3 = _rstd(x1v)
        n3 = x1v * r3
        dg3_ref[...] += _colsum(dh * n3)
        dx1 = dy_ref[...] + _norm_bwd(dh * g3_ref[...], n3, r3)
        dx1_ref[...] = dx1
        mv = mix_ref[...]
        r2 = _rstd(mv)
        n2 = mv * r2
        dg2_ref[...] += _colsum(dx1 * n2)
        dmix_ref[...] = _norm_bwd(dx1 * g2_ref[...], n2, r2).astype(BF16)

    return _matmul_rows([(dgu, w_ffn_in)], "nt", "d_h2_norm_bwd", epilogue, [x1, dy, mix], [g_pre, g_post],
                        [(D, F32), (D, BF16)], (D, D))


def _gate_bwd(dmerged, gpre, ys, b_gate):
    def body(dm_ref, gp_ref, ya_ref, yb_ref, yc_ref, b_ref, dya_ref, dyb_ref, dyc_ref, dgp_ref, db_ref):
        dm = dm_ref[...].astype(F32)
        for k, (y_ref, dy_ref) in enumerate(((ya_ref, dya_ref), (yb_ref, dyb_ref), (yc_ref, dyc_ref))):
            cols = slice(k * D, (k + 1) * D)
            gate = jax.nn.sigmoid(gp_ref[:, cols].astype(F32) + b_ref[:, cols])
            dy_ref[...] = (dm * gate).astype(BF16)
            dgp = (dm * y_ref[...].astype(F32)) * (gate * (1.0 - gate))
            dgp_ref[:, cols] = dgp.astype(BF16)
            db_ref[:, cols] += _colsum(dgp)

    return _rowwise(body, "gate_bwd", gpre.shape[0], 256, [dmerged, gpre, *ys], [b_gate],
                    [(D, BF16), (D, BF16), (D, BF16), (3 * D, BF16)], (3 * D,))


def _d_h_norm_bwd(pairs, x, dx1, g_pre, ride):
    def epilogue(dh, x_ref, dx1_ref, g_ref, dx_ref, dg_ref):
        xv = x_ref[...]
        r = _rstd(xv)
        n = xv * r
        dg_ref[...] += _colsum(dh * n)
        dx_ref[...] = dx1_ref[...] + _norm_bwd(dh * g_ref[...], n, r)

    return _matmul_rows(pairs, "nt", "d_h_norm_bwd", epilogue, [x, dx1], [g_pre], [(D, F32)], (D,), ride=ride)


def _gain_grad(dmn, mem):
    def body(d_ref, m_ref, dg_ref):
        mv = m_ref[...]
        dg_ref[...] += _colsum(d_ref[...] * (mv * _rstd(mv)))

    return _rowwise(body, "mem_gain_grad", mem.shape[0], 256, [dmn, mem], [], [], (D,))[0]


def _head_rowsum(v, head0):
    return (jnp.sum(jnp.where(head0, v, 0.0), axis=1, keepdims=True),
            jnp.sum(jnp.where(head0, 0.0, v), axis=1, keepdims=True))


KT = 256
SB_SCALE = 0.125


def _make_suffix():
    tri = (lax.broadcasted_iota(jnp.int32, (KT, KT), 0) > lax.broadcasted_iota(jnp.int32, (KT, KT), 1)).astype(BF16)
    tri2 = jnp.concatenate([tri, tri], axis=0)

    def suffix(x):
        hi = x.astype(BF16)
        lo = (x - hi.astype(F32)).astype(BF16)
        return jnp.dot(jnp.concatenate([hi, lo], axis=1), tri2, preferred_element_type=F32)

    return suffix


def _sb_scores(qh, k, mask, suffix, run):
    z = lax.dot_general(qh, k, NT, preferred_element_type=F32)
    zc = jnp.minimum(z, 60.0)
    sp = jnp.log(1.0 + jnp.exp(zc))
    lb = zc - sp
    lk = -sp
    if mask is not None:
        lk = jnp.where(mask, lk, 0.0)
    a = jnp.exp(lb + suffix(lk) + run)
    if mask is not None:
        a = jnp.where(mask, a, 0.0)
    return lb, lk, a


QB = KT


def _sb_tiles(i, tile, init):
    st = tile(i, init, True)
    st = lax.fori_loop(0, lax.shift_right_logical(i, 1),
                       lambda t, s: tile(i - 2 - 2 * t, tile(i - 1 - 2 * t, s, False), False), st)
    return lax.cond((i & 1) == 1, lambda s: tile(0, s, False), lambda s: s, st)


def _sb_consts():
    head0 = lax.broadcasted_iota(jnp.int32, (QB, BLK), 1) < 64
    row = lax.broadcasted_iota(jnp.int32, (2 * QB, KT), 0) & (QB - 1)
    return head0, row > lax.broadcasted_iota(jnp.int32, (2 * QB, KT), 1)


def _stack_heads(v, head0):
    zero = jnp.zeros_like(v)
    return jnp.concatenate([jnp.where(head0, v, zero), jnp.where(head0, zero, v)], axis=0)


def _unstack_heads(v, head0):
    n = v.shape[0] // 2
    return jnp.where(head0, v[:n], v[n:])


def _sb_fwd(proj_a, seq, ride):
    bsz = proj_a.shape[0]
    n_ride = len(ride)

    def body(x_ref, *rest):
        ride_refs, (o_ref, o32_ref), rest = rest[:n_ride], rest[n_ride:n_ride + 2], rest[n_ride + 2:]
        gathered_refs, acc_ref, sems = rest[:n_ride], rest[n_ride], rest[n_ride + 1:]
        finish_ride = _riding_exchange(ride_refs, gathered_refs, sems, gather=True)
        head0, diag_mask = _sb_consts()
        suffix = _make_suffix()

        def qblock(i, carry):
            r0 = pl.multiple_of(i * QB, QB)
            qs = _stack_heads(x_ref[pl.ds(r0, QB), 0:128] * jnp.asarray(SB_SCALE, BF16), head0)

            def tile(jt, run, masked):
                c0 = pl.multiple_of(jt * KT, KT)
                k = x_ref[pl.ds(c0, KT), 128:256]
                v = x_ref[pl.ds(c0, KT), 256:384]
                _, lk, a = _sb_scores(qs, k, diag_mask if masked else None, suffix, run)
                pv = jnp.dot(a.astype(BF16), v, preferred_element_type=F32)
                if masked:
                    acc_ref[...] = pv
                else:
                    acc_ref[...] += pv
                return run + jnp.sum(lk, axis=1, keepdims=True)

            _sb_tiles(i, tile, jnp.zeros((2 * QB, 1), F32))
            o = _unstack_heads(acc_ref[...], head0)
            o32_ref[pl.ds(r0, QB), :] = o
            o_ref[pl.ds(r0, QB), :] = o.astype(BF16)
            return carry

        lax.fori_loop(0, seq // QB, qblock, 0)
        finish_ride()

    out_spec = pl.BlockSpec((None, seq, BLK), lambda b, hp: (b, 0, hp))
    return pl.pallas_call(
        body, name="sb_attn_fwd", grid=(bsz, 4),
        out_shape=[jax.ShapeDtypeStruct((bsz, seq, 512), BF16), jax.ShapeDtypeStruct((bsz, seq, 512), F32),
                   *_exchange_out(ride, True)],
        in_specs=[pl.BlockSpec((None, seq, 384), lambda b, hp: (b, 0, hp))] + [ANY] * n_ride,
        out_specs=[out_spec, out_spec] + [ANY] * n_ride,
        scratch_shapes=[pltpu.VMEM((2 * QB, BLK), F32), *_exchange_sems(n_ride)],
        compiler_params=_params("arbitrary", "arbitrary"),
    )(proj_a, *ride)


def _sb_bwd(proj_a, d_o, o_a, seq, ride):
    bsz = proj_a.shape[0]
    n_ride = len(ride)

    def body(x_ref, do_ref, o_ref, *rest):
        ride_refs, d_ref, rest = rest[:n_ride], rest[n_ride], rest[n_ride + 1:]
        received_refs, (dq_acc, dk_acc, dv_acc), sems = rest[:n_ride], rest[n_ride:n_ride + 3], rest[n_ride + 3:]
        finish_ride = _riding_exchange(ride_refs, received_refs, sems, gather=False)
        head0, diag_mask = _sb_consts()
        suffix = _make_suffix()
        dk_acc[...] = jnp.zeros_like(dk_acc)
        dv_acc[...] = jnp.zeros_like(dv_acc)

        def qblock(i, carry):
            r0 = pl.multiple_of(i * QB, QB)
            qs = _stack_heads(x_ref[pl.ds(r0, QB), 0:128] * jnp.asarray(SB_SCALE, BF16), head0)
            do = do_ref[pl.ds(r0, QB), :]
            dos = _stack_heads(do, head0)
            dsum = jnp.concatenate(_head_rowsum(do.astype(F32) * o_ref[pl.ds(r0, QB), :], head0), axis=0)

            def tile(jt, st, masked):
                run, grun = st
                c0 = pl.multiple_of(jt * KT, KT)
                k = x_ref[pl.ds(c0, KT), 128:256]
                v = x_ref[pl.ds(c0, KT), 256:384]
                lb, lk, a = _sb_scores(qs, k, diag_mask if masked else None, suffix, run)
                a16 = a.astype(BF16)
                g = a16.astype(F32) * lax.dot_general(dos, v, NT, preferred_element_type=F32)
                before = dsum - ((grun + suffix(g)) + g)
                dz = g - jnp.exp(lb) * (g + before)
                if masked:
                    dz = jnp.where(diag_mask, dz, 0.0)
                dz = dz.astype(BF16)
                dq = jnp.dot(dz, k, preferred_element_type=F32)
                if masked:
                    dq_acc[...] = dq
                else:
                    dq_acc[...] += dq
                dk_acc[pl.ds(c0, KT), :] += lax.dot_general(dz, qs, TN, preferred_element_type=F32)
                dv_acc[pl.ds(c0, KT), :] += lax.dot_general(a16, dos, TN, preferred_element_type=F32)
                return run + jnp.sum(lk, axis=1, keepdims=True), grun + jnp.sum(g, axis=1, keepdims=True)

            z1 = jnp.zeros((2 * QB, 1), F32)
            _sb_tiles(i, tile, (z1, z1))
            d_ref[pl.ds(r0, QB), 0:128] = (_unstack_heads(dq_acc[...], head0) * SB_SCALE).astype(BF16)
            return carry

        lax.fori_loop(0, seq // QB, qblock, 0)
        d_ref[:, 128:256] = dk_acc[...].astype(BF16)
        d_ref[:, 256:384] = dv_acc[...].astype(BF16)
        finish_ride()

    return pl.pallas_call(
        body, name="sb_attn_bwd", grid=(bsz, 4),
        out_shape=[jax.ShapeDtypeStruct((bsz, seq, WA), BF16), *_exchange_out(ride, False)],
        in_specs=[pl.BlockSpec((None, seq, 384), lambda b, hp: (b, 0, hp)),
                  pl.BlockSpec((None, seq, BLK), lambda b, hp: (b, 0, hp)),
                  pl.BlockSpec((None, seq, BLK), lambda b, hp: (b, 0, hp))] + [ANY] * n_ride,
        out_specs=[pl.BlockSpec((None, seq, 384), lambda b, hp: (b, 0, hp))] + [ANY] * n_ride,
        scratch_shapes=[pltpu.VMEM((2 * QB, BLK), F32), pltpu.VMEM((seq, BLK), F32), pltpu.VMEM((seq, BLK), F32),
                        *_exchange_sems(n_ride)],
        compiler_params=_params("arbitrary", "arbitrary"),
    )(proj_a, d_o, o_a, *ride)


def _rope_tables(seq):
    inv_freq = ROPE_THETA ** (-jnp.arange(32, dtype=F32) * 2.0 / 64)
    ang = jnp.arange(seq).astype(F32)[:, None] * inv_freq[None, :]
    cos, sin = jnp.cos(ang), jnp.sin(ang)
    return jnp.tile(cos, (1, 4)), jnp.concatenate([-sin, sin, -sin, sin], axis=1)


def _make_rope(n_rows):
    lane = lax.broadcasted_iota(jnp.int32, (n_rows, BLK), 1)
    first = (lane & 63) < 32

    def rope(x, cos, sin):
        partner = jnp.where(first, pltpu.roll(x, 96, 1), pltpu.roll(x, 32, 1))
        return x * cos + partner * sin

    return rope


DIL_UNROLL = 4


def _dil_consts():
    head0 = lax.broadcasted_iota(jnp.int32, (BLK, BLK), 1) < 64
    row = lax.broadcasted_iota(jnp.int32, (2 * BLK, 2 * BLK), 0) & (BLK - 1)
    col = lax.broadcasted_iota(jnp.int32, (2 * BLK, 2 * BLK), 1)
    valid_prev = jnp.logical_and(col < BLK, col >= row)
    valid_cur = jnp.logical_and(col >= BLK, row >= col - BLK)
    return head0, valid_prev, valid_cur


def _dil_blocks(dil, seq, block):
    nq = seq // dil // BLK

    def rows(r, i):
        if dil == 1:
            return pl.ds(pl.multiple_of(i * BLK, BLK), BLK)
        return pl.ds(r + (dil * BLK) * i, BLK, stride=dil)

    def step(t, carry):
        for u in range(DIL_UNROLL):
            n = t * DIL_UNROLL + u
            r, i = lax.div(n, nq), lax.rem(n, nq)
            block(rows(r, i), rows(r, jnp.maximum(i - 1, 0)), i)
        return carry

    lax.fori_loop(0, seq // BLK // DIL_UNROLL, step, 0)


def _dil_scores(qf, kf, vf, cur, prev, i, consts):
    head0, valid_prev, valid_cur = consts
    qs = _stack_heads(qf[cur, :].astype(BF16), head0)
    kcat = jnp.concatenate([kf[prev, :], kf[cur, :]], axis=0).astype(BF16)
    vcat = jnp.concatenate([vf[prev, :], vf[cur, :]], axis=0).astype(BF16)
    valid = jnp.logical_or(valid_cur, jnp.logical_and(valid_prev, i > 0))
    s = lax.dot_general(qs, kcat, NT, preferred_element_type=F32) * 0.125
    return qs, kcat, vcat, s, valid


def _head_cols(v):
    return jnp.concatenate([v[:, 0:1], v[:, 64:65]], axis=0)


def _dil_load_qkv(x_ref, c, rope, cos, sin, qf, kf, vf):
    qf[...] = rope(x_ref[:, c:c + 128].astype(F32), cos, sin).astype(BF16).astype(F32)
    kf[...] = rope(x_ref[:, c + 128:c + 256].astype(F32), cos, sin).astype(BF16).astype(F32)
    vf[...] = x_ref[:, c + 256:c + 384].astype(F32)


def _dil_fwd(proj_b, cos_t, sin_t, seq):
    bsz = proj_b.shape[0]

    def body(x_ref, cos_ref, sin_ref, ob_ref, lse_ref, qf, kf, vf, og, lg):
        consts = _dil_consts()
        head0 = consts[0]
        rope = _make_rope(seq)
        cos, sin = cos_ref[...], sin_ref[...]
        for g, dil in enumerate(DIL_GROUPS):
            _dil_load_qkv(x_ref, 384 * g, rope, cos, sin, qf, kf, vf)

            def block(cur, prev, i, g=g):
                _, _, vcat, s, valid = _dil_scores(qf, kf, vf, cur, prev, i, consts)
                s = jnp.where(valid, s, NEG_INF)
                m = jnp.max(s, axis=1, keepdims=True)
                p = jnp.exp(s - m)
                den = jnp.sum(p, axis=1, keepdims=True)
                o = jnp.dot(p.astype(BF16), vcat, preferred_element_type=F32) / den
                og[g, cur, :] = _unstack_heads(o, head0)
                lg[g, cur, :] = _unstack_heads(jnp.broadcast_to(m + jnp.log(den), (2 * BLK, BLK)), head0)

            _dil_blocks(dil, seq, block)
        ls = [lg[0], lg[1], lg[2]]
        m = jnp.maximum(jnp.maximum(ls[0], ls[1]), ls[2])
        ws = [jnp.exp(l - m) for l in ls]
        den = (ws[0] + ws[1]) + ws[2]
        ob_ref[...] = (((ws[0] * og[0] + ws[1] * og[1]) + ws[2] * og[2]) / den).astype(BF16)
        lse_ref[...] = m + jnp.log(den)

    tab_spec = pl.BlockSpec((seq, BLK), lambda b, hp: (0, 0))
    out_spec = pl.BlockSpec((None, seq, BLK), lambda b, hp: (b, 0, hp))
    slab = pltpu.VMEM((seq, BLK), F32)
    return pl.pallas_call(
        body, name="dil_attn_fwd", grid=(bsz, 2),
        out_shape=(jax.ShapeDtypeStruct((bsz, seq, 256), BF16), jax.ShapeDtypeStruct((bsz, seq, 256), F32)),
        in_specs=[pl.BlockSpec((None, seq, WB // 2), lambda b, hp: (b, 0, hp)), tab_spec, tab_spec],
        out_specs=(out_spec, out_spec),
        scratch_shapes=[slab, slab, slab, pltpu.VMEM((3, seq, BLK), F32), pltpu.VMEM((3, seq, BLK), F32)],
        compiler_params=_params("parallel", "parallel"),
    )(proj_b, cos_t, sin_t)


def _dil_bwd(proj_b, cos_t, sin_t, d_ob, o_b, lse, seq):
    bsz = proj_b.shape[0]

    def body(x_ref, cos_ref, sin_ref, do_ref, ob_ref, lse_ref, d_ref, qf, kf, vf, dof, dsf, dq_s, dk_acc, dv_acc):
        consts = _dil_consts()
        head0 = consts[0]
        rope = _make_rope(seq)
        cos, sin = cos_ref[...], sin_ref[...]
        do_all = do_ref[...].astype(F32)
        dof[...] = do_all
        head0_all = lax.broadcasted_iota(jnp.int32, (seq, BLK), 1) < 64
        d0, d1 = _head_rowsum(do_all * ob_ref[...].astype(F32), head0_all)
        dsf[...] = jnp.where(head0_all, d0, d1)
        for g, dil in enumerate(DIL_GROUPS):
            _dil_load_qkv(x_ref, 384 * g, rope, cos, sin, qf, kf, vf)
            dk_acc[...] = jnp.zeros_like(dk_acc)
            dv_acc[...] = jnp.zeros_like(dv_acc)

            def block(cur, prev, i):
                qs, kcat, vcat, s, valid = _dil_scores(qf, kf, vf, cur, prev, i, consts)
                dos = _stack_heads(dof[cur, :].astype(BF16), head0)
                p = jnp.where(valid, jnp.exp(s - _head_cols(lse_ref[cur, :])), 0.0)
                dp = lax.dot_general(dos, vcat, NT, preferred_element_type=F32)
                ds = ((p * (dp - _head_cols(dsf[cur, :]))) * 0.125).astype(BF16)
                dq_s[cur, :] = _unstack_heads(jnp.dot(ds, kcat, preferred_element_type=F32), head0)
                dk = lax.dot_general(ds, qs, TN, preferred_element_type=F32)
                dv = lax.dot_general(p.astype(BF16), dos, TN, preferred_element_type=F32)
                dk_acc[prev, :] += dk[:BLK]
                dk_acc[cur, :] += dk[BLK:]
                dv_acc[prev, :] += dv[:BLK]
                dv_acc[cur, :] += dv[BLK:]

            _dil_blocks(dil, seq, block)
            c = 384 * g
            d_ref[:, c:c + 128] = rope(dq_s[...], cos, -sin).astype(BF16)
            d_ref[:, c + 128:c + 256] = rope(dk_acc[...], cos, -sin).astype(BF16)
            d_ref[:, c + 256:c + 384] = dv_acc[...].astype(BF16)

    x_spec = pl.BlockSpec((None, seq, WB // 2), lambda b, hp: (b, 0, hp))
    tab_spec = pl.BlockSpec((seq, BLK), lambda b, hp: (0, 0))
    tok_spec = pl.BlockSpec((None, seq, BLK), lambda b, hp: (b, 0, hp))
    return pl.pallas_call(
        body, name="dil_attn_bwd", grid=(bsz, 2),
        out_shape=jax.ShapeDtypeStruct((bsz, seq, WB), BF16),
        in_specs=[x_spec, tab_spec, tab_spec, tok_spec, tok_spec, tok_spec], out_specs=x_spec,
        scratch_shapes=[pltpu.VMEM((seq, BLK), F32)] * 8,
        compiler_params=_params("parallel", "parallel"),
    )(proj_b, cos_t, sin_t, d_ob, o_b, lse)


MEM_SCALE = 128 ** -0.5
MEM_QB = 1024


def _mem_fwd(proj_a, kv, seq):
    bsz = proj_a.shape[0]

    def body(q_ref, k_ref, v_ref, o_ref):
        k, v = k_ref[...], v_ref[...]

        def qblock(i, carry):
            r0 = pl.multiple_of(i * MEM_QB, MEM_QB)
            s = lax.dot_general(q_ref[pl.ds(r0, MEM_QB), :], k, NT, preferred_element_type=F32) * MEM_SCALE
            p = jnp.exp(s - jnp.max(s, axis=1, keepdims=True))
            p = p / jnp.sum(p, axis=1, keepdims=True)
            o_ref[pl.ds(r0, MEM_QB), :] = jnp.dot(p.astype(BF16), v, preferred_element_type=F32).astype(BF16)
            return carry

        lax.fori_loop(0, seq // MEM_QB, qblock, 0)

    return pl.pallas_call(
        body, name="mem_attn_fwd", grid=(bsz, 4),
        out_shape=jax.ShapeDtypeStruct((bsz, seq, 512), BF16),
        in_specs=[pl.BlockSpec((None, seq, BLK), lambda b, h: (b, 0, 12 + h)),
                  pl.BlockSpec((None, MEM_LEN, BLK), lambda b, h: (b, 0, h)),
                  pl.BlockSpec((None, MEM_LEN, BLK), lambda b, h: (b, 0, 4 + h))],
        out_specs=pl.BlockSpec((None, seq, BLK), lambda b, h: (b, 0, h)),
        compiler_params=_params("parallel", "parallel"),
    )(proj_a, kv, kv)


def _mem_bwd(proj_a, kv, d_o, d_proj_a, seq):
    bsz = proj_a.shape[0]

    def body(q_ref, k_ref, v_ref, do_ref, _, dq_ref, dk_ref, dv_ref):
        k, v = k_ref[...], v_ref[...]

        def qblock(i, carry):
            dk, dv = carry
            r0 = pl.multiple_of(i * MEM_QB, MEM_QB)
            q, do = q_ref[pl.ds(r0, MEM_QB), :], do_ref[pl.ds(r0, MEM_QB), :]
            s = lax.dot_general(q, k, NT, preferred_element_type=F32) * MEM_SCALE
            p = jnp.exp(s - jnp.max(s, axis=1, keepdims=True))
            p = p / jnp.sum(p, axis=1, keepdims=True)
            dp = lax.dot_general(do, v, NT, preferred_element_type=F32)
            ds = ((p * (dp - jnp.sum(p * dp, axis=1, keepdims=True))) * MEM_SCALE).astype(BF16)
            dq_ref[pl.ds(r0, MEM_QB), :] = jnp.dot(ds, k, preferred_element_type=F32).astype(BF16)
            dk = dk + lax.dot_general(ds, q, TN, preferred_element_type=F32)
            dv = dv + lax.dot_general(p.astype(BF16), do, TN, preferred_element_type=F32)
            return dk, dv

        zero = jnp.zeros((MEM_LEN, BLK), F32)
        dk, dv = lax.fori_loop(0, seq // MEM_QB, qblock, (zero, zero))
        dk_ref[...] = dk.astype(BF16)
        dv_ref[...] = dv.astype(BF16)

    kv_spec = pl.BlockSpec((None, MEM_LEN, BLK), lambda b, h: (b, 0, h))
    return pl.pallas_call(
        body, name="mem_attn_bwd", grid=(bsz, 4),
        out_shape=(jax.ShapeDtypeStruct((bsz, seq, WA), BF16), jax.ShapeDtypeStruct((bsz, MEM_LEN, 512), BF16),
                   jax.ShapeDtypeStruct((bsz, MEM_LEN, 512), BF16)),
        in_specs=[pl.BlockSpec((None, seq, BLK), lambda b, h: (b, 0, 12 + h)), kv_spec,
                  pl.BlockSpec((None, MEM_LEN, BLK), lambda b, h: (b, 0, 4 + h)),
                  pl.BlockSpec((None, seq, BLK), lambda b, h: (b, 0, h)), ANY],
        out_specs=(pl.BlockSpec((None, seq, BLK), lambda b, h: (b, 0, 12 + h)), kv_spec, kv_spec),
        input_output_aliases={4: 0},
        compiler_params=_params("parallel", "parallel"),
    )(proj_a, kv, kv, d_o, d_proj_a)


def _mesh_pos():
    return lax.axis_index("x"), lax.axis_index("y"), lax.axis_index("c")


def _all_gather(shard, name):
    m_per, n = shard.shape

    def body(x_ref, out_ref, send_sems, recv_sems, local_sem):
        x, y, c = _mesh_pos()
        me, sibling = (x, y, c), (x, y, 1 - c)
        chips = [(1 - x, y), (x, 1 - y), (1 - x, 1 - y)]

        def rows(px, py, pc):
            return out_ref.at[pl.ds((4 * px + 2 * py + pc) * m_per, m_per), :]

        def copy(k, block, to, src=None):
            return pltpu.make_async_remote_copy(
                src_ref=rows(*block) if src is None else src, dst_ref=rows(*block),
                send_sem=send_sems.at[k], recv_sem=recv_sems.at[k], device_id=to, device_id_type=MESH)

        mine = pltpu.make_async_copy(x_ref, rows(*me), local_sem)
        mine.start()
        first = [copy(0, me, sibling, src=x_ref)]
        first += [copy(1 + j, me, (*chip, c), src=x_ref) for j, chip in enumerate(chips)]
        for cp in first:
            cp.start()
        passed = [copy(4 + j, (*chip, c), sibling) for j, chip in enumerate(chips)]
        for j, chip in enumerate(chips):
            copy(1 + j, (*chip, c), me).wait_recv()
            passed[j].start()
        copy(0, sibling, me).wait_recv()
        for j, chip in enumerate(chips):
            copy(4 + j, (*chip, 1 - c), me).wait_recv()
        for cp in first + passed:
            cp.wait_send()
        mine.wait()

    return pl.pallas_call(
        body, name=name, out_shape=jax.ShapeDtypeStruct((N_DEV * m_per, n), shard.dtype),
        in_specs=[ANY], out_specs=ANY,
        scratch_shapes=[pltpu.SemaphoreType.DMA((7,)), pltpu.SemaphoreType.DMA((7,)), pltpu.SemaphoreType.DMA(())],
    )(shard)


def _exchange_sems(n_arrays):
    return [pltpu.SemaphoreType.DMA((7 * n_arrays,)), pltpu.SemaphoreType.DMA((7 * n_arrays,)),
            pltpu.SemaphoreType.DMA((n_arrays,))]


def _exchange_out(srcs, gather):
    return [jax.ShapeDtypeStruct((N_DEV, *s.shape[-2:]), s.dtype) for s in srcs]


def _direct_exchange(src_refs, dst_refs, send_sems, recv_sems, local_sems, gather):
    x, y, c = _mesh_pos()
    me = 4 * x + 2 * y + c
    owns, sends, recvs = [], [], []
    for a, (src, dst) in enumerate(zip(src_refs, dst_refs)):
        owns.append(pltpu.make_async_copy(src if gather else src.at[me], dst.at[me], local_sems.at[a]))
        for j in range(1, N_DEV):
            px = 1 - x if j & 4 else x
            py = 1 - y if j & 2 else y
            pc = 1 - c if j & 1 else c
            peer = 4 * px + 2 * py + pc
            sems = dict(send_sem=send_sems.at[7 * a + j - 1], recv_sem=recv_sems.at[7 * a + j - 1],
                        device_id=(px, py, pc), device_id_type=MESH)
            sends.append(pltpu.make_async_remote_copy(
                src_ref=src if gather else src.at[peer], dst_ref=dst.at[me], **sems))
            recvs.append(pltpu.make_async_remote_copy(
                src_ref=src if gather else src.at[me], dst_ref=dst.at[peer], **sems))

    def start():
        for cp in owns + sends:
            cp.start()

    def wait():
        for cp in recvs:
            cp.wait_recv()
        for cp in sends:
            cp.wait_send()
        for cp in owns:
            cp.wait()

    return start, wait


def _riding_exchange(src_refs, dst_refs, sems, gather):
    start, wait = _direct_exchange(src_refs, dst_refs, *sems, gather)
    ids = [pl.program_id(a) for a in range(2)]
    last = [pl.num_programs(a) - 1 for a in range(2)]
    pl.when(jnp.logical_and(ids[0] == 0, ids[1] == 0))(start)
    return lambda: pl.when(jnp.logical_and(ids[0] == last[0], ids[1] == last[1]))(wait)


def _exchange(srcs, gather, name):
    n = len(srcs)

    def body(*refs):
        start, wait = _direct_exchange(refs[:n], refs[n:2 * n], *refs[2 * n:], gather=gather)
        start()
        wait()

    return pl.pallas_call(
        body, name=name, out_shape=_exchange_out(srcs, gather),
        in_specs=[ANY] * n, out_specs=[ANY] * n, scratch_shapes=_exchange_sems(n),
    )(*srcs)


def _adamw(w, g, m, v):
    m = ADAM_B1 * m + (1.0 - ADAM_B1) * g
    v = ADAM_B2 * v + (1.0 - ADAM_B2) * (g * g)
    m_hat = m / (1.0 - ADAM_B1 ** ADAM_STEP)
    v_hat = v / (1.0 - ADAM_B2 ** ADAM_STEP)
    return -ADAM_LR * (m_hat / (jnp.sqrt(v_hat) + ADAM_EPS) + ADAM_WD * w), m, v


def _reduce_adamw(recv, w, m, v, name):
    _, k, n = w.shape
    tr = max(t for t in range(16, 257, 16) if k % t == 0)

    def body(r_ref, w_ref, m_ref, v_ref, g_out, d_out, m_out, v_out):
        g = r_ref[0].astype(F32)
        for s in range(1, N_DEV):
            g = g + r_ref[s].astype(F32)
        g_out[...] = g
        d_out[...], m_out[...], v_out[...] = _adamw(w_ref[...], g, m_ref[...], v_ref[...])

    spec = pl.BlockSpec((None, tr, n), lambda i: (0, i, 0))
    return pl.pallas_call(
        body, name=name, grid=(k // tr,),
        out_shape=[jax.ShapeDtypeStruct((1, k, n), F32)] * 4,
        in_specs=[pl.BlockSpec((N_DEV, tr, n), lambda i: (0, i, 0)), spec, spec, spec],
        out_specs=[spec] * 4, compiler_params=_params("arbitrary"),
    )(recv, w, m, v)


def _small_adamw(gathered, w, m, v):
    def body(g_ref, w_ref, m_ref, v_ref, g_out, d_out, m_out, v_out, loss_out):
        tot = g_ref[0]
        for s in range(1, N_DEV):
            tot = tot + g_ref[s]
        g = tot[0:8]
        g_out[...] = g
        d_out[...], m_out[...], v_out[...] = _adamw(w_ref[...], g, m_ref[...], v_ref[...])
        loss_out[...] = jnp.broadcast_to((0.5 / D) * jnp.sum(tot[8:9], axis=1, keepdims=True), (8, BLK))

    out = [jax.ShapeDtypeStruct((8, D), F32)] * 4 + [jax.ShapeDtypeStruct((8, BLK), F32)]
    return pl.pallas_call(body, name="small_adamw", out_shape=out, compiler_params=_params())(gathered, w, m, v)


def _pick_chunks(w, chunks):
    return jnp.concatenate([w[:, BLK * c:BLK * (c + 1)] for c in chunks], axis=1)


def _whole_weight(gathered, i):
    _, k, n = gathered.shape
    if BY_ROWS[i]:
        return gathered.reshape(N_DEV * k, n)
    return gathered.transpose(1, 0, 2).reshape(k, N_DEV * n)


def _shard_parts(grad, i):
    if BY_ROWS[i]:
        return grad.reshape(N_DEV, grad.shape[0] // N_DEV, grad.shape[1])
    k, n8 = grad.shape
    return grad.reshape(k, N_DEV, n8 // N_DEV).transpose(1, 0, 2)


def kernel(x, mem, g_pre_mix, g_post_mix, g_pre_ffn, g_post_ffn, g_mem, w_in, w_mem_kv, w_br_sb, w_br_dil, w_br_mem, w_gate, b_gate, w_o, w_ffn_in, w_ffn_out, loss_target, m_g_pre_mix, m_g_post_mix, m_g_pre_ffn, m_g_post_ffn, m_g_mem, m_w_in, m_w_mem_kv, m_w_br_sb, m_w_br_dil, m_w_br_mem, m_w_gate, m_b_gate, m_w_o, m_w_ffn_in, m_w_ffn_out, v_g_pre_mix, v_g_post_mix, v_g_pre_ffn, v_g_post_ffn, v_g_mem, v_w_in, v_w_mem_kv, v_w_br_sb, v_w_br_dil, v_w_br_mem, v_w_gate, v_b_gate, v_w_o, v_w_ffn_in, v_w_ffn_out):
    bsz, seq, _ = x.shape
    tokens = bsz * seq
    xf, tgt, memf = x.reshape(tokens, D), loss_target.reshape(tokens, D), mem.reshape(bsz * MEM_LEN, D)
    big_w = [w_in, w_mem_kv, w_br_sb, w_br_dil, w_br_mem, w_gate, w_o, w_ffn_in, w_ffn_out]
    big_m = [m_w_in, m_w_mem_kv, m_w_br_sb, m_w_br_dil, m_w_br_mem, m_w_gate, m_w_o, m_w_ffn_in, m_w_ffn_out]
    big_v = [v_w_in, v_w_mem_kv, v_w_br_sb, v_w_br_dil, v_w_br_mem, v_w_gate, v_w_o, v_w_ffn_in, v_w_ffn_out]

    shards = [w[0].astype(BF16) for w in big_w]
    k_in, n_in = shards[0].shape
    fw_in = _whole_weight(_all_gather(shards[0], "weight_all_gather").reshape(N_DEV, k_in, n_in), 0)
    w_a, w_b = _pick_chunks(fw_in, CHUNKS_A), _pick_chunks(fw_in, CHUNKS_B)

    h = _norm_fwd(xf, g_pre_mix, "pre_mix_norm")
    proj_a = _matmul(h, w_a, "nn", BF16, "proj_a").reshape(bsz, seq, WA)
    proj_b = _matmul(h, w_b, "nn", BF16, "proj_b").reshape(bsz, seq, WB)
    o_a, o_a32, *behind = _sb_fwd(proj_a, seq, [shards[i] for i in GATHER_BEHIND])
    fw_mem_kv, fw_br_sb, fw_br_dil, fw_br_mem, fw_gate, fw_o, fw_ffn_in, fw_ffn_out = (
        _whole_weight(g, i) for g, i in zip(behind, GATHER_BEHIND))
    gpre = _matmul(h, fw_gate, "nn", BF16, "gate_proj")
    cos_t, sin_t = _rope_tables(seq)
    o_b, lse_b = _dil_fwd(proj_b, cos_t, sin_t, seq)
    mn = _norm_fwd(memf, g_mem, "mem_norm")
    kv = _matmul(mn, fw_mem_kv, "nn", BF16, "mem_kv_proj").reshape(bsz, MEM_LEN, D)
    o_c = _mem_fwd(proj_a, kv, seq)
    o_a2, o_b2, o_c2 = o_a.reshape(tokens, 512), o_b.reshape(tokens, 256), o_c.reshape(tokens, 512)
    ys = [_matmul(o_a2, fw_br_sb, "nn", BF16, "branch_sb"), _matmul(o_b2, fw_br_dil, "nn", BF16, "branch_dil"),
          _matmul(o_c2, fw_br_mem, "nn", BF16, "branch_mem")]
    merged = _gate_merge(gpre, ys, b_gate)
    mix, x1, h2 = _out_proj_norm(merged, fw_o, xf, g_post_mix, g_pre_ffn)
    gu = _matmul(h2, fw_ffn_in, "nn", BF16, "ffn_in")
    f = _swiglu_fwd(gu)
    dy, dfo, dg_post_ffn, loss_lanes = _ffn_out_loss(f, fw_ffn_out, x1, tgt, g_post_ffn)

    gw_ffn_out = _matmul(f, dfo, "tn", BF16, "gw_ffn_out")
    dgu = _d_ffn_swiglu_bwd(dfo, fw_ffn_out, gu)
    gw_ffn_in = _matmul(h2, dgu, "tn", BF16, "gw_ffn_in")
    dx1, dmix, dg_pre_ffn, dg_post_mix = _d_h2_norm_bwd(dgu, fw_ffn_in, x1, dy, mix, g_pre_ffn, g_post_mix)
    dmerged = _matmul(dmix, fw_o, "nt", BF16, "d_merged")
    gw_o = _matmul(merged, dmix, "tn", BF16, "gw_o")
    dya, dyb, dyc, dgpre, db_gate = _gate_bwd(dmerged, gpre, ys, b_gate)
    d_oa = _matmul(dya, fw_br_sb, "nt", BF16, "d_o_sb").reshape(bsz, seq, 512)
    d_ob = _matmul(dyb, fw_br_dil, "nt", BF16, "d_o_dil").reshape(bsz, seq, 256)
    d_oc = _matmul(dyc, fw_br_mem, "nt", BF16, "d_o_mem").reshape(bsz, seq, 512)
    gw_br_sb = _matmul(o_a2, dya, "tn", BF16, "gw_br_sb")
    gw_br_dil = _matmul(o_b2, dyb, "tn", BF16, "gw_br_dil")
    gw_br_mem = _matmul(o_c2, dyc, "tn", BF16, "gw_br_mem")
    gw_gate = _matmul(h, dgpre, "tn", BF16, "gw_gate")
    grads = {2: gw_br_sb, 3: gw_br_dil, 4: gw_br_mem, 5: gw_gate, 6: gw_o, 7: gw_ffn_in, 8: gw_ffn_out}
    d_proj_a, *recv_behind = _sb_bwd(proj_a, d_oa, o_a32, seq, [_shard_parts(grads[i], i) for i in REDUCE_BEHIND])
    d_proj_a, dk_m, dv_m = _mem_bwd(proj_a, kv, d_oc, d_proj_a, seq)
    d_proj_b = _dil_bwd(proj_b, cos_t, sin_t, d_ob, o_b, lse_b, seq).reshape(tokens, WB)
    d_proj_a = d_proj_a.reshape(tokens, WA)
    gw_a = _matmul(h, d_proj_a, "tn", BF16, "gw_in_a")
    gw_b = _matmul(h, d_proj_b, "tn", BF16, "gw_in_b")
    dkv = jnp.concatenate([dk_m, dv_m], axis=-1).reshape(bsz * MEM_LEN, D)
    gw_mem_kv = _matmul(mn, dkv, "tn", BF16, "gw_mem_kv")
    dmn = _matmul(dkv, fw_mem_kv, "nt", F32, "d_mem_norm")
    dg_mem = _gain_grad(dmn, memf)
    gw_ab = jnp.concatenate([gw_a, gw_b], axis=1)
    where = {c: i for i, c in enumerate(CHUNKS_A + CHUNKS_B)}
    grads = {0: _pick_chunks(gw_ab, [where[c] for c in range(34)]), 1: gw_mem_kv}
    dx, dg_pre_mix, *recv_last = _d_h_norm_bwd(
        [(dgpre, fw_gate), (d_proj_a, w_a), (d_proj_b, w_b)], xf, dx1, g_pre_mix,
        [_shard_parts(grads[i], i) for i in REDUCE_LAST])

    received = dict(zip(REDUCE_BEHIND + REDUCE_LAST, [*recv_behind, *recv_last]))
    adam = [_reduce_adamw(received[i], big_w[i], big_m[i], big_v[i], "reduce_adamw_" + BIG_NAMES[i])
            for i in range(len(big_w))]
    big = [[a[k] for a in adam] for k in range(4)]

    small = jnp.concatenate([dg_pre_mix, dg_post_mix, dg_pre_ffn, dg_post_ffn, dg_mem, db_gate.reshape(3, D),
                             loss_lanes, jnp.zeros((7, D), F32)], axis=0)
    small_all, = _exchange([small], True, "small_all_gather")

    def small_pack(gs, b):
        return jnp.concatenate([*gs, b.reshape(3, D)], axis=0)

    sm = _small_adamw(
        small_all, small_pack([g_pre_mix, g_post_mix, g_pre_ffn, g_post_ffn, g_mem], b_gate),
        small_pack([m_g_pre_mix, m_g_post_mix, m_g_pre_ffn, m_g_post_ffn, m_g_mem], m_b_gate),
        small_pack([v_g_pre_mix, v_g_post_mix, v_g_pre_ffn, v_g_post_ffn, v_g_mem], v_b_gate))
    loss = sm[4][0, 0]

    def leaves(k):
        t, bw = sm[k], big[k]
        return [t[0:1], t[1:2], t[2:3], t[3:4], t[4:5], *bw[0:6], t[5:8].reshape(1, 3 * D), *bw[6:9]]

    return (loss, dx.reshape(bsz, seq, D), *leaves(0), *leaves(1), *leaves(2), *leaves(3))
```

```python
import functools

import jax
import jax.numpy as jnp
from jax import lax
from jax.experimental import pallas as pl
from jax.experimental.pallas import tpu as pltpu

F32 = jnp.float32
BF16 = jnp.bfloat16
D = 1024
BLK = 128
MEM_LEN = 256
D_FF = 2816
NORM_EPS = 1e-6
NEG_INF = -1e30
ROPE_THETA = 10000.0
ADAM_LR, ADAM_B1, ADAM_B2, ADAM_EPS, ADAM_WD, ADAM_STEP = 0.001, 0.9, 0.999, 1e-08, 0.01, 10
N_DEV = 8
VMEM_LIMIT_BYTES = 56 * 1024 * 1024
MESH = pl.DeviceIdType.MESH
ANY = pl.BlockSpec(memory_space=pl.ANY)

NT = (((1,), (1,)), ((), ()))
TN = (((0,), (0,)), ((), ()))
NN = (((1,), (0,)), ((), ()))
_DIMS = {"nn": NN, "nt": NT, "tn": TN}

BIG_NAMES = ("w_in", "w_mem_kv", "w_br_sb", "w_br_dil", "w_br_mem", "w_gate", "w_o", "w_ffn_in", "w_ffn_out")
BY_ROWS = (False, True, False, False, False, False, True, False, True)
GATHER_FIRST = (0,)
GATHER_BEHIND = (1, 2, 3, 4, 5, 6, 7, 8)
REDUCE_BEHIND = (2, 3, 4, 5, 6, 7, 8)
REDUCE_LAST = (0, 1)

CHUNKS_A = tuple(c for hp in range(4) for c in (hp, 4 + hp, 8 + hp)) + (30, 31, 32, 33)
CHUNKS_B = tuple(c for hp in range(2) for g in range(3) for c in (12 + 6 * g + hp, 14 + 6 * g + hp, 16 + 6 * g + hp))
WA, WB = 128 * len(CHUNKS_A), 128 * len(CHUNKS_B)
DIL_GROUPS = (1, 4, 16)


def _params(*sem):
    return pltpu.CompilerParams(dimension_semantics=sem or None, vmem_limit_bytes=VMEM_LIMIT_BYTES)


def _tile(n, cap):
    if n <= 128:
        return n
    assert n % 128 == 0, n
    best = 128
    for t in range(128, min(n, cap) + 1, 128):
        if n % t == 0:
            best = t
    return best


def _k_steps(k, nk, step):
    if nk == 1:
        step(True, True)
        return
    pl.when(k == 0)(functools.partial(step, True, False))
    if nk > 2:
        pl.when(jnp.logical_and(k > 0, k < nk - 1))(functools.partial(step, False, False))
    pl.when(k == nk - 1)(functools.partial(step, False, True))


def _matmul(a, b, mode, out_dtype, name, tm_cap=1536, tn_cap=1536, tk_cap=1536):
    if mode == "tn":
        (K, M), N = a.shape, b.shape[1]
    elif mode == "nt":
        (M, K), N = a.shape, b.shape[0]
    else:
        (M, K), N = a.shape, b.shape[1]
    tm, tn, tk = _tile(M, tm_cap), _tile(N, tn_cap), _tile(K, tk_cap)
    nm, nn, nk = M // tm, N // tn, K // tk
    dims = _DIMS[mode]

    def body(a_ref, b_ref, o_ref, *acc):
        def step(first, last):
            d = lax.dot_general(a_ref[...], b_ref[...], dims, preferred_element_type=F32)
            if not first:
                d = d + acc[0][...]
            if last:
                o_ref[...] = d.astype(o_ref.dtype)
            else:
                acc[0][...] = d

        _k_steps(pl.program_id(2), nk, step)

    n_outer = nk == 1 and (a.size * nn + b.size) < (a.size + b.size * nm)
    if n_outer:
        grid, ij = (nn, nm, nk), (lambda g0, g1: (g1, g0))
    else:
        grid, ij = (nm, nn, nk), (lambda g0, g1: (g0, g1))
    if mode == "tn":
        a_spec = pl.BlockSpec((tk, tm), lambda g0, g1, k: (k, ij(g0, g1)[0]))
    else:
        a_spec = pl.BlockSpec((tm, tk), lambda g0, g1, k: (ij(g0, g1)[0], k))
    if mode == "nt":
        b_spec = pl.BlockSpec((tn, tk), lambda g0, g1, k: (ij(g0, g1)[1], k))
    else:
        b_spec = pl.BlockSpec((tk, tn), lambda g0, g1, k: (k, ij(g0, g1)[1]))
    return pl.pallas_call(
        body, name=name, grid=grid,
        out_shape=jax.ShapeDtypeStruct((M, N), out_dtype),
        in_specs=[a_spec, b_spec],
        out_specs=pl.BlockSpec((tm, tn), lambda g0, g1, k: ij(g0, g1)),
        scratch_shapes=[pltpu.VMEM((tm, tn), F32)] if nk > 1 else [],
        compiler_params=_params("parallel", "parallel", "arbitrary"),
    )(a, b)


def _rowwise(body, name, rows, tr, row_ins, vec_ins, row_outs, acc_outs=()):
    tr = min(tr, rows)
    assert rows % tr == 0
    in_specs, args = [], []
    for r in row_ins:
        arr, w, cb = r if isinstance(r, tuple) else (r, r.shape[1], 0)
        in_specs.append(pl.BlockSpec((tr, w), functools.partial(lambda i, cb: (i, cb), cb=cb)))
        args.append(arr)
    for v in vec_ins:
        in_specs.append(pl.BlockSpec(v.shape, lambda i: (0, 0)))
        args.append(v)
    out_shape = [jax.ShapeDtypeStruct((rows, w), dt) for w, dt in row_outs]
    out_shape += [jax.ShapeDtypeStruct((1, w), F32) for w in acc_outs]
    out_specs = [pl.BlockSpec((tr, w), lambda i: (i, 0)) for w, _ in row_outs]
    out_specs += [pl.BlockSpec((1, w), lambda i: (0, 0)) for w in acc_outs]
    n_acc = len(acc_outs)

    def wrapped(*refs):
        if n_acc:
            @pl.when(pl.program_id(0) == 0)
            def _():
                for r in refs[len(refs) - n_acc:]:
                    r[...] = jnp.zeros_like(r)
        body(*refs)

    return pl.pallas_call(
        wrapped, name=name, grid=(rows // tr,), out_shape=out_shape, in_specs=in_specs, out_specs=out_specs,
        compiler_params=_params("arbitrary"),
    )(*args)


def _rstd(x):
    return lax.rsqrt(jnp.mean(x * x, axis=-1, keepdims=True) + NORM_EPS)


def _norm_bwd(u, n, r):
    return r * (u - n * jnp.mean(u * n, axis=-1, keepdims=True))


def _colsum(v):
    return jnp.sum(v, axis=0, keepdims=True)


def _norm_fwd(x, g, name):
    def body(x_ref, g_ref, h_ref):
        xv = x_ref[...]
        h_ref[...] = ((xv * _rstd(xv)) * g_ref[...]).astype(BF16)

    return _rowwise(body, name, x.shape[0], 512, [x], [g], [(D, BF16)])[0]


def _matmul_rows(pairs, mode, name, epilogue, row_ins=(), vec_ins=(), row_outs=(), acc_outs=(), ride=None,
                 tm=512, tk_cap=1536):
    M = pairs[0][0].shape[0]
    N = pairs[0][1].shape[1] if mode == "nn" else pairs[0][1].shape[0]
    tm = min(tm, M)
    tks = [_tile(a.shape[1], tk_cap) for a, _ in pairs]
    nks = [a.shape[1] // tk for (a, _), tk in zip(pairs, tks)]
    offs = [sum(nks[:p]) for p in range(len(pairs))]
    nm, nk = M // tm, sum(nks)
    dims = _DIMS[mode]
    n_ab, n_extra, n_out = 2 * len(pairs), len(row_ins) + len(vec_ins), len(row_outs) + len(acc_outs)
    n_ride = 0 if ride is None else len(ride)

    def body(*refs):
        ab, extra, rest = refs[:n_ab], refs[n_ab:n_ab + n_extra], refs[n_ab + n_extra:]
        ride_refs, outs, rest = rest[:n_ride], rest[n_ride:n_ride + n_out], rest[n_ride + n_out:]
        received_refs, rest = rest[:n_ride], rest[n_ride:]
        if n_ride:
            finish_ride = _riding_exchange(ride_refs, received_refs, rest[len(rest) - 3:], gather=False)
        i, k = pl.program_id(0), pl.program_id(1)
        if acc_outs:
            @pl.when(jnp.logical_and(i == 0, k == 0))
            def _():
                for r in outs[len(row_outs):]:
                    r[...] = jnp.zeros_like(r)

        def step(p, first, last):
            d = lax.dot_general(ab[2 * p][...], ab[2 * p + 1][...], dims, preferred_element_type=F32)
            if not first:
                d = d + rest[0][...]
            if last:
                epilogue(d, *extra, *outs)
            else:
                rest[0][...] = d

        last_p = len(pairs) - 1
        if nk == 1:
            step(0, True, True)
        else:
            pl.when(k == 0)(functools.partial(step, 0, True, False))
            for p in range(len(pairs)):
                lo, hi = max(offs[p], 1), min(offs[p] + nks[p], nk - 1)
                if hi > lo:
                    pl.when(jnp.logical_and(k >= lo, k < hi))(functools.partial(step, p, False, False))
            pl.when(k == nk - 1)(functools.partial(step, last_p, False, True))
        if n_ride:
            finish_ride()

    in_specs, args = [], []
    for p, ((a, b), tk) in enumerate(zip(pairs, tks)):
        step = functools.partial(lambda k, p: jnp.clip(k - offs[p], 0, nks[p] - 1), p=p)
        in_specs.append(pl.BlockSpec((tm, tk), functools.partial(lambda i, k, step: (i, step(k)), step=step)))
        if mode == "nn":
            in_specs.append(pl.BlockSpec((tk, N), functools.partial(lambda i, k, step: (step(k), 0), step=step)))
        else:
            in_specs.append(pl.BlockSpec((N, tk), functools.partial(lambda i, k, step: (0, step(k)), step=step)))
        args += [a, b]
    in_specs += [pl.BlockSpec((tm, r.shape[1]), lambda i, k: (i, 0)) for r in row_ins]
    in_specs += [pl.BlockSpec(v.shape, lambda i, k: (0, 0)) for v in vec_ins]
    in_specs += [ANY] * n_ride
    out_shape = [jax.ShapeDtypeStruct((M, w), dt) for w, dt in row_outs]
    out_shape += [jax.ShapeDtypeStruct((1, w), F32) for w in acc_outs]
    out_specs = [pl.BlockSpec((tm, w), lambda i, k: (i, 0)) for w, _ in row_outs]
    out_specs += [pl.BlockSpec((1, w), lambda i, k: (0, 0)) for w in acc_outs]
    scratch = [pltpu.VMEM((tm, N), F32)] if nk > 1 else []
    if n_ride:
        out_shape += _exchange_out(ride, False)
        out_specs += [ANY] * n_ride
        scratch += _exchange_sems(n_ride)
    return pl.pallas_call(
        body, name=name, grid=(nm, nk), out_shape=out_shape, in_specs=in_specs, out_specs=out_specs,
        scratch_shapes=scratch, compiler_params=_params("arbitrary", "arbitrary"),
    )(*args, *row_ins, *vec_ins, *(ride or []))


def _out_proj_norm(merged, w_o, x, g_post, g_pre):
    def epilogue(mv, x_ref, g2_ref, g3_ref, mix_ref, x1_ref, h2_ref):
        mix_ref[...] = mv
        x1 = x_ref[...] + (mv * _rstd(mv)) * g2_ref[...]
        x1_ref[...] = x1
        h2_ref[...] = ((x1 * _rstd(x1)) * g3_ref[...]).astype(BF16)

    return _matmul_rows([(merged, w_o)], "nn", "out_proj_norm", epilogue, [x], [g_post, g_pre],
                        [(D, F32), (D, F32), (D, BF16)])


def _gate_merge(gpre, ys, b_gate):
    def body(gp_ref, ya_ref, yb_ref, yc_ref, b_ref, m_ref):
        acc = None
        for k, y_ref in enumerate((ya_ref, yb_ref, yc_ref)):
            cols = slice(k * D, (k + 1) * D)
            gate = jax.nn.sigmoid(gp_ref[:, cols].astype(F32) + b_ref[:, cols])
            term = gate * y_ref[...].astype(F32)
            acc = term if acc is None else acc + term
        m_ref[...] = acc.astype(BF16)

    return _rowwise(body, "gate_merge", gpre.shape[0], 256, [gpre, *ys], [b_gate], [(D, BF16)])[0]


def _swiglu_fwd(gu):
    def body(a_ref, b_ref, f_ref):
        a = a_ref[...].astype(F32)
        f_ref[...] = (a * jax.nn.sigmoid(a) * b_ref[...].astype(F32)).astype(BF16)

    return _rowwise(body, "swiglu_fwd", gu.shape[0], 256, [(gu, D_FF, 0), (gu, D_FF, 1)], [], [(D_FF, BF16)])[0]


def _ffn_out_loss(f, w_ffn_out, x1, tgt, g_post):
    def epilogue(fo_v, x1_ref, t_ref, g_ref, dy_ref, dfo_ref, dg_ref, loss_ref):
        r = _rstd(fo_v)
        n = fo_v * r
        err = (x1_ref[...] + n * g_ref[...]) - t_ref[...]
        loss_ref[...] += _colsum(err * err)
        dy = err * (1.0 / D)
        dy_ref[...] = dy
        dg_ref[...] += _colsum(dy * n)
        dfo_ref[...] = _norm_bwd(dy * g_ref[...], n, r).astype(BF16)

    return _matmul_rows([(f, w_ffn_out)], "nn", "ffn_out_loss", epilogue, [x1, tgt], [g_post],
                        [(D, F32), (D, BF16)], (D, D))


def _d_ffn_swiglu_bwd(dfo, w_ffn_out, gu):
    def epilogue(d, gu_ref, dgu_ref):
        a = gu_ref[:, :D_FF].astype(F32)
        b = gu_ref[:, D_FF:].astype(F32)
        s = jax.nn.sigmoid(a)
        dgu_ref[:, :D_FF] = (d * b * (s * (1.0 + a * (1.0 - s)))).astype(BF16)
        dgu_ref[:, D_FF:] = (d * (a * s)).astype(BF16)

    return _matmul_rows([(dfo, w_ffn_out)], "nt", "d_ffn_swiglu_bwd", epilogue, [gu], [], [(2 * D_FF, BF16)],
                        tm=256)[0]


def _d_h2_norm_bwd(dgu, w_ffn_in, x1, dy, mix, g_pre, g_post):
    def epilogue(dh, x1_ref, dy_ref, mix_ref, g3_ref, g2_ref, dx1_ref, dmix_ref, dg3_ref, dg2_ref):
        x1v = x1_ref[...]
        r3 = _rstd(x1v)
        n3 = x1v * r3
        dg3_ref[...] += _colsum(dh * n3)
        dx1 = dy_ref[...] + _norm_bwd(dh * g3_ref[...], n3, r3)
        dx1_ref[...] = dx1
        mv = mix_ref[...]
        r2 = _rstd(mv)
        n2 = mv * r2
        dg2_ref[...] += _colsum(dx1 * n2)
        dmix_ref[...] = _norm_bwd(dx1 * g2_ref[...], n2, r2).astype(BF16)

    return _matmul_rows([(dgu, w_ffn_in)], "nt", "d_h2_norm_bwd", epilogue, [x1, dy, mix], [g_pre, g_post],
                        [(D, F32), (D, BF16)], (D, D))


def _gate_bwd(dmerged, gpre, ys, b_gate):
    def body(dm_ref, gp_ref, ya_ref, yb_ref, yc_ref, b_ref, dya_ref, dyb_ref, dyc_ref, dgp_ref, db_ref):
        dm = dm_ref[...].astype(F32)
        for k, (y_ref, dy_ref) in enumerate(((ya_ref, dya_ref), (yb_ref, dyb_ref), (yc_ref, dyc_ref))):
            cols = slice(k * D, (k + 1) * D)
            gate = jax.nn.sigmoid(gp_ref[:, cols].astype(F32) + b_ref[:, cols])
            dy_ref[...] = (dm * gate).astype(BF16)
            dgp = (dm * y_ref[...].astype(F32)) * (gate * (1.0 - gate))
            dgp_ref[:, cols] = dgp.astype(BF16)
            db_ref[:, cols] += _colsum(dgp)

    return _rowwise(body, "gate_bwd", gpre.shape[0], 256, [dmerged, gpre, *ys], [b_gate],
                    [(D, BF16), (D, BF16), (D, BF16), (3 * D, BF16)], (3 * D,))


def _d_h_norm_bwd(pairs, x, dx1, g_pre, ride):
    def epilogue(dh, x_ref, dx1_ref, g_ref, dx_ref, dg_ref):
        xv = x_ref[...]
        r = _rstd(xv)
        n = xv * r
        dg_ref[...] += _colsum(dh * n)
        dx_ref[...] = dx1_ref[...] + _norm_bwd(dh * g_ref[...], n, r)

    return _matmul_rows(pairs, "nt", "d_h_norm_bwd", epilogue, [x, dx1], [g_pre], [(D, F32)], (D,), ride=ride)


def _gain_grad(dmn, mem):
    def body(d_ref, m_ref, dg_ref):
        mv = m_ref[...]
        dg_ref[...] += _colsum(d_ref[...] * (mv * _rstd(mv)))

    return _rowwise(body, "mem_gain_grad", mem.shape[0], 256, [dmn, mem], [], [], (D,))[0]


def _head_rowsum(v, head0):
    return (jnp.sum(jnp.where(head0, v, 0.0), axis=1, keepdims=True),
            jnp.sum(jnp.where(head0, 0.0, v), axis=1, keepdims=True))


KT = 256
SB_SCALE = 0.125


def _make_suffix(inclusive):
    row, col = lax.broadcasted_iota(jnp.int32, (KT, KT), 0), lax.broadcasted_iota(jnp.int32, (KT, KT), 1)
    tri = (row >= col if inclusive else row > col).astype(BF16)
    tri2 = jnp.concatenate([tri, tri], axis=0)

    def suffix(x):
        hi = x.astype(BF16)
        lo = (x - hi.astype(F32)).astype(BF16)
        return jnp.dot(jnp.concatenate([hi, lo], axis=1), tri2, preferred_element_type=F32)

    return suffix


def _sb_scores(qh, k, mask, suffix_incl, later):
    z = lax.dot_general(qh, k, NT, preferred_element_type=F32)
    zc = jnp.minimum(z, 60.0)
    sp = jnp.log(1.0 + jnp.exp(zc))
    if mask is not None:
        sp = jnp.where(mask, sp, 0.0)
    a = jnp.exp((zc - suffix_incl(sp)) - later)
    if mask is not None:
        a = jnp.where(mask, a, 0.0)
    return zc, sp, a


QB = KT


def _sb_tiles(i, tile, init):
    st = tile(i, init, True)
    st = lax.fori_loop(0, lax.shift_right_logical(i, 1),
                       lambda t, s: tile(i - 2 - 2 * t, tile(i - 1 - 2 * t, s, False), False), st)
    return lax.cond((i & 1) == 1, lambda s: tile(0, s, False), lambda s: s, st)


def _sb_consts():
    head0 = lax.broadcasted_iota(jnp.int32, (QB, BLK), 1) < 64
    row = lax.broadcasted_iota(jnp.int32, (2 * QB, KT), 0) & (QB - 1)
    return head0, row > lax.broadcasted_iota(jnp.int32, (2 * QB, KT), 1)


def _stack_heads(v, head0):
    zero = jnp.zeros_like(v)
    return jnp.concatenate([jnp.where(head0, v, zero), jnp.where(head0, zero, v)], axis=0)


def _unstack_heads(v, head0):
    n = v.shape[0] // 2
    return jnp.where(head0, v[:n], v[n:])


def _sb_fwd(proj_a, seq, ride):
    bsz = proj_a.shape[0]
    n_ride = len(ride)

    def body(x_ref, *rest):
        ride_refs, (o_ref, o32_ref), rest = rest[:n_ride], rest[n_ride:n_ride + 2], rest[n_ride + 2:]
        gathered_refs, acc_ref, sems = rest[:n_ride], rest[n_ride], rest[n_ride + 1:]
        finish_ride = _riding_exchange(ride_refs, gathered_refs, sems, gather=True)
        head0, diag_mask = _sb_consts()
        suffix_incl = _make_suffix(True)

        def qblock(i, carry):
            r0 = pl.multiple_of(i * QB, QB)
            qs = _stack_heads(x_ref[pl.ds(r0, QB), 0:128] * jnp.asarray(SB_SCALE, BF16), head0)

            def tile(jt, later, masked):
                c0 = pl.multiple_of(jt * KT, KT)
                k = x_ref[pl.ds(c0, KT), 128:256]
                v = x_ref[pl.ds(c0, KT), 256:384]
                _, sp, a = _sb_scores(qs, k, diag_mask if masked else None, suffix_incl, later)
                pv = jnp.dot(a.astype(BF16), v, preferred_element_type=F32)
                if masked:
                    acc_ref[...] = pv
                else:
                    acc_ref[...] += pv
                return later + jnp.sum(sp, axis=1, keepdims=True)

            _sb_tiles(i, tile, jnp.zeros((2 * QB, 1), F32))
            o = _unstack_heads(acc_ref[...], head0)
            o32_ref[pl.ds(r0, QB), :] = o
            o_ref[pl.ds(r0, QB), :] = o.astype(BF16)
            return carry

        lax.fori_loop(0, seq // QB, qblock, 0)
        finish_ride()

    out_spec = pl.BlockSpec((None, seq, BLK), lambda b, hp: (b, 0, hp))
    return pl.pallas_call(
        body, name="sb_attn_fwd", grid=(bsz, 4),
        out_shape=[jax.ShapeDtypeStruct((bsz, seq, 512), BF16), jax.ShapeDtypeStruct((bsz, seq, 512), F32),
                   *_exchange_out(ride, True)],
        in_specs=[pl.BlockSpec((None, seq, 384), lambda b, hp: (b, 0, hp))] + [ANY] * n_ride,
        out_specs=[out_spec, out_spec] + [ANY] * n_ride,
        scratch_shapes=[pltpu.VMEM((2 * QB, BLK), F32), *_exchange_sems(n_ride)],
        compiler_params=_params("arbitrary", "arbitrary"),
    )(proj_a, *ride)


def _sb_bwd(proj_a, d_o, o_a, seq, ride):
    bsz = proj_a.shape[0]
    n_ride = len(ride)

    def body(x_ref, do_ref, o_ref, *rest):
        ride_refs, d_ref, rest = rest[:n_ride], rest[n_ride], rest[n_ride + 1:]
        received_refs, (dq_acc, dk_acc, dv_acc), sems = rest[:n_ride], rest[n_ride:n_ride + 3], rest[n_ride + 3:]
        finish_ride = _riding_exchange(ride_refs, received_refs, sems, gather=False)
        head0, diag_mask = _sb_consts()
        suffix_incl, suffix_excl = _make_suffix(True), _make_suffix(False)
        dk_acc[...] = jnp.zeros_like(dk_acc)
        dv_acc[...] = jnp.zeros_like(dv_acc)

        def qblock(i, carry):
            r0 = pl.multiple_of(i * QB, QB)
            qs = _stack_heads(x_ref[pl.ds(r0, QB), 0:128] * jnp.asarray(SB_SCALE, BF16), head0)
            do = do_ref[pl.ds(r0, QB), :]
            dos = _stack_heads(do, head0)
            dsum = jnp.concatenate(_head_rowsum(do.astype(F32) * o_ref[pl.ds(r0, QB), :], head0), axis=0)

            def tile(jt, st, masked):
                later, rest_g = st
                c0 = pl.multiple_of(jt * KT, KT)
                k = x_ref[pl.ds(c0, KT), 128:256]
                v = x_ref[pl.ds(c0, KT), 256:384]
                zc, sp, a = _sb_scores(qs, k, diag_mask if masked else None, suffix_incl, later)
                a16 = a.astype(BF16)
                g = a16.astype(F32) * lax.dot_general(dos, v, NT, preferred_element_type=F32)
                dz = g - jnp.exp(zc - sp) * (rest_g - suffix_excl(g))
                if masked:
                    dz = jnp.where(diag_mask, dz, 0.0)
                dz = dz.astype(BF16)
                dq = jnp.dot(dz, k, preferred_element_type=F32)
                if masked:
                    dq_acc[...] = dq
                else:
                    dq_acc[...] += dq
                dk_acc[pl.ds(c0, KT), :] += lax.dot_general(dz, qs, TN, preferred_element_type=F32)
                dv_acc[pl.ds(c0, KT), :] += lax.dot_general(a16, dos, TN, preferred_element_type=F32)
                return later + jnp.sum(sp, axis=1, keepdims=True), rest_g - jnp.sum(g, axis=1, keepdims=True)

            _sb_tiles(i, tile, (jnp.zeros((2 * QB, 1), F32), dsum))
            d_ref[pl.ds(r0, QB), 0:128] = (_unstack_heads(dq_acc[...], head0) * SB_SCALE).astype(BF16)
            return carry

        lax.fori_loop(0, seq // QB, qblock, 0)
        d_ref[:, 128:256] = dk_acc[...].astype(BF16)
        d_ref[:, 256:384] = dv_acc[...].astype(BF16)
        finish_ride()

    return pl.pallas_call(
        body, name="sb_attn_bwd", grid=(bsz, 4),
        out_shape=[jax.ShapeDtypeStruct((bsz, seq, WA), BF16), *_exchange_out(ride, False)],
        in_specs=[pl.BlockSpec((None, seq, 384), lambda b, hp: (b, 0, hp)),
                  pl.BlockSpec((None, seq, BLK), lambda b, hp: (b, 0, hp)),
                  pl.BlockSpec((None, seq, BLK), lambda b, hp: (b, 0, hp))] + [ANY] * n_ride,
        out_specs=[pl.BlockSpec((None, seq, 384), lambda b, hp: (b, 0, hp))] + [ANY] * n_ride,
        scratch_shapes=[pltpu.VMEM((2 * QB, BLK), F32), pltpu.VMEM((seq, BLK), F32), pltpu.VMEM((seq, BLK), F32),
                        *_exchange_sems(n_ride)],
        compiler_params=_params("arbitrary", "arbitrary"),
    )(proj_a, d_o, o_a, *ride)


def _rope_tables(seq):
    inv_freq = ROPE_THETA ** (-jnp.arange(32, dtype=F32) * 2.0 / 64)
    ang = jnp.arange(seq).astype(F32)[:, None] * inv_freq[None, :]
    cos, sin = jnp.cos(ang), jnp.sin(ang)
    return jnp.tile(cos, (1, 4)), jnp.concatenate([-sin, sin, -sin, sin], axis=1)


def _make_rope(n_rows):
    lane = lax.broadcasted_iota(jnp.int32, (n_rows, BLK), 1)
    first = (lane & 63) < 32

    def rope(x, cos, sin):
        partner = jnp.where(first, pltpu.roll(x, 96, 1), pltpu.roll(x, 32, 1))
        return x * cos + partner * sin

    return rope


DIL_UNROLL = 4


def _dil_consts():
    head0 = lax.broadcasted_iota(jnp.int32, (BLK, BLK), 1) < 64
    row = lax.broadcasted_iota(jnp.int32, (2 * BLK, 2 * BLK), 0) & (BLK - 1)
    col = lax.broadcasted_iota(jnp.int32, (2 * BLK, 2 * BLK), 1)
    valid_prev = jnp.logical_and(col < BLK, col >= row)
    valid_cur = jnp.logical_and(col >= BLK, row >= col - BLK)
    return head0, valid_prev, valid_cur


def _dil_blocks(dil, seq, block):
    nq = seq // dil // BLK

    def rows(r, i):
        if dil == 1:
            return pl.ds(pl.multiple_of(i * BLK, BLK), BLK)
        return pl.ds(r + (dil * BLK) * i, BLK, stride=dil)

    def step(t, carry):
        for u in range(DIL_UNROLL):
            n = t * DIL_UNROLL + u
            r, i = lax.div(n, nq), lax.rem(n, nq)
            block(rows(r, i), rows(r, jnp.maximum(i - 1, 0)), i)
        return carry

    lax.fori_loop(0, seq // BLK // DIL_UNROLL, step, 0)


def _dil_scores(qf, kf, vf, cur, prev, i, consts):
    head0, valid_prev, valid_cur = consts
    qs = _stack_heads(qf[cur, :].astype(BF16), head0)
    kcat = jnp.concatenate([kf[prev, :], kf[cur, :]], axis=0).astype(BF16)
    vcat = jnp.concatenate([vf[prev, :], vf[cur, :]], axis=0).astype(BF16)
    valid = jnp.logical_or(valid_cur, jnp.logical_and(valid_prev, i > 0))
    s = lax.dot_general(qs, kcat, NT, preferred_element_type=F32) * 0.125
    return qs, kcat, vcat, s, valid


def _head_cols(v):
    return jnp.concatenate([v[:, 0:1], v[:, 64:65]], axis=0)


def _dil_load_qkv(x_ref, c, rope, cos, sin, qf, kf, vf):
    qf[...] = rope(x_ref[:, c:c + 128].astype(F32), cos, sin).astype(BF16).astype(F32)
    kf[...] = rope(x_ref[:, c + 128:c + 256].astype(F32), cos, sin).astype(BF16).astype(F32)
    vf[...] = x_ref[:, c + 256:c + 384].astype(F32)


def _dil_fwd(proj_b, cos_t, sin_t, seq):
    bsz = proj_b.shape[0]

    def body(x_ref, cos_ref, sin_ref, ob_ref, lse_ref, qf, kf, vf, og, lg):
        consts = _dil_consts()
        head0 = consts[0]
        rope = _make_rope(seq)
        cos, sin = cos_ref[...], sin_ref[...]
        for g, dil in enumerate(DIL_GROUPS):
            _dil_load_qkv(x_ref, 384 * g, rope, cos, sin, qf, kf, vf)

            def block(cur, prev, i, g=g):
                _, _, vcat, s, valid = _dil_scores(qf, kf, vf, cur, prev, i, consts)
                s = jnp.where(valid, s, NEG_INF)
                m = jnp.max(s, axis=1, keepdims=True)
                p = jnp.exp(s - m)
                den = jnp.sum(p, axis=1, keepdims=True)
                o = jnp.dot(p.astype(BF16), vcat, preferred_element_type=F32) / den
                og[g, cur, :] = _unstack_heads(o, head0)
                lg[g, cur, :] = _unstack_heads(jnp.broadcast_to(m + jnp.log(den), (2 * BLK, BLK)), head0)

            _dil_blocks(dil, seq, block)
        ls = [lg[0], lg[1], lg[2]]
        m = jnp.maximum(jnp.maximum(ls[0], ls[1]), ls[2])
        ws = [jnp.exp(l - m) for l in ls]
        den = (ws[0] + ws[1]) + ws[2]
        ob_ref[...] = (((ws[0] * og[0] + ws[1] * og[1]) + ws[2] * og[2]) / den).astype(BF16)
        lse_ref[...] = m + jnp.log(den)

    tab_spec = pl.BlockSpec((seq, BLK), lambda b, hp: (0, 0))
    out_spec = pl.BlockSpec((None, seq, BLK), lambda b, hp: (b, 0, hp))
    slab = pltpu.VMEM((seq, BLK), F32)
    return pl.pallas_call(
        body, name="dil_attn_fwd", grid=(bsz, 2),
        out_shape=(jax.ShapeDtypeStruct((bsz, seq, 256), BF16), jax.ShapeDtypeStruct((bsz, seq, 256), F32)),
        in_specs=[pl.BlockSpec((None, seq, WB // 2), lambda b, hp: (b, 0, hp)), tab_spec, tab_spec],
        out_specs=(out_spec, out_spec),
        scratch_shapes=[slab, slab, slab, pltpu.VMEM((3, seq, BLK), F32), pltpu.VMEM((3, seq, BLK), F32)],
        compiler_params=_params("parallel", "parallel"),
    )(proj_b, cos_t, sin_t)


def _dil_bwd(proj_b, cos_t, sin_t, d_ob, o_b, lse, seq):
    bsz = proj_b.shape[0]

    def body(x_ref, cos_ref, sin_ref, do_ref, ob_ref, lse_ref, d_ref, qf, kf, vf, dof, dsf, dq_s, dk_acc, dv_acc):
        consts = _dil_consts()
        head0 = consts[0]
        rope = _make_rope(seq)
        cos, sin = cos_ref[...], sin_ref[...]
        do_all = do_ref[...].astype(F32)
        dof[...] = do_all
        head0_all = lax.broadcasted_iota(jnp.int32, (seq, BLK), 1) < 64
        d0, d1 = _head_rowsum(do_all * ob_ref[...].astype(F32), head0_all)
        dsf[...] = jnp.where(head0_all, d0, d1)
        for g, dil in enumerate(DIL_GROUPS):
            _dil_load_qkv(x_ref, 384 * g, rope, cos, sin, qf, kf, vf)
            dk_acc[...] = jnp.zeros_like(dk_acc)
            dv_acc[...] = jnp.zeros_like(dv_acc)

            def block(cur, prev, i):
                qs, kcat, vcat, s, valid = _dil_scores(qf, kf, vf, cur, prev, i, consts)
                dos = _stack_heads(dof[cur, :].astype(BF16), head0)
                p = jnp.where(valid, jnp.exp(s - _head_cols(lse_ref[cur, :])), 0.0)
                dp = lax.dot_general(dos, vcat, NT, preferred_element_type=F32)
                ds = ((p * (dp - _head_cols(dsf[cur, :]))) * 0.125).astype(BF16)
                dq_s[cur, :] = _unstack_heads(jnp.dot(ds, kcat, preferred_element_type=F32), head0)
                dk = lax.dot_general(ds, qs, TN, preferred_element_type=F32)
                dv = lax.dot_general(p.astype(BF16), dos, TN, preferred_element_type=F32)
                dk_acc[prev, :] += dk[:BLK]
                dk_acc[cur, :] += dk[BLK:]
                dv_acc[prev, :] += dv[:BLK]
                dv_acc[cur, :] += dv[BLK:]

            _dil_blocks(dil, seq, block)
            c = 384 * g
            d_ref[:, c:c + 128] = rope(dq_s[...], cos, -sin).astype(BF16)
            d_ref[:, c + 128:c + 256] = rope(dk_acc[...], cos, -sin).astype(BF16)
            d_ref[:, c + 256:c + 384] = dv_acc[...].astype(BF16)

    x_spec = pl.BlockSpec((None, seq, WB // 2), lambda b, hp: (b, 0, hp))
    tab_spec = pl.BlockSpec((seq, BLK), lambda b, hp: (0, 0))
    tok_spec = pl.BlockSpec((None, seq, BLK), lambda b, hp: (b, 0, hp))
    return pl.pallas_call(
        body, name="dil_attn_bwd", grid=(bsz, 2),
        out_shape=jax.ShapeDtypeStruct((bsz, seq, WB), BF16),
        in_specs=[x_spec, tab_spec, tab_spec, tok_spec, tok_spec, tok_spec], out_specs=x_spec,
        scratch_shapes=[pltpu.VMEM((seq, BLK), F32)] * 8,
        compiler_params=_params("parallel", "parallel"),
    )(proj_b, cos_t, sin_t, d_ob, o_b, lse)


MEM_SCALE = 128 ** -0.5
MEM_QB = 1024


def _mem_fwd(proj_a, kv, seq):
    bsz = proj_a.shape[0]

    def body(q_ref, k_ref, v_ref, o_ref):
        k, v = k_ref[...], v_ref[...]

        def qblock(i, carry):
            r0 = pl.multiple_of(i * MEM_QB, MEM_QB)
            s = lax.dot_general(q_ref[pl.ds(r0, MEM_QB), :], k, NT, preferred_element_type=F32) * MEM_SCALE
            p = jnp.exp(s - jnp.max(s, axis=1, keepdims=True))
            p = p / jnp.sum(p, axis=1, keepdims=True)
            o_ref[pl.ds(r0, MEM_QB), :] = jnp.dot(p.astype(BF16), v, preferred_element_type=F32).astype(BF16)
            return carry

        lax.fori_loop(0, seq // MEM_QB, qblock, 0)

    return pl.pallas_call(
        body, name="mem_attn_fwd", grid=(bsz, 4),
        out_shape=jax.ShapeDtypeStruct((bsz, seq, 512), BF16),
        in_specs=[pl.BlockSpec((None, seq, BLK), lambda b, h: (b, 0, 12 + h)),
                  pl.BlockSpec((None, MEM_LEN, BLK), lambda b, h: (b, 0, h)),
                  pl.BlockSpec((None, MEM_LEN, BLK), lambda b, h: (b, 0, 4 + h))],
        out_specs=pl.BlockSpec((None, seq, BLK), lambda b, h: (b, 0, h)),
        compiler_params=_params("parallel", "parallel"),
    )(proj_a, kv, kv)


def _mem_bwd(proj_a, kv, d_o, d_proj_a, seq):
    bsz = proj_a.shape[0]

    def body(q_ref, k_ref, v_ref, do_ref, _, dq_ref, dk_ref, dv_ref):
        k, v = k_ref[...], v_ref[...]

        def qblock(i, carry):
            dk, dv = carry
            r0 = pl.multiple_of(i * MEM_QB, MEM_QB)
            q, do = q_ref[pl.ds(r0, MEM_QB), :], do_ref[pl.ds(r0, MEM_QB), :]
            s = lax.dot_general(q, k, NT, preferred_element_type=F32) * MEM_SCALE
            p = jnp.exp(s - jnp.max(s, axis=1, keepdims=True))
            p = p / jnp.sum(p, axis=1, keepdims=True)
            dp = lax.dot_general(do, v, NT, preferred_element_type=F32)
            ds = ((p * (dp - jnp.sum(p * dp, axis=1, keepdims=True))) * MEM_SCALE).astype(BF16)
            dq_ref[pl.ds(r0, MEM_QB), :] = jnp.dot(ds, k, preferred_element_type=F32).astype(BF16)
            dk = dk + lax.dot_general(ds, q, TN, preferred_element_type=F32)
            dv = dv + lax.dot_general(p.astype(BF16), do, TN, preferred_element_type=F32)
            return dk, dv

        zero = jnp.zeros((MEM_LEN, BLK), F32)
        dk, dv = lax.fori_loop(0, seq // MEM_QB, qblock, (zero, zero))
        dk_ref[...] = dk.astype(BF16)
        dv_ref[...] = dv.astype(BF16)

    kv_spec = pl.BlockSpec((None, MEM_LEN, BLK), lambda b, h: (b, 0, h))
    return pl.pallas_call(
        body, name="mem_attn_bwd", grid=(bsz, 4),
        out_shape=(jax.ShapeDtypeStruct((bsz, seq, WA), BF16), jax.ShapeDtypeStruct((bsz, MEM_LEN, 512), BF16),
                   jax.ShapeDtypeStruct((bsz, MEM_LEN, 512), BF16)),
        in_specs=[pl.BlockSpec((None, seq, BLK), lambda b, h: (b, 0, 12 + h)), kv_spec,
                  pl.BlockSpec((None, MEM_LEN, BLK), lambda b, h: (b, 0, 4 + h)),
                  pl.BlockSpec((None, seq, BLK), lambda b, h: (b, 0, h)), ANY],
        out_specs=(pl.BlockSpec((None, seq, BLK), lambda b, h: (b, 0, 12 + h)), kv_spec, kv_spec),
        input_output_aliases={4: 0},
        compiler_params=_params("parallel", "parallel"),
    )(proj_a, kv, kv, d_o, d_proj_a)


def _mesh_pos():
    return lax.axis_index("x"), lax.axis_index("y"), lax.axis_index("c")


def _all_gather(shard, name):
    m_per, n = shard.shape

    def body(x_ref, out_ref, send_sems, recv_sems, local_sem):
        x, y, c = _mesh_pos()
        me, sibling = (x, y, c), (x, y, 1 - c)
        chips = [(1 - x, y), (x, 1 - y), (1 - x, 1 - y)]

        def rows(px, py, pc):
            return out_ref.at[pl.ds((4 * px + 2 * py + pc) * m_per, m_per), :]

        def copy(k, block, to, src=None):
            return pltpu.make_async_remote_copy(
                src_ref=rows(*block) if src is None else src, dst_ref=rows(*block),
                send_sem=send_sems.at[k], recv_sem=recv_sems.at[k], device_id=to, device_id_type=MESH)

        mine = pltpu.make_async_copy(x_ref, rows(*me), local_sem)
        mine.start()
        first = [copy(0, me, sibling, src=x_ref)]
        first += [copy(1 + j, me, (*chip, c), src=x_ref) for j, chip in enumerate(chips)]
        for cp in first:
            cp.start()
        passed = [copy(4 + j, (*chip, c), sibling) for j, chip in enumerate(chips)]
        for j, chip in enumerate(chips):
            copy(1 + j, (*chip, c), me).wait_recv()
            passed[j].start()
        copy(0, sibling, me).wait_recv()
        for j, chip in enumerate(chips):
            copy(4 + j, (*chip, 1 - c), me).wait_recv()
        for cp in first + passed:
            cp.wait_send()
        mine.wait()

    return pl.pallas_call(
        body, name=name, out_shape=jax.ShapeDtypeStruct((N_DEV * m_per, n), shard.dtype),
        in_specs=[ANY], out_specs=ANY,
        scratch_shapes=[pltpu.SemaphoreType.DMA((7,)), pltpu.SemaphoreType.DMA((7,)), pltpu.SemaphoreType.DMA(())],
    )(shard)


def _exchange_sems(n_arrays):
    return [pltpu.SemaphoreType.DMA((7 * n_arrays,)), pltpu.SemaphoreType.DMA((7 * n_arrays,)),
            pltpu.SemaphoreType.DMA((n_arrays,))]


def _exchange_out(srcs, gather):
    return [jax.ShapeDtypeStruct((N_DEV, *s.shape[-2:]), s.dtype) for s in srcs]


def _direct_exchange(src_refs, dst_refs, send_sems, recv_sems, local_sems, gather):
    x, y, c = _mesh_pos()
    me = 4 * x + 2 * y + c
    owns, sends, recvs = [], [], []
    for a, (src, dst) in enumerate(zip(src_refs, dst_refs)):
        owns.append(pltpu.make_async_copy(src if gather else src.at[me], dst.at[me], local_sems.at[a]))
        for j in range(1, N_DEV):
            px = 1 - x if j & 4 else x
            py = 1 - y if j & 2 else y
            pc = 1 - c if j & 1 else c
            peer = 4 * px + 2 * py + pc
            sems = dict(send_sem=send_sems.at[7 * a + j - 1], recv_sem=recv_sems.at[7 * a + j - 1],
                        device_id=(px, py, pc), device_id_type=MESH)
            sends.append(pltpu.make_async_remote_copy(
                src_ref=src if gather else src.at[peer], dst_ref=dst.at[me], **sems))
            recvs.append(pltpu.make_async_remote_copy(
                src_ref=src if gather else src.at[me], dst_ref=dst.at[peer], **sems))

    def start():
        for cp in owns + sends:
            cp.start()

    def wait():
        for cp in recvs:
            cp.wait_recv()
        for cp in sends:
            cp.wait_send()
        for cp in owns:
            cp.wait()

    return start, wait


def _riding_exchange(src_refs, dst_refs, sems, gather):
    start, wait = _direct_exchange(src_refs, dst_refs, *sems, gather)
    ids = [pl.program_id(a) for a in range(2)]
    last = [pl.num_programs(a) - 1 for a in range(2)]
    pl.when(jnp.logical_and(ids[0] == 0, ids[1] == 0))(start)
    return lambda: pl.when(jnp.logical_and(ids[0] == last[0], ids[1] == last[1]))(wait)


def _exchange(srcs, gather, name):
    n = len(srcs)

    def body(*refs):
        start, wait = _direct_exchange(refs[:n], refs[n:2 * n], *refs[2 * n:], gather=gather)
        start()
        wait()

    return pl.pallas_call(
        body, name=name, out_shape=_exchange_out(srcs, gather),
        in_specs=[ANY] * n, out_specs=[ANY] * n, scratch_shapes=_exchange_sems(n),
    )(*srcs)


def _adamw(w, g, m, v):
    m = ADAM_B1 * m + (1.0 - ADAM_B1) * g
    v = ADAM_B2 * v + (1.0 - ADAM_B2) * (g * g)
    m_hat = m / (1.0 - ADAM_B1 ** ADAM_STEP)
    v_hat = v / (1.0 - ADAM_B2 ** ADAM_STEP)
    return -ADAM_LR * (m_hat / (jnp.sqrt(v_hat) + ADAM_EPS) + ADAM_WD * w), m, v


def _reduce_adamw(recv, w, m, v, name):
    _, k, n = w.shape
    tr = max(t for t in range(16, 257, 16) if k % t == 0)

    def body(r_ref, w_ref, m_ref, v_ref, g_out, d_out, m_out, v_out):
        g = r_ref[0].astype(F32)
        for s in range(1, N_DEV):
            g = g + r_ref[s].astype(F32)
        g_out[...] = g
        d_out[...], m_out[...], v_out[...] = _adamw(w_ref[...], g, m_ref[...], v_ref[...])

    spec = pl.BlockSpec((None, tr, n), lambda i: (0, i, 0))
    return pl.pallas_call(
        body, name=name, grid=(k // tr,),
        out_shape=[jax.ShapeDtypeStruct((1, k, n), F32)] * 4,
        in_specs=[pl.BlockSpec((N_DEV, tr, n), lambda i: (0, i, 0)), spec, spec, spec],
        out_specs=[spec] * 4, compiler_params=_params("arbitrary"),
    )(recv, w, m, v)


def _small_adamw(gathered, w, m, v):
    def body(g_ref, w_ref, m_ref, v_ref, g_out, d_out, m_out, v_out, loss_out):
        tot = g_ref[0]
        for s in range(1, N_DEV):
            tot = tot + g_ref[s]
        g = tot[0:8]
        g_out[...] = g
        d_out[...], m_out[...], v_out[...] = _adamw(w_ref[...], g, m_ref[...], v_ref[...])
        loss_out[...] = jnp.broadcast_to((0.5 / D) * jnp.sum(tot[8:9], axis=1, keepdims=True), (8, BLK))

    out = [jax.ShapeDtypeStruct((8, D), F32)] * 4 + [jax.ShapeDtypeStruct((8, BLK), F32)]
    return pl.pallas_call(body, name="small_adamw", out_shape=out, compiler_params=_params())(gathered, w, m, v)


def _pick_chunks(w, chunks):
    return jnp.concatenate([w[:, BLK * c:BLK * (c + 1)] for c in chunks], axis=1)


def _whole_weight(gathered, i):
    _, k, n = gathered.shape
    if BY_ROWS[i]:
        return gathered.reshape(N_DEV * k, n)
    return gathered.transpose(1, 0, 2).reshape(k, N_DEV * n)


def _shard_parts(grad, i):
    if BY_ROWS[i]:
        return grad.reshape(N_DEV, grad.shape[0] // N_DEV, grad.shape[1])
    k, n8 = grad.shape
    return grad.reshape(k, N_DEV, n8 // N_DEV).transpose(1, 0, 2)


def kernel(x, mem, g_pre_mix, g_post_mix, g_pre_ffn, g_post_ffn, g_mem, w_in, w_mem_kv, w_br_sb, w_br_dil, w_br_mem, w_gate, b_gate, w_o, w_ffn_in, w_ffn_out, loss_target, m_g_pre_mix, m_g_post_mix, m_g_pre_ffn, m_g_post_ffn, m_g_mem, m_w_in, m_w_mem_kv, m_w_br_sb, m_w_br_dil, m_w_br_mem, m_w_gate, m_b_gate, m_w_o, m_w_ffn_in, m_w_ffn_out, v_g_pre_mix, v_g_post_mix, v_g_pre_ffn, v_g_post_ffn, v_g_mem, v_w_in, v_w_mem_kv, v_w_br_sb, v_w_br_dil, v_w_br_mem, v_w_gate, v_b_gate, v_w_o, v_w_ffn_in, v_w_ffn_out):
    bsz, seq, _ = x.shape
    tokens = bsz * seq
    xf, tgt, memf = x.reshape(tokens, D), loss_target.reshape(tokens, D), mem.reshape(bsz * MEM_LEN, D)
    big_w = [w_in, w_mem_kv, w_br_sb, w_br_dil, w_br_mem, w_gate, w_o, w_ffn_in, w_ffn_out]
    big_m = [m_w_in, m_w_mem_kv, m_w_br_sb, m_w_br_dil, m_w_br_mem, m_w_gate, m_w_o, m_w_ffn_in, m_w_ffn_out]
    big_v = [v_w_in, v_w_mem_kv, v_w_br_sb, v_w_br_dil, v_w_br_mem, v_w_gate, v_w_o, v_w_ffn_in, v_w_ffn_out]

    shards = [w[0].astype(BF16) for w in big_w]
    k_in, n_in = shards[0].shape
    fw_in = _whole_weight(_all_gather(shards[0], "weight_all_gather").reshape(N_DEV, k_in, n_in), 0)
    w_a, w_b = _pick_chunks(fw_in, CHUNKS_A), _pick_chunks(fw_in, CHUNKS_B)

    h = _norm_fwd(xf, g_pre_mix, "pre_mix_norm")
    proj_a = _matmul(h, w_a, "nn", BF16, "proj_a").reshape(bsz, seq, WA)
    proj_b = _matmul(h, w_b, "nn", BF16, "proj_b").reshape(bsz, seq, WB)
    o_a, o_a32, *behind = _sb_fwd(proj_a, seq, [shards[i] for i in GATHER_BEHIND])
    fw_mem_kv, fw_br_sb, fw_br_dil, fw_br_mem, fw_gate, fw_o, fw_ffn_in, fw_ffn_out = (
        _whole_weight(g, i) for g, i in zip(behind, GATHER_BEHIND))
    gpre = _matmul(h, fw_gate, "nn", BF16, "gate_proj")
    cos_t, sin_t = _rope_tables(seq)
    o_b, lse_b = _dil_fwd(proj_b, cos_t, sin_t, seq)
    mn = _norm_fwd(memf, g_mem, "mem_norm")
    kv = _matmul(mn, fw_mem_kv, "nn", BF16, "mem_kv_proj").reshape(bsz, MEM_LEN, D)
    o_c = _mem_fwd(proj_a, kv, seq)
    o_a2, o_b2, o_c2 = o_a.reshape(tokens, 512), o_b.reshape(tokens, 256), o_c.reshape(tokens, 512)
    ys = [_matmul(o_a2, fw_br_sb, "nn", BF16, "branch_sb"), _matmul(o_b2, fw_br_dil, "nn", BF16, "branch_dil"),
          _matmul(o_c2, fw_br_mem, "nn", BF16, "branch_mem")]
    merged = _gate_merge(gpre, ys, b_gate)
    mix, x1, h2 = _out_proj_norm(merged, fw_o, xf, g_post_mix, g_pre_ffn)
    gu = _matmul(h2, fw_ffn_in, "nn", BF16, "ffn_in")
    f = _swiglu_fwd(gu)
    dy, dfo, dg_post_ffn, loss_lanes = _ffn_out_loss(f, fw_ffn_out, x1, tgt, g_post_ffn)

    gw_ffn_out = _matmul(f, dfo, "tn", BF16, "gw_ffn_out")
    dgu = _d_ffn_swiglu_bwd(dfo, fw_ffn_out, gu)
    gw_ffn_in = _matmul(h2, dgu, "tn", BF16, "gw_ffn_in")
    dx1, dmix, dg_pre_ffn, dg_post_mix = _d_h2_norm_bwd(dgu, fw_ffn_in, x1, dy, mix, g_pre_ffn, g_post_mix)
    dmerged = _matmul(dmix, fw_o, "nt", BF16, "d_merged")
    gw_o = _matmul(merged, dmix, "tn", BF16, "gw_o")
    dya, dyb, dyc, dgpre, db_gate = _gate_bwd(dmerged, gpre, ys, b_gate)
    d_oa = _matmul(dya, fw_br_sb, "nt", BF16, "d_o_sb").reshape(bsz, seq, 512)
    d_ob = _matmul(dyb, fw_br_dil, "nt", BF16, "d_o_dil").reshape(bsz, seq, 256)
    d_oc = _matmul(dyc, fw_br_mem, "nt", BF16, "d_o_mem").reshape(bsz, seq, 512)
    gw_br_sb = _matmul(o_a2, dya, "tn", BF16, "gw_br_sb")
    gw_br_dil = _matmul(o_b2, dyb, "tn", BF16, "gw_br_dil")
    gw_br_mem = _matmul(o_c2, dyc, "tn", BF16, "gw_br_mem")
    gw_gate = _matmul(h, dgpre, "tn", BF16, "gw_gate")
    grads = {2: gw_br_sb, 3: gw_br_dil, 4: gw_br_mem, 5: gw_gate, 6: gw_o, 7: gw_ffn_in, 8: gw_ffn_out}
    d_proj_a, *recv_behind = _sb_bwd(proj_a, d_oa, o_a32, seq, [_shard_parts(grads[i], i) for i in REDUCE_BEHIND])
    d_proj_a, dk_m, dv_m = _mem_bwd(proj_a, kv, d_oc, d_proj_a, seq)
    d_proj_b = _dil_bwd(proj_b, cos_t, sin_t, d_ob, o_b, lse_b, seq).reshape(tokens, WB)
    d_proj_a = d_proj_a.reshape(tokens, WA)
    gw_a = _matmul(h, d_proj_a, "tn", BF16, "gw_in_a")
    gw_b = _matmul(h, d_proj_b, "tn", BF16, "gw_in_b")
    dkv = jnp.concatenate([dk_m, dv_m], axis=-1).reshape(bsz * MEM_LEN, D)
    gw_mem_kv = _matmul(mn, dkv, "tn", BF16, "gw_mem_kv")
    dmn = _matmul(dkv, fw_mem_kv, "nt", F32, "d_mem_norm")
    dg_mem = _gain_grad(dmn, memf)
    gw_ab = jnp.concatenate([gw_a, gw_b], axis=1)
    where = {c: i for i, c in enumerate(CHUNKS_A + CHUNKS_B)}
    grads = {0: _pick_chunks(gw_ab, [where[c] for c in range(34)]), 1: gw_mem_kv}
    dx, dg_pre_mix, *recv_last = _d_h_norm_bwd(
        [(dgpre, fw_gate), (d_proj_a, w_a), (d_proj_b, w_b)], xf, dx1, g_pre_mix,
        [_shard_parts(grads[i], i) for i in REDUCE_LAST])

    received = dict(zip(REDUCE_BEHIND + REDUCE_LAST, [*recv_behind, *recv_last]))
    adam = [_reduce_adamw(received[i], big_w[i], big_m[i], big_v[i], "reduce_adamw_" + BIG_NAMES[i])
            for i in range(len(big_w))]
    big = [[a[k] for a in adam] for k in range(4)]

    small = jnp.concatenate([dg_pre_mix, dg_post_mix, dg_pre_ffn, dg_post_ffn, dg_mem, db_gate.reshape(3, D),
                             loss_lanes, jnp.zeros((7, D), F32)], axis=0)
    small_all, = _exchange([small], True, "small_all_gather")

    def small_pack(gs, b):
        return jnp.concatenate([*gs, b.reshape(3, D)], axis=0)

    sm = _small_adamw(
        small_all, small_pack([g_pre_mix, g_post_mix, g_pre_ffn, g_post_ffn, g_mem], b_gate),
        small_pack([m_g_pre_mix, m_g_post_mix, m_g_pre_ffn, m_g_post_ffn, m_g_mem], m_b_gate),
        small_pack([v_g_pre_mix, v_g_post_mix, v_g_pre_ffn, v_g_post_ffn, v_g_mem], v_b_gate))
    loss = sm[4][0, 0]

    def leaves(k):
        t, bw = sm[k], big[k]
        return [t[0:1], t[1:2], t[2:3], t[3:4], t[4:5], *bw[0:6], t[5:8].reshape(1, 3 * D), *bw[6:9]]

    return (loss, dx.reshape(bsz, seq, D), *leaves(0), *leaves(1), *leaves(2), *leaves(3))
```

```python
import functools

import jax
import jax.numpy as jnp
from jax import lax
from jax.experimental import pallas as pl
from jax.experimental.pallas import tpu as pltpu

F32 = jnp.float32
BF16 = jnp.bfloat16
D = 1024
BLK = 128
MEM_LEN = 256
D_FF = 2816
NORM_EPS = 1e-6
NEG_INF = -1e30
ROPE_THETA = 10000.0
ADAM_LR, ADAM_B1, ADAM_B2, ADAM_EPS, ADAM_WD, ADAM_STEP = 0.001, 0.9, 0.999, 1e-08, 0.01, 10
N_DEV = 8
VMEM_LIMIT_BYTES = 56 * 1024 * 1024
MESH = pl.DeviceIdType.MESH
ANY = pl.BlockSpec(memory_space=pl.ANY)

NT = (((1,), (1,)), ((), ()))
TN = (((0,), (0,)), ((), ()))
NN = (((1,), (0,)), ((), ()))
_DIMS = {"nn": NN, "nt": NT, "tn": TN}

BIG_NAMES = ("w_in", "w_mem_kv", "w_br_sb", "w_br_dil", "w_br_mem", "w_gate", "w_o", "w_ffn_in", "w_ffn_out")
BY_ROWS = (False, True, False, False, False, False, True, False, True)
GATHER_FIRST = (0,)
GATHER_BEHIND = (1, 2, 3, 4, 5, 6, 7, 8)
REDUCE_BEHIND = (2, 3, 4, 5, 6, 7, 8)
REDUCE_LAST = (0, 1)

CHUNKS_A = tuple(c for hp in range(4) for c in (hp, 4 + hp, 8 + hp)) + (30, 31, 32, 33)
CHUNKS_B = tuple(c for hp in range(2) for g in range(3) for c in (12 + 6 * g + hp, 14 + 6 * g + hp, 16 + 6 * g + hp))
WA, WB = 128 * len(CHUNKS_A), 128 * len(CHUNKS_B)
DIL_GROUPS = (1, 4, 16)


def _params(*sem):
    return pltpu.CompilerParams(dimension_semantics=sem or None, vmem_limit_bytes=VMEM_LIMIT_BYTES)


def _tile(n, cap):
    if n <= 128:
        return n
    assert n % 128 == 0, n
    best = 128
    for t in range(128, min(n, cap) + 1, 128):
        if n % t == 0:
            best = t
    return best


def _k_steps(k, nk, step):
    if nk == 1:
        step(True, True)
        return
    pl.when(k == 0)(functools.partial(step, True, False))
    if nk > 2:
        pl.when(jnp.logical_and(k > 0, k < nk - 1))(functools.partial(step, False, False))
    pl.when(k == nk - 1)(functools.partial(step, False, True))


def _matmul(a, b, mode, out_dtype, name, tm_cap=1536, tn_cap=1536, tk_cap=1536):
    if mode == "tn":
        (K, M), N = a.shape, b.shape[1]
    elif mode == "nt":
        (M, K), N = a.shape, b.shape[0]
    else:
        (M, K), N = a.shape, b.shape[1]
    tm, tn, tk = _tile(M, tm_cap), _tile(N, tn_cap), _tile(K, tk_cap)
    nm, nn, nk = M // tm, N // tn, K // tk
    dims = _DIMS[mode]

    def body(a_ref, b_ref, o_ref, *acc):
        def step(first, last):
            d = lax.dot_general(a_ref[...], b_ref[...], dims, preferred_element_type=F32)
            if not first:
                d = d + acc[0][...]
            if last:
                o_ref[...] = d.astype(o_ref.dtype)
            else:
                acc[0][...] = d

        _k_steps(pl.program_id(2), nk, step)

    n_outer = nk == 1 and (a.size * nn + b.size) < (a.size + b.size * nm)
    if n_outer:
        grid, ij = (nn, nm, nk), (lambda g0, g1: (g1, g0))
    else:
        grid, ij = (nm, nn, nk), (lambda g0, g1: (g0, g1))
    if mode == "tn":
        a_spec = pl.BlockSpec((tk, tm), lambda g0, g1, k: (k, ij(g0, g1)[0]))
    else:
        a_spec = pl.BlockSpec((tm, tk), lambda g0, g1, k: (ij(g0, g1)[0], k))
    if mode == "nt":
        b_spec = pl.BlockSpec((tn, tk), lambda g0, g1, k: (ij(g0, g1)[1], k))
    else:
        b_spec = pl.BlockSpec((tk, tn), lambda g0, g1, k: (k, ij(g0, g1)[1]))
    return pl.pallas_call(
        body, name=name, grid=grid,
        out_shape=jax.ShapeDtypeStruct((M, N), out_dtype),
        in_specs=[a_spec, b_spec],
        out_specs=pl.BlockSpec((tm, tn), lambda g0, g1, k: ij(g0, g1)),
        scratch_shapes=[pltpu.VMEM((tm, tn), F32)] if nk > 1 else [],
        compiler_params=_params("parallel", "parallel", "arbitrary"),
    )(a, b)


def _rowwise(body, name, rows, tr, row_ins, vec_ins, row_outs, acc_outs=()):
    tr = min(tr, rows)
    assert rows % tr == 0
    in_specs, args = [], []
    for r in row_ins:
        arr, w, cb = r if isinstance(r, tuple) else (r, r.shape[1], 0)
        in_specs.append(pl.BlockSpec((tr, w), functools.partial(lambda i, cb: (i, cb), cb=cb)))
        args.append(arr)
    for v in vec_ins:
        in_specs.append(pl.BlockSpec(v.shape, lambda i: (0, 0)))
        args.append(v)
    out_shape = [jax.ShapeDtypeStruct((rows, w), dt) for w, dt in row_outs]
    out_shape += [jax.ShapeDtypeStruct((1, w), F32) for w in acc_outs]
    out_specs = [pl.BlockSpec((tr, w), lambda i: (i, 0)) for w, _ in row_outs]
    out_specs += [pl.BlockSpec((1, w), lambda i: (0, 0)) for w in acc_outs]
    n_acc = len(acc_outs)

    def wrapped(*refs):
        if n_acc:
            @pl.when(pl.program_id(0) == 0)
            def _():
                for r in refs[len(refs) - n_acc:]:
                    r[...] = jnp.zeros_like(r)
        body(*refs)

    return pl.pallas_call(
        wrapped, name=name, grid=(rows // tr,), out_shape=out_shape, in_specs=in_specs, out_specs=out_specs,
        compiler_params=_params("arbitrary"),
    )(*args)


def _rstd(x):
    return lax.rsqrt(jnp.mean(x * x, axis=-1, keepdims=True) + NORM_EPS)


def _norm_bwd(u, n, r):
    return r * (u - n * jnp.mean(u * n, axis=-1, keepdims=True))


def _colsum(v):
    return jnp.sum(v, axis=0, keepdims=True)


def _norm_fwd(x, g, name):
    def body(x_ref, g_ref, h_ref):
        xv = x_ref[...]
        h_ref[...] = ((xv * _rstd(xv)) * g_ref[...]).astype(BF16)

    return _rowwise(body, name, x.shape[0], 512, [x], [g], [(D, BF16)])[0]


def _matmul_rows(pairs, mode, name, epilogue, row_ins=(), vec_ins=(), row_outs=(), acc_outs=(), ride=None,
                 tm=512, tk_cap=1536):
    M = pairs[0][0].shape[0]
    N = pairs[0][1].shape[1] if mode == "nn" else pairs[0][1].shape[0]
    tm = min(tm, M)
    tks = [_tile(a.shape[1], tk_cap) for a, _ in pairs]
    nks = [a.shape[1] // tk for (a, _), tk in zip(pairs, tks)]
    offs = [sum(nks[:p]) for p in range(len(pairs))]
    nm, nk = M // tm, sum(nks)
    dims = _DIMS[mode]
    n_ab, n_extra, n_out = 2 * len(pairs), len(row_ins) + len(vec_ins), len(row_outs) + len(acc_outs)
    n_ride = 0 if ride is None else len(ride)

    def body(*refs):
        ab, extra, rest = refs[:n_ab], refs[n_ab:n_ab + n_extra], refs[n_ab + n_extra:]
        ride_refs, outs, rest = rest[:n_ride], rest[n_ride:n_ride + n_out], rest[n_ride + n_out:]
        received_refs, rest = rest[:n_ride], rest[n_ride:]
        if n_ride:
            finish_ride = _riding_exchange(ride_refs, received_refs, rest[len(rest) - 3:], gather=False)
        i, k = pl.program_id(0), pl.program_id(1)
        if acc_outs:
            @pl.when(jnp.logical_and(i == 0, k == 0))
            def _():
                for r in outs[len(row_outs):]:
                    r[...] = jnp.zeros_like(r)

        def step(p, first, last):
            d = lax.dot_general(ab[2 * p][...], ab[2 * p + 1][...], dims, preferred_element_type=F32)
            if not first:
                d = d + rest[0][...]
            if last:
                epilogue(d, *extra, *outs)
            else:
                rest[0][...] = d

        last_p = len(pairs) - 1
        if nk == 1:
            step(0, True, True)
        else:
            pl.when(k == 0)(functools.partial(step, 0, True, False))
            for p in range(len(pairs)):
                lo, hi = max(offs[p], 1), min(offs[p] + nks[p], nk - 1)
                if hi > lo:
                    pl.when(jnp.logical_and(k >= lo, k < hi))(functools.partial(step, p, False, False))
            pl.when(k == nk - 1)(functools.partial(step, last_p, False, True))
        if n_ride:
            finish_ride()

    in_specs, args = [], []
    for p, ((a, b), tk) in enumerate(zip(pairs, tks)):
        step = functools.partial(lambda k, p: jnp.clip(k - offs[p], 0, nks[p] - 1), p=p)
        in_specs.append(pl.BlockSpec((tm, tk), functools.partial(lambda i, k, step: (i, step(k)), step=step)))
        if mode == "nn":
            in_specs.append(pl.BlockSpec((tk, N), functools.partial(lambda i, k, step: (step(k), 0), step=step)))
        else:
            in_specs.append(pl.BlockSpec((N, tk), functools.partial(lambda i, k, step: (0, step(k)), step=step)))
        args += [a, b]
    in_specs += [pl.BlockSpec((tm, r.shape[1]), lambda i, k: (i, 0)) for r in row_ins]
    in_specs += [pl.BlockSpec(v.shape, lambda i, k: (0, 0)) for v in vec_ins]
    in_specs += [ANY] * n_ride
    out_shape = [jax.ShapeDtypeStruct((M, w), dt) for w, dt in row_outs]
    out_shape += [jax.ShapeDtypeStruct((1, w), F32) for w in acc_outs]
    out_specs = [pl.BlockSpec((tm, w), lambda i, k: (i, 0)) for w, _ in row_outs]
    out_specs += [pl.BlockSpec((1, w), lambda i, k: (0, 0)) for w in acc_outs]
    scratch = [pltpu.VMEM((tm, N), F32)] if nk > 1 else []
    if n_ride:
        out_shape += _exchange_out(ride, False)
        out_specs += [ANY] * n_ride
        scratch += _exchange_sems(n_ride)
    return pl.pallas_call(
        body, name=name, grid=(nm, nk), out_shape=out_shape, in_specs=in_specs, out_specs=out_specs,
        scratch_shapes=scratch, compiler_params=_params("arbitrary", "arbitrary"),
    )(*args, *row_ins, *vec_ins, *(ride or []))


def _out_proj_norm(merged, w_o, x, g_post, g_pre):
    def epilogue(mv, x_ref, g2_ref, g3_ref, mix_ref, x1_ref, h2_ref):
        mix_ref[...] = mv
        x1 = x_ref[...] + (mv * _rstd(mv)) * g2_ref[...]
        x1_ref[...] = x1
        h2_ref[...] = ((x1 * _rstd(x1)) * g3_ref[...]).astype(BF16)

    return _matmul_rows([(merged, w_o)], "nn", "out_proj_norm", epilogue, [x], [g_post, g_pre],
                        [(D, F32), (D, F32), (D, BF16)])


def _gate_merge(gpre, ys, b_gate):
    def body(gp_ref, ya_ref, yb_ref, yc_ref, b_ref, m_ref):
        acc = None
        for k, y_ref in enumerate((ya_ref, yb_ref, yc_ref)):
            cols = slice(k * D, (k + 1) * D)
            gate = jax.nn.sigmoid(gp_ref[:, cols].astype(F32) + b_ref[:, cols])
            term = gate * y_ref[...].astype(F32)
            acc = term if acc is None else acc + term
        m_ref[...] = acc.astype(BF16)

    return _rowwise(body, "gate_merge", gpre.shape[0], 256, [gpre, *ys], [b_gate], [(D, BF16)])[0]


def _ffn_in_swiglu(h2, w_ffn_in):
    tokens = h2.shape[0]
    tm, tn = min(512, tokens), _tile(D_FF, 1536)
    nj = D_FF // tn

    def body(h_ref, wa_ref, wb_ref, a_ref, b_ref, f_ref):
        hv = h_ref[...]
        a = jnp.dot(hv, wa_ref[...], preferred_element_type=F32)
        b = jnp.dot(hv, wb_ref[...], preferred_element_type=F32)
        a_ref[...] = a.astype(BF16)
        b_ref[...] = b.astype(BF16)
        f_ref[...] = (a * jax.nn.sigmoid(a) * b).astype(BF16)

    out = jax.ShapeDtypeStruct((tokens, D_FF), BF16)
    o_spec = pl.BlockSpec((tm, tn), lambda j, i: (i, j))
    return pl.pallas_call(
        body, name="ffn_in_swiglu", grid=(nj, tokens // tm), out_shape=(out, out, out),
        in_specs=[pl.BlockSpec((tm, D), lambda j, i: (i, 0)), pl.BlockSpec((D, tn), lambda j, i: (0, j)),
                  pl.BlockSpec((D, tn), lambda j, i: (0, j + nj))],
        out_specs=(o_spec, o_spec, o_spec), compiler_params=_params("parallel", "parallel"),
    )(h2, w_ffn_in, w_ffn_in)


def _ffn_out_loss(f, w_ffn_out, x1, tgt, g_post):
    def epilogue(fo_v, x1_ref, t_ref, g_ref, dy_ref, dfo_ref, dg_ref, loss_ref):
        r = _rstd(fo_v)
        n = fo_v * r
        err = (x1_ref[...] + n * g_ref[...]) - t_ref[...]
        loss_ref[...] += _colsum(err * err)
        dy = err * (1.0 / D)
        dy_ref[...] = dy
        dg_ref[...] += _colsum(dy * n)
        dfo_ref[...] = _norm_bwd(dy * g_ref[...], n, r).astype(BF16)

    return _matmul_rows([(f, w_ffn_out)], "nn", "ffn_out_loss", epilogue, [x1, tgt], [g_post],
                        [(D, F32), (D, BF16)], (D, D))


def _d_ffn_swiglu_bwd(dfo, w_ffn_out, gu_a, gu_b):
    def epilogue(d, a_ref, b_ref, dgu_ref):
        a = a_ref[...].astype(F32)
        b = b_ref[...].astype(F32)
        s = jax.nn.sigmoid(a)
        dgu_ref[:, :D_FF] = (d * b * (s * (1.0 + a * (1.0 - s)))).astype(BF16)
        dgu_ref[:, D_FF:] = (d * (a * s)).astype(BF16)

    return _matmul_rows([(dfo, w_ffn_out)], "nt", "d_ffn_swiglu_bwd", epilogue, [gu_a, gu_b], [],
                        [(2 * D_FF, BF16)], tm=256)[0]


def _d_h2_norm_bwd(dgu, w_ffn_in, x1, dy, mix, g_pre, g_post):
    def epilogue(dh, x1_ref, dy_ref, mix_ref, g3_ref, g2_ref, dx1_ref, dmix_ref, dg3_ref, dg2_ref):
        x1v = x1_ref[...]
        r3 = _rstd(x1v)
        n3 = x1v * r3
        dg3_ref[...] += _colsum(dh * n3)
        dx1 = dy_ref[...] + _norm_bwd(dh * g3_ref[...], n3, r3)
        dx1_ref[...] = dx1
        mv = mix_ref[...]
        r2 = _rstd(mv)
        n2 = mv * r2
        dg2_ref[...] += _colsum(dx1 * n2)
        dmix_ref[...] = _norm_bwd(dx1 * g2_ref[...], n2, r2).astype(BF16)

    return _matmul_rows([(dgu, w_ffn_in)], "nt", "d_h2_norm_bwd", epilogue, [x1, dy, mix], [g_pre, g_post],
                        [(D, F32), (D, BF16)], (D, D))


def _d_merged_gate_bwd(dmix, w_o, gpre, ys, b_gate):
    def epilogue(dm, gp_ref, ya_ref, yb_ref, yc_ref, b_ref, dya_ref, dyb_ref, dyc_ref, dgp_ref, db_ref):
        for k, (y_ref, dy_ref) in enumerate(((ya_ref, dya_ref), (yb_ref, dyb_ref), (yc_ref, dyc_ref))):
            cols = slice(k * D, (k + 1) * D)
            gate = jax.nn.sigmoid(gp_ref[:, cols].astype(F32) + b_ref[:, cols])
            dy_ref[...] = (dm * gate).astype(BF16)
            dgp = (dm * y_ref[...].astype(F32)) * (gate * (1.0 - gate))
            dgp_ref[:, cols] = dgp.astype(BF16)
            db_ref[:, cols] += _colsum(dgp)

    return _matmul_rows([(dmix, w_o)], "nt", "d_merged_gate_bwd", epilogue, [gpre, *ys], [b_gate],
                        [(D, BF16), (D, BF16), (D, BF16), (3 * D, BF16)], (3 * D,))


def _d_h_norm_bwd(pairs, x, dx1, g_pre, ride):
    def epilogue(dh, x_ref, dx1_ref, g_ref, dx_ref, dg_ref):
        xv = x_ref[...]
        r = _rstd(xv)
        n = xv * r
        dg_ref[...] += _colsum(dh * n)
        dx_ref[...] = dx1_ref[...] + _norm_bwd(dh * g_ref[...], n, r)

    return _matmul_rows(pairs, "nt", "d_h_norm_bwd", epilogue, [x, dx1], [g_pre], [(D, F32)], (D,), ride=ride)


def _gain_grad(dmn, mem):
    def body(d_ref, m_ref, dg_ref):
        mv = m_ref[...]
        dg_ref[...] += _colsum(d_ref[...] * (mv * _rstd(mv)))

    return _rowwise(body, "mem_gain_grad", mem.shape[0], 256, [dmn, mem], [], [], (D,))[0]


def _head_rowsum(v, head0):
    return (jnp.sum(jnp.where(head0, v, 0.0), axis=1, keepdims=True),
            jnp.sum(jnp.where(head0, 0.0, v), axis=1, keepdims=True))


KT = 256
SB_SCALE = 0.125


def _make_suffix(inclusive):
    row, col = lax.broadcasted_iota(jnp.int32, (KT, KT), 0), lax.broadcasted_iota(jnp.int32, (KT, KT), 1)
    tri = (row >= col if inclusive else row > col).astype(BF16)
    tri2 = jnp.concatenate([tri, tri], axis=0)

    def suffix(x):
        hi = x.astype(BF16)
        lo = (x - hi.astype(F32)).astype(BF16)
        return jnp.dot(jnp.concatenate([hi, lo], axis=1), tri2, preferred_element_type=F32)

    return suffix


def _sb_scores(qh, k, mask, suffix_incl, later):
    z = lax.dot_general(qh, k, NT, preferred_element_type=F32)
    zc = jnp.minimum(z, 60.0)
    sp = jnp.log(1.0 + jnp.exp(zc))
    if mask is not None:
        sp = jnp.where(mask, sp, 0.0)
    a = jnp.exp((zc - suffix_incl(sp)) - later)
    if mask is not None:
        a = jnp.where(mask, a, 0.0)
    return zc, sp, a


QB = KT


def _sb_tiles(i, tile, init):
    st = tile(i, init, True)
    st = lax.fori_loop(0, lax.shift_right_logical(i, 1),
                       lambda t, s: tile(i - 2 - 2 * t, tile(i - 1 - 2 * t, s, False), False), st)
    return lax.cond((i & 1) == 1, lambda s: tile(0, s, False), lambda s: s, st)


def _sb_consts():
    head0 = lax.broadcasted_iota(jnp.int32, (QB, BLK), 1) < 64
    row = lax.broadcasted_iota(jnp.int32, (2 * QB, KT), 0) & (QB - 1)
    return head0, row > lax.broadcasted_iota(jnp.int32, (2 * QB, KT), 1)


def _stack_heads(v, head0):
    zero = jnp.zeros_like(v)
    return jnp.concatenate([jnp.where(head0, v, zero), jnp.where(head0, zero, v)], axis=0)


def _unstack_heads(v, head0):
    n = v.shape[0] // 2
    return jnp.where(head0, v[:n], v[n:])


def _sb_fwd(proj_a, seq, ride):
    bsz = proj_a.shape[0]
    n_ride = len(ride)

    def body(x_ref, *rest):
        ride_refs, (o_ref, o32_ref), rest = rest[:n_ride], rest[n_ride:n_ride + 2], rest[n_ride + 2:]
        gathered_refs, acc_ref, sems = rest[:n_ride], rest[n_ride], rest[n_ride + 1:]
        finish_ride = _riding_exchange(ride_refs, gathered_refs, sems, gather=True)
        head0, diag_mask = _sb_consts()
        suffix_incl = _make_suffix(True)

        def qblock(i, carry):
            r0 = pl.multiple_of(i * QB, QB)
            qs = _stack_heads(x_ref[pl.ds(r0, QB), 0:128] * jnp.asarray(SB_SCALE, BF16), head0)

            def tile(jt, later, masked):
                c0 = pl.multiple_of(jt * KT, KT)
                k = x_ref[pl.ds(c0, KT), 128:256]
                v = x_ref[pl.ds(c0, KT), 256:384]
                _, sp, a = _sb_scores(qs, k, diag_mask if masked else None, suffix_incl, later)
                pv = jnp.dot(a.astype(BF16), v, preferred_element_type=F32)
                if masked:
                    acc_ref[...] = pv
                else:
                    acc_ref[...] += pv
                return later + jnp.sum(sp, axis=1, keepdims=True)

            _sb_tiles(i, tile, jnp.zeros((2 * QB, 1), F32))
            o = _unstack_heads(acc_ref[...], head0)
            o32_ref[pl.ds(r0, QB), :] = o
            o_ref[pl.ds(r0, QB), :] = o.astype(BF16)
            return carry

        lax.fori_loop(0, seq // QB, qblock, 0)
        finish_ride()

    out_spec = pl.BlockSpec((None, seq, BLK), lambda b, hp: (b, 0, hp))
    return pl.pallas_call(
        body, name="sb_attn_fwd", grid=(bsz, 4),
        out_shape=[jax.ShapeDtypeStruct((bsz, seq, 512), BF16), jax.ShapeDtypeStruct((bsz, seq, 512), F32),
                   *_exchange_out(ride, True)],
        in_specs=[pl.BlockSpec((None, seq, 384), lambda b, hp: (b, 0, hp))] + [ANY] * n_ride,
        out_specs=[out_spec, out_spec] + [ANY] * n_ride,
        scratch_shapes=[pltpu.VMEM((2 * QB, BLK), F32), *_exchange_sems(n_ride)],
        compiler_params=_params("arbitrary", "arbitrary"),
    )(proj_a, *ride)


def _sb_bwd(proj_a, d_o, o_a, seq, ride):
    bsz = proj_a.shape[0]
    n_ride = len(ride)

    def body(x_ref, do_ref, o_ref, *rest):
        ride_refs, d_ref, rest = rest[:n_ride], rest[n_ride], rest[n_ride + 1:]
        received_refs, (dq_acc, dk_acc, dv_acc), sems = rest[:n_ride], rest[n_ride:n_ride + 3], rest[n_ride + 3:]
        finish_ride = _riding_exchange(ride_refs, received_refs, sems, gather=False)
        head0, diag_mask = _sb_consts()
        suffix_incl, suffix_excl = _make_suffix(True), _make_suffix(False)
        dk_acc[...] = jnp.zeros_like(dk_acc)
        dv_acc[...] = jnp.zeros_like(dv_acc)

        def qblock(i, carry):
            r0 = pl.multiple_of(i * QB, QB)
            qs = _stack_heads(x_ref[pl.ds(r0, QB), 0:128] * jnp.asarray(SB_SCALE, BF16), head0)
            do = do_ref[pl.ds(r0, QB), :]
            dos = _stack_heads(do, head0)
            dsum = jnp.concatenate(_head_rowsum(do.astype(F32) * o_ref[pl.ds(r0, QB), :], head0), axis=0)

            def tile(jt, st, masked):
                later, rest_g = st
                c0 = pl.multiple_of(jt * KT, KT)
                k = x_ref[pl.ds(c0, KT), 128:256]
                v = x_ref[pl.ds(c0, KT), 256:384]
                zc, sp, a = _sb_scores(qs, k, diag_mask if masked else None, suffix_incl, later)
                a16 = a.astype(BF16)
                g = a16.astype(F32) * lax.dot_general(dos, v, NT, preferred_element_type=F32)
                dz = g - jnp.exp(zc - sp) * (rest_g - suffix_excl(g))
                if masked:
                    dz = jnp.where(diag_mask, dz, 0.0)
                dz = dz.astype(BF16)
                dq = jnp.dot(dz, k, preferred_element_type=F32)
                if masked:
                    dq_acc[...] = dq
                else:
                    dq_acc[...] += dq
                dk_acc[pl.ds(c0, KT), :] += lax.dot_general(dz, qs, TN, preferred_element_type=F32)
                dv_acc[pl.ds(c0, KT), :] += lax.dot_general(a16, dos, TN, preferred_element_type=F32)
                return later + jnp.sum(sp, axis=1, keepdims=True), rest_g - jnp.sum(g, axis=1, keepdims=True)

            _sb_tiles(i, tile, (jnp.zeros((2 * QB, 1), F32), dsum))
            d_ref[pl.ds(r0, QB), 0:128] = (_unstack_heads(dq_acc[...], head0) * SB_SCALE).astype(BF16)
            return carry

        lax.fori_loop(0, seq // QB, qblock, 0)
        d_ref[:, 128:256] = dk_acc[...].astype(BF16)
        d_ref[:, 256:384] = dv_acc[...].astype(BF16)
        finish_ride()

    return pl.pallas_call(
        body, name="sb_attn_bwd", grid=(bsz, 4),
        out_shape=[jax.ShapeDtypeStruct((bsz, seq, WA), BF16), *_exchange_out(ride, False)],
        in_specs=[pl.BlockSpec((None, seq, 384), lambda b, hp: (b, 0, hp)),
                  pl.BlockSpec((None, seq, BLK), lambda b, hp: (b, 0, hp)),
                  pl.BlockSpec((None, seq, BLK), lambda b, hp: (b, 0, hp))] + [ANY] * n_ride,
        out_specs=[pl.BlockSpec((None, seq, 384), lambda b, hp: (b, 0, hp))] + [ANY] * n_ride,
        scratch_shapes=[pltpu.VMEM((2 * QB, BLK), F32), pltpu.VMEM((seq, BLK), F32), pltpu.VMEM((seq, BLK), F32),
                        *_exchange_sems(n_ride)],
        compiler_params=_params("arbitrary", "arbitrary"),
    )(proj_a, d_o, o_a, *ride)


def _rope_tables(seq):
    inv_freq = ROPE_THETA ** (-jnp.arange(32, dtype=F32) * 2.0 / 64)
    ang = jnp.arange(seq).astype(F32)[:, None] * inv_freq[None, :]
    cos, sin = jnp.cos(ang), jnp.sin(ang)
    return jnp.tile(cos, (1, 4)), jnp.concatenate([-sin, sin, -sin, sin], axis=1)


def _make_rope(n_rows):
    lane = lax.broadcasted_iota(jnp.int32, (n_rows, BLK), 1)
    first = (lane & 63) < 32

    def rope(x, cos, sin):
        partner = jnp.where(first, pltpu.roll(x, 96, 1), pltpu.roll(x, 32, 1))
        return x * cos + partner * sin

    return rope


DIL_UNROLL = 4


def _dil_consts():
    head0 = lax.broadcasted_iota(jnp.int32, (BLK, BLK), 1) < 64
    row = lax.broadcasted_iota(jnp.int32, (2 * BLK, 2 * BLK), 0) & (BLK - 1)
    col = lax.broadcasted_iota(jnp.int32, (2 * BLK, 2 * BLK), 1)
    valid_prev = jnp.logical_and(col < BLK, col >= row)
    valid_cur = jnp.logical_and(col >= BLK, row >= col - BLK)
    return head0, valid_prev, valid_cur


def _dil_blocks(dil, seq, block):
    nq = seq // dil // BLK

    def rows(r, i):
        if dil == 1:
            return pl.ds(pl.multiple_of(i * BLK, BLK), BLK)
        return pl.ds(r + (dil * BLK) * i, BLK, stride=dil)

    def step(t, carry):
        for u in range(DIL_UNROLL):
            n = t * DIL_UNROLL + u
            r, i = lax.div(n, nq), lax.rem(n, nq)
            block(rows(r, i), rows(r, jnp.maximum(i - 1, 0)), i)
        return carry

    lax.fori_loop(0, seq // BLK // DIL_UNROLL, step, 0)


def _dil_scores(qf, kf, vf, cur, prev, i, consts):
    head0, valid_prev, valid_cur = consts
    qs = _stack_heads(qf[cur, :].astype(BF16), head0)
    kcat = jnp.concatenate([kf[prev, :], kf[cur, :]], axis=0).astype(BF16)
    vcat = jnp.concatenate([vf[prev, :], vf[cur, :]], axis=0).astype(BF16)
    valid = jnp.logical_or(valid_cur, jnp.logical_and(valid_prev, i > 0))
    s = lax.dot_general(qs, kcat, NT, preferred_element_type=F32) * 0.125
    return qs, kcat, vcat, s, valid


def _head_cols(v):
    return jnp.concatenate([v[:, 0:1], v[:, 64:65]], axis=0)


def _dil_load_qkv(x_ref, c, rope, cos, sin, qf, kf, vf):
    qf[...] = rope(x_ref[:, c:c + 128].astype(F32), cos, sin).astype(BF16).astype(F32)
    kf[...] = rope(x_ref[:, c + 128:c + 256].astype(F32), cos, sin).astype(BF16).astype(F32)
    vf[...] = x_ref[:, c + 256:c + 384].astype(F32)


def _dil_fwd(proj_b, cos_t, sin_t, seq):
    bsz = proj_b.shape[0]

    def body(x_ref, cos_ref, sin_ref, ob_ref, lse_ref, qf, kf, vf, og, lg):
        consts = _dil_consts()
        head0 = consts[0]
        rope = _make_rope(seq)
        cos, sin = cos_ref[...], sin_ref[...]
        for g, dil in enumerate(DIL_GROUPS):
            _dil_load_qkv(x_ref, 384 * g, rope, cos, sin, qf, kf, vf)

            def block(cur, prev, i, g=g):
                _, _, vcat, s, valid = _dil_scores(qf, kf, vf, cur, prev, i, consts)
                s = jnp.where(valid, s, NEG_INF)
                m = jnp.max(s, axis=1, keepdims=True)
                p = jnp.exp(s - m)
                den = jnp.sum(p, axis=1, keepdims=True)
                o = jnp.dot(p.astype(BF16), vcat, preferred_element_type=F32) / den
                og[g, cur, :] = _unstack_heads(o, head0)
                lg[g, cur, :] = _unstack_heads(jnp.broadcast_to(m + jnp.log(den), (2 * BLK, BLK)), head0)

            _dil_blocks(dil, seq, block)
        ls = [lg[0], lg[1], lg[2]]
        m = jnp.maximum(jnp.maximum(ls[0], ls[1]), ls[2])
        ws = [jnp.exp(l - m) for l in ls]
        den = (ws[0] + ws[1]) + ws[2]
        ob_ref[...] = (((ws[0] * og[0] + ws[1] * og[1]) + ws[2] * og[2]) / den).astype(BF16)
        lse_ref[...] = m + jnp.log(den)

    tab_spec = pl.BlockSpec((seq, BLK), lambda b, hp: (0, 0))
    out_spec = pl.BlockSpec((None, seq, BLK), lambda b, hp: (b, 0, hp))
    slab = pltpu.VMEM((seq, BLK), F32)
    return pl.pallas_call(
        body, name="dil_attn_fwd", grid=(bsz, 2),
        out_shape=(jax.ShapeDtypeStruct((bsz, seq, 256), BF16), jax.ShapeDtypeStruct((bsz, seq, 256), F32)),
        in_specs=[pl.BlockSpec((None, seq, WB // 2), lambda b, hp: (b, 0, hp)), tab_spec, tab_spec],
        out_specs=(out_spec, out_spec),
        scratch_shapes=[slab, slab, slab, pltpu.VMEM((3, seq, BLK), F32), pltpu.VMEM((3, seq, BLK), F32)],
        compiler_params=_params("parallel", "parallel"),
    )(proj_b, cos_t, sin_t)


def _dil_bwd(proj_b, cos_t, sin_t, d_ob, o_b, lse, seq):
    bsz = proj_b.shape[0]

    def body(x_ref, cos_ref, sin_ref, do_ref, ob_ref, lse_ref, d_ref, qf, kf, vf, dof, dsf, dq_s, dk_acc, dv_acc):
        consts = _dil_consts()
        head0 = consts[0]
        rope = _make_rope(seq)
        cos, sin = cos_ref[...], sin_ref[...]
        do_all = do_ref[...].astype(F32)
        dof[...] = do_all
        head0_all = lax.broadcasted_iota(jnp.int32, (seq, BLK), 1) < 64
        d0, d1 = _head_rowsum(do_all * ob_ref[...].astype(F32), head0_all)
        dsf[...] = jnp.where(head0_all, d0, d1)
        for g, dil in enumerate(DIL_GROUPS):
            _dil_load_qkv(x_ref, 384 * g, rope, cos, sin, qf, kf, vf)
            dk_acc[...] = jnp.zeros_like(dk_acc)
            dv_acc[...] = jnp.zeros_like(dv_acc)

            def block(cur, prev, i):
                qs, kcat, vcat, s, valid = _dil_scores(qf, kf, vf, cur, prev, i, consts)
                dos = _stack_heads(dof[cur, :].astype(BF16), head0)
                p = jnp.where(valid, jnp.exp(s - _head_cols(lse_ref[cur, :])), 0.0)
                dp = lax.dot_general(dos, vcat, NT, preferred_element_type=F32)
                ds = ((p * (dp - _head_cols(dsf[cur, :]))) * 0.125).astype(BF16)
                dq_s[cur, :] = _unstack_heads(jnp.dot(ds, kcat, preferred_element_type=F32), head0)
                dk = lax.dot_general(ds, qs, TN, preferred_element_type=F32)
                dv = lax.dot_general(p.astype(BF16), dos, TN, preferred_element_type=F32)
                dk_acc[prev, :] += dk[:BLK]
                dk_acc[cur, :] += dk[BLK:]
                dv_acc[prev, :] += dv[:BLK]
                dv_acc[cur, :] += dv[BLK:]

            _dil_blocks(dil, seq, block)
            c = 384 * g
            d_ref[:, c:c + 128] = rope(dq_s[...], cos, -sin).astype(BF16)
            d_ref[:, c + 128:c + 256] = rope(dk_acc[...], cos, -sin).astype(BF16)
            d_ref[:, c + 256:c + 384] = dv_acc[...].astype(BF16)

    x_spec = pl.BlockSpec((None, seq, WB // 2), lambda b, hp: (b, 0, hp))
    tab_spec = pl.BlockSpec((seq, BLK), lambda b, hp: (0, 0))
    tok_spec = pl.BlockSpec((None, seq, BLK), lambda b, hp: (b, 0, hp))
    return pl.pallas_call(
        body, name="dil_attn_bwd", grid=(bsz, 2),
        out_shape=jax.ShapeDtypeStruct((bsz, seq, WB), BF16),
        in_specs=[x_spec, tab_spec, tab_spec, tok_spec, tok_spec, tok_spec], out_specs=x_spec,
        scratch_shapes=[pltpu.VMEM((seq, BLK), F32)] * 8,
        compiler_params=_params("parallel", "parallel"),
    )(proj_b, cos_t, sin_t, d_ob, o_b, lse)


MEM_SCALE = 128 ** -0.5
MEM_QB = 1024


def _mem_fwd(proj_a, kv, seq):
    bsz = proj_a.shape[0]

    def body(q_ref, k_ref, v_ref, o_ref):
        k, v = k_ref[...], v_ref[...]

        def qblock(i, carry):
            r0 = pl.multiple_of(i * MEM_QB, MEM_QB)
            s = lax.dot_general(q_ref[pl.ds(r0, MEM_QB), :], k, NT, preferred_element_type=F32) * MEM_SCALE
            p = jnp.exp(s - jnp.max(s, axis=1, keepdims=True))
            p = p / jnp.sum(p, axis=1, keepdims=True)
            o_ref[pl.ds(r0, MEM_QB), :] = jnp.dot(p.astype(BF16), v, preferred_element_type=F32).astype(BF16)
            return carry

        lax.fori_loop(0, seq // MEM_QB, qblock, 0)

    return pl.pallas_call(
        body, name="mem_attn_fwd", grid=(bsz, 4),
        out_shape=jax.ShapeDtypeStruct((bsz, seq, 512), BF16),
        in_specs=[pl.BlockSpec((None, seq, BLK), lambda b, h: (b, 0, 12 + h)),
                  pl.BlockSpec((None, MEM_LEN, BLK), lambda b, h: (b, 0, h)),
                  pl.BlockSpec((None, MEM_LEN, BLK), lambda b, h: (b, 0, 4 + h))],
        out_specs=pl.BlockSpec((None, seq, BLK), lambda b, h: (b, 0, h)),
        compiler_params=_params("parallel", "parallel"),
    )(proj_a, kv, kv)


def _mem_bwd(proj_a, kv, d_o, d_proj_a, seq):
    bsz = proj_a.shape[0]

    def body(q_ref, k_ref, v_ref, do_ref, _, dq_ref, dk_ref, dv_ref):
        k, v = k_ref[...], v_ref[...]

        def qblock(i, carry):
            dk, dv = carry
            r0 = pl.multiple_of(i * MEM_QB, MEM_QB)
            q, do = q_ref[pl.ds(r0, MEM_QB), :], do_ref[pl.ds(r0, MEM_QB), :]
            s = lax.dot_general(q, k, NT, preferred_element_type=F32) * MEM_SCALE
            p = jnp.exp(s - jnp.max(s, axis=1, keepdims=True))
            p = p / jnp.sum(p, axis=1, keepdims=True)
            dp = lax.dot_general(do, v, NT, preferred_element_type=F32)
            ds = ((p * (dp - jnp.sum(p * dp, axis=1, keepdims=True))) * MEM_SCALE).astype(BF16)
            dq_ref[pl.ds(r0, MEM_QB), :] = jnp.dot(ds, k, preferred_element_type=F32).astype(BF16)
            dk = dk + lax.dot_general(ds, q, TN, preferred_element_type=F32)
            dv = dv + lax.dot_general(p.astype(BF16), do, TN, preferred_element_type=F32)
            return dk, dv

        zero = jnp.zeros((MEM_LEN, BLK), F32)
        dk, dv = lax.fori_loop(0, seq // MEM_QB, qblock, (zero, zero))
        dk_ref[...] = dk.astype(BF16)
        dv_ref[...] = dv.astype(BF16)

    kv_spec = pl.BlockSpec((None, MEM_LEN, BLK), lambda b, h: (b, 0, h))
    return pl.pallas_call(
        body, name="mem_attn_bwd", grid=(bsz, 4),
        out_shape=(jax.ShapeDtypeStruct((bsz, seq, WA), BF16), jax.ShapeDtypeStruct((bsz, MEM_LEN, 512), BF16),
                   jax.ShapeDtypeStruct((bsz, MEM_LEN, 512), BF16)),
        in_specs=[pl.BlockSpec((None, seq, BLK), lambda b, h: (b, 0, 12 + h)), kv_spec,
                  pl.BlockSpec((None, MEM_LEN, BLK), lambda b, h: (b, 0, 4 + h)),
                  pl.BlockSpec((None, seq, BLK), lambda b, h: (b, 0, h)), ANY],
        out_specs=(pl.BlockSpec((None, seq, BLK), lambda b, h: (b, 0, 12 + h)), kv_spec, kv_spec),
        input_output_aliases={4: 0},
        compiler_params=_params("parallel", "parallel"),
    )(proj_a, kv, kv, d_o, d_proj_a)


def _mesh_pos():
    return lax.axis_index("x"), lax.axis_index("y"), lax.axis_index("c")


def _all_gather(shard, name):
    m_per, n = shard.shape

    def body(x_ref, out_ref, send_sems, recv_sems, local_sem):
        x, y, c = _mesh_pos()
        me, sibling = (x, y, c), (x, y, 1 - c)
        chips = [(1 - x, y), (x, 1 - y), (1 - x, 1 - y)]

        def rows(px, py, pc):
            return out_ref.at[pl.ds((4 * px + 2 * py + pc) * m_per, m_per), :]

        def copy(k, block, to, src=None):
            return pltpu.make_async_remote_copy(
                src_ref=rows(*block) if src is None else src, dst_ref=rows(*block),
                send_sem=send_sems.at[k], recv_sem=recv_sems.at[k], device_id=to, device_id_type=MESH)

        mine = pltpu.make_async_copy(x_ref, rows(*me), local_sem)
        mine.start()
        first = [copy(0, me, sibling, src=x_ref)]
        first += [copy(1 + j, me, (*chip, c), src=x_ref) for j, chip in enumerate(chips)]
        for cp in first:
            cp.start()
        passed = [copy(4 + j, (*chip, c), sibling) for j, chip in enumerate(chips)]
        for j, chip in enumerate(chips):
            copy(1 + j, (*chip, c), me).wait_recv()
            passed[j].start()
        copy(0, sibling, me).wait_recv()
        for j, chip in enumerate(chips):
            copy(4 + j, (*chip, 1 - c), me).wait_recv()
        for cp in first + passed:
            cp.wait_send()
        mine.wait()

    return pl.pallas_call(
        body, name=name, out_shape=jax.ShapeDtypeStruct((N_DEV * m_per, n), shard.dtype),
        in_specs=[ANY], out_specs=ANY,
        scratch_shapes=[pltpu.SemaphoreType.DMA((7,)), pltpu.SemaphoreType.DMA((7,)), pltpu.SemaphoreType.DMA(())],
    )(shard)


def _exchange_sems(n_arrays):
    return [pltpu.SemaphoreType.DMA((7 * n_arrays,)), pltpu.SemaphoreType.DMA((7 * n_arrays,)),
            pltpu.SemaphoreType.DMA((n_arrays,))]


def _exchange_out(srcs, gather):
    return [jax.ShapeDtypeStruct((N_DEV, *s.shape[-2:]), s.dtype) for s in srcs]


def _direct_exchange(src_refs, dst_refs, send_sems, recv_sems, local_sems, gather):
    x, y, c = _mesh_pos()
    me = 4 * x + 2 * y + c
    owns, sends, recvs = [], [], []
    for a, (src, dst) in enumerate(zip(src_refs, dst_refs)):
        owns.append(pltpu.make_async_copy(src if gather else src.at[me], dst.at[me], local_sems.at[a]))
        for j in range(1, N_DEV):
            px = 1 - x if j & 4 else x
            py = 1 - y if j & 2 else y
            pc = 1 - c if j & 1 else c
            peer = 4 * px + 2 * py + pc
            sems = dict(send_sem=send_sems.at[7 * a + j - 1], recv_sem=recv_sems.at[7 * a + j - 1],
                        device_id=(px, py, pc), device_id_type=MESH)
            sends.append(pltpu.make_async_remote_copy(
                src_ref=src if gather else src.at[peer], dst_ref=dst.at[me], **sems))
            recvs.append(pltpu.make_async_remote_copy(
                src_ref=src if gather else src.at[me], dst_ref=dst.at[peer], **sems))

    def start():
        for cp in owns + sends:
            cp.start()

    def wait():
        for cp in recvs:
            cp.wait_recv()
        for cp in sends:
            cp.wait_send()
        for cp in owns:
            cp.wait()

    return start, wait


def _riding_exchange(src_refs, dst_refs, sems, gather):
    start, wait = _direct_exchange(src_refs, dst_refs, *sems, gather)
    ids = [pl.program_id(a) for a in range(2)]
    last = [pl.num_programs(a) - 1 for a in range(2)]
    pl.when(jnp.logical_and(ids[0] == 0, ids[1] == 0))(start)
    return lambda: pl.when(jnp.logical_and(ids[0] == last[0], ids[1] == last[1]))(wait)


def _exchange(srcs, gather, name):
    n = len(srcs)

    def body(*refs):
        start, wait = _direct_exchange(refs[:n], refs[n:2 * n], *refs[2 * n:], gather=gather)
        start()
        wait()

    return pl.pallas_call(
        body, name=name, out_shape=_exchange_out(srcs, gather),
        in_specs=[ANY] * n, out_specs=[ANY] * n, scratch_shapes=_exchange_sems(n),
    )(*srcs)


def _adamw(w, g, m, v):
    m = ADAM_B1 * m + (1.0 - ADAM_B1) * g
    v = ADAM_B2 * v + (1.0 - ADAM_B2) * (g * g)
    m_hat = m / (1.0 - ADAM_B1 ** ADAM_STEP)
    v_hat = v / (1.0 - ADAM_B2 ** ADAM_STEP)
    return -ADAM_LR * (m_hat / (jnp.sqrt(v_hat) + ADAM_EPS) + ADAM_WD * w), m, v


def _reduce_adamw(recv, w, m, v, name):
    _, k, n = w.shape
    tr = max(t for t in range(16, 257, 16) if k % t == 0)

    def body(r_ref, w_ref, m_ref, v_ref, g_out, d_out, m_out, v_out):
        g = r_ref[0].astype(F32)
        for s in range(1, N_DEV):
            g = g + r_ref[s].astype(F32)
        g_out[...] = g
        d_out[...], m_out[...], v_out[...] = _adamw(w_ref[...], g, m_ref[...], v_ref[...])

    spec = pl.BlockSpec((None, tr, n), lambda i: (0, i, 0))
    return pl.pallas_call(
        body, name=name, grid=(k // tr,),
        out_shape=[jax.ShapeDtypeStruct((1, k, n), F32)] * 4,
        in_specs=[pl.BlockSpec((N_DEV, tr, n), lambda i: (0, i, 0)), spec, spec, spec],
        out_specs=[spec] * 4, compiler_params=_params("arbitrary"),
    )(recv, w, m, v)


def _small_adamw(gathered, w, m, v):
    def body(g_ref, w_ref, m_ref, v_ref, g_out, d_out, m_out, v_out, loss_out):
        tot = g_ref[0]
        for s in range(1, N_DEV):
            tot = tot + g_ref[s]
        g = tot[0:8]
        g_out[...] = g
        d_out[...], m_out[...], v_out[...] = _adamw(w_ref[...], g, m_ref[...], v_ref[...])
        loss_out[...] = jnp.broadcast_to((0.5 / D) * jnp.sum(tot[8:9], axis=1, keepdims=True), (8, BLK))

    out = [jax.ShapeDtypeStruct((8, D), F32)] * 4 + [jax.ShapeDtypeStruct((8, BLK), F32)]
    return pl.pallas_call(body, name="small_adamw", out_shape=out, compiler_params=_params())(gathered, w, m, v)


def _pick_chunks(w, chunks):
    return jnp.concatenate([w[:, BLK * c:BLK * (c + 1)] for c in chunks], axis=1)


def _whole_weight(gathered, i):
    _, k, n = gathered.shape
    if BY_ROWS[i]:
        return gathered.reshape(N_DEV * k, n)
    return gathered.transpose(1, 0, 2).reshape(k, N_DEV * n)


def _shard_parts(grad, i):
    if BY_ROWS[i]:
        return grad.reshape(N_DEV, grad.shape[0] // N_DEV, grad.shape[1])
    k, n8 = grad.shape
    return grad.reshape(k, N_DEV, n8 // N_DEV).transpose(1, 0, 2)


def kernel(x, mem, g_pre_mix, g_post_mix, g_pre_ffn, g_post_ffn, g_mem, w_in, w_mem_kv, w_br_sb, w_br_dil, w_br_mem, w_gate, b_gate, w_o, w_ffn_in, w_ffn_out, loss_target, m_g_pre_mix, m_g_post_mix, m_g_pre_ffn, m_g_post_ffn, m_g_mem, m_w_in, m_w_mem_kv, m_w_br_sb, m_w_br_dil, m_w_br_mem, m_w_gate, m_b_gate, m_w_o, m_w_ffn_in, m_w_ffn_out, v_g_pre_mix, v_g_post_mix, v_g_pre_ffn, v_g_post_ffn, v_g_mem, v_w_in, v_w_mem_kv, v_w_br_sb, v_w_br_dil, v_w_br_mem, v_w_gate, v_b_gate, v_w_o, v_w_ffn_in, v_w_ffn_out):
    bsz, seq, _ = x.shape
    tokens = bsz * seq
    xf, tgt, memf = x.reshape(tokens, D), loss_target.reshape(tokens, D), mem.reshape(bsz * MEM_LEN, D)
    big_w = [w_in, w_mem_kv, w_br_sb, w_br_dil, w_br_mem, w_gate, w_o, w_ffn_in, w_ffn_out]
    big_m = [m_w_in, m_w_mem_kv, m_w_br_sb, m_w_br_dil, m_w_br_mem, m_w_gate, m_w_o, m_w_ffn_in, m_w_ffn_out]
    big_v = [v_w_in, v_w_mem_kv, v_w_br_sb, v_w_br_dil, v_w_br_mem, v_w_gate, v_w_o, v_w_ffn_in, v_w_ffn_out]

    shards = [w[0].astype(BF16) for w in big_w]
    k_in, n_in = shards[0].shape
    fw_in = _whole_weight(_all_gather(shards[0], "weight_all_gather").reshape(N_DEV, k_in, n_in), 0)
    w_a, w_b = _pick_chunks(fw_in, CHUNKS_A), _pick_chunks(fw_in, CHUNKS_B)

    h = _norm_fwd(xf, g_pre_mix, "pre_mix_norm")
    proj_a = _matmul(h, w_a, "nn", BF16, "proj_a").reshape(bsz, seq, WA)
    proj_b = _matmul(h, w_b, "nn", BF16, "proj_b").reshape(bsz, seq, WB)
    o_a, o_a32, *behind = _sb_fwd(proj_a, seq, [shards[i] for i in GATHER_BEHIND])
    fw_mem_kv, fw_br_sb, fw_br_dil, fw_br_mem, fw_gate, fw_o, fw_ffn_in, fw_ffn_out = (
        _whole_weight(g, i) for g, i in zip(behind, GATHER_BEHIND))
    gpre = _matmul(h, fw_gate, "nn", BF16, "gate_proj")
    cos_t, sin_t = _rope_tables(seq)
    o_b, lse_b = _dil_fwd(proj_b, cos_t, sin_t, seq)
    mn = _norm_fwd(memf, g_mem, "mem_norm")
    kv = _matmul(mn, fw_mem_kv, "nn", BF16, "mem_kv_proj").reshape(bsz, MEM_LEN, D)
    o_c = _mem_fwd(proj_a, kv, seq)
    o_a2, o_b2, o_c2 = o_a.reshape(tokens, 512), o_b.reshape(tokens, 256), o_c.reshape(tokens, 512)
    ys = [_matmul(o_a2, fw_br_sb, "nn", BF16, "branch_sb"), _matmul(o_b2, fw_br_dil, "nn", BF16, "branch_dil"),
          _matmul(o_c2, fw_br_mem, "nn", BF16, "branch_mem")]
    merged = _gate_merge(gpre, ys, b_gate)
    mix, x1, h2 = _out_proj_norm(merged, fw_o, xf, g_post_mix, g_pre_ffn)
    gu_a, gu_b, f = _ffn_in_swiglu(h2, fw_ffn_in)
    dy, dfo, dg_post_ffn, loss_lanes = _ffn_out_loss(f, fw_ffn_out, x1, tgt, g_post_ffn)

    gw_ffn_out = _matmul(f, dfo, "tn", BF16, "gw_ffn_out")
    dgu = _d_ffn_swiglu_bwd(dfo, fw_ffn_out, gu_a, gu_b)
    gw_ffn_in = _matmul(h2, dgu, "tn", BF16, "gw_ffn_in")
    dx1, dmix, dg_pre_ffn, dg_post_mix = _d_h2_norm_bwd(dgu, fw_ffn_in, x1, dy, mix, g_pre_ffn, g_post_mix)
    gw_o = _matmul(merged, dmix, "tn", BF16, "gw_o")
    dya, dyb, dyc, dgpre, db_gate = _d_merged_gate_bwd(dmix, fw_o, gpre, ys, b_gate)
    d_oa = _matmul(dya, fw_br_sb, "nt", BF16, "d_o_sb").reshape(bsz, seq, 512)
    d_ob = _matmul(dyb, fw_br_dil, "nt", BF16, "d_o_dil").reshape(bsz, seq, 256)
    d_oc = _matmul(dyc, fw_br_mem, "nt", BF16, "d_o_mem").reshape(bsz, seq, 512)
    gw_br_sb = _matmul(o_a2, dya, "tn", BF16, "gw_br_sb")
    gw_br_dil = _matmul(o_b2, dyb, "tn", BF16, "gw_br_dil")
    gw_br_mem = _matmul(o_c2, dyc, "tn", BF16, "gw_br_mem")
    gw_gate = _matmul(h, dgpre, "tn", BF16, "gw_gate")
    grads = {2: gw_br_sb, 3: gw_br_dil, 4: gw_br_mem, 5: gw_gate, 6: gw_o, 7: gw_ffn_in, 8: gw_ffn_out}
    d_proj_a, *recv_behind = _sb_bwd(proj_a, d_oa, o_a32, seq, [_shard_parts(grads[i], i) for i in REDUCE_BEHIND])
    d_proj_a, dk_m, dv_m = _mem_bwd(proj_a, kv, d_oc, d_proj_a, seq)
    d_proj_b = _dil_bwd(proj_b, cos_t, sin_t, d_ob, o_b, lse_b, seq).reshape(tokens, WB)
    d_proj_a = d_proj_a.reshape(tokens, WA)
    gw_a = _matmul(h, d_proj_a, "tn", BF16, "gw_in_a")
    gw_b = _matmul(h, d_proj_b, "tn", BF16, "gw_in_b")
    dkv = jnp.concatenate([dk_m, dv_m], axis=-1).reshape(bsz * MEM_LEN, D)
    gw_mem_kv = _matmul(mn, dkv, "tn", BF16, "gw_mem_kv")
    dmn = _matmul(dkv, fw_mem_kv, "nt", F32, "d_mem_norm")
    dg_mem = _gain_grad(dmn, memf)
    gw_ab = jnp.concatenate([gw_a, gw_b], axis=1)
    where = {c: i for i, c in enumerate(CHUNKS_A + CHUNKS_B)}
    grads = {0: _pick_chunks(gw_ab, [where[c] for c in range(34)]), 1: gw_mem_kv}
    dx, dg_pre_mix, *recv_last = _d_h_norm_bwd(
        [(dgpre, fw_gate), (d_proj_a, w_a), (d_proj_b, w_b)], xf, dx1, g_pre_mix,
        [_shard_parts(grads[i], i) for i in REDUCE_LAST])

    received = dict(zip(REDUCE_BEHIND + REDUCE_LAST, [*recv_behind, *recv_last]))
    adam = [_reduce_adamw(received[i], big_w[i], big_m[i], big_v[i], "reduce_adamw_" + BIG_NAMES[i])
            for i in range(len(big_w))]
    big = [[a[k] for a in adam] for k in range(4)]

    small = jnp.concatenate([dg_pre_mix, dg_post_mix, dg_pre_ffn, dg_post_ffn, dg_mem, db_gate.reshape(3, D),
                             loss_lanes, jnp.zeros((7, D), F32)], axis=0)
    small_all, = _exchange([small], True, "small_all_gather")

    def small_pack(gs, b):
        return jnp.concatenate([*gs, b.reshape(3, D)], axis=0)

    sm = _small_adamw(
        small_all, small_pack([g_pre_mix, g_post_mix, g_pre_ffn, g_post_ffn, g_mem], b_gate),
        small_pack([m_g_pre_mix, m_g_post_mix, m_g_pre_ffn, m_g_post_ffn, m_g_mem], m_b_gate),
        small_pack([v_g_pre_mix, v_g_post_mix, v_g_pre_ffn, v_g_post_ffn, v_g_mem], v_b_gate))
    loss = sm[4][0, 0]

    def leaves(k):
        t, bw = sm[k], big[k]
        return [t[0:1], t[1:2], t[2:3], t[3:4], t[4:5], *bw[0:6], t[5:8].reshape(1, 3 * D), *bw[6:9]]

    return (loss, dx.reshape(bsz, seq, D), *leaves(0), *leaves(1), *leaves(2), *leaves(3))
```

```python
import functools

import jax
import jax.numpy as jnp
from jax import lax
from jax.experimental import pallas as pl
from jax.experimental.pallas import tpu as pltpu

F32 = jnp.float32
BF16 = jnp.bfloat16
D = 1024
BLK = 128
MEM_LEN = 256
D_FF = 2816
NORM_EPS = 1e-6
NEG_INF = -1e30
ROPE_THETA = 10000.0
ADAM_LR, ADAM_B1, ADAM_B2, ADAM_EPS, ADAM_WD, ADAM_STEP = 0.001, 0.9, 0.999, 1e-08, 0.01, 10
N_DEV = 8
VMEM_LIMIT_BYTES = 56 * 1024 * 1024
MESH = pl.DeviceIdType.MESH
ANY = pl.BlockSpec(memory_space=pl.ANY)

NT = (((1,), (1,)), ((), ()))
TN = (((0,), (0,)), ((), ()))
NN = (((1,), (0,)), ((), ()))
_DIMS = {"nn": NN, "nt": NT, "tn": TN}

BIG_NAMES = ("w_in", "w_mem_kv", "w_br_sb", "w_br_dil", "w_br_mem", "w_gate", "w_o", "w_ffn_in", "w_ffn_out")
BY_ROWS = (False, True, False, False, False, False, True, False, True)
GATHER_FIRST = (0,)
GATHER_BEHIND = (1, 2, 3, 4, 5, 6, 7, 8)
REDUCE_BEHIND = (2, 3, 4, 5, 6, 7, 8)
REDUCE_LAST = (0, 1)

CHUNKS_A = tuple(c for hp in range(4) for c in (hp, 4 + hp, 8 + hp)) + (30, 31, 32, 33)
CHUNKS_B = tuple(c for hp in range(2) for g in range(3) for c in (12 + 6 * g + hp, 14 + 6 * g + hp, 16 + 6 * g + hp))
WA, WB = 128 * len(CHUNKS_A), 128 * len(CHUNKS_B)
DIL_GROUPS = (1, 4, 16)


def _params(*sem):
    return pltpu.CompilerParams(dimension_semantics=sem or None, vmem_limit_bytes=VMEM_LIMIT_BYTES)


def _tile(n, cap):
    if n <= 128:
        return n
    assert n % 128 == 0, n
    best = 128
    for t in range(128, min(n, cap) + 1, 128):
        if n % t == 0:
            best = t
    return best


def _k_steps(k, nk, step):
    if nk == 1:
        step(True, True)
        return
    pl.when(k == 0)(functools.partial(step, True, False))
    if nk > 2:
        pl.when(jnp.logical_and(k > 0, k < nk - 1))(functools.partial(step, False, False))
    pl.when(k == nk - 1)(functools.partial(step, False, True))


def _matmul(a, b, mode, out_dtype, name, tm_cap=1536, tn_cap=1536, tk_cap=1536):
    if mode == "tn":
        (K, M), N = a.shape, b.shape[1]
    elif mode == "nt":
        (M, K), N = a.shape, b.shape[0]
    else:
        (M, K), N = a.shape, b.shape[1]
    tm, tn, tk = _tile(M, tm_cap), _tile(N, tn_cap), _tile(K, tk_cap)
    nm, nn, nk = M // tm, N // tn, K // tk
    dims = _DIMS[mode]

    def body(a_ref, b_ref, o_ref, *acc):
        def step(first, last):
            d = lax.dot_general(a_ref[...], b_ref[...], dims, preferred_element_type=F32)
            if not first:
                d = d + acc[0][...]
            if last:
                o_ref[...] = d.astype(o_ref.dtype)
            else:
                acc[0][...] = d

        _k_steps(pl.program_id(2), nk, step)

    n_outer = nk == 1 and (a.size * nn + b.size) < (a.size + b.size * nm)
    if n_outer:
        grid, ij = (nn, nm, nk), (lambda g0, g1: (g1, g0))
    else:
        grid, ij = (nm, nn, nk), (lambda g0, g1: (g0, g1))
    if mode == "tn":
        a_spec = pl.BlockSpec((tk, tm), lambda g0, g1, k: (k, ij(g0, g1)[0]))
    else:
        a_spec = pl.BlockSpec((tm, tk), lambda g0, g1, k: (ij(g0, g1)[0], k))
    if mode == "nt":
        b_spec = pl.BlockSpec((tn, tk), lambda g0, g1, k: (ij(g0, g1)[1], k))
    else:
        b_spec = pl.BlockSpec((tk, tn), lambda g0, g1, k: (k, ij(g0, g1)[1]))
    return pl.pallas_call(
        body, name=name, grid=grid,
        out_shape=jax.ShapeDtypeStruct((M, N), out_dtype),
        in_specs=[a_spec, b_spec],
        out_specs=pl.BlockSpec((tm, tn), lambda g0, g1, k: ij(g0, g1)),
        scratch_shapes=[pltpu.VMEM((tm, tn), F32)] if nk > 1 else [],
        compiler_params=_params("parallel", "parallel", "arbitrary"),
    )(a, b)


def _rowwise(body, name, rows, tr, row_ins, vec_ins, row_outs, acc_outs=()):
    tr = min(tr, rows)
    assert rows % tr == 0
    in_specs, args = [], []
    for r in row_ins:
        arr, w, cb = r if isinstance(r, tuple) else (r, r.shape[1], 0)
        in_specs.append(pl.BlockSpec((tr, w), functools.partial(lambda i, cb: (i, cb), cb=cb)))
        args.append(arr)
    for v in vec_ins:
        in_specs.append(pl.BlockSpec(v.shape, lambda i: (0, 0)))
        args.append(v)
    out_shape = [jax.ShapeDtypeStruct((rows, w), dt) for w, dt in row_outs]
    out_shape += [jax.ShapeDtypeStruct((1, w), F32) for w in acc_outs]
    out_specs = [pl.BlockSpec((tr, w), lambda i: (i, 0)) for w, _ in row_outs]
    out_specs += [pl.BlockSpec((1, w), lambda i: (0, 0)) for w in acc_outs]
    n_acc = len(acc_outs)

    def wrapped(*refs):
        if n_acc:
            @pl.when(pl.program_id(0) == 0)
            def _():
                for r in refs[len(refs) - n_acc:]:
                    r[...] = jnp.zeros_like(r)
        body(*refs)

    return pl.pallas_call(
        wrapped, name=name, grid=(rows // tr,), out_shape=out_shape, in_specs=in_specs, out_specs=out_specs,
        compiler_params=_params("arbitrary"),
    )(*args)


def _rstd(x):
    return lax.rsqrt(jnp.mean(x * x, axis=-1, keepdims=True) + NORM_EPS)


def _norm_bwd(u, n, r):
    return r * (u - n * jnp.mean(u * n, axis=-1, keepdims=True))


def _colsum(v):
    return jnp.sum(v, axis=0, keepdims=True)


def _norm_fwd(x, g, name):
    def body(x_ref, g_ref, h_ref):
        xv = x_ref[...]
        h_ref[...] = ((xv * _rstd(xv)) * g_ref[...]).astype(BF16)

    return _rowwise(body, name, x.shape[0], 512, [x], [g], [(D, BF16)])[0]


def _matmul_rows(pairs, mode, name, epilogue, row_ins=(), vec_ins=(), row_outs=(), acc_outs=(), ride=None,
                 tm=512, tk_cap=1536):
    M = pairs[0][0].shape[0]
    N = pairs[0][1].shape[1] if mode == "nn" else pairs[0][1].shape[0]
    tm = min(tm, M)
    tks = [_tile(a.shape[1], tk_cap) for a, _ in pairs]
    nks = [a.shape[1] // tk for (a, _), tk in zip(pairs, tks)]
    offs = [sum(nks[:p]) for p in range(len(pairs))]
    nm, nk = M // tm, sum(nks)
    dims = _DIMS[mode]
    n_ab, n_extra, n_out = 2 * len(pairs), len(row_ins) + len(vec_ins), len(row_outs) + len(acc_outs)
    n_ride = 0 if ride is None else len(ride)

    def body(*refs):
        ab, extra, rest = refs[:n_ab], refs[n_ab:n_ab + n_extra], refs[n_ab + n_extra:]
        ride_refs, outs, rest = rest[:n_ride], rest[n_ride:n_ride + n_out], rest[n_ride + n_out:]
        received_refs, rest = rest[:n_ride], rest[n_ride:]
        if n_ride:
            finish_ride = _riding_exchange(ride_refs, received_refs, rest[len(rest) - 3:], gather=False)
        i, k = pl.program_id(0), pl.program_id(1)
        if acc_outs:
            @pl.when(jnp.logical_and(i == 0, k == 0))
            def _():
                for r in outs[len(row_outs):]:
                    r[...] = jnp.zeros_like(r)

        def step(p, first, last):
            d = lax.dot_general(ab[2 * p][...], ab[2 * p + 1][...], dims, preferred_element_type=F32)
            if not first:
                d = d + rest[0][...]
            if last:
                epilogue(d, *extra, *outs)
            else:
                rest[0][...] = d

        last_p = len(pairs) - 1
        if nk == 1:
            step(0, True, True)
        else:
            pl.when(k == 0)(functools.partial(step, 0, True, False))
            for p in range(len(pairs)):
                lo, hi = max(offs[p], 1), min(offs[p] + nks[p], nk - 1)
                if hi > lo:
                    pl.when(jnp.logical_and(k >= lo, k < hi))(functools.partial(step, p, False, False))
            pl.when(k == nk - 1)(functools.partial(step, last_p, False, True))
        if n_ride:
            finish_ride()

    in_specs, args = [], []
    for p, ((a, b), tk) in enumerate(zip(pairs, tks)):
        step = functools.partial(lambda k, p: jnp.clip(k - offs[p], 0, nks[p] - 1), p=p)
        in_specs.append(pl.BlockSpec((tm, tk), functools.partial(lambda i, k, step: (i, step(k)), step=step)))
        if mode == "nn":
            in_specs.append(pl.BlockSpec((tk, N), functools.partial(lambda i, k, step: (step(k), 0), step=step)))
        else:
            in_specs.append(pl.BlockSpec((N, tk), functools.partial(lambda i, k, step: (0, step(k)), step=step)))
        args += [a, b]
    in_specs += [pl.BlockSpec((tm, r.shape[1]), lambda i, k: (i, 0)) for r in row_ins]
    in_specs += [pl.BlockSpec(v.shape, lambda i, k: (0, 0)) for v in vec_ins]
    in_specs += [ANY] * n_ride
    out_shape = [jax.ShapeDtypeStruct((M, w), dt) for w, dt in row_outs]
    out_shape += [jax.ShapeDtypeStruct((1, w), F32) for w in acc_outs]
    out_specs = [pl.BlockSpec((tm, w), lambda i, k: (i, 0)) for w, _ in row_outs]
    out_specs += [pl.BlockSpec((1, w), lambda i, k: (0, 0)) for w in acc_outs]
    scratch = [pltpu.VMEM((tm, N), F32)] if nk > 1 else []
    if n_ride:
        out_shape += _exchange_out(ride, False)
        out_specs += [ANY] * n_ride
        scratch += _exchange_sems(n_ride)
    return pl.pallas_call(
        body, name=name, grid=(nm, nk), out_shape=out_shape, in_specs=in_specs, out_specs=out_specs,
        scratch_shapes=scratch, compiler_params=_params("arbitrary", "arbitrary"),
    )(*args, *row_ins, *vec_ins, *(ride or []))


def _out_proj_norm(merged, w_o, x, g_post, g_pre):
    def epilogue(mv, x_ref, g2_ref, g3_ref, mix_ref, x1_ref, h2_ref):
        mix_ref[...] = mv
        x1 = x_ref[...] + (mv * _rstd(mv)) * g2_ref[...]
        x1_ref[...] = x1
        h2_ref[...] = ((x1 * _rstd(x1)) * g3_ref[...]).astype(BF16)

    return _matmul_rows([(merged, w_o)], "nn", "out_proj_norm", epilogue, [x], [g_post, g_pre],
                        [(D, F32), (D, F32), (D, BF16)])


def _gate_merge(gpre, ys, b_gate):
    def body(gp_ref, ya_ref, yb_ref, yc_ref, b_ref, m_ref):
        acc = None
        for k, y_ref in enumerate((ya_ref, yb_ref, yc_ref)):
            cols = slice(k * D, (k + 1) * D)
            gate = jax.nn.sigmoid(gp_ref[:, cols].astype(F32) + b_ref[:, cols])
            term = gate * y_ref[...].astype(F32)
            acc = term if acc is None else acc + term
        m_ref[...] = acc.astype(BF16)

    return _rowwise(body, "gate_merge", gpre.shape[0], 256, [gpre, *ys], [b_gate], [(D, BF16)])[0]


def _ffn_in_swiglu(h2, w_ffn_in):
    tokens = h2.shape[0]
    tm, tn = min(512, tokens), _tile(D_FF, 1536)
    nj = D_FF // tn

    def body(h_ref, wa_ref, wb_ref, a_ref, b_ref, f_ref):
        hv = h_ref[...]
        a = jnp.dot(hv, wa_ref[...], preferred_element_type=F32)
        b = jnp.dot(hv, wb_ref[...], preferred_element_type=F32)
        a_ref[...] = a.astype(BF16)
        b_ref[...] = b.astype(BF16)
        f_ref[...] = (a * jax.nn.sigmoid(a) * b).astype(BF16)

    out = jax.ShapeDtypeStruct((tokens, D_FF), BF16)
    o_spec = pl.BlockSpec((tm, tn), lambda j, i: (i, j))
    return pl.pallas_call(
        body, name="ffn_in_swiglu", grid=(nj, tokens // tm), out_shape=(out, out, out),
        in_specs=[pl.BlockSpec((tm, D), lambda j, i: (i, 0)), pl.BlockSpec((D, tn), lambda j, i: (0, j)),
                  pl.BlockSpec((D, tn), lambda j, i: (0, j + nj))],
        out_specs=(o_spec, o_spec, o_spec), compiler_params=_params("parallel", "parallel"),
    )(h2, w_ffn_in, w_ffn_in)


def _ffn_out_loss(f, w_ffn_out, x1, tgt, g_post):
    def epilogue(fo_v, x1_ref, t_ref, g_ref, dy_ref, dfo_ref, dg_ref, loss_ref):
        r = _rstd(fo_v)
        n = fo_v * r
        err = (x1_ref[...] + n * g_ref[...]) - t_ref[...]
        loss_ref[...] += _colsum(err * err)
        dy = err * (1.0 / D)
        dy_ref[...] = dy
        dg_ref[...] += _colsum(dy * n)
        dfo_ref[...] = _norm_bwd(dy * g_ref[...], n, r).astype(BF16)

    return _matmul_rows([(f, w_ffn_out)], "nn", "ffn_out_loss", epilogue, [x1, tgt], [g_post],
                        [(D, F32), (D, BF16)], (D, D))


def _d_ffn_swiglu_bwd(dfo, w_ffn_out, gu_a, gu_b):
    def epilogue(d, a_ref, b_ref, dgu_ref):
        a = a_ref[...].astype(F32)
        b = b_ref[...].astype(F32)
        s = jax.nn.sigmoid(a)
        dgu_ref[:, :D_FF] = (d * b * (s * (1.0 + a * (1.0 - s)))).astype(BF16)
        dgu_ref[:, D_FF:] = (d * (a * s)).astype(BF16)

    return _matmul_rows([(dfo, w_ffn_out)], "nt", "d_ffn_swiglu_bwd", epilogue, [gu_a, gu_b], [],
                        [(2 * D_FF, BF16)], tm=256)[0]


def _d_h2_norm_bwd(dgu, w_ffn_in, x1, dy, mix, g_pre, g_post):
    def epilogue(dh, x1_ref, dy_ref, mix_ref, g3_ref, g2_ref, dx1_ref, dmix_ref, dg3_ref, dg2_ref):
        x1v = x1_ref[...]
        r3 = _rstd(x1v)
        n3 = x1v * r3
        dg3_ref[...] += _colsum(dh * n3)
        dx1 = dy_ref[...] + _norm_bwd(dh * g3_ref[...], n3, r3)
        dx1_ref[...] = dx1
        mv = mix_ref[...]
        r2 = _rstd(mv)
        n2 = mv * r2
        dg2_ref[...] += _colsum(dx1 * n2)
        dmix_ref[...] = _norm_bwd(dx1 * g2_ref[...], n2, r2).astype(BF16)

    return _matmul_rows([(dgu, w_ffn_in)], "nt", "d_h2_norm_bwd", epilogue, [x1, dy, mix], [g_pre, g_post],
                        [(D, F32), (D, BF16)], (D, D))


def _d_merged_gate_bwd(dmix, w_o, gpre, ys, b_gate):
    def epilogue(dm, gp_ref, ya_ref, yb_ref, yc_ref, b_ref, dya_ref, dyb_ref, dyc_ref, dgp_ref, db_ref):
        for k, (y_ref, dy_ref) in enumerate(((ya_ref, dya_ref), (yb_ref, dyb_ref), (yc_ref, dyc_ref))):
            cols = slice(k * D, (k + 1) * D)
            gate = jax.nn.sigmoid(gp_ref[:, cols].astype(F32) + b_ref[:, cols])
            dy_ref[...] = (dm * gate).astype(BF16)
            dgp = (dm * y_ref[...].astype(F32)) * (gate * (1.0 - gate))
            dgp_ref[:, cols] = dgp.astype(BF16)
            db_ref[:, cols] += _colsum(dgp)

    return _matmul_rows([(dmix, w_o)], "nt", "d_merged_gate_bwd", epilogue, [gpre, *ys], [b_gate],
                        [(D, BF16), (D, BF16), (D, BF16), (3 * D, BF16)], (3 * D,))


def _d_h_norm_bwd(pairs, x, dx1, g_pre, ride):
    def epilogue(dh, x_ref, dx1_ref, g_ref, dx_ref, dg_ref):
        xv = x_ref[...]
        r = _rstd(xv)
        n = xv * r
        dg_ref[...] += _colsum(dh * n)
        dx_ref[...] = dx1_ref[...] + _norm_bwd(dh * g_ref[...], n, r)

    return _matmul_rows(pairs, "nt", "d_h_norm_bwd", epilogue, [x, dx1], [g_pre], [(D, F32)], (D,), ride=ride)


def _gain_grad(dmn, mem):
    def body(d_ref, m_ref, dg_ref):
        mv = m_ref[...]
        dg_ref[...] += _colsum(d_ref[...] * (mv * _rstd(mv)))

    return _rowwise(body, "mem_gain_grad", mem.shape[0], 256, [dmn, mem], [], [], (D,))[0]


def _head_rowsum(v, head0):
    return (jnp.sum(jnp.where(head0, v, 0.0), axis=1, keepdims=True),
            jnp.sum(jnp.where(head0, 0.0, v), axis=1, keepdims=True))


KT = 256
SB_SCALE = 0.125


def _make_suffix(inclusive):
    row, col = lax.broadcasted_iota(jnp.int32, (KT, KT), 0), lax.broadcasted_iota(jnp.int32, (KT, KT), 1)
    tri = (row >= col if inclusive else row > col).astype(BF16)
    tri2 = jnp.concatenate([tri, tri], axis=0)

    def suffix(x):
        hi = x.astype(BF16)
        lo = (x - hi.astype(F32)).astype(BF16)
        return jnp.dot(jnp.concatenate([hi, lo], axis=1), tri2, preferred_element_type=F32)

    return suffix


def _sb_scores(qh, k, mask, suffix_incl, later):
    z = lax.dot_general(qh, k, NT, preferred_element_type=F32)
    zc = jnp.minimum(z, 60.0)
    sp = jnp.log(1.0 + jnp.exp(zc))
    if mask is not None:
        sp = jnp.where(mask, sp, 0.0)
    a = jnp.exp((zc - suffix_incl(sp)) - later)
    if mask is not None:
        a = jnp.where(mask, a, 0.0)
    return zc, sp, a


QB = KT


def _sb_tiles(i, tile, init):
    st = lax.cond(i > 0, lambda s: tile(i - 1, tile(i, s, True), False), lambda s: tile(i, s, True), init)
    rest = jnp.maximum(i - 1, 0)
    st = lax.fori_loop(0, lax.shift_right_logical(rest, 1),
                       lambda t, s: tile(rest - 2 - 2 * t, tile(rest - 1 - 2 * t, s, False), False), st)
    return lax.cond((rest & 1) == 1, lambda s: tile(0, s, False), lambda s: s, st)


def _sb_consts():
    head0 = lax.broadcasted_iota(jnp.int32, (QB, BLK), 1) < 64
    row = lax.broadcasted_iota(jnp.int32, (2 * QB, KT), 0) & (QB - 1)
    return head0, row > lax.broadcasted_iota(jnp.int32, (2 * QB, KT), 1)


def _stack_heads(v, head0):
    zero = jnp.zeros_like(v)
    return jnp.concatenate([jnp.where(head0, v, zero), jnp.where(head0, zero, v)], axis=0)


def _unstack_heads(v, head0):
    n = v.shape[0] // 2
    return jnp.where(head0, v[:n], v[n:])


def _sb_fwd(proj_a, seq, ride):
    bsz = proj_a.shape[0]
    n_ride = len(ride)

    def body(x_ref, *rest):
        ride_refs, (o_ref, o32_ref), rest = rest[:n_ride], rest[n_ride:n_ride + 2], rest[n_ride + 2:]
        gathered_refs, acc_ref, sems = rest[:n_ride], rest[n_ride], rest[n_ride + 1:]
        finish_ride = _riding_exchange(ride_refs, gathered_refs, sems, gather=True)
        head0, diag_mask = _sb_consts()
        suffix_incl = _make_suffix(True)

        def qblock(i, carry):
            r0 = pl.multiple_of(i * QB, QB)
            qs = _stack_heads(x_ref[pl.ds(r0, QB), 0:128] * jnp.asarray(SB_SCALE, BF16), head0)

            def tile(jt, later, masked):
                c0 = pl.multiple_of(jt * KT, KT)
                k = x_ref[pl.ds(c0, KT), 128:256]
                v = x_ref[pl.ds(c0, KT), 256:384]
                _, sp, a = _sb_scores(qs, k, diag_mask if masked else None, suffix_incl, later)
                pv = jnp.dot(a.astype(BF16), v, preferred_element_type=F32)
                if masked:
                    acc_ref[...] = pv
                else:
                    acc_ref[...] += pv
                return later + jnp.sum(sp, axis=1, keepdims=True)

            _sb_tiles(i, tile, jnp.zeros((2 * QB, 1), F32))
            o = _unstack_heads(acc_ref[...], head0)
            o32_ref[pl.ds(r0, QB), :] = o
            o_ref[pl.ds(r0, QB), :] = o.astype(BF16)
            return carry

        lax.fori_loop(0, seq // QB, qblock, 0)
        finish_ride()

    out_spec = pl.BlockSpec((None, seq, BLK), lambda b, hp: (b, 0, hp))
    return pl.pallas_call(
        body, name="sb_attn_fwd", grid=(bsz, 4),
        out_shape=[jax.ShapeDtypeStruct((bsz, seq, 512), BF16), jax.ShapeDtypeStruct((bsz, seq, 512), F32),
                   *_exchange_out(ride, True)],
        in_specs=[pl.BlockSpec((None, seq, 384), lambda b, hp: (b, 0, hp))] + [ANY] * n_ride,
        out_specs=[out_spec, out_spec] + [ANY] * n_ride,
        scratch_shapes=[pltpu.VMEM((2 * QB, BLK), F32), *_exchange_sems(n_ride)],
        compiler_params=_params("arbitrary", "arbitrary"),
    )(proj_a, *ride)


def _sb_bwd(proj_a, d_o, o_a, seq, ride):
    bsz = proj_a.shape[0]
    n_ride = len(ride)

    def body(x_ref, do_ref, o_ref, *rest):
        ride_refs, d_ref, rest = rest[:n_ride], rest[n_ride], rest[n_ride + 1:]
        received_refs, (dq_acc, dk_acc, dv_acc), sems = rest[:n_ride], rest[n_ride:n_ride + 3], rest[n_ride + 3:]
        finish_ride = _riding_exchange(ride_refs, received_refs, sems, gather=False)
        head0, diag_mask = _sb_consts()
        suffix_incl, suffix_excl = _make_suffix(True), _make_suffix(False)
        dk_acc[...] = jnp.zeros_like(dk_acc)
        dv_acc[...] = jnp.zeros_like(dv_acc)

        def qblock(i, carry):
            r0 = pl.multiple_of(i * QB, QB)
            qs = _stack_heads(x_ref[pl.ds(r0, QB), 0:128] * jnp.asarray(SB_SCALE, BF16), head0)
            do = do_ref[pl.ds(r0, QB), :]
            dos = _stack_heads(do, head0)
            dsum = jnp.concatenate(_head_rowsum(do.astype(F32) * o_ref[pl.ds(r0, QB), :], head0), axis=0)

            def tile(jt, st, masked):
                later, rest_g = st
                c0 = pl.multiple_of(jt * KT, KT)
                k = x_ref[pl.ds(c0, KT), 128:256]
                v = x_ref[pl.ds(c0, KT), 256:384]
                zc, sp, a = _sb_scores(qs, k, diag_mask if masked else None, suffix_incl, later)
                a16 = a.astype(BF16)
                g = a16.astype(F32) * lax.dot_general(dos, v, NT, preferred_element_type=F32)
                dz = g - jnp.exp(zc - sp) * (rest_g - suffix_excl(g))
                if masked:
                    dz = jnp.where(diag_mask, dz, 0.0)
                dz = dz.astype(BF16)
                dq = jnp.dot(dz, k, preferred_element_type=F32)
                if masked:
                    dq_acc[...] = dq
                else:
                    dq_acc[...] += dq
                dk_acc[pl.ds(c0, KT), :] += lax.dot_general(dz, qs, TN, preferred_element_type=F32)
                dv_acc[pl.ds(c0, KT), :] += lax.dot_general(a16, dos, TN, preferred_element_type=F32)
                return later + jnp.sum(sp, axis=1, keepdims=True), rest_g - jnp.sum(g, axis=1, keepdims=True)

            _sb_tiles(i, tile, (jnp.zeros((2 * QB, 1), F32), dsum))
            d_ref[pl.ds(r0, QB), 0:128] = (_unstack_heads(dq_acc[...], head0) * SB_SCALE).astype(BF16)
            return carry

        lax.fori_loop(0, seq // QB, qblock, 0)
        d_ref[:, 128:256] = dk_acc[...].astype(BF16)
        d_ref[:, 256:384] = dv_acc[...].astype(BF16)
        finish_ride()

    return pl.pallas_call(
        body, name="sb_attn_bwd", grid=(bsz, 4),
        out_shape=[jax.ShapeDtypeStruct((bsz, seq, WA), BF16), *_exchange_out(ride, False)],
        in_specs=[pl.BlockSpec((None, seq, 384), lambda b, hp: (b, 0, hp)),
                  pl.BlockSpec((None, seq, BLK), lambda b, hp: (b, 0, hp)),
                  pl.BlockSpec((None, seq, BLK), lambda b, hp: (b, 0, hp))] + [ANY] * n_ride,
        out_specs=[pl.BlockSpec((None, seq, 384), lambda b, hp: (b, 0, hp))] + [ANY] * n_ride,
        scratch_shapes=[pltpu.VMEM((2 * QB, BLK), F32), pltpu.VMEM((seq, BLK), F32), pltpu.VMEM((seq, BLK), F32),
                        *_exchange_sems(n_ride)],
        compiler_params=_params("arbitrary", "arbitrary"),
    )(proj_a, d_o, o_a, *ride)


def _rope_tables(seq):
    inv_freq = ROPE_THETA ** (-jnp.arange(32, dtype=F32) * 2.0 / 64)
    ang = jnp.arange(seq).astype(F32)[:, None] * inv_freq[None, :]
    cos, sin = jnp.cos(ang), jnp.sin(ang)
    return jnp.tile(cos, (1, 4)), jnp.concatenate([-sin, sin, -sin, sin], axis=1)


def _make_rope(n_rows):
    lane = lax.broadcasted_iota(jnp.int32, (n_rows, BLK), 1)
    first = (lane & 63) < 32

    def rope(x, cos, sin):
        partner = jnp.where(first, pltpu.roll(x, 96, 1), pltpu.roll(x, 32, 1))
        return x * cos + partner * sin

    return rope


DIL_UNROLL = 8


def _dil_consts():
    head0 = lax.broadcasted_iota(jnp.int32, (BLK, BLK), 1) < 64
    row = lax.broadcasted_iota(jnp.int32, (2 * BLK, 2 * BLK), 0) & (BLK - 1)
    col = lax.broadcasted_iota(jnp.int32, (2 * BLK, 2 * BLK), 1)
    valid_prev = jnp.logical_and(col < BLK, col >= row)
    valid_cur = jnp.logical_and(col >= BLK, row >= col - BLK)
    return head0, valid_prev, valid_cur


def _dil_blocks(dil, seq, block):
    nq = seq // dil // BLK

    def rows(r, i):
        if dil == 1:
            return pl.ds(pl.multiple_of(i * BLK, BLK), BLK)
        return pl.ds(r + (dil * BLK) * i, BLK, stride=dil)

    def step(t, carry):
        for u in range(DIL_UNROLL):
            n = t * DIL_UNROLL + u
            r, i = lax.div(n, nq), lax.rem(n, nq)
            block(rows(r, i), rows(r, jnp.maximum(i - 1, 0)), i)
        return carry

    lax.fori_loop(0, seq // BLK // DIL_UNROLL, step, 0)


def _dil_scores(qf, kf, vf, cur, prev, i, consts):
    head0, valid_prev, valid_cur = consts
    qs = _stack_heads(qf[cur, :].astype(BF16), head0)
    kcat = jnp.concatenate([kf[prev, :], kf[cur, :]], axis=0).astype(BF16)
    vcat = jnp.concatenate([vf[prev, :], vf[cur, :]], axis=0).astype(BF16)
    valid = jnp.logical_or(valid_cur, jnp.logical_and(valid_prev, i > 0))
    s = lax.dot_general(qs, kcat, NT, preferred_element_type=F32) * 0.125
    return qs, kcat, vcat, s, valid


def _head_cols(v):
    return jnp.concatenate([v[:, 0:1], v[:, 64:65]], axis=0)


def _dil_load_qkv(x_ref, c, rope, cos, sin, qf, kf, vf):
    qf[...] = rope(x_ref[:, c:c + 128].astype(F32), cos, sin).astype(BF16).astype(F32)
    kf[...] = rope(x_ref[:, c + 128:c + 256].astype(F32), cos, sin).astype(BF16).astype(F32)
    vf[...] = x_ref[:, c + 256:c + 384].astype(F32)


def _dil_fwd(proj_b, cos_t, sin_t, seq):
    bsz = proj_b.shape[0]

    def body(x_ref, cos_ref, sin_ref, ob_ref, lse_ref, qf, kf, vf, og, lg):
        consts = _dil_consts()
        head0 = consts[0]
        rope = _make_rope(seq)
        cos, sin = cos_ref[...], sin_ref[...]
        for g, dil in enumerate(DIL_GROUPS):
            _dil_load_qkv(x_ref, 384 * g, rope, cos, sin, qf, kf, vf)

            def block(cur, prev, i, g=g):
                _, _, vcat, s, valid = _dil_scores(qf, kf, vf, cur, prev, i, consts)
                s = jnp.where(valid, s, NEG_INF)
                m = jnp.max(s, axis=1, keepdims=True)
                p = jnp.exp(s - m)
                den = jnp.sum(p, axis=1, keepdims=True)
                o = jnp.dot(p.astype(BF16), vcat, preferred_element_type=F32) / den
                og[g, cur, :] = _unstack_heads(o, head0)
                lg[g, cur, :] = _unstack_heads(jnp.broadcast_to(m + jnp.log(den), (2 * BLK, BLK)), head0)

            _dil_blocks(dil, seq, block)
        ls = [lg[0], lg[1], lg[2]]
        m = jnp.maximum(jnp.maximum(ls[0], ls[1]), ls[2])
        ws = [jnp.exp(l - m) for l in ls]
        den = (ws[0] + ws[1]) + ws[2]
        ob_ref[...] = (((ws[0] * og[0] + ws[1] * og[1]) + ws[2] * og[2]) / den).astype(BF16)
        lse_ref[...] = m + jnp.log(den)

    tab_spec = pl.BlockSpec((seq, BLK), lambda b, hp: (0, 0))
    out_spec = pl.BlockSpec((None, seq, BLK), lambda b, hp: (b, 0, hp))
    slab = pltpu.VMEM((seq, BLK), F32)
    return pl.pallas_call(
        body, name="dil_attn_fwd", grid=(bsz, 2),
        out_shape=(jax.ShapeDtypeStruct((bsz, seq, 256), BF16), jax.ShapeDtypeStruct((bsz, seq, 256), F32)),
        in_specs=[pl.BlockSpec((None, seq, WB // 2), lambda b, hp: (b, 0, hp)), tab_spec, tab_spec],
        out_specs=(out_spec, out_spec),
        scratch_shapes=[slab, slab, slab, pltpu.VMEM((3, seq, BLK), F32), pltpu.VMEM((3, seq, BLK), F32)],
        compiler_params=_params("parallel", "parallel"),
    )(proj_b, cos_t, sin_t)


def _dil_bwd(proj_b, cos_t, sin_t, d_ob, o_b, lse, seq):
    bsz = proj_b.shape[0]

    def body(x_ref, cos_ref, sin_ref, do_ref, ob_ref, lse_ref, d_ref, qf, kf, vf, dof, dsf, dq_s, dk_acc, dv_acc):
        consts = _dil_consts()
        head0 = consts[0]
        rope = _make_rope(seq)
        cos, sin = cos_ref[...], sin_ref[...]
        do_all = do_ref[...].astype(F32)
        dof[...] = do_all
        head0_all = lax.broadcasted_iota(jnp.int32, (seq, BLK), 1) < 64
        d0, d1 = _head_rowsum(do_all * ob_ref[...].astype(F32), head0_all)
        dsf[...] = jnp.where(head0_all, d0, d1)
        for g, dil in enumerate(DIL_GROUPS):
            _dil_load_qkv(x_ref, 384 * g, rope, cos, sin, qf, kf, vf)
            dk_acc[...] = jnp.zeros_like(dk_acc)
            dv_acc[...] = jnp.zeros_like(dv_acc)

            def block(cur, prev, i):
                qs, kcat, vcat, s, valid = _dil_scores(qf, kf, vf, cur, prev, i, consts)
                dos = _stack_heads(dof[cur, :].astype(BF16), head0)
                p = jnp.where(valid, jnp.exp(s - _head_cols(lse_ref[cur, :])), 0.0)
                dp = lax.dot_general(dos, vcat, NT, preferred_element_type=F32)
                ds = ((p * (dp - _head_cols(dsf[cur, :]))) * 0.125).astype(BF16)
                dq_s[cur, :] = _unstack_heads(jnp.dot(ds, kcat, preferred_element_type=F32), head0)
                dk = lax.dot_general(ds, qs, TN, preferred_element_type=F32)
                dv = lax.dot_general(p.astype(BF16), dos, TN, preferred_element_type=F32)
                dk_acc[prev, :] += dk[:BLK]
                dk_acc[cur, :] += dk[BLK:]
                dv_acc[prev, :] += dv[:BLK]
                dv_acc[cur, :] += dv[BLK:]

            _dil_blocks(dil, seq, block)
            c = 384 * g
            d_ref[:, c:c + 128] = rope(dq_s[...], cos, -sin).astype(BF16)
            d_ref[:, c + 128:c + 256] = rope(dk_acc[...], cos, -sin).astype(BF16)
            d_ref[:, c + 256:c + 384] = dv_acc[...].astype(BF16)

    x_spec = pl.BlockSpec((None, seq, WB // 2), lambda b, hp: (b, 0, hp))
    tab_spec = pl.BlockSpec((seq, BLK), lambda b, hp: (0, 0))
    tok_spec = pl.BlockSpec((None, seq, BLK), lambda b, hp: (b, 0, hp))
    return pl.pallas_call(
        body, name="dil_attn_bwd", grid=(bsz, 2),
        out_shape=jax.ShapeDtypeStruct((bsz, seq, WB), BF16),
        in_specs=[x_spec, tab_spec, tab_spec, tok_spec, tok_spec, tok_spec], out_specs=x_spec,
        scratch_shapes=[pltpu.VMEM((seq, BLK), F32)] * 8,
        compiler_params=_params("parallel", "parallel"),
    )(proj_b, cos_t, sin_t, d_ob, o_b, lse)


MEM_SCALE = 128 ** -0.5
MEM_QB = 1024


def _mem_fwd(proj_a, kv, seq):
    bsz = proj_a.shape[0]

    def body(q_ref, k_ref, v_ref, o_ref):
        k, v = k_ref[...], v_ref[...]

        def qblock(i, carry):
            r0 = pl.multiple_of(i * MEM_QB, MEM_QB)
            s = lax.dot_general(q_ref[pl.ds(r0, MEM_QB), :], k, NT, preferred_element_type=F32) * MEM_SCALE
            p = jnp.exp(s - jnp.max(s, axis=1, keepdims=True))
            p = p / jnp.sum(p, axis=1, keepdims=True)
            o_ref[pl.ds(r0, MEM_QB), :] = jnp.dot(p.astype(BF16), v, preferred_element_type=F32).astype(BF16)
            return carry

        lax.fori_loop(0, seq // MEM_QB, qblock, 0)

    return pl.pallas_call(
        body, name="mem_attn_fwd", grid=(bsz, 4),
        out_shape=jax.ShapeDtypeStruct((bsz, seq, 512), BF16),
        in_specs=[pl.BlockSpec((None, seq, BLK), lambda b, h: (b, 0, 12 + h)),
                  pl.BlockSpec((None, MEM_LEN, BLK), lambda b, h: (b, 0, h)),
                  pl.BlockSpec((None, MEM_LEN, BLK), lambda b, h: (b, 0, 4 + h))],
        out_specs=pl.BlockSpec((None, seq, BLK), lambda b, h: (b, 0, h)),
        compiler_params=_params("parallel", "parallel"),
    )(proj_a, kv, kv)


def _mem_bwd(proj_a, kv, d_o, d_proj_a, seq):
    bsz = proj_a.shape[0]

    def body(q_ref, k_ref, v_ref, do_ref, _, dq_ref, dk_ref, dv_ref):
        k, v = k_ref[...], v_ref[...]

        def qblock(i, carry):
            dk, dv = carry
            r0 = pl.multiple_of(i * MEM_QB, MEM_QB)
            q, do = q_ref[pl.ds(r0, MEM_QB), :], do_ref[pl.ds(r0, MEM_QB), :]
            s = lax.dot_general(q, k, NT, preferred_element_type=F32) * MEM_SCALE
            p = jnp.exp(s - jnp.max(s, axis=1, keepdims=True))
            p = p / jnp.sum(p, axis=1, keepdims=True)
            dp = lax.dot_general(do, v, NT, preferred_element_type=F32)
            ds = ((p * (dp - jnp.sum(p * dp, axis=1, keepdims=True))) * MEM_SCALE).astype(BF16)
            dq_ref[pl.ds(r0, MEM_QB), :] = jnp.dot(ds, k, preferred_element_type=F32).astype(BF16)
            dk = dk + lax.dot_general(ds, q, TN, preferred_element_type=F32)
            dv = dv + lax.dot_general(p.astype(BF16), do, TN, preferred_element_type=F32)
            return dk, dv

        zero = jnp.zeros((MEM_LEN, BLK), F32)
        dk, dv = lax.fori_loop(0, seq // MEM_QB, qblock, (zero, zero))
        dk_ref[...] = dk.astype(BF16)
        dv_ref[...] = dv.astype(BF16)

    kv_spec = pl.BlockSpec((None, MEM_LEN, BLK), lambda b, h: (b, 0, h))
    return pl.pallas_call(
        body, name="mem_attn_bwd", grid=(bsz, 4),
        out_shape=(jax.ShapeDtypeStruct((bsz, seq, WA), BF16), jax.ShapeDtypeStruct((bsz, MEM_LEN, 512), BF16),
                   jax.ShapeDtypeStruct((bsz, MEM_LEN, 512), BF16)),
        in_specs=[pl.BlockSpec((None, seq, BLK), lambda b, h: (b, 0, 12 + h)), kv_spec,
                  pl.BlockSpec((None, MEM_LEN, BLK), lambda b, h: (b, 0, 4 + h)),
                  pl.BlockSpec((None, seq, BLK), lambda b, h: (b, 0, h)), ANY],
        out_specs=(pl.BlockSpec((None, seq, BLK), lambda b, h: (b, 0, 12 + h)), kv_spec, kv_spec),
        input_output_aliases={4: 0},
        compiler_params=_params("parallel", "parallel"),
    )(proj_a, kv, kv, d_o, d_proj_a)


def _mesh_pos():
    return lax.axis_index("x"), lax.axis_index("y"), lax.axis_index("c")


def _all_gather(shard, name):
    m_per, n = shard.shape

    def body(x_ref, out_ref, send_sems, recv_sems, local_sem):
        x, y, c = _mesh_pos()
        me, sibling = (x, y, c), (x, y, 1 - c)
        chips = [(1 - x, y), (x, 1 - y), (1 - x, 1 - y)]

        def rows(px, py, pc):
            return out_ref.at[pl.ds((4 * px + 2 * py + pc) * m_per, m_per), :]

        def copy(k, block, to, src=None):
            return pltpu.make_async_remote_copy(
                src_ref=rows(*block) if src is None else src, dst_ref=rows(*block),
                send_sem=send_sems.at[k], recv_sem=recv_sems.at[k], device_id=to, device_id_type=MESH)

        mine = pltpu.make_async_copy(x_ref, rows(*me), local_sem)
        mine.start()
        first = [copy(0, me, sibling, src=x_ref)]
        first += [copy(1 + j, me, (*chip, c), src=x_ref) for j, chip in enumerate(chips)]
        for cp in first:
            cp.start()
        passed = [copy(4 + j, (*chip, c), sibling) for j, chip in enumerate(chips)]
        for j, chip in enumerate(chips):
            copy(1 + j, (*chip, c), me).wait_recv()
            passed[j].start()
        copy(0, sibling, me).wait_recv()
        for j, chip in enumerate(chips):
            copy(4 + j, (*chip, 1 - c), me).wait_recv()
        for cp in first + passed:
            cp.wait_send()
        mine.wait()

    return pl.pallas_call(
        body, name=name, out_shape=jax.ShapeDtypeStruct((N_DEV * m_per, n), shard.dtype),
        in_specs=[ANY], out_specs=ANY,
        scratch_shapes=[pltpu.SemaphoreType.DMA((7,)), pltpu.SemaphoreType.DMA((7,)), pltpu.SemaphoreType.DMA(())],
    )(shard)


def _exchange_sems(n_arrays):
    return [pltpu.SemaphoreType.DMA((7 * n_arrays,)), pltpu.SemaphoreType.DMA((7 * n_arrays,)),
            pltpu.SemaphoreType.DMA((n_arrays,))]


def _exchange_out(srcs, gather):
    return [jax.ShapeDtypeStruct((N_DEV, *s.shape[-2:]), s.dtype) for s in srcs]


def _direct_exchange(src_refs, dst_refs, send_sems, recv_sems, local_sems, gather):
    x, y, c = _mesh_pos()
    me = 4 * x + 2 * y + c
    owns, sends, recvs = [], [], []
    for a, (src, dst) in enumerate(zip(src_refs, dst_refs)):
        owns.append(pltpu.make_async_copy(src if gather else src.at[me], dst.at[me], local_sems.at[a]))
        for j in range(1, N_DEV):
            px = 1 - x if j & 4 else x
            py = 1 - y if j & 2 else y
            pc = 1 - c if j & 1 else c
            peer = 4 * px + 2 * py + pc
            sems = dict(send_sem=send_sems.at[7 * a + j - 1], recv_sem=recv_sems.at[7 * a + j - 1],
                        device_id=(px, py, pc), device_id_type=MESH)
            sends.append(pltpu.make_async_remote_copy(
                src_ref=src if gather else src.at[peer], dst_ref=dst.at[me], **sems))
            recvs.append(pltpu.make_async_remote_copy(
                src_ref=src if gather else src.at[me], dst_ref=dst.at[peer], **sems))

    def start():
        for cp in owns + sends:
            cp.start()

    def wait():
        for cp in recvs:
            cp.wait_recv()
        for cp in sends:
            cp.wait_send()
        for cp in owns:
            cp.wait()

    return start, wait


def _riding_exchange(src_refs, dst_refs, sems, gather):
    start, wait = _direct_exchange(src_refs, dst_refs, *sems, gather)
    ids = [pl.program_id(a) for a in range(2)]
    last = [pl.num_programs(a) - 1 for a in range(2)]
    pl.when(jnp.logical_and(ids[0] == 0, ids[1] == 0))(start)
    return lambda: pl.when(jnp.logical_and(ids[0] == last[0], ids[1] == last[1]))(wait)


def _exchange(srcs, gather, name):
    n = len(srcs)

    def body(*refs):
        start, wait = _direct_exchange(refs[:n], refs[n:2 * n], *refs[2 * n:], gather=gather)
        start()
        wait()

    return pl.pallas_call(
        body, name=name, out_shape=_exchange_out(srcs, gather),
        in_specs=[ANY] * n, out_specs=[ANY] * n, scratch_shapes=_exchange_sems(n),
    )(*srcs)


def _adamw(w, g, m, v):
    m = ADAM_B1 * m + (1.0 - ADAM_B1) * g
    v = ADAM_B2 * v + (1.0 - ADAM_B2) * (g * g)
    m_hat = m / (1.0 - ADAM_B1 ** ADAM_STEP)
    v_hat = v / (1.0 - ADAM_B2 ** ADAM_STEP)
    return -ADAM_LR * (m_hat / (jnp.sqrt(v_hat) + ADAM_EPS) + ADAM_WD * w), m, v


def _reduce_adamw(recv, w, m, v, name):
    _, k, n = w.shape
    tr = max(t for t in range(16, 257, 16) if k % t == 0)

    def body(r_ref, w_ref, m_ref, v_ref, g_out, d_out, m_out, v_out):
        g = r_ref[0].astype(F32)
        for s in range(1, N_DEV):
            g = g + r_ref[s].astype(F32)
        g_out[...] = g
        d_out[...], m_out[...], v_out[...] = _adamw(w_ref[...], g, m_ref[...], v_ref[...])

    spec = pl.BlockSpec((None, tr, n), lambda i: (0, i, 0))
    return pl.pallas_call(
        body, name=name, grid=(k // tr,),
        out_shape=[jax.ShapeDtypeStruct((1, k, n), F32)] * 4,
        in_specs=[pl.BlockSpec((N_DEV, tr, n), lambda i: (0, i, 0)), spec, spec, spec],
        out_specs=[spec] * 4, compiler_params=_params("arbitrary"),
    )(recv, w, m, v)


def _small_adamw(gathered, w, m, v):
    def body(g_ref, w_ref, m_ref, v_ref, g_out, d_out, m_out, v_out, loss_out):
        tot = g_ref[0]
        for s in range(1, N_DEV):
            tot = tot + g_ref[s]
        g = tot[0:8]
        g_out[...] = g
        d_out[...], m_out[...], v_out[...] = _adamw(w_ref[...], g, m_ref[...], v_ref[...])
        loss_out[...] = jnp.broadcast_to((0.5 / D) * jnp.sum(tot[8:9], axis=1, keepdims=True), (8, BLK))

    out = [jax.ShapeDtypeStruct((8, D), F32)] * 4 + [jax.ShapeDtypeStruct((8, BLK), F32)]
    return pl.pallas_call(body, name="small_adamw", out_shape=out, compiler_params=_params())(gathered, w, m, v)


def _pick_chunks(w, chunks):
    return jnp.concatenate([w[:, BLK * c:BLK * (c + 1)] for c in chunks], axis=1)


def _whole_weight(gathered, i):
    _, k, n = gathered.shape
    if BY_ROWS[i]:
        return gathered.reshape(N_DEV * k, n)
    return gathered.transpose(1, 0, 2).reshape(k, N_DEV * n)


def _shard_parts(grad, i):
    if BY_ROWS[i]:
        return grad.reshape(N_DEV, grad.shape[0] // N_DEV, grad.shape[1])
    k, n8 = grad.shape
    return grad.reshape(k, N_DEV, n8 // N_DEV).transpose(1, 0, 2)


def kernel(x, mem, g_pre_mix, g_post_mix, g_pre_ffn, g_post_ffn, g_mem, w_in, w_mem_kv, w_br_sb, w_br_dil, w_br_mem, w_gate, b_gate, w_o, w_ffn_in, w_ffn_out, loss_target, m_g_pre_mix, m_g_post_mix, m_g_pre_ffn, m_g_post_ffn, m_g_mem, m_w_in, m_w_mem_kv, m_w_br_sb, m_w_br_dil, m_w_br_mem, m_w_gate, m_b_gate, m_w_o, m_w_ffn_in, m_w_ffn_out, v_g_pre_mix, v_g_post_mix, v_g_pre_ffn, v_g_post_ffn, v_g_mem, v_w_in, v_w_mem_kv, v_w_br_sb, v_w_br_dil, v_w_br_mem, v_w_gate, v_b_gate, v_w_o, v_w_ffn_in, v_w_ffn_out):
    bsz, seq, _ = x.shape
    tokens = bsz * seq
    xf, tgt, memf = x.reshape(tokens, D), loss_target.reshape(tokens, D), mem.reshape(bsz * MEM_LEN, D)
    big_w = [w_in, w_mem_kv, w_br_sb, w_br_dil, w_br_mem, w_gate, w_o, w_ffn_in, w_ffn_out]
    big_m = [m_w_in, m_w_mem_kv, m_w_br_sb, m_w_br_dil, m_w_br_mem, m_w_gate, m_w_o, m_w_ffn_in, m_w_ffn_out]
    big_v = [v_w_in, v_w_mem_kv, v_w_br_sb, v_w_br_dil, v_w_br_mem, v_w_gate, v_w_o, v_w_ffn_in, v_w_ffn_out]

    shards = [w[0].astype(BF16) for w in big_w]
    k_in, n_in = shards[0].shape
    fw_in = _whole_weight(_all_gather(shards[0], "weight_all_gather").reshape(N_DEV, k_in, n_in), 0)
    w_a, w_b = _pick_chunks(fw_in, CHUNKS_A), _pick_chunks(fw_in, CHUNKS_B)

    h = _norm_fwd(xf, g_pre_mix, "pre_mix_norm")
    proj_a = _matmul(h, w_a, "nn", BF16, "proj_a").reshape(bsz, seq, WA)
    proj_b = _matmul(h, w_b, "nn", BF16, "proj_b").reshape(bsz, seq, WB)
    o_a, o_a32, *behind = _sb_fwd(proj_a, seq, [shards[i] for i in GATHER_BEHIND])
    fw_mem_kv, fw_br_sb, fw_br_dil, fw_br_mem, fw_gate, fw_o, fw_ffn_in, fw_ffn_out = (
        _whole_weight(g, i) for g, i in zip(behind, GATHER_BEHIND))
    gpre = _matmul(h, fw_gate, "nn", BF16, "gate_proj")
    cos_t, sin_t = _rope_tables(seq)
    o_b, lse_b = _dil_fwd(proj_b, cos_t, sin_t, seq)
    mn = _norm_fwd(memf, g_mem, "mem_norm")
    kv = _matmul(mn, fw_mem_kv, "nn", BF16, "mem_kv_proj").reshape(bsz, MEM_LEN, D)
    o_c = _mem_fwd(proj_a, kv, seq)
    o_a2, o_b2, o_c2 = o_a.reshape(tokens, 512), o_b.reshape(tokens, 256), o_c.reshape(tokens, 512)
    ys = [_matmul(o_a2, fw_br_sb, "nn", BF16, "branch_sb"), _matmul(o_b2, fw_br_dil, "nn", BF16, "branch_dil"),
          _matmul(o_c2, fw_br_mem, "nn", BF16, "branch_mem")]
    merged = _gate_merge(gpre, ys, b_gate)
    mix, x1, h2 = _out_proj_norm(merged, fw_o, xf, g_post_mix, g_pre_ffn)
    gu_a, gu_b, f = _ffn_in_swiglu(h2, fw_ffn_in)
    dy, dfo, dg_post_ffn, loss_lanes = _ffn_out_loss(f, fw_ffn_out, x1, tgt, g_post_ffn)

    gw_ffn_out = _matmul(f, dfo, "tn", BF16, "gw_ffn_out")
    dgu = _d_ffn_swiglu_bwd(dfo, fw_ffn_out, gu_a, gu_b)
    gw_ffn_in = _matmul(h2, dgu, "tn", BF16, "gw_ffn_in")
    dx1, dmix, dg_pre_ffn, dg_post_mix = _d_h2_norm_bwd(dgu, fw_ffn_in, x1, dy, mix, g_pre_ffn, g_post_mix)
    gw_o = _matmul(merged, dmix, "tn", BF16, "gw_o")
    dya, dyb, dyc, dgpre, db_gate = _d_merged_gate_bwd(dmix, fw_o, gpre, ys, b_gate)
    d_oa = _matmul(dya, fw_br_sb, "nt", BF16, "d_o_sb").reshape(bsz, seq, 512)
    d_ob = _matmul(dyb, fw_br_dil, "nt", BF16, "d_o_dil").reshape(bsz, seq, 256)
    d_oc = _matmul(dyc, fw_br_mem, "nt", BF16, "d_o_mem").reshape(bsz, seq, 512)
    gw_br_sb = _matmul(o_a2, dya, "tn", BF16, "gw_br_sb")
    gw_br_dil = _matmul(o_b2, dyb, "tn", BF16, "gw_br_dil")
    gw_br_mem = _matmul(o_c2, dyc, "tn", BF16, "gw_br_mem")
    gw_gate = _matmul(h, dgpre, "tn", BF16, "gw_gate")
    grads = {2: gw_br_sb, 3: gw_br_dil, 4: gw_br_mem, 5: gw_gate, 6: gw_o, 7: gw_ffn_in, 8: gw_ffn_out}
    d_proj_a, *recv_behind = _sb_bwd(proj_a, d_oa, o_a32, seq, [_shard_parts(grads[i], i) for i in REDUCE_BEHIND])
    d_proj_a, dk_m, dv_m = _mem_bwd(proj_a, kv, d_oc, d_proj_a, seq)
    d_proj_b = _dil_bwd(proj_b, cos_t, sin_t, d_ob, o_b, lse_b, seq).reshape(tokens, WB)
    d_proj_a = d_proj_a.reshape(tokens, WA)
    gw_a = _matmul(h, d_proj_a, "tn", BF16, "gw_in_a")
    gw_b = _matmul(h, d_proj_b, "tn", BF16, "gw_in_b")
    dkv = jnp.concatenate([dk_m, dv_m], axis=-1).reshape(bsz * MEM_LEN, D)
    gw_mem_kv = _matmul(mn, dkv, "tn", BF16, "gw_mem_kv")
    dmn = _matmul(dkv, fw_mem_kv, "nt", F32, "d_mem_norm")
    dg_mem = _gain_grad(dmn, memf)
    gw_ab = jnp.concatenate([gw_a, gw_b], axis=1)
    where = {c: i for i, c in enumerate(CHUNKS_A + CHUNKS_B)}
    grads = {0: _pick_chunks(gw_ab, [where[c] for c in range(34)]), 1: gw_mem_kv}
    dx, dg_pre_mix, *recv_last = _d_h_norm_bwd(
        [(dgpre, fw_gate), (d_proj_a, w_a), (d_proj_b, w_b)], xf, dx1, g_pre_mix,
        [_shard_parts(grads[i], i) for i in REDUCE_LAST])

    received = dict(zip(REDUCE_BEHIND + REDUCE_LAST, [*recv_behind, *recv_last]))
    adam = [_reduce_adamw(received[i], big_w[i], big_m[i], big_v[i], "reduce_adamw_" + BIG_NAMES[i])
            for i in range(len(big_w))]
    big = [[a[k] for a in adam] for k in range(4)]

    small = jnp.concatenate([dg_pre_mix, dg_post_mix, dg_pre_ffn, dg_post_ffn, dg_mem, db_gate.reshape(3, D),
                             loss_lanes, jnp.zeros((7, D), F32)], axis=0)
    small_all, = _exchange([small], True, "small_all_gather")

    def small_pack(gs, b):
        return jnp.concatenate([*gs, b.reshape(3, D)], axis=0)

    sm = _small_adamw(
        small_all, small_pack([g_pre_mix, g_post_mix, g_pre_ffn, g_post_ffn, g_mem], b_gate),
        small_pack([m_g_pre_mix, m_g_post_mix, m_g_pre_ffn, m_g_post_ffn, m_g_mem], m_b_gate),
        small_pack([v_g_pre_mix, v_g_post_mix, v_g_pre_ffn, v_g_post_ffn, v_g_mem], v_b_gate))
    loss = sm[4][0, 0]

    def leaves(k):
        t, bw = sm[k], big[k]
        return [t[0:1], t[1:2], t[2:3], t[3:4], t[4:5], *bw[0:6], t[5:8].reshape(1, 3 * D), *bw[6:9]]

    return (loss, dx.reshape(bsz, seq, D), *leaves(0), *leaves(1), *leaves(2), *leaves(3))
```

```python
import functools

import jax
import jax.numpy as jnp
from jax import lax
from jax.experimental import pallas as pl
from jax.experimental.pallas import tpu as pltpu

F32 = jnp.float32
BF16 = jnp.bfloat16
D = 1024
BLK = 128
MEM_LEN = 256
D_FF = 2816
NORM_EPS = 1e-6
NEG_INF = -1e30
ROPE_THETA = 10000.0
ADAM_LR, ADAM_B1, ADAM_B2, ADAM_EPS, ADAM_WD, ADAM_STEP = 0.001, 0.9, 0.999, 1e-08, 0.01, 10
N_DEV = 8
VMEM_LIMIT_BYTES = 56 * 1024 * 1024
MESH = pl.DeviceIdType.MESH
ANY = pl.BlockSpec(memory_space=pl.ANY)

NT = (((1,), (1,)), ((), ()))
TN = (((0,), (0,)), ((), ()))
NN = (((1,), (0,)), ((), ()))
_DIMS = {"nn": NN, "nt": NT, "tn": TN}

BIG_NAMES = ("w_in", "w_mem_kv", "w_br_sb", "w_br_dil", "w_br_mem", "w_gate", "w_o", "w_ffn_in", "w_ffn_out")
BY_ROWS = (False, True, False, False, False, False, True, False, True)
GATHER_FIRST = (0,)
GATHER_BEHIND = (1, 2, 3, 4, 5, 6, 7, 8)
REDUCE_BEHIND = (2, 3, 4, 5, 6, 7, 8)
REDUCE_LAST = (0, 1)

CHUNKS_A = tuple(c for hp in range(4) for c in (hp, 4 + hp, 8 + hp)) + (30, 31, 32, 33)
CHUNKS_B = tuple(c for hp in range(2) for g in range(3) for c in (12 + 6 * g + hp, 14 + 6 * g + hp, 16 + 6 * g + hp))
WA, WB = 128 * len(CHUNKS_A), 128 * len(CHUNKS_B)
DIL_GROUPS = (1, 4, 16)


def _params(*sem):
    return pltpu.CompilerParams(dimension_semantics=sem or None, vmem_limit_bytes=VMEM_LIMIT_BYTES)


def _tile(n, cap):
    if n <= 128:
        return n
    assert n % 128 == 0, n
    best = 128
    for t in range(128, min(n, cap) + 1, 128):
        if n % t == 0:
            best = t
    return best


def _k_steps(k, nk, step):
    if nk == 1:
        step(True, True)
        return
    pl.when(k == 0)(functools.partial(step, True, False))
    if nk > 2:
        pl.when(jnp.logical_and(k > 0, k < nk - 1))(functools.partial(step, False, False))
    pl.when(k == nk - 1)(functools.partial(step, False, True))


def _matmul(a, b, mode, out_dtype, name, tm_cap=1536, tn_cap=1536, tk_cap=1536):
    if mode == "tn":
        (K, M), N = a.shape, b.shape[1]
    elif mode == "nt":
        (M, K), N = a.shape, b.shape[0]
    else:
        (M, K), N = a.shape, b.shape[1]
    tm, tn, tk = _tile(M, tm_cap), _tile(N, tn_cap), _tile(K, tk_cap)
    nm, nn, nk = M // tm, N // tn, K // tk
    dims = _DIMS[mode]

    def body(a_ref, b_ref, o_ref, *acc):
        def step(first, last):
            d = lax.dot_general(a_ref[...], b_ref[...], dims, preferred_element_type=F32)
            if not first:
                d = d + acc[0][...]
            if last:
                o_ref[...] = d.astype(o_ref.dtype)
            else:
                acc[0][...] = d

        _k_steps(pl.program_id(2), nk, step)

    n_outer = nk == 1 and (a.size * nn + b.size) < (a.size + b.size * nm)
    if n_outer:
        grid, ij = (nn, nm, nk), (lambda g0, g1: (g1, g0))
    else:
        grid, ij = (nm, nn, nk), (lambda g0, g1: (g0, g1))
    if mode == "tn":
        a_spec = pl.BlockSpec((tk, tm), lambda g0, g1, k: (k, ij(g0, g1)[0]))
    else:
        a_spec = pl.BlockSpec((tm, tk), lambda g0, g1, k: (ij(g0, g1)[0], k))
    if mode == "nt":
        b_spec = pl.BlockSpec((tn, tk), lambda g0, g1, k: (ij(g0, g1)[1], k))
    else:
        b_spec = pl.BlockSpec((tk, tn), lambda g0, g1, k: (k, ij(g0, g1)[1]))
    return pl.pallas_call(
        body, name=name, grid=grid,
        out_shape=jax.ShapeDtypeStruct((M, N), out_dtype),
        in_specs=[a_spec, b_spec],
        out_specs=pl.BlockSpec((tm, tn), lambda g0, g1, k: ij(g0, g1)),
        scratch_shapes=[pltpu.VMEM((tm, tn), F32)] if nk > 1 else [],
        compiler_params=_params("parallel", "parallel", "arbitrary"),
    )(a, b)


def _rowwise(body, name, rows, tr, row_ins, vec_ins, row_outs, acc_outs=()):
    tr = min(tr, rows)
    assert rows % tr == 0
    in_specs, args = [], []
    for r in row_ins:
        arr, w, cb = r if isinstance(r, tuple) else (r, r.shape[1], 0)
        in_specs.append(pl.BlockSpec((tr, w), functools.partial(lambda i, cb: (i, cb), cb=cb)))
        args.append(arr)
    for v in vec_ins:
        in_specs.append(pl.BlockSpec(v.shape, lambda i: (0, 0)))
        args.append(v)
    out_shape = [jax.ShapeDtypeStruct((rows, w), dt) for w, dt in row_outs]
    out_shape += [jax.ShapeDtypeStruct((1, w), F32) for w in acc_outs]
    out_specs = [pl.BlockSpec((tr, w), lambda i: (i, 0)) for w, _ in row_outs]
    out_specs += [pl.BlockSpec((1, w), lambda i: (0, 0)) for w in acc_outs]
    n_acc = len(acc_outs)

    def wrapped(*refs):
        if n_acc:
            @pl.when(pl.program_id(0) == 0)
            def _():
                for r in refs[len(refs) - n_acc:]:
                    r[...] = jnp.zeros_like(r)
        body(*refs)

    return pl.pallas_call(
        wrapped, name=name, grid=(rows // tr,), out_shape=out_shape, in_specs=in_specs, out_specs=out_specs,
        compiler_params=_params("arbitrary"),
    )(*args)


def _rstd(x):
    return lax.rsqrt(jnp.mean(x * x, axis=-1, keepdims=True) + NORM_EPS)


def _norm_bwd(u, n, r):
    return r * (u - n * jnp.mean(u * n, axis=-1, keepdims=True))


def _colsum(v):
    return jnp.sum(v, axis=0, keepdims=True)


def _norm_fwd(x, g, name):
    def body(x_ref, g_ref, h_ref):
        xv = x_ref[...]
        h_ref[...] = ((xv * _rstd(xv)) * g_ref[...]).astype(BF16)

    return _rowwise(body, name, x.shape[0], 512, [x], [g], [(D, BF16)])[0]


def _matmul_rows(pairs, mode, name, epilogue, row_ins=(), vec_ins=(), row_outs=(), acc_outs=(), ride=None,
                 tm=512, tk_cap=1536, epi_rows=None):
    M = pairs[0][0].shape[0]
    N = pairs[0][1].shape[1] if mode == "nn" else pairs[0][1].shape[0]
    tm = min(tm, M)
    tks = [_tile(a.shape[1], tk_cap) for a, _ in pairs]
    nks = [a.shape[1] // tk for (a, _), tk in zip(pairs, tks)]
    offs = [sum(nks[:p]) for p in range(len(pairs))]
    nm, nk = M // tm, sum(nks)
    dims = _DIMS[mode]
    n_ab, n_extra, n_out = 2 * len(pairs), len(row_ins) + len(vec_ins), len(row_outs) + len(acc_outs)
    n_ride = 0 if ride is None else len(ride)

    def body(*refs):
        ab, extra, rest = refs[:n_ab], refs[n_ab:n_ab + n_extra], refs[n_ab + n_extra:]
        ride_refs, outs, rest = rest[:n_ride], rest[n_ride:n_ride + n_out], rest[n_ride + n_out:]
        received_refs, rest = rest[:n_ride], rest[n_ride:]
        if n_ride:
            finish_ride = _riding_exchange(ride_refs, received_refs, rest[len(rest) - 3:], gather=False)
        i, k = pl.program_id(0), pl.program_id(1)
        if acc_outs:
            @pl.when(jnp.logical_and(i == 0, k == 0))
            def _():
                for r in outs[len(row_outs):]:
                    r[...] = jnp.zeros_like(r)

        def step(p, first, last):
            d = lax.dot_general(ab[2 * p][...], ab[2 * p + 1][...], dims, preferred_element_type=F32)
            if not first:
                d = d + rest[0][...]
            if not last:
                rest[0][...] = d
            elif epi_rows is None:
                epilogue(d, *extra, *outs)
            else:
                rest[0][...] = d
                for c in range(tm // epi_rows):
                    rows = pl.ds(c * epi_rows, epi_rows)
                    sliced = [r.at[rows] for r in extra[:len(row_ins)]] + list(extra[len(row_ins):])
                    sliced += [r.at[rows] for r in outs[:len(row_outs)]] + list(outs[len(row_outs):])
                    epilogue(rest[0][rows, :], *sliced)

        last_p = len(pairs) - 1
        if nk == 1:
            step(0, True, True)
        else:
            pl.when(k == 0)(functools.partial(step, 0, True, False))
            for p in range(len(pairs)):
                lo, hi = max(offs[p], 1), min(offs[p] + nks[p], nk - 1)
                if hi > lo:
                    pl.when(jnp.logical_and(k >= lo, k < hi))(functools.partial(step, p, False, False))
            pl.when(k == nk - 1)(functools.partial(step, last_p, False, True))
        if n_ride:
            finish_ride()

    in_specs, args = [], []
    for p, ((a, b), tk) in enumerate(zip(pairs, tks)):
        step = functools.partial(lambda k, p: jnp.clip(k - offs[p], 0, nks[p] - 1), p=p)
        in_specs.append(pl.BlockSpec((tm, tk), functools.partial(lambda i, k, step: (i, step(k)), step=step)))
        if mode == "nn":
            in_specs.append(pl.BlockSpec((tk, N), functools.partial(lambda i, k, step: (step(k), 0), step=step)))
        else:
            in_specs.append(pl.BlockSpec((N, tk), functools.partial(lambda i, k, step: (0, step(k)), step=step)))
        args += [a, b]
    in_specs += [pl.BlockSpec((tm, r.shape[1]), lambda i, k: (i, 0)) for r in row_ins]
    in_specs += [pl.BlockSpec(v.shape, lambda i, k: (0, 0)) for v in vec_ins]
    in_specs += [ANY] * n_ride
    out_shape = [jax.ShapeDtypeStruct((M, w), dt) for w, dt in row_outs]
    out_shape += [jax.ShapeDtypeStruct((1, w), F32) for w in acc_outs]
    out_specs = [pl.BlockSpec((tm, w), lambda i, k: (i, 0)) for w, _ in row_outs]
    out_specs += [pl.BlockSpec((1, w), lambda i, k: (0, 0)) for w in acc_outs]
    scratch = [pltpu.VMEM((tm, N), F32)] if nk > 1 else []
    if n_ride:
        out_shape += _exchange_out(ride, False)
        out_specs += [ANY] * n_ride
        scratch += _exchange_sems(n_ride)
    return pl.pallas_call(
        body, name=name, grid=(nm, nk), out_shape=out_shape, in_specs=in_specs, out_specs=out_specs,
        scratch_shapes=scratch, compiler_params=_params("arbitrary", "arbitrary"),
    )(*args, *row_ins, *vec_ins, *(ride or []))


def _merge_out_proj_norm(gpre, ys, b_gate, w_o, x, g_post, g_pre):
    tokens = x.shape[0]
    tm = min(512, tokens)

    def body(gp_ref, ya_ref, yb_ref, yc_ref, b_ref, w_ref, x_ref, g2_ref, g3_ref, m_ref, mix_ref, x1_ref, h2_ref):
        acc = None
        for k, y_ref in enumerate((ya_ref, yb_ref, yc_ref)):
            cols = slice(k * D, (k + 1) * D)
            gate = jax.nn.sigmoid(gp_ref[:, cols].astype(F32) + b_ref[:, cols])
            term = gate * y_ref[...].astype(F32)
            acc = term if acc is None else acc + term
        merged = acc.astype(BF16)
        m_ref[...] = merged
        mv = jnp.dot(merged, w_ref[...], preferred_element_type=F32)
        mix_ref[...] = mv
        x1 = x_ref[...] + (mv * _rstd(mv)) * g2_ref[...]
        x1_ref[...] = x1
        h2_ref[...] = ((x1 * _rstd(x1)) * g3_ref[...]).astype(BF16)

    def rows(w):
        return pl.BlockSpec((tm, w), lambda i: (i, 0))

    def whole(a):
        return pl.BlockSpec(a.shape, lambda i: (0, 0))

    return pl.pallas_call(
        body, name="merge_out_proj_norm", grid=(tokens // tm,),
        out_shape=[jax.ShapeDtypeStruct((tokens, D), dt) for dt in (BF16, F32, F32, BF16)],
        in_specs=[rows(3 * D), rows(D), rows(D), rows(D), whole(b_gate), whole(w_o), rows(D), whole(g_post),
                  whole(g_pre)],
        out_specs=[rows(D)] * 4, compiler_params=_params("parallel"),
    )(gpre, *ys, b_gate, w_o, x, g_post, g_pre)


def _ffn_in_swiglu(h2, w_ffn_in):
    tokens = h2.shape[0]
    tm, tn = min(512, tokens), _tile(D_FF, 1536)
    nj = D_FF // tn

    def body(h_ref, wa_ref, wb_ref, a_ref, b_ref, f_ref):
        hv = h_ref[...]
        a = jnp.dot(hv, wa_ref[...], preferred_element_type=F32)
        b = jnp.dot(hv, wb_ref[...], preferred_element_type=F32)
        a_ref[...] = a.astype(BF16)
        b_ref[...] = b.astype(BF16)
        f_ref[...] = (a * jax.nn.sigmoid(a) * b).astype(BF16)

    out = jax.ShapeDtypeStruct((tokens, D_FF), BF16)
    o_spec = pl.BlockSpec((tm, tn), lambda j, i: (i, j))
    return pl.pallas_call(
        body, name="ffn_in_swiglu", grid=(nj, tokens // tm), out_shape=(out, out, out),
        in_specs=[pl.BlockSpec((tm, D), lambda j, i: (i, 0)), pl.BlockSpec((D, tn), lambda j, i: (0, j)),
                  pl.BlockSpec((D, tn), lambda j, i: (0, j + nj))],
        out_specs=(o_spec, o_spec, o_spec), compiler_params=_params("parallel", "parallel"),
    )(h2, w_ffn_in, w_ffn_in)


def _ffn_out_loss(f, w_ffn_out, x1, tgt, g_post):
    def epilogue(fo_v, x1_ref, t_ref, g_ref, dy_ref, dfo_ref, dg_ref, loss_ref):
        r = _rstd(fo_v)
        n = fo_v * r
        err = (x1_ref[...] + n * g_ref[...]) - t_ref[...]
        loss_ref[...] += _colsum(err * err)
        dy = err * (1.0 / D)
        dy_ref[...] = dy
        dg_ref[...] += _colsum(dy * n)
        dfo_ref[...] = _norm_bwd(dy * g_ref[...], n, r).astype(BF16)

    return _matmul_rows([(f, w_ffn_out)], "nn", "ffn_out_loss", epilogue, [x1, tgt], [g_post],
                        [(D, F32), (D, BF16)], (D, D))


def _d_ffn_swiglu_bwd(dfo, w_ffn_out, gu_a, gu_b):
    def epilogue(d, a_ref, b_ref, dgu_ref):
        a = a_ref[...].astype(F32)
        b = b_ref[...].astype(F32)
        s = jax.nn.sigmoid(a)
        dgu_ref[:, :D_FF] = (d * b * (s * (1.0 + a * (1.0 - s)))).astype(BF16)
        dgu_ref[:, D_FF:] = (d * (a * s)).astype(BF16)

    return _matmul_rows([(dfo, w_ffn_out)], "nt", "d_ffn_swiglu_bwd", epilogue, [gu_a, gu_b], [],
                        [(2 * D_FF, BF16)], tm=256)[0]


def _d_h2_norm_bwd(dgu, w_ffn_in, x1, dy, mix, g_pre, g_post):
    def epilogue(dh, x1_ref, dy_ref, mix_ref, g3_ref, g2_ref, dx1_ref, dmix_ref, dg3_ref, dg2_ref):
        x1v = x1_ref[...]
        r3 = _rstd(x1v)
        n3 = x1v * r3
        dg3_ref[...] += _colsum(dh * n3)
        dx1 = dy_ref[...] + _norm_bwd(dh * g3_ref[...], n3, r3)
        dx1_ref[...] = dx1
        mv = mix_ref[...]
        r2 = _rstd(mv)
        n2 = mv * r2
        dg2_ref[...] += _colsum(dx1 * n2)
        dmix_ref[...] = _norm_bwd(dx1 * g2_ref[...], n2, r2).astype(BF16)

    return _matmul_rows([(dgu, w_ffn_in)], "nt", "d_h2_norm_bwd", epilogue, [x1, dy, mix], [g_pre, g_post],
                        [(D, F32), (D, BF16)], (D, D))


def _d_merged_gate_bwd(dmix, w_o, gpre, ys, b_gate):
    def epilogue(dm, gp_ref, ya_ref, yb_ref, yc_ref, b_ref, dya_ref, dyb_ref, dyc_ref, dgp_ref, db_ref):
        for k, (y_ref, dy_ref) in enumerate(((ya_ref, dya_ref), (yb_ref, dyb_ref), (yc_ref, dyc_ref))):
            cols = slice(k * D, (k + 1) * D)
            gate = jax.nn.sigmoid(gp_ref[:, cols].astype(F32) + b_ref[:, cols])
            dy_ref[...] = (dm * gate).astype(BF16)
            dgp = (dm * y_ref[...].astype(F32)) * (gate * (1.0 - gate))
            dgp_ref[:, cols] = dgp.astype(BF16)
            db_ref[:, cols] += _colsum(dgp)

    return _matmul_rows([(dmix, w_o)], "nt", "d_merged_gate_bwd", epilogue, [gpre, *ys], [b_gate],
                        [(D, BF16), (D, BF16), (D, BF16), (3 * D, BF16)], (3 * D,))


def _d_h_norm_bwd(pairs, x, dx1, g_pre, ride):
    def epilogue(dh, x_ref, dx1_ref, g_ref, dx_ref, dg_ref):
        xv = x_ref[...]
        r = _rstd(xv)
        n = xv * r
        dg_ref[...] += _colsum(dh * n)
        dx_ref[...] = dx1_ref[...] + _norm_bwd(dh * g_ref[...], n, r)

    return _matmul_rows(pairs, "nt", "d_h_norm_bwd", epilogue, [x, dx1], [g_pre], [(D, F32)], (D,), ride=ride,
                        tm=1024, tk_cap=512, epi_rows=256)


def _gain_grad(dmn, mem):
    def body(d_ref, m_ref, dg_ref):
        mv = m_ref[...]
        dg_ref[...] += _colsum(d_ref[...] * (mv * _rstd(mv)))

    return _rowwise(body, "mem_gain_grad", mem.shape[0], 256, [dmn, mem], [], [], (D,))[0]


def _head_rowsum(v, head0):
    return (jnp.sum(jnp.where(head0, v, 0.0), axis=1, keepdims=True),
            jnp.sum(jnp.where(head0, 0.0, v), axis=1, keepdims=True))


KT = 256
SB_SCALE = 0.125


def _make_suffix(inclusive):
    row, col = lax.broadcasted_iota(jnp.int32, (KT, KT), 0), lax.broadcasted_iota(jnp.int32, (KT, KT), 1)
    tri = (row >= col if inclusive else row > col).astype(BF16)
    tri2 = jnp.concatenate([tri, tri], axis=0)

    def suffix(x):
        hi = x.astype(BF16)
        lo = (x - hi.astype(F32)).astype(BF16)
        return jnp.dot(jnp.concatenate([hi, lo], axis=1), tri2, preferred_element_type=F32)

    return suffix


def _sb_scores(qh, k, mask, suffix_incl, later):
    z = lax.dot_general(qh, k, NT, preferred_element_type=F32)
    zc = jnp.minimum(z, 60.0)
    sp = jnp.log(1.0 + jnp.exp(zc))
    if mask is not None:
        sp = jnp.where(mask, sp, 0.0)
    a = jnp.exp((zc - suffix_incl(sp)) - later)
    if mask is not None:
        a = jnp.where(mask, a, 0.0)
    return zc, sp, a


QB = KT


def _sb_tiles(i, tile, init):
    st = lax.cond(i > 0, lambda s: tile(i - 1, tile(i, s, True), False), lambda s: tile(i, s, True), init)
    rest = jnp.maximum(i - 1, 0)
    st = lax.fori_loop(0, lax.shift_right_logical(rest, 1),
                       lambda t, s: tile(rest - 2 - 2 * t, tile(rest - 1 - 2 * t, s, False), False), st)
    return lax.cond((rest & 1) == 1, lambda s: tile(0, s, False), lambda s: s, st)


def _sb_consts():
    head0 = lax.broadcasted_iota(jnp.int32, (QB, BLK), 1) < 64
    row = lax.broadcasted_iota(jnp.int32, (2 * QB, KT), 0) & (QB - 1)
    return head0, row > lax.broadcasted_iota(jnp.int32, (2 * QB, KT), 1)


def _stack_heads(v, head0):
    zero = jnp.zeros_like(v)
    return jnp.concatenate([jnp.where(head0, v, zero), jnp.where(head0, zero, v)], axis=0)


def _unstack_heads(v, head0):
    n = v.shape[0] // 2
    return jnp.where(head0, v[:n], v[n:])


def _sb_fwd(proj_a, seq, ride):
    bsz = proj_a.shape[0]
    n_ride = len(ride)

    def body(x_ref, *rest):
        ride_refs, (o_ref, o32_ref), rest = rest[:n_ride], rest[n_ride:n_ride + 2], rest[n_ride + 2:]
        gathered_refs, acc_ref, sems = rest[:n_ride], rest[n_ride], rest[n_ride + 1:]
        finish_ride = _riding_exchange(ride_refs, gathered_refs, sems, gather=True)
        head0, diag_mask = _sb_consts()
        suffix_incl = _make_suffix(True)

        def qblock(i, carry):
            r0 = pl.multiple_of(i * QB, QB)
            qs = _stack_heads(x_ref[pl.ds(r0, QB), 0:128] * jnp.asarray(SB_SCALE, BF16), head0)

            def tile(jt, later, masked):
                c0 = pl.multiple_of(jt * KT, KT)
                k = x_ref[pl.ds(c0, KT), 128:256]
                v = x_ref[pl.ds(c0, KT), 256:384]
                _, sp, a = _sb_scores(qs, k, diag_mask if masked else None, suffix_incl, later)
                pv = jnp.dot(a.astype(BF16), v, preferred_element_type=F32)
                if masked:
                    acc_ref[...] = pv
                else:
                    acc_ref[...] += pv
                return later + jnp.sum(sp, axis=1, keepdims=True)

            _sb_tiles(i, tile, jnp.zeros((2 * QB, 1), F32))
            o = _unstack_heads(acc_ref[...], head0)
            o32_ref[pl.ds(r0, QB), :] = o
            o_ref[pl.ds(r0, QB), :] = o.astype(BF16)
            return carry

        lax.fori_loop(0, seq // QB, qblock, 0)
        finish_ride()

    out_spec = pl.BlockSpec((None, seq, BLK), lambda b, hp: (b, 0, hp))
    return pl.pallas_call(
        body, name="sb_attn_fwd", grid=(bsz, 4),
        out_shape=[jax.ShapeDtypeStruct((bsz, seq, 512), BF16), jax.ShapeDtypeStruct((bsz, seq, 512), F32),
                   *_exchange_out(ride, True)],
        in_specs=[pl.BlockSpec((None, seq, 384), lambda b, hp: (b, 0, hp))] + [ANY] * n_ride,
        out_specs=[out_spec, out_spec] + [ANY] * n_ride,
        scratch_shapes=[pltpu.VMEM((2 * QB, BLK), F32), *_exchange_sems(n_ride)],
        compiler_params=_params("arbitrary", "arbitrary"),
    )(proj_a, *ride)


def _sb_bwd(proj_a, d_o, o_a, seq, ride):
    bsz = proj_a.shape[0]
    n_ride = len(ride)

    def body(x_ref, do_ref, o_ref, *rest):
        ride_refs, d_ref, rest = rest[:n_ride], rest[n_ride], rest[n_ride + 1:]
        received_refs, (dq_acc, dk_acc, dv_acc), sems = rest[:n_ride], rest[n_ride:n_ride + 3], rest[n_ride + 3:]
        finish_ride = _riding_exchange(ride_refs, received_refs, sems, gather=False)
        head0, diag_mask = _sb_consts()
        suffix_incl, suffix_excl = _make_suffix(True), _make_suffix(False)
        dk_acc[...] = jnp.zeros_like(dk_acc)
        dv_acc[...] = jnp.zeros_like(dv_acc)

        def qblock(i, carry):
            r0 = pl.multiple_of(i * QB, QB)
            qs = _stack_heads(x_ref[pl.ds(r0, QB), 0:128] * jnp.asarray(SB_SCALE, BF16), head0)
            do = do_ref[pl.ds(r0, QB), :]
            dos = _stack_heads(do, head0)
            dsum = jnp.concatenate(_head_rowsum(do.astype(F32) * o_ref[pl.ds(r0, QB), :], head0), axis=0)

            def tile(jt, st, masked):
                later, rest_g = st
                c0 = pl.multiple_of(jt * KT, KT)
                k = x_ref[pl.ds(c0, KT), 128:256]
                v = x_ref[pl.ds(c0, KT), 256:384]
                zc, sp, a = _sb_scores(qs, k, diag_mask if masked else None, suffix_incl, later)
                a16 = a.astype(BF16)
                g = a16.astype(F32) * lax.dot_general(dos, v, NT, preferred_element_type=F32)
                dz = g - jnp.exp(zc - sp) * (rest_g - suffix_excl(g))
                if masked:
                    dz = jnp.where(diag_mask, dz, 0.0)
                dz = dz.astype(BF16)
                dq = jnp.dot(dz, k, preferred_element_type=F32)
                if masked:
                    dq_acc[...] = dq
                else:
                    dq_acc[...] += dq
                dk_acc[pl.ds(c0, KT), :] += lax.dot_general(dz, qs, TN, preferred_element_type=F32)
                dv_acc[pl.ds(c0, KT), :] += lax.dot_general(a16, dos, TN, preferred_element_type=F32)
                return later + jnp.sum(sp, axis=1, keepdims=True), rest_g - jnp.sum(g, axis=1, keepdims=True)

            _sb_tiles(i, tile, (jnp.zeros((2 * QB, 1), F32), dsum))
            d_ref[pl.ds(r0, QB), 0:128] = (_unstack_heads(dq_acc[...], head0) * SB_SCALE).astype(BF16)
            return carry

        lax.fori_loop(0, seq // QB, qblock, 0)
        d_ref[:, 128:256] = dk_acc[...].astype(BF16)
        d_ref[:, 256:384] = dv_acc[...].astype(BF16)
        finish_ride()

    return pl.pallas_call(
        body, name="sb_attn_bwd", grid=(bsz, 4),
        out_shape=[jax.ShapeDtypeStruct((bsz, seq, WA), BF16), *_exchange_out(ride, False)],
        in_specs=[pl.BlockSpec((None, seq, 384), lambda b, hp: (b, 0, hp)),
                  pl.BlockSpec((None, seq, BLK), lambda b, hp: (b, 0, hp)),
                  pl.BlockSpec((None, seq, BLK), lambda b, hp: (b, 0, hp))] + [ANY] * n_ride,
        out_specs=[pl.BlockSpec((None, seq, 384), lambda b, hp: (b, 0, hp))] + [ANY] * n_ride,
        scratch_shapes=[pltpu.VMEM((2 * QB, BLK), F32), pltpu.VMEM((seq, BLK), F32), pltpu.VMEM((seq, BLK), F32),
                        *_exchange_sems(n_ride)],
        compiler_params=_params("arbitrary", "arbitrary"),
    )(proj_a, d_o, o_a, *ride)


def _rope_tables(seq):
    inv_freq = ROPE_THETA ** (-jnp.arange(32, dtype=F32) * 2.0 / 64)
    ang = jnp.arange(seq).astype(F32)[:, None] * inv_freq[None, :]
    cos, sin = jnp.cos(ang), jnp.sin(ang)
    return jnp.tile(cos, (1, 4)), jnp.concatenate([-sin, sin, -sin, sin], axis=1)


def _make_rope(n_rows):
    lane = lax.broadcasted_iota(jnp.int32, (n_rows, BLK), 1)
    first = (lane & 63) < 32

    def rope(x, cos, sin):
        partner = jnp.where(first, pltpu.roll(x, 96, 1), pltpu.roll(x, 32, 1))
        return x * cos + partner * sin

    return rope


DIL_UNROLL = 8


def _dil_consts():
    head0 = lax.broadcasted_iota(jnp.int32, (BLK, BLK), 1) < 64
    row = lax.broadcasted_iota(jnp.int32, (2 * BLK, 2 * BLK), 0) & (BLK - 1)
    col = lax.broadcasted_iota(jnp.int32, (2 * BLK, 2 * BLK), 1)
    valid_prev = jnp.logical_and(col < BLK, col >= row)
    valid_cur = jnp.logical_and(col >= BLK, row >= col - BLK)
    return head0, valid_prev, valid_cur


def _dil_blocks(dil, seq, block):
    nq = seq // dil // BLK

    def rows(r, i):
        if dil == 1:
            return pl.ds(pl.multiple_of(i * BLK, BLK), BLK)
        return pl.ds(r + (dil * BLK) * i, BLK, stride=dil)

    def step(t, carry):
        for u in range(DIL_UNROLL):
            n = t * DIL_UNROLL + u
            r, i = lax.div(n, nq), lax.rem(n, nq)
            block(rows(r, i), rows(r, jnp.maximum(i - 1, 0)), i)
        return carry

    lax.fori_loop(0, seq // BLK // DIL_UNROLL, step, 0)


def _dil_scores(qf, kf, vf, cur, prev, i, consts):
    head0, valid_prev, valid_cur = consts
    qs = _stack_heads(qf[cur, :].astype(BF16), head0)
    kcat = jnp.concatenate([kf[prev, :], kf[cur, :]], axis=0).astype(BF16)
    vcat = jnp.concatenate([vf[prev, :], vf[cur, :]], axis=0).astype(BF16)
    valid = jnp.logical_or(valid_cur, jnp.logical_and(valid_prev, i > 0))
    s = lax.dot_general(qs, kcat, NT, preferred_element_type=F32) * 0.125
    return qs, kcat, vcat, s, valid


def _head_cols(v):
    return jnp.concatenate([v[:, 0:1], v[:, 64:65]], axis=0)


def _dil_load_qkv(x_ref, c, rope, cos, sin, qf, kf, vf):
    qf[...] = rope(x_ref[:, c:c + 128].astype(F32), cos, sin).astype(BF16).astype(F32)
    kf[...] = rope(x_ref[:, c + 128:c + 256].astype(F32), cos, sin).astype(BF16).astype(F32)
    vf[...] = x_ref[:, c + 256:c + 384].astype(F32)


def _dil_fwd(proj_b, cos_t, sin_t, seq):
    bsz = proj_b.shape[0]

    def body(x_ref, cos_ref, sin_ref, ob_ref, lse_ref, qf, kf, vf, og, lg):
        consts = _dil_consts()
        head0 = consts[0]
        rope = _make_rope(seq)
        cos, sin = cos_ref[...], sin_ref[...]
        for g, dil in enumerate(DIL_GROUPS):
            _dil_load_qkv(x_ref, 384 * g, rope, cos, sin, qf, kf, vf)

            def block(cur, prev, i, g=g):
                _, _, vcat, s, valid = _dil_scores(qf, kf, vf, cur, prev, i, consts)
                s = jnp.where(valid, s, NEG_INF)
                m = jnp.max(s, axis=1, keepdims=True)
                p = jnp.exp(s - m)
                den = jnp.sum(p, axis=1, keepdims=True)
                o = jnp.dot(p.astype(BF16), vcat, preferred_element_type=F32) / den
                og[g, cur, :] = _unstack_heads(o, head0)
                lg[g, cur, :] = _unstack_heads(jnp.broadcast_to(m + jnp.log(den), (2 * BLK, BLK)), head0)

            _dil_blocks(dil, seq, block)
        ls = [lg[0], lg[1], lg[2]]
        m = jnp.maximum(jnp.maximum(ls[0], ls[1]), ls[2])
        ws = [jnp.exp(l - m) for l in ls]
        den = (ws[0] + ws[1]) + ws[2]
        ob_ref[...] = (((ws[0] * og[0] + ws[1] * og[1]) + ws[2] * og[2]) / den).astype(BF16)
        lse_ref[...] = m + jnp.log(den)

    tab_spec = pl.BlockSpec((seq, BLK), lambda b, hp: (0, 0))
    out_spec = pl.BlockSpec((None, seq, BLK), lambda b, hp: (b, 0, hp))
    slab = pltpu.VMEM((seq, BLK), F32)
    return pl.pallas_call(
        body, name="dil_attn_fwd", grid=(bsz, 2),
        out_shape=(jax.ShapeDtypeStruct((bsz, seq, 256), BF16), jax.ShapeDtypeStruct((bsz, seq, 256), F32)),
        in_specs=[pl.BlockSpec((None, seq, WB // 2), lambda b, hp: (b, 0, hp)), tab_spec, tab_spec],
        out_specs=(out_spec, out_spec),
        scratch_shapes=[slab, slab, slab, pltpu.VMEM((3, seq, BLK), F32), pltpu.VMEM((3, seq, BLK), F32)],
        compiler_params=_params("parallel", "parallel"),
    )(proj_b, cos_t, sin_t)


def _dil_bwd(proj_b, cos_t, sin_t, d_ob, o_b, lse, seq):
    bsz = proj_b.shape[0]

    def body(x_ref, cos_ref, sin_ref, do_ref, ob_ref, lse_ref, d_ref, qf, kf, vf, dof, dsf, dq_s, dk_acc, dv_acc):
        consts = _dil_consts()
        head0 = consts[0]
        rope = _make_rope(seq)
        cos, sin = cos_ref[...], sin_ref[...]
        do_all = do_ref[...].astype(F32)
        dof[...] = do_all
        head0_all = lax.broadcasted_iota(jnp.int32, (seq, BLK), 1) < 64
        d0, d1 = _head_rowsum(do_all * ob_ref[...].astype(F32), head0_all)
        dsf[...] = jnp.where(head0_all, d0, d1)
        for g, dil in enumerate(DIL_GROUPS):
            _dil_load_qkv(x_ref, 384 * g, rope, cos, sin, qf, kf, vf)
            dk_acc[...] = jnp.zeros_like(dk_acc)
            dv_acc[...] = jnp.zeros_like(dv_acc)

            def block(cur, prev, i):
                qs, kcat, vcat, s, valid = _dil_scores(qf, kf, vf, cur, prev, i, consts)
                dos = _stack_heads(dof[cur, :].astype(BF16), head0)
                p = jnp.where(valid, jnp.exp(s - _head_cols(lse_ref[cur, :])), 0.0)
                dp = lax.dot_general(dos, vcat, NT, preferred_element_type=F32)
                ds = ((p * (dp - _head_cols(dsf[cur, :]))) * 0.125).astype(BF16)
                dq_s[cur, :] = _unstack_heads(jnp.dot(ds, kcat, preferred_element_type=F32), head0)
                dk = lax.dot_general(ds, qs, TN, preferred_element_type=F32)
                dv = lax.dot_general(p.astype(BF16), dos, TN, preferred_element_type=F32)
                dk_acc[prev, :] += dk[:BLK]
                dk_acc[cur, :] += dk[BLK:]
                dv_acc[prev, :] += dv[:BLK]
                dv_acc[cur, :] += dv[BLK:]

            _dil_blocks(dil, seq, block)
            c = 384 * g
            d_ref[:, c:c + 128] = rope(dq_s[...], cos, -sin).astype(BF16)
            d_ref[:, c + 128:c + 256] = rope(dk_acc[...], cos, -sin).astype(BF16)
            d_ref[:, c + 256:c + 384] = dv_acc[...].astype(BF16)

    x_spec = pl.BlockSpec((None, seq, WB // 2), lambda b, hp: (b, 0, hp))
    tab_spec = pl.BlockSpec((seq, BLK), lambda b, hp: (0, 0))
    tok_spec = pl.BlockSpec((None, seq, BLK), lambda b, hp: (b, 0, hp))
    return pl.pallas_call(
        body, name="dil_attn_bwd", grid=(bsz, 2),
        out_shape=jax.ShapeDtypeStruct((bsz, seq, WB), BF16),
        in_specs=[x_spec, tab_spec, tab_spec, tok_spec, tok_spec, tok_spec], out_specs=x_spec,
        scratch_shapes=[pltpu.VMEM((seq, BLK), F32)] * 8,
        compiler_params=_params("parallel", "parallel"),
    )(proj_b, cos_t, sin_t, d_ob, o_b, lse)


MEM_SCALE = 128 ** -0.5
MEM_QB = 2048


def _mem_fwd(proj_a, kv, seq):
    bsz = proj_a.shape[0]

    def body(q_ref, k_ref, v_ref, o_ref):
        k, v = k_ref[...], v_ref[...]

        def qblock(i, carry):
            r0 = pl.multiple_of(i * MEM_QB, MEM_QB)
            s = lax.dot_general(q_ref[pl.ds(r0, MEM_QB), :], k, NT, preferred_element_type=F32) * MEM_SCALE
            p = jnp.exp(s - jnp.max(s, axis=1, keepdims=True))
            p = p / jnp.sum(p, axis=1, keepdims=True)
            o_ref[pl.ds(r0, MEM_QB), :] = jnp.dot(p.astype(BF16), v, preferred_element_type=F32).astype(BF16)
            return carry

        lax.fori_loop(0, seq // MEM_QB, qblock, 0)

    return pl.pallas_call(
        body, name="mem_attn_fwd", grid=(bsz, 4),
        out_shape=jax.ShapeDtypeStruct((bsz, seq, 512), BF16),
        in_specs=[pl.BlockSpec((None, seq, BLK), lambda b, h: (b, 0, 12 + h)),
                  pl.BlockSpec((None, MEM_LEN, BLK), lambda b, h: (b, 0, h)),
                  pl.BlockSpec((None, MEM_LEN, BLK), lambda b, h: (b, 0, 4 + h))],
        out_specs=pl.BlockSpec((None, seq, BLK), lambda b, h: (b, 0, h)),
        compiler_params=_params("parallel", "parallel"),
    )(proj_a, kv, kv)


def _mem_bwd(proj_a, kv, d_o, d_proj_a, seq):
    bsz = proj_a.shape[0]

    def body(q_ref, k_ref, v_ref, do_ref, _, dq_ref, dk_ref, dv_ref):
        k, v = k_ref[...], v_ref[...]

        def qblock(i, carry):
            dk, dv = carry
            r0 = pl.multiple_of(i * MEM_QB, MEM_QB)
            q, do = q_ref[pl.ds(r0, MEM_QB), :], do_ref[pl.ds(r0, MEM_QB), :]
            s = lax.dot_general(q, k, NT, preferred_element_type=F32) * MEM_SCALE
            p = jnp.exp(s - jnp.max(s, axis=1, keepdims=True))
            p = p / jnp.sum(p, axis=1, keepdims=True)
            dp = lax.dot_general(do, v, NT, preferred_element_type=F32)
            ds = ((p * (dp - jnp.sum(p * dp, axis=1, keepdims=True))) * MEM_SCALE).astype(BF16)
            dq_ref[pl.ds(r0, MEM_QB), :] = jnp.dot(ds, k, preferred_element_type=F32).astype(BF16)
            dk = dk + lax.dot_general(ds, q, TN, preferred_element_type=F32)
            dv = dv + lax.dot_general(p.astype(BF16), do, TN, preferred_element_type=F32)
            return dk, dv

        zero = jnp.zeros((MEM_LEN, BLK), F32)
        dk, dv = lax.fori_loop(0, seq // MEM_QB, qblock, (zero, zero))
        dk_ref[...] = dk.astype(BF16)
        dv_ref[...] = dv.astype(BF16)

    kv_spec = pl.BlockSpec((None, MEM_LEN, BLK), lambda b, h: (b, 0, h))
    return pl.pallas_call(
        body, name="mem_attn_bwd", grid=(bsz, 4),
        out_shape=(jax.ShapeDtypeStruct((bsz, seq, WA), BF16), jax.ShapeDtypeStruct((bsz, MEM_LEN, 512), BF16),
                   jax.ShapeDtypeStruct((bsz, MEM_LEN, 512), BF16)),
        in_specs=[pl.BlockSpec((None, seq, BLK), lambda b, h: (b, 0, 12 + h)), kv_spec,
                  pl.BlockSpec((None, MEM_LEN, BLK), lambda b, h: (b, 0, 4 + h)),
                  pl.BlockSpec((None, seq, BLK), lambda b, h: (b, 0, h)), ANY],
        out_specs=(pl.BlockSpec((None, seq, BLK), lambda b, h: (b, 0, 12 + h)), kv_spec, kv_spec),
        input_output_aliases={4: 0},
        compiler_params=_params("parallel", "parallel"),
    )(proj_a, kv, kv, d_o, d_proj_a)


def _mesh_pos():
    return lax.axis_index("x"), lax.axis_index("y"), lax.axis_index("c")


def _all_gather(shard, name):
    m_per, n = shard.shape

    def body(x_ref, out_ref, send_sems, recv_sems, local_sem):
        x, y, c = _mesh_pos()
        me, sibling = (x, y, c), (x, y, 1 - c)
        chips = [(1 - x, y), (x, 1 - y), (1 - x, 1 - y)]

        def rows(px, py, pc):
            return out_ref.at[pl.ds((4 * px + 2 * py + pc) * m_per, m_per), :]

        def copy(k, block, to, src=None):
            return pltpu.make_async_remote_copy(
                src_ref=rows(*block) if src is None else src, dst_ref=rows(*block),
                send_sem=send_sems.at[k], recv_sem=recv_sems.at[k], device_id=to, device_id_type=MESH)

        mine = pltpu.make_async_copy(x_ref, rows(*me), local_sem)
        mine.start()
        first = [copy(0, me, sibling, src=x_ref)]
        first += [copy(1 + j, me, (*chip, c), src=x_ref) for j, chip in enumerate(chips)]
        for cp in first:
            cp.start()
        passed = [copy(4 + j, (*chip, c), sibling) for j, chip in enumerate(chips)]
        for j, chip in enumerate(chips):
            copy(1 + j, (*chip, c), me).wait_recv()
            passed[j].start()
        copy(0, sibling, me).wait_recv()
        for j, chip in enumerate(chips):
            copy(4 + j, (*chip, 1 - c), me).wait_recv()
        for cp in first + passed:
            cp.wait_send()
        mine.wait()

    return pl.pallas_call(
        body, name=name, out_shape=jax.ShapeDtypeStruct((N_DEV * m_per, n), shard.dtype),
        in_specs=[ANY], out_specs=ANY,
        scratch_shapes=[pltpu.SemaphoreType.DMA((7,)), pltpu.SemaphoreType.DMA((7,)), pltpu.SemaphoreType.DMA(())],
    )(shard)


def _exchange_sems(n_arrays):
    return [pltpu.SemaphoreType.DMA((7 * n_arrays,)), pltpu.SemaphoreType.DMA((7 * n_arrays,)),
            pltpu.SemaphoreType.DMA((n_arrays,))]


def _exchange_out(srcs, gather):
    return [jax.ShapeDtypeStruct((N_DEV, *s.shape[-2:]), s.dtype) for s in srcs]


def _direct_exchange(src_refs, dst_refs, send_sems, recv_sems, local_sems, gather):
    x, y, c = _mesh_pos()
    me = 4 * x + 2 * y + c
    owns, sends, recvs = [], [], []
    for a, (src, dst) in enumerate(zip(src_refs, dst_refs)):
        owns.append(pltpu.make_async_copy(src if gather else src.at[me], dst.at[me], local_sems.at[a]))
        for j in range(1, N_DEV):
            px = 1 - x if j & 4 else x
            py = 1 - y if j & 2 else y
            pc = 1 - c if j & 1 else c
            peer = 4 * px + 2 * py + pc
            sems = dict(send_sem=send_sems.at[7 * a + j - 1], recv_sem=recv_sems.at[7 * a + j - 1],
                        device_id=(px, py, pc), device_id_type=MESH)
            sends.append(pltpu.make_async_remote_copy(
                src_ref=src if gather else src.at[peer], dst_ref=dst.at[me], **sems))
            recvs.append(pltpu.make_async_remote_copy(
                src_ref=src if gather else src.at[me], dst_ref=dst.at[peer], **sems))

    def start():
        for cp in owns + sends:
            cp.start()

    def wait():
        for cp in recvs:
            cp.wait_recv()
        for cp in sends:
            cp.wait_send()
        for cp in owns:
            cp.wait()

    return start, wait


def _riding_exchange(src_refs, dst_refs, sems, gather):
    start, wait = _direct_exchange(src_refs, dst_refs, *sems, gather)
    ids = [pl.program_id(a) for a in range(2)]
    last = [pl.num_programs(a) - 1 for a in range(2)]
    pl.when(jnp.logical_and(ids[0] == 0, ids[1] == 0))(start)
    return lambda: pl.when(jnp.logical_and(ids[0] == last[0], ids[1] == last[1]))(wait)


def _exchange(srcs, gather, name):
    n = len(srcs)

    def body(*refs):
        start, wait = _direct_exchange(refs[:n], refs[n:2 * n], *refs[2 * n:], gather=gather)
        start()
        wait()

    return pl.pallas_call(
        body, name=name, out_shape=_exchange_out(srcs, gather),
        in_specs=[ANY] * n, out_specs=[ANY] * n, scratch_shapes=_exchange_sems(n),
    )(*srcs)


def _adamw(w, g, m, v):
    m = ADAM_B1 * m + (1.0 - ADAM_B1) * g
    v = ADAM_B2 * v + (1.0 - ADAM_B2) * (g * g)
    m_hat = m / (1.0 - ADAM_B1 ** ADAM_STEP)
    v_hat = v / (1.0 - ADAM_B2 ** ADAM_STEP)
    return -ADAM_LR * (m_hat / (jnp.sqrt(v_hat) + ADAM_EPS) + ADAM_WD * w), m, v


def _reduce_adamw(recv, w, m, v, name):
    _, k, n = w.shape
    tr = max(t for t in range(16, 257, 16) if k % t == 0)

    def body(r_ref, w_ref, m_ref, v_ref, g_out, d_out, m_out, v_out):
        g = r_ref[0].astype(F32)
        for s in range(1, N_DEV):
            g = g + r_ref[s].astype(F32)
        g_out[...] = g
        d_out[...], m_out[...], v_out[...] = _adamw(w_ref[...], g, m_ref[...], v_ref[...])

    spec = pl.BlockSpec((None, tr, n), lambda i: (0, i, 0))
    return pl.pallas_call(
        body, name=name, grid=(k // tr,),
        out_shape=[jax.ShapeDtypeStruct((1, k, n), F32)] * 4,
        in_specs=[pl.BlockSpec((N_DEV, tr, n), lambda i: (0, i, 0)), spec, spec, spec],
        out_specs=[spec] * 4, compiler_params=_params("arbitrary"),
    )(recv, w, m, v)


def _small_adamw(gathered, w, m, v):
    def body(g_ref, w_ref, m_ref, v_ref, g_out, d_out, m_out, v_out, loss_out):
        tot = g_ref[0]
        for s in range(1, N_DEV):
            tot = tot + g_ref[s]
        g = tot[0:8]
        g_out[...] = g
        d_out[...], m_out[...], v_out[...] = _adamw(w_ref[...], g, m_ref[...], v_ref[...])
        loss_out[...] = jnp.broadcast_to((0.5 / D) * jnp.sum(tot[8:9], axis=1, keepdims=True), (8, BLK))

    out = [jax.ShapeDtypeStruct((8, D), F32)] * 4 + [jax.ShapeDtypeStruct((8, BLK), F32)]
    return pl.pallas_call(body, name="small_adamw", out_shape=out, compiler_params=_params())(gathered, w, m, v)


def _pick_chunks(w, chunks):
    return jnp.concatenate([w[:, BLK * c:BLK * (c + 1)] for c in chunks], axis=1)


def _whole_weight(gathered, i):
    _, k, n = gathered.shape
    if BY_ROWS[i]:
        return gathered.reshape(N_DEV * k, n)
    return gathered.transpose(1, 0, 2).reshape(k, N_DEV * n)


def _shard_parts(grad, i):
    if BY_ROWS[i]:
        return grad.reshape(N_DEV, grad.shape[0] // N_DEV, grad.shape[1])
    k, n8 = grad.shape
    return grad.reshape(k, N_DEV, n8 // N_DEV).transpose(1, 0, 2)


def kernel(x, mem, g_pre_mix, g_post_mix, g_pre_ffn, g_post_ffn, g_mem, w_in, w_mem_kv, w_br_sb, w_br_dil, w_br_mem, w_gate, b_gate, w_o, w_ffn_in, w_ffn_out, loss_target, m_g_pre_mix, m_g_post_mix, m_g_pre_ffn, m_g_post_ffn, m_g_mem, m_w_in, m_w_mem_kv, m_w_br_sb, m_w_br_dil, m_w_br_mem, m_w_gate, m_b_gate, m_w_o, m_w_ffn_in, m_w_ffn_out, v_g_pre_mix, v_g_post_mix, v_g_pre_ffn, v_g_post_ffn, v_g_mem, v_w_in, v_w_mem_kv, v_w_br_sb, v_w_br_dil, v_w_br_mem, v_w_gate, v_b_gate, v_w_o, v_w_ffn_in, v_w_ffn_out):
    bsz, seq, _ = x.shape
    tokens = bsz * seq
    xf, tgt, memf = x.reshape(tokens, D), loss_target.reshape(tokens, D), mem.reshape(bsz * MEM_LEN, D)
    big_w = [w_in, w_mem_kv, w_br_sb, w_br_dil, w_br_mem, w_gate, w_o, w_ffn_in, w_ffn_out]
    big_m = [m_w_in, m_w_mem_kv, m_w_br_sb, m_w_br_dil, m_w_br_mem, m_w_gate, m_w_o, m_w_ffn_in, m_w_ffn_out]
    big_v = [v_w_in, v_w_mem_kv, v_w_br_sb, v_w_br_dil, v_w_br_mem, v_w_gate, v_w_o, v_w_ffn_in, v_w_ffn_out]

    shards = [w[0].astype(BF16) for w in big_w]
    k_in, n_in = shards[0].shape
    fw_in = _whole_weight(_all_gather(shards[0], "weight_all_gather").reshape(N_DEV, k_in, n_in), 0)
    w_a, w_b = _pick_chunks(fw_in, CHUNKS_A), _pick_chunks(fw_in, CHUNKS_B)

    h = _norm_fwd(xf, g_pre_mix, "pre_mix_norm")
    proj_a = _matmul(h, w_a, "nn", BF16, "proj_a").reshape(bsz, seq, WA)
    proj_b = _matmul(h, w_b, "nn", BF16, "proj_b").reshape(bsz, seq, WB)
    o_a, o_a32, *behind = _sb_fwd(proj_a, seq, [shards[i] for i in GATHER_BEHIND])
    fw_mem_kv, fw_br_sb, fw_br_dil, fw_br_mem, fw_gate, fw_o, fw_ffn_in, fw_ffn_out = (
        _whole_weight(g, i) for g, i in zip(behind, GATHER_BEHIND))
    gpre = _matmul(h, fw_gate, "nn", BF16, "gate_proj")
    cos_t, sin_t = _rope_tables(seq)
    o_b, lse_b = _dil_fwd(proj_b, cos_t, sin_t, seq)
    mn = _norm_fwd(memf, g_mem, "mem_norm")
    kv = _matmul(mn, fw_mem_kv, "nn", BF16, "mem_kv_proj").reshape(bsz, MEM_LEN, D)
    o_c = _mem_fwd(proj_a, kv, seq)
    o_a2, o_b2, o_c2 = o_a.reshape(tokens, 512), o_b.reshape(tokens, 256), o_c.reshape(tokens, 512)
    ys = [_matmul(o_a2, fw_br_sb, "nn", BF16, "branch_sb"), _matmul(o_b2, fw_br_dil, "nn", BF16, "branch_dil"),
          _matmul(o_c2, fw_br_mem, "nn", BF16, "branch_mem")]
    merged, mix, x1, h2 = _merge_out_proj_norm(gpre, ys, b_gate, fw_o, xf, g_post_mix, g_pre_ffn)
    gu_a, gu_b, f = _ffn_in_swiglu(h2, fw_ffn_in)
    dy, dfo, dg_post_ffn, loss_lanes = _ffn_out_loss(f, fw_ffn_out, x1, tgt, g_post_ffn)

    gw_ffn_out = _matmul(f, dfo, "tn", BF16, "gw_ffn_out")
    dgu = _d_ffn_swiglu_bwd(dfo, fw_ffn_out, gu_a, gu_b)
    gw_ffn_in = _matmul(h2, dgu, "tn", BF16, "gw_ffn_in")
    dx1, dmix, dg_pre_ffn, dg_post_mix = _d_h2_norm_bwd(dgu, fw_ffn_in, x1, dy, mix, g_pre_ffn, g_post_mix)
    gw_o = _matmul(merged, dmix, "tn", BF16, "gw_o")
    dya, dyb, dyc, dgpre, db_gate = _d_merged_gate_bwd(dmix, fw_o, gpre, ys, b_gate)
    d_oa = _matmul(dya, fw_br_sb, "nt", BF16, "d_o_sb").reshape(bsz, seq, 512)
    d_ob = _matmul(dyb, fw_br_dil, "nt", BF16, "d_o_dil").reshape(bsz, seq, 256)
    d_oc = _matmul(dyc, fw_br_mem, "nt", BF16, "d_o_mem").reshape(bsz, seq, 512)
    gw_br_sb = _matmul(o_a2, dya, "tn", BF16, "gw_br_sb")
    gw_br_dil = _matmul(o_b2, dyb, "tn", BF16, "gw_br_dil")
    gw_br_mem = _matmul(o_c2, dyc, "tn", BF16, "gw_br_mem")
    gw_gate = _matmul(h, dgpre, "tn", BF16, "gw_gate")
    grads = {2: gw_br_sb, 3: gw_br_dil, 4: gw_br_mem, 5: gw_gate, 6: gw_o, 7: gw_ffn_in, 8: gw_ffn_out}
    d_proj_a, *recv_behind = _sb_bwd(proj_a, d_oa, o_a32, seq, [_shard_parts(grads[i], i) for i in REDUCE_BEHIND])
    d_proj_a, dk_m, dv_m = _mem_bwd(proj_a, kv, d_oc, d_proj_a, seq)
    d_proj_b = _dil_bwd(proj_b, cos_t, sin_t, d_ob, o_b, lse_b, seq).reshape(tokens, WB)
    d_proj_a = d_proj_a.reshape(tokens, WA)
    gw_a = _matmul(h, d_proj_a, "tn", BF16, "gw_in_a")
    gw_b = _matmul(h, d_proj_b, "tn", BF16, "gw_in_b")
    dkv = jnp.concatenate([dk_m, dv_m], axis=-1).reshape(bsz * MEM_LEN, D)
    gw_mem_kv = _matmul(mn, dkv, "tn", BF16, "gw_mem_kv")
    dmn = _matmul(dkv, fw_mem_kv, "nt", F32, "d_mem_norm")
    dg_mem = _gain_grad(dmn, memf)
    gw_ab = jnp.concatenate([gw_a, gw_b], axis=1)
    where = {c: i for i, c in enumerate(CHUNKS_A + CHUNKS_B)}
    grads = {0: _pick_chunks(gw_ab, [where[c] for c in range(34)]), 1: gw_mem_kv}
    dx, dg_pre_mix, *recv_last = _d_h_norm_bwd(
        [(dgpre, fw_gate), (d_proj_a, w_a), (d_proj_b, w_b)], xf, dx1, g_pre_mix,
        [_shard_parts(grads[i], i) for i in REDUCE_LAST])

    received = dict(zip(REDUCE_BEHIND + REDUCE_LAST, [*recv_behind, *recv_last]))
    adam = [_reduce_adamw(received[i], big_w[i], big_m[i], big_v[i], "reduce_adamw_" + BIG_NAMES[i])
            for i in range(len(big_w))]
    big = [[a[k] for a in adam] for k in range(4)]

    small = jnp.concatenate([dg_pre_mix, dg_post_mix, dg_pre_ffn, dg_post_ffn, dg_mem, db_gate.reshape(3, D),
                             loss_lanes, jnp.zeros((7, D), F32)], axis=0)
    small_all, = _exchange([small], True, "small_all_gather")

    def small_pack(gs, b):
        return jnp.concatenate([*gs, b.reshape(3, D)], axis=0)

    sm = _small_adamw(
        small_all, small_pack([g_pre_mix, g_post_mix, g_pre_ffn, g_post_ffn, g_mem], b_gate),
        small_pack([m_g_pre_mix, m_g_post_mix, m_g_pre_ffn, m_g_post_ffn, m_g_mem], m_b_gate),
        small_pack([v_g_pre_mix, v_g_post_mix, v_g_pre_ffn, v_g_post_ffn, v_g_mem], v_b_gate))
    loss = sm[4][0, 0]

    def leaves(k):
        t, bw = sm[k], big[k]
        return [t[0:1], t[1:2], t[2:3], t[3:4], t[4:5], *bw[0:6], t[5:8].reshape(1, 3 * D), *bw[6:9]]

    return (loss, dx.reshape(bsz, seq, D), *leaves(0), *leaves(1), *leaves(2), *leaves(3))
```

```python
import functools

import jax
import jax.numpy as jnp
from jax import lax
from jax.experimental import pallas as pl
from jax.experimental.pallas import tpu as pltpu

F32 = jnp.float32
BF16 = jnp.bfloat16
D = 1024
BLK = 128
MEM_LEN = 256
D_FF = 2816
NORM_EPS = 1e-6
NEG_INF = -1e30
ROPE_THETA = 10000.0
ADAM_LR, ADAM_B1, ADAM_B2, ADAM_EPS, ADAM_WD, ADAM_STEP = 0.001, 0.9, 0.999, 1e-08, 0.01, 10
N_DEV = 8
VMEM_LIMIT_BYTES = 56 * 1024 * 1024
MESH = pl.DeviceIdType.MESH
ANY = pl.BlockSpec(memory_space=pl.ANY)

NT = (((1,), (1,)), ((), ()))
TN = (((0,), (0,)), ((), ()))
NN = (((1,), (0,)), ((), ()))
_DIMS = {"nn": NN, "nt": NT, "tn": TN}

BIG_NAMES = ("w_in", "w_mem_kv", "w_br_sb", "w_br_dil", "w_br_mem", "w_gate", "w_o", "w_ffn_in", "w_ffn_out")
BY_ROWS = (False, True, False, False, False, False, True, False, True)
GATHER_FIRST = (0,)
GATHER_BEHIND = (1, 2, 3, 4, 5, 6, 7, 8)
REDUCE_BEHIND = (2, 3, 4, 5, 6, 7, 8)
REDUCE_LAST = (0, 1)

CHUNKS_A = tuple(c for hp in range(4) for c in (hp, 4 + hp, 8 + hp)) + (30, 31, 32, 33)
CHUNKS_B = tuple(c for hp in range(2) for g in range(3) for c in (12 + 6 * g + hp, 14 + 6 * g + hp, 16 + 6 * g + hp))
WA, WB = 128 * len(CHUNKS_A), 128 * len(CHUNKS_B)
DIL_GROUPS = (1, 4, 16)


def _params(*sem):
    return pltpu.CompilerParams(dimension_semantics=sem or None, vmem_limit_bytes=VMEM_LIMIT_BYTES)


def _tile(n, cap):
    if n <= 128:
        return n
    assert n % 128 == 0, n
    best = 128
    for t in range(128, min(n, cap) + 1, 128):
        if n % t == 0:
            best = t
    return best


def _k_steps(k, nk, step):
    if nk == 1:
        step(True, True)
        return
    pl.when(k == 0)(functools.partial(step, True, False))
    if nk > 2:
        pl.when(jnp.logical_and(k > 0, k < nk - 1))(functools.partial(step, False, False))
    pl.when(k == nk - 1)(functools.partial(step, False, True))


def _matmul(a, b, mode, out_dtype, name, tm_cap=1536, tn_cap=1536, tk_cap=1536):
    if mode == "tn":
        (K, M), N = a.shape, b.shape[1]
    elif mode == "nt":
        (M, K), N = a.shape, b.shape[0]
    else:
        (M, K), N = a.shape, b.shape[1]
    if mode == "tn":
        tk_cap = 2 * tk_cap
    tm, tn, tk = _tile(M, tm_cap), _tile(N, tn_cap), _tile(K, tk_cap)
    nm, nn, nk = M // tm, N // tn, K // tk
    dims = _DIMS[mode]

    def body(a_ref, b_ref, o_ref, *acc):
        def step(first, last):
            d = lax.dot_general(a_ref[...], b_ref[...], dims, preferred_element_type=F32)
            if not first:
                d = d + acc[0][...]
            if last:
                o_ref[...] = d.astype(o_ref.dtype)
            else:
                acc[0][...] = d

        _k_steps(pl.program_id(2), nk, step)

    n_outer = nk == 1 and (a.size * nn + b.size) < (a.size + b.size * nm)
    if n_outer:
        grid, ij = (nn, nm, nk), (lambda g0, g1: (g1, g0))
    else:
        grid, ij = (nm, nn, nk), (lambda g0, g1: (g0, g1))
    if mode == "tn":
        a_spec = pl.BlockSpec((tk, tm), lambda g0, g1, k: (k, ij(g0, g1)[0]))
    else:
        a_spec = pl.BlockSpec((tm, tk), lambda g0, g1, k: (ij(g0, g1)[0], k))
    if mode == "nt":
        b_spec = pl.BlockSpec((tn, tk), lambda g0, g1, k: (ij(g0, g1)[1], k))
    else:
        b_spec = pl.BlockSpec((tk, tn), lambda g0, g1, k: (k, ij(g0, g1)[1]))
    return pl.pallas_call(
        body, name=name, grid=grid,
        out_shape=jax.ShapeDtypeStruct((M, N), out_dtype),
        in_specs=[a_spec, b_spec],
        out_specs=pl.BlockSpec((tm, tn), lambda g0, g1, k: ij(g0, g1)),
        scratch_shapes=[pltpu.VMEM((tm, tn), F32)] if nk > 1 else [],
        compiler_params=_params("parallel", "parallel", "arbitrary"),
    )(a, b)


def _rowwise(body, name, rows, tr, row_ins, vec_ins, row_outs, acc_outs=()):
    tr = min(tr, rows)
    assert rows % tr == 0
    in_specs, args = [], []
    for r in row_ins:
        arr, w, cb = r if isinstance(r, tuple) else (r, r.shape[1], 0)
        in_specs.append(pl.BlockSpec((tr, w), functools.partial(lambda i, cb: (i, cb), cb=cb)))
        args.append(arr)
    for v in vec_ins:
        in_specs.append(pl.BlockSpec(v.shape, lambda i: (0, 0)))
        args.append(v)
    out_shape = [jax.ShapeDtypeStruct((rows, w), dt) for w, dt in row_outs]
    out_shape += [jax.ShapeDtypeStruct((1, w), F32) for w in acc_outs]
    out_specs = [pl.BlockSpec((tr, w), lambda i: (i, 0)) for w, _ in row_outs]
    out_specs += [pl.BlockSpec((1, w), lambda i: (0, 0)) for w in acc_outs]
    n_acc = len(acc_outs)

    def wrapped(*refs):
        if n_acc:
            @pl.when(pl.program_id(0) == 0)
            def _():
                for r in refs[len(refs) - n_acc:]:
                    r[...] = jnp.zeros_like(r)
        body(*refs)

    return pl.pallas_call(
        wrapped, name=name, grid=(rows // tr,), out_shape=out_shape, in_specs=in_specs, out_specs=out_specs,
        compiler_params=_params("arbitrary"),
    )(*args)


def _rstd(x):
    return lax.rsqrt(jnp.mean(x * x, axis=-1, keepdims=True) + NORM_EPS)


def _norm_bwd(u, n, r):
    return r * (u - n * jnp.mean(u * n, axis=-1, keepdims=True))


def _colsum(v):
    return jnp.sum(v, axis=0, keepdims=True)


def _norm_fwd(x, g, name):
    def body(x_ref, g_ref, h_ref):
        xv = x_ref[...]
        h_ref[...] = ((xv * _rstd(xv)) * g_ref[...]).astype(BF16)

    return _rowwise(body, name, x.shape[0], 512, [x], [g], [(D, BF16)])[0]


def _matmul_rows(pairs, mode, name, epilogue, row_ins=(), vec_ins=(), row_outs=(), acc_outs=(), ride=None,
                 tm=512, tk_cap=1536, epi_rows=None):
    M = pairs[0][0].shape[0]
    N = pairs[0][1].shape[1] if mode == "nn" else pairs[0][1].shape[0]
    tm = min(tm, M)
    tks = [_tile(a.shape[1], tk_cap) for a, _ in pairs]
    nks = [a.shape[1] // tk for (a, _), tk in zip(pairs, tks)]
    offs = [sum(nks[:p]) for p in range(len(pairs))]
    nm, nk = M // tm, sum(nks)
    dims = _DIMS[mode]
    n_ab, n_extra, n_out = 2 * len(pairs), len(row_ins) + len(vec_ins), len(row_outs) + len(acc_outs)
    n_ride = 0 if ride is None else len(ride)

    def body(*refs):
        ab, extra, rest = refs[:n_ab], refs[n_ab:n_ab + n_extra], refs[n_ab + n_extra:]
        ride_refs, outs, rest = rest[:n_ride], rest[n_ride:n_ride + n_out], rest[n_ride + n_out:]
        received_refs, rest = rest[:n_ride], rest[n_ride:]
        if n_ride:
            finish_ride = _riding_exchange(ride_refs, received_refs, rest[len(rest) - 3:], gather=False)
        i, k = pl.program_id(0), pl.program_id(1)
        if acc_outs:
            @pl.when(jnp.logical_and(i == 0, k == 0))
            def _():
                for r in outs[len(row_outs):]:
                    r[...] = jnp.zeros_like(r)

        def step(p, first, last):
            d = lax.dot_general(ab[2 * p][...], ab[2 * p + 1][...], dims, preferred_element_type=F32)
            if not first:
                d = d + rest[0][...]
            if not last:
                rest[0][...] = d
            elif epi_rows is None:
                epilogue(d, *extra, *outs)
            else:
                rest[0][...] = d
                for c in range(tm // epi_rows):
                    rows = pl.ds(c * epi_rows, epi_rows)
                    sliced = [r.at[rows] for r in extra[:len(row_ins)]] + list(extra[len(row_ins):])
                    sliced += [r.at[rows] for r in outs[:len(row_outs)]] + list(outs[len(row_outs):])
                    epilogue(rest[0][rows, :], *sliced)

        last_p = len(pairs) - 1
        if nk == 1:
            step(0, True, True)
        else:
            pl.when(k == 0)(functools.partial(step, 0, True, False))
            for p in range(len(pairs)):
                lo, hi = max(offs[p], 1), min(offs[p] + nks[p], nk - 1)
                if hi > lo:
                    pl.when(jnp.logical_and(k >= lo, k < hi))(functools.partial(step, p, False, False))
            pl.when(k == nk - 1)(functools.partial(step, last_p, False, True))
        if n_ride:
            finish_ride()

    in_specs, args = [], []
    for p, ((a, b), tk) in enumerate(zip(pairs, tks)):
        step = functools.partial(lambda k, p: jnp.clip(k - offs[p], 0, nks[p] - 1), p=p)
        in_specs.append(pl.BlockSpec((tm, tk), functools.partial(lambda i, k, step: (i, step(k)), step=step)))
        if mode == "nn":
            in_specs.append(pl.BlockSpec((tk, N), functools.partial(lambda i, k, step: (step(k), 0), step=step)))
        else:
            in_specs.append(pl.BlockSpec((N, tk), functools.partial(lambda i, k, step: (0, step(k)), step=step)))
        args += [a, b]
    in_specs += [pl.BlockSpec((tm, r.shape[1]), lambda i, k: (i, 0)) for r in row_ins]
    in_specs += [pl.BlockSpec(v.shape, lambda i, k: (0, 0)) for v in vec_ins]
    in_specs += [ANY] * n_ride
    out_shape = [jax.ShapeDtypeStruct((M, w), dt) for w, dt in row_outs]
    out_shape += [jax.ShapeDtypeStruct((1, w), F32) for w in acc_outs]
    out_specs = [pl.BlockSpec((tm, w), lambda i, k: (i, 0)) for w, _ in row_outs]
    out_specs += [pl.BlockSpec((1, w), lambda i, k: (0, 0)) for w in acc_outs]
    scratch = [pltpu.VMEM((tm, N), F32)] if nk > 1 else []
    if n_ride:
        out_shape += _exchange_out(ride, False)
        out_specs += [ANY] * n_ride
        scratch += _exchange_sems(n_ride)
    return pl.pallas_call(
        body, name=name, grid=(nm, nk), out_shape=out_shape, in_specs=in_specs, out_specs=out_specs,
        scratch_shapes=scratch, compiler_params=_params("arbitrary", "arbitrary"),
    )(*args, *row_ins, *vec_ins, *(ride or []))


def _merge_out_proj_norm(gpre, ys, b_gate, w_o, x, g_post, g_pre):
    tokens = x.shape[0]
    tm = min(512, tokens)

    def body(gp_ref, ya_ref, yb_ref, yc_ref, b_ref, w_ref, x_ref, g2_ref, g3_ref, m_ref, mix_ref, x1_ref, h2_ref):
        acc = None
        for k, y_ref in enumerate((ya_ref, yb_ref, yc_ref)):
            cols = slice(k * D, (k + 1) * D)
            gate = jax.nn.sigmoid(gp_ref[:, cols].astype(F32) + b_ref[:, cols])
            term = gate * y_ref[...].astype(F32)
            acc = term if acc is None else acc + term
        merged = acc.astype(BF16)
        m_ref[...] = merged
        mv = jnp.dot(merged, w_ref[...], preferred_element_type=F32)
        mix_ref[...] = mv
        x1 = x_ref[...] + (mv * _rstd(mv)) * g2_ref[...]
        x1_ref[...] = x1
        h2_ref[...] = ((x1 * _rstd(x1)) * g3_ref[...]).astype(BF16)

    def rows(w):
        return pl.BlockSpec((tm, w), lambda i: (i, 0))

    def whole(a):
        return pl.BlockSpec(a.shape, lambda i: (0, 0))

    return pl.pallas_call(
        body, name="merge_out_proj_norm", grid=(tokens // tm,),
        out_shape=[jax.ShapeDtypeStruct((tokens, D), dt) for dt in (BF16, F32, F32, BF16)],
        in_specs=[rows(3 * D), rows(D), rows(D), rows(D), whole(b_gate), whole(w_o), rows(D), whole(g_post),
                  whole(g_pre)],
        out_specs=[rows(D)] * 4, compiler_params=_params("parallel"),
    )(gpre, *ys, b_gate, w_o, x, g_post, g_pre)


def _ffn_in_swiglu(h2, w_ffn_in):
    tokens = h2.shape[0]
    tm, tn = min(512, tokens), _tile(D_FF, 1536)
    nj = D_FF // tn

    def body(h_ref, wa_ref, wb_ref, a_ref, b_ref, f_ref):
        hv = h_ref[...]
        a = jnp.dot(hv, wa_ref[...], preferred_element_type=F32)
        b = jnp.dot(hv, wb_ref[...], preferred_element_type=F32)
        a_ref[...] = a.astype(BF16)
        b_ref[...] = b.astype(BF16)
        f_ref[...] = (a * jax.nn.sigmoid(a) * b).astype(BF16)

    out = jax.ShapeDtypeStruct((tokens, D_FF), BF16)
    o_spec = pl.BlockSpec((tm, tn), lambda j, i: (i, j))
    return pl.pallas_call(
        body, name="ffn_in_swiglu", grid=(nj, tokens // tm), out_shape=(out, out, out),
        in_specs=[pl.BlockSpec((tm, D), lambda j, i: (i, 0)), pl.BlockSpec((D, tn), lambda j, i: (0, j)),
                  pl.BlockSpec((D, tn), lambda j, i: (0, j + nj))],
        out_specs=(o_spec, o_spec, o_spec), compiler_params=_params("parallel", "parallel"),
    )(h2, w_ffn_in, w_ffn_in)


def _ffn_out_loss(f, w_ffn_out, x1, tgt, g_post):
    def epilogue(fo_v, x1_ref, t_ref, g_ref, dy_ref, dfo_ref, dg_ref, loss_ref):
        r = _rstd(fo_v)
        n = fo_v * r
        err = (x1_ref[...] + n * g_ref[...]) - t_ref[...]
        loss_ref[...] += _colsum(err * err)
        dy = err * (1.0 / D)
        dy_ref[...] = dy
        dg_ref[...] += _colsum(dy * n)
        dfo_ref[...] = _norm_bwd(dy * g_ref[...], n, r).astype(BF16)

    return _matmul_rows([(f, w_ffn_out)], "nn", "ffn_out_loss", epilogue, [x1, tgt], [g_post],
                        [(D, F32), (D, BF16)], (D, D))


def _d_ffn_swiglu_bwd(dfo, w_ffn_out, gu_a, gu_b):
    def epilogue(d, a_ref, b_ref, dgu_ref):
        a = a_ref[...].astype(F32)
        b = b_ref[...].astype(F32)
        s = jax.nn.sigmoid(a)
        dgu_ref[:, :D_FF] = (d * b * (s * (1.0 + a * (1.0 - s)))).astype(BF16)
        dgu_ref[:, D_FF:] = (d * (a * s)).astype(BF16)

    return _matmul_rows([(dfo, w_ffn_out)], "nt", "d_ffn_swiglu_bwd", epilogue, [gu_a, gu_b], [],
                        [(2 * D_FF, BF16)], tm=512, tk_cap=512, epi_rows=128)[0]


def _d_h2_norm_bwd(dgu, w_ffn_in, x1, dy, mix, g_pre, g_post):
    def epilogue(dh, x1_ref, dy_ref, mix_ref, g3_ref, g2_ref, dx1_ref, dmix_ref, dg3_ref, dg2_ref):
        x1v = x1_ref[...]
        r3 = _rstd(x1v)
        n3 = x1v * r3
        dg3_ref[...] += _colsum(dh * n3)
        dx1 = dy_ref[...] + _norm_bwd(dh * g3_ref[...], n3, r3)
        dx1_ref[...] = dx1
        mv = mix_ref[...]
        r2 = _rstd(mv)
        n2 = mv * r2
        dg2_ref[...] += _colsum(dx1 * n2)
        dmix_ref[...] = _norm_bwd(dx1 * g2_ref[...], n2, r2).astype(BF16)

    return _matmul_rows([(dgu, w_ffn_in)], "nt", "d_h2_norm_bwd", epilogue, [x1, dy, mix], [g_pre, g_post],
                        [(D, F32), (D, BF16)], (D, D))


def _d_merged_gate_bwd(dmix, w_o, gpre, ys, b_gate):
    def epilogue(dm, gp_ref, ya_ref, yb_ref, yc_ref, b_ref, dya_ref, dyb_ref, dyc_ref, dgp_ref, db_ref):
        for k, (y_ref, dy_ref) in enumerate(((ya_ref, dya_ref), (yb_ref, dyb_ref), (yc_ref, dyc_ref))):
            cols = slice(k * D, (k + 1) * D)
            gate = jax.nn.sigmoid(gp_ref[:, cols].astype(F32) + b_ref[:, cols])
            dy_ref[...] = (dm * gate).astype(BF16)
            dgp = (dm * y_ref[...].astype(F32)) * (gate * (1.0 - gate))
            dgp_ref[:, cols] = dgp.astype(BF16)
            db_ref[:, cols] += _colsum(dgp)

    return _matmul_rows([(dmix, w_o)], "nt", "d_merged_gate_bwd", epilogue, [gpre, *ys], [b_gate],
                        [(D, BF16), (D, BF16), (D, BF16), (3 * D, BF16)], (3 * D,))


def _d_h_norm_bwd(pairs, x, dx1, g_pre, ride):
    def epilogue(dh, x_ref, dx1_ref, g_ref, dx_ref, dg_ref):
        xv = x_ref[...]
        r = _rstd(xv)
        n = xv * r
        dg_ref[...] += _colsum(dh * n)
        dx_ref[...] = dx1_ref[...] + _norm_bwd(dh * g_ref[...], n, r)

    return _matmul_rows(pairs, "nt", "d_h_norm_bwd", epilogue, [x, dx1], [g_pre], [(D, F32)], (D,), ride=ride,
                        tm=1024, tk_cap=512, epi_rows=256)


def _gain_grad(dmn, mem):
    def body(d_ref, m_ref, dg_ref):
        mv = m_ref[...]
        dg_ref[...] += _colsum(d_ref[...] * (mv * _rstd(mv)))

    return _rowwise(body, "mem_gain_grad", mem.shape[0], 256, [dmn, mem], [], [], (D,))[0]


def _head_rowsum(v, head0):
    return (jnp.sum(jnp.where(head0, v, 0.0), axis=1, keepdims=True),
            jnp.sum(jnp.where(head0, 0.0, v), axis=1, keepdims=True))


KT = 256
SB_SCALE = 0.125


def _make_suffix(inclusive):
    row, col = lax.broadcasted_iota(jnp.int32, (KT, KT), 0), lax.broadcasted_iota(jnp.int32, (KT, KT), 1)
    tri = (row >= col if inclusive else row > col).astype(BF16)
    tri2 = jnp.concatenate([tri, tri], axis=0)

    def suffix(x):
        hi = x.astype(BF16)
        lo = (x - hi.astype(F32)).astype(BF16)
        return jnp.dot(jnp.concatenate([hi, lo], axis=1), tri2, preferred_element_type=F32)

    return suffix


def _sb_scores(qh, k, mask, suffix_incl, later):
    z = lax.dot_general(qh, k, NT, preferred_element_type=F32)
    zc = jnp.minimum(z, 60.0)
    sp = jnp.log(1.0 + jnp.exp(zc))
    if mask is not None:
        sp = jnp.where(mask, sp, 0.0)
    a = jnp.exp((zc - suffix_incl(sp)) - later)
    if mask is not None:
        a = jnp.where(mask, a, 0.0)
    return zc, sp, a


QB = KT


def _sb_tiles(i, tile, init):
    st = lax.cond(i > 0, lambda s: tile(i - 1, tile(i, s, True), False), lambda s: tile(i, s, True), init)
    rest = jnp.maximum(i - 1, 0)
    st = lax.fori_loop(0, lax.shift_right_logical(rest, 1),
                       lambda t, s: tile(rest - 2 - 2 * t, tile(rest - 1 - 2 * t, s, False), False), st)
    return lax.cond((rest & 1) == 1, lambda s: tile(0, s, False), lambda s: s, st)


def _sb_consts():
    head0 = lax.broadcasted_iota(jnp.int32, (QB, BLK), 1) < 64
    row = lax.broadcasted_iota(jnp.int32, (2 * QB, KT), 0) & (QB - 1)
    return head0, row > lax.broadcasted_iota(jnp.int32, (2 * QB, KT), 1)


def _stack_heads(v, head0):
    zero = jnp.zeros_like(v)
    return jnp.concatenate([jnp.where(head0, v, zero), jnp.where(head0, zero, v)], axis=0)


def _unstack_heads(v, head0):
    n = v.shape[0] // 2
    return jnp.where(head0, v[:n], v[n:])


def _sb_fwd(proj_a, seq, ride):
    bsz = proj_a.shape[0]
    n_ride = len(ride)

    def body(x_ref, *rest):
        ride_refs, (o_ref, o32_ref), rest = rest[:n_ride], rest[n_ride:n_ride + 2], rest[n_ride + 2:]
        gathered_refs, acc_ref, sems = rest[:n_ride], rest[n_ride], rest[n_ride + 1:]
        finish_ride = _riding_exchange(ride_refs, gathered_refs, sems, gather=True)
        head0, diag_mask = _sb_consts()
        suffix_incl = _make_suffix(True)

        def qblock(i, carry):
            r0 = pl.multiple_of(i * QB, QB)
            qs = _stack_heads(x_ref[pl.ds(r0, QB), 0:128] * jnp.asarray(SB_SCALE, BF16), head0)

            def tile(jt, later, masked):
                c0 = pl.multiple_of(jt * KT, KT)
                k = x_ref[pl.ds(c0, KT), 128:256]
                v = x_ref[pl.ds(c0, KT), 256:384]
                _, sp, a = _sb_scores(qs, k, diag_mask if masked else None, suffix_incl, later)
                pv = jnp.dot(a.astype(BF16), v, preferred_element_type=F32)
                if masked:
                    acc_ref[...] = pv
                else:
                    acc_ref[...] += pv
                return later + jnp.sum(sp, axis=1, keepdims=True)

            _sb_tiles(i, tile, jnp.zeros((2 * QB, 1), F32))
            o = _unstack_heads(acc_ref[...], head0)
            o32_ref[pl.ds(r0, QB), :] = o
            o_ref[pl.ds(r0, QB), :] = o.astype(BF16)
            return carry

        lax.fori_loop(0, seq // QB, qblock, 0)
        finish_ride()

    out_spec = pl.BlockSpec((None, seq, BLK), lambda b, hp: (b, 0, hp))
    return pl.pallas_call(
        body, name="sb_attn_fwd", grid=(bsz, 4),
        out_shape=[jax.ShapeDtypeStruct((bsz, seq, 512), BF16), jax.ShapeDtypeStruct((bsz, seq, 512), F32),
                   *_exchange_out(ride, True)],
        in_specs=[pl.BlockSpec((None, seq, 384), lambda b, hp: (b, 0, hp))] + [ANY] * n_ride,
        out_specs=[out_spec, out_spec] + [ANY] * n_ride,
        scratch_shapes=[pltpu.VMEM((2 * QB, BLK), F32), *_exchange_sems(n_ride)],
        compiler_params=_params("arbitrary", "arbitrary"),
    )(proj_a, *ride)


def _sb_bwd(proj_a, d_o, o_a, seq, ride):
    bsz = proj_a.shape[0]
    n_ride = len(ride)

    def body(x_ref, do_ref, o_ref, *rest):
        ride_refs, d_ref, rest = rest[:n_ride], rest[n_ride], rest[n_ride + 1:]
        received_refs, (dq_acc, dk_acc, dv_acc), sems = rest[:n_ride], rest[n_ride:n_ride + 3], rest[n_ride + 3:]
        finish_ride = _riding_exchange(ride_refs, received_refs, sems, gather=False)
        head0, diag_mask = _sb_consts()
        suffix_incl, suffix_excl = _make_suffix(True), _make_suffix(False)
        dk_acc[...] = jnp.zeros_like(dk_acc)
        dv_acc[...] = jnp.zeros_like(dv_acc)

        def qblock(i, carry):
            r0 = pl.multiple_of(i * QB, QB)
            qs = _stack_heads(x_ref[pl.ds(r0, QB), 0:128] * jnp.asarray(SB_SCALE, BF16), head0)
            do = do_ref[pl.ds(r0, QB), :]
            dos = _stack_heads(do, head0)
            dsum = jnp.concatenate(_head_rowsum(do.astype(F32) * o_ref[pl.ds(r0, QB), :], head0), axis=0)

            def tile(jt, st, masked):
                later, rest_g = st
                c0 = pl.multiple_of(jt * KT, KT)
                k = x_ref[pl.ds(c0, KT), 128:256]
                v = x_ref[pl.ds(c0, KT), 256:384]
                zc, sp, a = _sb_scores(qs, k, diag_mask if masked else None, suffix_incl, later)
                a16 = a.astype(BF16)
                g = a16.astype(F32) * lax.dot_general(dos, v, NT, preferred_element_type=F32)
                dz = g - jnp.exp(zc - sp) * (rest_g - suffix_excl(g))
                if masked:
                    dz = jnp.where(diag_mask, dz, 0.0)
                dz = dz.astype(BF16)
                dq = jnp.dot(dz, k, preferred_element_type=F32)
                if masked:
                    dq_acc[...] = dq
                else:
                    dq_acc[...] += dq
                dk_acc[pl.ds(c0, KT), :] += lax.dot_general(dz, qs, TN, preferred_element_type=F32)
                dv_acc[pl.ds(c0, KT), :] += lax.dot_general(a16, dos, TN, preferred_element_type=F32)
                return later + jnp.sum(sp, axis=1, keepdims=True), rest_g - jnp.sum(g, axis=1, keepdims=True)

            _sb_tiles(i, tile, (jnp.zeros((2 * QB, 1), F32), dsum))
            d_ref[pl.ds(r0, QB), 0:128] = (_unstack_heads(dq_acc[...], head0) * SB_SCALE).astype(BF16)
            return carry

        lax.fori_loop(0, seq // QB, qblock, 0)
        d_ref[:, 128:256] = dk_acc[...].astype(BF16)
        d_ref[:, 256:384] = dv_acc[...].astype(BF16)
        finish_ride()

    return pl.pallas_call(
        body, name="sb_attn_bwd", grid=(bsz, 4),
        out_shape=[jax.ShapeDtypeStruct((bsz, seq, WA), BF16), *_exchange_out(ride, False)],
        in_specs=[pl.BlockSpec((None, seq, 384), lambda b, hp: (b, 0, hp)),
                  pl.BlockSpec((None, seq, BLK), lambda b, hp: (b, 0, hp)),
                  pl.BlockSpec((None, seq, BLK), lambda b, hp: (b, 0, hp))] + [ANY] * n_ride,
        out_specs=[pl.BlockSpec((None, seq, 384), lambda b, hp: (b, 0, hp))] + [ANY] * n_ride,
        scratch_shapes=[pltpu.VMEM((2 * QB, BLK), F32), pltpu.VMEM((seq, BLK), F32), pltpu.VMEM((seq, BLK), F32),
                        *_exchange_sems(n_ride)],
        compiler_params=_params("arbitrary", "arbitrary"),
    )(proj_a, d_o, o_a, *ride)


def _rope_tables(seq):
    inv_freq = ROPE_THETA ** (-jnp.arange(32, dtype=F32) * 2.0 / 64)
    ang = jnp.arange(seq).astype(F32)[:, None] * inv_freq[None, :]
    cos, sin = jnp.cos(ang), jnp.sin(ang)
    return jnp.tile(cos, (1, 4)), jnp.concatenate([-sin, sin, -sin, sin], axis=1)


def _make_rope(n_rows):
    lane = lax.broadcasted_iota(jnp.int32, (n_rows, BLK), 1)
    first = (lane & 63) < 32

    def rope(x, cos, sin):
        partner = jnp.where(first, pltpu.roll(x, 96, 1), pltpu.roll(x, 32, 1))
        return x * cos + partner * sin

    return rope


DIL_UNROLL = 8


def _dil_consts():
    head0 = lax.broadcasted_iota(jnp.int32, (BLK, BLK), 1) < 64
    row = lax.broadcasted_iota(jnp.int32, (2 * BLK, 2 * BLK), 0) & (BLK - 1)
    col = lax.broadcasted_iota(jnp.int32, (2 * BLK, 2 * BLK), 1)
    valid_prev = jnp.logical_and(col < BLK, col >= row)
    valid_cur = jnp.logical_and(col >= BLK, row >= col - BLK)
    return head0, valid_prev, valid_cur


def _dil_blocks(dil, seq, block):
    nq = seq // dil // BLK

    def rows(r, i):
        if dil == 1:
            return pl.ds(pl.multiple_of(i * BLK, BLK), BLK)
        return pl.ds(r + (dil * BLK) * i, BLK, stride=dil)

    def step(t, carry):
        for u in range(DIL_UNROLL):
            n = t * DIL_UNROLL + u
            r, i = lax.div(n, nq), lax.rem(n, nq)
            block(rows(r, i), rows(r, jnp.maximum(i - 1, 0)), i)
        return carry

    lax.fori_loop(0, seq // BLK // DIL_UNROLL, step, 0)


def _dil_scores(qf, kf, vf, cur, prev, i, consts):
    head0, valid_prev, valid_cur = consts
    qs = _stack_heads(qf[cur, :].astype(BF16), head0)
    kcat = jnp.concatenate([kf[prev, :], kf[cur, :]], axis=0).astype(BF16)
    vcat = jnp.concatenate([vf[prev, :], vf[cur, :]], axis=0).astype(BF16)
    valid = jnp.logical_or(valid_cur, jnp.logical_and(valid_prev, i > 0))
    s = lax.dot_general(qs, kcat, NT, preferred_element_type=F32) * 0.125
    return qs, kcat, vcat, s, valid


def _head_cols(v):
    return jnp.concatenate([v[:, 0:1], v[:, 64:65]], axis=0)


def _dil_load_qkv(x_ref, c, rope, cos, sin, qf, kf, vf):
    qf[...] = rope(x_ref[:, c:c + 128].astype(F32), cos, sin).astype(BF16).astype(F32)
    kf[...] = rope(x_ref[:, c + 128:c + 256].astype(F32), cos, sin).astype(BF16).astype(F32)
    vf[...] = x_ref[:, c + 256:c + 384].astype(F32)


def _dil_fwd(proj_b, cos_t, sin_t, seq):
    bsz = proj_b.shape[0]

    def body(x_ref, cos_ref, sin_ref, ob_ref, lse_ref, qf, kf, vf, og, lg):
        consts = _dil_consts()
        head0 = consts[0]
        rope = _make_rope(seq)
        cos, sin = cos_ref[...], sin_ref[...]
        for g, dil in enumerate(DIL_GROUPS):
            _dil_load_qkv(x_ref, 384 * g, rope, cos, sin, qf, kf, vf)

            def block(cur, prev, i, g=g):
                _, _, vcat, s, valid = _dil_scores(qf, kf, vf, cur, prev, i, consts)
                s = jnp.where(valid, s, NEG_INF)
                m = jnp.max(s, axis=1, keepdims=True)
                p = jnp.exp(s - m)
                den = jnp.sum(p, axis=1, keepdims=True)
                o = jnp.dot(p.astype(BF16), vcat, preferred_element_type=F32) / den
                og[g, cur, :] = _unstack_heads(o, head0)
                lg[g, cur, :] = _unstack_heads(jnp.broadcast_to(m + jnp.log(den), (2 * BLK, BLK)), head0)

            _dil_blocks(dil, seq, block)
        ls = [lg[0], lg[1], lg[2]]
        m = jnp.maximum(jnp.maximum(ls[0], ls[1]), ls[2])
        ws = [jnp.exp(l - m) for l in ls]
        den = (ws[0] + ws[1]) + ws[2]
        ob_ref[...] = (((ws[0] * og[0] + ws[1] * og[1]) + ws[2] * og[2]) / den).astype(BF16)
        lse_ref[...] = m + jnp.log(den)

    tab_spec = pl.BlockSpec((seq, BLK), lambda b, hp: (0, 0))
    out_spec = pl.BlockSpec((None, seq, BLK), lambda b, hp: (b, 0, hp))
    slab = pltpu.VMEM((seq, BLK), F32)
    return pl.pallas_call(
        body, name="dil_attn_fwd", grid=(bsz, 2),
        out_shape=(jax.ShapeDtypeStruct((bsz, seq, 256), BF16), jax.ShapeDtypeStruct((bsz, seq, 256), F32)),
        in_specs=[pl.BlockSpec((None, seq, WB // 2), lambda b, hp: (b, 0, hp)), tab_spec, tab_spec],
        out_specs=(out_spec, out_spec),
        scratch_shapes=[slab, slab, slab, pltpu.VMEM((3, seq, BLK), F32), pltpu.VMEM((3, seq, BLK), F32)],
        compiler_params=_params("parallel", "parallel"),
    )(proj_b, cos_t, sin_t)


def _dil_bwd(proj_b, cos_t, sin_t, d_ob, o_b, lse, seq):
    bsz = proj_b.shape[0]

    def body(x_ref, cos_ref, sin_ref, do_ref, ob_ref, lse_ref, d_ref, qf, kf, vf, dof, dsf, dq_s, dk_acc, dv_acc):
        consts = _dil_consts()
        head0 = consts[0]
        rope = _make_rope(seq)
        cos, sin = cos_ref[...], sin_ref[...]
        do_all = do_ref[...].astype(F32)
        dof[...] = do_all
        head0_all = lax.broadcasted_iota(jnp.int32, (seq, BLK), 1) < 64
        d0, d1 = _head_rowsum(do_all * ob_ref[...].astype(F32), head0_all)
        dsf[...] = jnp.where(head0_all, d0, d1)
        for g, dil in enumerate(DIL_GROUPS):
            _dil_load_qkv(x_ref, 384 * g, rope, cos, sin, qf, kf, vf)
            dk_acc[...] = jnp.zeros_like(dk_acc)
            dv_acc[...] = jnp.zeros_like(dv_acc)

            def block(cur, prev, i):
                qs, kcat, vcat, s, valid = _dil_scores(qf, kf, vf, cur, prev, i, consts)
                dos = _stack_heads(dof[cur, :].astype(BF16), head0)
                p = jnp.where(valid, jnp.exp(s - _head_cols(lse_ref[cur, :])), 0.0)
                dp = lax.dot_general(dos, vcat, NT, preferred_element_type=F32)
                ds = ((p * (dp - _head_cols(dsf[cur, :]))) * 0.125).astype(BF16)
                dq_s[cur, :] = _unstack_heads(jnp.dot(ds, kcat, preferred_element_type=F32), head0)
                dk = lax.dot_general(ds, qs, TN, preferred_element_type=F32)
                dv = lax.dot_general(p.astype(BF16), dos, TN, preferred_element_type=F32)
                dk_acc[prev, :] += dk[:BLK]
                dk_acc[cur, :] += dk[BLK:]
                dv_acc[prev, :] += dv[:BLK]
                dv_acc[cur, :] += dv[BLK:]

            _dil_blocks(dil, seq, block)
            c = 384 * g
            d_ref[:, c:c + 128] = rope(dq_s[...], cos, -sin).astype(BF16)
            d_ref[:, c + 128:c + 256] = rope(dk_acc[...], cos, -sin).astype(BF16)
            d_ref[:, c + 256:c + 384] = dv_acc[...].astype(BF16)

    x_spec = pl.BlockSpec((None, seq, WB // 2), lambda b, hp: (b, 0, hp))
    tab_spec = pl.BlockSpec((seq, BLK), lambda b, hp: (0, 0))
    tok_spec = pl.BlockSpec((None, seq, BLK), lambda b, hp: (b, 0, hp))
    return pl.pallas_call(
        body, name="dil_attn_bwd", grid=(bsz, 2),
        out_shape=jax.ShapeDtypeStruct((bsz, seq, WB), BF16),
        in_specs=[x_spec, tab_spec, tab_spec, tok_spec, tok_spec, tok_spec], out_specs=x_spec,
        scratch_shapes=[pltpu.VMEM((seq, BLK), F32)] * 8,
        compiler_params=_params("parallel", "parallel"),
    )(proj_b, cos_t, sin_t, d_ob, o_b, lse)


MEM_SCALE = 128 ** -0.5
MEM_QB = 2048


def _mem_fwd(proj_a, kv, seq):
    bsz = proj_a.shape[0]

    def body(q_ref, k_ref, v_ref, o_ref):
        k, v = k_ref[...], v_ref[...]

        def qblock(i, carry):
            r0 = pl.multiple_of(i * MEM_QB, MEM_QB)
            s = lax.dot_general(q_ref[pl.ds(r0, MEM_QB), :], k, NT, preferred_element_type=F32) * MEM_SCALE
            p = jnp.exp(s - jnp.max(s, axis=1, keepdims=True))
            p = p / jnp.sum(p, axis=1, keepdims=True)
            o_ref[pl.ds(r0, MEM_QB), :] = jnp.dot(p.astype(BF16), v, preferred_element_type=F32).astype(BF16)
            return carry

        lax.fori_loop(0, seq // MEM_QB, qblock, 0)

    return pl.pallas_call(
        body, name="mem_attn_fwd", grid=(bsz, 4),
        out_shape=jax.ShapeDtypeStruct((bsz, seq, 512), BF16),
        in_specs=[pl.BlockSpec((None, seq, BLK), lambda b, h: (b, 0, 12 + h)),
                  pl.BlockSpec((None, MEM_LEN, BLK), lambda b, h: (b, 0, h)),
                  pl.BlockSpec((None, MEM_LEN, BLK), lambda b, h: (b, 0, 4 + h))],
        out_specs=pl.BlockSpec((None, seq, BLK), lambda b, h: (b, 0, h)),
        compiler_params=_params("parallel", "parallel"),
    )(proj_a, kv, kv)


def _mem_bwd(proj_a, kv, d_o, d_proj_a, seq):
    bsz = proj_a.shape[0]

    def body(q_ref, k_ref, v_ref, do_ref, _, dq_ref, dk_ref, dv_ref):
        k, v = k_ref[...], v_ref[...]

        def qblock(i, carry):
            dk, dv = carry
            r0 = pl.multiple_of(i * MEM_QB, MEM_QB)
            q, do = q_ref[pl.ds(r0, MEM_QB), :], do_ref[pl.ds(r0, MEM_QB), :]
            s = lax.dot_general(q, k, NT, preferred_element_type=F32) * MEM_SCALE
            p = jnp.exp(s - jnp.max(s, axis=1, keepdims=True))
            p = p / jnp.sum(p, axis=1, keepdims=True)
            dp = lax.dot_general(do, v, NT, preferred_element_type=F32)
            ds = ((p * (dp - jnp.sum(p * dp, axis=1, keepdims=True))) * MEM_SCALE).astype(BF16)
            dq_ref[pl.ds(r0, MEM_QB), :] = jnp.dot(ds, k, preferred_element_type=F32).astype(BF16)
            dk = dk + lax.dot_general(ds, q, TN, preferred_element_type=F32)
            dv = dv + lax.dot_general(p.astype(BF16), do, TN, preferred_element_type=F32)
            return dk, dv

        zero = jnp.zeros((MEM_LEN, BLK), F32)
        dk, dv = lax.fori_loop(0, seq // MEM_QB, qblock, (zero, zero))
        dk_ref[...] = dk.astype(BF16)
        dv_ref[...] = dv.astype(BF16)

    kv_spec = pl.BlockSpec((None, MEM_LEN, BLK), lambda b, h: (b, 0, h))
    return pl.pallas_call(
        body, name="mem_attn_bwd", grid=(bsz, 4),
        out_shape=(jax.ShapeDtypeStruct((bsz, seq, WA), BF16), jax.ShapeDtypeStruct((bsz, MEM_LEN, 512), BF16),
                   jax.ShapeDtypeStruct((bsz, MEM_LEN, 512), BF16)),
        in_specs=[pl.BlockSpec((None, seq, BLK), lambda b, h: (b, 0, 12 + h)), kv_spec,
                  pl.BlockSpec((None, MEM_LEN, BLK), lambda b, h: (b, 0, 4 + h)),
                  pl.BlockSpec((None, seq, BLK), lambda b, h: (b, 0, h)), ANY],
        out_specs=(pl.BlockSpec((None, seq, BLK), lambda b, h: (b, 0, 12 + h)), kv_spec, kv_spec),
        input_output_aliases={4: 0},
        compiler_params=_params("parallel", "parallel"),
    )(proj_a, kv, kv, d_o, d_proj_a)


def _mesh_pos():
    return lax.axis_index("x"), lax.axis_index("y"), lax.axis_index("c")


def _all_gather(shard, name):
    m_per, n = shard.shape

    def body(x_ref, out_ref, send_sems, recv_sems, local_sem):
        x, y, c = _mesh_pos()
        me, sibling = (x, y, c), (x, y, 1 - c)
        chips = [(1 - x, y), (x, 1 - y), (1 - x, 1 - y)]

        def rows(px, py, pc):
            return out_ref.at[pl.ds((4 * px + 2 * py + pc) * m_per, m_per), :]

        def copy(k, block, to, src=None):
            return pltpu.make_async_remote_copy(
                src_ref=rows(*block) if src is None else src, dst_ref=rows(*block),
                send_sem=send_sems.at[k], recv_sem=recv_sems.at[k], device_id=to, device_id_type=MESH)

        mine = pltpu.make_async_copy(x_ref, rows(*me), local_sem)
        mine.start()
        first = [copy(0, me, sibling, src=x_ref)]
        first += [copy(1 + j, me, (*chip, c), src=x_ref) for j, chip in enumerate(chips)]
        for cp in first:
            cp.start()
        passed = [copy(4 + j, (*chip, c), sibling) for j, chip in enumerate(chips)]
        for j, chip in enumerate(chips):
            copy(1 + j, (*chip, c), me).wait_recv()
            passed[j].start()
        copy(0, sibling, me).wait_recv()
        for j, chip in enumerate(chips):
            copy(4 + j, (*chip, 1 - c), me).wait_recv()
        for cp in first + passed:
            cp.wait_send()
        mine.wait()

    return pl.pallas_call(
        body, name=name, out_shape=jax.ShapeDtypeStruct((N_DEV * m_per, n), shard.dtype),
        in_specs=[ANY], out_specs=ANY,
        scratch_shapes=[pltpu.SemaphoreType.DMA((7,)), pltpu.SemaphoreType.DMA((7,)), pltpu.SemaphoreType.DMA(())],
    )(shard)


def _exchange_sems(n_arrays):
    return [pltpu.SemaphoreType.DMA((7 * n_arrays,)), pltpu.SemaphoreType.DMA((7 * n_arrays,)),
            pltpu.SemaphoreType.DMA((n_arrays,))]


def _exchange_out(srcs, gather):
    return [jax.ShapeDtypeStruct((N_DEV, *s.shape[-2:]), s.dtype) for s in srcs]


def _direct_exchange(src_refs, dst_refs, send_sems, recv_sems, local_sems, gather):
    x, y, c = _mesh_pos()
    me = 4 * x + 2 * y + c
    owns, sends, recvs = [], [], []
    for a, (src, dst) in enumerate(zip(src_refs, dst_refs)):
        owns.append(pltpu.make_async_copy(src if gather else src.at[me], dst.at[me], local_sems.at[a]))
        for j in range(1, N_DEV):
            px = 1 - x if j & 4 else x
            py = 1 - y if j & 2 else y
            pc = 1 - c if j & 1 else c
            peer = 4 * px + 2 * py + pc
            sems = dict(send_sem=send_sems.at[7 * a + j - 1], recv_sem=recv_sems.at[7 * a + j - 1],
                        device_id=(px, py, pc), device_id_type=MESH)
            sends.append(pltpu.make_async_remote_copy(
                src_ref=src if gather else src.at[peer], dst_ref=dst.at[me], **sems))
            recvs.append(pltpu.make_async_remote_copy(
                src_ref=src if gather else src.at[me], dst_ref=dst.at[peer], **sems))

    def start():
        for cp in owns + sends:
            cp.start()

    def wait():
        for cp in recvs:
            cp.wait_recv()
        for cp in sends:
            cp.wait_send()
        for cp in owns:
            cp.wait()

    return start, wait


def _riding_exchange(src_refs, dst_refs, sems, gather):
    start, wait = _direct_exchange(src_refs, dst_refs, *sems, gather)
    ids = [pl.program_id(a) for a in range(2)]
    last = [pl.num_programs(a) - 1 for a in range(2)]
    pl.when(jnp.logical_and(ids[0] == 0, ids[1] == 0))(start)
    return lambda: pl.when(jnp.logical_and(ids[0] == last[0], ids[1] == last[1]))(wait)


def _exchange(srcs, gather, name):
    n = len(srcs)

    def body(*refs):
        start, wait = _direct_exchange(refs[:n], refs[n:2 * n], *refs[2 * n:], gather=gather)
        start()
        wait()

    return pl.pallas_call(
        body, name=name, out_shape=_exchange_out(srcs, gather),
        in_specs=[ANY] * n, out_specs=[ANY] * n, scratch_shapes=_exchange_sems(n),
    )(*srcs)


def _adamw(w, g, m, v):
    m = ADAM_B1 * m + (1.0 - ADAM_B1) * g
    v = ADAM_B2 * v + (1.0 - ADAM_B2) * (g * g)
    m_hat = m / (1.0 - ADAM_B1 ** ADAM_STEP)
    v_hat = v / (1.0 - ADAM_B2 ** ADAM_STEP)
    return -ADAM_LR * (m_hat / (jnp.sqrt(v_hat) + ADAM_EPS) + ADAM_WD * w), m, v


def _reduce_adamw(recv, w, m, v, name):
    _, k, n = w.shape
    tr = max(t for t in range(16, 257, 16) if k % t == 0)

    def body(r_ref, w_ref, m_ref, v_ref, g_out, d_out, m_out, v_out):
        g = r_ref[0].astype(F32)
        for s in range(1, N_DEV):
            g = g + r_ref[s].astype(F32)
        g_out[...] = g
        d_out[...], m_out[...], v_out[...] = _adamw(w_ref[...], g, m_ref[...], v_ref[...])

    spec = pl.BlockSpec((None, tr, n), lambda i: (0, i, 0))
    return pl.pallas_call(
        body, name=name, grid=(k // tr,),
        out_shape=[jax.ShapeDtypeStruct((1, k, n), F32)] * 4,
        in_specs=[pl.BlockSpec((N_DEV, tr, n), lambda i: (0, i, 0)), spec, spec, spec],
        out_specs=[spec] * 4, compiler_params=_params("arbitrary"),
    )(recv, w, m, v)


def _small_adamw(gathered, w, m, v):
    def body(g_ref, w_ref, m_ref, v_ref, g_out, d_out, m_out, v_out, loss_out):
        tot = g_ref[0]
        for s in range(1, N_DEV):
            tot = tot + g_ref[s]
        g = tot[0:8]
        g_out[...] = g
        d_out[...], m_out[...], v_out[...] = _adamw(w_ref[...], g, m_ref[...], v_ref[...])
        loss_out[...] = jnp.broadcast_to((0.5 / D) * jnp.sum(tot[8:9], axis=1, keepdims=True), (8, BLK))

    out = [jax.ShapeDtypeStruct((8, D), F32)] * 4 + [jax.ShapeDtypeStruct((8, BLK), F32)]
    return pl.pallas_call(body, name="small_adamw", out_shape=out, compiler_params=_params())(gathered, w, m, v)


def _pick_chunks(w, chunks):
    return jnp.concatenate([w[:, BLK * c:BLK * (c + 1)] for c in chunks], axis=1)


def _whole_weight(gathered, i):
    _, k, n = gathered.shape
    if BY_ROWS[i]:
        return gathered.reshape(N_DEV * k, n)
    return gathered.transpose(1, 0, 2).reshape(k, N_DEV * n)


def _shard_parts(grad, i):
    if BY_ROWS[i]:
        return grad.reshape(N_DEV, grad.shape[0] // N_DEV, grad.shape[1])
    k, n8 = grad.shape
    return grad.reshape(k, N_DEV, n8 // N_DEV).transpose(1, 0, 2)


def kernel(x, mem, g_pre_mix, g_post_mix, g_pre_ffn, g_post_ffn, g_mem, w_in, w_mem_kv, w_br_sb, w_br_dil, w_br_mem, w_gate, b_gate, w_o, w_ffn_in, w_ffn_out, loss_target, m_g_pre_mix, m_g_post_mix, m_g_pre_ffn, m_g_post_ffn, m_g_mem, m_w_in, m_w_mem_kv, m_w_br_sb, m_w_br_dil, m_w_br_mem, m_w_gate, m_b_gate, m_w_o, m_w_ffn_in, m_w_ffn_out, v_g_pre_mix, v_g_post_mix, v_g_pre_ffn, v_g_post_ffn, v_g_mem, v_w_in, v_w_mem_kv, v_w_br_sb, v_w_br_dil, v_w_br_mem, v_w_gate, v_b_gate, v_w_o, v_w_ffn_in, v_w_ffn_out):
    bsz, seq, _ = x.shape
    tokens = bsz * seq
    xf, tgt, memf = x.reshape(tokens, D), loss_target.reshape(tokens, D), mem.reshape(bsz * MEM_LEN, D)
    big_w = [w_in, w_mem_kv, w_br_sb, w_br_dil, w_br_mem, w_gate, w_o, w_ffn_in, w_ffn_out]
    big_m = [m_w_in, m_w_mem_kv, m_w_br_sb, m_w_br_dil, m_w_br_mem, m_w_gate, m_w_o, m_w_ffn_in, m_w_ffn_out]
    big_v = [v_w_in, v_w_mem_kv, v_w_br_sb, v_w_br_dil, v_w_br_mem, v_w_gate, v_w_o, v_w_ffn_in, v_w_ffn_out]

    shards = [w[0].astype(BF16) for w in big_w]
    k_in, n_in = shards[0].shape
    fw_in = _whole_weight(_all_gather(shards[0], "weight_all_gather").reshape(N_DEV, k_in, n_in), 0)
    w_a, w_b = _pick_chunks(fw_in, CHUNKS_A), _pick_chunks(fw_in, CHUNKS_B)

    h = _norm_fwd(xf, g_pre_mix, "pre_mix_norm")
    proj_a = _matmul(h, w_a, "nn", BF16, "proj_a").reshape(bsz, seq, WA)
    proj_b = _matmul(h, w_b, "nn", BF16, "proj_b").reshape(bsz, seq, WB)
    o_a, o_a32, *behind = _sb_fwd(proj_a, seq, [shards[i] for i in GATHER_BEHIND])
    fw_mem_kv, fw_br_sb, fw_br_dil, fw_br_mem, fw_gate, fw_o, fw_ffn_in, fw_ffn_out = (
        _whole_weight(g, i) for g, i in zip(behind, GATHER_BEHIND))
    gpre = _matmul(h, fw_gate, "nn", BF16, "gate_proj")
    cos_t, sin_t = _rope_tables(seq)
    o_b, lse_b = _dil_fwd(proj_b, cos_t, sin_t, seq)
    mn = _norm_fwd(memf, g_mem, "mem_norm")
    kv = _matmul(mn, fw_mem_kv, "nn", BF16, "mem_kv_proj").reshape(bsz, MEM_LEN, D)
    o_c = _mem_fwd(proj_a, kv, seq)
    o_a2, o_b2, o_c2 = o_a.reshape(tokens, 512), o_b.reshape(tokens, 256), o_c.reshape(tokens, 512)
    ys = [_matmul(o_a2, fw_br_sb, "nn", BF16, "branch_sb"), _matmul(o_b2, fw_br_dil, "nn", BF16, "branch_dil"),
          _matmul(o_c2, fw_br_mem, "nn", BF16, "branch_mem")]
    merged, mix, x1, h2 = _merge_out_proj_norm(gpre, ys, b_gate, fw_o, xf, g_post_mix, g_pre_ffn)
    gu_a, gu_b, f = _ffn_in_swiglu(h2, fw_ffn_in)
    dy, dfo, dg_post_ffn, loss_lanes = _ffn_out_loss(f, fw_ffn_out, x1, tgt, g_post_ffn)

    gw_ffn_out = _matmul(f, dfo, "tn", BF16, "gw_ffn_out")
    dgu = _d_ffn_swiglu_bwd(dfo, fw_ffn_out, gu_a, gu_b)
    gw_ffn_in = _matmul(h2, dgu, "tn", BF16, "gw_ffn_in")
    dx1, dmix, dg_pre_ffn, dg_post_mix = _d_h2_norm_bwd(dgu, fw_ffn_in, x1, dy, mix, g_pre_ffn, g_post_mix)
    gw_o = _matmul(merged, dmix, "tn", BF16, "gw_o")
    dya, dyb, dyc, dgpre, db_gate = _d_merged_gate_bwd(dmix, fw_o, gpre, ys, b_gate)
    d_oa = _matmul(dya, fw_br_sb, "nt", BF16, "d_o_sb").reshape(bsz, seq, 512)
    d_ob = _matmul(dyb, fw_br_dil, "nt", BF16, "d_o_dil").reshape(bsz, seq, 256)
    d_oc = _matmul(dyc, fw_br_mem, "nt", BF16, "d_o_mem").reshape(bsz, seq, 512)
    gw_br_sb = _matmul(o_a2, dya, "tn", BF16, "gw_br_sb")
    gw_br_dil = _matmul(o_b2, dyb, "tn", BF16, "gw_br_dil")
    gw_br_mem = _matmul(o_c2, dyc, "tn", BF16, "gw_br_mem")
    gw_gate = _matmul(h, dgpre, "tn", BF16, "gw_gate")
    grads = {2: gw_br_sb, 3: gw_br_dil, 4: gw_br_mem, 5: gw_gate, 6: gw_o, 7: gw_ffn_in, 8: gw_ffn_out}
    d_proj_a, *recv_behind = _sb_bwd(proj_a, d_oa, o_a32, seq, [_shard_parts(grads[i], i) for i in REDUCE_BEHIND])
    d_proj_a, dk_m, dv_m = _mem_bwd(proj_a, kv, d_oc, d_proj_a, seq)
    d_proj_b = _dil_bwd(proj_b, cos_t, sin_t, d_ob, o_b, lse_b, seq).reshape(tokens, WB)
    d_proj_a = d_proj_a.reshape(tokens, WA)
    gw_a = _matmul(h, d_proj_a, "tn", BF16, "gw_in_a")
    gw_b = _matmul(h, d_proj_b, "tn", BF16, "gw_in_b")
    dkv = jnp.concatenate([dk_m, dv_m], axis=-1).reshape(bsz * MEM_LEN, D)
    gw_mem_kv = _matmul(mn, dkv, "tn", BF16, "gw_mem_kv")
    dmn = _matmul(dkv, fw_mem_kv, "nt", F32, "d_mem_norm")
    dg_mem = _gain_grad(dmn, memf)
    gw_ab = jnp.concatenate([gw_a, gw_b], axis=1)
    where = {c: i for i, c in enumerate(CHUNKS_A + CHUNKS_B)}
    grads = {0: _pick_chunks(gw_ab, [where[c] for c in range(34)]), 1: gw_mem_kv}
    dx, dg_pre_mix, *recv_last = _d_h_norm_bwd(
        [(dgpre, fw_gate), (d_proj_a, w_a), (d_proj_b, w_b)], xf, dx1, g_pre_mix,
        [_shard_parts(grads[i], i) for i in REDUCE_LAST])

    received = dict(zip(REDUCE_BEHIND + REDUCE_LAST, [*recv_behind, *recv_last]))
    adam = [_reduce_adamw(received[i], big_w[i], big_m[i], big_v[i], "reduce_adamw_" + BIG_NAMES[i])
            for i in range(len(big_w))]
    big = [[a[k] for a in adam] for k in range(4)]

    small = jnp.concatenate([dg_pre_mix, dg_post_mix, dg_pre_ffn, dg_post_ffn, dg_mem, db_gate.reshape(3, D),
                             loss_lanes, jnp.zeros((7, D), F32)], axis=0)
    small_all, = _exchange([small], True, "small_all_gather")

    def small_pack(gs, b):
        return jnp.concatenate([*gs, b.reshape(3, D)], axis=0)

    sm = _small_adamw(
        small_all, small_pack([g_pre_mix, g_post_mix, g_pre_ffn, g_post_ffn, g_mem], b_gate),
        small_pack([m_g_pre_mix, m_g_post_mix, m_g_pre_ffn, m_g_post_ffn, m_g_mem], m_b_gate),
        small_pack([v_g_pre_mix, v_g_post_mix, v_g_pre_ffn, v_g_post_ffn, v_g_mem], v_b_gate))
    loss = sm[4][0, 0]

    def leaves(k):
        t, bw = sm[k], big[k]
        return [t[0:1], t[1:2], t[2:3], t[3:4], t[4:5], *bw[0:6], t[5:8].reshape(1, 3 * D), *bw[6:9]]

    return (loss, dx.reshape(bsz, seq, D), *leaves(0), *leaves(1), *leaves(2), *leaves(3))
```

```python
import functools

import jax
import jax.numpy as jnp
from jax import lax
from jax.experimental import pallas as pl
from jax.experimental.pallas import tpu as pltpu

F32 = jnp.float32
BF16 = jnp.bfloat16
D = 1024
BLK = 128
MEM_LEN = 256
D_FF = 2816
NORM_EPS = 1e-6
NEG_INF = -1e30
ROPE_THETA = 10000.0
ADAM_LR, ADAM_B1, ADAM_B2, ADAM_EPS, ADAM_WD, ADAM_STEP = 0.001, 0.9, 0.999, 1e-08, 0.01, 10
N_DEV = 8
VMEM_LIMIT_BYTES = 56 * 1024 * 1024
MESH = pl.DeviceIdType.MESH
ANY = pl.BlockSpec(memory_space=pl.ANY)

NT = (((1,), (1,)), ((), ()))
TN = (((0,), (0,)), ((), ()))
NN = (((1,), (0,)), ((), ()))
_DIMS = {"nn": NN, "nt": NT, "tn": TN}

BIG_NAMES = ("w_in", "w_mem_kv", "w_br_sb", "w_br_dil", "w_br_mem", "w_gate", "w_o", "w_ffn_in", "w_ffn_out")
BY_ROWS = (False, True, False, False, False, False, True, False, True)
GATHER_FIRST = (0,)
GATHER_BEHIND = (1, 2, 3, 4, 5, 6, 7, 8)
REDUCE_BEHIND = (2, 3, 4, 5, 6, 7, 8)
REDUCE_LAST = (0, 1)

CHUNKS_A = tuple(c for hp in range(4) for c in (hp, 4 + hp, 8 + hp)) + (30, 31, 32, 33)
CHUNKS_B = tuple(c for hp in range(2) for g in range(3) for c in (12 + 6 * g + hp, 14 + 6 * g + hp, 16 + 6 * g + hp))
WA, WB = 128 * len(CHUNKS_A), 128 * len(CHUNKS_B)
DIL_GROUPS = (1, 4, 16)


def _params(*sem):
    return pltpu.CompilerParams(dimension_semantics=sem or None, vmem_limit_bytes=VMEM_LIMIT_BYTES)


def _tile(n, cap):
    if n <= 128:
        return n
    assert n % 128 == 0, n
    best = 128
    for t in range(128, min(n, cap) + 1, 128):
        if n % t == 0:
            best = t
    return best


def _k_steps(k, nk, step):
    if nk == 1:
        step(True, True)
        return
    pl.when(k == 0)(functools.partial(step, True, False))
    if nk > 2:
        pl.when(jnp.logical_and(k > 0, k < nk - 1))(functools.partial(step, False, False))
    pl.when(k == nk - 1)(functools.partial(step, False, True))


def _matmul(a, b, mode, out_dtype, name, tm_cap=1536, tn_cap=1536, tk_cap=1536):
    if mode == "tn":
        (K, M), N = a.shape, b.shape[1]
    elif mode == "nt":
        (M, K), N = a.shape, b.shape[0]
    else:
        (M, K), N = a.shape, b.shape[1]
    if mode == "tn":
        tk_cap = 2 * tk_cap
    tm, tn, tk = _tile(M, tm_cap), _tile(N, tn_cap), _tile(K, tk_cap)
    nm, nn, nk = M // tm, N // tn, K // tk
    dims = _DIMS[mode]

    def body(a_ref, b_ref, o_ref, *acc):
        def step(first, last):
            d = lax.dot_general(a_ref[...], b_ref[...], dims, preferred_element_type=F32)
            if not first:
                d = d + acc[0][...]
            if last:
                o_ref[...] = d.astype(o_ref.dtype)
            else:
                acc[0][...] = d

        _k_steps(pl.program_id(2), nk, step)

    n_outer = nk == 1 and (a.size * nn + b.size) < (a.size + b.size * nm)
    if n_outer:
        grid, ij = (nn, nm, nk), (lambda g0, g1: (g1, g0))
    else:
        grid, ij = (nm, nn, nk), (lambda g0, g1: (g0, g1))
    if mode == "tn":
        a_spec = pl.BlockSpec((tk, tm), lambda g0, g1, k: (k, ij(g0, g1)[0]))
    else:
        a_spec = pl.BlockSpec((tm, tk), lambda g0, g1, k: (ij(g0, g1)[0], k))
    if mode == "nt":
        b_spec = pl.BlockSpec((tn, tk), lambda g0, g1, k: (ij(g0, g1)[1], k))
    else:
        b_spec = pl.BlockSpec((tk, tn), lambda g0, g1, k: (k, ij(g0, g1)[1]))
    return pl.pallas_call(
        body, name=name, grid=grid,
        out_shape=jax.ShapeDtypeStruct((M, N), out_dtype),
        in_specs=[a_spec, b_spec],
        out_specs=pl.BlockSpec((tm, tn), lambda g0, g1, k: ij(g0, g1)),
        scratch_shapes=[pltpu.VMEM((tm, tn), F32)] if nk > 1 else [],
        compiler_params=_params("parallel", "parallel", "arbitrary"),
    )(a, b)


def _rowwise(body, name, rows, tr, row_ins, vec_ins, row_outs, acc_outs=()):
    tr = min(tr, rows)
    assert rows % tr == 0
    in_specs, args = [], []
    for r in row_ins:
        arr, w, cb = r if isinstance(r, tuple) else (r, r.shape[1], 0)
        in_specs.append(pl.BlockSpec((tr, w), functools.partial(lambda i, cb: (i, cb), cb=cb)))
        args.append(arr)
    for v in vec_ins:
        in_specs.append(pl.BlockSpec(v.shape, lambda i: (0, 0)))
        args.append(v)
    out_shape = [jax.ShapeDtypeStruct((rows, w), dt) for w, dt in row_outs]
    out_shape += [jax.ShapeDtypeStruct((1, w), F32) for w in acc_outs]
    out_specs = [pl.BlockSpec((tr, w), lambda i: (i, 0)) for w, _ in row_outs]
    out_specs += [pl.BlockSpec((1, w), lambda i: (0, 0)) for w in acc_outs]
    n_acc = len(acc_outs)

    def wrapped(*refs):
        if n_acc:
            @pl.when(pl.program_id(0) == 0)
            def _():
                for r in refs[len(refs) - n_acc:]:
                    r[...] = jnp.zeros_like(r)
        body(*refs)

    return pl.pallas_call(
        wrapped, name=name, grid=(rows // tr,), out_shape=out_shape, in_specs=in_specs, out_specs=out_specs,
        compiler_params=_params("arbitrary"),
    )(*args)


def _rstd(x):
    return lax.rsqrt(jnp.mean(x * x, axis=-1, keepdims=True) + NORM_EPS)


def _norm_bwd(u, n, r):
    return r * (u - n * jnp.mean(u * n, axis=-1, keepdims=True))


def _colsum(v):
    return jnp.sum(v, axis=0, keepdims=True)


def _norm_fwd(x, g, name):
    def body(x_ref, g_ref, h_ref):
        xv = x_ref[...]
        h_ref[...] = ((xv * _rstd(xv)) * g_ref[...]).astype(BF16)

    return _rowwise(body, name, x.shape[0], 512, [x], [g], [(D, BF16)])[0]


def _matmul_rows(pairs, mode, name, epilogue, row_ins=(), vec_ins=(), row_outs=(), acc_outs=(), ride=None,
                 tm=512, tk_cap=1536, epi_rows=None):
    M = pairs[0][0].shape[0]
    N = pairs[0][1].shape[1] if mode == "nn" else pairs[0][1].shape[0]
    tm = min(tm, M)
    tks = [_tile(a.shape[1], tk_cap) for a, _ in pairs]
    nks = [a.shape[1] // tk for (a, _), tk in zip(pairs, tks)]
    offs = [sum(nks[:p]) for p in range(len(pairs))]
    nm, nk = M // tm, sum(nks)
    dims = _DIMS[mode]
    n_ab, n_extra, n_out = 2 * len(pairs), len(row_ins) + len(vec_ins), len(row_outs) + len(acc_outs)
    n_ride = 0 if ride is None else len(ride)

    def body(*refs):
        ab, extra, rest = refs[:n_ab], refs[n_ab:n_ab + n_extra], refs[n_ab + n_extra:]
        ride_refs, outs, rest = rest[:n_ride], rest[n_ride:n_ride + n_out], rest[n_ride + n_out:]
        received_refs, rest = rest[:n_ride], rest[n_ride:]
        if n_ride:
            finish_ride = _riding_exchange(ride_refs, received_refs, rest[len(rest) - 3:], gather=False)
        i, k = pl.program_id(0), pl.program_id(1)
        if acc_outs:
            @pl.when(jnp.logical_and(i == 0, k == 0))
            def _():
                for r in outs[len(row_outs):]:
                    r[...] = jnp.zeros_like(r)

        def step(p, first, last):
            d = lax.dot_general(ab[2 * p][...], ab[2 * p + 1][...], dims, preferred_element_type=F32)
            if not first:
                d = d + rest[0][...]
            if not last:
                rest[0][...] = d
            elif epi_rows is None:
                epilogue(d, *extra, *outs)
            else:
                rest[0][...] = d
                for c in range(tm // epi_rows):
                    rows = pl.ds(c * epi_rows, epi_rows)
                    sliced = [r.at[rows] for r in extra[:len(row_ins)]] + list(extra[len(row_ins):])
                    sliced += [r.at[rows] for r in outs[:len(row_outs)]] + list(outs[len(row_outs):])
                    epilogue(rest[0][rows, :], *sliced)

        last_p = len(pairs) - 1
        if nk == 1:
            step(0, True, True)
        else:
            pl.when(k == 0)(functools.partial(step, 0, True, False))
            for p in range(len(pairs)):
                lo, hi = max(offs[p], 1), min(offs[p] + nks[p], nk - 1)
                if hi > lo:
                    pl.when(jnp.logical_and(k >= lo, k < hi))(functools.partial(step, p, False, False))
            pl.when(k == nk - 1)(functools.partial(step, last_p, False, True))
        if n_ride:
            finish_ride()

    in_specs, args = [], []
    for p, ((a, b), tk) in enumerate(zip(pairs, tks)):
        step = functools.partial(lambda k, p: jnp.clip(k - offs[p], 0, nks[p] - 1), p=p)
        in_specs.append(pl.BlockSpec((tm, tk), functools.partial(lambda i, k, step: (i, step(k)), step=step)))
        if mode == "nn":
            in_specs.append(pl.BlockSpec((tk, N), functools.partial(lambda i, k, step: (step(k), 0), step=step)))
        else:
            in_specs.append(pl.BlockSpec((N, tk), functools.partial(lambda i, k, step: (0, step(k)), step=step)))
        args += [a, b]
    in_specs += [pl.BlockSpec((tm, r.shape[1]), lambda i, k: (i, 0)) for r in row_ins]
    in_specs += [pl.BlockSpec(v.shape, lambda i, k: (0, 0)) for v in vec_ins]
    in_specs += [ANY] * n_ride
    out_shape = [jax.ShapeDtypeStruct((M, w), dt) for w, dt in row_outs]
    out_shape += [jax.ShapeDtypeStruct((1, w), F32) for w in acc_outs]
    out_specs = [pl.BlockSpec((tm, w), lambda i, k: (i, 0)) for w, _ in row_outs]
    out_specs += [pl.BlockSpec((1, w), lambda i, k: (0, 0)) for w in acc_outs]
    scratch = [pltpu.VMEM((tm, N), F32)] if nk > 1 else []
    if n_ride:
        out_shape += _exchange_out(ride, False)
        out_specs += [ANY] * n_ride
        scratch += _exchange_sems(n_ride)
    return pl.pallas_call(
        body, name=name, grid=(nm, nk), out_shape=out_shape, in_specs=in_specs, out_specs=out_specs,
        scratch_shapes=scratch, compiler_params=_params("arbitrary", "arbitrary"),
    )(*args, *row_ins, *vec_ins, *(ride or []))


def _merge_out_proj_norm(gpre, ys, b_gate, w_o, x, g_post, g_pre):
    tokens = x.shape[0]
    tm = min(512, tokens)

    def body(gp_ref, ya_ref, yb_ref, yc_ref, b_ref, w_ref, x_ref, g2_ref, g3_ref, m_ref, mix_ref, x1_ref, h2_ref):
        acc = None
        for k, y_ref in enumerate((ya_ref, yb_ref, yc_ref)):
            cols = slice(k * D, (k + 1) * D)
            gate = jax.nn.sigmoid(gp_ref[:, cols].astype(F32) + b_ref[:, cols])
            term = gate * y_ref[...].astype(F32)
            acc = term if acc is None else acc + term
        merged = acc.astype(BF16)
        m_ref[...] = merged
        mv = jnp.dot(merged, w_ref[...], preferred_element_type=F32)
        mix_ref[...] = mv
        x1 = x_ref[...] + (mv * _rstd(mv)) * g2_ref[...]
        x1_ref[...] = x1
        h2_ref[...] = ((x1 * _rstd(x1)) * g3_ref[...]).astype(BF16)

    def rows(w):
        return pl.BlockSpec((tm, w), lambda i: (i, 0))

    def whole(a):
        return pl.BlockSpec(a.shape, lambda i: (0, 0))

    return pl.pallas_call(
        body, name="merge_out_proj_norm", grid=(tokens // tm,),
        out_shape=[jax.ShapeDtypeStruct((tokens, D), dt) for dt in (BF16, F32, F32, BF16)],
        in_specs=[rows(3 * D), rows(D), rows(D), rows(D), whole(b_gate), whole(w_o), rows(D), whole(g_post),
                  whole(g_pre)],
        out_specs=[rows(D)] * 4, compiler_params=_params("parallel"),
    )(gpre, *ys, b_gate, w_o, x, g_post, g_pre)


def _ffn_in_swiglu(h2, w_ffn_in):
    tokens = h2.shape[0]
    tm, tn = min(512, tokens), _tile(D_FF, 1536)
    nj = D_FF // tn

    def body(h_ref, wa_ref, wb_ref, a_ref, b_ref, f_ref):
        hv = h_ref[...]
        a = jnp.dot(hv, wa_ref[...], preferred_element_type=F32)
        b = jnp.dot(hv, wb_ref[...], preferred_element_type=F32)
        a_ref[...] = a.astype(BF16)
        b_ref[...] = b.astype(BF16)
        f_ref[...] = (a * jax.nn.sigmoid(a) * b).astype(BF16)

    out = jax.ShapeDtypeStruct((tokens, D_FF), BF16)
    o_spec = pl.BlockSpec((tm, tn), lambda j, i: (i, j))
    return pl.pallas_call(
        body, name="ffn_in_swiglu", grid=(nj, tokens // tm), out_shape=(out, out, out),
        in_specs=[pl.BlockSpec((tm, D), lambda j, i: (i, 0)), pl.BlockSpec((D, tn), lambda j, i: (0, j)),
                  pl.BlockSpec((D, tn), lambda j, i: (0, j + nj))],
        out_specs=(o_spec, o_spec, o_spec), compiler_params=_params("parallel", "parallel"),
    )(h2, w_ffn_in, w_ffn_in)


def _ffn_out_loss(f, w_ffn_out, x1, tgt, g_post):
    def epilogue(fo_v, x1_ref, t_ref, g_ref, dy_ref, dfo_ref, dg_ref, loss_ref):
        r = _rstd(fo_v)
        n = fo_v * r
        err = (x1_ref[...] + n * g_ref[...]) - t_ref[...]
        loss_ref[...] += _colsum(err * err)
        dy = err * (1.0 / D)
        dy_ref[...] = dy
        dg_ref[...] += _colsum(dy * n)
        dfo_ref[...] = _norm_bwd(dy * g_ref[...], n, r).astype(BF16)

    return _matmul_rows([(f, w_ffn_out)], "nn", "ffn_out_loss", epilogue, [x1, tgt], [g_post],
                        [(D, F32), (D, BF16)], (D, D))


def _d_ffn_swiglu_bwd(dfo, w_ffn_out, gu_a, gu_b):
    def epilogue(d, a_ref, b_ref, dgu_ref):
        a = a_ref[...].astype(F32)
        b = b_ref[...].astype(F32)
        s = jax.nn.sigmoid(a)
        dgu_ref[:, :D_FF] = (d * b * (s * (1.0 + a * (1.0 - s)))).astype(BF16)
        dgu_ref[:, D_FF:] = (d * (a * s)).astype(BF16)

    return _matmul_rows([(dfo, w_ffn_out)], "nt", "d_ffn_swiglu_bwd", epilogue, [gu_a, gu_b], [],
                        [(2 * D_FF, BF16)], tm=256)[0]


def _d_h2_norm_bwd(dgu, w_ffn_in, x1, dy, mix, g_pre, g_post):
    def epilogue(dh, x1_ref, dy_ref, mix_ref, g3_ref, g2_ref, dx1_ref, dmix_ref, dg3_ref, dg2_ref):
        x1v = x1_ref[...]
        r3 = _rstd(x1v)
        n3 = x1v * r3
        dg3_ref[...] += _colsum(dh * n3)
        dx1 = dy_ref[...] + _norm_bwd(dh * g3_ref[...], n3, r3)
        dx1_ref[...] = dx1
        mv = mix_ref[...]
        r2 = _rstd(mv)
        n2 = mv * r2
        dg2_ref[...] += _colsum(dx1 * n2)
        dmix_ref[...] = _norm_bwd(dx1 * g2_ref[...], n2, r2).astype(BF16)

    return _matmul_rows([(dgu, w_ffn_in)], "nt", "d_h2_norm_bwd", epilogue, [x1, dy, mix], [g_pre, g_post],
                        [(D, F32), (D, BF16)], (D, D))


def _d_merged_gate_bwd(dmix, w_o, gpre, ys, b_gate):
    def epilogue(dm, gp_ref, ya_ref, yb_ref, yc_ref, b_ref, dya_ref, dyb_ref, dyc_ref, dgp_ref, db_ref):
        for k, (y_ref, dy_ref) in enumerate(((ya_ref, dya_ref), (yb_ref, dyb_ref), (yc_ref, dyc_ref))):
            cols = slice(k * D, (k + 1) * D)
            gate = jax.nn.sigmoid(gp_ref[:, cols].astype(F32) + b_ref[:, cols])
            dy_ref[...] = (dm * gate).astype(BF16)
            dgp = (dm * y_ref[...].astype(F32)) * (gate * (1.0 - gate))
            dgp_ref[:, cols] = dgp.astype(BF16)
            db_ref[:, cols] += _colsum(dgp)

    return _matmul_rows([(dmix, w_o)], "nt", "d_merged_gate_bwd", epilogue, [gpre, *ys], [b_gate],
                        [(D, BF16), (D, BF16), (D, BF16), (3 * D, BF16)], (3 * D,))


def _d_h_norm_bwd(pairs, x, dx1, g_pre, ride):
    def epilogue(dh, x_ref, dx1_ref, g_ref, dx_ref, dg_ref):
        xv = x_ref[...]
        r = _rstd(xv)
        n = xv * r
        dg_ref[...] += _colsum(dh * n)
        dx_ref[...] = dx1_ref[...] + _norm_bwd(dh * g_ref[...], n, r)

    return _matmul_rows(pairs, "nt", "d_h_norm_bwd", epilogue, [x, dx1], [g_pre], [(D, F32)], (D,), ride=ride,
                        tm=1024, tk_cap=512, epi_rows=256)


def _gain_grad(dmn, mem):
    def body(d_ref, m_ref, dg_ref):
        mv = m_ref[...]
        dg_ref[...] += _colsum(d_ref[...] * (mv * _rstd(mv)))

    return _rowwise(body, "mem_gain_grad", mem.shape[0], 256, [dmn, mem], [], [], (D,))[0]


def _head_rowsum(v, head0):
    return (jnp.sum(jnp.where(head0, v, 0.0), axis=1, keepdims=True),
            jnp.sum(jnp.where(head0, 0.0, v), axis=1, keepdims=True))


KT = 256
SB_SCALE = 0.125


def _make_suffix(inclusive):
    row, col = lax.broadcasted_iota(jnp.int32, (KT, KT), 0), lax.broadcasted_iota(jnp.int32, (KT, KT), 1)
    tri = (row >= col if inclusive else row > col).astype(BF16)
    tri2 = jnp.concatenate([tri, tri], axis=0)

    def suffix(x):
        hi = x.astype(BF16)
        lo = (x - hi.astype(F32)).astype(BF16)
        return jnp.dot(jnp.concatenate([hi, lo], axis=1), tri2, preferred_element_type=F32)

    return suffix


def _sb_scores(qh, k, mask, suffix_incl, later):
    z = lax.dot_general(qh, k, NT, preferred_element_type=F32)
    zc = jnp.minimum(z, 60.0)
    sp = jnp.log(1.0 + jnp.exp(zc))
    if mask is not None:
        sp = jnp.where(mask, sp, 0.0)
    a = jnp.exp((zc - suffix_incl(sp)) - later)
    if mask is not None:
        a = jnp.where(mask, a, 0.0)
    return zc, sp, a


QB = KT


def _sb_tiles(i, tile, init):
    st = lax.cond(i > 0, lambda s: tile(i - 1, tile(i, s, True), False), lambda s: tile(i, s, True), init)
    rest = jnp.maximum(i - 1, 0)
    st = lax.fori_loop(0, lax.shift_right_logical(rest, 1),
                       lambda t, s: tile(rest - 2 - 2 * t, tile(rest - 1 - 2 * t, s, False), False), st)
    return lax.cond((rest & 1) == 1, lambda s: tile(0, s, False), lambda s: s, st)


def _sb_consts():
    head0 = lax.broadcasted_iota(jnp.int32, (QB, BLK), 1) < 64
    row = lax.broadcasted_iota(jnp.int32, (2 * QB, KT), 0) & (QB - 1)
    return head0, row > lax.broadcasted_iota(jnp.int32, (2 * QB, KT), 1)


def _stack_heads(v, head0):
    zero = jnp.zeros_like(v)
    return jnp.concatenate([jnp.where(head0, v, zero), jnp.where(head0, zero, v)], axis=0)


def _unstack_heads(v, head0):
    n = v.shape[0] // 2
    return jnp.where(head0, v[:n], v[n:])


def _sb_fwd(proj_a, seq, ride):
    bsz = proj_a.shape[0]
    n_ride = len(ride)

    def body(x_ref, *rest):
        ride_refs, (o_ref, o32_ref), rest = rest[:n_ride], rest[n_ride:n_ride + 2], rest[n_ride + 2:]
        gathered_refs, acc_ref, sems = rest[:n_ride], rest[n_ride], rest[n_ride + 1:]
        finish_ride = _riding_exchange(ride_refs, gathered_refs, sems, gather=True)
        head0, diag_mask = _sb_consts()
        suffix_incl = _make_suffix(True)

        def qblock(i, carry):
            r0 = pl.multiple_of(i * QB, QB)
            qs = _stack_heads(x_ref[pl.ds(r0, QB), 0:128] * jnp.asarray(SB_SCALE, BF16), head0)

            def tile(jt, later, masked):
                c0 = pl.multiple_of(jt * KT, KT)
                k = x_ref[pl.ds(c0, KT), 128:256]
                v = x_ref[pl.ds(c0, KT), 256:384]
                _, sp, a = _sb_scores(qs, k, diag_mask if masked else None, suffix_incl, later)
                pv = jnp.dot(a.astype(BF16), v, preferred_element_type=F32)
                if masked:
                    acc_ref[...] = pv
                else:
                    acc_ref[...] += pv
                return later + jnp.sum(sp, axis=1, keepdims=True)

            _sb_tiles(i, tile, jnp.zeros((2 * QB, 1), F32))
            o = _unstack_heads(acc_ref[...], head0)
            o32_ref[pl.ds(r0, QB), :] = o
            o_ref[pl.ds(r0, QB), :] = o.astype(BF16)
            return carry

        lax.fori_loop(0, seq // QB, qblock, 0)
        finish_ride()

    out_spec = pl.BlockSpec((None, seq, BLK), lambda b, hp: (b, 0, hp))
    return pl.pallas_call(
        body, name="sb_attn_fwd", grid=(bsz, 4),
        out_shape=[jax.ShapeDtypeStruct((bsz, seq, 512), BF16), jax.ShapeDtypeStruct((bsz, seq, 512), F32),
                   *_exchange_out(ride, True)],
        in_specs=[pl.BlockSpec((None, seq, 384), lambda b, hp: (b, 0, hp))] + [ANY] * n_ride,
        out_specs=[out_spec, out_spec] + [ANY] * n_ride,
        scratch_shapes=[pltpu.VMEM((2 * QB, BLK), F32), *_exchange_sems(n_ride)],
        compiler_params=_params("arbitrary", "arbitrary"),
    )(proj_a, *ride)


def _sb_bwd(proj_a, d_o, o_a, seq, ride):
    bsz = proj_a.shape[0]
    n_ride = len(ride)

    def body(x_ref, do_ref, o_ref, *rest):
        ride_refs, d_ref, rest = rest[:n_ride], rest[n_ride], rest[n_ride + 1:]
        received_refs, (dq_acc, dk_acc, dv_acc), sems = rest[:n_ride], rest[n_ride:n_ride + 3], rest[n_ride + 3:]
        finish_ride = _riding_exchange(ride_refs, received_refs, sems, gather=False)
        head0, diag_mask = _sb_consts()
        suffix_incl, suffix_excl = _make_suffix(True), _make_suffix(False)
        dk_acc[...] = jnp.zeros_like(dk_acc)
        dv_acc[...] = jnp.zeros_like(dv_acc)

        def qblock(i, carry):
            r0 = pl.multiple_of(i * QB, QB)
            qs = _stack_heads(x_ref[pl.ds(r0, QB), 0:128] * jnp.asarray(SB_SCALE, BF16), head0)
            do = do_ref[pl.ds(r0, QB), :]
            dos = _stack_heads(do, head0)
            dsum = jnp.concatenate(_head_rowsum(do.astype(F32) * o_ref[pl.ds(r0, QB), :], head0), axis=0)

            def tile(jt, st, masked):
                later, rest_g = st
                c0 = pl.multiple_of(jt * KT, KT)
                k = x_ref[pl.ds(c0, KT), 128:256]
                v = x_ref[pl.ds(c0, KT), 256:384]
                zc, sp, a = _sb_scores(qs, k, diag_mask if masked else None, suffix_incl, later)
                a16 = a.astype(BF16)
                g = a16.astype(F32) * lax.dot_general(dos, v, NT, preferred_element_type=F32)
                dz = g - jnp.exp(zc - sp) * (rest_g - suffix_excl(g))
                if masked:
                    dz = jnp.where(diag_mask, dz, 0.0)
                dz = dz.astype(BF16)
                dq = jnp.dot(dz, k, preferred_element_type=F32)
                if masked:
                    dq_acc[...] = dq
                else:
                    dq_acc[...] += dq
                dk_acc[pl.ds(c0, KT), :] += lax.dot_general(dz, qs, TN, preferred_element_type=F32)
                dv_acc[pl.ds(c0, KT), :] += lax.dot_general(a16, dos, TN, preferred_element_type=F32)
                return later + jnp.sum(sp, axis=1, keepdims=True), rest_g - jnp.sum(g, axis=1, keepdims=True)

            _sb_tiles(i, tile, (jnp.zeros((2 * QB, 1), F32), dsum))
            d_ref[pl.ds(r0, QB), 0:128] = (_unstack_heads(dq_acc[...], head0) * SB_SCALE).astype(BF16)
            return carry

        lax.fori_loop(0, seq // QB, qblock, 0)
        d_ref[:, 128:256] = dk_acc[...].astype(BF16)
        d_ref[:, 256:384] = dv_acc[...].astype(BF16)
        finish_ride()

    return pl.pallas_call(
        body, name="sb_attn_bwd", grid=(bsz, 4),
        out_shape=[jax.ShapeDtypeStruct((bsz, seq, WA), BF16), *_exchange_out(ride, False)],
        in_specs=[pl.BlockSpec((None, seq, 384), lambda b, hp: (b, 0, hp)),
                  pl.BlockSpec((None, seq, BLK), lambda b, hp: (b, 0, hp)),
                  pl.BlockSpec((None, seq, BLK), lambda b, hp: (b, 0, hp))] + [ANY] * n_ride,
        out_specs=[pl.BlockSpec((None, seq, 384), lambda b, hp: (b, 0, hp))] + [ANY] * n_ride,
        scratch_shapes=[pltpu.VMEM((2 * QB, BLK), F32), pltpu.VMEM((seq, BLK), F32), pltpu.VMEM((seq, BLK), F32),
                        *_exchange_sems(n_ride)],
        compiler_params=_params("arbitrary", "arbitrary"),
    )(proj_a, d_o, o_a, *ride)


def _rope_tables(seq):
    inv_freq = ROPE_THETA ** (-jnp.arange(32, dtype=F32) * 2.0 / 64)
    ang = jnp.arange(seq).astype(F32)[:, None] * inv_freq[None, :]
    cos, sin = jnp.cos(ang), jnp.sin(ang)
    return jnp.tile(cos, (1, 4)), jnp.concatenate([-sin, sin, -sin, sin], axis=1)


def _make_rope(n_rows):
    lane = lax.broadcasted_iota(jnp.int32, (n_rows, BLK), 1)
    first = (lane & 63) < 32

    def rope(x, cos, sin):
        partner = jnp.where(first, pltpu.roll(x, 96, 1), pltpu.roll(x, 32, 1))
        return x * cos + partner * sin

    return rope


DIL_UNROLL = 8


def _dil_consts():
    head0 = lax.broadcasted_iota(jnp.int32, (BLK, BLK), 1) < 64
    row = lax.broadcasted_iota(jnp.int32, (2 * BLK, 2 * BLK), 0) & (BLK - 1)
    col = lax.broadcasted_iota(jnp.int32, (2 * BLK, 2 * BLK), 1)
    valid_prev = jnp.logical_and(col < BLK, col >= row)
    valid_cur = jnp.logical_and(col >= BLK, row >= col - BLK)
    return head0, valid_prev, valid_cur


def _dil_blocks(dil, seq, block):
    nq = seq // dil // BLK

    def rows(r, i):
        if dil == 1:
            return pl.ds(pl.multiple_of(i * BLK, BLK), BLK)
        return pl.ds(r + (dil * BLK) * i, BLK, stride=dil)

    def step(t, carry):
        for u in range(DIL_UNROLL):
            n = t * DIL_UNROLL + u
            r, i = lax.div(n, nq), lax.rem(n, nq)
            block(rows(r, i), rows(r, jnp.maximum(i - 1, 0)), i)
        return carry

    lax.fori_loop(0, seq // BLK // DIL_UNROLL, step, 0)


def _dil_scores(qf, kf, vf, cur, prev, i, consts):
    head0, valid_prev, valid_cur = consts
    qs = _stack_heads(qf[cur, :].astype(BF16), head0)
    kcat = jnp.concatenate([kf[prev, :], kf[cur, :]], axis=0).astype(BF16)
    vcat = jnp.concatenate([vf[prev, :], vf[cur, :]], axis=0).astype(BF16)
    valid = jnp.logical_or(valid_cur, jnp.logical_and(valid_prev, i > 0))
    s = lax.dot_general(qs, kcat, NT, preferred_element_type=F32) * 0.125
    return qs, kcat, vcat, s, valid


def _head_cols(v):
    return jnp.concatenate([v[:, 0:1], v[:, 64:65]], axis=0)


def _dil_load_qkv(x_ref, c, rope, cos, sin, qf, kf, vf):
    qf[...] = rope(x_ref[:, c:c + 128].astype(F32), cos, sin).astype(BF16).astype(F32)
    kf[...] = rope(x_ref[:, c + 128:c + 256].astype(F32), cos, sin).astype(BF16).astype(F32)
    vf[...] = x_ref[:, c + 256:c + 384].astype(F32)


def _dil_fwd(proj_b, cos_t, sin_t, seq):
    bsz = proj_b.shape[0]

    def body(x_ref, cos_ref, sin_ref, ob_ref, lse_ref, qf, kf, vf, og, lg):
        consts = _dil_consts()
        head0 = consts[0]
        rope = _make_rope(seq)
        cos, sin = cos_ref[...], sin_ref[...]
        for g, dil in enumerate(DIL_GROUPS):
            _dil_load_qkv(x_ref, 384 * g, rope, cos, sin, qf, kf, vf)

            def block(cur, prev, i, g=g):
                _, _, vcat, s, valid = _dil_scores(qf, kf, vf, cur, prev, i, consts)
                s = jnp.where(valid, s, NEG_INF)
                m = jnp.max(s, axis=1, keepdims=True)
                p = jnp.exp(s - m)
                den = jnp.sum(p, axis=1, keepdims=True)
                o = jnp.dot(p.astype(BF16), vcat, preferred_element_type=F32) / den
                og[g, cur, :] = _unstack_heads(o, head0)
                lg[g, cur, :] = _unstack_heads(jnp.broadcast_to(m + jnp.log(den), (2 * BLK, BLK)), head0)

            _dil_blocks(dil, seq, block)
        ls = [lg[0], lg[1], lg[2]]
        m = jnp.maximum(jnp.maximum(ls[0], ls[1]), ls[2])
        ws = [jnp.exp(l - m) for l in ls]
        den = (ws[0] + ws[1]) + ws[2]
        ob_ref[...] = (((ws[0] * og[0] + ws[1] * og[1]) + ws[2] * og[2]) / den).astype(BF16)
        lse_ref[...] = m + jnp.log(den)

    tab_spec = pl.BlockSpec((seq, BLK), lambda b, hp: (0, 0))
    out_spec = pl.BlockSpec((None, seq, BLK), lambda b, hp: (b, 0, hp))
    slab = pltpu.VMEM((seq, BLK), F32)
    return pl.pallas_call(
        body, name="dil_attn_fwd", grid=(bsz, 2),
        out_shape=(jax.ShapeDtypeStruct((bsz, seq, 256), BF16), jax.ShapeDtypeStruct((bsz, seq, 256), F32)),
        in_specs=[pl.BlockSpec((None, seq, WB // 2), lambda b, hp: (b, 0, hp)), tab_spec, tab_spec],
        out_specs=(out_spec, out_spec),
        scratch_shapes=[slab, slab, slab, pltpu.VMEM((3, seq, BLK), F32), pltpu.VMEM((3, seq, BLK), F32)],
        compiler_params=_params("parallel", "parallel"),
    )(proj_b, cos_t, sin_t)


def _dil_bwd(proj_b, cos_t, sin_t, d_ob, o_b, lse, seq):
    bsz = proj_b.shape[0]

    def body(x_ref, cos_ref, sin_ref, do_ref, ob_ref, lse_ref, d_ref, qf, kf, vf, dof, dsf, dq_s, dk_acc, dv_acc):
        consts = _dil_consts()
        head0 = consts[0]
        rope = _make_rope(seq)
        cos, sin = cos_ref[...], sin_ref[...]
        do_all = do_ref[...].astype(F32)
        dof[...] = do_all
        head0_all = lax.broadcasted_iota(jnp.int32, (seq, BLK), 1) < 64
        d0, d1 = _head_rowsum(do_all * ob_ref[...].astype(F32), head0_all)
        dsf[...] = jnp.where(head0_all, d0, d1)
        for g, dil in enumerate(DIL_GROUPS):
            _dil_load_qkv(x_ref, 384 * g, rope, cos, sin, qf, kf, vf)
            dk_acc[...] = jnp.zeros_like(dk_acc)
            dv_acc[...] = jnp.zeros_like(dv_acc)

            def block(cur, prev, i):
                qs, kcat, vcat, s, valid = _dil_scores(qf, kf, vf, cur, prev, i, consts)
                dos = _stack_heads(dof[cur, :].astype(BF16), head0)
                p = jnp.where(valid, jnp.exp(s - _head_cols(lse_ref[cur, :])), 0.0)
                dp = lax.dot_general(dos, vcat, NT, preferred_element_type=F32)
                ds = ((p * (dp - _head_cols(dsf[cur, :]))) * 0.125).astype(BF16)
                dq_s[cur, :] = _unstack_heads(jnp.dot(ds, kcat, preferred_element_type=F32), head0)
                dk = lax.dot_general(ds, qs, TN, preferred_element_type=F32)
                dv = lax.dot_general(p.astype(BF16), dos, TN, preferred_element_type=F32)
                dk_acc[prev, :] += dk[:BLK]
                dk_acc[cur, :] += dk[BLK:]
                dv_acc[prev, :] += dv[:BLK]
                dv_acc[cur, :] += dv[BLK:]

            _dil_blocks(dil, seq, block)
            c = 384 * g
            d_ref[:, c:c + 128] = rope(dq_s[...], cos, -sin).astype(BF16)
            d_ref[:, c + 128:c + 256] = rope(dk_acc[...], cos, -sin).astype(BF16)
            d_ref[:, c + 256:c + 384] = dv_acc[...].astype(BF16)

    x_spec = pl.BlockSpec((None, seq, WB // 2), lambda b, hp: (b, 0, hp))
    tab_spec = pl.BlockSpec((seq, BLK), lambda b, hp: (0, 0))
    tok_spec = pl.BlockSpec((None, seq, BLK), lambda b, hp: (b, 0, hp))
    return pl.pallas_call(
        body, name="dil_attn_bwd", grid=(bsz, 2),
        out_shape=jax.ShapeDtypeStruct((bsz, seq, WB), BF16),
        in_specs=[x_spec, tab_spec, tab_spec, tok_spec, tok_spec, tok_spec], out_specs=x_spec,
        scratch_shapes=[pltpu.VMEM((seq, BLK), F32)] * 8,
        compiler_params=_params("parallel", "parallel"),
    )(proj_b, cos_t, sin_t, d_ob, o_b, lse)


MEM_SCALE = 128 ** -0.5
MEM_QB = 2048


def _mem_fwd(proj_a, kv, seq):
    bsz = proj_a.shape[0]

    def body(q_ref, k_ref, v_ref, o_ref):
        k, v = k_ref[...], v_ref[...]

        def qblock(i, carry):
            r0 = pl.multiple_of(i * MEM_QB, MEM_QB)
            s = lax.dot_general(q_ref[pl.ds(r0, MEM_QB), :], k, NT, preferred_element_type=F32) * MEM_SCALE
            p = jnp.exp(s - jnp.max(s, axis=1, keepdims=True))
            p = p / jnp.sum(p, axis=1, keepdims=True)
            o_ref[pl.ds(r0, MEM_QB), :] = jnp.dot(p.astype(BF16), v, preferred_element_type=F32).astype(BF16)
            return carry

        lax.fori_loop(0, seq // MEM_QB, qblock, 0)

    return pl.pallas_call(
        body, name="mem_attn_fwd", grid=(bsz, 4),
        out_shape=jax.ShapeDtypeStruct((bsz, seq, 512), BF16),
        in_specs=[pl.BlockSpec((None, seq, BLK), lambda b, h: (b, 0, 12 + h)),
                  pl.BlockSpec((None, MEM_LEN, BLK), lambda b, h: (b, 0, h)),
                  pl.BlockSpec((None, MEM_LEN, BLK), lambda b, h: (b, 0, 4 + h))],
        out_specs=pl.BlockSpec((None, seq, BLK), lambda b, h: (b, 0, h)),
        compiler_params=_params("parallel", "parallel"),
    )(proj_a, kv, kv)


def _mem_bwd(proj_a, kv, d_o, d_proj_a, seq):
    bsz = proj_a.shape[0]

    def body(q_ref, k_ref, v_ref, do_ref, _, dq_ref, dk_ref, dv_ref):
        k, v = k_ref[...], v_ref[...]

        def qblock(i, carry):
            dk, dv = carry
            r0 = pl.multiple_of(i * MEM_QB, MEM_QB)
            q, do = q_ref[pl.ds(r0, MEM_QB), :], do_ref[pl.ds(r0, MEM_QB), :]
            s = lax.dot_general(q, k, NT, preferred_element_type=F32) * MEM_SCALE
            p = jnp.exp(s - jnp.max(s, axis=1, keepdims=True))
            p = p / jnp.sum(p, axis=1, keepdims=True)
            dp = lax.dot_general(do, v, NT, preferred_element_type=F32)
            ds = ((p * (dp - jnp.sum(p * dp, axis=1, keepdims=True))) * MEM_SCALE).astype(BF16)
            dq_ref[pl.ds(r0, MEM_QB), :] = jnp.dot(ds, k, preferred_element_type=F32).astype(BF16)
            dk = dk + lax.dot_general(ds, q, TN, preferred_element_type=F32)
            dv = dv + lax.dot_general(p.astype(BF16), do, TN, preferred_element_type=F32)
            return dk, dv

        zero = jnp.zeros((MEM_LEN, BLK), F32)
        dk, dv = lax.fori_loop(0, seq // MEM_QB, qblock, (zero, zero))
        dk_ref[...] = dk.astype(BF16)
        dv_ref[...] = dv.astype(BF16)

    kv_spec = pl.BlockSpec((None, MEM_LEN, BLK), lambda b, h: (b, 0, h))
    return pl.pallas_call(
        body, name="mem_attn_bwd", grid=(bsz, 4),
        out_shape=(jax.ShapeDtypeStruct((bsz, seq, WA), BF16), jax.ShapeDtypeStruct((bsz, MEM_LEN, 512), BF16),
                   jax.ShapeDtypeStruct((bsz, MEM_LEN, 512), BF16)),
        in_specs=[pl.BlockSpec((None, seq, BLK), lambda b, h: (b, 0, 12 + h)), kv_spec,
                  pl.BlockSpec((None, MEM_LEN, BLK), lambda b, h: (b, 0, 4 + h)),
                  pl.BlockSpec((None, seq, BLK), lambda b, h: (b, 0, h)), ANY],
        out_specs=(pl.BlockSpec((None, seq, BLK), lambda b, h: (b, 0, 12 + h)), kv_spec, kv_spec),
        input_output_aliases={4: 0},
        compiler_params=_params("parallel", "parallel"),
    )(proj_a, kv, kv, d_o, d_proj_a)


def _mesh_pos():
    return lax.axis_index("x"), lax.axis_index("y"), lax.axis_index("c")


def _all_gather(shard, name):
    m_per, n = shard.shape

    def body(x_ref, out_ref, send_sems, recv_sems, local_sem):
        x, y, c = _mesh_pos()
        me, sibling = (x, y, c), (x, y, 1 - c)
        chips = [(1 - x, y), (x, 1 - y), (1 - x, 1 - y)]

        def rows(px, py, pc):
            return out_ref.at[pl.ds((4 * px + 2 * py + pc) * m_per, m_per), :]

        def copy(k, block, to, src=None):
            return pltpu.make_async_remote_copy(
                src_ref=rows(*block) if src is None else src, dst_ref=rows(*block),
                send_sem=send_sems.at[k], recv_sem=recv_sems.at[k], device_id=to, device_id_type=MESH)

        mine = pltpu.make_async_copy(x_ref, rows(*me), local_sem)
        mine.start()
        first = [copy(0, me, sibling, src=x_ref)]
        first += [copy(1 + j, me, (*chip, c), src=x_ref) for j, chip in enumerate(chips)]
        for cp in first:
            cp.start()
        passed = [copy(4 + j, (*chip, c), sibling) for j, chip in enumerate(chips)]
        for j, chip in enumerate(chips):
            copy(1 + j, (*chip, c), me).wait_recv()
            passed[j].start()
        copy(0, sibling, me).wait_recv()
        for j, chip in enumerate(chips):
            copy(4 + j, (*chip, 1 - c), me).wait_recv()
        for cp in first + passed:
            cp.wait_send()
        mine.wait()

    return pl.pallas_call(
        body, name=name, out_shape=jax.ShapeDtypeStruct((N_DEV * m_per, n), shard.dtype),
        in_specs=[ANY], out_specs=ANY,
        scratch_shapes=[pltpu.SemaphoreType.DMA((7,)), pltpu.SemaphoreType.DMA((7,)), pltpu.SemaphoreType.DMA(())],
    )(shard)


def _exchange_sems(n_arrays):
    return [pltpu.SemaphoreType.DMA((7 * n_arrays,)), pltpu.SemaphoreType.DMA((7 * n_arrays,)),
            pltpu.SemaphoreType.DMA((n_arrays,))]


def _exchange_out(srcs, gather):
    return [jax.ShapeDtypeStruct((N_DEV, *s.shape[-2:]), s.dtype) for s in srcs]


def _direct_exchange(src_refs, dst_refs, send_sems, recv_sems, local_sems, gather):
    x, y, c = _mesh_pos()
    me = 4 * x + 2 * y + c
    owns, sends, recvs = [], [], []
    for a, (src, dst) in enumerate(zip(src_refs, dst_refs)):
        owns.append(pltpu.make_async_copy(src if gather else src.at[me], dst.at[me], local_sems.at[a]))
        for j in range(1, N_DEV):
            px = 1 - x if j & 4 else x
            py = 1 - y if j & 2 else y
            pc = 1 - c if j & 1 else c
            peer = 4 * px + 2 * py + pc
            sems = dict(send_sem=send_sems.at[7 * a + j - 1], recv_sem=recv_sems.at[7 * a + j - 1],
                        device_id=(px, py, pc), device_id_type=MESH)
            sends.append(pltpu.make_async_remote_copy(
                src_ref=src if gather else src.at[peer], dst_ref=dst.at[me], **sems))
            recvs.append(pltpu.make_async_remote_copy(
                src_ref=src if gather else src.at[me], dst_ref=dst.at[peer], **sems))

    def start():
        for cp in owns + sends:
            cp.start()

    def wait():
        for cp in recvs:
            cp.wait_recv()
        for cp in sends:
            cp.wait_send()
        for cp in owns:
            cp.wait()

    return start, wait


def _riding_exchange(src_refs, dst_refs, sems, gather):
    start, wait = _direct_exchange(src_refs, dst_refs, *sems, gather)
    ids = [pl.program_id(a) for a in range(2)]
    last = [pl.num_programs(a) - 1 for a in range(2)]
    pl.when(jnp.logical_and(ids[0] == 0, ids[1] == 0))(start)
    return lambda: pl.when(jnp.logical_and(ids[0] == last[0], ids[1] == last[1]))(wait)


def _exchange(srcs, gather, name):
    n = len(srcs)

    def body(*refs):
        start, wait = _direct_exchange(refs[:n], refs[n:2 * n], *refs[2 * n:], gather=gather)
        start()
        wait()

    return pl.pallas_call(
        body, name=name, out_shape=_exchange_out(srcs, gather),
        in_specs=[ANY] * n, out_specs=[ANY] * n, scratch_shapes=_exchange_sems(n),
    )(*srcs)


def _adamw(w, g, m, v):
    m = ADAM_B1 * m + (1.0 - ADAM_B1) * g
    v = ADAM_B2 * v + (1.0 - ADAM_B2) * (g * g)
    m_hat = m / (1.0 - ADAM_B1 ** ADAM_STEP)
    v_hat = v / (1.0 - ADAM_B2 ** ADAM_STEP)
    return -ADAM_LR * (m_hat / (jnp.sqrt(v_hat) + ADAM_EPS) + ADAM_WD * w), m, v


def _reduce_adamw(recv, w, m, v, name):
    _, k, n = w.shape
    tr = max(t for t in range(16, 257, 16) if k % t == 0)

    def body(r_ref, w_ref, m_ref, v_ref, g_out, d_out, m_out, v_out):
        g = r_ref[0].astype(F32)
        for s in range(1, N_DEV):
            g = g + r_ref[s].astype(F32)
        g_out[...] = g
        d_out[...], m_out[...], v_out[...] = _adamw(w_ref[...], g, m_ref[...], v_ref[...])

    spec = pl.BlockSpec((None, tr, n), lambda i: (0, i, 0))
    return pl.pallas_call(
        body, name=name, grid=(k // tr,),
        out_shape=[jax.ShapeDtypeStruct((1, k, n), F32)] * 4,
        in_specs=[pl.BlockSpec((N_DEV, tr, n), lambda i: (0, i, 0)), spec, spec, spec],
        out_specs=[spec] * 4, compiler_params=_params("arbitrary"),
    )(recv, w, m, v)


def _small_adamw(gathered, w, m, v):
    def body(g_ref, w_ref, m_ref, v_ref, g_out, d_out, m_out, v_out, loss_out):
        tot = g_ref[0]
        for s in range(1, N_DEV):
            tot = tot + g_ref[s]
        g = tot[0:8]
        g_out[...] = g
        d_out[...], m_out[...], v_out[...] = _adamw(w_ref[...], g, m_ref[...], v_ref[...])
        loss_out[...] = jnp.broadcast_to((0.5 / D) * jnp.sum(tot[8:9], axis=1, keepdims=True), (8, BLK))

    out = [jax.ShapeDtypeStruct((8, D), F32)] * 4 + [jax.ShapeDtypeStruct((8, BLK), F32)]
    return pl.pallas_call(body, name="small_adamw", out_shape=out, compiler_params=_params())(gathered, w, m, v)


def _pick_chunks(w, chunks):
    return jnp.concatenate([w[:, BLK * c:BLK * (c + 1)] for c in chunks], axis=1)


def _whole_weight(gathered, i):
    _, k, n = gathered.shape
    if BY_ROWS[i]:
        return gathered.reshape(N_DEV * k, n)
    return gathered.transpose(1, 0, 2).reshape(k, N_DEV * n)


def _shard_parts(grad, i):
    if BY_ROWS[i]:
        return grad.reshape(N_DEV, grad.shape[0] // N_DEV, grad.shape[1])
    k, n8 = grad.shape
    return grad.reshape(k, N_DEV, n8 // N_DEV).transpose(1, 0, 2)


def kernel(x, mem, g_pre_mix, g_post_mix, g_pre_ffn, g_post_ffn, g_mem, w_in, w_mem_kv, w_br_sb, w_br_dil, w_br_mem, w_gate, b_gate, w_o, w_ffn_in, w_ffn_out, loss_target, m_g_pre_mix, m_g_post_mix, m_g_pre_ffn, m_g_post_ffn, m_g_mem, m_w_in, m_w_mem_kv, m_w_br_sb, m_w_br_dil, m_w_br_mem, m_w_gate, m_b_gate, m_w_o, m_w_ffn_in, m_w_ffn_out, v_g_pre_mix, v_g_post_mix, v_g_pre_ffn, v_g_post_ffn, v_g_mem, v_w_in, v_w_mem_kv, v_w_br_sb, v_w_br_dil, v_w_br_mem, v_w_gate, v_b_gate, v_w_o, v_w_ffn_in, v_w_ffn_out):
    bsz, seq, _ = x.shape
    tokens = bsz * seq
    xf, tgt, memf = x.reshape(tokens, D), loss_target.reshape(tokens, D), mem.reshape(bsz * MEM_LEN, D)
    big_w = [w_in, w_mem_kv, w_br_sb, w_br_dil, w_br_mem, w_gate, w_o, w_ffn_in, w_ffn_out]
    big_m = [m_w_in, m_w_mem_kv, m_w_br_sb, m_w_br_dil, m_w_br_mem, m_w_gate, m_w_o, m_w_ffn_in, m_w_ffn_out]
    big_v = [v_w_in, v_w_mem_kv, v_w_br_sb, v_w_br_dil, v_w_br_mem, v_w_gate, v_w_o, v_w_ffn_in, v_w_ffn_out]

    shards = [w[0].astype(BF16) for w in big_w]
    k_in, n_in = shards[0].shape
    fw_in = _whole_weight(_all_gather(shards[0], "weight_all_gather").reshape(N_DEV, k_in, n_in), 0)
    w_a, w_b = _pick_chunks(fw_in, CHUNKS_A), _pick_chunks(fw_in, CHUNKS_B)

    h = _norm_fwd(xf, g_pre_mix, "pre_mix_norm")
    proj_a = _matmul(h, w_a, "nn", BF16, "proj_a").reshape(bsz, seq, WA)
    proj_b = _matmul(h, w_b, "nn", BF16, "proj_b").reshape(bsz, seq, WB)
    o_a, o_a32, *behind = _sb_fwd(proj_a, seq, [shards[i] for i in GATHER_BEHIND])
    fw_mem_kv, fw_br_sb, fw_br_dil, fw_br_mem, fw_gate, fw_o, fw_ffn_in, fw_ffn_out = (
        _whole_weight(g, i) for g, i in zip(behind, GATHER_BEHIND))
    gpre = _matmul(h, fw_gate, "nn", BF16, "gate_proj")
    cos_t, sin_t = _rope_tables(seq)
    o_b, lse_b = _dil_fwd(proj_b, cos_t, sin_t, seq)
    mn = _norm_fwd(memf, g_mem, "mem_norm")
    kv = _matmul(mn, fw_mem_kv, "nn", BF16, "mem_kv_proj").reshape(bsz, MEM_LEN, D)
    o_c = _mem_fwd(proj_a, kv, seq)
    o_a2, o_b2, o_c2 = o_a.reshape(tokens, 512), o_b.reshape(tokens, 256), o_c.reshape(tokens, 512)
    ys = [_matmul(o_a2, fw_br_sb, "nn", BF16, "branch_sb"), _matmul(o_b2, fw_br_dil, "nn", BF16, "branch_dil"),
          _matmul(o_c2, fw_br_mem, "nn", BF16, "branch_mem")]
    merged, mix, x1, h2 = _merge_out_proj_norm(gpre, ys, b_gate, fw_o, xf, g_post_mix, g_pre_ffn)
    gu_a, gu_b, f = _ffn_in_swiglu(h2, fw_ffn_in)
    dy, dfo, dg_post_ffn, loss_lanes = _ffn_out_loss(f, fw_ffn_out, x1, tgt, g_post_ffn)

    gw_ffn_out = _matmul(f, dfo, "tn", BF16, "gw_ffn_out")
    dgu = _d_ffn_swiglu_bwd(dfo, fw_ffn_out, gu_a, gu_b)
    gw_ffn_in = _matmul(h2, dgu, "tn", BF16, "gw_ffn_in")
    dx1, dmix, dg_pre_ffn, dg_post_mix = _d_h2_norm_bwd(dgu, fw_ffn_in, x1, dy, mix, g_pre_ffn, g_post_mix)
    gw_o = _matmul(merged, dmix, "tn", BF16, "gw_o")
    dya, dyb, dyc, dgpre, db_gate = _d_merged_gate_bwd(dmix, fw_o, gpre, ys, b_gate)
    d_oa = _matmul(dya, fw_br_sb, "nt", BF16, "d_o_sb").reshape(bsz, seq, 512)
    d_ob = _matmul(dyb, fw_br_dil, "nt", BF16, "d_o_dil").reshape(bsz, seq, 256)
    d_oc = _matmul(dyc, fw_br_mem, "nt", BF16, "d_o_mem").reshape(bsz, seq, 512)
    gw_br_sb = _matmul(o_a2, dya, "tn", BF16, "gw_br_sb")
    gw_br_dil = _matmul(o_b2, dyb, "tn", BF16, "gw_br_dil")
    gw_br_mem = _matmul(o_c2, dyc, "tn", BF16, "gw_br_mem")
    gw_gate = _matmul(h, dgpre, "tn", BF16, "gw_gate")
    grads = {2: gw_br_sb, 3: gw_br_dil, 4: gw_br_mem, 5: gw_gate, 6: gw_o, 7: gw_ffn_in, 8: gw_ffn_out}
    d_proj_a, *recv_behind = _sb_bwd(proj_a, d_oa, o_a32, seq, [_shard_parts(grads[i], i) for i in REDUCE_BEHIND])
    d_proj_a, dk_m, dv_m = _mem_bwd(proj_a, kv, d_oc, d_proj_a, seq)
    d_proj_b = _dil_bwd(proj_b, cos_t, sin_t, d_ob, o_b, lse_b, seq).reshape(tokens, WB)
    d_proj_a = d_proj_a.reshape(tokens, WA)
    gw_a = _matmul(h, d_proj_a, "tn", BF16, "gw_in_a")
    gw_b = _matmul(h, d_proj_b, "tn", BF16, "gw_in_b")
    dkv = jnp.concatenate([dk_m, dv_m], axis=-1).reshape(bsz * MEM_LEN, D)
    gw_mem_kv = _matmul(mn, dkv, "tn", BF16, "gw_mem_kv")
    dmn = _matmul(dkv, fw_mem_kv, "nt", F32, "d_mem_norm")
    dg_mem = _gain_grad(dmn, memf)
    gw_ab = jnp.concatenate([gw_a, gw_b], axis=1)
    where = {c: i for i, c in enumerate(CHUNKS_A + CHUNKS_B)}
    grads = {0: _pick_chunks(gw_ab, [where[c] for c in range(34)]), 1: gw_mem_kv}
    dx, dg_pre_mix, *recv_last = _d_h_norm_bwd(
        [(dgpre, fw_gate), (d_proj_a, w_a), (d_proj_b, w_b)], xf, dx1, g_pre_mix,
        [_shard_parts(grads[i], i) for i in REDUCE_LAST])

    received = dict(zip(REDUCE_BEHIND + REDUCE_LAST, [*recv_behind, *recv_last]))
    adam = [_reduce_adamw(received[i], big_w[i], big_m[i], big_v[i], "reduce_adamw_" + BIG_NAMES[i])
            for i in range(len(big_w))]
    big = [[a[k] for a in adam] for k in range(4)]

    small = jnp.concatenate([dg_pre_mix, dg_post_mix, dg_pre_ffn, dg_post_ffn, dg_mem, db_gate.reshape(3, D),
                             loss_lanes, jnp.zeros((7, D), F32)], axis=0)
    small_all, = _exchange([small], True, "small_all_gather")

    def small_pack(gs, b):
        return jnp.concatenate([*gs, b.reshape(3, D)], axis=0)

    sm = _small_adamw(
        small_all, small_pack([g_pre_mix, g_post_mix, g_pre_ffn, g_post_ffn, g_mem], b_gate),
        small_pack([m_g_pre_mix, m_g_post_mix, m_g_pre_ffn, m_g_post_ffn, m_g_mem], m_b_gate),
        small_pack([v_g_pre_mix, v_g_post_mix, v_g_pre_ffn, v_g_post_ffn, v_g_mem], v_b_gate))
    loss = sm[4][0, 0]

    def leaves(k):
        t, bw = sm[k], big[k]
        return [t[0:1], t[1:2], t[2:3], t[3:4], t[4:5], *bw[0:6], t[5:8].reshape(1, 3 * D), *bw[6:9]]

    return (loss, dx.reshape(bsz, seq, D), *leaves(0), *leaves(1), *leaves(2), *leaves(3))
```

```python
import functools

import jax
import jax.numpy as jnp
from jax import lax
from jax.experimental import pallas as pl
from jax.experimental.pallas import tpu as pltpu

F32 = jnp.float32
BF16 = jnp.bfloat16
D = 1024
BLK = 128
MEM_LEN = 256
D_FF = 2816
NORM_EPS = 1e-6
NEG_INF = -1e30
ROPE_THETA = 10000.0
ADAM_LR, ADAM_B1, ADAM_B2, ADAM_EPS, ADAM_WD, ADAM_STEP = 0.001, 0.9, 0.999, 1e-08, 0.01, 10
N_DEV = 8
VMEM_LIMIT_BYTES = 56 * 1024 * 1024
MESH = pl.DeviceIdType.MESH
ANY = pl.BlockSpec(memory_space=pl.ANY)

NT = (((1,), (1,)), ((), ()))
TN = (((0,), (0,)), ((), ()))
NN = (((1,), (0,)), ((), ()))
_DIMS = {"nn": NN, "nt": NT, "tn": TN}

BIG_NAMES = ("w_in", "w_mem_kv", "w_br_sb", "w_br_dil", "w_br_mem", "w_gate", "w_o", "w_ffn_in", "w_ffn_out")
BY_ROWS = (False, True, False, False, False, False, True, False, True)
GATHER_FIRST = (0,)
GATHER_BEHIND = (1, 2, 3, 4, 5, 6, 7, 8)
REDUCE_BEHIND = (2, 3, 4, 5, 6, 7, 8)
REDUCE_LAST = (0, 1)

CHUNKS_A = tuple(c for hp in range(4) for c in (hp, 4 + hp, 8 + hp)) + (30, 31, 32, 33)
CHUNKS_B = tuple(c for hp in range(2) for g in range(3) for c in (12 + 6 * g + hp, 14 + 6 * g + hp, 16 + 6 * g + hp))
WA, WB = 128 * len(CHUNKS_A), 128 * len(CHUNKS_B)
DIL_GROUPS = (1, 4, 16)


def _params(*sem):
    return pltpu.CompilerParams(dimension_semantics=sem or None, vmem_limit_bytes=VMEM_LIMIT_BYTES)


def _tile(n, cap):
    if n <= 128:
        return n
    assert n % 128 == 0, n
    best = 128
    for t in range(128, min(n, cap) + 1, 128):
        if n % t == 0:
            best = t
    return best


def _k_steps(k, nk, step):
    if nk == 1:
        step(True, True)
        return
    pl.when(k == 0)(functools.partial(step, True, False))
    if nk > 2:
        pl.when(jnp.logical_and(k > 0, k < nk - 1))(functools.partial(step, False, False))
    pl.when(k == nk - 1)(functools.partial(step, False, True))


def _matmul(a, b, mode, out_dtype, name, tm_cap=1536, tn_cap=1536, tk_cap=1536):
    if mode == "tn":
        (K, M), N = a.shape, b.shape[1]
    elif mode == "nt":
        (M, K), N = a.shape, b.shape[0]
    else:
        (M, K), N = a.shape, b.shape[1]
    if mode == "tn":
        tk_cap = 2 * tk_cap
    tm, tn, tk = _tile(M, tm_cap), _tile(N, tn_cap), _tile(K, tk_cap)
    nm, nn, nk = M // tm, N // tn, K // tk
    dims = _DIMS[mode]

    def body(a_ref, b_ref, o_ref, *acc):
        def step(first, last):
            d = lax.dot_general(a_ref[...], b_ref[...], dims, preferred_element_type=F32)
            if not first:
                d = d + acc[0][...]
            if last:
                o_ref[...] = d.astype(o_ref.dtype)
            else:
                acc[0][...] = d

        _k_steps(pl.program_id(2), nk, step)

    n_outer = nk == 1 and (a.size * nn + b.size) < (a.size + b.size * nm)
    if n_outer:
        grid, ij = (nn, nm, nk), (lambda g0, g1: (g1, g0))
    else:
        grid, ij = (nm, nn, nk), (lambda g0, g1: (g0, g1))
    if mode == "tn":
        a_spec = pl.BlockSpec((tk, tm), lambda g0, g1, k: (k, ij(g0, g1)[0]))
    else:
        a_spec = pl.BlockSpec((tm, tk), lambda g0, g1, k: (ij(g0, g1)[0], k))
    if mode == "nt":
        b_spec = pl.BlockSpec((tn, tk), lambda g0, g1, k: (ij(g0, g1)[1], k))
    else:
        b_spec = pl.BlockSpec((tk, tn), lambda g0, g1, k: (k, ij(g0, g1)[1]))
    return pl.pallas_call(
        body, name=name, grid=grid,
        out_shape=jax.ShapeDtypeStruct((M, N), out_dtype),
        in_specs=[a_spec, b_spec],
        out_specs=pl.BlockSpec((tm, tn), lambda g0, g1, k: ij(g0, g1)),
        scratch_shapes=[pltpu.VMEM((tm, tn), F32)] if nk > 1 else [],
        compiler_params=_params("parallel", "parallel", "arbitrary"),
    )(a, b)


def _rowwise(body, name, rows, tr, row_ins, vec_ins, row_outs, acc_outs=()):
    tr = min(tr, rows)
    assert rows % tr == 0
    in_specs, args = [], []
    for r in row_ins:
        arr, w, cb = r if isinstance(r, tuple) else (r, r.shape[1], 0)
        in_specs.append(pl.BlockSpec((tr, w), functools.partial(lambda i, cb: (i, cb), cb=cb)))
        args.append(arr)
    for v in vec_ins:
        in_specs.append(pl.BlockSpec(v.shape, lambda i: (0, 0)))
        args.append(v)
    out_shape = [jax.ShapeDtypeStruct((rows, w), dt) for w, dt in row_outs]
    out_shape += [jax.ShapeDtypeStruct((1, w), F32) for w in acc_outs]
    out_specs = [pl.BlockSpec((tr, w), lambda i: (i, 0)) for w, _ in row_outs]
    out_specs += [pl.BlockSpec((1, w), lambda i: (0, 0)) for w in acc_outs]
    n_acc = len(acc_outs)

    def wrapped(*refs):
        if n_acc:
            @pl.when(pl.program_id(0) == 0)
            def _():
                for r in refs[len(refs) - n_acc:]:
                    r[...] = jnp.zeros_like(r)
        body(*refs)

    return pl.pallas_call(
        wrapped, name=name, grid=(rows // tr,), out_shape=out_shape, in_specs=in_specs, out_specs=out_specs,
        compiler_params=_params("arbitrary"),
    )(*args)


def _rstd(x):
    return lax.rsqrt(jnp.mean(x * x, axis=-1, keepdims=True) + NORM_EPS)


def _norm_bwd(u, n, r):
    return r * (u - n * jnp.mean(u * n, axis=-1, keepdims=True))


def _colsum(v):
    return jnp.sum(v, axis=0, keepdims=True)


def _norm_fwd(x, g, name):
    def body(x_ref, g_ref, h_ref):
        xv = x_ref[...]
        h_ref[...] = ((xv * _rstd(xv)) * g_ref[...]).astype(BF16)

    return _rowwise(body, name, x.shape[0], 512, [x], [g], [(D, BF16)])[0]


def _matmul_rows(pairs, mode, name, epilogue, row_ins=(), vec_ins=(), row_outs=(), acc_outs=(), ride=None,
                 tm=512, tk_cap=1536, epi_rows=None):
    M = pairs[0][0].shape[0]
    N = pairs[0][1].shape[1] if mode == "nn" else pairs[0][1].shape[0]
    tm = min(tm, M)
    tks = [_tile(a.shape[1], tk_cap) for a, _ in pairs]
    nks = [a.shape[1] // tk for (a, _), tk in zip(pairs, tks)]
    offs = [sum(nks[:p]) for p in range(len(pairs))]
    nm, nk = M // tm, sum(nks)
    dims = _DIMS[mode]
    n_ab, n_extra, n_out = 2 * len(pairs), len(row_ins) + len(vec_ins), len(row_outs) + len(acc_outs)
    n_ride = 0 if ride is None else len(ride)

    def body(*refs):
        ab, extra, rest = refs[:n_ab], refs[n_ab:n_ab + n_extra], refs[n_ab + n_extra:]
        ride_refs, outs, rest = rest[:n_ride], rest[n_ride:n_ride + n_out], rest[n_ride + n_out:]
        received_refs, rest = rest[:n_ride], rest[n_ride:]
        if n_ride:
            finish_ride = _riding_exchange(ride_refs, received_refs, rest[len(rest) - 3:], gather=False)
        i, k = pl.program_id(0), pl.program_id(1)
        if acc_outs:
            @pl.when(jnp.logical_and(i == 0, k == 0))
            def _():
                for r in outs[len(row_outs):]:
                    r[...] = jnp.zeros_like(r)

        def step(p, first, last):
            d = lax.dot_general(ab[2 * p][...], ab[2 * p + 1][...], dims, preferred_element_type=F32)
            if not first:
                d = d + rest[0][...]
            if not last:
                rest[0][...] = d
            elif epi_rows is None:
                epilogue(d, *extra, *outs)
            else:
                rest[0][...] = d
                for c in range(tm // epi_rows):
                    rows = pl.ds(c * epi_rows, epi_rows)
                    sliced = [r.at[rows] for r in extra[:len(row_ins)]] + list(extra[len(row_ins):])
                    sliced += [r.at[rows] for r in outs[:len(row_outs)]] + list(outs[len(row_outs):])
                    epilogue(rest[0][rows, :], *sliced)

        last_p = len(pairs) - 1
        if nk == 1:
            step(0, True, True)
        else:
            pl.when(k == 0)(functools.partial(step, 0, True, False))
            for p in range(len(pairs)):
                lo, hi = max(offs[p], 1), min(offs[p] + nks[p], nk - 1)
                if hi > lo:
                    pl.when(jnp.logical_and(k >= lo, k < hi))(functools.partial(step, p, False, False))
            pl.when(k == nk - 1)(functools.partial(step, last_p, False, True))
        if n_ride:
            finish_ride()

    in_specs, args = [], []
    for p, ((a, b), tk) in enumerate(zip(pairs, tks)):
        step = functools.partial(lambda k, p: jnp.clip(k - offs[p], 0, nks[p] - 1), p=p)
        in_specs.append(pl.BlockSpec((tm, tk), functools.partial(lambda i, k, step: (i, step(k)), step=step)))
        if mode == "nn":
            in_specs.append(pl.BlockSpec((tk, N), functools.partial(lambda i, k, step: (step(k), 0), step=step)))
        else:
            in_specs.append(pl.BlockSpec((N, tk), functools.partial(lambda i, k, step: (0, step(k)), step=step)))
        args += [a, b]
    in_specs += [pl.BlockSpec((tm, r.shape[1]), lambda i, k: (i, 0)) for r in row_ins]
    in_specs += [pl.BlockSpec(v.shape, lambda i, k: (0, 0)) for v in vec_ins]
    in_specs += [ANY] * n_ride
    out_shape = [jax.ShapeDtypeStruct((M, w), dt) for w, dt in row_outs]
    out_shape += [jax.ShapeDtypeStruct((1, w), F32) for w in acc_outs]
    out_specs = [pl.BlockSpec((tm, w), lambda i, k: (i, 0)) for w, _ in row_outs]
    out_specs += [pl.BlockSpec((1, w), lambda i, k: (0, 0)) for w in acc_outs]
    scratch = [pltpu.VMEM((tm, N), F32)] if nk > 1 else []
    if n_ride:
        out_shape += _exchange_out(ride, False)
        out_specs += [ANY] * n_ride
        scratch += _exchange_sems(n_ride)
    return pl.pallas_call(
        body, name=name, grid=(nm, nk), out_shape=out_shape, in_specs=in_specs, out_specs=out_specs,
        scratch_shapes=scratch, compiler_params=_params("arbitrary", "arbitrary"),
    )(*args, *row_ins, *vec_ins, *(ride or []))


def _merge_out_proj_norm(gpre, ys, b_gate, w_o, x, g_post, g_pre):
    tokens = x.shape[0]
    tm = min(512, tokens)

    def body(gp_ref, ya_ref, yb_ref, yc_ref, b_ref, w_ref, x_ref, g2_ref, g3_ref, m_ref, mix_ref, x1_ref, h2_ref):
        acc = None
        for k, y_ref in enumerate((ya_ref, yb_ref, yc_ref)):
            cols = slice(k * D, (k + 1) * D)
            gate = jax.nn.sigmoid(gp_ref[:, cols].astype(F32) + b_ref[:, cols])
            term = gate * y_ref[...].astype(F32)
            acc = term if acc is None else acc + term
        merged = acc.astype(BF16)
        m_ref[...] = merged
        mv = jnp.dot(merged, w_ref[...], preferred_element_type=F32)
        mix_ref[...] = mv
        x1 = x_ref[...] + (mv * _rstd(mv)) * g2_ref[...]
        x1_ref[...] = x1
        h2_ref[...] = ((x1 * _rstd(x1)) * g3_ref[...]).astype(BF16)

    def rows(w):
        return pl.BlockSpec((tm, w), lambda i: (i, 0))

    def whole(a):
        return pl.BlockSpec(a.shape, lambda i: (0, 0))

    return pl.pallas_call(
        body, name="merge_out_proj_norm", grid=(tokens // tm,),
        out_shape=[jax.ShapeDtypeStruct((tokens, D), dt) for dt in (BF16, F32, F32, BF16)],
        in_specs=[rows(3 * D), rows(D), rows(D), rows(D), whole(b_gate), whole(w_o), rows(D), whole(g_post),
                  whole(g_pre)],
        out_specs=[rows(D)] * 4, compiler_params=_params("parallel"),
    )(gpre, *ys, b_gate, w_o, x, g_post, g_pre)


def _ffn_in_swiglu(h2, w_ffn_in):
    tokens = h2.shape[0]
    tm, tn = min(512, tokens), _tile(D_FF, 1536)
    nj = D_FF // tn

    def body(h_ref, wa_ref, wb_ref, a_ref, b_ref, f_ref):
        hv = h_ref[...]
        a = jnp.dot(hv, wa_ref[...], preferred_element_type=F32)
        b = jnp.dot(hv, wb_ref[...], preferred_element_type=F32)
        a_ref[...] = a.astype(BF16)
        b_ref[...] = b.astype(BF16)
        f_ref[...] = (a * jax.nn.sigmoid(a) * b).astype(BF16)

    out = jax.ShapeDtypeStruct((tokens, D_FF), BF16)
    o_spec = pl.BlockSpec((tm, tn), lambda j, i: (i, j))
    return pl.pallas_call(
        body, name="ffn_in_swiglu", grid=(nj, tokens // tm), out_shape=(out, out, out),
        in_specs=[pl.BlockSpec((tm, D), lambda j, i: (i, 0)), pl.BlockSpec((D, tn), lambda j, i: (0, j)),
                  pl.BlockSpec((D, tn), lambda j, i: (0, j + nj))],
        out_specs=(o_spec, o_spec, o_spec), compiler_params=_params("parallel", "parallel"),
    )(h2, w_ffn_in, w_ffn_in)


def _ffn_out_loss(f, w_ffn_out, x1, tgt, g_post):
    def epilogue(fo_v, x1_ref, t_ref, g_ref, dy_ref, dfo_ref, dg_ref, loss_ref):
        r = _rstd(fo_v)
        n = fo_v * r
        err = (x1_ref[...] + n * g_ref[...]) - t_ref[...]
        loss_ref[...] += _colsum(err * err)
        dy = err * (1.0 / D)
        dy_ref[...] = dy
        dg_ref[...] += _colsum(dy * n)
        dfo_ref[...] = _norm_bwd(dy * g_ref[...], n, r).astype(BF16)

    return _matmul_rows([(f, w_ffn_out)], "nn", "ffn_out_loss", epilogue, [x1, tgt], [g_post],
                        [(D, F32), (D, BF16)], (D, D), tk_cap=D_FF)


def _d_ffn_swiglu_bwd(dfo, w_ffn_out, gu_a, gu_b):
    def epilogue(d, a_ref, b_ref, dgu_ref):
        a = a_ref[...].astype(F32)
        b = b_ref[...].astype(F32)
        s = jax.nn.sigmoid(a)
        dgu_ref[:, :D_FF] = (d * b * (s * (1.0 + a * (1.0 - s)))).astype(BF16)
        dgu_ref[:, D_FF:] = (d * (a * s)).astype(BF16)

    return _matmul_rows([(dfo, w_ffn_out)], "nt", "d_ffn_swiglu_bwd", epilogue, [gu_a, gu_b], [],
                        [(2 * D_FF, BF16)], tm=256)[0]


def _d_h2_norm_bwd(dgu, w_ffn_in, x1, dy, mix, g_pre, g_post):
    def epilogue(dh, x1_ref, dy_ref, mix_ref, g3_ref, g2_ref, dx1_ref, dmix_ref, dg3_ref, dg2_ref):
        x1v = x1_ref[...]
        r3 = _rstd(x1v)
        n3 = x1v * r3
        dg3_ref[...] += _colsum(dh * n3)
        dx1 = dy_ref[...] + _norm_bwd(dh * g3_ref[...], n3, r3)
        dx1_ref[...] = dx1
        mv = mix_ref[...]
        r2 = _rstd(mv)
        n2 = mv * r2
        dg2_ref[...] += _colsum(dx1 * n2)
        dmix_ref[...] = _norm_bwd(dx1 * g2_ref[...], n2, r2).astype(BF16)

    return _matmul_rows([(dgu, w_ffn_in)], "nt", "d_h2_norm_bwd", epilogue, [x1, dy, mix], [g_pre, g_post],
                        [(D, F32), (D, BF16)], (D, D), tk_cap=D_FF)


def _d_merged_gate_bwd(dmix, w_o, gpre, ys, b_gate):
    def epilogue(dm, gp_ref, ya_ref, yb_ref, yc_ref, b_ref, dya_ref, dyb_ref, dyc_ref, dgp_ref, db_ref):
        for k, (y_ref, dy_ref) in enumerate(((ya_ref, dya_ref), (yb_ref, dyb_ref), (yc_ref, dyc_ref))):
            cols = slice(k * D, (k + 1) * D)
            gate = jax.nn.sigmoid(gp_ref[:, cols].astype(F32) + b_ref[:, cols])
            dy_ref[...] = (dm * gate).astype(BF16)
            dgp = (dm * y_ref[...].astype(F32)) * (gate * (1.0 - gate))
            dgp_ref[:, cols] = dgp.astype(BF16)
            db_ref[:, cols] += _colsum(dgp)

    return _matmul_rows([(dmix, w_o)], "nt", "d_merged_gate_bwd", epilogue, [gpre, *ys], [b_gate],
                        [(D, BF16), (D, BF16), (D, BF16), (3 * D, BF16)], (3 * D,))


def _d_h_norm_bwd(pairs, x, dx1, g_pre, ride):
    def epilogue(dh, x_ref, dx1_ref, g_ref, dx_ref, dg_ref):
        xv = x_ref[...]
        r = _rstd(xv)
        n = xv * r
        dg_ref[...] += _colsum(dh * n)
        dx_ref[...] = dx1_ref[...] + _norm_bwd(dh * g_ref[...], n, r)

    return _matmul_rows(pairs, "nt", "d_h_norm_bwd", epilogue, [x, dx1], [g_pre], [(D, F32)], (D,), ride=ride,
                        tm=1024, tk_cap=512, epi_rows=256)


def _gain_grad(dmn, mem):
    def body(d_ref, m_ref, dg_ref):
        mv = m_ref[...]
        dg_ref[...] += _colsum(d_ref[...] * (mv * _rstd(mv)))

    return _rowwise(body, "mem_gain_grad", mem.shape[0], 256, [dmn, mem], [], [], (D,))[0]


def _head_rowsum(v, head0):
    return (jnp.sum(jnp.where(head0, v, 0.0), axis=1, keepdims=True),
            jnp.sum(jnp.where(head0, 0.0, v), axis=1, keepdims=True))


KT = 256
SB_SCALE = 0.125


def _make_suffix(inclusive):
    row, col = lax.broadcasted_iota(jnp.int32, (KT, KT), 0), lax.broadcasted_iota(jnp.int32, (KT, KT), 1)
    tri = (row >= col if inclusive else row > col).astype(BF16)
    tri2 = jnp.concatenate([tri, tri], axis=0)

    def suffix(x):
        hi = x.astype(BF16)
        lo = (x - hi.astype(F32)).astype(BF16)
        return jnp.dot(jnp.concatenate([hi, lo], axis=1), tri2, preferred_element_type=F32)

    return suffix


def _sb_scores(qh, k, mask, suffix_incl, later):
    z = lax.dot_general(qh, k, NT, preferred_element_type=F32)
    zc = jnp.minimum(z, 60.0)
    sp = jnp.log(1.0 + jnp.exp(zc))
    if mask is not None:
        sp = jnp.where(mask, sp, 0.0)
    a = jnp.exp((zc - suffix_incl(sp)) - later)
    if mask is not None:
        a = jnp.where(mask, a, 0.0)
    return zc, sp, a


QB = KT


def _sb_tiles(i, tile, init):
    st = lax.cond(i > 0, lambda s: tile(i - 1, tile(i, s, True), False), lambda s: tile(i, s, True), init)
    rest = jnp.maximum(i - 1, 0)
    st = lax.fori_loop(0, lax.shift_right_logical(rest, 1),
                       lambda t, s: tile(rest - 2 - 2 * t, tile(rest - 1 - 2 * t, s, False), False), st)
    return lax.cond((rest & 1) == 1, lambda s: tile(0, s, False), lambda s: s, st)


def _sb_consts():
    head0 = lax.broadcasted_iota(jnp.int32, (QB, BLK), 1) < 64
    row = lax.broadcasted_iota(jnp.int32, (2 * QB, KT), 0) & (QB - 1)
    return head0, row > lax.broadcasted_iota(jnp.int32, (2 * QB, KT), 1)


def _stack_heads(v, head0):
    zero = jnp.zeros_like(v)
    return jnp.concatenate([jnp.where(head0, v, zero), jnp.where(head0, zero, v)], axis=0)


def _unstack_heads(v, head0):
    n = v.shape[0] // 2
    return jnp.where(head0, v[:n], v[n:])


def _sb_fwd(proj_a, seq, ride):
    bsz = proj_a.shape[0]
    n_ride = len(ride)

    def body(x_ref, *rest):
        ride_refs, (o_ref, o32_ref), rest = rest[:n_ride], rest[n_ride:n_ride + 2], rest[n_ride + 2:]
        gathered_refs, acc_ref, sems = rest[:n_ride], rest[n_ride], rest[n_ride + 1:]
        finish_ride = _riding_exchange(ride_refs, gathered_refs, sems, gather=True)
        head0, diag_mask = _sb_consts()
        suffix_incl = _make_suffix(True)

        def qblock(i, carry):
            r0 = pl.multiple_of(i * QB, QB)
            qs = _stack_heads(x_ref[pl.ds(r0, QB), 0:128] * jnp.asarray(SB_SCALE, BF16), head0)

            def tile(jt, later, masked):
                c0 = pl.multiple_of(jt * KT, KT)
                k = x_ref[pl.ds(c0, KT), 128:256]
                v = x_ref[pl.ds(c0, KT), 256:384]
                _, sp, a = _sb_scores(qs, k, diag_mask if masked else None, suffix_incl, later)
                pv = jnp.dot(a.astype(BF16), v, preferred_element_type=F32)
                if masked:
                    acc_ref[...] = pv
                else:
                    acc_ref[...] += pv
                return later + jnp.sum(sp, axis=1, keepdims=True)

            _sb_tiles(i, tile, jnp.zeros((2 * QB, 1), F32))
            o = _unstack_heads(acc_ref[...], head0)
            o32_ref[pl.ds(r0, QB), :] = o
            o_ref[pl.ds(r0, QB), :] = o.astype(BF16)
            return carry

        lax.fori_loop(0, seq // QB, qblock, 0)
        finish_ride()

    out_spec = pl.BlockSpec((None, seq, BLK), lambda b, hp: (b, 0, hp))
    return pl.pallas_call(
        body, name="sb_attn_fwd", grid=(bsz, 4),
        out_shape=[jax.ShapeDtypeStruct((bsz, seq, 512), BF16), jax.ShapeDtypeStruct((bsz, seq, 512), F32),
                   *_exchange_out(ride, True)],
        in_specs=[pl.BlockSpec((None, seq, 384), lambda b, hp: (b, 0, hp))] + [ANY] * n_ride,
        out_specs=[out_spec, out_spec] + [ANY] * n_ride,
        scratch_shapes=[pltpu.VMEM((2 * QB, BLK), F32), *_exchange_sems(n_ride)],
        compiler_params=_params("arbitrary", "arbitrary"),
    )(proj_a, *ride)


def _sb_bwd(proj_a, d_o, o_a, seq, ride):
    bsz = proj_a.shape[0]
    n_ride = len(ride)

    def body(x_ref, do_ref, o_ref, *rest):
        ride_refs, d_ref, rest = rest[:n_ride], rest[n_ride], rest[n_ride + 1:]
        received_refs, (dq_acc, dk_acc, dv_acc), sems = rest[:n_ride], rest[n_ride:n_ride + 3], rest[n_ride + 3:]
        finish_ride = _riding_exchange(ride_refs, received_refs, sems, gather=False)
        head0, diag_mask = _sb_consts()
        suffix_incl, suffix_excl = _make_suffix(True), _make_suffix(False)
        dk_acc[...] = jnp.zeros_like(dk_acc)
        dv_acc[...] = jnp.zeros_like(dv_acc)

        def qblock(i, carry):
            r0 = pl.multiple_of(i * QB, QB)
            qs = _stack_heads(x_ref[pl.ds(r0, QB), 0:128] * jnp.asarray(SB_SCALE, BF16), head0)
            do = do_ref[pl.ds(r0, QB), :]
            dos = _stack_heads(do, head0)
            dsum = jnp.concatenate(_head_rowsum(do.astype(F32) * o_ref[pl.ds(r0, QB), :], head0), axis=0)

            def tile(jt, st, masked):
                later, rest_g = st
                c0 = pl.multiple_of(jt * KT, KT)
                k = x_ref[pl.ds(c0, KT), 128:256]
                v = x_ref[pl.ds(c0, KT), 256:384]
                zc, sp, a = _sb_scores(qs, k, diag_mask if masked else None, suffix_incl, later)
                a16 = a.astype(BF16)
                g = a16.astype(F32) * lax.dot_general(dos, v, NT, preferred_element_type=F32)
                dz = g - jnp.exp(zc - sp) * (rest_g - suffix_excl(g))
                if masked:
                    dz = jnp.where(diag_mask, dz, 0.0)
                dz = dz.astype(BF16)
                dq = jnp.dot(dz, k, preferred_element_type=F32)
                if masked:
                    dq_acc[...] = dq
                else:
                    dq_acc[...] += dq
                dk_acc[pl.ds(c0, KT), :] += lax.dot_general(dz, qs, TN, preferred_element_type=F32)
                dv_acc[pl.ds(c0, KT), :] += lax.dot_general(a16, dos, TN, preferred_element_type=F32)
                return later + jnp.sum(sp, axis=1, keepdims=True), rest_g - jnp.sum(g, axis=1, keepdims=True)

            _sb_tiles(i, tile, (jnp.zeros((2 * QB, 1), F32), dsum))
            d_ref[pl.ds(r0, QB), 0:128] = (_unstack_heads(dq_acc[...], head0) * SB_SCALE).astype(BF16)
            return carry

        lax.fori_loop(0, seq // QB, qblock, 0)
        d_ref[:, 128:256] = dk_acc[...].astype(BF16)
        d_ref[:, 256:384] = dv_acc[...].astype(BF16)
        finish_ride()

    return pl.pallas_call(
        body, name="sb_attn_bwd", grid=(bsz, 4),
        out_shape=[jax.ShapeDtypeStruct((bsz, seq, WA), BF16), *_exchange_out(ride, False)],
        in_specs=[pl.BlockSpec((None, seq, 384), lambda b, hp: (b, 0, hp)),
                  pl.BlockSpec((None, seq, BLK), lambda b, hp: (b, 0, hp)),
                  pl.BlockSpec((None, seq, BLK), lambda b, hp: (b, 0, hp))] + [ANY] * n_ride,
        out_specs=[pl.BlockSpec((None, seq, 384), lambda b, hp: (b, 0, hp))] + [ANY] * n_ride,
        scratch_shapes=[pltpu.VMEM((2 * QB, BLK), F32), pltpu.VMEM((seq, BLK), F32), pltpu.VMEM((seq, BLK), F32),
                        *_exchange_sems(n_ride)],
        compiler_params=_params("arbitrary", "arbitrary"),
    )(proj_a, d_o, o_a, *ride)


def _rope_tables(seq):
    inv_freq = ROPE_THETA ** (-jnp.arange(32, dtype=F32) * 2.0 / 64)
    ang = jnp.arange(seq).astype(F32)[:, None] * inv_freq[None, :]
    cos, sin = jnp.cos(ang), jnp.sin(ang)
    return jnp.tile(cos, (1, 4)), jnp.concatenate([-sin, sin, -sin, sin], axis=1)


def _make_rope(n_rows):
    lane = lax.broadcasted_iota(jnp.int32, (n_rows, BLK), 1)
    first = (lane & 63) < 32

    def rope(x, cos, sin):
        partner = jnp.where(first, pltpu.roll(x, 96, 1), pltpu.roll(x, 32, 1))
        return x * cos + partner * sin

    return rope


DIL_UNROLL = 8


def _dil_consts():
    head0 = lax.broadcasted_iota(jnp.int32, (BLK, BLK), 1) < 64
    row = lax.broadcasted_iota(jnp.int32, (2 * BLK, 2 * BLK), 0) & (BLK - 1)
    col = lax.broadcasted_iota(jnp.int32, (2 * BLK, 2 * BLK), 1)
    valid_prev = jnp.logical_and(col < BLK, col >= row)
    valid_cur = jnp.logical_and(col >= BLK, row >= col - BLK)
    return head0, valid_prev, valid_cur


def _dil_blocks(dil, seq, block):
    nq = seq // dil // BLK

    def rows(r, i):
        if dil == 1:
            return pl.ds(pl.multiple_of(i * BLK, BLK), BLK)
        return pl.ds(r + (dil * BLK) * i, BLK, stride=dil)

    def step(t, carry):
        for u in range(DIL_UNROLL):
            n = t * DIL_UNROLL + u
            r, i = lax.div(n, nq), lax.rem(n, nq)
            block(rows(r, i), rows(r, jnp.maximum(i - 1, 0)), i)
        return carry

    lax.fori_loop(0, seq // BLK // DIL_UNROLL, step, 0)


def _dil_scores(qf, kf, vf, cur, prev, i, consts):
    head0, valid_prev, valid_cur = consts
    qs = _stack_heads(qf[cur, :].astype(BF16), head0)
    kcat = jnp.concatenate([kf[prev, :], kf[cur, :]], axis=0).astype(BF16)
    vcat = jnp.concatenate([vf[prev, :], vf[cur, :]], axis=0).astype(BF16)
    valid = jnp.logical_or(valid_cur, jnp.logical_and(valid_prev, i > 0))
    s = lax.dot_general(qs, kcat, NT, preferred_element_type=F32) * 0.125
    return qs, kcat, vcat, s, valid


def _head_cols(v):
    return jnp.concatenate([v[:, 0:1], v[:, 64:65]], axis=0)


def _dil_load_qkv(x_ref, c, rope, cos, sin, qf, kf, vf):
    qf[...] = rope(x_ref[:, c:c + 128].astype(F32), cos, sin).astype(BF16).astype(F32)
    kf[...] = rope(x_ref[:, c + 128:c + 256].astype(F32), cos, sin).astype(BF16).astype(F32)
    vf[...] = x_ref[:, c + 256:c + 384].astype(F32)


def _dil_fwd(proj_b, cos_t, sin_t, seq):
    bsz = proj_b.shape[0]

    def body(x_ref, cos_ref, sin_ref, ob_ref, lse_ref, qf, kf, vf, og, lg):
        consts = _dil_consts()
        head0 = consts[0]
        rope = _make_rope(seq)
        cos, sin = cos_ref[...], sin_ref[...]
        for g, dil in enumerate(DIL_GROUPS):
            _dil_load_qkv(x_ref, 384 * g, rope, cos, sin, qf, kf, vf)

            def block(cur, prev, i, g=g):
                _, _, vcat, s, valid = _dil_scores(qf, kf, vf, cur, prev, i, consts)
                s = jnp.where(valid, s, NEG_INF)
                m = jnp.max(s, axis=1, keepdims=True)
                p = jnp.exp(s - m)
                den = jnp.sum(p, axis=1, keepdims=True)
                o = jnp.dot(p.astype(BF16), vcat, preferred_element_type=F32) / den
                og[g, cur, :] = _unstack_heads(o, head0)
                lg[g, cur, :] = _unstack_heads(jnp.broadcast_to(m + jnp.log(den), (2 * BLK, BLK)), head0)

            _dil_blocks(dil, seq, block)
        ls = [lg[0], lg[1], lg[2]]
        m = jnp.maximum(jnp.maximum(ls[0], ls[1]), ls[2])
        ws = [jnp.exp(l - m) for l in ls]
        den = (ws[0] + ws[1]) + ws[2]
        ob_ref[...] = (((ws[0] * og[0] + ws[1] * og[1]) + ws[2] * og[2]) / den).astype(BF16)
        lse_ref[...] = m + jnp.log(den)

    tab_spec = pl.BlockSpec((seq, BLK), lambda b, hp: (0, 0))
    out_spec = pl.BlockSpec((None, seq, BLK), lambda b, hp: (b, 0, hp))
    slab = pltpu.VMEM((seq, BLK), F32)
    return pl.pallas_call(
        body, name="dil_attn_fwd", grid=(bsz, 2),
        out_shape=(jax.ShapeDtypeStruct((bsz, seq, 256), BF16), jax.ShapeDtypeStruct((bsz, seq, 256), F32)),
        in_specs=[pl.BlockSpec((None, seq, WB // 2), lambda b, hp: (b, 0, hp)), tab_spec, tab_spec],
        out_specs=(out_spec, out_spec),
        scratch_shapes=[slab, slab, slab, pltpu.VMEM((3, seq, BLK), F32), pltpu.VMEM((3, seq, BLK), F32)],
        compiler_params=_params("parallel", "parallel"),
    )(proj_b, cos_t, sin_t)


def _dil_bwd(proj_b, cos_t, sin_t, d_ob, o_b, lse, seq):
    bsz = proj_b.shape[0]

    def body(x_ref, cos_ref, sin_ref, do_ref, ob_ref, lse_ref, d_ref, qf, kf, vf, dof, dsf, dq_s, dk_acc, dv_acc):
        consts = _dil_consts()
        head0 = consts[0]
        rope = _make_rope(seq)
        cos, sin = cos_ref[...], sin_ref[...]
        do_all = do_ref[...].astype(F32)
        dof[...] = do_all
        head0_all = lax.broadcasted_iota(jnp.int32, (seq, BLK), 1) < 64
        d0, d1 = _head_rowsum(do_all * ob_ref[...].astype(F32), head0_all)
        dsf[...] = jnp.where(head0_all, d0, d1)
        for g, dil in enumerate(DIL_GROUPS):
            _dil_load_qkv(x_ref, 384 * g, rope, cos, sin, qf, kf, vf)
            dk_acc[...] = jnp.zeros_like(dk_acc)
            dv_acc[...] = jnp.zeros_like(dv_acc)

            def block(cur, prev, i):
                qs, kcat, vcat, s, valid = _dil_scores(qf, kf, vf, cur, prev, i, consts)
                dos = _stack_heads(dof[cur, :].astype(BF16), head0)
                p = jnp.where(valid, jnp.exp(s - _head_cols(lse_ref[cur, :])), 0.0)
                dp = lax.dot_general(dos, vcat, NT, preferred_element_type=F32)
                ds = ((p * (dp - _head_cols(dsf[cur, :]))) * 0.125).astype(BF16)
                dq_s[cur, :] = _unstack_heads(jnp.dot(ds, kcat, preferred_element_type=F32), head0)
                dk = lax.dot_general(ds, qs, TN, preferred_element_type=F32)
                dv = lax.dot_general(p.astype(BF16), dos, TN, preferred_element_type=F32)
                dk_acc[prev, :] += dk[:BLK]
                dk_acc[cur, :] += dk[BLK:]
                dv_acc[prev, :] += dv[:BLK]
                dv_acc[cur, :] += dv[BLK:]

            _dil_blocks(dil, seq, block)
            c = 384 * g
            d_ref[:, c:c + 128] = rope(dq_s[...], cos, -sin).astype(BF16)
            d_ref[:, c + 128:c + 256] = rope(dk_acc[...], cos, -sin).astype(BF16)
            d_ref[:, c + 256:c + 384] = dv_acc[...].astype(BF16)

    x_spec = pl.BlockSpec((None, seq, WB // 2), lambda b, hp: (b, 0, hp))
    tab_spec = pl.BlockSpec((seq, BLK), lambda b, hp: (0, 0))
    tok_spec = pl.BlockSpec((None, seq, BLK), lambda b, hp: (b, 0, hp))
    return pl.pallas_call(
        body, name="dil_attn_bwd", grid=(bsz, 2),
        out_shape=jax.ShapeDtypeStruct((bsz, seq, WB), BF16),
        in_specs=[x_spec, tab_spec, tab_spec, tok_spec, tok_spec, tok_spec], out_specs=x_spec,
        scratch_shapes=[pltpu.VMEM((seq, BLK), F32)] * 8,
        compiler_params=_params("parallel", "parallel"),
    )(proj_b, cos_t, sin_t, d_ob, o_b, lse)


MEM_SCALE = 128 ** -0.5
MEM_QB = 2048


def _mem_fwd(proj_a, kv, seq):
    bsz = proj_a.shape[0]

    def body(q_ref, k_ref, v_ref, o_ref):
        k, v = k_ref[...], v_ref[...]

        def qblock(i, carry):
            r0 = pl.multiple_of(i * MEM_QB, MEM_QB)
            s = lax.dot_general(q_ref[pl.ds(r0, MEM_QB), :], k, NT, preferred_element_type=F32) * MEM_SCALE
            p = jnp.exp(s - jnp.max(s, axis=1, keepdims=True))
            p = p / jnp.sum(p, axis=1, keepdims=True)
            o_ref[pl.ds(r0, MEM_QB), :] = jnp.dot(p.astype(BF16), v, preferred_element_type=F32).astype(BF16)
            return carry

        lax.fori_loop(0, seq // MEM_QB, qblock, 0)

    return pl.pallas_call(
        body, name="mem_attn_fwd", grid=(bsz, 4),
        out_shape=jax.ShapeDtypeStruct((bsz, seq, 512), BF16),
        in_specs=[pl.BlockSpec((None, seq, BLK), lambda b, h: (b, 0, 12 + h)),
                  pl.BlockSpec((None, MEM_LEN, BLK), lambda b, h: (b, 0, h)),
                  pl.BlockSpec((None, MEM_LEN, BLK), lambda b, h: (b, 0, 4 + h))],
        out_specs=pl.BlockSpec((None, seq, BLK), lambda b, h: (b, 0, h)),
        compiler_params=_params("parallel", "parallel"),
    )(proj_a, kv, kv)


def _mem_bwd(proj_a, kv, d_o, d_proj_a, seq):
    bsz = proj_a.shape[0]

    def body(q_ref, k_ref, v_ref, do_ref, _, dq_ref, dk_ref, dv_ref):
        k, v = k_ref[...], v_ref[...]

        def qblock(i, carry):
            dk, dv = carry
            r0 = pl.multiple_of(i * MEM_QB, MEM_QB)
            q, do = q_ref[pl.ds(r0, MEM_QB), :], do_ref[pl.ds(r0, MEM_QB), :]
            s = lax.dot_general(q, k, NT, preferred_element_type=F32) * MEM_SCALE
            p = jnp.exp(s - jnp.max(s, axis=1, keepdims=True))
            p = p / jnp.sum(p, axis=1, keepdims=True)
            dp = lax.dot_general(do, v, NT, preferred_element_type=F32)
            ds = ((p * (dp - jnp.sum(p * dp, axis=1, keepdims=True))) * MEM_SCALE).astype(BF16)
            dq_ref[pl.ds(r0, MEM_QB), :] = jnp.dot(ds, k, preferred_element_type=F32).astype(BF16)
            dk = dk + lax.dot_general(ds, q, TN, preferred_element_type=F32)
            dv = dv + lax.dot_general(p.astype(BF16), do, TN, preferred_element_type=F32)
            return dk, dv

        zero = jnp.zeros((MEM_LEN, BLK), F32)
        dk, dv = lax.fori_loop(0, seq // MEM_QB, qblock, (zero, zero))
        dk_ref[...] = dk.astype(BF16)
        dv_ref[...] = dv.astype(BF16)

    kv_spec = pl.BlockSpec((None, MEM_LEN, BLK), lambda b, h: (b, 0, h))
    return pl.pallas_call(
        body, name="mem_attn_bwd", grid=(bsz, 4),
        out_shape=(jax.ShapeDtypeStruct((bsz, seq, WA), BF16), jax.ShapeDtypeStruct((bsz, MEM_LEN, 512), BF16),
                   jax.ShapeDtypeStruct((bsz, MEM_LEN, 512), BF16)),
        in_specs=[pl.BlockSpec((None, seq, BLK), lambda b, h: (b, 0, 12 + h)), kv_spec,
                  pl.BlockSpec((None, MEM_LEN, BLK), lambda b, h: (b, 0, 4 + h)),
                  pl.BlockSpec((None, seq, BLK), lambda b, h: (b, 0, h)), ANY],
        out_specs=(pl.BlockSpec((None, seq, BLK), lambda b, h: (b, 0, 12 + h)), kv_spec, kv_spec),
        input_output_aliases={4: 0},
        compiler_params=_params("parallel", "parallel"),
    )(proj_a, kv, kv, d_o, d_proj_a)


def _mesh_pos():
    return lax.axis_index("x"), lax.axis_index("y"), lax.axis_index("c")


def _all_gather(shard, name):
    m_per, n = shard.shape

    def body(x_ref, out_ref, send_sems, recv_sems, local_sem):
        x, y, c = _mesh_pos()
        me, sibling = (x, y, c), (x, y, 1 - c)
        chips = [(1 - x, y), (x, 1 - y), (1 - x, 1 - y)]

        def rows(px, py, pc):
            return out_ref.at[pl.ds((4 * px + 2 * py + pc) * m_per, m_per), :]

        def copy(k, block, to, src=None):
            return pltpu.make_async_remote_copy(
                src_ref=rows(*block) if src is None else src, dst_ref=rows(*block),
                send_sem=send_sems.at[k], recv_sem=recv_sems.at[k], device_id=to, device_id_type=MESH)

        mine = pltpu.make_async_copy(x_ref, rows(*me), local_sem)
        mine.start()
        first = [copy(0, me, sibling, src=x_ref)]
        first += [copy(1 + j, me, (*chip, c), src=x_ref) for j, chip in enumerate(chips)]
        for cp in first:
            cp.start()
        passed = [copy(4 + j, (*chip, c), sibling) for j, chip in enumerate(chips)]
        for j, chip in enumerate(chips):
            copy(1 + j, (*chip, c), me).wait_recv()
            passed[j].start()
        copy(0, sibling, me).wait_recv()
        for j, chip in enumerate(chips):
            copy(4 + j, (*chip, 1 - c), me).wait_recv()
        for cp in first + passed:
            cp.wait_send()
        mine.wait()

    return pl.pallas_call(
        body, name=name, out_shape=jax.ShapeDtypeStruct((N_DEV * m_per, n), shard.dtype),
        in_specs=[ANY], out_specs=ANY,
        scratch_shapes=[pltpu.SemaphoreType.DMA((7,)), pltpu.SemaphoreType.DMA((7,)), pltpu.SemaphoreType.DMA(())],
    )(shard)


def _exchange_sems(n_arrays):
    return [pltpu.SemaphoreType.DMA((7 * n_arrays,)), pltpu.SemaphoreType.DMA((7 * n_arrays,)),
            pltpu.SemaphoreType.DMA((n_arrays,))]


def _exchange_out(srcs, gather):
    return [jax.ShapeDtypeStruct((N_DEV, *s.shape[-2:]), s.dtype) for s in srcs]


def _direct_exchange(src_refs, dst_refs, send_sems, recv_sems, local_sems, gather):
    x, y, c = _mesh_pos()
    me = 4 * x + 2 * y + c
    owns, sends, recvs = [], [], []
    for a, (src, dst) in enumerate(zip(src_refs, dst_refs)):
        owns.append(pltpu.make_async_copy(src if gather else src.at[me], dst.at[me], local_sems.at[a]))
        for j in range(1, N_DEV):
            px = 1 - x if j & 4 else x
            py = 1 - y if j & 2 else y
            pc = 1 - c if j & 1 else c
            peer = 4 * px + 2 * py + pc
            sems = dict(send_sem=send_sems.at[7 * a + j - 1], recv_sem=recv_sems.at[7 * a + j - 1],
                        device_id=(px, py, pc), device_id_type=MESH)
            sends.append(pltpu.make_async_remote_copy(
                src_ref=src if gather else src.at[peer], dst_ref=dst.at[me], **sems))
            recvs.append(pltpu.make_async_remote_copy(
                src_ref=src if gather else src.at[me], dst_ref=dst.at[peer], **sems))

    def start():
        for cp in owns + sends:
            cp.start()

    def wait():
        for cp in recvs:
            cp.wait_recv()
        for cp in sends:
            cp.wait_send()
        for cp in owns:
            cp.wait()

    return start, wait


def _riding_exchange(src_refs, dst_refs, sems, gather):
    start, wait = _direct_exchange(src_refs, dst_refs, *sems, gather)
    ids = [pl.program_id(a) for a in range(2)]
    last = [pl.num_programs(a) - 1 for a in range(2)]
    pl.when(jnp.logical_and(ids[0] == 0, ids[1] == 0))(start)
    return lambda: pl.when(jnp.logical_and(ids[0] == last[0], ids[1] == last[1]))(wait)


def _exchange(srcs, gather, name):
    n = len(srcs)

    def body(*refs):
        start, wait = _direct_exchange(refs[:n], refs[n:2 * n], *refs[2 * n:], gather=gather)
        start()
        wait()

    return pl.pallas_call(
        body, name=name, out_shape=_exchange_out(srcs, gather),
        in_specs=[ANY] * n, out_specs=[ANY] * n, scratch_shapes=_exchange_sems(n),
    )(*srcs)


def _adamw(w, g, m, v):
    m = ADAM_B1 * m + (1.0 - ADAM_B1) * g
    v = ADAM_B2 * v + (1.0 - ADAM_B2) * (g * g)
    m_hat = m / (1.0 - ADAM_B1 ** ADAM_STEP)
    v_hat = v / (1.0 - ADAM_B2 ** ADAM_STEP)
    return -ADAM_LR * (m_hat / (jnp.sqrt(v_hat) + ADAM_EPS) + ADAM_WD * w), m, v


def _reduce_adamw(recv, w, m, v, name):
    _, k, n = w.shape
    tr = max(t for t in range(16, 257, 16) if k % t == 0)

    def body(r_ref, w_ref, m_ref, v_ref, g_out, d_out, m_out, v_out):
        g = r_ref[0].astype(F32)
        for s in range(1, N_DEV):
            g = g + r_ref[s].astype(F32)
        g_out[...] = g
        d_out[...], m_out[...], v_out[...] = _adamw(w_ref[...], g, m_ref[...], v_ref[...])

    spec = pl.BlockSpec((None, tr, n), lambda i: (0, i, 0))
    return pl.pallas_call(
        body, name=name, grid=(k // tr,),
        out_shape=[jax.ShapeDtypeStruct((1, k, n), F32)] * 4,
        in_specs=[pl.BlockSpec((N_DEV, tr, n), lambda i: (0, i, 0)), spec, spec, spec],
        out_specs=[spec] * 4, compiler_params=_params("arbitrary"),
    )(recv, w, m, v)


def _small_adamw(gathered, w, m, v):
    def body(g_ref, w_ref, m_ref, v_ref, g_out, d_out, m_out, v_out, loss_out):
        tot = g_ref[0]
        for s in range(1, N_DEV):
            tot = tot + g_ref[s]
        g = tot[0:8]
        g_out[...] = g
        d_out[...], m_out[...], v_out[...] = _adamw(w_ref[...], g, m_ref[...], v_ref[...])
        loss_out[...] = jnp.broadcast_to((0.5 / D) * jnp.sum(tot[8:9], axis=1, keepdims=True), (8, BLK))

    out = [jax.ShapeDtypeStruct((8, D), F32)] * 4 + [jax.ShapeDtypeStruct((8, BLK), F32)]
    return pl.pallas_call(body, name="small_adamw", out_shape=out, compiler_params=_params())(gathered, w, m, v)


def _pick_chunks(w, chunks):
    return jnp.concatenate([w[:, BLK * c:BLK * (c + 1)] for c in chunks], axis=1)


def _whole_weight(gathered, i):
    _, k, n = gathered.shape
    if BY_ROWS[i]:
        return gathered.reshape(N_DEV * k, n)
    return gathered.transpose(1, 0, 2).reshape(k, N_DEV * n)


def _shard_parts(grad, i):
    if BY_ROWS[i]:
        return grad.reshape(N_DEV, grad.shape[0] // N_DEV, grad.shape[1])
    k, n8 = grad.shape
    return grad.reshape(k, N_DEV, n8 // N_DEV).transpose(1, 0, 2)


def kernel(x, mem, g_pre_mix, g_post_mix, g_pre_ffn, g_post_ffn, g_mem, w_in, w_mem_kv, w_br_sb, w_br_dil, w_br_mem, w_gate, b_gate, w_o, w_ffn_in, w_ffn_out, loss_target, m_g_pre_mix, m_g_post_mix, m_g_pre_ffn, m_g_post_ffn, m_g_mem, m_w_in, m_w_mem_kv, m_w_br_sb, m_w_br_dil, m_w_br_mem, m_w_gate, m_b_gate, m_w_o, m_w_ffn_in, m_w_ffn_out, v_g_pre_mix, v_g_post_mix, v_g_pre_ffn, v_g_post_ffn, v_g_mem, v_w_in, v_w_mem_kv, v_w_br_sb, v_w_br_dil, v_w_br_mem, v_w_gate, v_b_gate, v_w_o, v_w_ffn_in, v_w_ffn_out):
    bsz, seq, _ = x.shape
    tokens = bsz * seq
    xf, tgt, memf = x.reshape(tokens, D), loss_target.reshape(tokens, D), mem.reshape(bsz * MEM_LEN, D)
    big_w = [w_in, w_mem_kv, w_br_sb, w_br_dil, w_br_mem, w_gate, w_o, w_ffn_in, w_ffn_out]
    big_m = [m_w_in, m_w_mem_kv, m_w_br_sb, m_w_br_dil, m_w_br_mem, m_w_gate, m_w_o, m_w_ffn_in, m_w_ffn_out]
    big_v = [v_w_in, v_w_mem_kv, v_w_br_sb, v_w_br_dil, v_w_br_mem, v_w_gate, v_w_o, v_w_ffn_in, v_w_ffn_out]

    shards = [w[0].astype(BF16) for w in big_w]
    k_in, n_in = shards[0].shape
    fw_in = _whole_weight(_all_gather(shards[0], "weight_all_gather").reshape(N_DEV, k_in, n_in), 0)
    w_a, w_b = _pick_chunks(fw_in, CHUNKS_A), _pick_chunks(fw_in, CHUNKS_B)

    h = _norm_fwd(xf, g_pre_mix, "pre_mix_norm")
    proj_a = _matmul(h, w_a, "nn", BF16, "proj_a").reshape(bsz, seq, WA)
    proj_b = _matmul(h, w_b, "nn", BF16, "proj_b").reshape(bsz, seq, WB)
    o_a, o_a32, *behind = _sb_fwd(proj_a, seq, [shards[i] for i in GATHER_BEHIND])
    fw_mem_kv, fw_br_sb, fw_br_dil, fw_br_mem, fw_gate, fw_o, fw_ffn_in, fw_ffn_out = (
        _whole_weight(g, i) for g, i in zip(behind, GATHER_BEHIND))
    gpre = _matmul(h, fw_gate, "nn", BF16, "gate_proj")
    cos_t, sin_t = _rope_tables(seq)
    o_b, lse_b = _dil_fwd(proj_b, cos_t, sin_t, seq)
    mn = _norm_fwd(memf, g_mem, "mem_norm")
    kv = _matmul(mn, fw_mem_kv, "nn", BF16, "mem_kv_proj").reshape(bsz, MEM_LEN, D)
    o_c = _mem_fwd(proj_a, kv, seq)
    o_a2, o_b2, o_c2 = o_a.reshape(tokens, 512), o_b.reshape(tokens, 256), o_c.reshape(tokens, 512)
    ys = [_matmul(o_a2, fw_br_sb, "nn", BF16, "branch_sb"), _matmul(o_b2, fw_br_dil, "nn", BF16, "branch_dil"),
          _matmul(o_c2, fw_br_mem, "nn", BF16, "branch_mem")]
    merged, mix, x1, h2 = _merge_out_proj_norm(gpre, ys, b_gate, fw_o, xf, g_post_mix, g_pre_ffn)
    gu_a, gu_b, f = _ffn_in_swiglu(h2, fw_ffn_in)
    dy, dfo, dg_post_ffn, loss_lanes = _ffn_out_loss(f, fw_ffn_out, x1, tgt, g_post_ffn)

    gw_ffn_out = _matmul(f, dfo, "tn", BF16, "gw_ffn_out")
    dgu = _d_ffn_swiglu_bwd(dfo, fw_ffn_out, gu_a, gu_b)
    gw_ffn_in = _matmul(h2, dgu, "tn", BF16, "gw_ffn_in")
    dx1, dmix, dg_pre_ffn, dg_post_mix = _d_h2_norm_bwd(dgu, fw_ffn_in, x1, dy, mix, g_pre_ffn, g_post_mix)
    gw_o = _matmul(merged, dmix, "tn", BF16, "gw_o")
    dya, dyb, dyc, dgpre, db_gate = _d_merged_gate_bwd(dmix, fw_o, gpre, ys, b_gate)
    d_oa = _matmul(dya, fw_br_sb, "nt", BF16, "d_o_sb").reshape(bsz, seq, 512)
    d_ob = _matmul(dyb, fw_br_dil, "nt", BF16, "d_o_dil").reshape(bsz, seq, 256)
    d_oc = _matmul(dyc, fw_br_mem, "nt", BF16, "d_o_mem").reshape(bsz, seq, 512)
    gw_br_sb = _matmul(o_a2, dya, "tn", BF16, "gw_br_sb")
    gw_br_dil = _matmul(o_b2, dyb, "tn", BF16, "gw_br_dil")
    gw_br_mem = _matmul(o_c2, dyc, "tn", BF16, "gw_br_mem")
    gw_gate = _matmul(h, dgpre, "tn", BF16, "gw_gate")
    grads = {2: gw_br_sb, 3: gw_br_dil, 4: gw_br_mem, 5: gw_gate, 6: gw_o, 7: gw_ffn_in, 8: gw_ffn_out}
    d_proj_a, *recv_behind = _sb_bwd(proj_a, d_oa, o_a32, seq, [_shard_parts(grads[i], i) for i in REDUCE_BEHIND])
    d_proj_a, dk_m, dv_m = _mem_bwd(proj_a, kv, d_oc, d_proj_a, seq)
    d_proj_b = _dil_bwd(proj_b, cos_t, sin_t, d_ob, o_b, lse_b, seq).reshape(tokens, WB)
    d_proj_a = d_proj_a.reshape(tokens, WA)
    gw_a = _matmul(h, d_proj_a, "tn", BF16, "gw_in_a")
    gw_b = _matmul(h, d_proj_b, "tn", BF16, "gw_in_b")
    dkv = jnp.concatenate([dk_m, dv_m], axis=-1).reshape(bsz * MEM_LEN, D)
    gw_mem_kv = _matmul(mn, dkv, "tn", BF16, "gw_mem_kv")
    dmn = _matmul(dkv, fw_mem_kv, "nt", F32, "d_mem_norm")
    dg_mem = _gain_grad(dmn, memf)
    gw_ab = jnp.concatenate([gw_a, gw_b], axis=1)
    where = {c: i for i, c in enumerate(CHUNKS_A + CHUNKS_B)}
    grads = {0: _pick_chunks(gw_ab, [where[c] for c in range(34)]), 1: gw_mem_kv}
    dx, dg_pre_mix, *recv_last = _d_h_norm_bwd(
        [(dgpre, fw_gate), (d_proj_a, w_a), (d_proj_b, w_b)], xf, dx1, g_pre_mix,
        [_shard_parts(grads[i], i) for i in REDUCE_LAST])

    received = dict(zip(REDUCE_BEHIND + REDUCE_LAST, [*recv_behind, *recv_last]))
    adam = [_reduce_adamw(received[i], big_w[i], big_m[i], big_v[i], "reduce_adamw_" + BIG_NAMES[i])
            for i in range(len(big_w))]
    big = [[a[k] for a in adam] for k in range(4)]

    small = jnp.concatenate([dg_pre_mix, dg_post_mix, dg_pre_ffn, dg_post_ffn, dg_mem, db_gate.reshape(3, D),
                             loss_lanes, jnp.zeros((7, D), F32)], axis=0)
    small_all, = _exchange([small], True, "small_all_gather")

    def small_pack(gs, b):
        return jnp.concatenate([*gs, b.reshape(3, D)], axis=0)

    sm = _small_adamw(
        small_all, small_pack([g_pre_mix, g_post_mix, g_pre_ffn, g_post_ffn, g_mem], b_gate),
        small_pack([m_g_pre_mix, m_g_post_mix, m_g_pre_ffn, m_g_post_ffn, m_g_mem], m_b_gate),
        small_pack([v_g_pre_mix, v_g_post_mix, v_g_pre_ffn, v_g_post_ffn, v_g_mem], v_b_gate))
    loss = sm[4][0, 0]

    def leaves(k):
        t, bw = sm[k], big[k]
        return [t[0:1], t[1:2], t[2:3], t[3:4], t[4:5], *bw[0:6], t[5:8].reshape(1, 3 * D), *bw[6:9]]

    return (loss, dx.reshape(bsz, seq, D), *leaves(0), *leaves(1), *leaves(2), *leaves(3))
```

```python
import functools

import jax
import jax.numpy as jnp
from jax import lax
from jax.experimental import pallas as pl
from jax.experimental.pallas import tpu as pltpu

F32 = jnp.float32
BF16 = jnp.bfloat16
D = 1024
BLK = 128
MEM_LEN = 256
D_FF = 2816
NORM_EPS = 1e-6
NEG_INF = -1e30
ROPE_THETA = 10000.0
ADAM_LR, ADAM_B1, ADAM_B2, ADAM_EPS, ADAM_WD, ADAM_STEP = 0.001, 0.9, 0.999, 1e-08, 0.01, 10
N_DEV = 8
VMEM_LIMIT_BYTES = 56 * 1024 * 1024
MESH = pl.DeviceIdType.MESH
ANY = pl.BlockSpec(memory_space=pl.ANY)

NT = (((1,), (1,)), ((), ()))
TN = (((0,), (0,)), ((), ()))
NN = (((1,), (0,)), ((), ()))
_DIMS = {"nn": NN, "nt": NT, "tn": TN}

BIG_NAMES = ("w_in", "w_mem_kv", "w_br_sb", "w_br_dil", "w_br_mem", "w_gate", "w_o", "w_ffn_in", "w_ffn_out")
BY_ROWS = (False, True, False, False, False, False, True, False, True)
GATHER_FIRST = (0,)
GATHER_BEHIND = (1, 2, 3, 4, 5, 6, 7, 8)
REDUCE_BEHIND = (2, 3, 4, 5, 6, 7, 8)
REDUCE_LAST = (0, 1)

CHUNKS_A = tuple(c for hp in range(4) for c in (hp, 4 + hp, 8 + hp)) + (30, 31, 32, 33)
CHUNKS_B = tuple(c for hp in range(2) for g in range(3) for c in (12 + 6 * g + hp, 14 + 6 * g + hp, 16 + 6 * g + hp))
WA, WB = 128 * len(CHUNKS_A), 128 * len(CHUNKS_B)
DIL_GROUPS = (1, 4, 16)


def _params(*sem):
    return pltpu.CompilerParams(dimension_semantics=sem or None, vmem_limit_bytes=VMEM_LIMIT_BYTES)


def _tile(n, cap):
    if n <= 128:
        return n
    assert n % 128 == 0, n
    best = 128
    for t in range(128, min(n, cap) + 1, 128):
        if n % t == 0:
            best = t
    return best


def _k_steps(k, nk, step):
    if nk == 1:
        step(True, True)
        return
    pl.when(k == 0)(functools.partial(step, True, False))
    if nk > 2:
        pl.when(jnp.logical_and(k > 0, k < nk - 1))(functools.partial(step, False, False))
    pl.when(k == nk - 1)(functools.partial(step, False, True))


def _matmul(a, b, mode, out_dtype, name, tm_cap=1536, tn_cap=1536, tk_cap=1536):
    if mode == "tn":
        (K, M), N = a.shape, b.shape[1]
    elif mode == "nt":
        (M, K), N = a.shape, b.shape[0]
    else:
        (M, K), N = a.shape, b.shape[1]
    if mode == "tn":
        tk_cap = 2 * tk_cap
    tm, tn, tk = _tile(M, tm_cap), _tile(N, tn_cap), _tile(K, tk_cap)
    nm, nn, nk = M // tm, N // tn, K // tk
    dims = _DIMS[mode]

    def body(a_ref, b_ref, o_ref, *acc):
        def step(first, last):
            d = lax.dot_general(a_ref[...], b_ref[...], dims, preferred_element_type=F32)
            if not first:
                d = d + acc[0][...]
            if last:
                o_ref[...] = d.astype(o_ref.dtype)
            else:
                acc[0][...] = d

        _k_steps(pl.program_id(2), nk, step)

    n_outer = nk == 1 and (a.size * nn + b.size) < (a.size + b.size * nm)
    if n_outer:
        grid, ij = (nn, nm, nk), (lambda g0, g1: (g1, g0))
    else:
        grid, ij = (nm, nn, nk), (lambda g0, g1: (g0, g1))
    if mode == "tn":
        a_spec = pl.BlockSpec((tk, tm), lambda g0, g1, k: (k, ij(g0, g1)[0]))
    else:
        a_spec = pl.BlockSpec((tm, tk), lambda g0, g1, k: (ij(g0, g1)[0], k))
    if mode == "nt":
        b_spec = pl.BlockSpec((tn, tk), lambda g0, g1, k: (ij(g0, g1)[1], k))
    else:
        b_spec = pl.BlockSpec((tk, tn), lambda g0, g1, k: (k, ij(g0, g1)[1]))
    return pl.pallas_call(
        body, name=name, grid=grid,
        out_shape=jax.ShapeDtypeStruct((M, N), out_dtype),
        in_specs=[a_spec, b_spec],
        out_specs=pl.BlockSpec((tm, tn), lambda g0, g1, k: ij(g0, g1)),
        scratch_shapes=[pltpu.VMEM((tm, tn), F32)] if nk > 1 else [],
        compiler_params=_params("parallel", "parallel", "arbitrary"),
    )(a, b)


def _rowwise(body, name, rows, tr, row_ins, vec_ins, row_outs, acc_outs=()):
    tr = min(tr, rows)
    assert rows % tr == 0
    in_specs, args = [], []
    for r in row_ins:
        arr, w, cb = r if isinstance(r, tuple) else (r, r.shape[1], 0)
        in_specs.append(pl.BlockSpec((tr, w), functools.partial(lambda i, cb: (i, cb), cb=cb)))
        args.append(arr)
    for v in vec_ins:
        in_specs.append(pl.BlockSpec(v.shape, lambda i: (0, 0)))
        args.append(v)
    out_shape = [jax.ShapeDtypeStruct((rows, w), dt) for w, dt in row_outs]
    out_shape += [jax.ShapeDtypeStruct((1, w), F32) for w in acc_outs]
    out_specs = [pl.BlockSpec((tr, w), lambda i: (i, 0)) for w, _ in row_outs]
    out_specs += [pl.BlockSpec((1, w), lambda i: (0, 0)) for w in acc_outs]
    n_acc = len(acc_outs)

    def wrapped(*refs):
        if n_acc:
            @pl.when(pl.program_id(0) == 0)
            def _():
                for r in refs[len(refs) - n_acc:]:
                    r[...] = jnp.zeros_like(r)
        body(*refs)

    return pl.pallas_call(
        wrapped, name=name, grid=(rows // tr,), out_shape=out_shape, in_specs=in_specs, out_specs=out_specs,
        compiler_params=_params("arbitrary"),
    )(*args)


def _rstd(x):
    return lax.rsqrt(jnp.mean(x * x, axis=-1, keepdims=True) + NORM_EPS)


def _norm_bwd(u, n, r):
    return r * (u - n * jnp.mean(u * n, axis=-1, keepdims=True))


def _colsum(v):
    return jnp.sum(v, axis=0, keepdims=True)


def _norm_fwd(x, g, name):
    def body(x_ref, g_ref, h_ref):
        xv = x_ref[...]
        h_ref[...] = ((xv * _rstd(xv)) * g_ref[...]).astype(BF16)

    return _rowwise(body, name, x.shape[0], 512, [x], [g], [(D, BF16)])[0]


def _matmul_rows(pairs, mode, name, epilogue, row_ins=(), vec_ins=(), row_outs=(), acc_outs=(), ride=None,
                 tm=512, tk_cap=1536, epi_rows=None):
    M = pairs[0][0].shape[0]
    N = pairs[0][1].shape[1] if mode == "nn" else pairs[0][1].shape[0]
    tm = min(tm, M)
    tks = [_tile(a.shape[1], tk_cap) for a, _ in pairs]
    nks = [a.shape[1] // tk for (a, _), tk in zip(pairs, tks)]
    offs = [sum(nks[:p]) for p in range(len(pairs))]
    nm, nk = M // tm, sum(nks)
    dims = _DIMS[mode]
    n_ab, n_extra, n_out = 2 * len(pairs), len(row_ins) + len(vec_ins), len(row_outs) + len(acc_outs)
    n_ride = 0 if ride is None else len(ride)

    def body(*refs):
        ab, extra, rest = refs[:n_ab], refs[n_ab:n_ab + n_extra], refs[n_ab + n_extra:]
        ride_refs, outs, rest = rest[:n_ride], rest[n_ride:n_ride + n_out], rest[n_ride + n_out:]
        received_refs, rest = rest[:n_ride], rest[n_ride:]
        if n_ride:
            finish_ride = _riding_exchange(ride_refs, received_refs, rest[len(rest) - 3:], gather=False)
        i, k = pl.program_id(0), pl.program_id(1)
        if acc_outs:
            @pl.when(jnp.logical_and(i == 0, k == 0))
            def _():
                for r in outs[len(row_outs):]:
                    r[...] = jnp.zeros_like(r)

        def step(p, first, last):
            d = lax.dot_general(ab[2 * p][...], ab[2 * p + 1][...], dims, preferred_element_type=F32)
            if not first:
                d = d + rest[0][...]
            if not last:
                rest[0][...] = d
            elif epi_rows is None:
                epilogue(d, *extra, *outs)
            else:
                rest[0][...] = d
                for c in range(tm // epi_rows):
                    rows = pl.ds(c * epi_rows, epi_rows)
                    sliced = [r.at[rows] for r in extra[:len(row_ins)]] + list(extra[len(row_ins):])
                    sliced += [r.at[rows] for r in outs[:len(row_outs)]] + list(outs[len(row_outs):])
                    epilogue(rest[0][rows, :], *sliced)

        last_p = len(pairs) - 1
        if nk == 1:
            step(0, True, True)
        else:
            pl.when(k == 0)(functools.partial(step, 0, True, False))
            for p in range(len(pairs)):
                lo, hi = max(offs[p], 1), min(offs[p] + nks[p], nk - 1)
                if hi > lo:
                    pl.when(jnp.logical_and(k >= lo, k < hi))(functools.partial(step, p, False, False))
            pl.when(k == nk - 1)(functools.partial(step, last_p, False, True))
        if n_ride:
            finish_ride()

    in_specs, args = [], []
    for p, ((a, b), tk) in enumerate(zip(pairs, tks)):
        step = functools.partial(lambda k, p: jnp.clip(k - offs[p], 0, nks[p] - 1), p=p)
        in_specs.append(pl.BlockSpec((tm, tk), functools.partial(lambda i, k, step: (i, step(k)), step=step)))
        if mode == "nn":
            in_specs.append(pl.BlockSpec((tk, N), functools.partial(lambda i, k, step: (step(k), 0), step=step)))
        else:
            in_specs.append(pl.BlockSpec((N, tk), functools.partial(lambda i, k, step: (0, step(k)), step=step)))
        args += [a, b]
    in_specs += [pl.BlockSpec((tm, r.shape[1]), lambda i, k: (i, 0)) for r in row_ins]
    in_specs += [pl.BlockSpec(v.shape, lambda i, k: (0, 0)) for v in vec_ins]
    in_specs += [ANY] * n_ride
    out_shape = [jax.ShapeDtypeStruct((M, w), dt) for w, dt in row_outs]
    out_shape += [jax.ShapeDtypeStruct((1, w), F32) for w in acc_outs]
    out_specs = [pl.BlockSpec((tm, w), lambda i, k: (i, 0)) for w, _ in row_outs]
    out_specs += [pl.BlockSpec((1, w), lambda i, k: (0, 0)) for w in acc_outs]
    scratch = [pltpu.VMEM((tm, N), F32)] if nk > 1 else []
    if n_ride:
        out_shape += _exchange_out(ride, False)
        out_specs += [ANY] * n_ride
        scratch += _exchange_sems(n_ride)
    return pl.pallas_call(
        body, name=name, grid=(nm, nk), out_shape=out_shape, in_specs=in_specs, out_specs=out_specs,
        scratch_shapes=scratch, compiler_params=_params("arbitrary", "arbitrary"),
    )(*args, *row_ins, *vec_ins, *(ride or []))


def _merge_out_proj_norm(gpre, ys, b_gate, w_o, x, g_post, g_pre):
    tokens = x.shape[0]
    tm = min(512, tokens)

    def body(gp_ref, ya_ref, yb_ref, yc_ref, b_ref, w_ref, x_ref, g2_ref, g3_ref, m_ref, mix_ref, x1_ref, h2_ref):
        acc = None
        for k, y_ref in enumerate((ya_ref, yb_ref, yc_ref)):
            cols = slice(k * D, (k + 1) * D)
            gate = jax.nn.sigmoid(gp_ref[:, cols].astype(F32) + b_ref[:, cols])
            term = gate * y_ref[...].astype(F32)
            acc = term if acc is None else acc + term
        merged = acc.astype(BF16)
        m_ref[...] = merged
        mv = jnp.dot(merged, w_ref[...], preferred_element_type=F32)
        mix_ref[...] = mv
        x1 = x_ref[...] + (mv * _rstd(mv)) * g2_ref[...]
        x1_ref[...] = x1
        h2_ref[...] = ((x1 * _rstd(x1)) * g3_ref[...]).astype(BF16)

    def rows(w):
        return pl.BlockSpec((tm, w), lambda i: (i, 0))

    def whole(a):
        return pl.BlockSpec(a.shape, lambda i: (0, 0))

    return pl.pallas_call(
        body, name="merge_out_proj_norm", grid=(tokens // tm,),
        out_shape=[jax.ShapeDtypeStruct((tokens, D), dt) for dt in (BF16, F32, F32, BF16)],
        in_specs=[rows(3 * D), rows(D), rows(D), rows(D), whole(b_gate), whole(w_o), rows(D), whole(g_post),
                  whole(g_pre)],
        out_specs=[rows(D)] * 4, compiler_params=_params("parallel"),
    )(gpre, *ys, b_gate, w_o, x, g_post, g_pre)


def _ffn_in_swiglu(h2, w_ffn_in):
    tokens = h2.shape[0]
    tm, tn = min(512, tokens), _tile(D_FF, 1536)
    nj = D_FF // tn

    def body(h_ref, wa_ref, wb_ref, a_ref, b_ref, f_ref):
        hv = h_ref[...]
        a = jnp.dot(hv, wa_ref[...], preferred_element_type=F32)
        b = jnp.dot(hv, wb_ref[...], preferred_element_type=F32)
        a_ref[...] = a.astype(BF16)
        b_ref[...] = b.astype(BF16)
        f_ref[...] = (a * jax.nn.sigmoid(a) * b).astype(BF16)

    out = jax.ShapeDtypeStruct((tokens, D_FF), BF16)
    o_spec = pl.BlockSpec((tm, tn), lambda j, i: (i, j))
    return pl.pallas_call(
        body, name="ffn_in_swiglu", grid=(nj, tokens // tm), out_shape=(out, out, out),
        in_specs=[pl.BlockSpec((tm, D), lambda j, i: (i, 0)), pl.BlockSpec((D, tn), lambda j, i: (0, j)),
                  pl.BlockSpec((D, tn), lambda j, i: (0, j + nj))],
        out_specs=(o_spec, o_spec, o_spec), compiler_params=_params("parallel", "parallel"),
    )(h2, w_ffn_in, w_ffn_in)


def _ffn_out_loss(f, w_ffn_out, x1, tgt, g_post):
    def epilogue(fo_v, x1_ref, t_ref, g_ref, dy_ref, dfo_ref, dg_ref, loss_ref):
        r = _rstd(fo_v)
        n = fo_v * r
        err = (x1_ref[...] + n * g_ref[...]) - t_ref[...]
        loss_ref[...] += _colsum(err * err)
        dy = err * (1.0 / D)
        dy_ref[...] = dy
        dg_ref[...] += _colsum(dy * n)
        dfo_ref[...] = _norm_bwd(dy * g_ref[...], n, r).astype(BF16)

    return _matmul_rows([(f, w_ffn_out)], "nn", "ffn_out_loss", epilogue, [x1, tgt], [g_post],
                        [(D, F32), (D, BF16)], (D, D), tk_cap=D_FF)


def _d_ffn_swiglu_bwd(dfo, w_ffn_out, gu_a, gu_b):
    def epilogue(d, a_ref, b_ref, dgu_ref):
        a = a_ref[...].astype(F32)
        b = b_ref[...].astype(F32)
        s = jax.nn.sigmoid(a)
        dgu_ref[:, :D_FF] = (d * b * (s * (1.0 + a * (1.0 - s)))).astype(BF16)
        dgu_ref[:, D_FF:] = (d * (a * s)).astype(BF16)

    return _matmul_rows([(dfo, w_ffn_out)], "nt", "d_ffn_swiglu_bwd", epilogue, [gu_a, gu_b], [],
                        [(2 * D_FF, BF16)], tm=256)[0]


def _d_h2_norm_bwd(dgu, w_ffn_in, x1, dy, mix, g_pre, g_post):
    def epilogue(dh, x1_ref, dy_ref, mix_ref, g3_ref, g2_ref, dx1_ref, dmix_ref, dg3_ref, dg2_ref):
        x1v = x1_ref[...]
        r3 = _rstd(x1v)
        n3 = x1v * r3
        dg3_ref[...] += _colsum(dh * n3)
        dx1 = dy_ref[...] + _norm_bwd(dh * g3_ref[...], n3, r3)
        dx1_ref[...] = dx1
        mv = mix_ref[...]
        r2 = _rstd(mv)
        n2 = mv * r2
        dg2_ref[...] += _colsum(dx1 * n2)
        dmix_ref[...] = _norm_bwd(dx1 * g2_ref[...], n2, r2).astype(BF16)

    return _matmul_rows([(dgu, w_ffn_in)], "nt", "d_h2_norm_bwd", epilogue, [x1, dy, mix], [g_pre, g_post],
                        [(D, F32), (D, BF16)], (D, D), tk_cap=D_FF)


def _d_merged_gate_bwd(dmix, w_o, gpre, ys, b_gate):
    def epilogue(dm, gp_ref, ya_ref, yb_ref, yc_ref, b_ref, dya_ref, dyb_ref, dyc_ref, dgp_ref, db_ref):
        for k, (y_ref, dy_ref) in enumerate(((ya_ref, dya_ref), (yb_ref, dyb_ref), (yc_ref, dyc_ref))):
            cols = slice(k * D, (k + 1) * D)
            gate = jax.nn.sigmoid(gp_ref[:, cols].astype(F32) + b_ref[:, cols])
            dy_ref[...] = (dm * gate).astype(BF16)
            dgp = (dm * y_ref[...].astype(F32)) * (gate * (1.0 - gate))
            dgp_ref[:, cols] = dgp.astype(BF16)
            db_ref[:, cols] += _colsum(dgp)

    return _matmul_rows([(dmix, w_o)], "nt", "d_merged_gate_bwd", epilogue, [gpre, *ys], [b_gate],
                        [(D, BF16), (D, BF16), (D, BF16), (3 * D, BF16)], (3 * D,))


def _d_h_norm_bwd(pairs, x, dx1, g_pre, ride):
    def epilogue(dh, x_ref, dx1_ref, g_ref, dx_ref, dg_ref):
        xv = x_ref[...]
        r = _rstd(xv)
        n = xv * r
        dg_ref[...] += _colsum(dh * n)
        dx_ref[...] = dx1_ref[...] + _norm_bwd(dh * g_ref[...], n, r)

    return _matmul_rows(pairs, "nt", "d_h_norm_bwd", epilogue, [x, dx1], [g_pre], [(D, F32)], (D,), ride=ride,
                        tm=1024, tk_cap=768, epi_rows=256)


def _gain_grad(dmn, mem):
    def body(d_ref, m_ref, dg_ref):
        mv = m_ref[...]
        dg_ref[...] += _colsum(d_ref[...] * (mv * _rstd(mv)))

    return _rowwise(body, "mem_gain_grad", mem.shape[0], 256, [dmn, mem], [], [], (D,))[0]


def _head_rowsum(v, head0):
    return (jnp.sum(jnp.where(head0, v, 0.0), axis=1, keepdims=True),
            jnp.sum(jnp.where(head0, 0.0, v), axis=1, keepdims=True))


KT = 256
SB_SCALE = 0.125


def _make_suffix(inclusive):
    row, col = lax.broadcasted_iota(jnp.int32, (KT, KT), 0), lax.broadcasted_iota(jnp.int32, (KT, KT), 1)
    tri = (row >= col if inclusive else row > col).astype(BF16)
    tri2 = jnp.concatenate([tri, tri], axis=0)

    def suffix(x):
        hi = x.astype(BF16)
        lo = (x - hi.astype(F32)).astype(BF16)
        return jnp.dot(jnp.concatenate([hi, lo], axis=1), tri2, preferred_element_type=F32)

    return suffix


def _sb_scores(qh, k, mask, suffix_incl, later):
    z = lax.dot_general(qh, k, NT, preferred_element_type=F32)
    zc = jnp.minimum(z, 60.0)
    sp = jnp.log(1.0 + jnp.exp(zc))
    if mask is not None:
        sp = jnp.where(mask, sp, 0.0)
    a = jnp.exp((zc - suffix_incl(sp)) - later)
    if mask is not None:
        a = jnp.where(mask, a, 0.0)
    return zc, sp, a


QB = KT


def _sb_tiles(i, tile, init):
    st = lax.cond(i > 0, lambda s: tile(i - 1, tile(i, s, True), False), lambda s: tile(i, s, True), init)
    rest = jnp.maximum(i - 1, 0)
    st = lax.fori_loop(0, lax.shift_right_logical(rest, 1),
                       lambda t, s: tile(rest - 2 - 2 * t, tile(rest - 1 - 2 * t, s, False), False), st)
    return lax.cond((rest & 1) == 1, lambda s: tile(0, s, False), lambda s: s, st)


def _sb_consts():
    head0 = lax.broadcasted_iota(jnp.int32, (QB, BLK), 1) < 64
    row = lax.broadcasted_iota(jnp.int32, (2 * QB, KT), 0) & (QB - 1)
    return head0, row > lax.broadcasted_iota(jnp.int32, (2 * QB, KT), 1)


def _stack_heads(v, head0):
    zero = jnp.zeros_like(v)
    return jnp.concatenate([jnp.where(head0, v, zero), jnp.where(head0, zero, v)], axis=0)


def _unstack_heads(v, head0):
    n = v.shape[0] // 2
    return jnp.where(head0, v[:n], v[n:])


def _sb_fwd(proj_a, seq, ride):
    bsz = proj_a.shape[0]
    n_ride = len(ride)

    def body(x_ref, *rest):
        ride_refs, (o_ref, o32_ref), rest = rest[:n_ride], rest[n_ride:n_ride + 2], rest[n_ride + 2:]
        gathered_refs, acc_ref, sems = rest[:n_ride], rest[n_ride], rest[n_ride + 1:]
        finish_ride = _riding_exchange(ride_refs, gathered_refs, sems, gather=True)
        head0, diag_mask = _sb_consts()
        suffix_incl = _make_suffix(True)

        def qblock(i, carry):
            r0 = pl.multiple_of(i * QB, QB)
            qs = _stack_heads(x_ref[pl.ds(r0, QB), 0:128] * jnp.asarray(SB_SCALE, BF16), head0)

            def tile(jt, later, masked):
                c0 = pl.multiple_of(jt * KT, KT)
                k = x_ref[pl.ds(c0, KT), 128:256]
                v = x_ref[pl.ds(c0, KT), 256:384]
                _, sp, a = _sb_scores(qs, k, diag_mask if masked else None, suffix_incl, later)
                pv = jnp.dot(a.astype(BF16), v, preferred_element_type=F32)
                if masked:
                    acc_ref[...] = pv
                else:
                    acc_ref[...] += pv
                return later + jnp.sum(sp, axis=1, keepdims=True)

            _sb_tiles(i, tile, jnp.zeros((2 * QB, 1), F32))
            o = _unstack_heads(acc_ref[...], head0)
            o32_ref[pl.ds(r0, QB), :] = o
            o_ref[pl.ds(r0, QB), :] = o.astype(BF16)
            return carry

        lax.fori_loop(0, seq // QB, qblock, 0)
        finish_ride()

    out_spec = pl.BlockSpec((None, seq, BLK), lambda b, hp: (b, 0, hp))
    return pl.pallas_call(
        body, name="sb_attn_fwd", grid=(bsz, 4),
        out_shape=[jax.ShapeDtypeStruct((bsz, seq, 512), BF16), jax.ShapeDtypeStruct((bsz, seq, 512), F32),
                   *_exchange_out(ride, True)],
        in_specs=[pl.BlockSpec((None, seq, 384), lambda b, hp: (b, 0, hp))] + [ANY] * n_ride,
        out_specs=[out_spec, out_spec] + [ANY] * n_ride,
        scratch_shapes=[pltpu.VMEM((2 * QB, BLK), F32), *_exchange_sems(n_ride)],
        compiler_params=_params("arbitrary", "arbitrary"),
    )(proj_a, *ride)


def _sb_bwd(proj_a, d_o, o_a, seq, ride):
    bsz = proj_a.shape[0]
    n_ride = len(ride)

    def body(x_ref, do_ref, o_ref, *rest):
        ride_refs, d_ref, rest = rest[:n_ride], rest[n_ride], rest[n_ride + 1:]
        received_refs, (dq_acc, dk_acc, dv_acc), sems = rest[:n_ride], rest[n_ride:n_ride + 3], rest[n_ride + 3:]
        finish_ride = _riding_exchange(ride_refs, received_refs, sems, gather=False)
        head0, diag_mask = _sb_consts()
        suffix_incl, suffix_excl = _make_suffix(True), _make_suffix(False)
        dk_acc[...] = jnp.zeros_like(dk_acc)
        dv_acc[...] = jnp.zeros_like(dv_acc)

        def qblock(i, carry):
            r0 = pl.multiple_of(i * QB, QB)
            qs = _stack_heads(x_ref[pl.ds(r0, QB), 0:128] * jnp.asarray(SB_SCALE, BF16), head0)
            do = do_ref[pl.ds(r0, QB), :]
            dos = _stack_heads(do, head0)
            dsum = jnp.concatenate(_head_rowsum(do.astype(F32) * o_ref[pl.ds(r0, QB), :], head0), axis=0)

            def tile(jt, st, masked):
                later, rest_g = st
                c0 = pl.multiple_of(jt * KT, KT)
                k = x_ref[pl.ds(c0, KT), 128:256]
                v = x_ref[pl.ds(c0, KT), 256:384]
                zc, sp, a = _sb_scores(qs, k, diag_mask if masked else None, suffix_incl, later)
                a16 = a.astype(BF16)
                g = a16.astype(F32) * lax.dot_general(dos, v, NT, preferred_element_type=F32)
                dz = g - jnp.exp(zc - sp) * (rest_g - suffix_excl(g))
                if masked:
                    dz = jnp.where(diag_mask, dz, 0.0)
                dz = dz.astype(BF16)
                dq = jnp.dot(dz, k, preferred_element_type=F32)
                if masked:
                    dq_acc[...] = dq
                else:
                    dq_acc[...] += dq
                dk_acc[pl.ds(c0, KT), :] += lax.dot_general(dz, qs, TN, preferred_element_type=F32)
                dv_acc[pl.ds(c0, KT), :] += lax.dot_general(a16, dos, TN, preferred_element_type=F32)
                return later + jnp.sum(sp, axis=1, keepdims=True), rest_g - jnp.sum(g, axis=1, keepdims=True)

            _sb_tiles(i, tile, (jnp.zeros((2 * QB, 1), F32), dsum))
            d_ref[pl.ds(r0, QB), 0:128] = (_unstack_heads(dq_acc[...], head0) * SB_SCALE).astype(BF16)
            return carry

        lax.fori_loop(0, seq // QB, qblock, 0)
        d_ref[:, 128:256] = dk_acc[...].astype(BF16)
        d_ref[:, 256:384] = dv_acc[...].astype(BF16)
        finish_ride()

    return pl.pallas_call(
        body, name="sb_attn_bwd", grid=(bsz, 4),
        out_shape=[jax.ShapeDtypeStruct((bsz, seq, WA), BF16), *_exchange_out(ride, False)],
        in_specs=[pl.BlockSpec((None, seq, 384), lambda b, hp: (b, 0, hp)),
                  pl.BlockSpec((None, seq, BLK), lambda b, hp: (b, 0, hp)),
                  pl.BlockSpec((None, seq, BLK), lambda b, hp: (b, 0, hp))] + [ANY] * n_ride,
        out_specs=[pl.BlockSpec((None, seq, 384), lambda b, hp: (b, 0, hp))] + [ANY] * n_ride,
        scratch_shapes=[pltpu.VMEM((2 * QB, BLK), F32), pltpu.VMEM((seq, BLK), F32), pltpu.VMEM((seq, BLK), F32),
                        *_exchange_sems(n_ride)],
        compiler_params=_params("arbitrary", "arbitrary"),
    )(proj_a, d_o, o_a, *ride)


def _rope_tables(seq):
    inv_freq = ROPE_THETA ** (-jnp.arange(32, dtype=F32) * 2.0 / 64)
    ang = jnp.arange(seq).astype(F32)[:, None] * inv_freq[None, :]
    cos, sin = jnp.cos(ang), jnp.sin(ang)
    return jnp.tile(cos, (1, 4)), jnp.concatenate([-sin, sin, -sin, sin], axis=1)


def _make_rope(n_rows):
    lane = lax.broadcasted_iota(jnp.int32, (n_rows, BLK), 1)
    first = (lane & 63) < 32

    def rope(x, cos, sin):
        partner = jnp.where(first, pltpu.roll(x, 96, 1), pltpu.roll(x, 32, 1))
        return x * cos + partner * sin

    return rope


DIL_UNROLL = 8


def _dil_consts():
    head0 = lax.broadcasted_iota(jnp.int32, (BLK, BLK), 1) < 64
    row = lax.broadcasted_iota(jnp.int32, (2 * BLK, 2 * BLK), 0) & (BLK - 1)
    col = lax.broadcasted_iota(jnp.int32, (2 * BLK, 2 * BLK), 1)
    valid_prev = jnp.logical_and(col < BLK, col >= row)
    valid_cur = jnp.logical_and(col >= BLK, row >= col - BLK)
    return head0, valid_prev, valid_cur


def _dil_blocks(dil, seq, block):
    nq = seq // dil // BLK

    def rows(r, i):
        if dil == 1:
            return pl.ds(pl.multiple_of(i * BLK, BLK), BLK)
        return pl.ds(r + (dil * BLK) * i, BLK, stride=dil)

    def step(t, carry):
        for u in range(DIL_UNROLL):
            n = t * DIL_UNROLL + u
            r, i = lax.div(n, nq), lax.rem(n, nq)
            block(rows(r, i), rows(r, jnp.maximum(i - 1, 0)), i)
        return carry

    lax.fori_loop(0, seq // BLK // DIL_UNROLL, step, 0)


def _dil_scores(qf, kf, vf, cur, prev, i, consts):
    head0, valid_prev, valid_cur = consts
    qs = _stack_heads(qf[cur, :].astype(BF16), head0)
    kcat = jnp.concatenate([kf[prev, :], kf[cur, :]], axis=0).astype(BF16)
    vcat = jnp.concatenate([vf[prev, :], vf[cur, :]], axis=0).astype(BF16)
    valid = jnp.logical_or(valid_cur, jnp.logical_and(valid_prev, i > 0))
    s = lax.dot_general(qs, kcat, NT, preferred_element_type=F32) * 0.125
    return qs, kcat, vcat, s, valid


def _head_cols(v):
    return jnp.concatenate([v[:, 0:1], v[:, 64:65]], axis=0)


def _dil_load_qkv(x_ref, c, rope, cos, sin, qf, kf, vf):
    qf[...] = rope(x_ref[:, c:c + 128].astype(F32), cos, sin).astype(BF16).astype(F32)
    kf[...] = rope(x_ref[:, c + 128:c + 256].astype(F32), cos, sin).astype(BF16).astype(F32)
    vf[...] = x_ref[:, c + 256:c + 384].astype(F32)


def _dil_fwd(proj_b, cos_t, sin_t, seq):
    bsz = proj_b.shape[0]

    def body(x_ref, cos_ref, sin_ref, ob_ref, lse_ref, qf, kf, vf, og, lg):
        consts = _dil_consts()
        head0 = consts[0]
        rope = _make_rope(seq)
        cos, sin = cos_ref[...], sin_ref[...]
        for g, dil in enumerate(DIL_GROUPS):
            _dil_load_qkv(x_ref, 384 * g, rope, cos, sin, qf, kf, vf)

            def block(cur, prev, i, g=g):
                _, _, vcat, s, valid = _dil_scores(qf, kf, vf, cur, prev, i, consts)
                s = jnp.where(valid, s, NEG_INF)
                m = jnp.max(s, axis=1, keepdims=True)
                p = jnp.exp(s - m)
                den = jnp.sum(p, axis=1, keepdims=True)
                o = jnp.dot(p.astype(BF16), vcat, preferred_element_type=F32) / den
                og[g, cur, :] = _unstack_heads(o, head0)
                lg[g, cur, :] = _unstack_heads(jnp.broadcast_to(m + jnp.log(den), (2 * BLK, BLK)), head0)

            _dil_blocks(dil, seq, block)
        ls = [lg[0], lg[1], lg[2]]
        m = jnp.maximum(jnp.maximum(ls[0], ls[1]), ls[2])
        ws = [jnp.exp(l - m) for l in ls]
        den = (ws[0] + ws[1]) + ws[2]
        ob_ref[...] = (((ws[0] * og[0] + ws[1] * og[1]) + ws[2] * og[2]) / den).astype(BF16)
        lse_ref[...] = m + jnp.log(den)

    tab_spec = pl.BlockSpec((seq, BLK), lambda b, hp: (0, 0))
    out_spec = pl.BlockSpec((None, seq, BLK), lambda b, hp: (b, 0, hp))
    slab = pltpu.VMEM((seq, BLK), F32)
    return pl.pallas_call(
        body, name="dil_attn_fwd", grid=(bsz, 2),
        out_shape=(jax.ShapeDtypeStruct((bsz, seq, 256), BF16), jax.ShapeDtypeStruct((bsz, seq, 256), F32)),
        in_specs=[pl.BlockSpec((None, seq, WB // 2), lambda b, hp: (b, 0, hp)), tab_spec, tab_spec],
        out_specs=(out_spec, out_spec),
        scratch_shapes=[slab, slab, slab, pltpu.VMEM((3, seq, BLK), F32), pltpu.VMEM((3, seq, BLK), F32)],
        compiler_params=_params("parallel", "parallel"),
    )(proj_b, cos_t, sin_t)


def _dil_bwd(proj_b, cos_t, sin_t, d_ob, o_b, lse, seq):
    bsz = proj_b.shape[0]

    def body(x_ref, cos_ref, sin_ref, do_ref, ob_ref, lse_ref, d_ref, qf, kf, vf, dof, dsf, dq_s, dk_acc, dv_acc):
        consts = _dil_consts()
        head0 = consts[0]
        rope = _make_rope(seq)
        cos, sin = cos_ref[...], sin_ref[...]
        do_all = do_ref[...].astype(F32)
        dof[...] = do_all
        head0_all = lax.broadcasted_iota(jnp.int32, (seq, BLK), 1) < 64
        d0, d1 = _head_rowsum(do_all * ob_ref[...].astype(F32), head0_all)
        dsf[...] = jnp.where(head0_all, d0, d1)
        for g, dil in enumerate(DIL_GROUPS):
            _dil_load_qkv(x_ref, 384 * g, rope, cos, sin, qf, kf, vf)
            dk_acc[...] = jnp.zeros_like(dk_acc)
            dv_acc[...] = jnp.zeros_like(dv_acc)

            def block(cur, prev, i):
                qs, kcat, vcat, s, valid = _dil_scores(qf, kf, vf, cur, prev, i, consts)
                dos = _stack_heads(dof[cur, :].astype(BF16), head0)
                p = jnp.where(valid, jnp.exp(s - _head_cols(lse_ref[cur, :])), 0.0)
                dp = lax.dot_general(dos, vcat, NT, preferred_element_type=F32)
                ds = ((p * (dp - _head_cols(dsf[cur, :]))) * 0.125).astype(BF16)
                dq_s[cur, :] = _unstack_heads(jnp.dot(ds, kcat, preferred_element_type=F32), head0)
                dk = lax.dot_general(ds, qs, TN, preferred_element_type=F32)
                dv = lax.dot_general(p.astype(BF16), dos, TN, preferred_element_type=F32)
                dk_acc[prev, :] += dk[:BLK]
                dk_acc[cur, :] += dk[BLK:]
                dv_acc[prev, :] += dv[:BLK]
                dv_acc[cur, :] += dv[BLK:]

            _dil_blocks(dil, seq, block)
            c = 384 * g
            d_ref[:, c:c + 128] = rope(dq_s[...], cos, -sin).astype(BF16)
            d_ref[:, c + 128:c + 256] = rope(dk_acc[...], cos, -sin).astype(BF16)
            d_ref[:, c + 256:c + 384] = dv_acc[...].astype(BF16)

    x_spec = pl.BlockSpec((None, seq, WB // 2), lambda b, hp: (b, 0, hp))
    tab_spec = pl.BlockSpec((seq, BLK), lambda b, hp: (0, 0))
    tok_spec = pl.BlockSpec((None, seq, BLK), lambda b, hp: (b, 0, hp))
    return pl.pallas_call(
        body, name="dil_attn_bwd", grid=(bsz, 2),
        out_shape=jax.ShapeDtypeStruct((bsz, seq, WB), BF16),
        in_specs=[x_spec, tab_spec, tab_spec, tok_spec, tok_spec, tok_spec], out_specs=x_spec,
        scratch_shapes=[pltpu.VMEM((seq, BLK), F32)] * 8,
        compiler_params=_params("parallel", "parallel"),
    )(proj_b, cos_t, sin_t, d_ob, o_b, lse)


MEM_SCALE = 128 ** -0.5
MEM_QB = 2048


def _mem_fwd(proj_a, kv, seq):
    bsz = proj_a.shape[0]

    def body(q_ref, k_ref, v_ref, o_ref):
        k, v = k_ref[...], v_ref[...]

        def qblock(i, carry):
            r0 = pl.multiple_of(i * MEM_QB, MEM_QB)
            s = lax.dot_general(q_ref[pl.ds(r0, MEM_QB), :], k, NT, preferred_element_type=F32) * MEM_SCALE
            p = jnp.exp(s - jnp.max(s, axis=1, keepdims=True))
            p = p / jnp.sum(p, axis=1, keepdims=True)
            o_ref[pl.ds(r0, MEM_QB), :] = jnp.dot(p.astype(BF16), v, preferred_element_type=F32).astype(BF16)
            return carry

        lax.fori_loop(0, seq // MEM_QB, qblock, 0)

    return pl.pallas_call(
        body, name="mem_attn_fwd", grid=(bsz, 4),
        out_shape=jax.ShapeDtypeStruct((bsz, seq, 512), BF16),
        in_specs=[pl.BlockSpec((None, seq, BLK), lambda b, h: (b, 0, 12 + h)),
                  pl.BlockSpec((None, MEM_LEN, BLK), lambda b, h: (b, 0, h)),
                  pl.BlockSpec((None, MEM_LEN, BLK), lambda b, h: (b, 0, 4 + h))],
        out_specs=pl.BlockSpec((None, seq, BLK), lambda b, h: (b, 0, h)),
        compiler_params=_params("parallel", "parallel"),
    )(proj_a, kv, kv)


def _mem_bwd(proj_a, kv, d_o, d_proj_a, seq):
    bsz = proj_a.shape[0]

    def body(q_ref, k_ref, v_ref, do_ref, _, dq_ref, dk_ref, dv_ref):
        k, v = k_ref[...], v_ref[...]

        def qblock(i, carry):
            dk, dv = carry
            r0 = pl.multiple_of(i * MEM_QB, MEM_QB)
            q, do = q_ref[pl.ds(r0, MEM_QB), :], do_ref[pl.ds(r0, MEM_QB), :]
            s = lax.dot_general(q, k, NT, preferred_element_type=F32) * MEM_SCALE
            p = jnp.exp(s - jnp.max(s, axis=1, keepdims=True))
            p = p / jnp.sum(p, axis=1, keepdims=True)
            dp = lax.dot_general(do, v, NT, preferred_element_type=F32)
            ds = ((p * (dp - jnp.sum(p * dp, axis=1, keepdims=True))) * MEM_SCALE).astype(BF16)
            dq_ref[pl.ds(r0, MEM_QB), :] = jnp.dot(ds, k, preferred_element_type=F32).astype(BF16)
            dk = dk + lax.dot_general(ds, q, TN, preferred_element_type=F32)
            dv = dv + lax.dot_general(p.astype(BF16), do, TN, preferred_element_type=F32)
            return dk, dv

        zero = jnp.zeros((MEM_LEN, BLK), F32)
        dk, dv = lax.fori_loop(0, seq // MEM_QB, qblock, (zero, zero))
        dk_ref[...] = dk.astype(BF16)
        dv_ref[...] = dv.astype(BF16)

    kv_spec = pl.BlockSpec((None, MEM_LEN, BLK), lambda b, h: (b, 0, h))
    return pl.pallas_call(
        body, name="mem_attn_bwd", grid=(bsz, 4),
        out_shape=(jax.ShapeDtypeStruct((bsz, seq, WA), BF16), jax.ShapeDtypeStruct((bsz, MEM_LEN, 512), BF16),
                   jax.ShapeDtypeStruct((bsz, MEM_LEN, 512), BF16)),
        in_specs=[pl.BlockSpec((None, seq, BLK), lambda b, h: (b, 0, 12 + h)), kv_spec,
                  pl.BlockSpec((None, MEM_LEN, BLK), lambda b, h: (b, 0, 4 + h)),
                  pl.BlockSpec((None, seq, BLK), lambda b, h: (b, 0, h)), ANY],
        out_specs=(pl.BlockSpec((None, seq, BLK), lambda b, h: (b, 0, 12 + h)), kv_spec, kv_spec),
        input_output_aliases={4: 0},
        compiler_params=_params("parallel", "parallel"),
    )(proj_a, kv, kv, d_o, d_proj_a)


def _mesh_pos():
    return lax.axis_index("x"), lax.axis_index("y"), lax.axis_index("c")


def _all_gather(shard, name):
    m_per, n = shard.shape

    def body(x_ref, out_ref, send_sems, recv_sems, local_sem):
        x, y, c = _mesh_pos()
        me, sibling = (x, y, c), (x, y, 1 - c)
        chips = [(1 - x, y), (x, 1 - y), (1 - x, 1 - y)]

        def rows(px, py, pc):
            return out_ref.at[pl.ds((4 * px + 2 * py + pc) * m_per, m_per), :]

        def copy(k, block, to, src=None):
            return pltpu.make_async_remote_copy(
                src_ref=rows(*block) if src is None else src, dst_ref=rows(*block),
                send_sem=send_sems.at[k], recv_sem=recv_sems.at[k], device_id=to, device_id_type=MESH)

        mine = pltpu.make_async_copy(x_ref, rows(*me), local_sem)
        mine.start()
        first = [copy(0, me, sibling, src=x_ref)]
        first += [copy(1 + j, me, (*chip, c), src=x_ref) for j, chip in enumerate(chips)]
        for cp in first:
            cp.start()
        passed = [copy(4 + j, (*chip, c), sibling) for j, chip in enumerate(chips)]
        for j, chip in enumerate(chips):
            copy(1 + j, (*chip, c), me).wait_recv()
            passed[j].start()
        copy(0, sibling, me).wait_recv()
        for j, chip in enumerate(chips):
            copy(4 + j, (*chip, 1 - c), me).wait_recv()
        for cp in first + passed:
            cp.wait_send()
        mine.wait()

    return pl.pallas_call(
        body, name=name, out_shape=jax.ShapeDtypeStruct((N_DEV * m_per, n), shard.dtype),
        in_specs=[ANY], out_specs=ANY,
        scratch_shapes=[pltpu.SemaphoreType.DMA((7,)), pltpu.SemaphoreType.DMA((7,)), pltpu.SemaphoreType.DMA(())],
    )(shard)


def _exchange_sems(n_arrays):
    return [pltpu.SemaphoreType.DMA((7 * n_arrays,)), pltpu.SemaphoreType.DMA((7 * n_arrays,)),
            pltpu.SemaphoreType.DMA((n_arrays,))]


def _exchange_out(srcs, gather):
    return [jax.ShapeDtypeStruct((N_DEV, *s.shape[-2:]), s.dtype) for s in srcs]


def _direct_exchange(src_refs, dst_refs, send_sems, recv_sems, local_sems, gather):
    x, y, c = _mesh_pos()
    me = 4 * x + 2 * y + c
    owns, sends, recvs = [], [], []
    for a, (src, dst) in enumerate(zip(src_refs, dst_refs)):
        owns.append(pltpu.make_async_copy(src if gather else src.at[me], dst.at[me], local_sems.at[a]))
        for j in range(1, N_DEV):
            px = 1 - x if j & 4 else x
            py = 1 - y if j & 2 else y
            pc = 1 - c if j & 1 else c
            peer = 4 * px + 2 * py + pc
            sems = dict(send_sem=send_sems.at[7 * a + j - 1], recv_sem=recv_sems.at[7 * a + j - 1],
                        device_id=(px, py, pc), device_id_type=MESH)
            sends.append(pltpu.make_async_remote_copy(
                src_ref=src if gather else src.at[peer], dst_ref=dst.at[me], **sems))
            recvs.append(pltpu.make_async_remote_copy(
                src_ref=src if gather else src.at[me], dst_ref=dst.at[peer], **sems))

    def start():
        for cp in owns + sends:
            cp.start()

    def wait():
        for cp in recvs:
            cp.wait_recv()
        for cp in sends:
            cp.wait_send()
        for cp in owns:
            cp.wait()

    return start, wait


def _riding_exchange(src_refs, dst_refs, sems, gather):
    start, wait = _direct_exchange(src_refs, dst_refs, *sems, gather)
    ids = [pl.program_id(a) for a in range(2)]
    last = [pl.num_programs(a) - 1 for a in range(2)]
    pl.when(jnp.logical_and(ids[0] == 0, ids[1] == 0))(start)
    return lambda: pl.when(jnp.logical_and(ids[0] == last[0], ids[1] == last[1]))(wait)


def _exchange(srcs, gather, name):
    n = len(srcs)

    def body(*refs):
        start, wait = _direct_exchange(refs[:n], refs[n:2 * n], *refs[2 * n:], gather=gather)
        start()
        wait()

    return pl.pallas_call(
        body, name=name, out_shape=_exchange_out(srcs, gather),
        in_specs=[ANY] * n, out_specs=[ANY] * n, scratch_shapes=_exchange_sems(n),
    )(*srcs)


def _adamw(w, g, m, v):
    m = ADAM_B1 * m + (1.0 - ADAM_B1) * g
    v = ADAM_B2 * v + (1.0 - ADAM_B2) * (g * g)
    m_hat = m / (1.0 - ADAM_B1 ** ADAM_STEP)
    v_hat = v / (1.0 - ADAM_B2 ** ADAM_STEP)
    return -ADAM_LR * (m_hat / (jnp.sqrt(v_hat) + ADAM_EPS) + ADAM_WD * w), m, v


def _reduce_adamw(recv, w, m, v, name):
    _, k, n = w.shape
    tr = max(t for t in range(16, 257, 16) if k % t == 0)

    def body(r_ref, w_ref, m_ref, v_ref, g_out, d_out, m_out, v_out):
        g = r_ref[0].astype(F32)
        for s in range(1, N_DEV):
            g = g + r_ref[s].astype(F32)
        g_out[...] = g
        d_out[...], m_out[...], v_out[...] = _adamw(w_ref[...], g, m_ref[...], v_ref[...])

    spec = pl.BlockSpec((None, tr, n), lambda i: (0, i, 0))
    return pl.pallas_call(
        body, name=name, grid=(k // tr,),
        out_shape=[jax.ShapeDtypeStruct((1, k, n), F32)] * 4,
        in_specs=[pl.BlockSpec((N_DEV, tr, n), lambda i: (0, i, 0)), spec, spec, spec],
        out_specs=[spec] * 4, compiler_params=_params("arbitrary"),
    )(recv, w, m, v)


def _small_adamw(gathered, w, m, v):
    def body(g_ref, w_ref, m_ref, v_ref, g_out, d_out, m_out, v_out, loss_out):
        tot = g_ref[0]
        for s in range(1, N_DEV):
            tot = tot + g_ref[s]
        g = tot[0:8]
        g_out[...] = g
        d_out[...], m_out[...], v_out[...] = _adamw(w_ref[...], g, m_ref[...], v_ref[...])
        loss_out[...] = jnp.broadcast_to((0.5 / D) * jnp.sum(tot[8:9], axis=1, keepdims=True), (8, BLK))

    out = [jax.ShapeDtypeStruct((8, D), F32)] * 4 + [jax.ShapeDtypeStruct((8, BLK), F32)]
    return pl.pallas_call(body, name="small_adamw", out_shape=out, compiler_params=_params())(gathered, w, m, v)


def _pick_chunks(w, chunks):
    return jnp.concatenate([w[:, BLK * c:BLK * (c + 1)] for c in chunks], axis=1)


def _whole_weight(gathered, i):
    _, k, n = gathered.shape
    if BY_ROWS[i]:
        return gathered.reshape(N_DEV * k, n)
    return gathered.transpose(1, 0, 2).reshape(k, N_DEV * n)


def _shard_parts(grad, i):
    if BY_ROWS[i]:
        return grad.reshape(N_DEV, grad.shape[0] // N_DEV, grad.shape[1])
    k, n8 = grad.shape
    return grad.reshape(k, N_DEV, n8 // N_DEV).transpose(1, 0, 2)


def kernel(x, mem, g_pre_mix, g_post_mix, g_pre_ffn, g_post_ffn, g_mem, w_in, w_mem_kv, w_br_sb, w_br_dil, w_br_mem, w_gate, b_gate, w_o, w_ffn_in, w_ffn_out, loss_target, m_g_pre_mix, m_g_post_mix, m_g_pre_ffn, m_g_post_ffn, m_g_mem, m_w_in, m_w_mem_kv, m_w_br_sb, m_w_br_dil, m_w_br_mem, m_w_gate, m_b_gate, m_w_o, m_w_ffn_in, m_w_ffn_out, v_g_pre_mix, v_g_post_mix, v_g_pre_ffn, v_g_post_ffn, v_g_mem, v_w_in, v_w_mem_kv, v_w_br_sb, v_w_br_dil, v_w_br_mem, v_w_gate, v_b_gate, v_w_o, v_w_ffn_in, v_w_ffn_out):
    bsz, seq, _ = x.shape
    tokens = bsz * seq
    xf, tgt, memf = x.reshape(tokens, D), loss_target.reshape(tokens, D), mem.reshape(bsz * MEM_LEN, D)
    big_w = [w_in, w_mem_kv, w_br_sb, w_br_dil, w_br_mem, w_gate, w_o, w_ffn_in, w_ffn_out]
    big_m = [m_w_in, m_w_mem_kv, m_w_br_sb, m_w_br_dil, m_w_br_mem, m_w_gate, m_w_o, m_w_ffn_in, m_w_ffn_out]
    big_v = [v_w_in, v_w_mem_kv, v_w_br_sb, v_w_br_dil, v_w_br_mem, v_w_gate, v_w_o, v_w_ffn_in, v_w_ffn_out]

    shards = [w[0].astype(BF16) for w in big_w]
    k_in, n_in = shards[0].shape
    fw_in = _whole_weight(_all_gather(shards[0], "weight_all_gather").reshape(N_DEV, k_in, n_in), 0)
    w_a, w_b = _pick_chunks(fw_in, CHUNKS_A), _pick_chunks(fw_in, CHUNKS_B)

    h = _norm_fwd(xf, g_pre_mix, "pre_mix_norm")
    proj_a = _matmul(h, w_a, "nn", BF16, "proj_a").reshape(bsz, seq, WA)
    proj_b = _matmul(h, w_b, "nn", BF16, "proj_b").reshape(bsz, seq, WB)
    o_a, o_a32, *behind = _sb_fwd(proj_a, seq, [shards[i] for i in GATHER_BEHIND])
    fw_mem_kv, fw_br_sb, fw_br_dil, fw_br_mem, fw_gate, fw_o, fw_ffn_in, fw_ffn_out = (
        _whole_weight(g, i) for g, i in zip(behind, GATHER_BEHIND))
    gpre = _matmul(h, fw_gate, "nn", BF16, "gate_proj")
    cos_t, sin_t = _rope_tables(seq)
    o_b, lse_b = _dil_fwd(proj_b, cos_t, sin_t, seq)
    mn = _norm_fwd(memf, g_mem, "mem_norm")
    kv = _matmul(mn, fw_mem_kv, "nn", BF16, "mem_kv_proj").reshape(bsz, MEM_LEN, D)
    o_c = _mem_fwd(proj_a, kv, seq)
    o_a2, o_b2, o_c2 = o_a.reshape(tokens, 512), o_b.reshape(tokens, 256), o_c.reshape(tokens, 512)
    ys = [_matmul(o_a2, fw_br_sb, "nn", BF16, "branch_sb"), _matmul(o_b2, fw_br_dil, "nn", BF16, "branch_dil"),
          _matmul(o_c2, fw_br_mem, "nn", BF16, "branch_mem")]
    merged, mix, x1, h2 = _merge_out_proj_norm(gpre, ys, b_gate, fw_o, xf, g_post_mix, g_pre_ffn)
    gu_a, gu_b, f = _ffn_in_swiglu(h2, fw_ffn_in)
    dy, dfo, dg_post_ffn, loss_lanes = _ffn_out_loss(f, fw_ffn_out, x1, tgt, g_post_ffn)

    gw_ffn_out = _matmul(f, dfo, "tn", BF16, "gw_ffn_out")
    dgu = _d_ffn_swiglu_bwd(dfo, fw_ffn_out, gu_a, gu_b)
    gw_ffn_in = _matmul(h2, dgu, "tn", BF16, "gw_ffn_in")
    dx1, dmix, dg_pre_ffn, dg_post_mix = _d_h2_norm_bwd(dgu, fw_ffn_in, x1, dy, mix, g_pre_ffn, g_post_mix)
    gw_o = _matmul(merged, dmix, "tn", BF16, "gw_o")
    dya, dyb, dyc, dgpre, db_gate = _d_merged_gate_bwd(dmix, fw_o, gpre, ys, b_gate)
    d_oa = _matmul(dya, fw_br_sb, "nt", BF16, "d_o_sb").reshape(bsz, seq, 512)
    d_ob = _matmul(dyb, fw_br_dil, "nt", BF16, "d_o_dil").reshape(bsz, seq, 256)
    d_oc = _matmul(dyc, fw_br_mem, "nt", BF16, "d_o_mem").reshape(bsz, seq, 512)
    gw_br_sb = _matmul(o_a2, dya, "tn", BF16, "gw_br_sb")
    gw_br_dil = _matmul(o_b2, dyb, "tn", BF16, "gw_br_dil")
    gw_br_mem = _matmul(o_c2, dyc, "tn", BF16, "gw_br_mem")
    gw_gate = _matmul(h, dgpre, "tn", BF16, "gw_gate")
    grads = {2: gw_br_sb, 3: gw_br_dil, 4: gw_br_mem, 5: gw_gate, 6: gw_o, 7: gw_ffn_in, 8: gw_ffn_out}
    d_proj_a, *recv_behind = _sb_bwd(proj_a, d_oa, o_a32, seq, [_shard_parts(grads[i], i) for i in REDUCE_BEHIND])
    d_proj_a, dk_m, dv_m = _mem_bwd(proj_a, kv, d_oc, d_proj_a, seq)
    d_proj_b = _dil_bwd(proj_b, cos_t, sin_t, d_ob, o_b, lse_b, seq).reshape(tokens, WB)
    d_proj_a = d_proj_a.reshape(tokens, WA)
    gw_a = _matmul(h, d_proj_a, "tn", BF16, "gw_in_a")
    gw_b = _matmul(h, d_proj_b, "tn", BF16, "gw_in_b")
    dkv = jnp.concatenate([dk_m, dv_m], axis=-1).reshape(bsz * MEM_LEN, D)
    gw_mem_kv = _matmul(mn, dkv, "tn", BF16, "gw_mem_kv")
    dmn = _matmul(dkv, fw_mem_kv, "nt", F32, "d_mem_norm")
    dg_mem = _gain_grad(dmn, memf)
    gw_ab = jnp.concatenate([gw_a, gw_b], axis=1)
    where = {c: i for i, c in enumerate(CHUNKS_A + CHUNKS_B)}
    grads = {0: _pick_chunks(gw_ab, [where[c] for c in range(34)]), 1: gw_mem_kv}
    dx, dg_pre_mix, *recv_last = _d_h_norm_bwd(
        [(dgpre, fw_gate), (d_proj_a, w_a), (d_proj_b, w_b)], xf, dx1, g_pre_mix,
        [_shard_parts(grads[i], i) for i in REDUCE_LAST])

    received = dict(zip(REDUCE_BEHIND + REDUCE_LAST, [*recv_behind, *recv_last]))
    adam = [_reduce_adamw(received[i], big_w[i], big_m[i], big_v[i], "reduce_adamw_" + BIG_NAMES[i])
            for i in range(len(big_w))]
    big = [[a[k] for a in adam] for k in range(4)]

    small = jnp.concatenate([dg_pre_mix, dg_post_mix, dg_pre_ffn, dg_post_ffn, dg_mem, db_gate.reshape(3, D),
                             loss_lanes, jnp.zeros((7, D), F32)], axis=0)
    small_all, = _exchange([small], True, "small_all_gather")

    def small_pack(gs, b):
        return jnp.concatenate([*gs, b.reshape(3, D)], axis=0)

    sm = _small_adamw(
        small_all, small_pack([g_pre_mix, g_post_mix, g_pre_ffn, g_post_ffn, g_mem], b_gate),
        small_pack([m_g_pre_mix, m_g_post_mix, m_g_pre_ffn, m_g_post_ffn, m_g_mem], m_b_gate),
        small_pack([v_g_pre_mix, v_g_post_mix, v_g_pre_ffn, v_g_post_ffn, v_g_mem], v_b_gate))
    loss = sm[4][0, 0]

    def leaves(k):
        t, bw = sm[k], big[k]
        return [t[0:1], t[1:2], t[2:3], t[3:4], t[4:5], *bw[0:6], t[5:8].reshape(1, 3 * D), *bw[6:9]]

    return (loss, dx.reshape(bsz, seq, D), *leaves(0), *leaves(1), *leaves(2), *leaves(3))
```

```python
import functools

import jax
import jax.numpy as jnp
from jax import lax
from jax.experimental import pallas as pl
from jax.experimental.pallas import tpu as pltpu

F32 = jnp.float32
BF16 = jnp.bfloat16
D = 1024
BLK = 128
MEM_LEN = 256
D_FF = 2816
NORM_EPS = 1e-6
NEG_INF = -1e30
ROPE_THETA = 10000.0
ADAM_LR, ADAM_B1, ADAM_B2, ADAM_EPS, ADAM_WD, ADAM_STEP = 0.001, 0.9, 0.999, 1e-08, 0.01, 10
N_DEV = 8
VMEM_LIMIT_BYTES = 56 * 1024 * 1024
MESH = pl.DeviceIdType.MESH
ANY = pl.BlockSpec(memory_space=pl.ANY)

NT = (((1,), (1,)), ((), ()))
TN = (((0,), (0,)), ((), ()))
NN = (((1,), (0,)), ((), ()))
_DIMS = {"nn": NN, "nt": NT, "tn": TN}

BIG_NAMES = ("w_in", "w_mem_kv", "w_br_sb", "w_br_dil", "w_br_mem", "w_gate", "w_o", "w_ffn_in", "w_ffn_out")
BY_ROWS = (False, True, False, False, False, False, True, False, True)
GATHER_FIRST = (0,)
GATHER_BEHIND = (1, 2, 3, 4, 5, 6, 7, 8)
REDUCE_BEHIND = (2, 3, 4, 5, 6, 7, 8)
REDUCE_LAST = (0, 1)

CHUNKS_A = tuple(c for hp in range(4) for c in (hp, 4 + hp, 8 + hp)) + (30, 31, 32, 33)
CHUNKS_B = tuple(c for hp in range(2) for g in range(3) for c in (12 + 6 * g + hp, 14 + 6 * g + hp, 16 + 6 * g + hp))
WA, WB = 128 * len(CHUNKS_A), 128 * len(CHUNKS_B)
DIL_GROUPS = (1, 4, 16)


def _params(*sem):
    return pltpu.CompilerParams(dimension_semantics=sem or None, vmem_limit_bytes=VMEM_LIMIT_BYTES)


def _tile(n, cap):
    if n <= 128:
        return n
    assert n % 128 == 0, n
    best = 128
    for t in range(128, min(n, cap) + 1, 128):
        if n % t == 0:
            best = t
    return best


def _k_steps(k, nk, step):
    if nk == 1:
        step(True, True)
        return
    pl.when(k == 0)(functools.partial(step, True, False))
    if nk > 2:
        pl.when(jnp.logical_and(k > 0, k < nk - 1))(functools.partial(step, False, False))
    pl.when(k == nk - 1)(functools.partial(step, False, True))


def _matmul(a, b, mode, out_dtype, name, tm_cap=1536, tn_cap=1536, tk_cap=1536):
    if mode == "tn":
        (K, M), N = a.shape, b.shape[1]
    elif mode == "nt":
        (M, K), N = a.shape, b.shape[0]
    else:
        (M, K), N = a.shape, b.shape[1]
    if mode == "tn":
        tk_cap = 2 * tk_cap
    tm, tn, tk = _tile(M, tm_cap), _tile(N, tn_cap), _tile(K, tk_cap)
    nm, nn, nk = M // tm, N // tn, K // tk
    dims = _DIMS[mode]

    def body(a_ref, b_ref, o_ref, *acc):
        def step(first, last):
            d = lax.dot_general(a_ref[...], b_ref[...], dims, preferred_element_type=F32)
            if not first:
                d = d + acc[0][...]
            if last:
                o_ref[...] = d.astype(o_ref.dtype)
            else:
                acc[0][...] = d

        _k_steps(pl.program_id(2), nk, step)

    n_outer = nk == 1 and (a.size * nn + b.size) < (a.size + b.size * nm)
    if n_outer:
        grid, ij = (nn, nm, nk), (lambda g0, g1: (g1, g0))
    else:
        grid, ij = (nm, nn, nk), (lambda g0, g1: (g0, g1))
    if mode == "tn":
        a_spec = pl.BlockSpec((tk, tm), lambda g0, g1, k: (k, ij(g0, g1)[0]))
    else:
        a_spec = pl.BlockSpec((tm, tk), lambda g0, g1, k: (ij(g0, g1)[0], k))
    if mode == "nt":
        b_spec = pl.BlockSpec((tn, tk), lambda g0, g1, k: (ij(g0, g1)[1], k))
    else:
        b_spec = pl.BlockSpec((tk, tn), lambda g0, g1, k: (k, ij(g0, g1)[1]))
    return pl.pallas_call(
        body, name=name, grid=grid,
        out_shape=jax.ShapeDtypeStruct((M, N), out_dtype),
        in_specs=[a_spec, b_spec],
        out_specs=pl.BlockSpec((tm, tn), lambda g0, g1, k: ij(g0, g1)),
        scratch_shapes=[pltpu.VMEM((tm, tn), F32)] if nk > 1 else [],
        compiler_params=_params("parallel", "parallel", "arbitrary"),
    )(a, b)


def _rowwise(body, name, rows, tr, row_ins, vec_ins, row_outs, acc_outs=()):
    tr = min(tr, rows)
    assert rows % tr == 0
    in_specs, args = [], []
    for r in row_ins:
        arr, w, cb = r if isinstance(r, tuple) else (r, r.shape[1], 0)
        in_specs.append(pl.BlockSpec((tr, w), functools.partial(lambda i, cb: (i, cb), cb=cb)))
        args.append(arr)
    for v in vec_ins:
        in_specs.append(pl.BlockSpec(v.shape, lambda i: (0, 0)))
        args.append(v)
    out_shape = [jax.ShapeDtypeStruct((rows, w), dt) for w, dt in row_outs]
    out_shape += [jax.ShapeDtypeStruct((1, w), F32) for w in acc_outs]
    out_specs = [pl.BlockSpec((tr, w), lambda i: (i, 0)) for w, _ in row_outs]
    out_specs += [pl.BlockSpec((1, w), lambda i: (0, 0)) for w in acc_outs]
    n_acc = len(acc_outs)

    def wrapped(*refs):
        if n_acc:
            @pl.when(pl.program_id(0) == 0)
            def _():
                for r in refs[len(refs) - n_acc:]:
                    r[...] = jnp.zeros_like(r)
        body(*refs)

    return pl.pallas_call(
        wrapped, name=name, grid=(rows // tr,), out_shape=out_shape, in_specs=in_specs, out_specs=out_specs,
        compiler_params=_params("arbitrary"),
    )(*args)


def _rstd(x):
    return lax.rsqrt(jnp.mean(x * x, axis=-1, keepdims=True) + NORM_EPS)


def _norm_bwd(u, n, r):
    return r * (u - n * jnp.mean(u * n, axis=-1, keepdims=True))


def _colsum(v):
    return jnp.sum(v, axis=0, keepdims=True)


def _norm_fwd(x, g, name):
    def body(x_ref, g_ref, h_ref):
        xv = x_ref[...]
        h_ref[...] = ((xv * _rstd(xv)) * g_ref[...]).astype(BF16)

    return _rowwise(body, name, x.shape[0], 512, [x], [g], [(D, BF16)])[0]


def _matmul_rows(pairs, mode, name, epilogue, row_ins=(), vec_ins=(), row_outs=(), acc_outs=(), ride=None,
                 tm=512, tk_cap=1536, epi_rows=None):
    M = pairs[0][0].shape[0]
    N = pairs[0][1].shape[1] if mode == "nn" else pairs[0][1].shape[0]
    tm = min(tm, M)
    tks = [_tile(a.shape[1], tk_cap) for a, _ in pairs]
    nks = [a.shape[1] // tk for (a, _), tk in zip(pairs, tks)]
    offs = [sum(nks[:p]) for p in range(len(pairs))]
    nm, nk = M // tm, sum(nks)
    dims = _DIMS[mode]
    n_ab, n_extra, n_out = 2 * len(pairs), len(row_ins) + len(vec_ins), len(row_outs) + len(acc_outs)
    n_ride = 0 if ride is None else len(ride)

    def body(*refs):
        ab, extra, rest = refs[:n_ab], refs[n_ab:n_ab + n_extra], refs[n_ab + n_extra:]
        ride_refs, outs, rest = rest[:n_ride], rest[n_ride:n_ride + n_out], rest[n_ride + n_out:]
        received_refs, rest = rest[:n_ride], rest[n_ride:]
        if n_ride:
            finish_ride = _riding_exchange(ride_refs, received_refs, rest[len(rest) - 3:], gather=False)
        i, k = pl.program_id(0), pl.program_id(1)
        if acc_outs:
            @pl.when(jnp.logical_and(i == 0, k == 0))
            def _():
                for r in outs[len(row_outs):]:
                    r[...] = jnp.zeros_like(r)

        def step(p, first, last):
            d = lax.dot_general(ab[2 * p][...], ab[2 * p + 1][...], dims, preferred_element_type=F32)
            if not first:
                d = d + rest[0][...]
            if not last:
                rest[0][...] = d
            elif epi_rows is None:
                epilogue(d, *extra, *outs)
            else:
                rest[0][...] = d
                for c in range(tm // epi_rows):
                    rows = pl.ds(c * epi_rows, epi_rows)
                    sliced = [r.at[rows] for r in extra[:len(row_ins)]] + list(extra[len(row_ins):])
                    sliced += [r.at[rows] for r in outs[:len(row_outs)]] + list(outs[len(row_outs):])
                    epilogue(rest[0][rows, :], *sliced)

        last_p = len(pairs) - 1
        if nk == 1:
            step(0, True, True)
        else:
            pl.when(k == 0)(functools.partial(step, 0, True, False))
            for p in range(len(pairs)):
                lo, hi = max(offs[p], 1), min(offs[p] + nks[p], nk - 1)
                if hi > lo:
                    pl.when(jnp.logical_and(k >= lo, k < hi))(functools.partial(step, p, False, False))
            pl.when(k == nk - 1)(functools.partial(step, last_p, False, True))
        if n_ride:
            finish_ride()

    in_specs, args = [], []
    for p, ((a, b), tk) in enumerate(zip(pairs, tks)):
        step = functools.partial(lambda k, p: jnp.clip(k - offs[p], 0, nks[p] - 1), p=p)
        in_specs.append(pl.BlockSpec((tm, tk), functools.partial(lambda i, k, step: (i, step(k)), step=step)))
        if mode == "nn":
            in_specs.append(pl.BlockSpec((tk, N), functools.partial(lambda i, k, step: (step(k), 0), step=step)))
        else:
            in_specs.append(pl.BlockSpec((N, tk), functools.partial(lambda i, k, step: (0, step(k)), step=step)))
        args += [a, b]
    in_specs += [pl.BlockSpec((tm, r.shape[1]), lambda i, k: (i, 0)) for r in row_ins]
    in_specs += [pl.BlockSpec(v.shape, lambda i, k: (0, 0)) for v in vec_ins]
    in_specs += [ANY] * n_ride
    out_shape = [jax.ShapeDtypeStruct((M, w), dt) for w, dt in row_outs]
    out_shape += [jax.ShapeDtypeStruct((1, w), F32) for w in acc_outs]
    out_specs = [pl.BlockSpec((tm, w), lambda i, k: (i, 0)) for w, _ in row_outs]
    out_specs += [pl.BlockSpec((1, w), lambda i, k: (0, 0)) for w in acc_outs]
    scratch = [pltpu.VMEM((tm, N), F32)] if nk > 1 else []
    if n_ride:
        out_shape += _exchange_out(ride, False)
        out_specs += [ANY] * n_ride
        scratch += _exchange_sems(n_ride)
    return pl.pallas_call(
        body, name=name, grid=(nm, nk), out_shape=out_shape, in_specs=in_specs, out_specs=out_specs,
        scratch_shapes=scratch, compiler_params=_params("arbitrary", "arbitrary"),
    )(*args, *row_ins, *vec_ins, *(ride or []))


def _merge_out_proj_norm(gpre, ys, b_gate, w_o, x, g_post, g_pre):
    tokens = x.shape[0]
    tm = min(512, tokens)

    def body(gp_ref, ya_ref, yb_ref, yc_ref, b_ref, w_ref, x_ref, g2_ref, g3_ref, m_ref, mix_ref, x1_ref, h2_ref):
        acc = None
        for k, y_ref in enumerate((ya_ref, yb_ref, yc_ref)):
            cols = slice(k * D, (k + 1) * D)
            gate = jax.nn.sigmoid(gp_ref[:, cols].astype(F32) + b_ref[:, cols])
            term = gate * y_ref[...].astype(F32)
            acc = term if acc is None else acc + term
        merged = acc.astype(BF16)
        m_ref[...] = merged
        mv = jnp.dot(merged, w_ref[...], preferred_element_type=F32)
        mix_ref[...] = mv
        x1 = x_ref[...] + (mv * _rstd(mv)) * g2_ref[...]
        x1_ref[...] = x1
        h2_ref[...] = ((x1 * _rstd(x1)) * g3_ref[...]).astype(BF16)

    def rows(w):
        return pl.BlockSpec((tm, w), lambda i: (i, 0))

    def whole(a):
        return pl.BlockSpec(a.shape, lambda i: (0, 0))

    return pl.pallas_call(
        body, name="merge_out_proj_norm", grid=(tokens // tm,),
        out_shape=[jax.ShapeDtypeStruct((tokens, D), dt) for dt in (BF16, F32, F32, BF16)],
        in_specs=[rows(3 * D), rows(D), rows(D), rows(D), whole(b_gate), whole(w_o), rows(D), whole(g_post),
                  whole(g_pre)],
        out_specs=[rows(D)] * 4, compiler_params=_params("parallel"),
    )(gpre, *ys, b_gate, w_o, x, g_post, g_pre)


def _ffn_in_swiglu(h2, w_ffn_in):
    tokens = h2.shape[0]
    tm, tn = min(512, tokens), _tile(D_FF, 1536)
    nj = D_FF // tn

    def body(h_ref, wa_ref, wb_ref, a_ref, b_ref, f_ref):
        hv = h_ref[...]
        a = jnp.dot(hv, wa_ref[...], preferred_element_type=F32)
        b = jnp.dot(hv, wb_ref[...], preferred_element_type=F32)
        a_ref[...] = a.astype(BF16)
        b_ref[...] = b.astype(BF16)
        f_ref[...] = (a * jax.nn.sigmoid(a) * b).astype(BF16)

    out = jax.ShapeDtypeStruct((tokens, D_FF), BF16)
    o_spec = pl.BlockSpec((tm, tn), lambda j, i: (i, j))
    return pl.pallas_call(
        body, name="ffn_in_swiglu", grid=(nj, tokens // tm), out_shape=(out, out, out),
        in_specs=[pl.BlockSpec((tm, D), lambda j, i: (i, 0)), pl.BlockSpec((D, tn), lambda j, i: (0, j)),
                  pl.BlockSpec((D, tn), lambda j, i: (0, j + nj))],
        out_specs=(o_spec, o_spec, o_spec), compiler_params=_params("parallel", "parallel"),
    )(h2, w_ffn_in, w_ffn_in)


def _ffn_out_loss(f, w_ffn_out, x1, tgt, g_post):
    def epilogue(fo_v, x1_ref, t_ref, g_ref, dy_ref, dfo_ref, dg_ref, loss_ref):
        r = _rstd(fo_v)
        n = fo_v * r
        err = (x1_ref[...] + n * g_ref[...]) - t_ref[...]
        loss_ref[...] += _colsum(err * err)
        dy = err * (1.0 / D)
        dy_ref[...] = dy
        dg_ref[...] += _colsum(dy * n)
        dfo_ref[...] = _norm_bwd(dy * g_ref[...], n, r).astype(BF16)

    return _matmul_rows([(f, w_ffn_out)], "nn", "ffn_out_loss", epilogue, [x1, tgt], [g_post],
                        [(D, F32), (D, BF16)], (D, D), tk_cap=D_FF)


def _d_ffn_swiglu_bwd(dfo, w_ffn_out, gu_a, gu_b):
    def epilogue(d, a_ref, b_ref, dgu_ref):
        a = a_ref[...].astype(F32)
        b = b_ref[...].astype(F32)
        s = jax.nn.sigmoid(a)
        dgu_ref[:, :D_FF] = (d * b * (s * (1.0 + a * (1.0 - s)))).astype(BF16)
        dgu_ref[:, D_FF:] = (d * (a * s)).astype(BF16)

    return _matmul_rows([(dfo, w_ffn_out)], "nt", "d_ffn_swiglu_bwd", epilogue, [gu_a, gu_b], [],
                        [(2 * D_FF, BF16)], tm=256)[0]


def _d_h2_norm_bwd(dgu, w_ffn_in, x1, dy, mix, g_pre, g_post):
    def epilogue(dh, x1_ref, dy_ref, mix_ref, g3_ref, g2_ref, dx1_ref, dmix_ref, dg3_ref, dg2_ref):
        x1v = x1_ref[...]
        r3 = _rstd(x1v)
        n3 = x1v * r3
        dg3_ref[...] += _colsum(dh * n3)
        dx1 = dy_ref[...] + _norm_bwd(dh * g3_ref[...], n3, r3)
        dx1_ref[...] = dx1
        mv = mix_ref[...]
        r2 = _rstd(mv)
        n2 = mv * r2
        dg2_ref[...] += _colsum(dx1 * n2)
        dmix_ref[...] = _norm_bwd(dx1 * g2_ref[...], n2, r2).astype(BF16)

    return _matmul_rows([(dgu, w_ffn_in)], "nt", "d_h2_norm_bwd", epilogue, [x1, dy, mix], [g_pre, g_post],
                        [(D, F32), (D, BF16)], (D, D), tk_cap=D_FF)


def _d_merged_gate_bwd(dmix, w_o, gpre, ys, b_gate):
    def epilogue(dm, gp_ref, ya_ref, yb_ref, yc_ref, b_ref, dya_ref, dyb_ref, dyc_ref, dgp_ref, db_ref):
        for k, (y_ref, dy_ref) in enumerate(((ya_ref, dya_ref), (yb_ref, dyb_ref), (yc_ref, dyc_ref))):
            cols = slice(k * D, (k + 1) * D)
            gate = jax.nn.sigmoid(gp_ref[:, cols].astype(F32) + b_ref[:, cols])
            dy_ref[...] = (dm * gate).astype(BF16)
            dgp = (dm * y_ref[...].astype(F32)) * (gate * (1.0 - gate))
            dgp_ref[:, cols] = dgp.astype(BF16)
            db_ref[:, cols] += _colsum(dgp)

    return _matmul_rows([(dmix, w_o)], "nt", "d_merged_gate_bwd", epilogue, [gpre, *ys], [b_gate],
                        [(D, BF16), (D, BF16), (D, BF16), (3 * D, BF16)], (3 * D,))


def _d_h_norm_bwd(pairs, x, dx1, g_pre, ride):
    def epilogue(dh, x_ref, dx1_ref, g_ref, dx_ref, dg_ref):
        xv = x_ref[...]
        r = _rstd(xv)
        n = xv * r
        dg_ref[...] += _colsum(dh * n)
        dx_ref[...] = dx1_ref[...] + _norm_bwd(dh * g_ref[...], n, r)

    return _matmul_rows(pairs, "nt", "d_h_norm_bwd", epilogue, [x, dx1], [g_pre], [(D, F32)], (D,), ride=ride,
                        tm=1024, tk_cap=1024, epi_rows=256)


def _gain_grad(dmn, mem):
    def body(d_ref, m_ref, dg_ref):
        mv = m_ref[...]
        dg_ref[...] += _colsum(d_ref[...] * (mv * _rstd(mv)))

    return _rowwise(body, "mem_gain_grad", mem.shape[0], 256, [dmn, mem], [], [], (D,))[0]


def _head_rowsum(v, head0):
    return (jnp.sum(jnp.where(head0, v, 0.0), axis=1, keepdims=True),
            jnp.sum(jnp.where(head0, 0.0, v), axis=1, keepdims=True))


KT = 256
SB_SCALE = 0.125


def _make_suffix(inclusive):
    row, col = lax.broadcasted_iota(jnp.int32, (KT, KT), 0), lax.broadcasted_iota(jnp.int32, (KT, KT), 1)
    tri = (row >= col if inclusive else row > col).astype(BF16)
    tri2 = jnp.concatenate([tri, tri], axis=0)

    def suffix(x):
        hi = x.astype(BF16)
        lo = (x - hi.astype(F32)).astype(BF16)
        return jnp.dot(jnp.concatenate([hi, lo], axis=1), tri2, preferred_element_type=F32)

    return suffix


def _sb_scores(qh, k, mask, suffix_incl, later):
    z = lax.dot_general(qh, k, NT, preferred_element_type=F32)
    zc = jnp.minimum(z, 60.0)
    sp = jnp.log(1.0 + jnp.exp(zc))
    if mask is not None:
        sp = jnp.where(mask, sp, 0.0)
    a = jnp.exp((zc - suffix_incl(sp)) - later)
    if mask is not None:
        a = jnp.where(mask, a, 0.0)
    return zc, sp, a


QB = KT


def _sb_tiles(i, tile, init):
    st = lax.cond(i > 0, lambda s: tile(i - 1, tile(i, s, True), False), lambda s: tile(i, s, True), init)
    rest = jnp.maximum(i - 1, 0)
    st = lax.fori_loop(0, lax.shift_right_logical(rest, 1),
                       lambda t, s: tile(rest - 2 - 2 * t, tile(rest - 1 - 2 * t, s, False), False), st)
    return lax.cond((rest & 1) == 1, lambda s: tile(0, s, False), lambda s: s, st)


def _sb_consts():
    head0 = lax.broadcasted_iota(jnp.int32, (QB, BLK), 1) < 64
    row = lax.broadcasted_iota(jnp.int32, (2 * QB, KT), 0) & (QB - 1)
    return head0, row > lax.broadcasted_iota(jnp.int32, (2 * QB, KT), 1)


def _stack_heads(v, head0):
    zero = jnp.zeros_like(v)
    return jnp.concatenate([jnp.where(head0, v, zero), jnp.where(head0, zero, v)], axis=0)


def _unstack_heads(v, head0):
    n = v.shape[0] // 2
    return jnp.where(head0, v[:n], v[n:])


def _sb_fwd(proj_a, seq, ride):
    bsz = proj_a.shape[0]
    n_ride = len(ride)

    def body(x_ref, *rest):
        ride_refs, (o_ref, o32_ref), rest = rest[:n_ride], rest[n_ride:n_ride + 2], rest[n_ride + 2:]
        gathered_refs, acc_ref, sems = rest[:n_ride], rest[n_ride], rest[n_ride + 1:]
        finish_ride = _riding_exchange(ride_refs, gathered_refs, sems, gather=True)
        head0, diag_mask = _sb_consts()
        suffix_incl = _make_suffix(True)

        def qblock(i, carry):
            r0 = pl.multiple_of(i * QB, QB)
            qs = _stack_heads(x_ref[pl.ds(r0, QB), 0:128] * jnp.asarray(SB_SCALE, BF16), head0)

            def tile(jt, later, masked):
                c0 = pl.multiple_of(jt * KT, KT)
                k = x_ref[pl.ds(c0, KT), 128:256]
                v = x_ref[pl.ds(c0, KT), 256:384]
                _, sp, a = _sb_scores(qs, k, diag_mask if masked else None, suffix_incl, later)
                pv = jnp.dot(a.astype(BF16), v, preferred_element_type=F32)
                if masked:
                    acc_ref[...] = pv
                else:
                    acc_ref[...] += pv
                return later + jnp.sum(sp, axis=1, keepdims=True)

            _sb_tiles(i, tile, jnp.zeros((2 * QB, 1), F32))
            o = _unstack_heads(acc_ref[...], head0)
            o32_ref[pl.ds(r0, QB), :] = o
            o_ref[pl.ds(r0, QB), :] = o.astype(BF16)
            return carry

        lax.fori_loop(0, seq // QB, qblock, 0)
        finish_ride()

    out_spec = pl.BlockSpec((None, seq, BLK), lambda b, hp: (b, 0, hp))
    return pl.pallas_call(
        body, name="sb_attn_fwd", grid=(bsz, 4),
        out_shape=[jax.ShapeDtypeStruct((bsz, seq, 512), BF16), jax.ShapeDtypeStruct((bsz, seq, 512), F32),
                   *_exchange_out(ride, True)],
        in_specs=[pl.BlockSpec((None, seq, 384), lambda b, hp: (b, 0, hp))] + [ANY] * n_ride,
        out_specs=[out_spec, out_spec] + [ANY] * n_ride,
        scratch_shapes=[pltpu.VMEM((2 * QB, BLK), F32), *_exchange_sems(n_ride)],
        compiler_params=_params("arbitrary", "arbitrary"),
    )(proj_a, *ride)


def _sb_bwd(proj_a, d_o, o_a, seq, ride):
    bsz = proj_a.shape[0]
    n_ride = len(ride)

    def body(x_ref, do_ref, o_ref, *rest):
        ride_refs, d_ref, rest = rest[:n_ride], rest[n_ride], rest[n_ride + 1:]
        received_refs, (dq_acc, dk_acc, dv_acc), sems = rest[:n_ride], rest[n_ride:n_ride + 3], rest[n_ride + 3:]
        finish_ride = _riding_exchange(ride_refs, received_refs, sems, gather=False)
        head0, diag_mask = _sb_consts()
        suffix_incl, suffix_excl = _make_suffix(True), _make_suffix(False)
        dk_acc[...] = jnp.zeros_like(dk_acc)
        dv_acc[...] = jnp.zeros_like(dv_acc)

        def qblock(i, carry):
            r0 = pl.multiple_of(i * QB, QB)
            qs = _stack_heads(x_ref[pl.ds(r0, QB), 0:128] * jnp.asarray(SB_SCALE, BF16), head0)
            do = do_ref[pl.ds(r0, QB), :]
            dos = _stack_heads(do, head0)
            dsum = jnp.concatenate(_head_rowsum(do.astype(F32) * o_ref[pl.ds(r0, QB), :], head0), axis=0)

            def tile(jt, st, masked):
                later, rest_g = st
                c0 = pl.multiple_of(jt * KT, KT)
                k = x_ref[pl.ds(c0, KT), 128:256]
                v = x_ref[pl.ds(c0, KT), 256:384]
                zc, sp, a = _sb_scores(qs, k, diag_mask if masked else None, suffix_incl, later)
                a16 = a.astype(BF16)
                g = a16.astype(F32) * lax.dot_general(dos, v, NT, preferred_element_type=F32)
                dz = g - jnp.exp(zc - sp) * (rest_g - suffix_excl(g))
                if masked:
                    dz = jnp.where(diag_mask, dz, 0.0)
                dz = dz.astype(BF16)
                dq = jnp.dot(dz, k, preferred_element_type=F32)
                if masked:
                    dq_acc[...] = dq
                else:
                    dq_acc[...] += dq
                dk_acc[pl.ds(c0, KT), :] += lax.dot_general(dz, qs, TN, preferred_element_type=F32)
                dv_acc[pl.ds(c0, KT), :] += lax.dot_general(a16, dos, TN, preferred_element_type=F32)
                return later + jnp.sum(sp, axis=1, keepdims=True), rest_g - jnp.sum(g, axis=1, keepdims=True)

            _sb_tiles(i, tile, (jnp.zeros((2 * QB, 1), F32), dsum))
            d_ref[pl.ds(r0, QB), 0:128] = (_unstack_heads(dq_acc[...], head0) * SB_SCALE).astype(BF16)
            return carry

        lax.fori_loop(0, seq // QB, qblock, 0)
        d_ref[:, 128:256] = dk_acc[...].astype(BF16)
        d_ref[:, 256:384] = dv_acc[...].astype(BF16)
        finish_ride()

    return pl.pallas_call(
        body, name="sb_attn_bwd", grid=(bsz, 4),
        out_shape=[jax.ShapeDtypeStruct((bsz, seq, WA), BF16), *_exchange_out(ride, False)],
        in_specs=[pl.BlockSpec((None, seq, 384), lambda b, hp: (b, 0, hp)),
                  pl.BlockSpec((None, seq, BLK), lambda b, hp: (b, 0, hp)),
                  pl.BlockSpec((None, seq, BLK), lambda b, hp: (b, 0, hp))] + [ANY] * n_ride,
        out_specs=[pl.BlockSpec((None, seq, 384), lambda b, hp: (b, 0, hp))] + [ANY] * n_ride,
        scratch_shapes=[pltpu.VMEM((2 * QB, BLK), F32), pltpu.VMEM((seq, BLK), F32), pltpu.VMEM((seq, BLK), F32),
                        *_exchange_sems(n_ride)],
        compiler_params=_params("arbitrary", "arbitrary"),
    )(proj_a, d_o, o_a, *ride)


def _rope_tables(seq):
    inv_freq = ROPE_THETA ** (-jnp.arange(32, dtype=F32) * 2.0 / 64)
    ang = jnp.arange(seq).astype(F32)[:, None] * inv_freq[None, :]
    cos, sin = jnp.cos(ang), jnp.sin(ang)
    return jnp.tile(cos, (1, 4)), jnp.concatenate([-sin, sin, -sin, sin], axis=1)


def _make_rope(n_rows):
    lane = lax.broadcasted_iota(jnp.int32, (n_rows, BLK), 1)
    first = (lane & 63) < 32

    def rope(x, cos, sin):
        partner = jnp.where(first, pltpu.roll(x, 96, 1), pltpu.roll(x, 32, 1))
        return x * cos + partner * sin

    return rope


DIL_UNROLL = 8


def _dil_consts():
    head0 = lax.broadcasted_iota(jnp.int32, (BLK, BLK), 1) < 64
    row = lax.broadcasted_iota(jnp.int32, (2 * BLK, 2 * BLK), 0) & (BLK - 1)
    col = lax.broadcasted_iota(jnp.int32, (2 * BLK, 2 * BLK), 1)
    valid_prev = jnp.logical_and(col < BLK, col >= row)
    valid_cur = jnp.logical_and(col >= BLK, row >= col - BLK)
    return head0, valid_prev, valid_cur


def _dil_blocks(dil, seq, block):
    nq = seq // dil // BLK

    def rows(r, i):
        if dil == 1:
            return pl.ds(pl.multiple_of(i * BLK, BLK), BLK)
        return pl.ds(r + (dil * BLK) * i, BLK, stride=dil)

    def step(t, carry):
        for u in range(DIL_UNROLL):
            n = t * DIL_UNROLL + u
            r, i = lax.div(n, nq), lax.rem(n, nq)
            block(rows(r, i), rows(r, jnp.maximum(i - 1, 0)), i)
        return carry

    lax.fori_loop(0, seq // BLK // DIL_UNROLL, step, 0)


def _dil_scores(qf, kf, vf, cur, prev, i, consts):
    head0, valid_prev, valid_cur = consts
    qs = _stack_heads(qf[cur, :].astype(BF16), head0)
    kcat = jnp.concatenate([kf[prev, :], kf[cur, :]], axis=0).astype(BF16)
    vcat = jnp.concatenate([vf[prev, :], vf[cur, :]], axis=0).astype(BF16)
    valid = jnp.logical_or(valid_cur, jnp.logical_and(valid_prev, i > 0))
    s = lax.dot_general(qs, kcat, NT, preferred_element_type=F32) * 0.125
    return qs, kcat, vcat, s, valid


def _head_cols(v):
    return jnp.concatenate([v[:, 0:1], v[:, 64:65]], axis=0)


def _dil_load_qkv(x_ref, c, rope, cos, sin, qf, kf, vf):
    qf[...] = rope(x_ref[:, c:c + 128].astype(F32), cos, sin).astype(BF16).astype(F32)
    kf[...] = rope(x_ref[:, c + 128:c + 256].astype(F32), cos, sin).astype(BF16).astype(F32)
    vf[...] = x_ref[:, c + 256:c + 384].astype(F32)


def _dil_fwd(proj_b, cos_t, sin_t, seq):
    bsz = proj_b.shape[0]

    def body(x_ref, cos_ref, sin_ref, ob_ref, lse_ref, qf, kf, vf, og, lg):
        consts = _dil_consts()
        head0 = consts[0]
        rope = _make_rope(seq)
        cos, sin = cos_ref[...], sin_ref[...]
        for g, dil in enumerate(DIL_GROUPS):
            _dil_load_qkv(x_ref, 384 * g, rope, cos, sin, qf, kf, vf)

            def block(cur, prev, i, g=g):
                _, _, vcat, s, valid = _dil_scores(qf, kf, vf, cur, prev, i, consts)
                s = jnp.where(valid, s, NEG_INF)
                m = jnp.max(s, axis=1, keepdims=True)
                p = jnp.exp(s - m)
                den = jnp.sum(p, axis=1, keepdims=True)
                o = jnp.dot(p.astype(BF16), vcat, preferred_element_type=F32) / den
                og[g, cur, :] = _unstack_heads(o, head0)
                lg[g, cur, :] = _unstack_heads(jnp.broadcast_to(m + jnp.log(den), (2 * BLK, BLK)), head0)

            _dil_blocks(dil, seq, block)
        ls = [lg[0], lg[1], lg[2]]
        m = jnp.maximum(jnp.maximum(ls[0], ls[1]), ls[2])
        ws = [jnp.exp(l - m) for l in ls]
        den = (ws[0] + ws[1]) + ws[2]
        ob_ref[...] = (((ws[0] * og[0] + ws[1] * og[1]) + ws[2] * og[2]) / den).astype(BF16)
        lse_ref[...] = m + jnp.log(den)

    tab_spec = pl.BlockSpec((seq, BLK), lambda b, hp: (0, 0))
    out_spec = pl.BlockSpec((None, seq, BLK), lambda b, hp: (b, 0, hp))
    slab = pltpu.VMEM((seq, BLK), F32)
    return pl.pallas_call(
        body, name="dil_attn_fwd", grid=(bsz, 2),
        out_shape=(jax.ShapeDtypeStruct((bsz, seq, 256), BF16), jax.ShapeDtypeStruct((bsz, seq, 256), F32)),
        in_specs=[pl.BlockSpec((None, seq, WB // 2), lambda b, hp: (b, 0, hp)), tab_spec, tab_spec],
        out_specs=(out_spec, out_spec),
        scratch_shapes=[slab, slab, slab, pltpu.VMEM((3, seq, BLK), F32), pltpu.VMEM((3, seq, BLK), F32)],
        compiler_params=_params("parallel", "parallel"),
    )(proj_b, cos_t, sin_t)


def _dil_bwd(proj_b, cos_t, sin_t, d_ob, o_b, lse, seq):
    bsz = proj_b.shape[0]

    def body(x_ref, cos_ref, sin_ref, do_ref, ob_ref, lse_ref, d_ref, qf, kf, vf, dof, dsf, dq_s, dk_acc, dv_acc):
        consts = _dil_consts()
        head0 = consts[0]
        rope = _make_rope(seq)
        cos, sin = cos_ref[...], sin_ref[...]
        do_all = do_ref[...].astype(F32)
        dof[...] = do_all
        head0_all = lax.broadcasted_iota(jnp.int32, (seq, BLK), 1) < 64
        d0, d1 = _head_rowsum(do_all * ob_ref[...].astype(F32), head0_all)
        dsf[...] = jnp.where(head0_all, d0, d1)
        for g, dil in enumerate(DIL_GROUPS):
            _dil_load_qkv(x_ref, 384 * g, rope, cos, sin, qf, kf, vf)
            dk_acc[...] = jnp.zeros_like(dk_acc)
            dv_acc[...] = jnp.zeros_like(dv_acc)

            def block(cur, prev, i):
                qs, kcat, vcat, s, valid = _dil_scores(qf, kf, vf, cur, prev, i, consts)
                dos = _stack_heads(dof[cur, :].astype(BF16), head0)
                p = jnp.where(valid, jnp.exp(s - _head_cols(lse_ref[cur, :])), 0.0)
                dp = lax.dot_general(dos, vcat, NT, preferred_element_type=F32)
                ds = ((p * (dp - _head_cols(dsf[cur, :]))) * 0.125).astype(BF16)
                dq_s[cur, :] = _unstack_heads(jnp.dot(ds, kcat, preferred_element_type=F32), head0)
                dk = lax.dot_general(ds, qs, TN, preferred_element_type=F32)
                dv = lax.dot_general(p.astype(BF16), dos, TN, preferred_element_type=F32)
                dk_acc[prev, :] += dk[:BLK]
                dk_acc[cur, :] += dk[BLK:]
                dv_acc[prev, :] += dv[:BLK]
                dv_acc[cur, :] += dv[BLK:]

            _dil_blocks(dil, seq, block)
            c = 384 * g
            d_ref[:, c:c + 128] = rope(dq_s[...], cos, -sin).astype(BF16)
            d_ref[:, c + 128:c + 256] = rope(dk_acc[...], cos, -sin).astype(BF16)
            d_ref[:, c + 256:c + 384] = dv_acc[...].astype(BF16)

    x_spec = pl.BlockSpec((None, seq, WB // 2), lambda b, hp: (b, 0, hp))
    tab_spec = pl.BlockSpec((seq, BLK), lambda b, hp: (0, 0))
    tok_spec = pl.BlockSpec((None, seq, BLK), lambda b, hp: (b, 0, hp))
    return pl.pallas_call(
        body, name="dil_attn_bwd", grid=(bsz, 2),
        out_shape=jax.ShapeDtypeStruct((bsz, seq, WB), BF16),
        in_specs=[x_spec, tab_spec, tab_spec, tok_spec, tok_spec, tok_spec], out_specs=x_spec,
        scratch_shapes=[pltpu.VMEM((seq, BLK), F32)] * 8,
        compiler_params=_params("parallel", "parallel"),
    )(proj_b, cos_t, sin_t, d_ob, o_b, lse)


MEM_SCALE = 128 ** -0.5
MEM_QB = 2048


def _mem_fwd(proj_a, kv, seq):
    bsz = proj_a.shape[0]

    def body(q_ref, k_ref, v_ref, o_ref):
        k, v = k_ref[...], v_ref[...]

        def qblock(i, carry):
            r0 = pl.multiple_of(i * MEM_QB, MEM_QB)
            s = lax.dot_general(q_ref[pl.ds(r0, MEM_QB), :], k, NT, preferred_element_type=F32) * MEM_SCALE
            p = jnp.exp(s - jnp.max(s, axis=1, keepdims=True))
            p = p / jnp.sum(p, axis=1, keepdims=True)
            o_ref[pl.ds(r0, MEM_QB), :] = jnp.dot(p.astype(BF16), v, preferred_element_type=F32).astype(BF16)
            return carry

        lax.fori_loop(0, seq // MEM_QB, qblock, 0)

    return pl.pallas_call(
        body, name="mem_attn_fwd", grid=(bsz, 4),
        out_shape=jax.ShapeDtypeStruct((bsz, seq, 512), BF16),
        in_specs=[pl.BlockSpec((None, seq, BLK), lambda b, h: (b, 0, 12 + h)),
                  pl.BlockSpec((None, MEM_LEN, BLK), lambda b, h: (b, 0, h)),
                  pl.BlockSpec((None, MEM_LEN, BLK), lambda b, h: (b, 0, 4 + h))],
        out_specs=pl.BlockSpec((None, seq, BLK), lambda b, h: (b, 0, h)),
        compiler_params=_params("parallel", "parallel"),
    )(proj_a, kv, kv)


def _mem_bwd(proj_a, kv, d_o, d_proj_a, seq):
    bsz = proj_a.shape[0]

    def body(q_ref, k_ref, v_ref, do_ref, _, dq_ref, dk_ref, dv_ref):
        k, v = k_ref[...], v_ref[...]

        def qblock(i, carry):
            dk, dv = carry
            r0 = pl.multiple_of(i * MEM_QB, MEM_QB)
            q, do = q_ref[pl.ds(r0, MEM_QB), :], do_ref[pl.ds(r0, MEM_QB), :]
            s = lax.dot_general(q, k, NT, preferred_element_type=F32) * MEM_SCALE
            p = jnp.exp(s - jnp.max(s, axis=1, keepdims=True))
            p = p / jnp.sum(p, axis=1, keepdims=True)
            dp = lax.dot_general(do, v, NT, preferred_element_type=F32)
            ds = ((p * (dp - jnp.sum(p * dp, axis=1, keepdims=True))) * MEM_SCALE).astype(BF16)
            dq_ref[pl.ds(r0, MEM_QB), :] = jnp.dot(ds, k, preferred_element_type=F32).astype(BF16)
            dk = dk + lax.dot_general(ds, q, TN, preferred_element_type=F32)
            dv = dv + lax.dot_general(p.astype(BF16), do, TN, preferred_element_type=F32)
            return dk, dv

        zero = jnp.zeros((MEM_LEN, BLK), F32)
        dk, dv = lax.fori_loop(0, seq // MEM_QB, qblock, (zero, zero))
        dk_ref[...] = dk.astype(BF16)
        dv_ref[...] = dv.astype(BF16)

    kv_spec = pl.BlockSpec((None, MEM_LEN, BLK), lambda b, h: (b, 0, h))
    return pl.pallas_call(
        body, name="mem_attn_bwd", grid=(bsz, 4),
        out_shape=(jax.ShapeDtypeStruct((bsz, seq, WA), BF16), jax.ShapeDtypeStruct((bsz, MEM_LEN, 512), BF16),
                   jax.ShapeDtypeStruct((bsz, MEM_LEN, 512), BF16)),
        in_specs=[pl.BlockSpec((None, seq, BLK), lambda b, h: (b, 0, 12 + h)), kv_spec,
                  pl.BlockSpec((None, MEM_LEN, BLK), lambda b, h: (b, 0, 4 + h)),
                  pl.BlockSpec((None, seq, BLK), lambda b, h: (b, 0, h)), ANY],
        out_specs=(pl.BlockSpec((None, seq, BLK), lambda b, h: (b, 0, 12 + h)), kv_spec, kv_spec),
        input_output_aliases={4: 0},
        compiler_params=_params("parallel", "parallel"),
    )(proj_a, kv, kv, d_o, d_proj_a)


def _mesh_pos():
    return lax.axis_index("x"), lax.axis_index("y"), lax.axis_index("c")


def _all_gather(shard, name):
    m_per, n = shard.shape

    def body(x_ref, out_ref, send_sems, recv_sems, local_sem):
        x, y, c = _mesh_pos()
        me, sibling = (x, y, c), (x, y, 1 - c)
        chips = [(1 - x, y), (x, 1 - y), (1 - x, 1 - y)]

        def rows(px, py, pc):
            return out_ref.at[pl.ds((4 * px + 2 * py + pc) * m_per, m_per), :]

        def copy(k, block, to, src=None):
            return pltpu.make_async_remote_copy(
                src_ref=rows(*block) if src is None else src, dst_ref=rows(*block),
                send_sem=send_sems.at[k], recv_sem=recv_sems.at[k], device_id=to, device_id_type=MESH)

        mine = pltpu.make_async_copy(x_ref, rows(*me), local_sem)
        mine.start()
        first = [copy(0, me, sibling, src=x_ref)]
        first += [copy(1 + j, me, (*chip, c), src=x_ref) for j, chip in enumerate(chips)]
        for cp in first:
            cp.start()
        passed = [copy(4 + j, (*chip, c), sibling) for j, chip in enumerate(chips)]
        for j, chip in enumerate(chips):
            copy(1 + j, (*chip, c), me).wait_recv()
            passed[j].start()
        copy(0, sibling, me).wait_recv()
        for j, chip in enumerate(chips):
            copy(4 + j, (*chip, 1 - c), me).wait_recv()
        for cp in first + passed:
            cp.wait_send()
        mine.wait()

    return pl.pallas_call(
        body, name=name, out_shape=jax.ShapeDtypeStruct((N_DEV * m_per, n), shard.dtype),
        in_specs=[ANY], out_specs=ANY,
        scratch_shapes=[pltpu.SemaphoreType.DMA((7,)), pltpu.SemaphoreType.DMA((7,)), pltpu.SemaphoreType.DMA(())],
    )(shard)


def _exchange_sems(n_arrays):
    return [pltpu.SemaphoreType.DMA((7 * n_arrays,)), pltpu.SemaphoreType.DMA((7 * n_arrays,)),
            pltpu.SemaphoreType.DMA((n_arrays,))]


def _exchange_out(srcs, gather):
    return [jax.ShapeDtypeStruct((N_DEV, *s.shape[-2:]), s.dtype) for s in srcs]


def _direct_exchange(src_refs, dst_refs, send_sems, recv_sems, local_sems, gather):
    x, y, c = _mesh_pos()
    me = 4 * x + 2 * y + c
    owns, sends, recvs = [], [], []
    for a, (src, dst) in enumerate(zip(src_refs, dst_refs)):
        owns.append(pltpu.make_async_copy(src if gather else src.at[me], dst.at[me], local_sems.at[a]))
        for j in range(1, N_DEV):
            px = 1 - x if j & 4 else x
            py = 1 - y if j & 2 else y
            pc = 1 - c if j & 1 else c
            peer = 4 * px + 2 * py + pc
            sems = dict(send_sem=send_sems.at[7 * a + j - 1], recv_sem=recv_sems.at[7 * a + j - 1],
                        device_id=(px, py, pc), device_id_type=MESH)
            sends.append(pltpu.make_async_remote_copy(
                src_ref=src if gather else src.at[peer], dst_ref=dst.at[me], **sems))
            recvs.append(pltpu.make_async_remote_copy(
                src_ref=src if gather else src.at[me], dst_ref=dst.at[peer], **sems))

    def start():
        for cp in owns + sends:
            cp.start()

    def wait():
        for cp in recvs:
            cp.wait_recv()
        for cp in sends:
            cp.wait_send()
        for cp in owns:
            cp.wait()

    return start, wait


def _riding_exchange(src_refs, dst_refs, sems, gather):
    start, wait = _direct_exchange(src_refs, dst_refs, *sems, gather)
    ids = [pl.program_id(a) for a in range(2)]
    last = [pl.num_programs(a) - 1 for a in range(2)]
    pl.when(jnp.logical_and(ids[0] == 0, ids[1] == 0))(start)
    return lambda: pl.when(jnp.logical_and(ids[0] == last[0], ids[1] == last[1]))(wait)


def _exchange(srcs, gather, name):
    n = len(srcs)

    def body(*refs):
        start, wait = _direct_exchange(refs[:n], refs[n:2 * n], *refs[2 * n:], gather=gather)
        start()
        wait()

    return pl.pallas_call(
        body, name=name, out_shape=_exchange_out(srcs, gather),
        in_specs=[ANY] * n, out_specs=[ANY] * n, scratch_shapes=_exchange_sems(n),
    )(*srcs)


def _adamw(w, g, m, v):
    m = ADAM_B1 * m + (1.0 - ADAM_B1) * g
    v = ADAM_B2 * v + (1.0 - ADAM_B2) * (g * g)
    m_hat = m / (1.0 - ADAM_B1 ** ADAM_STEP)
    v_hat = v / (1.0 - ADAM_B2 ** ADAM_STEP)
    return -ADAM_LR * (m_hat / (jnp.sqrt(v_hat) + ADAM_EPS) + ADAM_WD * w), m, v


def _reduce_adamw(recv, w, m, v, name):
    _, k, n = w.shape
    tr = max(t for t in range(16, 257, 16) if k % t == 0)

    def body(r_ref, w_ref, m_ref, v_ref, g_out, d_out, m_out, v_out):
        g = r_ref[0].astype(F32)
        for s in range(1, N_DEV):
            g = g + r_ref[s].astype(F32)
        g_out[...] = g
        d_out[...], m_out[...], v_out[...] = _adamw(w_ref[...], g, m_ref[...], v_ref[...])

    spec = pl.BlockSpec((None, tr, n), lambda i: (0, i, 0))
    return pl.pallas_call(
        body, name=name, grid=(k // tr,),
        out_shape=[jax.ShapeDtypeStruct((1, k, n), F32)] * 4,
        in_specs=[pl.BlockSpec((N_DEV, tr, n), lambda i: (0, i, 0)), spec, spec, spec],
        out_specs=[spec] * 4, compiler_params=_params("arbitrary"),
    )(recv, w, m, v)


def _small_adamw(gathered, w, m, v):
    def body(g_ref, w_ref, m_ref, v_ref, g_out, d_out, m_out, v_out, loss_out):
        tot = g_ref[0]
        for s in range(1, N_DEV):
            tot = tot + g_ref[s]
        g = tot[0:8]
        g_out[...] = g
        d_out[...], m_out[...], v_out[...] = _adamw(w_ref[...], g, m_ref[...], v_ref[...])
        loss_out[...] = jnp.broadcast_to((0.5 / D) * jnp.sum(tot[8:9], axis=1, keepdims=True), (8, BLK))

    out = [jax.ShapeDtypeStruct((8, D), F32)] * 4 + [jax.ShapeDtypeStruct((8, BLK), F32)]
    return pl.pallas_call(body, name="small_adamw", out_shape=out, compiler_params=_params())(gathered, w, m, v)


def _pick_chunks(w, chunks):
    return jnp.concatenate([w[:, BLK * c:BLK * (c + 1)] for c in chunks], axis=1)


def _whole_weight(gathered, i):
    _, k, n = gathered.shape
    if BY_ROWS[i]:
        return gathered.reshape(N_DEV * k, n)
    return gathered.transpose(1, 0, 2).reshape(k, N_DEV * n)


def _shard_parts(grad, i):
    if BY_ROWS[i]:
        return grad.reshape(N_DEV, grad.shape[0] // N_DEV, grad.shape[1])
    k, n8 = grad.shape
    return grad.reshape(k, N_DEV, n8 // N_DEV).transpose(1, 0, 2)


def kernel(x, mem, g_pre_mix, g_post_mix, g_pre_ffn, g_post_ffn, g_mem, w_in, w_mem_kv, w_br_sb, w_br_dil, w_br_mem, w_gate, b_gate, w_o, w_ffn_in, w_ffn_out, loss_target, m_g_pre_mix, m_g_post_mix, m_g_pre_ffn, m_g_post_ffn, m_g_mem, m_w_in, m_w_mem_kv, m_w_br_sb, m_w_br_dil, m_w_br_mem, m_w_gate, m_b_gate, m_w_o, m_w_ffn_in, m_w_ffn_out, v_g_pre_mix, v_g_post_mix, v_g_pre_ffn, v_g_post_ffn, v_g_mem, v_w_in, v_w_mem_kv, v_w_br_sb, v_w_br_dil, v_w_br_mem, v_w_gate, v_b_gate, v_w_o, v_w_ffn_in, v_w_ffn_out):
    bsz, seq, _ = x.shape
    tokens = bsz * seq
    xf, tgt, memf = x.reshape(tokens, D), loss_target.reshape(tokens, D), mem.reshape(bsz * MEM_LEN, D)
    big_w = [w_in, w_mem_kv, w_br_sb, w_br_dil, w_br_mem, w_gate, w_o, w_ffn_in, w_ffn_out]
    big_m = [m_w_in, m_w_mem_kv, m_w_br_sb, m_w_br_dil, m_w_br_mem, m_w_gate, m_w_o, m_w_ffn_in, m_w_ffn_out]
    big_v = [v_w_in, v_w_mem_kv, v_w_br_sb, v_w_br_dil, v_w_br_mem, v_w_gate, v_w_o, v_w_ffn_in, v_w_ffn_out]

    shards = [w[0].astype(BF16) for w in big_w]
    k_in, n_in = shards[0].shape
    fw_in = _whole_weight(_all_gather(shards[0], "weight_all_gather").reshape(N_DEV, k_in, n_in), 0)
    w_a, w_b = _pick_chunks(fw_in, CHUNKS_A), _pick_chunks(fw_in, CHUNKS_B)

    h = _norm_fwd(xf, g_pre_mix, "pre_mix_norm")
    proj_a = _matmul(h, w_a, "nn", BF16, "proj_a").reshape(bsz, seq, WA)
    proj_b = _matmul(h, w_b, "nn", BF16, "proj_b").reshape(bsz, seq, WB)
    o_a, o_a32, *behind = _sb_fwd(proj_a, seq, [shards[i] for i in GATHER_BEHIND])
    fw_mem_kv, fw_br_sb, fw_br_dil, fw_br_mem, fw_gate, fw_o, fw_ffn_in, fw_ffn_out = (
        _whole_weight(g, i) for g, i in zip(behind, GATHER_BEHIND))
    gpre = _matmul(h, fw_gate, "nn", BF16, "gate_proj")
    cos_t, sin_t = _rope_tables(seq)
    o_b, lse_b = _dil_fwd(proj_b, cos_t, sin_t, seq)
    mn = _norm_fwd(memf, g_mem, "mem_norm")
    kv = _matmul(mn, fw_mem_kv, "nn", BF16, "mem_kv_proj").reshape(bsz, MEM_LEN, D)
    o_c = _mem_fwd(proj_a, kv, seq)
    o_a2, o_b2, o_c2 = o_a.reshape(tokens, 512), o_b.reshape(tokens, 256), o_c.reshape(tokens, 512)
    ys = [_matmul(o_a2, fw_br_sb, "nn", BF16, "branch_sb"), _matmul(o_b2, fw_br_dil, "nn", BF16, "branch_dil"),
          _matmul(o_c2, fw_br_mem, "nn", BF16, "branch_mem")]
    merged, mix, x1, h2 = _merge_out_proj_norm(gpre, ys, b_gate, fw_o, xf, g_post_mix, g_pre_ffn)
    gu_a, gu_b, f = _ffn_in_swiglu(h2, fw_ffn_in)
    dy, dfo, dg_post_ffn, loss_lanes = _ffn_out_loss(f, fw_ffn_out, x1, tgt, g_post_ffn)

    gw_ffn_out = _matmul(f, dfo, "tn", BF16, "gw_ffn_out")
    dgu = _d_ffn_swiglu_bwd(dfo, fw_ffn_out, gu_a, gu_b)
    gw_ffn_in = _matmul(h2, dgu, "tn", BF16, "gw_ffn_in")
    dx1, dmix, dg_pre_ffn, dg_post_mix = _d_h2_norm_bwd(dgu, fw_ffn_in, x1, dy, mix, g_pre_ffn, g_post_mix)
    gw_o = _matmul(merged, dmix, "tn", BF16, "gw_o")
    dya, dyb, dyc, dgpre, db_gate = _d_merged_gate_bwd(dmix, fw_o, gpre, ys, b_gate)
    d_oa = _matmul(dya, fw_br_sb, "nt", BF16, "d_o_sb").reshape(bsz, seq, 512)
    d_ob = _matmul(dyb, fw_br_dil, "nt", BF16, "d_o_dil").reshape(bsz, seq, 256)
    d_oc = _matmul(dyc, fw_br_mem, "nt", BF16, "d_o_mem").reshape(bsz, seq, 512)
    gw_br_sb = _matmul(o_a2, dya, "tn", BF16, "gw_br_sb")
    gw_br_dil = _matmul(o_b2, dyb, "tn", BF16, "gw_br_dil")
    gw_br_mem = _matmul(o_c2, dyc, "tn", BF16, "gw_br_mem")
    gw_gate = _matmul(h, dgpre, "tn", BF16, "gw_gate")
    grads = {2: gw_br_sb, 3: gw_br_dil, 4: gw_br_mem, 5: gw_gate, 6: gw_o, 7: gw_ffn_in, 8: gw_ffn_out}
    d_proj_a, *recv_behind = _sb_bwd(proj_a, d_oa, o_a32, seq, [_shard_parts(grads[i], i) for i in REDUCE_BEHIND])
    d_proj_a, dk_m, dv_m = _mem_bwd(proj_a, kv, d_oc, d_proj_a, seq)
    d_proj_b = _dil_bwd(proj_b, cos_t, sin_t, d_ob, o_b, lse_b, seq).reshape(tokens, WB)
    d_proj_a = d_proj_a.reshape(tokens, WA)
    gw_a = _matmul(h, d_proj_a, "tn", BF16, "gw_in_a")
    gw_b = _matmul(h, d_proj_b, "tn", BF16, "gw_in_b")
    dkv = jnp.concatenate([dk_m, dv_m], axis=-1).reshape(bsz * MEM_LEN, D)
    gw_mem_kv = _matmul(mn, dkv, "tn", BF16, "gw_mem_kv")
    dmn = _matmul(dkv, fw_mem_kv, "nt", F32, "d_mem_norm")
    dg_mem = _gain_grad(dmn, memf)
    gw_ab = jnp.concatenate([gw_a, gw_b], axis=1)
    where = {c: i for i, c in enumerate(CHUNKS_A + CHUNKS_B)}
    grads = {0: _pick_chunks(gw_ab, [where[c] for c in range(34)]), 1: gw_mem_kv}
    dx, dg_pre_mix, *recv_last = _d_h_norm_bwd(
        [(dgpre, fw_gate), (d_proj_a, w_a), (d_proj_b, w_b)], xf, dx1, g_pre_mix,
        [_shard_parts(grads[i], i) for i in REDUCE_LAST])

    received = dict(zip(REDUCE_BEHIND + REDUCE_LAST, [*recv_behind, *recv_last]))
    adam = [_reduce_adamw(received[i], big_w[i], big_m[i], big_v[i], "reduce_adamw_" + BIG_NAMES[i])
            for i in range(len(big_w))]
    big = [[a[k] for a in adam] for k in range(4)]

    small = jnp.concatenate([dg_pre_mix, dg_post_mix, dg_pre_ffn, dg_post_ffn, dg_mem, db_gate.reshape(3, D),
                             loss_lanes, jnp.zeros((7, D), F32)], axis=0)
    small_all, = _exchange([small], True, "small_all_gather")

    def small_pack(gs, b):
        return jnp.concatenate([*gs, b.reshape(3, D)], axis=0)

    sm = _small_adamw(
        small_all, small_pack([g_pre_mix, g_post_mix, g_pre_ffn, g_post_ffn, g_mem], b_gate),
        small_pack([m_g_pre_mix, m_g_post_mix, m_g_pre_ffn, m_g_post_ffn, m_g_mem], m_b_gate),
        small_pack([v_g_pre_mix, v_g_post_mix, v_g_pre_ffn, v_g_post_ffn, v_g_mem], v_b_gate))
    loss = sm[4][0, 0]

    def leaves(k):
        t, bw = sm[k], big[k]
        return [t[0:1], t[1:2], t[2:3], t[3:4], t[4:5], *bw[0:6], t[5:8].reshape(1, 3 * D), *bw[6:9]]

    return (loss, dx.reshape(bsz, seq, D), *leaves(0), *leaves(1), *leaves(2), *leaves(3))
```

```python
import functools

import jax
import jax.numpy as jnp
from jax import lax
from jax.experimental import pallas as pl
from jax.experimental.pallas import tpu as pltpu

F32 = jnp.float32
BF16 = jnp.bfloat16
D = 1024
BLK = 128
MEM_LEN = 256
D_FF = 2816
NORM_EPS = 1e-6
NEG_INF = -1e30
ROPE_THETA = 10000.0
ADAM_LR, ADAM_B1, ADAM_B2, ADAM_EPS, ADAM_WD, ADAM_STEP = 0.001, 0.9, 0.999, 1e-08, 0.01, 10
N_DEV = 8
VMEM_LIMIT_BYTES = 56 * 1024 * 1024
MESH = pl.DeviceIdType.MESH
ANY = pl.BlockSpec(memory_space=pl.ANY)

NT = (((1,), (1,)), ((), ()))
TN = (((0,), (0,)), ((), ()))
NN = (((1,), (0,)), ((), ()))
_DIMS = {"nn": NN, "nt": NT, "tn": TN}

BIG_NAMES = ("w_in", "w_mem_kv", "w_br_sb", "w_br_dil", "w_br_mem", "w_gate", "w_o", "w_ffn_in", "w_ffn_out")
BY_ROWS = (False, True, False, False, False, False, True, False, True)
GATHER_FIRST = (0,)
GATHER_BEHIND = (1, 2, 3, 4, 5, 6, 7, 8)
REDUCE_BEHIND = (2, 3, 4, 5, 6, 7, 8)
REDUCE_LAST = (0, 1)

CHUNKS_A = tuple(c for hp in range(4) for c in (hp, 4 + hp, 8 + hp)) + (30, 31, 32, 33)
CHUNKS_B = tuple(c for hp in range(2) for g in range(3) for c in (12 + 6 * g + hp, 14 + 6 * g + hp, 16 + 6 * g + hp))
WA, WB = 128 * len(CHUNKS_A), 128 * len(CHUNKS_B)
DIL_GROUPS = (1, 4, 16)


def _params(*sem):
    return pltpu.CompilerParams(dimension_semantics=sem or None, vmem_limit_bytes=VMEM_LIMIT_BYTES)


def _tile(n, cap):
    if n <= 128:
        return n
    assert n % 128 == 0, n
    best = 128
    for t in range(128, min(n, cap) + 1, 128):
        if n % t == 0:
            best = t
    return best


def _k_steps(k, nk, step):
    if nk == 1:
        step(True, True)
        return
    pl.when(k == 0)(functools.partial(step, True, False))
    if nk > 2:
        pl.when(jnp.logical_and(k > 0, k < nk - 1))(functools.partial(step, False, False))
    pl.when(k == nk - 1)(functools.partial(step, False, True))


def _matmul(a, b, mode, out_dtype, name, tm_cap=1536, tn_cap=1536, tk_cap=1536):
    if mode == "tn":
        (K, M), N = a.shape, b.shape[1]
    elif mode == "nt":
        (M, K), N = a.shape, b.shape[0]
    else:
        (M, K), N = a.shape, b.shape[1]
    if mode == "tn":
        tk_cap = 2 * tk_cap
    tm, tn, tk = _tile(M, tm_cap), _tile(N, tn_cap), _tile(K, tk_cap)
    nm, nn, nk = M // tm, N // tn, K // tk
    dims = _DIMS[mode]

    def body(a_ref, b_ref, o_ref, *acc):
        def step(first, last):
            d = lax.dot_general(a_ref[...], b_ref[...], dims, preferred_element_type=F32)
            if not first:
                d = d + acc[0][...]
            if last:
                o_ref[...] = d.astype(o_ref.dtype)
            else:
                acc[0][...] = d

        _k_steps(pl.program_id(2), nk, step)

    n_outer = nk == 1 and (a.size * nn + b.size) < (a.size + b.size * nm)
    if n_outer:
        grid, ij = (nn, nm, nk), (lambda g0, g1: (g1, g0))
    else:
        grid, ij = (nm, nn, nk), (lambda g0, g1: (g0, g1))
    if mode == "tn":
        a_spec = pl.BlockSpec((tk, tm), lambda g0, g1, k: (k, ij(g0, g1)[0]))
    else:
        a_spec = pl.BlockSpec((tm, tk), lambda g0, g1, k: (ij(g0, g1)[0], k))
    if mode == "nt":
        b_spec = pl.BlockSpec((tn, tk), lambda g0, g1, k: (ij(g0, g1)[1], k))
    else:
        b_spec = pl.BlockSpec((tk, tn), lambda g0, g1, k: (k, ij(g0, g1)[1]))
    return pl.pallas_call(
        body, name=name, grid=grid,
        out_shape=jax.ShapeDtypeStruct((M, N), out_dtype),
        in_specs=[a_spec, b_spec],
        out_specs=pl.BlockSpec((tm, tn), lambda g0, g1, k: ij(g0, g1)),
        scratch_shapes=[pltpu.VMEM((tm, tn), F32)] if nk > 1 else [],
        compiler_params=_params("parallel", "parallel", "arbitrary"),
    )(a, b)


def _rowwise(body, name, rows, tr, row_ins, vec_ins, row_outs, acc_outs=()):
    tr = min(tr, rows)
    assert rows % tr == 0
    in_specs, args = [], []
    for r in row_ins:
        arr, w, cb = r if isinstance(r, tuple) else (r, r.shape[1], 0)
        in_specs.append(pl.BlockSpec((tr, w), functools.partial(lambda i, cb: (i, cb), cb=cb)))
        args.append(arr)
    for v in vec_ins:
        in_specs.append(pl.BlockSpec(v.shape, lambda i: (0, 0)))
        args.append(v)
    out_shape = [jax.ShapeDtypeStruct((rows, w), dt) for w, dt in row_outs]
    out_shape += [jax.ShapeDtypeStruct((1, w), F32) for w in acc_outs]
    out_specs = [pl.BlockSpec((tr, w), lambda i: (i, 0)) for w, _ in row_outs]
    out_specs += [pl.BlockSpec((1, w), lambda i: (0, 0)) for w in acc_outs]
    n_acc = len(acc_outs)

    def wrapped(*refs):
        if n_acc:
            @pl.when(pl.program_id(0) == 0)
            def _():
                for r in refs[len(refs) - n_acc:]:
                    r[...] = jnp.zeros_like(r)
        body(*refs)

    return pl.pallas_call(
        wrapped, name=name, grid=(rows // tr,), out_shape=out_shape, in_specs=in_specs, out_specs=out_specs,
        compiler_params=_params("arbitrary"),
    )(*args)


def _rstd(x):
    return lax.rsqrt(jnp.mean(x * x, axis=-1, keepdims=True) + NORM_EPS)


def _norm_bwd(u, n, r):
    return r * (u - n * jnp.mean(u * n, axis=-1, keepdims=True))


def _colsum(v):
    return jnp.sum(v, axis=0, keepdims=True)


def _norm_fwd(x, g, name):
    def body(x_ref, g_ref, h_ref):
        xv = x_ref[...]
        h_ref[...] = ((xv * _rstd(xv)) * g_ref[...]).astype(BF16)

    return _rowwise(body, name, x.shape[0], 512, [x], [g], [(D, BF16)])[0]


def _matmul_rows(pairs, mode, name, epilogue, row_ins=(), vec_ins=(), row_outs=(), acc_outs=(), ride=None,
                 tm=512, tk_cap=1536, epi_rows=None):
    M = pairs[0][0].shape[0]
    N = pairs[0][1].shape[1] if mode == "nn" else pairs[0][1].shape[0]
    tm = min(tm, M)
    tks = [_tile(a.shape[1], tk_cap) for a, _ in pairs]
    nks = [a.shape[1] // tk for (a, _), tk in zip(pairs, tks)]
    offs = [sum(nks[:p]) for p in range(len(pairs))]
    nm, nk = M // tm, sum(nks)
    dims = _DIMS[mode]
    n_ab, n_extra, n_out = 2 * len(pairs), len(row_ins) + len(vec_ins), len(row_outs) + len(acc_outs)
    n_ride = 0 if ride is None else len(ride)

    def body(*refs):
        ab, extra, rest = refs[:n_ab], refs[n_ab:n_ab + n_extra], refs[n_ab + n_extra:]
        ride_refs, outs, rest = rest[:n_ride], rest[n_ride:n_ride + n_out], rest[n_ride + n_out:]
        received_refs, rest = rest[:n_ride], rest[n_ride:]
        if n_ride:
            finish_ride = _riding_exchange(ride_refs, received_refs, rest[len(rest) - 3:], gather=False)
        i, k = pl.program_id(0), pl.program_id(1)
        if acc_outs:
            @pl.when(jnp.logical_and(i == 0, k == 0))
            def _():
                for r in outs[len(row_outs):]:
                    r[...] = jnp.zeros_like(r)

        def step(p, first, last):
            d = lax.dot_general(ab[2 * p][...], ab[2 * p + 1][...], dims, preferred_element_type=F32)
            if not first:
                d = d + rest[0][...]
            if not last:
                rest[0][...] = d
            elif epi_rows is None:
                epilogue(d, *extra, *outs)
            else:
                rest[0][...] = d
                for c in range(tm // epi_rows):
                    rows = pl.ds(c * epi_rows, epi_rows)
                    sliced = [r.at[rows] for r in extra[:len(row_ins)]] + list(extra[len(row_ins):])
                    sliced += [r.at[rows] for r in outs[:len(row_outs)]] + list(outs[len(row_outs):])
                    epilogue(rest[0][rows, :], *sliced)

        last_p = len(pairs) - 1
        if nk == 1:
            step(0, True, True)
        else:
            pl.when(k == 0)(functools.partial(step, 0, True, False))
            for p in range(len(pairs)):
                lo, hi = max(offs[p], 1), min(offs[p] + nks[p], nk - 1)
                if hi > lo:
                    pl.when(jnp.logical_and(k >= lo, k < hi))(functools.partial(step, p, False, False))
            pl.when(k == nk - 1)(functools.partial(step, last_p, False, True))
        if n_ride:
            finish_ride()

    in_specs, args = [], []
    for p, ((a, b), tk) in enumerate(zip(pairs, tks)):
        step = functools.partial(lambda k, p: jnp.clip(k - offs[p], 0, nks[p] - 1), p=p)
        in_specs.append(pl.BlockSpec((tm, tk), functools.partial(lambda i, k, step: (i, step(k)), step=step)))
        once = pl.Buffered(1) if nk == 1 else None
        if mode == "nn":
            in_specs.append(pl.BlockSpec((tk, N), functools.partial(lambda i, k, step: (step(k), 0), step=step),
                                         pipeline_mode=once))
        else:
            in_specs.append(pl.BlockSpec((N, tk), functools.partial(lambda i, k, step: (0, step(k)), step=step),
                                         pipeline_mode=once))
        args += [a, b]
    in_specs += [pl.BlockSpec((tm, r.shape[1]), lambda i, k: (i, 0)) for r in row_ins]
    in_specs += [pl.BlockSpec(v.shape, lambda i, k: (0, 0)) for v in vec_ins]
    in_specs += [ANY] * n_ride
    out_shape = [jax.ShapeDtypeStruct((M, w), dt) for w, dt in row_outs]
    out_shape += [jax.ShapeDtypeStruct((1, w), F32) for w in acc_outs]
    out_specs = [pl.BlockSpec((tm, w), lambda i, k: (i, 0)) for w, _ in row_outs]
    out_specs += [pl.BlockSpec((1, w), lambda i, k: (0, 0)) for w in acc_outs]
    scratch = [pltpu.VMEM((tm, N), F32)] if nk > 1 else []
    if n_ride:
        out_shape += _exchange_out(ride, False)
        out_specs += [ANY] * n_ride
        scratch += _exchange_sems(n_ride)
    return pl.pallas_call(
        body, name=name, grid=(nm, nk), out_shape=out_shape, in_specs=in_specs, out_specs=out_specs,
        scratch_shapes=scratch, compiler_params=_params("arbitrary", "arbitrary"),
    )(*args, *row_ins, *vec_ins, *(ride or []))


def _merge_out_proj_norm(gpre, ys, b_gate, w_o, x, g_post, g_pre):
    tokens = x.shape[0]
    tm = min(512, tokens)

    def body(gp_ref, ya_ref, yb_ref, yc_ref, b_ref, w_ref, x_ref, g2_ref, g3_ref, m_ref, mix_ref, x1_ref, h2_ref):
        acc = None
        for k, y_ref in enumerate((ya_ref, yb_ref, yc_ref)):
            cols = slice(k * D, (k + 1) * D)
            gate = jax.nn.sigmoid(gp_ref[:, cols].astype(F32) + b_ref[:, cols])
            term = gate * y_ref[...].astype(F32)
            acc = term if acc is None else acc + term
        merged = acc.astype(BF16)
        m_ref[...] = merged
        mv = jnp.dot(merged, w_ref[...], preferred_element_type=F32)
        mix_ref[...] = mv
        x1 = x_ref[...] + (mv * _rstd(mv)) * g2_ref[...]
        x1_ref[...] = x1
        h2_ref[...] = ((x1 * _rstd(x1)) * g3_ref[...]).astype(BF16)

    def rows(w):
        return pl.BlockSpec((tm, w), lambda i: (i, 0))

    def whole(a):
        return pl.BlockSpec(a.shape, lambda i: (0, 0))

    return pl.pallas_call(
        body, name="merge_out_proj_norm", grid=(tokens // tm,),
        out_shape=[jax.ShapeDtypeStruct((tokens, D), dt) for dt in (BF16, F32, F32, BF16)],
        in_specs=[rows(3 * D), rows(D), rows(D), rows(D), whole(b_gate), whole(w_o), rows(D), whole(g_post),
                  whole(g_pre)],
        out_specs=[rows(D)] * 4, compiler_params=_params("parallel"),
    )(gpre, *ys, b_gate, w_o, x, g_post, g_pre)


def _ffn_in_swiglu(h2, w_ffn_in):
    tokens = h2.shape[0]
    tm, tn = min(512, tokens), _tile(D_FF, 1536)
    nj = D_FF // tn

    def body(h_ref, wa_ref, wb_ref, a_ref, b_ref, f_ref):
        hv = h_ref[...]
        a = jnp.dot(hv, wa_ref[...], preferred_element_type=F32)
        b = jnp.dot(hv, wb_ref[...], preferred_element_type=F32)
        a_ref[...] = a.astype(BF16)
        b_ref[...] = b.astype(BF16)
        f_ref[...] = (a * jax.nn.sigmoid(a) * b).astype(BF16)

    out = jax.ShapeDtypeStruct((tokens, D_FF), BF16)
    o_spec = pl.BlockSpec((tm, tn), lambda j, i: (i, j))
    return pl.pallas_call(
        body, name="ffn_in_swiglu", grid=(nj, tokens // tm), out_shape=(out, out, out),
        in_specs=[pl.BlockSpec((tm, D), lambda j, i: (i, 0)), pl.BlockSpec((D, tn), lambda j, i: (0, j)),
                  pl.BlockSpec((D, tn), lambda j, i: (0, j + nj))],
        out_specs=(o_spec, o_spec, o_spec), compiler_params=_params("parallel", "parallel"),
    )(h2, w_ffn_in, w_ffn_in)


def _ffn_out_loss(f, w_ffn_out, x1, tgt, g_post):
    def epilogue(fo_v, x1_ref, t_ref, g_ref, dy_ref, dfo_ref, dg_ref, loss_ref):
        r = _rstd(fo_v)
        n = fo_v * r
        err = (x1_ref[...] + n * g_ref[...]) - t_ref[...]
        loss_ref[...] += _colsum(err * err)
        dy = err * (1.0 / D)
        dy_ref[...] = dy
        dg_ref[...] += _colsum(dy * n)
        dfo_ref[...] = _norm_bwd(dy * g_ref[...], n, r).astype(BF16)

    return _matmul_rows([(f, w_ffn_out)], "nn", "ffn_out_loss", epilogue, [x1, tgt], [g_post],
                        [(D, F32), (D, BF16)], (D, D), tk_cap=D_FF)


def _d_ffn_swiglu_bwd(dfo, w_ffn_out, gu_a, gu_b):
    def epilogue(d, a_ref, b_ref, dgu_ref):
        a = a_ref[...].astype(F32)
        b = b_ref[...].astype(F32)
        s = jax.nn.sigmoid(a)
        dgu_ref[:, :D_FF] = (d * b * (s * (1.0 + a * (1.0 - s)))).astype(BF16)
        dgu_ref[:, D_FF:] = (d * (a * s)).astype(BF16)

    return _matmul_rows([(dfo, w_ffn_out)], "nt", "d_ffn_swiglu_bwd", epilogue, [gu_a, gu_b], [],
                        [(2 * D_FF, BF16)], tm=256)[0]


def _d_h2_norm_bwd(dgu, w_ffn_in, x1, dy, mix, g_pre, g_post):
    def epilogue(dh, x1_ref, dy_ref, mix_ref, g3_ref, g2_ref, dx1_ref, dmix_ref, dg3_ref, dg2_ref):
        x1v = x1_ref[...]
        r3 = _rstd(x1v)
        n3 = x1v * r3
        dg3_ref[...] += _colsum(dh * n3)
        dx1 = dy_ref[...] + _norm_bwd(dh * g3_ref[...], n3, r3)
        dx1_ref[...] = dx1
        mv = mix_ref[...]
        r2 = _rstd(mv)
        n2 = mv * r2
        dg2_ref[...] += _colsum(dx1 * n2)
        dmix_ref[...] = _norm_bwd(dx1 * g2_ref[...], n2, r2).astype(BF16)

    return _matmul_rows([(dgu, w_ffn_in)], "nt", "d_h2_norm_bwd", epilogue, [x1, dy, mix], [g_pre, g_post],
                        [(D, F32), (D, BF16)], (D, D), tk_cap=D_FF)


def _d_merged_gate_bwd(dmix, w_o, gpre, ys, b_gate):
    def epilogue(dm, gp_ref, ya_ref, yb_ref, yc_ref, b_ref, dya_ref, dyb_ref, dyc_ref, dgp_ref, db_ref):
        for k, (y_ref, dy_ref) in enumerate(((ya_ref, dya_ref), (yb_ref, dyb_ref), (yc_ref, dyc_ref))):
            cols = slice(k * D, (k + 1) * D)
            gate = jax.nn.sigmoid(gp_ref[:, cols].astype(F32) + b_ref[:, cols])
            dy_ref[...] = (dm * gate).astype(BF16)
            dgp = (dm * y_ref[...].astype(F32)) * (gate * (1.0 - gate))
            dgp_ref[:, cols] = dgp.astype(BF16)
            db_ref[:, cols] += _colsum(dgp)

    return _matmul_rows([(dmix, w_o)], "nt", "d_merged_gate_bwd", epilogue, [gpre, *ys], [b_gate],
                        [(D, BF16), (D, BF16), (D, BF16), (3 * D, BF16)], (3 * D,))


def _d_h_norm_bwd(pairs, x, dx1, g_pre, ride):
    def epilogue(dh, x_ref, dx1_ref, g_ref, dx_ref, dg_ref):
        xv = x_ref[...]
        r = _rstd(xv)
        n = xv * r
        dg_ref[...] += _colsum(dh * n)
        dx_ref[...] = dx1_ref[...] + _norm_bwd(dh * g_ref[...], n, r)

    return _matmul_rows(pairs, "nt", "d_h_norm_bwd", epilogue, [x, dx1], [g_pre], [(D, F32)], (D,), ride=ride,
                        tm=1024, tk_cap=1024, epi_rows=256)


def _gain_grad(dmn, mem):
    def body(d_ref, m_ref, dg_ref):
        mv = m_ref[...]
        dg_ref[...] += _colsum(d_ref[...] * (mv * _rstd(mv)))

    return _rowwise(body, "mem_gain_grad", mem.shape[0], 256, [dmn, mem], [], [], (D,))[0]


def _head_rowsum(v, head0):
    return (jnp.sum(jnp.where(head0, v, 0.0), axis=1, keepdims=True),
            jnp.sum(jnp.where(head0, 0.0, v), axis=1, keepdims=True))


KT = 256
SB_SCALE = 0.125


def _make_suffix(inclusive):
    row, col = lax.broadcasted_iota(jnp.int32, (KT, KT), 0), lax.broadcasted_iota(jnp.int32, (KT, KT), 1)
    tri = (row >= col if inclusive else row > col).astype(BF16)
    tri2 = jnp.concatenate([tri, tri], axis=0)

    def suffix(x):
        hi = x.astype(BF16)
        lo = (x - hi.astype(F32)).astype(BF16)
        return jnp.dot(jnp.concatenate([hi, lo], axis=1), tri2, preferred_element_type=F32)

    return suffix


def _sb_scores(qh, k, mask, suffix_incl, later):
    z = lax.dot_general(qh, k, NT, preferred_element_type=F32)
    zc = jnp.minimum(z, 60.0)
    sp = jnp.log(1.0 + jnp.exp(zc))
    if mask is not None:
        sp = jnp.where(mask, sp, 0.0)
    a = jnp.exp((zc - suffix_incl(sp)) - later)
    if mask is not None:
        a = jnp.where(mask, a, 0.0)
    return zc, sp, a


QB = KT


def _sb_tiles(i, tile, init):
    st = lax.cond(i > 0, lambda s: tile(i - 1, tile(i, s, True), False), lambda s: tile(i, s, True), init)
    rest = jnp.maximum(i - 1, 0)
    st = lax.fori_loop(0, lax.shift_right_logical(rest, 1),
                       lambda t, s: tile(rest - 2 - 2 * t, tile(rest - 1 - 2 * t, s, False), False), st)
    return lax.cond((rest & 1) == 1, lambda s: tile(0, s, False), lambda s: s, st)


def _sb_consts():
    head0 = lax.broadcasted_iota(jnp.int32, (QB, BLK), 1) < 64
    row = lax.broadcasted_iota(jnp.int32, (2 * QB, KT), 0) & (QB - 1)
    return head0, row > lax.broadcasted_iota(jnp.int32, (2 * QB, KT), 1)


def _stack_heads(v, head0):
    zero = jnp.zeros_like(v)
    return jnp.concatenate([jnp.where(head0, v, zero), jnp.where(head0, zero, v)], axis=0)


def _unstack_heads(v, head0):
    n = v.shape[0] // 2
    return jnp.where(head0, v[:n], v[n:])


def _sb_fwd(proj_a, seq, ride):
    bsz = proj_a.shape[0]
    n_ride = len(ride)

    def body(x_ref, *rest):
        ride_refs, (o_ref, o32_ref), rest = rest[:n_ride], rest[n_ride:n_ride + 2], rest[n_ride + 2:]
        gathered_refs, acc_ref, sems = rest[:n_ride], rest[n_ride], rest[n_ride + 1:]
        finish_ride = _riding_exchange(ride_refs, gathered_refs, sems, gather=True)
        head0, diag_mask = _sb_consts()
        suffix_incl = _make_suffix(True)

        def qblock(i, carry):
            r0 = pl.multiple_of(i * QB, QB)
            qs = _stack_heads(x_ref[pl.ds(r0, QB), 0:128] * jnp.asarray(SB_SCALE, BF16), head0)

            def tile(jt, later, masked):
                c0 = pl.multiple_of(jt * KT, KT)
                k = x_ref[pl.ds(c0, KT), 128:256]
                v = x_ref[pl.ds(c0, KT), 256:384]
                _, sp, a = _sb_scores(qs, k, diag_mask if masked else None, suffix_incl, later)
                pv = jnp.dot(a.astype(BF16), v, preferred_element_type=F32)
                if masked:
                    acc_ref[...] = pv
                else:
                    acc_ref[...] += pv
                return later + jnp.sum(sp, axis=1, keepdims=True)

            _sb_tiles(i, tile, jnp.zeros((2 * QB, 1), F32))
            o = _unstack_heads(acc_ref[...], head0)
            o32_ref[pl.ds(r0, QB), :] = o
            o_ref[pl.ds(r0, QB), :] = o.astype(BF16)
            return carry

        lax.fori_loop(0, seq // QB, qblock, 0)
        finish_ride()

    out_spec = pl.BlockSpec((None, seq, BLK), lambda b, hp: (b, 0, hp))
    return pl.pallas_call(
        body, name="sb_attn_fwd", grid=(bsz, 4),
        out_shape=[jax.ShapeDtypeStruct((bsz, seq, 512), BF16), jax.ShapeDtypeStruct((bsz, seq, 512), F32),
                   *_exchange_out(ride, True)],
        in_specs=[pl.BlockSpec((None, seq, 384), lambda b, hp: (b, 0, hp))] + [ANY] * n_ride,
        out_specs=[out_spec, out_spec] + [ANY] * n_ride,
        scratch_shapes=[pltpu.VMEM((2 * QB, BLK), F32), *_exchange_sems(n_ride)],
        compiler_params=_params("arbitrary", "arbitrary"),
    )(proj_a, *ride)


def _sb_bwd(proj_a, d_o, o_a, seq, ride):
    bsz = proj_a.shape[0]
    n_ride = len(ride)

    def body(x_ref, do_ref, o_ref, *rest):
        ride_refs, d_ref, rest = rest[:n_ride], rest[n_ride], rest[n_ride + 1:]
        received_refs, (dq_acc, dk_acc, dv_acc), sems = rest[:n_ride], rest[n_ride:n_ride + 3], rest[n_ride + 3:]
        finish_ride = _riding_exchange(ride_refs, received_refs, sems, gather=False)
        head0, diag_mask = _sb_consts()
        suffix_incl, suffix_excl = _make_suffix(True), _make_suffix(False)
        dk_acc[...] = jnp.zeros_like(dk_acc)
        dv_acc[...] = jnp.zeros_like(dv_acc)

        def qblock(i, carry):
            r0 = pl.multiple_of(i * QB, QB)
            qs = _stack_heads(x_ref[pl.ds(r0, QB), 0:128] * jnp.asarray(SB_SCALE, BF16), head0)
            do = do_ref[pl.ds(r0, QB), :]
            dos = _stack_heads(do, head0)
            dsum = jnp.concatenate(_head_rowsum(do.astype(F32) * o_ref[pl.ds(r0, QB), :], head0), axis=0)

            def tile(jt, st, masked):
                later, rest_g = st
                c0 = pl.multiple_of(jt * KT, KT)
                k = x_ref[pl.ds(c0, KT), 128:256]
                v = x_ref[pl.ds(c0, KT), 256:384]
                zc, sp, a = _sb_scores(qs, k, diag_mask if masked else None, suffix_incl, later)
                a16 = a.astype(BF16)
                g = a16.astype(F32) * lax.dot_general(dos, v, NT, preferred_element_type=F32)
                dz = g - jnp.exp(zc - sp) * (rest_g - suffix_excl(g))
                if masked:
                    dz = jnp.where(diag_mask, dz, 0.0)
                dz = dz.astype(BF16)
                dq = jnp.dot(dz, k, preferred_element_type=F32)
                if masked:
                    dq_acc[...] = dq
                else:
                    dq_acc[...] += dq
                dk_acc[pl.ds(c0, KT), :] += lax.dot_general(dz, qs, TN, preferred_element_type=F32)
                dv_acc[pl.ds(c0, KT), :] += lax.dot_general(a16, dos, TN, preferred_element_type=F32)
                return later + jnp.sum(sp, axis=1, keepdims=True), rest_g - jnp.sum(g, axis=1, keepdims=True)

            _sb_tiles(i, tile, (jnp.zeros((2 * QB, 1), F32), dsum))
            d_ref[pl.ds(r0, QB), 0:128] = (_unstack_heads(dq_acc[...], head0) * SB_SCALE).astype(BF16)
            return carry

        lax.fori_loop(0, seq // QB, qblock, 0)
        d_ref[:, 128:256] = dk_acc[...].astype(BF16)
        d_ref[:, 256:384] = dv_acc[...].astype(BF16)
        finish_ride()

    return pl.pallas_call(
        body, name="sb_attn_bwd", grid=(bsz, 4),
        out_shape=[jax.ShapeDtypeStruct((bsz, seq, WA), BF16), *_exchange_out(ride, False)],
        in_specs=[pl.BlockSpec((None, seq, 384), lambda b, hp: (b, 0, hp)),
                  pl.BlockSpec((None, seq, BLK), lambda b, hp: (b, 0, hp)),
                  pl.BlockSpec((None, seq, BLK), lambda b, hp: (b, 0, hp))] + [ANY] * n_ride,
        out_specs=[pl.BlockSpec((None, seq, 384), lambda b, hp: (b, 0, hp))] + [ANY] * n_ride,
        scratch_shapes=[pltpu.VMEM((2 * QB, BLK), F32), pltpu.VMEM((seq, BLK), F32), pltpu.VMEM((seq, BLK), F32),
                        *_exchange_sems(n_ride)],
        compiler_params=_params("arbitrary", "arbitrary"),
    )(proj_a, d_o, o_a, *ride)


def _rope_tables(seq):
    inv_freq = ROPE_THETA ** (-jnp.arange(32, dtype=F32) * 2.0 / 64)
    ang = jnp.arange(seq).astype(F32)[:, None] * inv_freq[None, :]
    cos, sin = jnp.cos(ang), jnp.sin(ang)
    return jnp.tile(cos, (1, 4)), jnp.concatenate([-sin, sin, -sin, sin], axis=1)


def _make_rope(n_rows):
    lane = lax.broadcasted_iota(jnp.int32, (n_rows, BLK), 1)
    first = (lane & 63) < 32

    def rope(x, cos, sin):
        partner = jnp.where(first, pltpu.roll(x, 96, 1), pltpu.roll(x, 32, 1))
        return x * cos + partner * sin

    return rope


DIL_UNROLL = 8


def _dil_consts():
    head0 = lax.broadcasted_iota(jnp.int32, (BLK, BLK), 1) < 64
    row = lax.broadcasted_iota(jnp.int32, (2 * BLK, 2 * BLK), 0) & (BLK - 1)
    col = lax.broadcasted_iota(jnp.int32, (2 * BLK, 2 * BLK), 1)
    valid_prev = jnp.logical_and(col < BLK, col >= row)
    valid_cur = jnp.logical_and(col >= BLK, row >= col - BLK)
    return head0, valid_prev, valid_cur


def _dil_blocks(dil, seq, block):
    nq = seq // dil // BLK

    def rows(r, i):
        if dil == 1:
            return pl.ds(pl.multiple_of(i * BLK, BLK), BLK)
        return pl.ds(r + (dil * BLK) * i, BLK, stride=dil)

    def step(t, carry):
        for u in range(DIL_UNROLL):
            n = t * DIL_UNROLL + u
            r, i = lax.div(n, nq), lax.rem(n, nq)
            block(rows(r, i), rows(r, jnp.maximum(i - 1, 0)), i)
        return carry

    lax.fori_loop(0, seq // BLK // DIL_UNROLL, step, 0)


def _dil_scores(qf, kf, vf, cur, prev, i, consts):
    head0, valid_prev, valid_cur = consts
    qs = _stack_heads(qf[cur, :].astype(BF16), head0)
    kcat = jnp.concatenate([kf[prev, :], kf[cur, :]], axis=0).astype(BF16)
    vcat = jnp.concatenate([vf[prev, :], vf[cur, :]], axis=0).astype(BF16)
    valid = jnp.logical_or(valid_cur, jnp.logical_and(valid_prev, i > 0))
    s = lax.dot_general(qs, kcat, NT, preferred_element_type=F32) * 0.125
    return qs, kcat, vcat, s, valid


def _head_cols(v):
    return jnp.concatenate([v[:, 0:1], v[:, 64:65]], axis=0)


def _dil_load_qkv(x_ref, c, rope, cos, sin, qf, kf, vf):
    qf[...] = rope(x_ref[:, c:c + 128].astype(F32), cos, sin).astype(BF16).astype(F32)
    kf[...] = rope(x_ref[:, c + 128:c + 256].astype(F32), cos, sin).astype(BF16).astype(F32)
    vf[...] = x_ref[:, c + 256:c + 384].astype(F32)


def _dil_fwd(proj_b, cos_t, sin_t, seq):
    bsz = proj_b.shape[0]

    def body(x_ref, cos_ref, sin_ref, ob_ref, lse_ref, qf, kf, vf, og, lg):
        consts = _dil_consts()
        head0 = consts[0]
        rope = _make_rope(seq)
        cos, sin = cos_ref[...], sin_ref[...]
        for g, dil in enumerate(DIL_GROUPS):
            _dil_load_qkv(x_ref, 384 * g, rope, cos, sin, qf, kf, vf)

            def block(cur, prev, i, g=g):
                _, _, vcat, s, valid = _dil_scores(qf, kf, vf, cur, prev, i, consts)
                s = jnp.where(valid, s, NEG_INF)
                m = jnp.max(s, axis=1, keepdims=True)
                p = jnp.exp(s - m)
                den = jnp.sum(p, axis=1, keepdims=True)
                o = jnp.dot(p.astype(BF16), vcat, preferred_element_type=F32) / den
                og[g, cur, :] = _unstack_heads(o, head0)
                lg[g, cur, :] = _unstack_heads(jnp.broadcast_to(m + jnp.log(den), (2 * BLK, BLK)), head0)

            _dil_blocks(dil, seq, block)
        ls = [lg[0], lg[1], lg[2]]
        m = jnp.maximum(jnp.maximum(ls[0], ls[1]), ls[2])
        ws = [jnp.exp(l - m) for l in ls]
        den = (ws[0] + ws[1]) + ws[2]
        ob_ref[...] = (((ws[0] * og[0] + ws[1] * og[1]) + ws[2] * og[2]) / den).astype(BF16)
        lse_ref[...] = m + jnp.log(den)

    tab_spec = pl.BlockSpec((seq, BLK), lambda b, hp: (0, 0))
    out_spec = pl.BlockSpec((None, seq, BLK), lambda b, hp: (b, 0, hp))
    slab = pltpu.VMEM((seq, BLK), F32)
    return pl.pallas_call(
        body, name="dil_attn_fwd", grid=(bsz, 2),
        out_shape=(jax.ShapeDtypeStruct((bsz, seq, 256), BF16), jax.ShapeDtypeStruct((bsz, seq, 256), F32)),
        in_specs=[pl.BlockSpec((None, seq, WB // 2), lambda b, hp: (b, 0, hp)), tab_spec, tab_spec],
        out_specs=(out_spec, out_spec),
        scratch_shapes=[slab, slab, slab, pltpu.VMEM((3, seq, BLK), F32), pltpu.VMEM((3, seq, BLK), F32)],
        compiler_params=_params("parallel", "parallel"),
    )(proj_b, cos_t, sin_t)


def _dil_bwd(proj_b, cos_t, sin_t, d_ob, o_b, lse, seq):
    bsz = proj_b.shape[0]

    def body(x_ref, cos_ref, sin_ref, do_ref, ob_ref, lse_ref, d_ref, qf, kf, vf, dof, dsf, dq_s, dk_acc, dv_acc):
        consts = _dil_consts()
        head0 = consts[0]
        rope = _make_rope(seq)
        cos, sin = cos_ref[...], sin_ref[...]
        do_all = do_ref[...].astype(F32)
        dof[...] = do_all
        head0_all = lax.broadcasted_iota(jnp.int32, (seq, BLK), 1) < 64
        d0, d1 = _head_rowsum(do_all * ob_ref[...].astype(F32), head0_all)
        dsf[...] = jnp.where(head0_all, d0, d1)
        for g, dil in enumerate(DIL_GROUPS):
            _dil_load_qkv(x_ref, 384 * g, rope, cos, sin, qf, kf, vf)
            dk_acc[...] = jnp.zeros_like(dk_acc)
            dv_acc[...] = jnp.zeros_like(dv_acc)

            def block(cur, prev, i):
                qs, kcat, vcat, s, valid = _dil_scores(qf, kf, vf, cur, prev, i, consts)
                dos = _stack_heads(dof[cur, :].astype(BF16), head0)
                p = jnp.where(valid, jnp.exp(s - _head_cols(lse_ref[cur, :])), 0.0)
                dp = lax.dot_general(dos, vcat, NT, preferred_element_type=F32)
                ds = ((p * (dp - _head_cols(dsf[cur, :]))) * 0.125).astype(BF16)
                dq_s[cur, :] = _unstack_heads(jnp.dot(ds, kcat, preferred_element_type=F32), head0)
                dk = lax.dot_general(ds, qs, TN, preferred_element_type=F32)
                dv = lax.dot_general(p.astype(BF16), dos, TN, preferred_element_type=F32)
                dk_acc[prev, :] += dk[:BLK]
                dk_acc[cur, :] += dk[BLK:]
                dv_acc[prev, :] += dv[:BLK]
                dv_acc[cur, :] += dv[BLK:]

            _dil_blocks(dil, seq, block)
            c = 384 * g
            d_ref[:, c:c + 128] = rope(dq_s[...], cos, -sin).astype(BF16)
            d_ref[:, c + 128:c + 256] = rope(dk_acc[...], cos, -sin).astype(BF16)
            d_ref[:, c + 256:c + 384] = dv_acc[...].astype(BF16)

    x_spec = pl.BlockSpec((None, seq, WB // 2), lambda b, hp: (b, 0, hp))
    tab_spec = pl.BlockSpec((seq, BLK), lambda b, hp: (0, 0))
    tok_spec = pl.BlockSpec((None, seq, BLK), lambda b, hp: (b, 0, hp))
    return pl.pallas_call(
        body, name="dil_attn_bwd", grid=(bsz, 2),
        out_shape=jax.ShapeDtypeStruct((bsz, seq, WB), BF16),
        in_specs=[x_spec, tab_spec, tab_spec, tok_spec, tok_spec, tok_spec], out_specs=x_spec,
        scratch_shapes=[pltpu.VMEM((seq, BLK), F32)] * 8,
        compiler_params=_params("parallel", "parallel"),
    )(proj_b, cos_t, sin_t, d_ob, o_b, lse)


MEM_SCALE = 128 ** -0.5
MEM_QB = 2048


def _mem_fwd(proj_a, kv, seq):
    bsz = proj_a.shape[0]

    def body(q_ref, k_ref, v_ref, o_ref):
        k, v = k_ref[...], v_ref[...]

        def qblock(i, carry):
            r0 = pl.multiple_of(i * MEM_QB, MEM_QB)
            s = lax.dot_general(q_ref[pl.ds(r0, MEM_QB), :], k, NT, preferred_element_type=F32) * MEM_SCALE
            p = jnp.exp(s - jnp.max(s, axis=1, keepdims=True))
            p = p / jnp.sum(p, axis=1, keepdims=True)
            o_ref[pl.ds(r0, MEM_QB), :] = jnp.dot(p.astype(BF16), v, preferred_element_type=F32).astype(BF16)
            return carry

        lax.fori_loop(0, seq // MEM_QB, qblock, 0)

    return pl.pallas_call(
        body, name="mem_attn_fwd", grid=(bsz, 4),
        out_shape=jax.ShapeDtypeStruct((bsz, seq, 512), BF16),
        in_specs=[pl.BlockSpec((None, seq, BLK), lambda b, h: (b, 0, 12 + h)),
                  pl.BlockSpec((None, MEM_LEN, BLK), lambda b, h: (b, 0, h)),
                  pl.BlockSpec((None, MEM_LEN, BLK), lambda b, h: (b, 0, 4 + h))],
        out_specs=pl.BlockSpec((None, seq, BLK), lambda b, h: (b, 0, h)),
        compiler_params=_params("parallel", "parallel"),
    )(proj_a, kv, kv)


def _mem_bwd(proj_a, kv, d_o, d_proj_a, seq):
    bsz = proj_a.shape[0]

    def body(q_ref, k_ref, v_ref, do_ref, _, dq_ref, dk_ref, dv_ref):
        k, v = k_ref[...], v_ref[...]

        def qblock(i, carry):
            dk, dv = carry
            r0 = pl.multiple_of(i * MEM_QB, MEM_QB)
            q, do = q_ref[pl.ds(r0, MEM_QB), :], do_ref[pl.ds(r0, MEM_QB), :]
            s = lax.dot_general(q, k, NT, preferred_element_type=F32) * MEM_SCALE
            p = jnp.exp(s - jnp.max(s, axis=1, keepdims=True))
            p = p / jnp.sum(p, axis=1, keepdims=True)
            dp = lax.dot_general(do, v, NT, preferred_element_type=F32)
            ds = ((p * (dp - jnp.sum(p * dp, axis=1, keepdims=True))) * MEM_SCALE).astype(BF16)
            dq_ref[pl.ds(r0, MEM_QB), :] = jnp.dot(ds, k, preferred_element_type=F32).astype(BF16)
            dk = dk + lax.dot_general(ds, q, TN, preferred_element_type=F32)
            dv = dv + lax.dot_general(p.astype(BF16), do, TN, preferred_element_type=F32)
            return dk, dv

        zero = jnp.zeros((MEM_LEN, BLK), F32)
        dk, dv = lax.fori_loop(0, seq // MEM_QB, qblock, (zero, zero))
        dk_ref[...] = dk.astype(BF16)
        dv_ref[...] = dv.astype(BF16)

    kv_spec = pl.BlockSpec((None, MEM_LEN, BLK), lambda b, h: (b, 0, h))
    return pl.pallas_call(
        body, name="mem_attn_bwd", grid=(bsz, 4),
        out_shape=(jax.ShapeDtypeStruct((bsz, seq, WA), BF16), jax.ShapeDtypeStruct((bsz, MEM_LEN, 512), BF16),
                   jax.ShapeDtypeStruct((bsz, MEM_LEN, 512), BF16)),
        in_specs=[pl.BlockSpec((None, seq, BLK), lambda b, h: (b, 0, 12 + h)), kv_spec,
                  pl.BlockSpec((None, MEM_LEN, BLK), lambda b, h: (b, 0, 4 + h)),
                  pl.BlockSpec((None, seq, BLK), lambda b, h: (b, 0, h)), ANY],
        out_specs=(pl.BlockSpec((None, seq, BLK), lambda b, h: (b, 0, 12 + h)), kv_spec, kv_spec),
        input_output_aliases={4: 0},
        compiler_params=_params("parallel", "parallel"),
    )(proj_a, kv, kv, d_o, d_proj_a)


def _mesh_pos():
    return lax.axis_index("x"), lax.axis_index("y"), lax.axis_index("c")


def _all_gather(shard, name):
    m_per, n = shard.shape

    def body(x_ref, out_ref, send_sems, recv_sems, local_sem):
        x, y, c = _mesh_pos()
        me, sibling = (x, y, c), (x, y, 1 - c)
        chips = [(1 - x, y), (x, 1 - y), (1 - x, 1 - y)]

        def rows(px, py, pc):
            return out_ref.at[pl.ds((4 * px + 2 * py + pc) * m_per, m_per), :]

        def copy(k, block, to, src=None):
            return pltpu.make_async_remote_copy(
                src_ref=rows(*block) if src is None else src, dst_ref=rows(*block),
                send_sem=send_sems.at[k], recv_sem=recv_sems.at[k], device_id=to, device_id_type=MESH)

        mine = pltpu.make_async_copy(x_ref, rows(*me), local_sem)
        mine.start()
        first = [copy(0, me, sibling, src=x_ref)]
        first += [copy(1 + j, me, (*chip, c), src=x_ref) for j, chip in enumerate(chips)]
        for cp in first:
            cp.start()
        passed = [copy(4 + j, (*chip, c), sibling) for j, chip in enumerate(chips)]
        for j, chip in enumerate(chips):
            copy(1 + j, (*chip, c), me).wait_recv()
            passed[j].start()
        copy(0, sibling, me).wait_recv()
        for j, chip in enumerate(chips):
            copy(4 + j, (*chip, 1 - c), me).wait_recv()
        for cp in first + passed:
            cp.wait_send()
        mine.wait()

    return pl.pallas_call(
        body, name=name, out_shape=jax.ShapeDtypeStruct((N_DEV * m_per, n), shard.dtype),
        in_specs=[ANY], out_specs=ANY,
        scratch_shapes=[pltpu.SemaphoreType.DMA((7,)), pltpu.SemaphoreType.DMA((7,)), pltpu.SemaphoreType.DMA(())],
    )(shard)


def _exchange_sems(n_arrays):
    return [pltpu.SemaphoreType.DMA((7 * n_arrays,)), pltpu.SemaphoreType.DMA((7 * n_arrays,)),
            pltpu.SemaphoreType.DMA((n_arrays,))]


def _exchange_out(srcs, gather):
    return [jax.ShapeDtypeStruct((N_DEV, *s.shape[-2:]), s.dtype) for s in srcs]


def _direct_exchange(src_refs, dst_refs, send_sems, recv_sems, local_sems, gather):
    x, y, c = _mesh_pos()
    me = 4 * x + 2 * y + c
    owns, sends, recvs = [], [], []
    for a, (src, dst) in enumerate(zip(src_refs, dst_refs)):
        owns.append(pltpu.make_async_copy(src if gather else src.at[me], dst.at[me], local_sems.at[a]))
        for j in range(1, N_DEV):
            px = 1 - x if j & 4 else x
            py = 1 - y if j & 2 else y
            pc = 1 - c if j & 1 else c
            peer = 4 * px + 2 * py + pc
            sems = dict(send_sem=send_sems.at[7 * a + j - 1], recv_sem=recv_sems.at[7 * a + j - 1],
                        device_id=(px, py, pc), device_id_type=MESH)
            sends.append(pltpu.make_async_remote_copy(
                src_ref=src if gather else src.at[peer], dst_ref=dst.at[me], **sems))
            recvs.append(pltpu.make_async_remote_copy(
                src_ref=src if gather else src.at[me], dst_ref=dst.at[peer], **sems))

    def start():
        for cp in owns + sends:
            cp.start()

    def wait():
        for cp in recvs:
            cp.wait_recv()
        for cp in sends:
            cp.wait_send()
        for cp in owns:
            cp.wait()

    return start, wait


def _riding_exchange(src_refs, dst_refs, sems, gather):
    start, wait = _direct_exchange(src_refs, dst_refs, *sems, gather)
    ids = [pl.program_id(a) for a in range(2)]
    last = [pl.num_programs(a) - 1 for a in range(2)]
    pl.when(jnp.logical_and(ids[0] == 0, ids[1] == 0))(start)
    return lambda: pl.when(jnp.logical_and(ids[0] == last[0], ids[1] == last[1]))(wait)


def _exchange(srcs, gather, name):
    n = len(srcs)

    def body(*refs):
        start, wait = _direct_exchange(refs[:n], refs[n:2 * n], *refs[2 * n:], gather=gather)
        start()
        wait()

    return pl.pallas_call(
        body, name=name, out_shape=_exchange_out(srcs, gather),
        in_specs=[ANY] * n, out_specs=[ANY] * n, scratch_shapes=_exchange_sems(n),
    )(*srcs)


def _adamw(w, g, m, v):
    m = ADAM_B1 * m + (1.0 - ADAM_B1) * g
    v = ADAM_B2 * v + (1.0 - ADAM_B2) * (g * g)
    m_hat = m / (1.0 - ADAM_B1 ** ADAM_STEP)
    v_hat = v / (1.0 - ADAM_B2 ** ADAM_STEP)
    return -ADAM_LR * (m_hat / (jnp.sqrt(v_hat) + ADAM_EPS) + ADAM_WD * w), m, v


def _reduce_adamw(recv, w, m, v, name):
    _, k, n = w.shape
    tr = max(t for t in range(16, 257, 16) if k % t == 0)

    def body(r_ref, w_ref, m_ref, v_ref, g_out, d_out, m_out, v_out):
        g = r_ref[0].astype(F32)
        for s in range(1, N_DEV):
            g = g + r_ref[s].astype(F32)
        g_out[...] = g
        d_out[...], m_out[...], v_out[...] = _adamw(w_ref[...], g, m_ref[...], v_ref[...])

    spec = pl.BlockSpec((None, tr, n), lambda i: (0, i, 0))
    return pl.pallas_call(
        body, name=name, grid=(k // tr,),
        out_shape=[jax.ShapeDtypeStruct((1, k, n), F32)] * 4,
        in_specs=[pl.BlockSpec((N_DEV, tr, n), lambda i: (0, i, 0)), spec, spec, spec],
        out_specs=[spec] * 4, compiler_params=_params("arbitrary"),
    )(recv, w, m, v)


def _small_adamw(gathered, w, m, v):
    def body(g_ref, w_ref, m_ref, v_ref, g_out, d_out, m_out, v_out, loss_out):
        tot = g_ref[0]
        for s in range(1, N_DEV):
            tot = tot + g_ref[s]
        g = tot[0:8]
        g_out[...] = g
        d_out[...], m_out[...], v_out[...] = _adamw(w_ref[...], g, m_ref[...], v_ref[...])
        loss_out[...] = jnp.broadcast_to((0.5 / D) * jnp.sum(tot[8:9], axis=1, keepdims=True), (8, BLK))

    out = [jax.ShapeDtypeStruct((8, D), F32)] * 4 + [jax.ShapeDtypeStruct((8, BLK), F32)]
    return pl.pallas_call(body, name="small_adamw", out_shape=out, compiler_params=_params())(gathered, w, m, v)


def _pick_chunks(w, chunks):
    return jnp.concatenate([w[:, BLK * c:BLK * (c + 1)] for c in chunks], axis=1)


def _whole_weight(gathered, i):
    _, k, n = gathered.shape
    if BY_ROWS[i]:
        return gathered.reshape(N_DEV * k, n)
    return gathered.transpose(1, 0, 2).reshape(k, N_DEV * n)


def _shard_parts(grad, i):
    if BY_ROWS[i]:
        return grad.reshape(N_DEV, grad.shape[0] // N_DEV, grad.shape[1])
    k, n8 = grad.shape
    return grad.reshape(k, N_DEV, n8 // N_DEV).transpose(1, 0, 2)


def kernel(x, mem, g_pre_mix, g_post_mix, g_pre_ffn, g_post_ffn, g_mem, w_in, w_mem_kv, w_br_sb, w_br_dil, w_br_mem, w_gate, b_gate, w_o, w_ffn_in, w_ffn_out, loss_target, m_g_pre_mix, m_g_post_mix, m_g_pre_ffn, m_g_post_ffn, m_g_mem, m_w_in, m_w_mem_kv, m_w_br_sb, m_w_br_dil, m_w_br_mem, m_w_gate, m_b_gate, m_w_o, m_w_ffn_in, m_w_ffn_out, v_g_pre_mix, v_g_post_mix, v_g_pre_ffn, v_g_post_ffn, v_g_mem, v_w_in, v_w_mem_kv, v_w_br_sb, v_w_br_dil, v_w_br_mem, v_w_gate, v_b_gate, v_w_o, v_w_ffn_in, v_w_ffn_out):
    bsz, seq, _ = x.shape
    tokens = bsz * seq
    xf, tgt, memf = x.reshape(tokens, D), loss_target.reshape(tokens, D), mem.reshape(bsz * MEM_LEN, D)
    big_w = [w_in, w_mem_kv, w_br_sb, w_br_dil, w_br_mem, w_gate, w_o, w_ffn_in, w_ffn_out]
    big_m = [m_w_in, m_w_mem_kv, m_w_br_sb, m_w_br_dil, m_w_br_mem, m_w_gate, m_w_o, m_w_ffn_in, m_w_ffn_out]
    big_v = [v_w_in, v_w_mem_kv, v_w_br_sb, v_w_br_dil, v_w_br_mem, v_w_gate, v_w_o, v_w_ffn_in, v_w_ffn_out]

    shards = [w[0].astype(BF16) for w in big_w]
    k_in, n_in = shards[0].shape
    fw_in = _whole_weight(_all_gather(shards[0], "weight_all_gather").reshape(N_DEV, k_in, n_in), 0)
    w_a, w_b = _pick_chunks(fw_in, CHUNKS_A), _pick_chunks(fw_in, CHUNKS_B)

    h = _norm_fwd(xf, g_pre_mix, "pre_mix_norm")
    proj_a = _matmul(h, w_a, "nn", BF16, "proj_a").reshape(bsz, seq, WA)
    proj_b = _matmul(h, w_b, "nn", BF16, "proj_b").reshape(bsz, seq, WB)
    o_a, o_a32, *behind = _sb_fwd(proj_a, seq, [shards[i] for i in GATHER_BEHIND])
    fw_mem_kv, fw_br_sb, fw_br_dil, fw_br_mem, fw_gate, fw_o, fw_ffn_in, fw_ffn_out = (
        _whole_weight(g, i) for g, i in zip(behind, GATHER_BEHIND))
    gpre = _matmul(h, fw_gate, "nn", BF16, "gate_proj")
    cos_t, sin_t = _rope_tables(seq)
    o_b, lse_b = _dil_fwd(proj_b, cos_t, sin_t, seq)
    mn = _norm_fwd(memf, g_mem, "mem_norm")
    kv = _matmul(mn, fw_mem_kv, "nn", BF16, "mem_kv_proj").reshape(bsz, MEM_LEN, D)
    o_c = _mem_fwd(proj_a, kv, seq)
    o_a2, o_b2, o_c2 = o_a.reshape(tokens, 512), o_b.reshape(tokens, 256), o_c.reshape(tokens, 512)
    ys = [_matmul(o_a2, fw_br_sb, "nn", BF16, "branch_sb"), _matmul(o_b2, fw_br_dil, "nn", BF16, "branch_dil"),
          _matmul(o_c2, fw_br_mem, "nn", BF16, "branch_mem")]
    merged, mix, x1, h2 = _merge_out_proj_norm(gpre, ys, b_gate, fw_o, xf, g_post_mix, g_pre_ffn)
    gu_a, gu_b, f = _ffn_in_swiglu(h2, fw_ffn_in)
    dy, dfo, dg_post_ffn, loss_lanes = _ffn_out_loss(f, fw_ffn_out, x1, tgt, g_post_ffn)

    gw_ffn_out = _matmul(f, dfo, "tn", BF16, "gw_ffn_out")
    dgu = _d_ffn_swiglu_bwd(dfo, fw_ffn_out, gu_a, gu_b)
    gw_ffn_in = _matmul(h2, dgu, "tn", BF16, "gw_ffn_in")
    dx1, dmix, dg_pre_ffn, dg_post_mix = _d_h2_norm_bwd(dgu, fw_ffn_in, x1, dy, mix, g_pre_ffn, g_post_mix)
    gw_o = _matmul(merged, dmix, "tn", BF16, "gw_o")
    dya, dyb, dyc, dgpre, db_gate = _d_merged_gate_bwd(dmix, fw_o, gpre, ys, b_gate)
    d_oa = _matmul(dya, fw_br_sb, "nt", BF16, "d_o_sb").reshape(bsz, seq, 512)
    d_ob = _matmul(dyb, fw_br_dil, "nt", BF16, "d_o_dil").reshape(bsz, seq, 256)
    d_oc = _matmul(dyc, fw_br_mem, "nt", BF16, "d_o_mem").reshape(bsz, seq, 512)
    gw_br_sb = _matmul(o_a2, dya, "tn", BF16, "gw_br_sb")
    gw_br_dil = _matmul(o_b2, dyb, "tn", BF16, "gw_br_dil")
    gw_br_mem = _matmul(o_c2, dyc, "tn", BF16, "gw_br_mem")
    gw_gate = _matmul(h, dgpre, "tn", BF16, "gw_gate")
    grads = {2: gw_br_sb, 3: gw_br_dil, 4: gw_br_mem, 5: gw_gate, 6: gw_o, 7: gw_ffn_in, 8: gw_ffn_out}
    d_proj_a, *recv_behind = _sb_bwd(proj_a, d_oa, o_a32, seq, [_shard_parts(grads[i], i) for i in REDUCE_BEHIND])
    d_proj_a, dk_m, dv_m = _mem_bwd(proj_a, kv, d_oc, d_proj_a, seq)
    d_proj_b = _dil_bwd(proj_b, cos_t, sin_t, d_ob, o_b, lse_b, seq).reshape(tokens, WB)
    d_proj_a = d_proj_a.reshape(tokens, WA)
    gw_a = _matmul(h, d_proj_a, "tn", BF16, "gw_in_a")
    gw_b = _matmul(h, d_proj_b, "tn", BF16, "gw_in_b")
    dkv = jnp.concatenate([dk_m, dv_m], axis=-1).reshape(bsz * MEM_LEN, D)
    gw_mem_kv = _matmul(mn, dkv, "tn", BF16, "gw_mem_kv")
    dmn = _matmul(dkv, fw_mem_kv, "nt", F32, "d_mem_norm")
    dg_mem = _gain_grad(dmn, memf)
    gw_ab = jnp.concatenate([gw_a, gw_b], axis=1)
    where = {c: i for i, c in enumerate(CHUNKS_A + CHUNKS_B)}
    grads = {0: _pick_chunks(gw_ab, [where[c] for c in range(34)]), 1: gw_mem_kv}
    dx, dg_pre_mix, *recv_last = _d_h_norm_bwd(
        [(dgpre, fw_gate), (d_proj_a, w_a), (d_proj_b, w_b)], xf, dx1, g_pre_mix,
        [_shard_parts(grads[i], i) for i in REDUCE_LAST])

    received = dict(zip(REDUCE_BEHIND + REDUCE_LAST, [*recv_behind, *recv_last]))
    adam = [_reduce_adamw(received[i], big_w[i], big_m[i], big_v[i], "reduce_adamw_" + BIG_NAMES[i])
            for i in range(len(big_w))]
    big = [[a[k] for a in adam] for k in range(4)]

    small = jnp.concatenate([dg_pre_mix, dg_post_mix, dg_pre_ffn, dg_post_ffn, dg_mem, db_gate.reshape(3, D),
                             loss_lanes, jnp.zeros((7, D), F32)], axis=0)
    small_all, = _exchange([small], True, "small_all_gather")

    def small_pack(gs, b):
        return jnp.concatenate([*gs, b.reshape(3, D)], axis=0)

    sm = _small_adamw(
        small_all, small_pack([g_pre_mix, g_post_mix, g_pre_ffn, g_post_ffn, g_mem], b_gate),
        small_pack([m_g_pre_mix, m_g_post_mix, m_g_pre_ffn, m_g_post_ffn, m_g_mem], m_b_gate),
        small_pack([v_g_pre_mix, v_g_post_mix, v_g_pre_ffn, v_g_post_ffn, v_g_mem], v_b_gate))
    loss = sm[4][0, 0]

    def leaves(k):
        t, bw = sm[k], big[k]
        return [t[0:1], t[1:2], t[2:3], t[3:4], t[4:5], *bw[0:6], t[5:8].reshape(1, 3 * D), *bw[6:9]]

    return (loss, dx.reshape(bsz, seq, D), *leaves(0), *leaves(1), *leaves(2), *leaves(3))
```
